```python
import math
import jax, jax.numpy as jnp
from jax import lax
import numpy as np

D_MODEL = 1024
BATCH = 8
SEQ = 8192
DEPTH = 4

N_MIXERS = 3
BLK = 128
D_FF = 2816
PLE_DIM = 256
LN_EPS = 1e-5
ROPE_THETA = 10000.0
DEEPNORM_ALPHA = (2 * DEPTH) ** 0.25
DEEPNORM_BETA = (8 * DEPTH) ** -0.25
SB_HEADS = 16
SB_HEAD_DIM = 64
SWA_Q_HEADS = 16
SWA_KV_HEADS = 2
SWA_HEAD_DIM = 64
SWA_WINDOW = 128
DIL_HEADS = 16
DIL_HEAD_DIM = 64
DIL_GROUPS = ((128, 1), (512, 4), (2048, 16))
N_SB = (DEPTH + 2) // 3
N_SWA = (DEPTH + 1) // 3
N_DIL = DEPTH // 3
NEG_INF = -1e30

kernel_name = 'hybrid_interleaved_sb_swa_dilated_macaron'


def _layernorm(x, g, b):
    xf = x.astype(jnp.float32)
    mu = xf.mean(-1, keepdims=True)
    var = jnp.square(xf - mu).mean(-1, keepdims=True)
    y = (xf - mu) * lax.rsqrt(var + LN_EPS) * g.astype(jnp.float32) + b.astype(jnp.float32)
    return y.astype(x.dtype)


def _swiglu(x, w_in, w_out):
    gate, up = jnp.split(x @ w_in, 2, axis=-1)
    return (jax.nn.silu(gate) * up) @ w_out


def _rope_tables(seq, dh):
    pos = jnp.arange(seq, dtype=jnp.float32)
    inv = ROPE_THETA ** (-jnp.arange(0, dh, 2, dtype=jnp.float32) / dh)
    ang = pos[:, None] * inv[None, :]
    return jnp.cos(ang), jnp.sin(ang)


def _rope(x, cos, sin):
    xf = x.astype(jnp.float32)
    half = xf.shape[-1] // 2
    x1, x2 = xf[..., :half], xf[..., half:]
    c, s = cos[None, :, None, :], sin[None, :, None, :]
    return jnp.concatenate([x1 * c - x2 * s, x2 * c + x1 * s], axis=-1).astype(x.dtype)


def _banded_attention(q, k, v, max_dist, sinks):
    bsz, L, hq, dh = q.shape
    hk = k.shape[2]
    g = hq // hk
    lp = -(-L // BLK) * BLK
    pad = ((0, 0), (0, lp - L), (0, 0), (0, 0))
    nb = lp // BLK
    qb = jnp.pad(q, pad).astype(jnp.float32).reshape(bsz, nb, BLK, hk, g, dh)
    kb = jnp.pad(k, pad).astype(jnp.float32).reshape(bsz, nb, BLK, hk, dh)
    vb = jnp.pad(v, pad).astype(jnp.float32).reshape(bsz, nb, BLK, hk, dh)
    prev = ((0, 0), (1, 0), (0, 0), (0, 0), (0, 0))
    kk = jnp.concatenate([jnp.pad(kb, prev)[:, :-1], kb], axis=2)
    vv = jnp.concatenate([jnp.pad(vb, prev)[:, :-1], vb], axis=2)
    s = jnp.einsum('bnqhgd,bnkhd->bnhgqk', qb, kk) * (dh ** -0.5)
    qi = jnp.arange(BLK)[:, None]
    kj = jnp.arange(2 * BLK)[None, :]
    diff = BLK + qi - kj
    blk = jnp.arange(nb)[:, None, None]
    valid = (diff >= 0) & (diff <= max_dist) & ((blk > 0) | (kj >= BLK))
    s = jnp.where(valid[None, :, None, None], s, NEG_INF)
    m = s.max(-1)
    if sinks is not None:
        sink = sinks.astype(jnp.float32).reshape(hk, g)[None, None, :, :, None]
        m = jnp.maximum(m, sink)
        e = jnp.exp(s - m[..., None])
        den = e.sum(-1) + jnp.exp(sink - m)
    else:
        e = jnp.exp(s - m[..., None])
        den = e.sum(-1)
    lse = m + jnp.log(den)
    o = jnp.einsum('bnhgqk,bnkhd->bnqhgd', e / den[..., None], vv)
    o = o.reshape(bsz, lp, hq, dh)[:, :L]
    lse = lse.transpose(0, 1, 4, 2, 3).reshape(bsz, lp, hq)[:, :L]
    return o, lse


def _stick_breaking(q, k, v):
    bsz, S, H, dh = q.shape
    sp_len = -(-S // BLK) * BLK
    nb = sp_len // BLK
    qb = jnp.pad(q, ((0, 0), (0, sp_len - S), (0, 0), (0, 0))).astype(jnp.float32)
    qb = qb.reshape(bsz, nb, BLK, H, dh).transpose(1, 0, 3, 2, 4)
    kf = k.astype(jnp.float32)
    vf = v.astype(jnp.float32)
    kpos = jnp.arange(S)

    def block(args):
        qblk, n = args
        z = jnp.einsum('bhqd,bshd->bhqs', qblk, kf) * (dh ** -0.5)
        qpos = n * BLK + jnp.arange(BLK)
        causal = kpos[None, :] < qpos[:, None]
        sp = jnp.where(causal, jax.nn.softplus(z), 0.0)
        after = lax.cumsum(sp, axis=3, reverse=True) - sp
        a = jnp.where(causal, jnp.exp(jax.nn.log_sigmoid(z) - after), 0.0)
        return jnp.einsum('bhqs,bshd->bqhd', a, vf)

    o = lax.map(block, (qb, jnp.arange(nb)))
    return o.transpose(1, 0, 2, 3, 4).reshape(bsz, sp_len, H, dh)[:, :S].astype(q.dtype)


def _sb_mixer(x, w_in, w_out):
    bsz, S, _ = x.shape
    h = (x @ w_in).reshape(bsz, S, 3, SB_HEADS, SB_HEAD_DIM)
    o = _stick_breaking(h[:, :, 0], h[:, :, 1], h[:, :, 2])
    return o.reshape(bsz, S, SB_HEADS * SB_HEAD_DIM) @ w_out


def _swa_mixer(x, w_in, sinks, w_out, cos, sin):
    bsz, S, _ = x.shape
    nq = SWA_Q_HEADS * SWA_HEAD_DIM
    nkv = SWA_KV_HEADS * SWA_HEAD_DIM
    h = x @ w_in
    q = _rope(h[..., :nq].reshape(bsz, S, SWA_Q_HEADS, SWA_HEAD_DIM), cos, sin)
    k = _rope(h[..., nq:nq + nkv].reshape(bsz, S, SWA_KV_HEADS, SWA_HEAD_DIM), cos, sin)
    v = h[..., nq + nkv:].reshape(bsz, S, SWA_KV_HEADS, SWA_HEAD_DIM)
    o, _ = _banded_attention(q, k, v, SWA_WINDOW - 1, sinks)
    return o.reshape(bsz, S, nq).astype(x.dtype) @ w_out


def _dilated_mixer(x, w_in, w_out, cos, sin):
    bsz, S, _ = x.shape
    h = (x @ w_in).reshape(bsz, S, len(DIL_GROUPS), 3, DIL_HEADS, DIL_HEAD_DIM)
    outs, lses = [], []
    for gi, (win, dil) in enumerate(DIL_GROUPS):
        q = _rope(h[:, :, gi, 0], cos, sin)
        k = _rope(h[:, :, gi, 1], cos, sin)
        v = h[:, :, gi, 2]
        sp_len = -(-S // dil) * dil
        ls = sp_len // dil

        def strided(a):
            a = jnp.pad(a, ((0, 0), (0, sp_len - S), (0, 0), (0, 0)))
            return a.reshape(bsz, ls, dil, DIL_HEADS, DIL_HEAD_DIM).transpose(0, 2, 1, 3, 4).reshape(
                bsz * dil, ls, DIL_HEADS, DIL_HEAD_DIM)

        o, lse = _banded_attention(strided(q), strided(k), strided(v), win // dil, None)
        o = o.reshape(bsz, dil, ls, DIL_HEADS, DIL_HEAD_DIM).transpose(0, 2, 1, 3, 4).reshape(
            bsz, sp_len, DIL_HEADS, DIL_HEAD_DIM)[:, :S]
        lse = lse.reshape(bsz, dil, ls, DIL_HEADS).transpose(0, 2, 1, 3).reshape(bsz, sp_len, DIL_HEADS)[:, :S]
        outs.append(o)
        lses.append(lse)
    w = jax.nn.softmax(jnp.stack(lses), axis=0)
    o = jnp.einsum('gbsh,gbshd->bshd', w, jnp.stack(outs))
    return o.reshape(bsz, S, DIL_HEADS * DIL_HEAD_DIM).astype(x.dtype) @ w_out


def _fwd_setup_inputs(seed: int = 0) -> dict:
    key = jax.random.key(seed)
    ks = jax.random.split(key, 18)
    f32 = jnp.float32

    def nrm(k, shape, scale):
        return jax.random.normal(k, shape, f32) * scale

    sb_w = SB_HEADS * SB_HEAD_DIM
    swa_in = (SWA_Q_HEADS + 2 * SWA_KV_HEADS) * SWA_HEAD_DIM
    swa_w = SWA_Q_HEADS * SWA_HEAD_DIM
    dil_in = len(DIL_GROUPS) * 3 * DIL_HEADS * DIL_HEAD_DIM
    dil_w = DIL_HEADS * DIL_HEAD_DIM
    return {
        'x': nrm(ks[0], (BATCH, SEQ, D_MODEL), 1.0),
        'p': nrm(ks[1], (DEPTH, BATCH, SEQ, PLE_DIM), 1.0),
        'ffn1_w_in': nrm(ks[2], (DEPTH, D_MODEL, 2 * D_FF), D_MODEL ** -0.5),
        'ffn1_w_out': nrm(ks[3], (DEPTH, D_FF, D_MODEL), D_FF ** -0.5 * DEEPNORM_BETA),
        'ffn2_w_in': nrm(ks[4], (DEPTH, D_MODEL, 2 * D_FF), D_MODEL ** -0.5),
        'ffn2_w_out': nrm(ks[5], (DEPTH, D_FF, D_MODEL), D_FF ** -0.5 * DEEPNORM_BETA),
        'ln_g': 1.0 + nrm(ks[6], (DEPTH, 3, D_MODEL), 0.02),
        'ln_b': nrm(ks[7], (DEPTH, 3, D_MODEL), 0.02),
        'sb_w_in': nrm(ks[8], (N_SB, D_MODEL, 3 * sb_w), D_MODEL ** -0.5),
        'sb_w_out': nrm(ks[9], (N_SB, sb_w, D_MODEL), sb_w ** -0.5 * DEEPNORM_BETA),
        'swa_w_in': nrm(ks[10], (N_SWA, D_MODEL, swa_in), D_MODEL ** -0.5),
        'swa_sinks': nrm(ks[11], (N_SWA, SWA_Q_HEADS), 0.5),
        'swa_w_out': nrm(ks[12], (N_SWA, swa_w, D_MODEL), swa_w ** -0.5 * DEEPNORM_BETA),
        'dil_w_in': nrm(ks[13], (N_DIL, D_MODEL, dil_in), D_MODEL ** -0.5),
        'dil_w_out': nrm(ks[14], (N_DIL, dil_w, D_MODEL), dil_w ** -0.5 * DEEPNORM_BETA),
        'ple_w_proj': nrm(ks[15], (DEPTH, PLE_DIM, D_MODEL), PLE_DIM ** -0.5),
        'ple_w_gate': nrm(ks[16], (DEPTH, D_MODEL, D_MODEL), D_MODEL ** -0.5),
    }


def _fwd_reference(x, p, ffn1_w_in, ffn1_w_out, ffn2_w_in, ffn2_w_out, ln_g, ln_b,
              sb_w_in, sb_w_out, swa_w_in, swa_sinks, swa_w_out, dil_w_in, dil_w_out,
              ple_w_proj, ple_w_gate):
    S = x.shape[1]
    cos, sin = _rope_tables(S, SWA_HEAD_DIM)
    a = DEEPNORM_ALPHA
    for i in range(DEPTH):
        kind, j = i % N_MIXERS, i // N_MIXERS
        x = _layernorm(a * x + 0.5 * _swiglu(x, ffn1_w_in[i], ffn1_w_out[i]), ln_g[i, 0], ln_b[i, 0])
        if kind == 0:
            mix = _sb_mixer(x, sb_w_in[j], sb_w_out[j])
        elif kind == 1:
            mix = _swa_mixer(x, swa_w_in[j], swa_sinks[j], swa_w_out[j], cos, sin)
        else:
            mix = _dilated_mixer(x, dil_w_in[j], dil_w_out[j], cos, sin)
        x = _layernorm(a * x + mix, ln_g[i, 1], ln_b[i, 1])
        x = _layernorm(a * x + 0.5 * _swiglu(x, ffn2_w_in[i], ffn2_w_out[i]), ln_g[i, 2], ln_b[i, 2])
        x = x + jax.nn.sigmoid(x @ ple_w_gate[i]) * (p[i] @ ple_w_proj[i])
    return x


import jax as _jax
import jax.numpy as _jnp

TWIN_FORMAT = 'train_step'
FWD_PARAMS = ['x', 'p', 'ffn1_w_in', 'ffn1_w_out', 'ffn2_w_in', 'ffn2_w_out', 'ln_g', 'ln_b', 'sb_w_in', 'sb_w_out', 'swa_w_in', 'swa_sinks', 'swa_w_out', 'dil_w_in', 'dil_w_out', 'ple_w_proj', 'ple_w_gate']
TWIN_WEIGHTS = ['ffn1_w_in', 'ffn1_w_out', 'ffn2_w_in', 'ffn2_w_out', 'ln_g', 'ln_b', 'sb_w_in', 'sb_w_out', 'swa_w_in', 'swa_sinks', 'swa_w_out', 'dil_w_in', 'dil_w_out', 'ple_w_proj', 'ple_w_gate']
TWIN_DIFF_INPUT = 'x'
TWIN_INPUTS = ['x', 'p', 'ffn1_w_in', 'ffn1_w_out', 'ffn2_w_in', 'ffn2_w_out', 'ln_g', 'ln_b', 'sb_w_in', 'sb_w_out', 'swa_w_in', 'swa_sinks', 'swa_w_out', 'dil_w_in', 'dil_w_out', 'ple_w_proj', 'ple_w_gate', 'loss_target', 'm_ffn1_w_in', 'm_ffn1_w_out', 'm_ffn2_w_in', 'm_ffn2_w_out', 'm_ln_g', 'm_ln_b', 'm_sb_w_in', 'm_sb_w_out', 'm_swa_w_in', 'm_swa_sinks', 'm_swa_w_out', 'm_dil_w_in', 'm_dil_w_out', 'm_ple_w_proj', 'm_ple_w_gate', 'v_ffn1_w_in', 'v_ffn1_w_out', 'v_ffn2_w_in', 'v_ffn2_w_out', 'v_ln_g', 'v_ln_b', 'v_sb_w_in', 'v_sb_w_out', 'v_swa_w_in', 'v_swa_sinks', 'v_swa_w_out', 'v_dil_w_in', 'v_dil_w_out', 'v_ple_w_proj', 'v_ple_w_gate']
TWIN_OUTPUTS = ['loss', 'grad_x', 'grad_ffn1_w_in', 'grad_ffn1_w_out', 'grad_ffn2_w_in', 'grad_ffn2_w_out', 'grad_ln_g', 'grad_ln_b', 'grad_sb_w_in', 'grad_sb_w_out', 'grad_swa_w_in', 'grad_swa_sinks', 'grad_swa_w_out', 'grad_dil_w_in', 'grad_dil_w_out', 'grad_ple_w_proj', 'grad_ple_w_gate', 'delta_ffn1_w_in', 'delta_ffn1_w_out', 'delta_ffn2_w_in', 'delta_ffn2_w_out', 'delta_ln_g', 'delta_ln_b', 'delta_sb_w_in', 'delta_sb_w_out', 'delta_swa_w_in', 'delta_swa_sinks', 'delta_swa_w_out', 'delta_dil_w_in', 'delta_dil_w_out', 'delta_ple_w_proj', 'delta_ple_w_gate', 'new_m_ffn1_w_in', 'new_m_ffn1_w_out', 'new_m_ffn2_w_in', 'new_m_ffn2_w_out', 'new_m_ln_g', 'new_m_ln_b', 'new_m_sb_w_in', 'new_m_sb_w_out', 'new_m_swa_w_in', 'new_m_swa_sinks', 'new_m_swa_w_out', 'new_m_dil_w_in', 'new_m_dil_w_out', 'new_m_ple_w_proj', 'new_m_ple_w_gate', 'new_v_ffn1_w_in', 'new_v_ffn1_w_out', 'new_v_ffn2_w_in', 'new_v_ffn2_w_out', 'new_v_ln_g', 'new_v_ln_b', 'new_v_sb_w_in', 'new_v_sb_w_out', 'new_v_swa_w_in', 'new_v_swa_sinks', 'new_v_swa_w_out', 'new_v_dil_w_in', 'new_v_dil_w_out', 'new_v_ple_w_proj', 'new_v_ple_w_gate']
TWIN_LEAF_KINDS = {'loss': 'loss', 'grad_x': 'grad_x', 'grad_ffn1_w_in': 'grad_w', 'grad_ffn1_w_out': 'grad_w', 'grad_ffn2_w_in': 'grad_w', 'grad_ffn2_w_out': 'grad_w', 'grad_ln_g': 'grad_w', 'grad_ln_b': 'grad_w', 'grad_sb_w_in': 'grad_w', 'grad_sb_w_out': 'grad_w', 'grad_swa_w_in': 'grad_w', 'grad_swa_sinks': 'grad_w', 'grad_swa_w_out': 'grad_w', 'grad_dil_w_in': 'grad_w', 'grad_dil_w_out': 'grad_w', 'grad_ple_w_proj': 'grad_w', 'grad_ple_w_gate': 'grad_w', 'delta_ffn1_w_in': 'delta_w', 'delta_ffn1_w_out': 'delta_w', 'delta_ffn2_w_in': 'delta_w', 'delta_ffn2_w_out': 'delta_w', 'delta_ln_g': 'delta_w', 'delta_ln_b': 'delta_w', 'delta_sb_w_in': 'delta_w', 'delta_sb_w_out': 'delta_w', 'delta_swa_w_in': 'delta_w', 'delta_swa_sinks': 'delta_w', 'delta_swa_w_out': 'delta_w', 'delta_dil_w_in': 'delta_w', 'delta_dil_w_out': 'delta_w', 'delta_ple_w_proj': 'delta_w', 'delta_ple_w_gate': 'delta_w', 'new_m_ffn1_w_in': 'new_m', 'new_m_ffn1_w_out': 'new_m', 'new_m_ffn2_w_in': 'new_m', 'new_m_ffn2_w_out': 'new_m', 'new_m_ln_g': 'new_m', 'new_m_ln_b': 'new_m', 'new_m_sb_w_in': 'new_m', 'new_m_sb_w_out': 'new_m', 'new_m_swa_w_in': 'new_m', 'new_m_swa_sinks': 'new_m', 'new_m_swa_w_out': 'new_m', 'new_m_dil_w_in': 'new_m', 'new_m_dil_w_out': 'new_m', 'new_m_ple_w_proj': 'new_m', 'new_m_ple_w_gate': 'new_m', 'new_v_ffn1_w_in': 'new_v', 'new_v_ffn1_w_out': 'new_v', 'new_v_ffn2_w_in': 'new_v', 'new_v_ffn2_w_out': 'new_v', 'new_v_ln_g': 'new_v', 'new_v_ln_b': 'new_v', 'new_v_sb_w_in': 'new_v', 'new_v_sb_w_out': 'new_v', 'new_v_swa_w_in': 'new_v', 'new_v_swa_sinks': 'new_v', 'new_v_swa_w_out': 'new_v', 'new_v_dil_w_in': 'new_v', 'new_v_dil_w_out': 'new_v', 'new_v_ple_w_proj': 'new_v', 'new_v_ple_w_gate': 'new_v'}


def _forward(args):
    return _fwd_reference(*[args[k] for k in FWD_PARAMS])


def _output_shape():
    def fwd():
        inp = _fwd_setup_inputs(0)
        return _fwd_reference(*[inp[k] for k in FWD_PARAMS])
    out = _jax.eval_shape(fwd)
    return out.shape, out.dtype

N_MICROBATCH = 1
ADAM_LR = 0.001
ADAM_B1 = 0.9
ADAM_B2 = 0.999
ADAM_EPS = 1e-08
ADAM_WD = 0.01
ADAM_STEP = 10
PER_EXAMPLE_BATCH_AXIS = {'x': 0, 'p': 1, 'loss_target': 0}
SHARED_INPUTS = []
_WEIGHT_DTYPES = {'ffn1_w_in': _jnp.float32, 'ffn1_w_out': _jnp.float32, 'ffn2_w_in': _jnp.float32, 'ffn2_w_out': _jnp.float32, 'ln_g': _jnp.float32, 'ln_b': _jnp.float32, 'sb_w_in': _jnp.float32, 'sb_w_out': _jnp.float32, 'swa_w_in': _jnp.float32, 'swa_sinks': _jnp.float32, 'swa_w_out': _jnp.float32, 'dil_w_in': _jnp.float32, 'dil_w_out': _jnp.float32, 'ple_w_proj': _jnp.float32, 'ple_w_gate': _jnp.float32}
MOMENT_SCALE = {'ffn1_w_in': 1.426690e-02, 'ffn1_w_out': 5.579253e-02, 'ffn2_w_in': 1.254507e-02, 'ffn2_w_out': 4.920495e-02, 'ln_g': 1.912559e+01, 'ln_b': 5.627309e+00, 'sb_w_in': 3.522974e-02, 'sb_w_out': 1.322824e-01, 'swa_w_in': 2.293114e-02, 'swa_sinks': 1.029154e-02, 'swa_w_out': 6.538931e-02, 'dil_w_in': 7.830695e-03, 'dil_w_out': 4.726211e-02, 'ple_w_proj': 4.242300e-01, 'ple_w_gate': 1.427332e-01}


def _to_microbatches(a, axis):
    t = _jnp.moveaxis(a, axis, 0)
    t = t.reshape((N_MICROBATCH, t.shape[0] // N_MICROBATCH) + t.shape[1:])
    return _jnp.moveaxis(t, 1, axis + 1)


def setup_inputs(seed: int = 0) -> dict:
    inp = _fwd_setup_inputs(seed)
    key = _jax.random.fold_in(_jax.random.key(seed), 7919)
    shape, _ = _output_shape()
    out = dict(inp)
    out["loss_target"] = _jax.random.normal(_jax.random.fold_in(key, 0), shape, _jnp.float32)
    for i, name in enumerate(TWIN_WEIGHTS):
        w = inp[name].astype(_jnp.float32)
        if MOMENT_SCALE is None:
            s = _jnp.sqrt(_jnp.mean(_jnp.square(w)) + 1e-30)
        else:
            s = MOMENT_SCALE[name]
        km, kv = _jax.random.split(_jax.random.fold_in(key, i + 1))
        out[name] = w
        out["m_" + name] = s * _jax.random.normal(km, w.shape, _jnp.float32)
        out["v_" + name] = (s * s) * _jax.random.uniform(kv, w.shape, _jnp.float32, 0.5, 1.5)
    if N_MICROBATCH > 1:
        for name, axis in PER_EXAMPLE_BATCH_AXIS.items():
            out[name] = _to_microbatches(out[name], axis)
    return {'x': out['x'], 'p': out['p'], 'ffn1_w_in': out['ffn1_w_in'], 'ffn1_w_out': out['ffn1_w_out'], 'ffn2_w_in': out['ffn2_w_in'], 'ffn2_w_out': out['ffn2_w_out'], 'ln_g': out['ln_g'], 'ln_b': out['ln_b'], 'sb_w_in': out['sb_w_in'], 'sb_w_out': out['sb_w_out'], 'swa_w_in': out['swa_w_in'], 'swa_sinks': out['swa_sinks'], 'swa_w_out': out['swa_w_out'], 'dil_w_in': out['dil_w_in'], 'dil_w_out': out['dil_w_out'], 'ple_w_proj': out['ple_w_proj'], 'ple_w_gate': out['ple_w_gate'], 'loss_target': out['loss_target'], 'm_ffn1_w_in': out['m_ffn1_w_in'], 'm_ffn1_w_out': out['m_ffn1_w_out'], 'm_ffn2_w_in': out['m_ffn2_w_in'], 'm_ffn2_w_out': out['m_ffn2_w_out'], 'm_ln_g': out['m_ln_g'], 'm_ln_b': out['m_ln_b'], 'm_sb_w_in': out['m_sb_w_in'], 'm_sb_w_out': out['m_sb_w_out'], 'm_swa_w_in': out['m_swa_w_in'], 'm_swa_sinks': out['m_swa_sinks'], 'm_swa_w_out': out['m_swa_w_out'], 'm_dil_w_in': out['m_dil_w_in'], 'm_dil_w_out': out['m_dil_w_out'], 'm_ple_w_proj': out['m_ple_w_proj'], 'm_ple_w_gate': out['m_ple_w_gate'], 'v_ffn1_w_in': out['v_ffn1_w_in'], 'v_ffn1_w_out': out['v_ffn1_w_out'], 'v_ffn2_w_in': out['v_ffn2_w_in'], 'v_ffn2_w_out': out['v_ffn2_w_out'], 'v_ln_g': out['v_ln_g'], 'v_ln_b': out['v_ln_b'], 'v_sb_w_in': out['v_sb_w_in'], 'v_sb_w_out': out['v_sb_w_out'], 'v_swa_w_in': out['v_swa_w_in'], 'v_swa_sinks': out['v_swa_sinks'], 'v_swa_w_out': out['v_swa_w_out'], 'v_dil_w_in': out['v_dil_w_in'], 'v_dil_w_out': out['v_dil_w_out'], 'v_ple_w_proj': out['v_ple_w_proj'], 'v_ple_w_gate': out['v_ple_w_gate']}


def _loss(weights, diff, rest, loss_target):
    with _jax.named_scope("forward"):
        args = {**rest, TWIN_DIFF_INPUT: diff, **{k: w.astype(_WEIGHT_DTYPES[k]) for k, w in weights.items()}}
        y = _forward(args)
    with _jax.named_scope("loss_head"):
        err = _jnp.square(y.astype(_jnp.float32) - loss_target)
        return 0.5 * _jnp.sum(_jnp.mean(err, axis=-1)) if err.ndim else 0.5 * err


def _adamw(w, g, m, v):
    m = ADAM_B1 * m + (1.0 - ADAM_B1) * g
    v = ADAM_B2 * v + (1.0 - ADAM_B2) * _jnp.square(g)
    m_hat = m / (1.0 - ADAM_B1 ** ADAM_STEP)
    v_hat = v / (1.0 - ADAM_B2 ** ADAM_STEP)
    delta = -ADAM_LR * (m_hat / (_jnp.sqrt(v_hat) + ADAM_EPS) + ADAM_WD * w)
    return delta, m, v


def reference(x, p, ffn1_w_in, ffn1_w_out, ffn2_w_in, ffn2_w_out, ln_g, ln_b, sb_w_in, sb_w_out, swa_w_in, swa_sinks, swa_w_out, dil_w_in, dil_w_out, ple_w_proj, ple_w_gate, loss_target, m_ffn1_w_in, m_ffn1_w_out, m_ffn2_w_in, m_ffn2_w_out, m_ln_g, m_ln_b, m_sb_w_in, m_sb_w_out, m_swa_w_in, m_swa_sinks, m_swa_w_out, m_dil_w_in, m_dil_w_out, m_ple_w_proj, m_ple_w_gate, v_ffn1_w_in, v_ffn1_w_out, v_ffn2_w_in, v_ffn2_w_out, v_ln_g, v_ln_b, v_sb_w_in, v_sb_w_out, v_swa_w_in, v_swa_sinks, v_swa_w_out, v_dil_w_in, v_dil_w_out, v_ple_w_proj, v_ple_w_gate):
    given = dict(x=x, p=p, ffn1_w_in=ffn1_w_in, ffn1_w_out=ffn1_w_out, ffn2_w_in=ffn2_w_in, ffn2_w_out=ffn2_w_out, ln_g=ln_g, ln_b=ln_b, sb_w_in=sb_w_in, sb_w_out=sb_w_out, swa_w_in=swa_w_in, swa_sinks=swa_sinks, swa_w_out=swa_w_out, dil_w_in=dil_w_in, dil_w_out=dil_w_out, ple_w_proj=ple_w_proj, ple_w_gate=ple_w_gate, loss_target=loss_target, m_ffn1_w_in=m_ffn1_w_in, m_ffn1_w_out=m_ffn1_w_out, m_ffn2_w_in=m_ffn2_w_in, m_ffn2_w_out=m_ffn2_w_out, m_ln_g=m_ln_g, m_ln_b=m_ln_b, m_sb_w_in=m_sb_w_in, m_sb_w_out=m_sb_w_out, m_swa_w_in=m_swa_w_in, m_swa_sinks=m_swa_sinks, m_swa_w_out=m_swa_w_out, m_dil_w_in=m_dil_w_in, m_dil_w_out=m_dil_w_out, m_ple_w_proj=m_ple_w_proj, m_ple_w_gate=m_ple_w_gate, v_ffn1_w_in=v_ffn1_w_in, v_ffn1_w_out=v_ffn1_w_out, v_ffn2_w_in=v_ffn2_w_in, v_ffn2_w_out=v_ffn2_w_out, v_ln_g=v_ln_g, v_ln_b=v_ln_b, v_sb_w_in=v_sb_w_in, v_sb_w_out=v_sb_w_out, v_swa_w_in=v_swa_w_in, v_swa_sinks=v_swa_sinks, v_swa_w_out=v_swa_w_out, v_dil_w_in=v_dil_w_in, v_dil_w_out=v_dil_w_out, v_ple_w_proj=v_ple_w_proj, v_ple_w_gate=v_ple_w_gate)
    weights = {n: given[n] for n in TWIN_WEIGHTS}
    shared = {n: given[n] for n in SHARED_INPUTS}
    per_example = {n: given[n] for n in ['x', 'p']}
    grad_fn = _jax.value_and_grad(_loss, argnums=(0, 1))

    def one_microbatch(ex, loss_target):
        ex = dict(ex)
        diff = ex.pop(TWIN_DIFF_INPUT)
        return grad_fn(weights, diff, {**shared, **ex}, loss_target)

    if N_MICROBATCH == 1:
        loss, (grad_w, grad_x) = one_microbatch(per_example, given["loss_target"])
    else:
        def body(carry, xs):
            loss_sum, grad_sum = carry
            l_k, (gw_k, gx_k) = one_microbatch(xs[0], xs[1])
            with _jax.named_scope("update"):
                return (loss_sum + l_k, _jax.tree.map(_jnp.add, grad_sum, gw_k)), gx_k

        init = (_jnp.zeros((), _jnp.float32), _jax.tree.map(_jnp.zeros_like, weights))
        (loss, grad_w), grad_x = _jax.lax.scan(body, init, (per_example, given["loss_target"]))
    with _jax.named_scope("update"):
        delta_w, new_m, new_v = {}, {}, {}
        for n in TWIN_WEIGHTS:
            delta_w[n], new_m[n], new_v[n] = _adamw(weights[n], grad_w[n], given["m_" + n], given["v_" + n])
    return (loss, grad_x, *[grad_w[n] for n in TWIN_WEIGHTS], *[delta_w[n] for n in TWIN_WEIGHTS],
            *[new_m[n] for n in TWIN_WEIGHTS], *[new_v[n] for n in TWIN_WEIGHTS])
```

```python
import functools
import math

import jax
import jax.numpy as jnp
from jax import lax
from jax.experimental import pallas as pl
from jax.experimental.pallas import tpu as pltpu

F32 = jnp.float32
MXU_DTYPE = jnp.bfloat16
MESH = pl.DeviceIdType.MESH

HEAD_DIM = 64
ATT_BLK = 128
SWA_WINDOW = 128
DIL_GROUPS = ((128, 1), (512, 4), (2048, 16))
LN_EPS = 1e-5
ROPE_THETA = 10000.0
NEG_INF = -1e30
ADAM_LR, ADAM_B1, ADAM_B2, ADAM_EPS, ADAM_WD, ADAM_STEP = 0.001, 0.9, 0.999, 1e-08, 0.01, 10

VMEM_LIMIT_BYTES = 56 * 1024 * 1024
FLAT_COLS = 1024
N_CHIPS = 4


def _params(sem=None):
    return pltpu.CompilerParams(dimension_semantics=sem, vmem_limit_bytes=VMEM_LIMIT_BYTES)


def _tile(n, target, unit):
    t = (min(target, n) // unit) * unit
    while t >= unit:
        if n % t == 0:
            return t
        t -= unit
    return n


def _dot(a, b, dims):
    return lax.dot_general(a.astype(MXU_DTYPE), b.astype(MXU_DTYPE), (dims, ((), ())), preferred_element_type=F32)


NN = ((1,), (0,))
NT = ((1,), (1,))
TN = ((0,), (0,))


def _mm(a, b, mode, out_dtype, name, split_a=False, split_b=False, tm=1024, tn=1408, tk=1408):
    dims = {"nn": NN, "nt": NT, "tn": TN}[mode]
    if split_a:
        m, k = a.shape[1], 2 * a.shape[2]
    elif mode == "tn":
        k, m = a.shape
    else:
        m, k = a.shape
    if split_b:
        n = 2 * b.shape[2]
    elif mode == "nt":
        n = b.shape[0]
    else:
        n = b.shape[1]
    tm = _tile(m, tm, 128)
    tn = _tile(n // 2 if split_b else n, tn, 128)
    tk = _tile(k // 2 if split_a else k, tk, 128)
    nk = k // tk
    nk_half = nk // 2
    nn_half = (n // tn) // 2

    if split_a:
        a_spec = pl.BlockSpec((None, tm, tk), lambda i, j, kk: (kk // nk_half, i, kk % nk_half))
    elif mode == "tn":
        a_spec = pl.BlockSpec((tk, tm), lambda i, j, kk: (kk, i))
    else:
        a_spec = pl.BlockSpec((tm, tk), lambda i, j, kk: (i, kk))
    if split_b:
        b_spec = pl.BlockSpec((None, tk, tn), lambda i, j, kk: (j // nn_half, kk, j % nn_half))
    elif mode == "nt":
        b_spec = pl.BlockSpec((tn, tk), lambda i, j, kk: (j, kk))
    else:
        b_spec = pl.BlockSpec((tk, tn), lambda i, j, kk: (kk, j))

    def body(a_ref, b_ref, o_ref, acc_ref):
        kk = pl.program_id(2)

        @pl.when(kk == 0)
        def _():
            acc_ref[...] = jnp.zeros_like(acc_ref)

        acc_ref[...] += _dot(a_ref[...], b_ref[...], dims)

        @pl.when(kk == nk - 1)
        def _():
            o_ref[...] = acc_ref[...].astype(o_ref.dtype)

    return pl.pallas_call(
        body,
        name=name,
        grid=(m // tm, n // tn, nk),
        in_specs=[a_spec, b_spec],
        out_specs=pl.BlockSpec((tm, tn), lambda i, j, kk: (i, j)),
        out_shape=jax.ShapeDtypeStruct((m, n), out_dtype),
        scratch_shapes=[pltpu.VMEM((tm, tn), F32)],
        compiler_params=_params(("parallel", "parallel", "arbitrary")),
    )(a, b)


def _sigmoid(x):
    return 1.0 / (1.0 + jnp.exp(-x))


def _ffn_in(xb, w_in, name):
    s, d = xb.shape
    f = w_in.shape[1] // 2
    tm = _tile(s, 1024, 128)
    tn = _tile(f, 256, 128)
    nj = f // tn

    def body(x_ref, wg_ref, wu_ref, g_ref, u_ref, a_ref):
        x = x_ref[...]
        g = _dot(x, wg_ref[...], NN)
        u = _dot(x, wu_ref[...], NN)
        g_ref[...] = g
        u_ref[...] = u
        a_ref[...] = (g * _sigmoid(g) * u).astype(a_ref.dtype)

    out = pl.BlockSpec((tm, tn), lambda i, j: (i, j))
    return pl.pallas_call(
        body,
        name=name,
        grid=(s // tm, nj),
        in_specs=[
            pl.BlockSpec((tm, d), lambda i, j: (i, 0)),
            pl.BlockSpec((d, tn), lambda i, j: (0, j)),
            pl.BlockSpec((d, tn), lambda i, j: (0, j + nj)),
        ],
        out_specs=[out, out, out],
        out_shape=[
            jax.ShapeDtypeStruct((s, f), F32),
            jax.ShapeDtypeStruct((s, f), F32),
            jax.ShapeDtypeStruct((s, f), MXU_DTYPE),
        ],
        compiler_params=_params(("parallel", "parallel")),
    )(xb, w_in, w_in)


def _ffn_dact(dyb, w_out, gate, up, name):
    s, d = dyb.shape
    f = w_out.shape[0]
    tm = _tile(s, 1024, 128)
    tn = _tile(f, 256, 128)

    def body(dy_ref, w_ref, g_ref, u_ref, o_ref):
        dact = _dot(dy_ref[...], w_ref[...], NT)
        g = g_ref[...]
        sig = _sigmoid(g)
        o_ref[0] = (dact * u_ref[...] * (sig * (1.0 + g * (1.0 - sig)))).astype(o_ref.dtype)
        o_ref[1] = (dact * (g * sig)).astype(o_ref.dtype)

    tile = pl.BlockSpec((tm, tn), lambda i, j: (i, j))
    return pl.pallas_call(
        body,
        name=name,
        grid=(s // tm, f // tn),
        in_specs=[
            pl.BlockSpec((tm, d), lambda i, j: (i, 0)),
            pl.BlockSpec((tn, d), lambda i, j: (j, 0)),
            tile,
            tile,
        ],
        out_specs=pl.BlockSpec((2, tm, tn), lambda i, j: (0, i, j)),
        out_shape=jax.ShapeDtypeStruct((2, s, f), MXU_DTYPE),
        compiler_params=_params(("parallel", "parallel")),
    )(dyb, w_out, gate, up)


def _ple_fwd(x, xb, p, w_gate, w_proj, name):
    s, d = x.shape
    pd = p.shape[1]
    tm = _tile(s, 1024, 128)
    tn = _tile(d, 512, 128)

    def body(x_ref, xb_ref, p_ref, wg_ref, wp_ref, o_ref, ob_ref, u_ref, e_ref):
        u = _dot(xb_ref[...], wg_ref[...], NN)
        e = _dot(p_ref[...], wp_ref[...], NN)
        out = x_ref[...] + _sigmoid(u) * e
        o_ref[...] = out
        ob_ref[...] = out.astype(ob_ref.dtype)
        u_ref[...] = u
        e_ref[...] = e

    tile = pl.BlockSpec((tm, tn), lambda i, j: (i, j))
    return pl.pallas_call(
        body,
        name=name,
        grid=(s // tm, d // tn),
        in_specs=[
            tile,
            pl.BlockSpec((tm, d), lambda i, j: (i, 0)),
            pl.BlockSpec((tm, pd), lambda i, j: (i, 0)),
            pl.BlockSpec((d, tn), lambda i, j: (0, j)),
            pl.BlockSpec((pd, tn), lambda i, j: (0, j)),
        ],
        out_specs=[tile, tile, tile, tile],
        out_shape=[
            jax.ShapeDtypeStruct((s, d), F32),
            jax.ShapeDtypeStruct((s, d), MXU_DTYPE),
            jax.ShapeDtypeStruct((s, d), F32),
            jax.ShapeDtypeStruct((s, d), F32),
        ],
        compiler_params=_params(("parallel", "parallel")),
    )(x, xb, p, w_gate, w_proj)


def _rows_spec(ts, d):
    return pl.BlockSpec((ts, d), lambda i: (i, 0))


def _ln_fwd(x, y, g, b, alpha, beta, name):
    s, d = x.shape
    ts = _tile(s, 512, 8)

    def body(x_ref, y_ref, g_ref, b_ref, o_ref, ob_ref, xh_ref, rs_ref):
        r = alpha * x_ref[...] + beta * y_ref[...]
        mu = jnp.mean(r, axis=1, keepdims=True)
        cen = r - mu
        var = jnp.mean(cen * cen, axis=1, keepdims=True)
        rstd = lax.rsqrt(var + LN_EPS)
        xhat = cen * rstd
        out = xhat * g_ref[...] + b_ref[...]
        o_ref[...] = out
        ob_ref[...] = out.astype(ob_ref.dtype)
        xh_ref[...] = xhat
        rs_ref[...] = rstd

    vec = pl.BlockSpec((1, d), lambda i: (0, 0))
    return pl.pallas_call(
        body,
        name=name,
        grid=(s // ts,),
        in_specs=[_rows_spec(ts, d), _rows_spec(ts, d), vec, vec],
        out_specs=[_rows_spec(ts, d), _rows_spec(ts, d), _rows_spec(ts, d), _rows_spec(ts, 1)],
        out_shape=[
            jax.ShapeDtypeStruct((s, d), F32),
            jax.ShapeDtypeStruct((s, d), MXU_DTYPE),
            jax.ShapeDtypeStruct((s, d), F32),
            jax.ShapeDtypeStruct((s, 1), F32),
        ],
        compiler_params=_params(("parallel",)),
    )(x, y, g.reshape(1, d), b.reshape(1, d))


def _ln_bwd(ga, gb, ca, xhat, rstd, g, beta, name):
    s, d = xhat.shape
    ts = _tile(s, 512, 8)

    def body(ga_ref, gb_ref, xh_ref, rs_ref, g_ref, dr_ref, dyb_ref, dg_ref, db_ref):
        @pl.when(pl.program_id(0) == 0)
        def _():
            dg_ref[...] = jnp.zeros_like(dg_ref)
            db_ref[...] = jnp.zeros_like(db_ref)

        dout = ca * ga_ref[...] + gb_ref[...]
        xhat = xh_ref[...]
        dg_ref[...] += jnp.sum(dout * xhat, axis=0, keepdims=True)
        db_ref[...] += jnp.sum(dout, axis=0, keepdims=True)
        dxh = dout * g_ref[...]
        m1 = jnp.mean(dxh, axis=1, keepdims=True)
        m2 = jnp.mean(dxh * xhat, axis=1, keepdims=True)
        dr = rs_ref[...] * (dxh - m1 - xhat * m2)
        dr_ref[...] = dr
        dyb_ref[...] = (beta * dr).astype(dyb_ref.dtype)

    vec = pl.BlockSpec((1, d), lambda i: (0, 0))
    return pl.pallas_call(
        body,
        name=name,
        grid=(s // ts,),
        in_specs=[_rows_spec(ts, d), _rows_spec(ts, d), _rows_spec(ts, d), _rows_spec(ts, 1), vec],
        out_specs=[_rows_spec(ts, d), _rows_spec(ts, d), vec, vec],
        out_shape=[
            jax.ShapeDtypeStruct((s, d), F32),
            jax.ShapeDtypeStruct((s, d), MXU_DTYPE),
            jax.ShapeDtypeStruct((1, d), F32),
            jax.ShapeDtypeStruct((1, d), F32),
        ],
        compiler_params=_params(("arbitrary",)),
    )(ga, gb, xhat, rstd, g.reshape(1, d))


def _ple_bwd(ga, gb, ca, u, e, name):
    s, d = u.shape
    ts = _tile(s, 512, 8)

    def body(ga_ref, gb_ref, u_ref, e_ref, dx_ref, du_ref, de_ref):
        dx = ca * ga_ref[...] + gb_ref[...]
        sig = _sigmoid(u_ref[...])
        dx_ref[...] = dx
        du_ref[...] = (dx * e_ref[...] * sig * (1.0 - sig)).astype(du_ref.dtype)
        de_ref[...] = (dx * sig).astype(de_ref.dtype)

    return pl.pallas_call(
        body,
        name=name,
        grid=(s // ts,),
        in_specs=[_rows_spec(ts, d)] * 4,
        out_specs=[_rows_spec(ts, d)] * 3,
        out_shape=[
            jax.ShapeDtypeStruct((s, d), F32),
            jax.ShapeDtypeStruct((s, d), MXU_DTYPE),
            jax.ShapeDtypeStruct((s, d), MXU_DTYPE),
        ],
        compiler_params=_params(("parallel",)),
    )(ga, gb, u, e)


def _axpy(ga, gb, ca, name):
    s, d = ga.shape
    ts = _tile(s, 512, 8)

    def body(ga_ref, gb_ref, o_ref):
        o_ref[...] = ca * ga_ref[...] + gb_ref[...]

    return pl.pallas_call(
        body,
        name=name,
        grid=(s // ts,),
        in_specs=[_rows_spec(ts, d)] * 2,
        out_specs=_rows_spec(ts, d),
        out_shape=jax.ShapeDtypeStruct((s, d), F32),
        compiler_params=_params(("parallel",)),
    )(ga, gb)


def _loss(y, target, name):
    s, d = y.shape
    ts = _tile(s, 512, 8)

    def body(y_ref, t_ref, l_ref, z_ref, dy_ref):
        @pl.when(pl.program_id(0) == 0)
        def _():
            l_ref[...] = jnp.zeros_like(l_ref)

        err = y_ref[...] - t_ref[...]
        l_ref[...] += (0.5 / d) * jnp.sum(jnp.sum(err * err, axis=1, keepdims=True), axis=0, keepdims=True)
        dy_ref[...] = err * (1.0 / d)
        z_ref[...] = jnp.zeros_like(z_ref)

    return pl.pallas_call(
        body,
        name=name,
        grid=(s // ts,),
        in_specs=[_rows_spec(ts, d)] * 2,
        out_specs=[pl.BlockSpec((1, 1), lambda i: (0, 0)), _rows_spec(ts, d), _rows_spec(ts, d)],
        out_shape=[
            jax.ShapeDtypeStruct((1, 1), F32),
            jax.ShapeDtypeStruct((s, d), F32),
            jax.ShapeDtypeStruct((s, d), F32),
        ],
        compiler_params=_params(("arbitrary",)),
    )(y, target)


def _rope_tables(seq):
    pos = jnp.arange(seq, dtype=F32)
    inv = ROPE_THETA ** (-jnp.arange(0, HEAD_DIM, 2, dtype=F32) / HEAD_DIM)
    ang = pos[:, None] * inv[None, :]
    cos, sin = jnp.cos(ang), jnp.sin(ang)
    cos2 = jnp.concatenate([cos, cos, cos, cos, jnp.ones((seq, 128), F32)], axis=1)
    sin2 = jnp.concatenate([-sin, sin, -sin, sin, jnp.zeros((seq, 128), F32)], axis=1)
    return cos2, sin2


def _rope(h, cos2, sin2, plain_block, out_dtype, name):
    s, n = h.shape
    ts = _tile(s, 512, 8)

    def body(h_ref, c_ref, s_ref, o_ref):
        x = h_ref[...].astype(F32)
        lane = lax.broadcasted_iota(jnp.int32, x.shape, 1)
        partner = jnp.where(lane % HEAD_DIM < HEAD_DIM // 2, pltpu.roll(x, 128 - HEAD_DIM // 2, 1), pltpu.roll(x, HEAD_DIM // 2, 1))
        o_ref[...] = (x * c_ref[...] + partner * s_ref[...]).astype(o_ref.dtype)

    tile = pl.BlockSpec((ts, 128), lambda i, j: (i, j))
    table = pl.BlockSpec((ts, 128), lambda i, j: (i, plain_block(j)))
    return pl.pallas_call(
        body,
        name=name,
        grid=(s // ts, n // 128),
        in_specs=[tile, table, table],
        out_specs=tile,
        out_shape=jax.ShapeDtypeStruct((s, n), out_dtype),
        compiler_params=_params(("parallel", "parallel")),
    )(h, cos2, sin2)


SB_TQ = 256
SB_TK = 128


def _split_dot(x, tri):
    hi = x.astype(MXU_DTYPE)
    lo = x - hi.astype(F32)
    return _dot(hi, tri, NN) + _dot(lo, tri, NN)


def _sb_tile(qs, kb, after_c, tri_after, valid):
    z = _dot(qs, kb, NT)
    l1p = jnp.log(1.0 + jnp.exp(-jnp.abs(z)))
    sp = jnp.maximum(z, 0.0) + l1p
    ls = jnp.minimum(z, 0.0) - l1p
    if valid is not None:
        sp = jnp.where(valid, sp, 0.0)
    after = after_c + _split_dot(sp, tri_after)
    a = jnp.exp(ls - after)
    if valid is not None:
        a = jnp.where(valid, a, 0.0)
    return sp, ls, a


def _sb_fwd(q, k, v, name):
    nh, s, dh = q.shape
    tq = min(SB_TQ, s)
    tk = SB_TK
    r = tq // tk
    scale = dh ** -0.5

    def body(q_ref, k_ref, v_ref, o_ref, ob_ref):
        i = pl.program_id(1)
        qs = q_ref[0] * scale
        qpos = i * tq + lax.broadcasted_iota(jnp.int32, (tq, tk), 0)
        kcol = lax.broadcasted_iota(jnp.int32, (tq, tk), 1)
        tri_after = (lax.broadcasted_iota(jnp.int32, (tk, tk), 0) > lax.broadcasted_iota(jnp.int32, (tk, tk), 1)).astype(MXU_DTYPE)

        def step(jj, carry, masked):
            after_c, acc = carry
            start = pl.multiple_of(jj * tk, tk)
            kb = k_ref[0, pl.ds(start, tk), :]
            vb = v_ref[0, pl.ds(start, tk), :]
            valid = (jj * tk + kcol < qpos) if masked else None
            sp, _, a = _sb_tile(qs, kb, after_c, tri_after, valid)
            return after_c + jnp.sum(sp, axis=1, keepdims=True), acc + _dot(a, vb, NN)

        carry = (jnp.zeros((tq, 1), F32), jnp.zeros((tq, dh), F32))
        for dd in range(r - 1, -1, -1):
            carry = step(i * r + dd, carry, True)
        carry = lax.fori_loop(0, i * r, lambda t, c: step(i * r - 1 - t, c, False), carry)
        o_ref[0] = carry[1]
        ob_ref[0] = carry[1].astype(ob_ref.dtype)

    blk = pl.BlockSpec((1, tq, dh), lambda h, i: (h, i, 0))
    full = pl.BlockSpec((1, s, dh), lambda h, i: (h, 0, 0))
    return pl.pallas_call(
        body,
        name=name,
        grid=(nh, s // tq),
        in_specs=[blk, full, full],
        out_specs=[blk, blk],
        out_shape=[jax.ShapeDtypeStruct((nh, s, dh), F32), jax.ShapeDtypeStruct((nh, s, dh), MXU_DTYPE)],
        compiler_params=_params(("parallel", "parallel")),
    )(q, k, v)


def _sb_bwd(q, k, v, o, do, name):
    nh, s, dh = q.shape
    tq = min(SB_TQ, s)
    tk = SB_TK
    r = tq // tk
    scale = dh ** -0.5

    def body(q_ref, k_ref, v_ref, o_ref, do_ref, dq_ref, dk_ref, dv_ref):
        i = pl.program_id(1)

        @pl.when(i == 0)
        def _():
            dk_ref[...] = jnp.zeros_like(dk_ref)
            dv_ref[...] = jnp.zeros_like(dv_ref)

        qs = q_ref[0] * scale
        dob = do_ref[0]
        total = jnp.sum(dob.astype(F32) * o_ref[0], axis=1, keepdims=True)
        qpos = i * tq + lax.broadcasted_iota(jnp.int32, (tq, tk), 0)
        kcol = lax.broadcasted_iota(jnp.int32, (tq, tk), 1)
        row = lax.broadcasted_iota(jnp.int32, (tk, tk), 0)
        col = lax.broadcasted_iota(jnp.int32, (tk, tk), 1)
        tri_after = (row > col).astype(MXU_DTYPE)
        tri_from = (row >= col).astype(MXU_DTYPE)

        def step(jj, carry, masked):
            after_c, from_c, dq = carry
            start = pl.multiple_of(jj * tk, tk)
            kb = k_ref[0, pl.ds(start, tk), :]
            vb = v_ref[0, pl.ds(start, tk), :]
            valid = (jj * tk + kcol < qpos) if masked else None
            sp, ls, a = _sb_tile(qs, kb, after_c, tri_after, valid)
            ab = a.astype(MXU_DTYPE)
            dl = ab.astype(F32) * _dot(dob, vb, NT)
            from_j = from_c + _split_dot(dl, tri_from)
            dz = dl - jnp.exp(ls) * (dl + total - from_j)
            if masked:
                dz = jnp.where(valid, dz, 0.0)
            dzb = dz.astype(MXU_DTYPE)
            dk_ref[0, pl.ds(start, tk), :] += _dot(dzb, qs, TN)
            dv_ref[0, pl.ds(start, tk), :] += _dot(ab, dob, TN)
            return (
                after_c + jnp.sum(sp, axis=1, keepdims=True),
                from_c + jnp.sum(dl, axis=1, keepdims=True),
                dq + _dot(dzb, kb, NN),
            )

        carry = (jnp.zeros((tq, 1), F32), jnp.zeros((tq, 1), F32), jnp.zeros((tq, dh), F32))
        for dd in range(r - 1, -1, -1):
            carry = step(i * r + dd, carry, True)
        carry = lax.fori_loop(0, i * r, lambda t, c: step(i * r - 1 - t, c, False), carry)
        dq_ref[0] = carry[2] * scale

    blk = pl.BlockSpec((1, tq, dh), lambda h, i: (h, i, 0))
    full = pl.BlockSpec((1, s, dh), lambda h, i: (h, 0, 0))
    shape = jax.ShapeDtypeStruct((nh, s, dh), F32)
    return pl.pallas_call(
        body,
        name=name,
        grid=(nh, s // tq),
        in_specs=[blk, full, full, blk, blk],
        out_specs=[blk, full, full],
        out_shape=[shape, shape, shape],
        compiler_params=_params(("parallel", "arbitrary")),
    )(q, k, v, o, do)


BAND_TQ = 512


def _band_scores(q_ref, k_ref, i, sub, tq, length, max_dist, scale):
    t0 = i * tq + sub * ATT_BLK
    ks = pl.multiple_of(jnp.minimum(jnp.maximum(t0 - ATT_BLK, 0), length - 2 * ATT_BLK), ATT_BLK)
    qs = q_ref[0, sub * ATT_BLK:(sub + 1) * ATT_BLK, :] * scale
    kw = k_ref[0, pl.ds(ks, 2 * ATT_BLK), :]
    sc = _dot(qs, kw, NT)
    diff = (t0 + lax.broadcasted_iota(jnp.int32, sc.shape, 0)) - (ks + lax.broadcasted_iota(jnp.int32, sc.shape, 1))
    valid = (diff >= 0) & (diff <= max_dist)
    return ks, qs, kw, jnp.where(valid, sc, NEG_INF)


def _band_fwd(q, k, v, sinks, max_dist, name):
    bq, length, dh = q.shape
    group = bq // k.shape[0]
    tq = min(BAND_TQ, length)
    scale = dh ** -0.5
    n_sink = 0 if sinks is None else sinks.shape[0]

    def body(*refs):
        if n_sink:
            sink_ref, q_ref, k_ref, v_ref, o_ref, lse_ref = refs
            sink = sink_ref[pl.program_id(0) % n_sink]
        else:
            q_ref, k_ref, v_ref, o_ref, lse_ref = refs
        i = pl.program_id(1)
        for sub in range(tq // ATT_BLK):
            ks, _, _, sc = _band_scores(q_ref, k_ref, i, sub, tq, length, max_dist, scale)
            m = jnp.max(sc, axis=1, keepdims=True)
            if n_sink:
                m = jnp.maximum(m, sink)
            e = jnp.exp(sc - m)
            den = jnp.sum(e, axis=1, keepdims=True)
            if n_sink:
                den = den + jnp.exp(sink - m)
            rows = slice(sub * ATT_BLK, (sub + 1) * ATT_BLK)
            o_ref[0, rows, :] = _dot(e / den, v_ref[0, pl.ds(ks, 2 * ATT_BLK), :], NN)
            lse_ref[0, rows, :] = m + jnp.log(den)

    qblk = pl.BlockSpec((1, tq, dh), lambda b, i: (b, i, 0))
    kfull = pl.BlockSpec((1, length, dh), lambda b, i: (b // group, 0, 0))
    in_specs = [qblk, kfull, kfull]
    args = [q, k, v]
    if n_sink:
        in_specs = [pl.BlockSpec(memory_space=pltpu.SMEM)] + in_specs
        args = [sinks] + args
    return pl.pallas_call(
        body,
        name=name,
        grid=(bq, length // tq),
        in_specs=in_specs,
        out_specs=[qblk, pl.BlockSpec((1, tq, 1), lambda b, i: (b, i, 0))],
        out_shape=[jax.ShapeDtypeStruct((bq, length, dh), F32), jax.ShapeDtypeStruct((bq, length, 1), F32)],
        compiler_params=_params(("parallel", "parallel")),
    )(*args)


def _band_bwd(q, k, v, do, lse, delta, sinks, max_dist, name):
    bq, length, dh = q.shape
    group = bq // k.shape[0]
    tq = min(BAND_TQ, length)
    scale = dh ** -0.5
    n_sink = 0 if sinks is None else sinks.shape[0]

    def body(*refs):
        if n_sink:
            sink_ref, q_ref, k_ref, v_ref, do_ref, lse_ref, dl_ref, dq_ref, dk_ref, dv_ref, ds_ref = refs
            sink = sink_ref[pl.program_id(0) % n_sink]
        else:
            q_ref, k_ref, v_ref, do_ref, lse_ref, dl_ref, dq_ref, dk_ref, dv_ref, ds_ref = refs
        i = pl.program_id(1)

        @pl.when((i == 0) & (pl.program_id(0) % group == 0))
        def _():
            dk_ref[...] = jnp.zeros_like(dk_ref)
            dv_ref[...] = jnp.zeros_like(dv_ref)

        @pl.when(i == 0)
        def _():
            ds_ref[...] = jnp.zeros_like(ds_ref)

        for sub in range(tq // ATT_BLK):
            ks, qs, kw, sc = _band_scores(q_ref, k_ref, i, sub, tq, length, max_dist, scale)
            rows = slice(sub * ATT_BLK, (sub + 1) * ATT_BLK)
            lse = lse_ref[0, rows, :]
            delta_r = dl_ref[0, rows, :]
            dob = do_ref[0, rows, :]
            p = jnp.exp(sc - lse)
            dp = _dot(dob, v_ref[0, pl.ds(ks, 2 * ATT_BLK), :], NT)
            dsb = (p * (dp - delta_r)).astype(MXU_DTYPE)
            dq_ref[0, rows, :] = _dot(dsb, kw, NN) * scale
            dk_ref[0, pl.ds(ks, 2 * ATT_BLK), :] += _dot(dsb, qs, TN)
            dv_ref[0, pl.ds(ks, 2 * ATT_BLK), :] += _dot(p, dob, TN)
            if n_sink:
                ds_ref[...] += jnp.sum(-jnp.exp(sink - lse) * delta_r, axis=0, keepdims=True)

    qblk = pl.BlockSpec((1, tq, dh), lambda b, i: (b, i, 0))
    qcol = pl.BlockSpec((1, tq, 1), lambda b, i: (b, i, 0))
    kfull = pl.BlockSpec((1, length, dh), lambda b, i: (b // group, 0, 0))
    in_specs = [qblk, kfull, kfull, qblk, qcol, qcol]
    args = [q, k, v, do, lse, delta]
    if n_sink:
        in_specs = [pl.BlockSpec(memory_space=pltpu.SMEM)] + in_specs
        args = [sinks] + args
    kshape = jax.ShapeDtypeStruct((k.shape[0], length, dh), F32)
    return pl.pallas_call(
        body,
        name=name,
        grid=(bq, length // tq),
        in_specs=in_specs,
        out_specs=[qblk, kfull, kfull, pl.BlockSpec((1, 8, 128), lambda b, i: (b, 0, 0))],
        out_shape=[jax.ShapeDtypeStruct((bq, length, dh), F32), kshape, kshape, jax.ShapeDtypeStruct((bq, 8, 128), F32)],
        compiler_params=_params(("arbitrary", "arbitrary")),
    )(*args)


def _merge_weights(lse_refs):
    lses = [r[0] for r in lse_refs]
    m = functools.reduce(jnp.maximum, lses)
    es = [jnp.exp(l - m) for l in lses]
    den = functools.reduce(lambda a, b: a + b, es)
    return [e / den for e in es]


def _merge_fwd(outs, lses, name):
    n = len(outs)
    nh, s, dh = outs[0].shape
    ts = _tile(s, 1024, 8)

    def body(*refs):
        ws = _merge_weights(refs[n:2 * n])
        o = functools.reduce(lambda a, b: a + b, [w * r[0] for w, r in zip(ws, refs[:n])])
        refs[2 * n][0] = o
        refs[2 * n + 1][0] = o.astype(MXU_DTYPE)

    blk = pl.BlockSpec((1, ts, dh), lambda h, i: (h, i, 0))
    col = pl.BlockSpec((1, ts, 1), lambda h, i: (h, i, 0))
    return pl.pallas_call(
        body,
        name=name,
        grid=(nh, s // ts),
        in_specs=[blk] * n + [col] * n,
        out_specs=[blk, blk],
        out_shape=[jax.ShapeDtypeStruct((nh, s, dh), F32), jax.ShapeDtypeStruct((nh, s, dh), MXU_DTYPE)],
        compiler_params=_params(("parallel", "parallel")),
    )(*outs, *lses)


def _merge_bwd(do, o, lses, name):
    n = len(lses)
    nh, s, dh = o.shape
    ts = _tile(s, 1024, 8)

    def body(*refs):
        do_ref, o_ref = refs[:2]
        ws = _merge_weights(refs[2:2 + n])
        dof = do_ref[0].astype(F32)
        base = jnp.sum(dof * o_ref[0], axis=1, keepdims=True)
        for g in range(n):
            refs[2 + n + g][0] = (ws[g] * dof).astype(MXU_DTYPE)
            refs[2 + 2 * n + g][0] = ws[g] * base

    blk = pl.BlockSpec((1, ts, dh), lambda h, i: (h, i, 0))
    col = pl.BlockSpec((1, ts, 1), lambda h, i: (h, i, 0))
    res = pl.pallas_call(
        body,
        name=name,
        grid=(nh, s // ts),
        in_specs=[blk, blk] + [col] * n,
        out_specs=[blk] * n + [col] * n,
        out_shape=[jax.ShapeDtypeStruct((nh, s, dh), MXU_DTYPE)] * n + [jax.ShapeDtypeStruct((nh, s, 1), F32)] * n,
        compiler_params=_params(("parallel", "parallel")),
    )(do, o, *lses)
    return res[:n], res[n:]


def _to_heads(a, dil=1):
    s, n = a.shape
    nh = n // HEAD_DIM
    a = a.reshape(s // dil, dil, nh, HEAD_DIM).transpose(1, 2, 0, 3)
    return a.reshape(dil * nh, s // dil, HEAD_DIM)


def _from_heads(a, dil=1):
    b, ls, c = a.shape
    nh = b // dil
    return a.reshape(dil, nh, ls, c).transpose(2, 0, 1, 3).reshape(ls * dil, nh * c)


def _restride(a, dil):
    nh, s, c = a.shape
    return a.reshape(nh, s // dil, dil, c).transpose(2, 0, 1, 3).reshape(dil * nh, s // dil, c)


def _unstride(a, dil):
    b, ls, c = a.shape
    nh = b // dil
    return a.reshape(dil, nh, ls, c).transpose(1, 2, 0, 3).reshape(nh, ls * dil, c)


ANY = pl.BlockSpec(memory_space=pl.ANY)


def _gather_chips(flat, name):
    def body(f_ref, g_ref, send_sems, recv_sems, local_sem):
        x, y, c = lax.axis_index("x"), lax.axis_index("y"), lax.axis_index("c")
        me, sibling = (x, y, c), (x, y, 1 - c)
        chips = [(1 - x, y), (x, 1 - y), (1 - x, 1 - y)]

        def slot(px, py, half):
            return g_ref.at[2 * px + py, half]

        def copy(k, src, dst, to):
            return pltpu.make_async_remote_copy(src_ref=src, dst_ref=dst, send_sem=send_sems.at[k], recv_sem=recv_sems.at[k], device_id=to, device_id_type=MESH)

        mine = pltpu.make_async_copy(f_ref, g_ref.at[2 * x + y], local_sem)
        mine.start()
        first = [copy(k, f_ref.at[c], slot(x, y, c), (*chip, c)) for k, chip in enumerate(chips)]
        for cp in first:
            cp.start()
        passed = [copy(3 + k, slot(*chip, c), slot(*chip, c), sibling) for k, chip in enumerate(chips)]
        for k, chip in enumerate(chips):
            copy(k, slot(*chip, c), slot(*chip, c), me).wait_recv()
            passed[k].start()
        for k, chip in enumerate(chips):
            copy(3 + k, slot(*chip, 1 - c), slot(*chip, 1 - c), me).wait_recv()
        for cp in first + passed:
            cp.wait_send()
        mine.wait()

    return pl.pallas_call(
        body,
        name=name,
        in_specs=[ANY],
        out_specs=ANY,
        out_shape=jax.ShapeDtypeStruct((N_CHIPS,) + flat.shape, flat.dtype),
        scratch_shapes=[pltpu.SemaphoreType.DMA((6,)), pltpu.SemaphoreType.DMA((6,)), pltpu.SemaphoreType.DMA],
    )(flat)


def _swap_sibling(a, name):
    def body(a_ref, o_ref, send_sem, recv_sem):
        x, y, c = lax.axis_index("x"), lax.axis_index("y"), lax.axis_index("c")
        cp = pltpu.make_async_remote_copy(src_ref=a_ref, dst_ref=o_ref, send_sem=send_sem, recv_sem=recv_sem, device_id=(x, y, 1 - c), device_id_type=MESH)
        cp.start()
        cp.wait()

    return pl.pallas_call(
        body,
        name=name,
        in_specs=[ANY],
        out_specs=ANY,
        out_shape=jax.ShapeDtypeStruct(a.shape, a.dtype),
        scratch_shapes=[pltpu.SemaphoreType.DMA, pltpu.SemaphoreType.DMA],
    )(a)


def _exchange_chips(part, name):
    def body(p_ref, q_ref, send_sems, recv_sems, local_sem):
        x, y, c = lax.axis_index("x"), lax.axis_index("y"), lax.axis_index("c")
        my_slot = 2 * x + y
        chips = [(1 - x, y), (x, 1 - y), (1 - x, 1 - y)]
        mine = pltpu.make_async_copy(p_ref.at[my_slot], q_ref.at[my_slot], local_sem)
        mine.start()
        sends = [
            pltpu.make_async_remote_copy(src_ref=p_ref.at[2 * px + py], dst_ref=q_ref.at[my_slot], send_sem=send_sems.at[k], recv_sem=recv_sems.at[k], device_id=(px, py, c), device_id_type=MESH)
            for k, (px, py) in enumerate(chips)
        ]
        for cp in sends:
            cp.start()
        for k, (px, py) in enumerate(chips):
            landed = q_ref.at[2 * px + py]
            pltpu.make_async_remote_copy(src_ref=landed, dst_ref=landed, send_sem=send_sems.at[k], recv_sem=recv_sems.at[k], device_id=(x, y, c), device_id_type=MESH).wait_recv()
        for cp in sends:
            cp.wait_send()
        mine.wait()

    return pl.pallas_call(
        body,
        name=name,
        in_specs=[ANY],
        out_specs=ANY,
        out_shape=jax.ShapeDtypeStruct(part.shape, part.dtype),
        scratch_shapes=[pltpu.SemaphoreType.DMA((3,)), pltpu.SemaphoreType.DMA((3,)), pltpu.SemaphoreType.DMA],
    )(part)


def _add2(a, b, name):
    n, r, c = a.shape
    tr = _tile(r, 512, 8)

    def body(a_ref, b_ref, o_ref):
        o_ref[...] = a_ref[...] + b_ref[...]

    blk = pl.BlockSpec((1, tr, c), lambda s, i: (s, i, 0))
    return pl.pallas_call(
        body,
        name=name,
        grid=(n, r // tr),
        in_specs=[blk, blk],
        out_specs=blk,
        out_shape=jax.ShapeDtypeStruct(a.shape, a.dtype),
        compiler_params=_params(("parallel", "parallel")),
    )(a, b)


def _sum_slots(q, name):
    n, r, c = q.shape
    tr = _tile(r, 512, 8)

    def body(q_ref, o_ref):
        acc = q_ref[0]
        for s in range(1, n):
            acc = acc + q_ref[s]
        o_ref[...] = acc

    return pl.pallas_call(
        body,
        name=name,
        grid=(r // tr,),
        in_specs=[pl.BlockSpec((n, tr, c), lambda i: (0, i, 0))],
        out_specs=pl.BlockSpec((tr, c), lambda i: (i, 0)),
        out_shape=jax.ShapeDtypeStruct((r, c), q.dtype),
        compiler_params=_params(("parallel",)),
    )(q)


def _adamw(w, g, m, v, name):
    r, c = w.shape
    tr = _tile(r, 256, 8)
    c1 = 1.0 - ADAM_B1 ** ADAM_STEP
    c2 = 1.0 - ADAM_B2 ** ADAM_STEP

    def body(w_ref, g_ref, m_ref, v_ref, d_ref, nm_ref, nv_ref):
        g = g_ref[...]
        nm = ADAM_B1 * m_ref[...] + (1.0 - ADAM_B1) * g
        nv = ADAM_B2 * v_ref[...] + (1.0 - ADAM_B2) * (g * g)
        d_ref[...] = -ADAM_LR * ((nm / c1) / (jnp.sqrt(nv / c2) + ADAM_EPS) + ADAM_WD * w_ref[...])
        nm_ref[...] = nm
        nv_ref[...] = nv

    blk = pl.BlockSpec((tr, c), lambda i: (i, 0))
    shape = jax.ShapeDtypeStruct((r, c), F32)
    return pl.pallas_call(
        body,
        name=name,
        grid=(r // tr,),
        in_specs=[blk] * 4,
        out_specs=[blk] * 3,
        out_shape=[shape] * 3,
        compiler_params=_params(("parallel",)),
    )(w, g, m, v)


WEIGHTS = ("ffn1_w_in", "ffn1_w_out", "ffn2_w_in", "ffn2_w_out", "ln_g", "ln_b", "sb_w_in", "sb_w_out", "swa_w_in", "swa_sinks", "swa_w_out", "dil_w_in", "dil_w_out", "ple_w_proj", "ple_w_gate")
SHARD_AXIS = {"ffn1_w_in": 2, "ffn1_w_out": 1, "ffn2_w_in": 2, "ffn2_w_out": 1, "ln_g": 2, "ln_b": 2, "sb_w_in": 2, "sb_w_out": 1, "swa_w_in": 2, "swa_sinks": None, "swa_w_out": 1, "dil_w_in": 2, "dil_w_out": 1, "ple_w_proj": 2, "ple_w_gate": 1}
FLAT_UNIT = 2 * FLAT_COLS
FLAT_ROWS = 512
FLAT_BLOCK = FLAT_ROWS * FLAT_COLS


def _padded(n):
    return -(-n // FLAT_UNIT) * FLAT_UNIT


def _pack(pieces, lead):
    cols = []
    for a in pieces:
        a = a.reshape(lead + (-1,))
        n = a.shape[-1]
        a = jnp.pad(a, [(0, 0)] * len(lead) + [(0, _padded(n) - n)])
        cols.append(a.reshape(lead + (2, -1)))
    total = sum(a.shape[-1] for a in cols)
    if total % FLAT_BLOCK:
        cols.append(jnp.zeros(lead + (2, FLAT_BLOCK - total % FLAT_BLOCK), cols[0].dtype))
    return jnp.concatenate(cols, axis=-1).reshape(lead + (2, -1, FLAT_COLS))


def _unpack(flat, sizes, lead):
    flat = flat.reshape(lead + (2, -1))
    out, off = [], 0
    for n in sizes:
        half = _padded(n) // 2
        out.append(flat[..., off:off + half].reshape(lead + (-1,))[..., :n])
        off += half
    return out


def _full_from_shards(g, axis):
    g = jnp.moveaxis(g, 0, axis)
    return g.reshape(g.shape[:axis] + (g.shape[axis] * g.shape[axis + 1],) + g.shape[axis + 2:])


def _shards_from_full(a, axis):
    a = a.reshape(a.shape[:axis] + (N_CHIPS, a.shape[axis] // N_CHIPS) + a.shape[axis + 1:])
    return jnp.moveaxis(a, axis, 0)


def _ffn_forward(x, xb, w_in, w_out, g, b, alpha, tag):
    gate, up, act = _ffn_in(xb, w_in, f"{tag}_in")
    y = _mm(act, w_out, "nn", F32, f"{tag}_out")
    out, outb, xhat, rstd = _ln_fwd(x, y, g, b, alpha, 0.5, f"{tag}_ln")
    return (out, outb), dict(xb=xb, gate=gate, up=up, act=act, xhat=xhat, rstd=rstd)


def _ffn_backward(dout, saved, w_in, w_out, g, tag):
    dr, dyb, dg, db = _ln_bwd(*dout, saved["xhat"], saved["rstd"], g, 0.5, f"{tag}_ln_bwd")
    dh = _ffn_dact(dyb, w_out, saved["gate"], saved["up"], f"{tag}_dact")
    dw_out = _mm(saved["act"], dyb, "tn", F32, f"{tag}_dw_out")
    dw_in = _mm(saved["xb"], dh, "tn", F32, f"{tag}_dw_in", split_b=True)
    dxb = _mm(dh, w_in, "nt", F32, f"{tag}_dx", split_a=True)
    return dr, dxb, dw_in, dw_out, dg, db


def _sb_forward(xb, w_in, w_out, tag):
    nw = w_out.shape[0]
    h = _mm(xb, w_in, "nn", MXU_DTYPE, f"{tag}_qkv")
    q, k, v = (_to_heads(h[:, j * nw:(j + 1) * nw]) for j in range(3))
    o, ob = _sb_fwd(q, k, v, f"{tag}_att")
    ob = _from_heads(ob)
    return _mm(ob, w_out, "nn", F32, f"{tag}_proj"), dict(q=q, k=k, v=v, o=o, ob=ob)


def _sb_backward(dmix, saved, xb, w_in, w_out, tag):
    dw_out = _mm(saved["ob"], dmix, "tn", F32, f"{tag}_dw_out")
    do = _to_heads(_mm(dmix, w_out, "nt", MXU_DTYPE, f"{tag}_do"))
    dq, dk, dv = _sb_bwd(saved["q"], saved["k"], saved["v"], saved["o"], do, f"{tag}_att_bwd")
    dh = jnp.concatenate([_from_heads(t) for t in (dq, dk, dv)], axis=1).astype(MXU_DTYPE)
    dw_in = _mm(xb, dh, "tn", F32, f"{tag}_dw_in")
    return _mm(dh, w_in, "nt", F32, f"{tag}_dx"), dw_in, dw_out


def _swa_forward(xb, w_in, sinks, w_out, tables, tag):
    nq = w_out.shape[0]
    nkv = (w_in.shape[1] - nq) // 2
    h = _mm(xb, w_in, "nn", F32, f"{tag}_qkv")
    n_rot = (nq + nkv) // 128
    hb = _rope(h, *tables, lambda j: j // n_rot, MXU_DTYPE, f"{tag}_rope")
    q, k, v = _to_heads(hb[:, :nq]), _to_heads(hb[:, nq:nq + nkv]), _to_heads(hb[:, nq + nkv:])
    o, lse = _band_fwd(q, k, v, sinks, SWA_WINDOW - 1, f"{tag}_att")
    o, ob = _merge_fwd([o], [lse], f"{tag}_cast")
    ob = _from_heads(ob)
    return _mm(ob, w_out, "nn", F32, f"{tag}_proj"), dict(q=q, k=k, v=v, o=o, lse=lse, ob=ob, n_rot=n_rot)


def _swa_backward(dmix, saved, xb, w_in, sinks, w_out, tables, tag):
    dw_out = _mm(saved["ob"], dmix, "tn", F32, f"{tag}_dw_out")
    do = _to_heads(_mm(dmix, w_out, "nt", MXU_DTYPE, f"{tag}_do"))
    (dog,), (delta,) = _merge_bwd(do, saved["o"], [saved["lse"]], f"{tag}_delta")
    dq, dk, dv, dsink = _band_bwd(saved["q"], saved["k"], saved["v"], dog, saved["lse"], delta, sinks, SWA_WINDOW - 1, f"{tag}_att_bwd")
    dh = jnp.concatenate([_from_heads(t) for t in (dq, dk, dv)], axis=1)
    n_rot = saved["n_rot"]
    dhb = _rope(dh, tables[0], -tables[1], lambda j: j // n_rot, MXU_DTYPE, f"{tag}_rope_bwd")
    dw_in = _mm(xb, dhb, "tn", F32, f"{tag}_dw_in")
    return _mm(dhb, w_in, "nt", F32, f"{tag}_dx"), dw_in, dw_out, dsink[:, 0, 0]


def _dil_forward(xb, w_in, w_out, tables, tag):
    nw = w_out.shape[0]
    blocks = nw // 128
    h = _mm(xb, w_in, "nn", F32, f"{tag}_qkv")
    hb = _rope(h, *tables, lambda j: (j % (3 * blocks)) // (2 * blocks), MXU_DTYPE, f"{tag}_rope")
    qkv, outs, lses = [], [], []
    for gi, (win, dil) in enumerate(DIL_GROUPS):
        base = gi * 3 * nw
        q, k, v = (_to_heads(hb[:, base + j * nw:base + (j + 1) * nw], dil) for j in range(3))
        o, lse = _band_fwd(q, k, v, None, win // dil, f"{tag}_att{gi}")
        qkv.append((q, k, v))
        outs.append(_unstride(o, dil))
        lses.append(_unstride(lse, dil))
    o, ob = _merge_fwd(outs, lses, f"{tag}_merge")
    ob = _from_heads(ob)
    return _mm(ob, w_out, "nn", F32, f"{tag}_proj"), dict(qkv=qkv, o=o, lses=lses, ob=ob, blocks=blocks)


def _dil_backward(dmix, saved, xb, w_in, w_out, tables, tag):
    dw_out = _mm(saved["ob"], dmix, "tn", F32, f"{tag}_dw_out")
    do = _to_heads(_mm(dmix, w_out, "nt", MXU_DTYPE, f"{tag}_do"))
    dogs, deltas = _merge_bwd(do, saved["o"], saved["lses"], f"{tag}_merge_bwd")
    parts = []
    for gi, (win, dil) in enumerate(DIL_GROUPS):
        q, k, v = saved["qkv"][gi]
        dq, dk, dv, _ = _band_bwd(q, k, v, _restride(dogs[gi], dil), _restride(saved["lses"][gi], dil), _restride(deltas[gi], dil), None, win // dil, f"{tag}_att{gi}_bwd")
        parts += [_from_heads(t, dil) for t in (dq, dk, dv)]
    dh = jnp.concatenate(parts, axis=1)
    blocks = saved["blocks"]
    dhb = _rope(dh, tables[0], -tables[1], lambda j: (j % (3 * blocks)) // (2 * blocks), MXU_DTYPE, f"{tag}_rope_bwd")
    dw_in = _mm(xb, dhb, "tn", F32, f"{tag}_dw_in")
    return _mm(dhb, w_in, "nt", F32, f"{tag}_dx"), dw_in, dw_out


def kernel(x, p, ffn1_w_in, ffn1_w_out, ffn2_w_in, ffn2_w_out, ln_g, ln_b, sb_w_in, sb_w_out, swa_w_in, swa_sinks, swa_w_out, dil_w_in, dil_w_out, ple_w_proj, ple_w_gate, loss_target, m_ffn1_w_in, m_ffn1_w_out, m_ffn2_w_in, m_ffn2_w_out, m_ln_g, m_ln_b, m_sb_w_in, m_sb_w_out, m_swa_w_in, m_swa_sinks, m_swa_w_out, m_dil_w_in, m_dil_w_out, m_ple_w_proj, m_ple_w_gate, v_ffn1_w_in, v_ffn1_w_out, v_ffn2_w_in, v_ffn2_w_out, v_ln_g, v_ln_b, v_sb_w_in, v_sb_w_out, v_swa_w_in, v_swa_sinks, v_swa_w_out, v_dil_w_in, v_dil_w_out, v_ple_w_proj, v_ple_w_gate):
    shard = dict(ffn1_w_in=ffn1_w_in, ffn1_w_out=ffn1_w_out, ffn2_w_in=ffn2_w_in, ffn2_w_out=ffn2_w_out, ln_g=ln_g, ln_b=ln_b, sb_w_in=sb_w_in, sb_w_out=sb_w_out, swa_w_in=swa_w_in, swa_sinks=swa_sinks, swa_w_out=swa_w_out, dil_w_in=dil_w_in, dil_w_out=dil_w_out, ple_w_proj=ple_w_proj, ple_w_gate=ple_w_gate)
    mom_m = dict(ffn1_w_in=m_ffn1_w_in, ffn1_w_out=m_ffn1_w_out, ffn2_w_in=m_ffn2_w_in, ffn2_w_out=m_ffn2_w_out, ln_g=m_ln_g, ln_b=m_ln_b, sb_w_in=m_sb_w_in, sb_w_out=m_sb_w_out, swa_w_in=m_swa_w_in, swa_sinks=m_swa_sinks, swa_w_out=m_swa_w_out, dil_w_in=m_dil_w_in, dil_w_out=m_dil_w_out, ple_w_proj=m_ple_w_proj, ple_w_gate=m_ple_w_gate)
    mom_v = dict(ffn1_w_in=v_ffn1_w_in, ffn1_w_out=v_ffn1_w_out, ffn2_w_in=v_ffn2_w_in, ffn2_w_out=v_ffn2_w_out, ln_g=v_ln_g, ln_b=v_ln_b, sb_w_in=v_sb_w_in, sb_w_out=v_sb_w_out, swa_w_in=v_swa_w_in, swa_sinks=v_swa_sinks, swa_w_out=v_swa_w_out, dil_w_in=v_dil_w_in, dil_w_out=v_dil_w_out, ple_w_proj=v_ple_w_proj, ple_w_gate=v_ple_w_gate)
    depth = ffn1_w_in.shape[0]
    alpha = (2 * depth) ** 0.25
    c = lax.axis_index("c")

    mat_names = [n for n in WEIGHTS if SHARD_AXIS[n] is not None and not n.startswith("ln_")]
    ln_names = ["ln_g", "ln_b"]
    full = {"swa_sinks": swa_sinks}
    for names, dtype, tag in ((mat_names, MXU_DTYPE, "gather_weights"), (ln_names, F32, "gather_ln")):
        flat = _pack([shard[n].astype(dtype) for n in names], ())
        got = _gather_chips(flat, tag)
        for n, g in zip(names, _unpack(got, [shard[n].size for n in names], (N_CHIPS,))):
            full[n] = _full_from_shards(g.reshape((N_CHIPS,) + shard[n].shape), SHARD_AXIS[n])

    seq = x.shape[1]
    tables = _rope_tables(seq)
    xf = x[0]
    xb = xf.astype(MXU_DTYPE)
    saved = []
    for i in range(depth):
        kind, j = i % 3, i // 3
        sv = {}
        (x1, x1b), sv["ffn1"] = _ffn_forward(xf, xb, full["ffn1_w_in"][i], full["ffn1_w_out"][i], full["ln_g"][i, 0], full["ln_b"][i, 0], alpha, f"l{i}_ffn1")
        if kind == 0:
            mix, sv["mix"] = _sb_forward(x1b, full["sb_w_in"][j], full["sb_w_out"][j], f"l{i}_sb")
        elif kind == 1:
            mix, sv["mix"] = _swa_forward(x1b, full["swa_w_in"][j], swa_sinks[j], full["swa_w_out"][j], tables, f"l{i}_swa")
        else:
            mix, sv["mix"] = _dil_forward(x1b, full["dil_w_in"][j], full["dil_w_out"][j], tables, f"l{i}_dil")
        x2, x2b, sv["xhat2"], sv["rstd2"] = _ln_fwd(x1, mix, full["ln_g"][i, 1], full["ln_b"][i, 1], alpha, 1.0, f"l{i}_mix_ln")
        sv["x1b"] = x1b
        (x3, x3b), sv["ffn2"] = _ffn_forward(x2, x2b, full["ffn2_w_in"][i], full["ffn2_w_out"][i], full["ln_g"][i, 2], full["ln_b"][i, 2], alpha, f"l{i}_ffn2")
        xf, xb, sv["u"], sv["e"] = _ple_fwd(x3, x3b, p[i, 0], full["ple_w_gate"][i], full["ple_w_proj"][i], f"l{i}_ple")
        sv["x3b"] = x3b
        saved.append(sv)

    loss_part, zeros, dy = _loss(xf, loss_target[0], "loss")
    loss = lax.psum(loss_part[0, 0], ("x", "y", "c"))

    grads = {n: [None] * full[n].shape[0] for n in WEIGHTS if n not in ("ln_g", "ln_b")}
    dln_g = [[None] * 3 for _ in range(depth)]
    dln_b = [[None] * 3 for _ in range(depth)]
    dout = (dy, zeros, 1.0)
    for i in reversed(range(depth)):
        kind, j = i % 3, i // 3
        sv = saved[i]
        dx4, dub, deb = _ple_bwd(*dout, sv["u"], sv["e"], f"l{i}_ple_bwd")
        grads["ple_w_gate"][i] = _mm(sv["x3b"], dub, "tn", F32, f"l{i}_ple_dw_gate")
        grads["ple_w_proj"][i] = _mm(p[i, 0], deb, "tn", F32, f"l{i}_ple_dw_proj")
        dxb = _mm(dub, full["ple_w_gate"][i], "nt", F32, f"l{i}_ple_dx")
        dr, dxb, grads["ffn2_w_in"][i], grads["ffn2_w_out"][i], dln_g[i][2], dln_b[i][2] = _ffn_backward((dx4, dxb, 1.0), sv["ffn2"], full["ffn2_w_in"][i], full["ffn2_w_out"][i], full["ln_g"][i, 2], f"l{i}_ffn2")
        dr, dmix, dln_g[i][1], dln_b[i][1] = _ln_bwd(dr, dxb, alpha, sv["xhat2"], sv["rstd2"], full["ln_g"][i, 1], 1.0, f"l{i}_mix_ln_bwd")
        if kind == 0:
            dxb, grads["sb_w_in"][j], grads["sb_w_out"][j] = _sb_backward(dmix, sv["mix"], sv["x1b"], full["sb_w_in"][j], full["sb_w_out"][j], f"l{i}_sb")
        elif kind == 1:
            dxb, grads["swa_w_in"][j], grads["swa_w_out"][j], grads["swa_sinks"][j] = _swa_backward(dmix, sv["mix"], sv["x1b"], full["swa_w_in"][j], swa_sinks[j], full["swa_w_out"][j], tables, f"l{i}_swa")
        else:
            dxb, grads["dil_w_in"][j], grads["dil_w_out"][j] = _dil_backward(dmix, sv["mix"], sv["x1b"], full["dil_w_in"][j], full["dil_w_out"][j], tables, f"l{i}_dil")
        dr, dxb, grads["ffn1_w_in"][i], grads["ffn1_w_out"][i], dln_g[i][0], dln_b[i][0] = _ffn_backward((dr, dxb, alpha), sv["ffn1"], full["ffn1_w_in"][i], full["ffn1_w_out"][i], full["ln_g"][i, 0], f"l{i}_ffn1")
        dout = (dr, dxb, alpha)
    grad_x = _axpy(*dout, "grad_x")[None]

    gfull = {n: jnp.stack(g) for n, g in grads.items()}
    gfull["ln_g"] = jnp.stack([jnp.concatenate(r, axis=0) for r in dln_g])
    gfull["ln_b"] = jnp.stack([jnp.concatenate(r, axis=0) for r in dln_b])
    pieces = []
    for n in WEIGHTS:
        if SHARD_AXIS[n] is None:
            pieces.append(jnp.broadcast_to(gfull[n][None], (N_CHIPS,) + gfull[n].shape))
        else:
            pieces.append(_shards_from_full(gfull[n], SHARD_AXIS[n]))
    packed = _pack(pieces, (N_CHIPS,))
    keep = lax.dynamic_index_in_dim(packed, c, 1, keepdims=False)
    give = lax.dynamic_index_in_dim(packed, 1 - c, 1, keepdims=False)
    chip_part = _add2(keep, _swap_sibling(give, "reduce_pair"), "reduce_pair_sum")
    half = _sum_slots(_exchange_chips(chip_part, "reduce_chips"), "reduce_chips_sum")
    other = _swap_sibling(half, "reduce_share")
    both = jnp.where(c == 0, jnp.stack([half, other]), jnp.stack([other, half]))
    gshard = {n: g.reshape(shard[n].shape) for n, g in zip(WEIGHTS, _unpack(both, [shard[n].size for n in WEIGHTS], ()))}

    delta, new_m, new_v = {}, {}, {}
    for n in WEIGHTS:
        shp = shard[n].shape
        two_d = (-1, shp[-1])
        d, nm, nv = _adamw(shard[n].reshape(two_d), gshard[n].reshape(two_d), mom_m[n].reshape(two_d), mom_v[n].reshape(two_d), f"adamw_{n}")
        delta[n], new_m[n], new_v[n] = d.reshape(shp), nm.reshape(shp), nv.reshape(shp)

    return (loss, grad_x, *[gshard[n] for n in WEIGHTS], *[delta[n] for n in WEIGHTS], *[new_m[n] for n in WEIGHTS], *[new_v[n] for n in WEIGHTS])
```

```python
import functools
import math

import jax
import jax.numpy as jnp
from jax import lax
from jax.experimental import pallas as pl
from jax.experimental.pallas import tpu as pltpu

F32 = jnp.float32
MXU_DTYPE = jnp.bfloat16
MESH = pl.DeviceIdType.MESH

HEAD_DIM = 64
ATT_BLK = 128
SWA_WINDOW = 128
DIL_GROUPS = ((128, 1), (512, 4), (2048, 16))
LN_EPS = 1e-5
ROPE_THETA = 10000.0
NEG_INF = -1e30
ADAM_LR, ADAM_B1, ADAM_B2, ADAM_EPS, ADAM_WD, ADAM_STEP = 0.001, 0.9, 0.999, 1e-08, 0.01, 10

VMEM_LIMIT_BYTES = 56 * 1024 * 1024
N_CHIPS = 4


def _params(sem=None):
    return pltpu.CompilerParams(dimension_semantics=sem, vmem_limit_bytes=VMEM_LIMIT_BYTES)


def _tile(n, target, unit):
    t = (min(target, n) // unit) * unit
    while t >= unit:
        if n % t == 0:
            return t
        t -= unit
    return n


def _dot(a, b, dims):
    return lax.dot_general(a.astype(MXU_DTYPE), b.astype(MXU_DTYPE), (dims, ((), ())), preferred_element_type=F32)


NN = ((1,), (0,))
NT = ((1,), (1,))
TN = ((0,), (0,))


def _mm(a, b, mode, out_dtype, name, split_a=False, split_b=False, tm=1024, tn=1408, tk=1408):
    dims = {"nn": NN, "nt": NT, "tn": TN}[mode]
    if split_a:
        m, k = a.shape[1], 2 * a.shape[2]
    elif mode == "tn":
        k, m = a.shape
    else:
        m, k = a.shape
    if split_b:
        n = 2 * b.shape[2]
    elif mode == "nt":
        n = b.shape[0]
    else:
        n = b.shape[1]
    tm = _tile(m, tm, 128)
    tn = _tile(n // 2 if split_b else n, tn, 128)
    tk = _tile(k // 2 if split_a else k, tk, 128)
    nk = k // tk
    nk_half = nk // 2
    nn_half = (n // tn) // 2

    if split_a:
        a_spec = pl.BlockSpec((None, tm, tk), lambda i, j, kk: (kk // nk_half, i, kk % nk_half))
    elif mode == "tn":
        a_spec = pl.BlockSpec((tk, tm), lambda i, j, kk: (kk, i))
    else:
        a_spec = pl.BlockSpec((tm, tk), lambda i, j, kk: (i, kk))
    if split_b:
        b_spec = pl.BlockSpec((None, tk, tn), lambda i, j, kk: (j // nn_half, kk, j % nn_half))
    elif mode == "nt":
        b_spec = pl.BlockSpec((tn, tk), lambda i, j, kk: (j, kk))
    else:
        b_spec = pl.BlockSpec((tk, tn), lambda i, j, kk: (kk, j))

    def body(a_ref, b_ref, o_ref, acc_ref):
        kk = pl.program_id(2)

        @pl.when(kk == 0)
        def _():
            acc_ref[...] = jnp.zeros_like(acc_ref)

        acc_ref[...] += _dot(a_ref[...], b_ref[...], dims)

        @pl.when(kk == nk - 1)
        def _():
            o_ref[...] = acc_ref[...].astype(o_ref.dtype)

    return pl.pallas_call(
        body,
        name=name,
        grid=(m // tm, n // tn, nk),
        in_specs=[a_spec, b_spec],
        out_specs=pl.BlockSpec((tm, tn), lambda i, j, kk: (i, j)),
        out_shape=jax.ShapeDtypeStruct((m, n), out_dtype),
        scratch_shapes=[pltpu.VMEM((tm, tn), F32)],
        compiler_params=_params(("parallel", "parallel", "arbitrary")),
    )(a, b)


def _sigmoid(x):
    return 1.0 / (1.0 + jnp.exp(-x))


def _ffn_in(xb, w_in, name):
    s, d = xb.shape
    f = w_in.shape[1] // 2
    tm = _tile(s, 1024, 128)
    tn = _tile(f, 256, 128)
    nj = f // tn

    def body(x_ref, wg_ref, wu_ref, g_ref, u_ref, a_ref):
        x = x_ref[...]
        g = _dot(x, wg_ref[...], NN)
        u = _dot(x, wu_ref[...], NN)
        g_ref[...] = g
        u_ref[...] = u
        a_ref[...] = (g * _sigmoid(g) * u).astype(a_ref.dtype)

    out = pl.BlockSpec((tm, tn), lambda i, j: (i, j))
    return pl.pallas_call(
        body,
        name=name,
        grid=(s // tm, nj),
        in_specs=[
            pl.BlockSpec((tm, d), lambda i, j: (i, 0)),
            pl.BlockSpec((d, tn), lambda i, j: (0, j)),
            pl.BlockSpec((d, tn), lambda i, j: (0, j + nj)),
        ],
        out_specs=[out, out, out],
        out_shape=[
            jax.ShapeDtypeStruct((s, f), F32),
            jax.ShapeDtypeStruct((s, f), F32),
            jax.ShapeDtypeStruct((s, f), MXU_DTYPE),
        ],
        compiler_params=_params(("parallel", "parallel")),
    )(xb, w_in, w_in)


def _ffn_dact(dyb, w_out, gate, up, name):
    s, d = dyb.shape
    f = w_out.shape[0]
    tm = _tile(s, 1024, 128)
    tn = _tile(f, 256, 128)

    def body(dy_ref, w_ref, g_ref, u_ref, o_ref):
        dact = _dot(dy_ref[...], w_ref[...], NT)
        g = g_ref[...]
        sig = _sigmoid(g)
        o_ref[0] = (dact * u_ref[...] * (sig * (1.0 + g * (1.0 - sig)))).astype(o_ref.dtype)
        o_ref[1] = (dact * (g * sig)).astype(o_ref.dtype)

    tile = pl.BlockSpec((tm, tn), lambda i, j: (i, j))
    return pl.pallas_call(
        body,
        name=name,
        grid=(s // tm, f // tn),
        in_specs=[
            pl.BlockSpec((tm, d), lambda i, j: (i, 0)),
            pl.BlockSpec((tn, d), lambda i, j: (j, 0)),
            tile,
            tile,
        ],
        out_specs=pl.BlockSpec((2, tm, tn), lambda i, j: (0, i, j)),
        out_shape=jax.ShapeDtypeStruct((2, s, f), MXU_DTYPE),
        compiler_params=_params(("parallel", "parallel")),
    )(dyb, w_out, gate, up)


def _ple_fwd(x, xb, p, w_gate, w_proj, name):
    s, d = x.shape
    pd = p.shape[1]
    tm = _tile(s, 1024, 128)
    tn = _tile(d, 512, 128)

    def body(x_ref, xb_ref, p_ref, wg_ref, wp_ref, o_ref, ob_ref, u_ref, e_ref):
        u = _dot(xb_ref[...], wg_ref[...], NN)
        e = _dot(p_ref[...], wp_ref[...], NN)
        out = x_ref[...] + _sigmoid(u) * e
        o_ref[...] = out
        ob_ref[...] = out.astype(ob_ref.dtype)
        u_ref[...] = u
        e_ref[...] = e

    tile = pl.BlockSpec((tm, tn), lambda i, j: (i, j))
    return pl.pallas_call(
        body,
        name=name,
        grid=(s // tm, d // tn),
        in_specs=[
            tile,
            pl.BlockSpec((tm, d), lambda i, j: (i, 0)),
            pl.BlockSpec((tm, pd), lambda i, j: (i, 0)),
            pl.BlockSpec((d, tn), lambda i, j: (0, j)),
            pl.BlockSpec((pd, tn), lambda i, j: (0, j)),
        ],
        out_specs=[tile, tile, tile, tile],
        out_shape=[
            jax.ShapeDtypeStruct((s, d), F32),
            jax.ShapeDtypeStruct((s, d), MXU_DTYPE),
            jax.ShapeDtypeStruct((s, d), F32),
            jax.ShapeDtypeStruct((s, d), F32),
        ],
        compiler_params=_params(("parallel", "parallel")),
    )(x, xb, p, w_gate, w_proj)


def _rows_spec(ts, d):
    return pl.BlockSpec((ts, d), lambda i: (i, 0))


def _ln_fwd(x, y, g, b, alpha, beta, name):
    s, d = x.shape
    ts = _tile(s, 512, 8)

    def body(x_ref, y_ref, g_ref, b_ref, o_ref, ob_ref, xh_ref, rs_ref):
        r = alpha * x_ref[...] + beta * y_ref[...]
        mu = jnp.mean(r, axis=1, keepdims=True)
        cen = r - mu
        var = jnp.mean(cen * cen, axis=1, keepdims=True)
        rstd = lax.rsqrt(var + LN_EPS)
        xhat = cen * rstd
        out = xhat * g_ref[...] + b_ref[...]
        o_ref[...] = out
        ob_ref[...] = out.astype(ob_ref.dtype)
        xh_ref[...] = xhat
        rs_ref[...] = rstd

    vec = pl.BlockSpec((1, d), lambda i: (0, 0))
    return pl.pallas_call(
        body,
        name=name,
        grid=(s // ts,),
        in_specs=[_rows_spec(ts, d), _rows_spec(ts, d), vec, vec],
        out_specs=[_rows_spec(ts, d), _rows_spec(ts, d), _rows_spec(ts, d), _rows_spec(ts, 1)],
        out_shape=[
            jax.ShapeDtypeStruct((s, d), F32),
            jax.ShapeDtypeStruct((s, d), MXU_DTYPE),
            jax.ShapeDtypeStruct((s, d), F32),
            jax.ShapeDtypeStruct((s, 1), F32),
        ],
        compiler_params=_params(("parallel",)),
    )(x, y, g.reshape(1, d), b.reshape(1, d))


def _ln_bwd(ga, gb, ca, xhat, rstd, g, beta, name):
    s, d = xhat.shape
    ts = _tile(s, 512, 8)

    def body(ga_ref, gb_ref, xh_ref, rs_ref, g_ref, dr_ref, dyb_ref, dg_ref, db_ref):
        @pl.when(pl.program_id(0) == 0)
        def _():
            dg_ref[...] = jnp.zeros_like(dg_ref)
            db_ref[...] = jnp.zeros_like(db_ref)

        dout = ca * ga_ref[...] + gb_ref[...]
        xhat = xh_ref[...]
        dg_ref[...] += jnp.sum(dout * xhat, axis=0, keepdims=True)
        db_ref[...] += jnp.sum(dout, axis=0, keepdims=True)
        dxh = dout * g_ref[...]
        m1 = jnp.mean(dxh, axis=1, keepdims=True)
        m2 = jnp.mean(dxh * xhat, axis=1, keepdims=True)
        dr = rs_ref[...] * (dxh - m1 - xhat * m2)
        dr_ref[...] = dr
        dyb_ref[...] = (beta * dr).astype(dyb_ref.dtype)

    vec = pl.BlockSpec((1, d), lambda i: (0, 0))
    return pl.pallas_call(
        body,
        name=name,
        grid=(s // ts,),
        in_specs=[_rows_spec(ts, d), _rows_spec(ts, d), _rows_spec(ts, d), _rows_spec(ts, 1), vec],
        out_specs=[_rows_spec(ts, d), _rows_spec(ts, d), vec, vec],
        out_shape=[
            jax.ShapeDtypeStruct((s, d), F32),
            jax.ShapeDtypeStruct((s, d), MXU_DTYPE),
            jax.ShapeDtypeStruct((1, d), F32),
            jax.ShapeDtypeStruct((1, d), F32),
        ],
        compiler_params=_params(("arbitrary",)),
    )(ga, gb, xhat, rstd, g.reshape(1, d))


def _ple_bwd(ga, gb, ca, u, e, name):
    s, d = u.shape
    ts = _tile(s, 512, 8)
    grads = [ga] if gb is None else [ga, gb]

    def body(*refs):
        u_ref, e_ref, dx_ref, du_ref, de_ref = refs[len(grads):]
        dx = ca * refs[0][...]
        if gb is not None:
            dx = dx + refs[1][...]
        sig = _sigmoid(u_ref[...])
        dx_ref[...] = dx
        du_ref[...] = (dx * e_ref[...] * sig * (1.0 - sig)).astype(du_ref.dtype)
        de_ref[...] = (dx * sig).astype(de_ref.dtype)

    return pl.pallas_call(
        body,
        name=name,
        grid=(s // ts,),
        in_specs=[_rows_spec(ts, d)] * (len(grads) + 2),
        out_specs=[_rows_spec(ts, d)] * 3,
        out_shape=[
            jax.ShapeDtypeStruct((s, d), F32),
            jax.ShapeDtypeStruct((s, d), MXU_DTYPE),
            jax.ShapeDtypeStruct((s, d), MXU_DTYPE),
        ],
        compiler_params=_params(("parallel",)),
    )(*grads, u, e)


def _axpy(ga, gb, ca, name):
    s, d = ga.shape
    ts = _tile(s, 512, 8)

    def body(ga_ref, gb_ref, o_ref):
        o_ref[...] = ca * ga_ref[...] + gb_ref[...]

    return pl.pallas_call(
        body,
        name=name,
        grid=(s // ts,),
        in_specs=[_rows_spec(ts, d)] * 2,
        out_specs=_rows_spec(ts, d),
        out_shape=jax.ShapeDtypeStruct((s, d), F32),
        compiler_params=_params(("parallel",)),
    )(ga, gb)


def _loss(y, target, name):
    s, d = y.shape
    ts = _tile(s, 512, 8)

    def body(y_ref, t_ref, l_ref, dy_ref):
        @pl.when(pl.program_id(0) == 0)
        def _():
            l_ref[...] = jnp.zeros_like(l_ref)

        err = y_ref[...] - t_ref[...]
        l_ref[...] += (0.5 / d) * jnp.sum(jnp.sum(err * err, axis=1, keepdims=True), axis=0, keepdims=True)
        dy_ref[...] = err * (1.0 / d)

    return pl.pallas_call(
        body,
        name=name,
        grid=(s // ts,),
        in_specs=[_rows_spec(ts, d)] * 2,
        out_specs=[pl.BlockSpec((1, 1), lambda i: (0, 0)), _rows_spec(ts, d)],
        out_shape=[jax.ShapeDtypeStruct((1, 1), F32), jax.ShapeDtypeStruct((s, d), F32)],
        compiler_params=_params(("arbitrary",)),
    )(y, target)


def _rope_tables(seq):
    pos = jnp.arange(seq, dtype=F32)
    inv = ROPE_THETA ** (-jnp.arange(0, HEAD_DIM, 2, dtype=F32) / HEAD_DIM)
    ang = pos[:, None] * inv[None, :]
    cos, sin = jnp.cos(ang), jnp.sin(ang)
    cos2 = jnp.concatenate([cos, cos, cos, cos, jnp.ones((seq, 128), F32)], axis=1)
    sin2 = jnp.concatenate([-sin, sin, -sin, sin, jnp.zeros((seq, 128), F32)], axis=1)
    return cos2, sin2


def _rope(h, cos2, sin2, plain_block, out_dtype, name):
    s, n = h.shape
    ts = _tile(s, 512, 8)

    def body(h_ref, c_ref, s_ref, o_ref):
        x = h_ref[...].astype(F32)
        lane = lax.broadcasted_iota(jnp.int32, x.shape, 1)
        partner = jnp.where(lane % HEAD_DIM < HEAD_DIM // 2, pltpu.roll(x, 128 - HEAD_DIM // 2, 1), pltpu.roll(x, HEAD_DIM // 2, 1))
        o_ref[...] = (x * c_ref[...] + partner * s_ref[...]).astype(o_ref.dtype)

    tile = pl.BlockSpec((ts, 128), lambda i, j: (i, j))
    table = pl.BlockSpec((ts, 128), lambda i, j: (i, plain_block(j)))
    return pl.pallas_call(
        body,
        name=name,
        grid=(s // ts, n // 128),
        in_specs=[tile, table, table],
        out_specs=tile,
        out_shape=jax.ShapeDtypeStruct((s, n), out_dtype),
        compiler_params=_params(("parallel", "parallel")),
    )(h, cos2, sin2)


SB_TQ = 512
SB_BLK = 128


def _tri2(strict):
    row = lax.broadcasted_iota(jnp.int32, (2 * SB_BLK, SB_BLK), 0) % SB_BLK
    col = lax.broadcasted_iota(jnp.int32, (2 * SB_BLK, SB_BLK), 1)
    return (row > col if strict else row >= col).astype(MXU_DTYPE)


def _cumsum_dot(x, tri2):
    hi = x.astype(MXU_DTYPE)
    lo = x - hi.astype(F32)
    return _dot(jnp.concatenate([hi, lo.astype(MXU_DTYPE)], axis=1), tri2, NN)


def _sb_tile(qs, kb, after_c, tri_after, valid):
    z = _dot(qs, kb, NT)
    l1p = jnp.log(1.0 + jnp.exp(-jnp.abs(z)))
    sp = jnp.maximum(z, 0.0) + l1p
    ls = jnp.minimum(z, 0.0) - l1p
    if valid is not None:
        sp = jnp.where(valid, sp, 0.0)
    after = after_c + _cumsum_dot(sp, tri_after)
    a = jnp.exp(ls - after)
    if valid is not None:
        a = jnp.where(valid, a, 0.0)
    return sp, ls, a


def _sb_setup(q_ref, k_ref, v_ref, n_sub, scale):
    qs = [q_ref[0, u * SB_BLK:(u + 1) * SB_BLK, :] * scale for u in range(n_sub)]
    row = lax.broadcasted_iota(jnp.int32, (SB_BLK, SB_BLK), 0)
    col = lax.broadcasted_iota(jnp.int32, (SB_BLK, SB_BLK), 1)

    def load(jj):
        start = pl.multiple_of(jj * SB_BLK, SB_BLK)
        return start, k_ref[0, pl.ds(start, SB_BLK), :], v_ref[0, pl.ds(start, SB_BLK), :]

    return qs, col < row, load


def _sb_fwd(q, k, v, name):
    nh, s, dh = q.shape
    tq = min(SB_TQ, s)
    n_sub = tq // SB_BLK
    scale = dh ** -0.5

    def body(q_ref, k_ref, v_ref, o_ref, ob_ref):
        base = pl.program_id(1) * n_sub
        qs, diag_valid, load = _sb_setup(q_ref, k_ref, v_ref, n_sub, scale)
        tri_after = _tri2(True)

        def step(u, kb, vb, carry, masked):
            after_c, acc = carry
            sp, _, a = _sb_tile(qs[u], kb, after_c, tri_after, diag_valid if masked else None)
            return after_c + jnp.sum(sp, axis=1, keepdims=True), acc + _dot(a, vb, NN)

        carry = [(jnp.zeros((SB_BLK, 1), F32), jnp.zeros((SB_BLK, dh), F32)) for _ in range(n_sub)]
        for d in range(n_sub - 1, -1, -1):
            _, kb, vb = load(base + d)
            for u in range(d, n_sub):
                carry[u] = step(u, kb, vb, carry[u], u == d)

        def rest(t, carry):
            _, kb, vb = load(base - 1 - t)
            return tuple(step(u, kb, vb, carry[u], False) for u in range(n_sub))

        carry = lax.fori_loop(0, base, rest, tuple(carry))
        for u in range(n_sub):
            rows = slice(u * SB_BLK, (u + 1) * SB_BLK)
            o_ref[0, rows, :] = carry[u][1]
            ob_ref[0, rows, :] = carry[u][1].astype(ob_ref.dtype)

    blk = pl.BlockSpec((1, tq, dh), lambda h, i: (h, i, 0))
    full = pl.BlockSpec((1, s, dh), lambda h, i: (h, 0, 0))
    return pl.pallas_call(
        body,
        name=name,
        grid=(nh, s // tq),
        in_specs=[blk, full, full],
        out_specs=[blk, blk],
        out_shape=[jax.ShapeDtypeStruct((nh, s, dh), F32), jax.ShapeDtypeStruct((nh, s, dh), MXU_DTYPE)],
        compiler_params=_params(("parallel", "parallel")),
    )(q, k, v)


def _sb_bwd(q, k, v, o, do, name):
    nh, s, dh = q.shape
    tq = min(SB_TQ, s)
    n_sub = tq // SB_BLK
    scale = dh ** -0.5

    def body(q_ref, k_ref, v_ref, o_ref, do_ref, dq_ref, dk_ref, dv_ref):
        i = pl.program_id(1)
        base = i * n_sub

        @pl.when(i == 0)
        def _():
            dk_ref[...] = jnp.zeros_like(dk_ref)
            dv_ref[...] = jnp.zeros_like(dv_ref)

        qs, diag_valid, load = _sb_setup(q_ref, k_ref, v_ref, n_sub, scale)
        dob = [do_ref[0, u * SB_BLK:(u + 1) * SB_BLK, :] for u in range(n_sub)]
        total = [jnp.sum(dob[u].astype(F32) * o_ref[0, u * SB_BLK:(u + 1) * SB_BLK, :], axis=1, keepdims=True) for u in range(n_sub)]
        tri_after = _tri2(True)
        tri_from = _tri2(False)

        def step(u, kb, vb, carry, masked):
            after_c, from_c, dq = carry
            sp, ls, a = _sb_tile(qs[u], kb, after_c, tri_after, diag_valid if masked else None)
            ab = a.astype(MXU_DTYPE)
            dl = ab.astype(F32) * _dot(dob[u], vb, NT)
            from_j = from_c + _cumsum_dot(dl, tri_from)
            dz = dl - jnp.exp(ls) * (dl + total[u] - from_j)
            if masked:
                dz = jnp.where(diag_valid, dz, 0.0)
            dzb = dz.astype(MXU_DTYPE)
            carry = (
                after_c + jnp.sum(sp, axis=1, keepdims=True),
                from_c + jnp.sum(dl, axis=1, keepdims=True),
                dq + _dot(dzb, kb, NN),
            )
            return carry, _dot(dzb, qs[u], TN), _dot(ab, dob[u], TN)

        def key_block(jj, carry, first_sub, diagonal):
            start, kb, vb = load(jj)
            carry = list(carry)
            dk = dv = None
            for u in range(first_sub, n_sub):
                carry[u], dk_u, dv_u = step(u, kb, vb, carry[u], diagonal and u == first_sub)
                dk = dk_u if dk is None else dk + dk_u
                dv = dv_u if dv is None else dv + dv_u
            dk_ref[0, pl.ds(start, SB_BLK), :] += dk
            dv_ref[0, pl.ds(start, SB_BLK), :] += dv
            return tuple(carry)

        carry = tuple((jnp.zeros((SB_BLK, 1), F32), jnp.zeros((SB_BLK, 1), F32), jnp.zeros((SB_BLK, dh), F32)) for _ in range(n_sub))
        for d in range(n_sub - 1, -1, -1):
            carry = key_block(base + d, carry, d, True)
        carry = lax.fori_loop(0, base, lambda t, c: key_block(base - 1 - t, c, 0, False), carry)
        for u in range(n_sub):
            dq_ref[0, u * SB_BLK:(u + 1) * SB_BLK, :] = carry[u][2] * scale

    blk = pl.BlockSpec((1, tq, dh), lambda h, i: (h, i, 0))
    full = pl.BlockSpec((1, s, dh), lambda h, i: (h, 0, 0))
    shape = jax.ShapeDtypeStruct((nh, s, dh), F32)
    return pl.pallas_call(
        body,
        name=name,
        grid=(nh, s // tq),
        in_specs=[blk, full, full, blk, blk],
        out_specs=[blk, full, full],
        out_shape=[shape, shape, shape],
        compiler_params=_params(("parallel", "arbitrary")),
    )(q, k, v, o, do)


BAND_TQ = 512


def _band_scores(q_ref, k_ref, i, sub, tq, length, max_dist, scale):
    t0 = i * tq + sub * ATT_BLK
    ks = pl.multiple_of(jnp.minimum(jnp.maximum(t0 - ATT_BLK, 0), length - 2 * ATT_BLK), ATT_BLK)
    qs = q_ref[0, sub * ATT_BLK:(sub + 1) * ATT_BLK, :] * scale
    kw = k_ref[0, pl.ds(ks, 2 * ATT_BLK), :]
    sc = _dot(qs, kw, NT)
    diff = (t0 + lax.broadcasted_iota(jnp.int32, sc.shape, 0)) - (ks + lax.broadcasted_iota(jnp.int32, sc.shape, 1))
    valid = (diff >= 0) & (diff <= max_dist)
    return ks, qs, kw, jnp.where(valid, sc, NEG_INF)


def _band_fwd(q, k, v, sinks, max_dist, name):
    bq, length, dh = q.shape
    group = bq // k.shape[0]
    tq = min(BAND_TQ, length)
    scale = dh ** -0.5
    n_sink = 0 if sinks is None else sinks.shape[0]

    def body(*refs):
        if n_sink:
            sink_ref, q_ref, k_ref, v_ref, o_ref, lse_ref = refs
            sink = sink_ref[pl.program_id(0) % n_sink]
        else:
            q_ref, k_ref, v_ref, o_ref, lse_ref = refs
        i = pl.program_id(1)
        for sub in range(tq // ATT_BLK):
            ks, _, _, sc = _band_scores(q_ref, k_ref, i, sub, tq, length, max_dist, scale)
            m = jnp.max(sc, axis=1, keepdims=True)
            if n_sink:
                m = jnp.maximum(m, sink)
            e = jnp.exp(sc - m)
            den = jnp.sum(e, axis=1, keepdims=True)
            if n_sink:
                den = den + jnp.exp(sink - m)
            rows = slice(sub * ATT_BLK, (sub + 1) * ATT_BLK)
            o_ref[0, rows, :] = _dot(e / den, v_ref[0, pl.ds(ks, 2 * ATT_BLK), :], NN)
            lse_ref[0, rows, :] = m + jnp.log(den)

    qblk = pl.BlockSpec((1, tq, dh), lambda b, i: (b, i, 0))
    kfull = pl.BlockSpec((1, length, dh), lambda b, i: (b // group, 0, 0))
    in_specs = [qblk, kfull, kfull]
    args = [q, k, v]
    if n_sink:
        in_specs = [pl.BlockSpec(memory_space=pltpu.SMEM)] + in_specs
        args = [sinks] + args
    return pl.pallas_call(
        body,
        name=name,
        grid=(bq, length // tq),
        in_specs=in_specs,
        out_specs=[qblk, pl.BlockSpec((1, tq, 1), lambda b, i: (b, i, 0))],
        out_shape=[jax.ShapeDtypeStruct((bq, length, dh), F32), jax.ShapeDtypeStruct((bq, length, 1), F32)],
        compiler_params=_params(("parallel", "parallel")),
    )(*args)


def _band_bwd(q, k, v, do, lse, delta, sinks, max_dist, name):
    bq, length, dh = q.shape
    group = bq // k.shape[0]
    tq = min(BAND_TQ, length)
    scale = dh ** -0.5
    n_sink = 0 if sinks is None else sinks.shape[0]

    def body(*refs):
        if n_sink:
            sink_ref, q_ref, k_ref, v_ref, do_ref, lse_ref, dl_ref, dq_ref, dk_ref, dv_ref, ds_ref = refs
            sink = sink_ref[pl.program_id(0) % n_sink]
        else:
            q_ref, k_ref, v_ref, do_ref, lse_ref, dl_ref, dq_ref, dk_ref, dv_ref, ds_ref = refs
        i = pl.program_id(1)

        @pl.when((i == 0) & (pl.program_id(0) % group == 0))
        def _():
            dk_ref[...] = jnp.zeros_like(dk_ref)
            dv_ref[...] = jnp.zeros_like(dv_ref)

        @pl.when(i == 0)
        def _():
            ds_ref[...] = jnp.zeros_like(ds_ref)

        for sub in range(tq // ATT_BLK):
            ks, qs, kw, sc = _band_scores(q_ref, k_ref, i, sub, tq, length, max_dist, scale)
            rows = slice(sub * ATT_BLK, (sub + 1) * ATT_BLK)
            lse = lse_ref[0, rows, :]
            delta_r = dl_ref[0, rows, :]
            dob = do_ref[0, rows, :]
            p = jnp.exp(sc - lse)
            dp = _dot(dob, v_ref[0, pl.ds(ks, 2 * ATT_BLK), :], NT)
            dsb = (p * (dp - delta_r)).astype(MXU_DTYPE)
            dq_ref[0, rows, :] = _dot(dsb, kw, NN) * scale
            dk_ref[0, pl.ds(ks, 2 * ATT_BLK), :] += _dot(dsb, qs, TN)
            dv_ref[0, pl.ds(ks, 2 * ATT_BLK), :] += _dot(p, dob, TN)
            if n_sink:
                ds_ref[...] += jnp.sum(-jnp.exp(sink - lse) * delta_r, axis=0, keepdims=True)

    qblk = pl.BlockSpec((1, tq, dh), lambda b, i: (b, i, 0))
    qcol = pl.BlockSpec((1, tq, 1), lambda b, i: (b, i, 0))
    kfull = pl.BlockSpec((1, length, dh), lambda b, i: (b // group, 0, 0))
    in_specs = [qblk, kfull, kfull, qblk, qcol, qcol]
    args = [q, k, v, do, lse, delta]
    if n_sink:
        in_specs = [pl.BlockSpec(memory_space=pltpu.SMEM)] + in_specs
        args = [sinks] + args
    kshape = jax.ShapeDtypeStruct((k.shape[0], length, dh), F32)
    return pl.pallas_call(
        body,
        name=name,
        grid=(bq, length // tq),
        in_specs=in_specs,
        out_specs=[qblk, kfull, kfull, pl.BlockSpec((1, 8, 128), lambda b, i: (b, 0, 0))],
        out_shape=[jax.ShapeDtypeStruct((bq, length, dh), F32), kshape, kshape, jax.ShapeDtypeStruct((bq, 8, 128), F32)],
        compiler_params=_params(("arbitrary", "arbitrary")),
    )(*args)


def _merge_weights(lse_refs):
    lses = [r[0] for r in lse_refs]
    m = functools.reduce(jnp.maximum, lses)
    es = [jnp.exp(l - m) for l in lses]
    den = functools.reduce(lambda a, b: a + b, es)
    return [e / den for e in es]


def _merge_fwd(outs, lses, name):
    n = len(outs)
    nh, s, dh = outs[0].shape
    ts = _tile(s, 1024, 8)

    def body(*refs):
        ws = _merge_weights(refs[n:2 * n])
        o = functools.reduce(lambda a, b: a + b, [w * r[0] for w, r in zip(ws, refs[:n])])
        refs[2 * n][0] = o
        refs[2 * n + 1][0] = o.astype(MXU_DTYPE)

    blk = pl.BlockSpec((1, ts, dh), lambda h, i: (h, i, 0))
    col = pl.BlockSpec((1, ts, 1), lambda h, i: (h, i, 0))
    return pl.pallas_call(
        body,
        name=name,
        grid=(nh, s // ts),
        in_specs=[blk] * n + [col] * n,
        out_specs=[blk, blk],
        out_shape=[jax.ShapeDtypeStruct((nh, s, dh), F32), jax.ShapeDtypeStruct((nh, s, dh), MXU_DTYPE)],
        compiler_params=_params(("parallel", "parallel")),
    )(*outs, *lses)


def _merge_bwd(do, o, lses, name):
    n = len(lses)
    nh, s, dh = o.shape
    ts = _tile(s, 1024, 8)

    def body(*refs):
        do_ref, o_ref = refs[:2]
        ws = _merge_weights(refs[2:2 + n])
        dof = do_ref[0].astype(F32)
        base = jnp.sum(dof * o_ref[0], axis=1, keepdims=True)
        for g in range(n):
            refs[2 + n + g][0] = (ws[g] * dof).astype(MXU_DTYPE)
            refs[2 + 2 * n + g][0] = ws[g] * base

    blk = pl.BlockSpec((1, ts, dh), lambda h, i: (h, i, 0))
    col = pl.BlockSpec((1, ts, 1), lambda h, i: (h, i, 0))
    res = pl.pallas_call(
        body,
        name=name,
        grid=(nh, s // ts),
        in_specs=[blk, blk] + [col] * n,
        out_specs=[blk] * n + [col] * n,
        out_shape=[jax.ShapeDtypeStruct((nh, s, dh), MXU_DTYPE)] * n + [jax.ShapeDtypeStruct((nh, s, 1), F32)] * n,
        compiler_params=_params(("parallel", "parallel")),
    )(do, o, *lses)
    return res[:n], res[n:]


def _to_heads(a, dil=1):
    s, n = a.shape
    nh = n // HEAD_DIM
    a = a.reshape(s // dil, dil, nh, HEAD_DIM).transpose(1, 2, 0, 3)
    return a.reshape(dil * nh, s // dil, HEAD_DIM)


def _from_heads(a, dil=1):
    b, ls, c = a.shape
    nh = b // dil
    return a.reshape(dil, nh, ls, c).transpose(2, 0, 1, 3).reshape(ls * dil, nh * c)


def _restride(a, dil):
    nh, s, c = a.shape
    return a.reshape(nh, s // dil, dil, c).transpose(2, 0, 1, 3).reshape(dil * nh, s // dil, c)


def _unstride(a, dil):
    b, ls, c = a.shape
    nh = b // dil
    return a.reshape(dil, nh, ls, c).transpose(1, 2, 0, 3).reshape(nh, ls * dil, c)


ANY = pl.BlockSpec(memory_space=pl.ANY)


def _position():
    x, y, c = lax.axis_index("x"), lax.axis_index("y"), lax.axis_index("c")
    return x, y, c, [(1 - x, y), (x, 1 - y), (1 - x, 1 - y)]


def _remote(src, dst, send_sems, recv_sems, k, to):
    return pltpu.make_async_remote_copy(src_ref=src, dst_ref=dst, send_sem=send_sems.at[k], recv_sem=recv_sems.at[k], device_id=to, device_id_type=MESH)


def _gather_chips(arrs, name):
    n = len(arrs)

    def body(*refs):
        f_refs, g_refs = refs[:n], refs[n:2 * n]
        send_sems, recv_sems, local_sems = refs[2 * n:]
        x, y, c, chips = _position()
        me, sibling = (x, y, c), (x, y, 1 - c)

        def half(j, ref, sel):
            rows = f_refs[j].shape[1] // 2
            return ref.at[:, pl.ds(sel * rows, rows), :]

        def slot(j, chip, sel):
            return half(j, g_refs[j].at[2 * chip[0] + chip[1]], sel)

        mine = [pltpu.make_async_copy(f_refs[j], g_refs[j].at[2 * x + y], local_sems.at[j]) for j in range(n)]
        first = [_remote(half(j, f_refs[j], c), slot(j, (x, y), c), send_sems, recv_sems, 6 * j + k, (*chip, c)) for j in range(n) for k, chip in enumerate(chips)]
        for cp in mine + first:
            cp.start()
        passed = []
        for j in range(n):
            for k, chip in enumerate(chips):
                _remote(slot(j, chip, c), slot(j, chip, c), send_sems, recv_sems, 6 * j + k, me).wait_recv()
                passed.append(_remote(slot(j, chip, c), slot(j, chip, c), send_sems, recv_sems, 6 * j + 3 + k, sibling))
                passed[-1].start()
        for j in range(n):
            for k, chip in enumerate(chips):
                _remote(slot(j, chip, 1 - c), slot(j, chip, 1 - c), send_sems, recv_sems, 6 * j + 3 + k, me).wait_recv()
        for cp in first + passed:
            cp.wait_send()
        for cp in mine:
            cp.wait()

    return pl.pallas_call(
        body,
        name=name,
        in_specs=[ANY] * n,
        out_specs=[ANY] * n,
        out_shape=[jax.ShapeDtypeStruct((N_CHIPS,) + a.shape, a.dtype) for a in arrs],
        scratch_shapes=[pltpu.SemaphoreType.DMA((6 * n,)), pltpu.SemaphoreType.DMA((6 * n,)), pltpu.SemaphoreType.DMA((n,))],
    )(*arrs)


def _swap_sibling(arrs, name):
    n = len(arrs)

    def body(*refs):
        x, y, c, _ = _position()
        send_sems, recv_sems = refs[2 * n:]
        copies = [_remote(refs[j], refs[n + j], send_sems, recv_sems, j, (x, y, 1 - c)) for j in range(n)]
        for cp in copies:
            cp.start()
        for cp in copies:
            cp.wait()

    return pl.pallas_call(
        body,
        name=name,
        in_specs=[ANY] * n,
        out_specs=[ANY] * n,
        out_shape=[jax.ShapeDtypeStruct(a.shape, a.dtype) for a in arrs],
        scratch_shapes=[pltpu.SemaphoreType.DMA((n,)), pltpu.SemaphoreType.DMA((n,))],
    )(*arrs)


def _exchange_chips(parts, name):
    n = len(parts)

    def body(*refs):
        p_refs, q_refs = refs[:n], refs[n:2 * n]
        send_sems, recv_sems, local_sems = refs[2 * n:]
        x, y, c, chips = _position()
        my_slot = 2 * x + y
        mine = [pltpu.make_async_copy(p_refs[j].at[my_slot], q_refs[j].at[my_slot], local_sems.at[j]) for j in range(n)]
        sends = [_remote(p_refs[j].at[2 * px + py], q_refs[j].at[my_slot], send_sems, recv_sems, 3 * j + k, (px, py, c)) for j in range(n) for k, (px, py) in enumerate(chips)]
        for cp in mine + sends:
            cp.start()
        for j in range(n):
            for k, (px, py) in enumerate(chips):
                landed = q_refs[j].at[2 * px + py]
                _remote(landed, landed, send_sems, recv_sems, 3 * j + k, (x, y, c)).wait_recv()
        for cp in sends:
            cp.wait_send()
        for cp in mine:
            cp.wait()

    return pl.pallas_call(
        body,
        name=name,
        in_specs=[ANY] * n,
        out_specs=[ANY] * n,
        out_shape=[jax.ShapeDtypeStruct(a.shape, a.dtype) for a in parts],
        scratch_shapes=[pltpu.SemaphoreType.DMA((3 * n,)), pltpu.SemaphoreType.DMA((3 * n,)), pltpu.SemaphoreType.DMA((n,))],
    )(*parts)


def _share_sibling(halves, name):
    n = len(halves)

    def body(*refs):
        h_refs, o_refs = refs[:n], refs[n:2 * n]
        send_sems, recv_sems, local_sems = refs[2 * n:]
        x, y, c, _ = _position()

        def rows(j, sel):
            r = h_refs[j].shape[0]
            return o_refs[j].at[pl.ds(sel * r, r), :]

        mine = [pltpu.make_async_copy(h_refs[j], rows(j, c), local_sems.at[j]) for j in range(n)]
        sends = [_remote(h_refs[j], rows(j, c), send_sems, recv_sems, j, (x, y, 1 - c)) for j in range(n)]
        for cp in mine + sends:
            cp.start()
        for j in range(n):
            _remote(rows(j, 1 - c), rows(j, 1 - c), send_sems, recv_sems, j, (x, y, c)).wait_recv()
        for cp in sends:
            cp.wait_send()
        for cp in mine:
            cp.wait()

    return pl.pallas_call(
        body,
        name=name,
        in_specs=[ANY] * n,
        out_specs=[ANY] * n,
        out_shape=[jax.ShapeDtypeStruct((2 * a.shape[0],) + a.shape[1:], a.dtype) for a in halves],
        scratch_shapes=[pltpu.SemaphoreType.DMA((n,)), pltpu.SemaphoreType.DMA((n,)), pltpu.SemaphoreType.DMA((n,))],
    )(*halves)


def _sum_rows(r, c, n_in):
    return _tile(r, max(8, (1 << 20) // (c * (n_in + 1))), 8)


def _add2(a, b, name):
    n, r, c = a.shape
    tr = _sum_rows(r, c, 2)

    def body(a_ref, b_ref, o_ref):
        o_ref[...] = a_ref[...] + b_ref[...]

    blk = pl.BlockSpec((1, tr, c), lambda s, i: (s, i, 0))
    return pl.pallas_call(
        body,
        name=name,
        grid=(n, r // tr),
        in_specs=[blk, blk],
        out_specs=blk,
        out_shape=jax.ShapeDtypeStruct(a.shape, a.dtype),
        compiler_params=_params(("parallel", "parallel")),
    )(a, b)


def _sum_slots(q, name):
    n, r, c = q.shape
    tr = _sum_rows(r, c, n)

    def body(q_ref, o_ref):
        acc = q_ref[0]
        for s in range(1, n):
            acc = acc + q_ref[s]
        o_ref[...] = acc

    return pl.pallas_call(
        body,
        name=name,
        grid=(r // tr,),
        in_specs=[pl.BlockSpec((n, tr, c), lambda i: (0, i, 0))],
        out_specs=pl.BlockSpec((tr, c), lambda i: (i, 0)),
        out_shape=jax.ShapeDtypeStruct((r, c), q.dtype),
        compiler_params=_params(("parallel",)),
    )(q)


def _adamw(w, g, m, v, name):
    r, c = w.shape
    tr = _tile(r, 256, 8)
    c1 = 1.0 - ADAM_B1 ** ADAM_STEP
    c2 = 1.0 - ADAM_B2 ** ADAM_STEP

    def body(w_ref, g_ref, m_ref, v_ref, d_ref, nm_ref, nv_ref):
        g = g_ref[...]
        nm = ADAM_B1 * m_ref[...] + (1.0 - ADAM_B1) * g
        nv = ADAM_B2 * v_ref[...] + (1.0 - ADAM_B2) * (g * g)
        d_ref[...] = -ADAM_LR * ((nm / c1) / (jnp.sqrt(nv / c2) + ADAM_EPS) + ADAM_WD * w_ref[...])
        nm_ref[...] = nm
        nv_ref[...] = nv

    blk = pl.BlockSpec((tr, c), lambda i: (i, 0))
    shape = jax.ShapeDtypeStruct((r, c), F32)
    return pl.pallas_call(
        body,
        name=name,
        grid=(r // tr,),
        in_specs=[blk] * 4,
        out_specs=[blk] * 3,
        out_shape=[shape] * 3,
        compiler_params=_params(("parallel",)),
    )(w, g, m, v)


WEIGHTS = ("ffn1_w_in", "ffn1_w_out", "ffn2_w_in", "ffn2_w_out", "ln_g", "ln_b", "sb_w_in", "sb_w_out", "swa_w_in", "swa_sinks", "swa_w_out", "dil_w_in", "dil_w_out", "ple_w_proj", "ple_w_gate")
SHARD_AXIS = {"ffn1_w_in": 2, "ffn1_w_out": 1, "ffn2_w_in": 2, "ffn2_w_out": 1, "ln_g": 2, "ln_b": 2, "sb_w_in": 2, "sb_w_out": 1, "swa_w_in": 2, "swa_sinks": None, "swa_w_out": 1, "dil_w_in": 2, "dil_w_out": 1, "ple_w_proj": 2, "ple_w_gate": 1}
SMALL = ("ln_g", "ln_b", "swa_sinks")
SMALL_COLS = 128
SMALL_UNIT = 16 * SMALL_COLS


def _pack_small(pieces, lead):
    flat = jnp.concatenate([a.reshape(lead + (-1,)) for a in pieces], axis=-1)
    n = flat.shape[-1]
    flat = jnp.pad(flat, [(0, 0)] * len(lead) + [(0, -n % SMALL_UNIT)])
    return flat.reshape(lead + (-1, SMALL_COLS))


def _unpack_small(buf, shapes, lead):
    flat = buf.reshape(lead + (-1,))
    out, off = [], 0
    for shp in shapes:
        n = math.prod(shp)
        out.append(flat[..., off:off + n].reshape(lead + tuple(shp)))
        off += n
    return out


def _reduce_group(grads, c, tag):
    keep = [lax.dynamic_slice_in_dim(g, c * (g.shape[1] // 2), g.shape[1] // 2, axis=1) for g in grads]
    give = [lax.dynamic_slice_in_dim(g, (1 - c) * (g.shape[1] // 2), g.shape[1] // 2, axis=1) for g in grads]
    got = _swap_sibling(give, f"{tag}_pair")
    part = [_add2(k, g, f"{tag}_pair_sum{j}") for j, (k, g) in enumerate(zip(keep, got))]
    landed = _exchange_chips(part, f"{tag}_chips")
    halves = [_sum_slots(q, f"{tag}_chips_sum{j}") for j, q in enumerate(landed)]
    return _share_sibling(halves, f"{tag}_share")


def _full_from_shards(g, axis):
    g = jnp.moveaxis(g, 0, axis)
    return g.reshape(g.shape[:axis] + (g.shape[axis] * g.shape[axis + 1],) + g.shape[axis + 2:])


def _shards_from_full(a, axis):
    a = a.reshape(a.shape[:axis] + (N_CHIPS, a.shape[axis] // N_CHIPS) + a.shape[axis + 1:])
    return jnp.moveaxis(a, axis, 0)


def _ffn_forward(x, xb, w_in, w_out, g, b, alpha, tag):
    gate, up, act = _ffn_in(xb, w_in, f"{tag}_in")
    y = _mm(act, w_out, "nn", F32, f"{tag}_out")
    out, outb, xhat, rstd = _ln_fwd(x, y, g, b, alpha, 0.5, f"{tag}_ln")
    return (out, outb), dict(xb=xb, gate=gate, up=up, act=act, xhat=xhat, rstd=rstd)


def _ffn_backward(dout, saved, w_in, w_out, g, tag):
    dr, dyb, dg, db = _ln_bwd(*dout, saved["xhat"], saved["rstd"], g, 0.5, f"{tag}_ln_bwd")
    dh = _ffn_dact(dyb, w_out, saved["gate"], saved["up"], f"{tag}_dact")
    dw_out = _mm(saved["act"], dyb, "tn", F32, f"{tag}_dw_out")
    dw_in = _mm(saved["xb"], dh, "tn", F32, f"{tag}_dw_in", split_b=True)
    dxb = _mm(dh, w_in, "nt", F32, f"{tag}_dx", split_a=True)
    return dr, dxb, dw_in, dw_out, dg, db


def _sb_forward(xb, w_in, w_out, tag):
    nw = w_out.shape[0]
    h = _mm(xb, w_in, "nn", MXU_DTYPE, f"{tag}_qkv")
    q, k, v = (_to_heads(h[:, j * nw:(j + 1) * nw]) for j in range(3))
    o, ob = _sb_fwd(q, k, v, f"{tag}_att")
    ob = _from_heads(ob)
    return _mm(ob, w_out, "nn", F32, f"{tag}_proj"), dict(q=q, k=k, v=v, o=o, ob=ob)


def _sb_backward(dmix, saved, xb, w_in, w_out, tag):
    dw_out = _mm(saved["ob"], dmix, "tn", F32, f"{tag}_dw_out")
    do = _to_heads(_mm(dmix, w_out, "nt", MXU_DTYPE, f"{tag}_do"))
    dq, dk, dv = _sb_bwd(saved["q"], saved["k"], saved["v"], saved["o"], do, f"{tag}_att_bwd")
    dh = jnp.concatenate([_from_heads(t) for t in (dq, dk, dv)], axis=1).astype(MXU_DTYPE)
    dw_in = _mm(xb, dh, "tn", F32, f"{tag}_dw_in")
    return _mm(dh, w_in, "nt", F32, f"{tag}_dx"), dw_in, dw_out


def _swa_forward(xb, w_in, sinks, w_out, tables, tag):
    nq = w_out.shape[0]
    nkv = (w_in.shape[1] - nq) // 2
    h = _mm(xb, w_in, "nn", F32, f"{tag}_qkv")
    n_rot = (nq + nkv) // 128
    hb = _rope(h, *tables, lambda j: j // n_rot, MXU_DTYPE, f"{tag}_rope")
    q, k, v = _to_heads(hb[:, :nq]), _to_heads(hb[:, nq:nq + nkv]), _to_heads(hb[:, nq + nkv:])
    o, lse = _band_fwd(q, k, v, sinks, SWA_WINDOW - 1, f"{tag}_att")
    o, ob = _merge_fwd([o], [lse], f"{tag}_cast")
    ob = _from_heads(ob)
    return _mm(ob, w_out, "nn", F32, f"{tag}_proj"), dict(q=q, k=k, v=v, o=o, lse=lse, ob=ob, n_rot=n_rot)


def _swa_backward(dmix, saved, xb, w_in, sinks, w_out, tables, tag):
    dw_out = _mm(saved["ob"], dmix, "tn", F32, f"{tag}_dw_out")
    do = _to_heads(_mm(dmix, w_out, "nt", MXU_DTYPE, f"{tag}_do"))
    (dog,), (delta,) = _merge_bwd(do, saved["o"], [saved["lse"]], f"{tag}_delta")
    dq, dk, dv, dsink = _band_bwd(saved["q"], saved["k"], saved["v"], dog, saved["lse"], delta, sinks, SWA_WINDOW - 1, f"{tag}_att_bwd")
    dh = jnp.concatenate([_from_heads(t) for t in (dq, dk, dv)], axis=1)
    n_rot = saved["n_rot"]
    dhb = _rope(dh, tables[0], -tables[1], lambda j: j // n_rot, MXU_DTYPE, f"{tag}_rope_bwd")
    dw_in = _mm(xb, dhb, "tn", F32, f"{tag}_dw_in")
    return _mm(dhb, w_in, "nt", F32, f"{tag}_dx"), dw_in, dw_out, dsink[:, 0, 0]


def _dil_forward(xb, w_in, w_out, tables, tag):
    nw = w_out.shape[0]
    blocks = nw // 128
    h = _mm(xb, w_in, "nn", F32, f"{tag}_qkv")
    hb = _rope(h, *tables, lambda j: (j % (3 * blocks)) // (2 * blocks), MXU_DTYPE, f"{tag}_rope")
    qkv, outs, lses = [], [], []
    for gi, (win, dil) in enumerate(DIL_GROUPS):
        base = gi * 3 * nw
        q, k, v = (_to_heads(hb[:, base + j * nw:base + (j + 1) * nw], dil) for j in range(3))
        o, lse = _band_fwd(q, k, v, None, win // dil, f"{tag}_att{gi}")
        qkv.append((q, k, v))
        outs.append(_unstride(o, dil))
        lses.append(_unstride(lse, dil))
    o, ob = _merge_fwd(outs, lses, f"{tag}_merge")
    ob = _from_heads(ob)
    return _mm(ob, w_out, "nn", F32, f"{tag}_proj"), dict(qkv=qkv, o=o, lses=lses, ob=ob, blocks=blocks)


def _dil_backward(dmix, saved, xb, w_in, w_out, tables, tag):
    dw_out = _mm(saved["ob"], dmix, "tn", F32, f"{tag}_dw_out")
    do = _to_heads(_mm(dmix, w_out, "nt", MXU_DTYPE, f"{tag}_do"))
    dogs, deltas = _merge_bwd(do, saved["o"], saved["lses"], f"{tag}_merge_bwd")
    parts = []
    for gi, (win, dil) in enumerate(DIL_GROUPS):
        q, k, v = saved["qkv"][gi]
        dq, dk, dv, _ = _band_bwd(q, k, v, _restride(dogs[gi], dil), _restride(saved["lses"][gi], dil), _restride(deltas[gi], dil), None, win // dil, f"{tag}_att{gi}_bwd")
        parts += [_from_heads(t, dil) for t in (dq, dk, dv)]
    dh = jnp.concatenate(parts, axis=1)
    blocks = saved["blocks"]
    dhb = _rope(dh, tables[0], -tables[1], lambda j: (j % (3 * blocks)) // (2 * blocks), MXU_DTYPE, f"{tag}_rope_bwd")
    dw_in = _mm(xb, dhb, "tn", F32, f"{tag}_dw_in")
    return _mm(dhb, w_in, "nt", F32, f"{tag}_dx"), dw_in, dw_out


def kernel(x, p, ffn1_w_in, ffn1_w_out, ffn2_w_in, ffn2_w_out, ln_g, ln_b, sb_w_in, sb_w_out, swa_w_in, swa_sinks, swa_w_out, dil_w_in, dil_w_out, ple_w_proj, ple_w_gate, loss_target, m_ffn1_w_in, m_ffn1_w_out, m_ffn2_w_in, m_ffn2_w_out, m_ln_g, m_ln_b, m_sb_w_in, m_sb_w_out, m_swa_w_in, m_swa_sinks, m_swa_w_out, m_dil_w_in, m_dil_w_out, m_ple_w_proj, m_ple_w_gate, v_ffn1_w_in, v_ffn1_w_out, v_ffn2_w_in, v_ffn2_w_out, v_ln_g, v_ln_b, v_sb_w_in, v_sb_w_out, v_swa_w_in, v_swa_sinks, v_swa_w_out, v_dil_w_in, v_dil_w_out, v_ple_w_proj, v_ple_w_gate):
    shard = dict(ffn1_w_in=ffn1_w_in, ffn1_w_out=ffn1_w_out, ffn2_w_in=ffn2_w_in, ffn2_w_out=ffn2_w_out, ln_g=ln_g, ln_b=ln_b, sb_w_in=sb_w_in, sb_w_out=sb_w_out, swa_w_in=swa_w_in, swa_sinks=swa_sinks, swa_w_out=swa_w_out, dil_w_in=dil_w_in, dil_w_out=dil_w_out, ple_w_proj=ple_w_proj, ple_w_gate=ple_w_gate)
    mom_m = dict(ffn1_w_in=m_ffn1_w_in, ffn1_w_out=m_ffn1_w_out, ffn2_w_in=m_ffn2_w_in, ffn2_w_out=m_ffn2_w_out, ln_g=m_ln_g, ln_b=m_ln_b, sb_w_in=m_sb_w_in, sb_w_out=m_sb_w_out, swa_w_in=m_swa_w_in, swa_sinks=m_swa_sinks, swa_w_out=m_swa_w_out, dil_w_in=m_dil_w_in, dil_w_out=m_dil_w_out, ple_w_proj=m_ple_w_proj, ple_w_gate=m_ple_w_gate)
    mom_v = dict(ffn1_w_in=v_ffn1_w_in, ffn1_w_out=v_ffn1_w_out, ffn2_w_in=v_ffn2_w_in, ffn2_w_out=v_ffn2_w_out, ln_g=v_ln_g, ln_b=v_ln_b, sb_w_in=v_sb_w_in, sb_w_out=v_sb_w_out, swa_w_in=v_swa_w_in, swa_sinks=v_swa_sinks, swa_w_out=v_swa_w_out, dil_w_in=v_dil_w_in, dil_w_out=v_dil_w_out, ple_w_proj=v_ple_w_proj, ple_w_gate=v_ple_w_gate)
    depth = ffn1_w_in.shape[0]
    alpha = (2 * depth) ** 0.25
    c = lax.axis_index("c")

    mats = [n for n in WEIGHTS if n not in SMALL]
    got = _gather_chips([shard[n].astype(MXU_DTYPE) for n in mats] + [_pack_small([ln_g, ln_b], ())[None]], "gather_weights")
    full = {"swa_sinks": swa_sinks}
    for n, g in zip(mats, got):
        full[n] = _full_from_shards(g, SHARD_AXIS[n])
    for n, g in zip(("ln_g", "ln_b"), _unpack_small(got[-1][:, 0], [ln_g.shape, ln_b.shape], (N_CHIPS,))):
        full[n] = _full_from_shards(g, SHARD_AXIS[n])

    seq = x.shape[1]
    tables = _rope_tables(seq)
    xf = x[0]
    xb = xf.astype(MXU_DTYPE)
    saved = []
    for i in range(depth):
        kind, j = i % 3, i // 3
        sv = {}
        (x1, x1b), sv["ffn1"] = _ffn_forward(xf, xb, full["ffn1_w_in"][i], full["ffn1_w_out"][i], full["ln_g"][i, 0], full["ln_b"][i, 0], alpha, f"l{i}_ffn1")
        if kind == 0:
            mix, sv["mix"] = _sb_forward(x1b, full["sb_w_in"][j], full["sb_w_out"][j], f"l{i}_sb")
        elif kind == 1:
            mix, sv["mix"] = _swa_forward(x1b, full["swa_w_in"][j], swa_sinks[j], full["swa_w_out"][j], tables, f"l{i}_swa")
        else:
            mix, sv["mix"] = _dil_forward(x1b, full["dil_w_in"][j], full["dil_w_out"][j], tables, f"l{i}_dil")
        x2, x2b, sv["xhat2"], sv["rstd2"] = _ln_fwd(x1, mix, full["ln_g"][i, 1], full["ln_b"][i, 1], alpha, 1.0, f"l{i}_mix_ln")
        sv["x1b"] = x1b
        (x3, x3b), sv["ffn2"] = _ffn_forward(x2, x2b, full["ffn2_w_in"][i], full["ffn2_w_out"][i], full["ln_g"][i, 2], full["ln_b"][i, 2], alpha, f"l{i}_ffn2")
        xf, xb, sv["u"], sv["e"] = _ple_fwd(x3, x3b, p[i, 0], full["ple_w_gate"][i], full["ple_w_proj"][i], f"l{i}_ple")
        sv["x3b"] = x3b
        saved.append(sv)

    loss_part, dy = _loss(xf, loss_target[0], "loss")
    loss = lax.psum(loss_part[0, 0], ("x", "y", "c"))

    grads = {n: [None] * full[n].shape[0] for n in WEIGHTS if n not in ("ln_g", "ln_b")}
    gsum = {n: [None] * full[n].shape[0] for n in mats}
    dln_g = [[None] * 3 for _ in range(depth)]
    dln_b = [[None] * 3 for _ in range(depth)]
    dout = (dy, None, 1.0)
    for i in reversed(range(depth)):
        kind, j = i % 3, i // 3
        mixer = ("sb", "swa", "dil")[kind]
        sv = saved[i]
        dx4, dub, deb = _ple_bwd(*dout, sv["u"], sv["e"], f"l{i}_ple_bwd")
        grads["ple_w_gate"][i] = _mm(sv["x3b"], dub, "tn", F32, f"l{i}_ple_dw_gate")
        grads["ple_w_proj"][i] = _mm(p[i, 0], deb, "tn", F32, f"l{i}_ple_dw_proj")
        dxb = _mm(dub, full["ple_w_gate"][i], "nt", F32, f"l{i}_ple_dx")
        dr, dxb, grads["ffn2_w_in"][i], grads["ffn2_w_out"][i], dln_g[i][2], dln_b[i][2] = _ffn_backward((dx4, dxb, 1.0), sv["ffn2"], full["ffn2_w_in"][i], full["ffn2_w_out"][i], full["ln_g"][i, 2], f"l{i}_ffn2")
        dr, dmix, dln_g[i][1], dln_b[i][1] = _ln_bwd(dr, dxb, alpha, sv["xhat2"], sv["rstd2"], full["ln_g"][i, 1], 1.0, f"l{i}_mix_ln_bwd")
        if kind == 0:
            dxb, grads["sb_w_in"][j], grads["sb_w_out"][j] = _sb_backward(dmix, sv["mix"], sv["x1b"], full["sb_w_in"][j], full["sb_w_out"][j], f"l{i}_sb")
        elif kind == 1:
            dxb, grads["swa_w_in"][j], grads["swa_w_out"][j], grads["swa_sinks"][j] = _swa_backward(dmix, sv["mix"], sv["x1b"], full["swa_w_in"][j], swa_sinks[j], full["swa_w_out"][j], tables, f"l{i}_swa")
        else:
            dxb, grads["dil_w_in"][j], grads["dil_w_out"][j] = _dil_backward(dmix, sv["mix"], sv["x1b"], full["dil_w_in"][j], full["dil_w_out"][j], tables, f"l{i}_dil")
        dr, dxb, grads["ffn1_w_in"][i], grads["ffn1_w_out"][i], dln_g[i][0], dln_b[i][0] = _ffn_backward((dr, dxb, alpha), sv["ffn1"], full["ffn1_w_in"][i], full["ffn1_w_out"][i], full["ln_g"][i, 0], f"l{i}_ffn1")
        dout = (dr, dxb, alpha)
        layer = [("ffn1_w_in", i), ("ffn1_w_out", i), (f"{mixer}_w_in", j), (f"{mixer}_w_out", j), ("ffn2_w_in", i), ("ffn2_w_out", i), ("ple_w_proj", i), ("ple_w_gate", i)]
        summed = _reduce_group([_shards_from_full(grads[n][k], SHARD_AXIS[n] - 1) for n, k in layer], c, f"l{i}_reduce")
        for (n, k), g in zip(layer, summed):
            gsum[n][k] = g
    grad_x = _axpy(*dout, "grad_x")[None]

    gshard = {n: jnp.stack(g) for n, g in gsum.items()}
    small = [
        _shards_from_full(jnp.stack([jnp.concatenate(r, axis=0) for r in dln_g]), SHARD_AXIS["ln_g"]),
        _shards_from_full(jnp.stack([jnp.concatenate(r, axis=0) for r in dln_b]), SHARD_AXIS["ln_b"]),
        jnp.broadcast_to(jnp.stack(grads["swa_sinks"])[None], (N_CHIPS,) + swa_sinks.shape),
    ]
    (small_sum,) = _reduce_group([_pack_small(small, (N_CHIPS,))], c, "small_reduce")
    for n, g in zip(SMALL, _unpack_small(small_sum, [shard[n].shape for n in SMALL], ())):
        gshard[n] = g

    delta, new_m, new_v = {}, {}, {}
    for n in WEIGHTS:
        shp = shard[n].shape
        two_d = (-1, shp[-1])
        d, nm, nv = _adamw(shard[n].reshape(two_d), gshard[n].reshape(two_d), mom_m[n].reshape(two_d), mom_v[n].reshape(two_d), f"adamw_{n}")
        delta[n], new_m[n], new_v[n] = d.reshape(shp), nm.reshape(shp), nv.reshape(shp)

    return (loss, grad_x, *[gshard[n] for n in WEIGHTS], *[delta[n] for n in WEIGHTS], *[new_m[n] for n in WEIGHTS], *[new_v[n] for n in WEIGHTS])
```

```python
import functools
import math

import jax
import jax.numpy as jnp
from jax import lax
from jax.experimental import pallas as pl
from jax.experimental.pallas import tpu as pltpu

F32 = jnp.float32
MXU_DTYPE = jnp.bfloat16
MESH = pl.DeviceIdType.MESH

HEAD_DIM = 64
ATT_BLK = 128
SWA_WINDOW = 128
DIL_GROUPS = ((128, 1), (512, 4), (2048, 16))
LN_EPS = 1e-5
ROPE_THETA = 10000.0
NEG_INF = -1e30
ADAM_LR, ADAM_B1, ADAM_B2, ADAM_EPS, ADAM_WD, ADAM_STEP = 0.001, 0.9, 0.999, 1e-08, 0.01, 10

VMEM_LIMIT_BYTES = 56 * 1024 * 1024
N_CHIPS = 4


def _params(sem=None):
    return pltpu.CompilerParams(dimension_semantics=sem, vmem_limit_bytes=VMEM_LIMIT_BYTES)


def _tile(n, target, unit):
    t = (min(target, n) // unit) * unit
    while t >= unit:
        if n % t == 0:
            return t
        t -= unit
    return n


def _dot(a, b, dims):
    return lax.dot_general(a.astype(MXU_DTYPE), b.astype(MXU_DTYPE), (dims, ((), ())), preferred_element_type=F32)


NN = ((1,), (0,))
NT = ((1,), (1,))
TN = ((0,), (0,))


def _mm(a, b, mode, out_dtype, name, split_a=False, split_b=False, tm=1024, tn=1408, tk=1408):
    dims = {"nn": NN, "nt": NT, "tn": TN}[mode]
    if split_a:
        m, k = a.shape[1], 2 * a.shape[2]
    elif mode == "tn":
        k, m = a.shape
    else:
        m, k = a.shape
    if split_b:
        n = 2 * b.shape[2]
    elif mode == "nt":
        n = b.shape[0]
    else:
        n = b.shape[1]
    tm = _tile(m, tm, 128)
    tn = _tile(n // 2 if split_b else n, tn, 128)
    tk = _tile(k // 2 if split_a else k, tk, 128)
    nk = k // tk
    nk_half = nk // 2
    nn_half = (n // tn) // 2

    if split_a:
        a_spec = pl.BlockSpec((None, tm, tk), lambda i, j, kk: (kk // nk_half, i, kk % nk_half))
    elif mode == "tn":
        a_spec = pl.BlockSpec((tk, tm), lambda i, j, kk: (kk, i))
    else:
        a_spec = pl.BlockSpec((tm, tk), lambda i, j, kk: (i, kk))
    if split_b:
        b_spec = pl.BlockSpec((None, tk, tn), lambda i, j, kk: (j // nn_half, kk, j % nn_half))
    elif mode == "nt":
        b_spec = pl.BlockSpec((tn, tk), lambda i, j, kk: (j, kk))
    else:
        b_spec = pl.BlockSpec((tk, tn), lambda i, j, kk: (kk, j))

    def body(a_ref, b_ref, o_ref, acc_ref):
        kk = pl.program_id(2)

        @pl.when(kk == 0)
        def _():
            acc_ref[...] = jnp.zeros_like(acc_ref)

        acc_ref[...] += _dot(a_ref[...], b_ref[...], dims)

        @pl.when(kk == nk - 1)
        def _():
            o_ref[...] = acc_ref[...].astype(o_ref.dtype)

    return pl.pallas_call(
        body,
        name=name,
        grid=(m // tm, n // tn, nk),
        in_specs=[a_spec, b_spec],
        out_specs=pl.BlockSpec((tm, tn), lambda i, j, kk: (i, j)),
        out_shape=jax.ShapeDtypeStruct((m, n), out_dtype),
        scratch_shapes=[pltpu.VMEM((tm, tn), F32)],
        compiler_params=_params(("parallel", "parallel", "arbitrary")),
    )(a, b)


def _sigmoid(x):
    return 1.0 / (1.0 + jnp.exp(-x))


def _ffn_in(xb, w_in, name):
    s, d = xb.shape
    f = w_in.shape[1] // 2
    tm = _tile(s, 1024, 128)
    tn = _tile(f, 256, 128)
    nj = f // tn

    def body(x_ref, wg_ref, wu_ref, g_ref, u_ref, a_ref):
        x = x_ref[...]
        g = _dot(x, wg_ref[...], NN)
        u = _dot(x, wu_ref[...], NN)
        g_ref[...] = g
        u_ref[...] = u
        a_ref[...] = (g * _sigmoid(g) * u).astype(a_ref.dtype)

    out = pl.BlockSpec((tm, tn), lambda i, j: (i, j))
    return pl.pallas_call(
        body,
        name=name,
        grid=(s // tm, nj),
        in_specs=[
            pl.BlockSpec((tm, d), lambda i, j: (i, 0)),
            pl.BlockSpec((d, tn), lambda i, j: (0, j)),
            pl.BlockSpec((d, tn), lambda i, j: (0, j + nj)),
        ],
        out_specs=[out, out, out],
        out_shape=[
            jax.ShapeDtypeStruct((s, f), F32),
            jax.ShapeDtypeStruct((s, f), F32),
            jax.ShapeDtypeStruct((s, f), MXU_DTYPE),
        ],
        compiler_params=_params(("parallel", "parallel")),
    )(xb, w_in, w_in)


def _ffn_dact(dyb, w_out, gate, up, name):
    s, d = dyb.shape
    f = w_out.shape[0]
    tm = _tile(s, 1024, 128)
    tn = _tile(f, 256, 128)

    def body(dy_ref, w_ref, g_ref, u_ref, o_ref):
        dact = _dot(dy_ref[...], w_ref[...], NT)
        g = g_ref[...]
        sig = _sigmoid(g)
        o_ref[0] = (dact * u_ref[...] * (sig * (1.0 + g * (1.0 - sig)))).astype(o_ref.dtype)
        o_ref[1] = (dact * (g * sig)).astype(o_ref.dtype)

    tile = pl.BlockSpec((tm, tn), lambda i, j: (i, j))
    return pl.pallas_call(
        body,
        name=name,
        grid=(s // tm, f // tn),
        in_specs=[
            pl.BlockSpec((tm, d), lambda i, j: (i, 0)),
            pl.BlockSpec((tn, d), lambda i, j: (j, 0)),
            tile,
            tile,
        ],
        out_specs=pl.BlockSpec((2, tm, tn), lambda i, j: (0, i, j)),
        out_shape=jax.ShapeDtypeStruct((2, s, f), MXU_DTYPE),
        compiler_params=_params(("parallel", "parallel")),
    )(dyb, w_out, gate, up)


def _ple_fwd(x, xb, p, w_gate, w_proj, name):
    s, d = x.shape
    pd = p.shape[1]
    tm = _tile(s, 1024, 128)
    tn = _tile(d, 512, 128)

    def body(x_ref, xb_ref, p_ref, wg_ref, wp_ref, o_ref, ob_ref, u_ref, e_ref):
        u = _dot(xb_ref[...], wg_ref[...], NN)
        e = _dot(p_ref[...], wp_ref[...], NN)
        out = x_ref[...] + _sigmoid(u) * e
        o_ref[...] = out
        ob_ref[...] = out.astype(ob_ref.dtype)
        u_ref[...] = u
        e_ref[...] = e

    tile = pl.BlockSpec((tm, tn), lambda i, j: (i, j))
    return pl.pallas_call(
        body,
        name=name,
        grid=(s // tm, d // tn),
        in_specs=[
            tile,
            pl.BlockSpec((tm, d), lambda i, j: (i, 0)),
            pl.BlockSpec((tm, pd), lambda i, j: (i, 0)),
            pl.BlockSpec((d, tn), lambda i, j: (0, j)),
            pl.BlockSpec((pd, tn), lambda i, j: (0, j)),
        ],
        out_specs=[tile, tile, tile, tile],
        out_shape=[
            jax.ShapeDtypeStruct((s, d), F32),
            jax.ShapeDtypeStruct((s, d), MXU_DTYPE),
            jax.ShapeDtypeStruct((s, d), F32),
            jax.ShapeDtypeStruct((s, d), F32),
        ],
        compiler_params=_params(("parallel", "parallel")),
    )(x, xb, p, w_gate, w_proj)


def _rows_spec(ts, d):
    return pl.BlockSpec((ts, d), lambda i: (i, 0))


def _ln_fwd(x, y, g, b, alpha, beta, name):
    s, d = x.shape
    ts = _tile(s, 512, 8)

    def body(x_ref, y_ref, g_ref, b_ref, o_ref, ob_ref, xh_ref, rs_ref):
        r = alpha * x_ref[...] + beta * y_ref[...]
        mu = jnp.mean(r, axis=1, keepdims=True)
        cen = r - mu
        var = jnp.mean(cen * cen, axis=1, keepdims=True)
        rstd = lax.rsqrt(var + LN_EPS)
        xhat = cen * rstd
        out = xhat * g_ref[...] + b_ref[...]
        o_ref[...] = out
        ob_ref[...] = out.astype(ob_ref.dtype)
        xh_ref[...] = xhat
        rs_ref[...] = rstd

    vec = pl.BlockSpec((1, d), lambda i: (0, 0))
    return pl.pallas_call(
        body,
        name=name,
        grid=(s // ts,),
        in_specs=[_rows_spec(ts, d), _rows_spec(ts, d), vec, vec],
        out_specs=[_rows_spec(ts, d), _rows_spec(ts, d), _rows_spec(ts, d), _rows_spec(ts, 1)],
        out_shape=[
            jax.ShapeDtypeStruct((s, d), F32),
            jax.ShapeDtypeStruct((s, d), MXU_DTYPE),
            jax.ShapeDtypeStruct((s, d), F32),
            jax.ShapeDtypeStruct((s, 1), F32),
        ],
        compiler_params=_params(("parallel",)),
    )(x, y, g.reshape(1, d), b.reshape(1, d))


def _ln_bwd(ga, gb, ca, xhat, rstd, g, beta, name):
    s, d = xhat.shape
    ts = _tile(s, 512, 8)

    def body(ga_ref, gb_ref, xh_ref, rs_ref, g_ref, dr_ref, dyb_ref, dg_ref, db_ref):
        @pl.when(pl.program_id(0) == 0)
        def _():
            dg_ref[...] = jnp.zeros_like(dg_ref)
            db_ref[...] = jnp.zeros_like(db_ref)

        dout = ca * ga_ref[...] + gb_ref[...]
        xhat = xh_ref[...]
        dg_ref[...] += jnp.sum(dout * xhat, axis=0, keepdims=True)
        db_ref[...] += jnp.sum(dout, axis=0, keepdims=True)
        dxh = dout * g_ref[...]
        m1 = jnp.mean(dxh, axis=1, keepdims=True)
        m2 = jnp.mean(dxh * xhat, axis=1, keepdims=True)
        dr = rs_ref[...] * (dxh - m1 - xhat * m2)
        dr_ref[...] = dr
        dyb_ref[...] = (beta * dr).astype(dyb_ref.dtype)

    vec = pl.BlockSpec((1, d), lambda i: (0, 0))
    return pl.pallas_call(
        body,
        name=name,
        grid=(s // ts,),
        in_specs=[_rows_spec(ts, d), _rows_spec(ts, d), _rows_spec(ts, d), _rows_spec(ts, 1), vec],
        out_specs=[_rows_spec(ts, d), _rows_spec(ts, d), vec, vec],
        out_shape=[
            jax.ShapeDtypeStruct((s, d), F32),
            jax.ShapeDtypeStruct((s, d), MXU_DTYPE),
            jax.ShapeDtypeStruct((1, d), F32),
            jax.ShapeDtypeStruct((1, d), F32),
        ],
        compiler_params=_params(("arbitrary",)),
    )(ga, gb, xhat, rstd, g.reshape(1, d))


def _ple_bwd(ga, gb, ca, u, e, name):
    s, d = u.shape
    ts = _tile(s, 512, 8)
    grads = [ga] if gb is None else [ga, gb]

    def body(*refs):
        u_ref, e_ref, dx_ref, du_ref, de_ref = refs[len(grads):]
        dx = ca * refs[0][...]
        if gb is not None:
            dx = dx + refs[1][...]
        sig = _sigmoid(u_ref[...])
        dx_ref[...] = dx
        du_ref[...] = (dx * e_ref[...] * sig * (1.0 - sig)).astype(du_ref.dtype)
        de_ref[...] = (dx * sig).astype(de_ref.dtype)

    return pl.pallas_call(
        body,
        name=name,
        grid=(s // ts,),
        in_specs=[_rows_spec(ts, d)] * (len(grads) + 2),
        out_specs=[_rows_spec(ts, d)] * 3,
        out_shape=[
            jax.ShapeDtypeStruct((s, d), F32),
            jax.ShapeDtypeStruct((s, d), MXU_DTYPE),
            jax.ShapeDtypeStruct((s, d), MXU_DTYPE),
        ],
        compiler_params=_params(("parallel",)),
    )(*grads, u, e)


def _axpy(ga, gb, ca, name):
    s, d = ga.shape
    ts = _tile(s, 512, 8)

    def body(ga_ref, gb_ref, o_ref):
        o_ref[...] = ca * ga_ref[...] + gb_ref[...]

    return pl.pallas_call(
        body,
        name=name,
        grid=(s // ts,),
        in_specs=[_rows_spec(ts, d)] * 2,
        out_specs=_rows_spec(ts, d),
        out_shape=jax.ShapeDtypeStruct((s, d), F32),
        compiler_params=_params(("parallel",)),
    )(ga, gb)


def _loss(y, target, name):
    s, d = y.shape
    ts = _tile(s, 512, 8)

    def body(y_ref, t_ref, l_ref, dy_ref):
        @pl.when(pl.program_id(0) == 0)
        def _():
            l_ref[...] = jnp.zeros_like(l_ref)

        err = y_ref[...] - t_ref[...]
        l_ref[...] += (0.5 / d) * jnp.sum(jnp.sum(err * err, axis=1, keepdims=True), axis=0, keepdims=True)
        dy_ref[...] = err * (1.0 / d)

    return pl.pallas_call(
        body,
        name=name,
        grid=(s // ts,),
        in_specs=[_rows_spec(ts, d)] * 2,
        out_specs=[pl.BlockSpec((1, 1), lambda i: (0, 0)), _rows_spec(ts, d)],
        out_shape=[jax.ShapeDtypeStruct((1, 1), F32), jax.ShapeDtypeStruct((s, d), F32)],
        compiler_params=_params(("arbitrary",)),
    )(y, target)


def _rope_tables(seq):
    pos = jnp.arange(seq, dtype=F32)
    inv = ROPE_THETA ** (-jnp.arange(0, HEAD_DIM, 2, dtype=F32) / HEAD_DIM)
    ang = pos[:, None] * inv[None, :]
    cos, sin = jnp.cos(ang), jnp.sin(ang)
    cos2 = jnp.concatenate([cos, cos, cos, cos, jnp.ones((seq, 128), F32)], axis=1)
    sin2 = jnp.concatenate([-sin, sin, -sin, sin, jnp.zeros((seq, 128), F32)], axis=1)
    return cos2, sin2


def _rope(h, cos2, sin2, plain_block, out_dtype, name):
    s, n = h.shape
    ts = _tile(s, 512, 8)

    def body(h_ref, c_ref, s_ref, o_ref):
        x = h_ref[...].astype(F32)
        lane = lax.broadcasted_iota(jnp.int32, x.shape, 1)
        partner = jnp.where(lane % HEAD_DIM < HEAD_DIM // 2, pltpu.roll(x, 128 - HEAD_DIM // 2, 1), pltpu.roll(x, HEAD_DIM // 2, 1))
        o_ref[...] = (x * c_ref[...] + partner * s_ref[...]).astype(o_ref.dtype)

    tile = pl.BlockSpec((ts, 128), lambda i, j: (i, j))
    table = pl.BlockSpec((ts, 128), lambda i, j: (i, plain_block(j)))
    return pl.pallas_call(
        body,
        name=name,
        grid=(s // ts, n // 128),
        in_specs=[tile, table, table],
        out_specs=tile,
        out_shape=jax.ShapeDtypeStruct((s, n), out_dtype),
        compiler_params=_params(("parallel", "parallel")),
    )(h, cos2, sin2)


SB_TQ = 512
SB_BLK = 128


def _tri2(strict):
    row = lax.broadcasted_iota(jnp.int32, (2 * SB_BLK, SB_BLK), 0) % SB_BLK
    col = lax.broadcasted_iota(jnp.int32, (2 * SB_BLK, SB_BLK), 1)
    return (row > col if strict else row >= col).astype(MXU_DTYPE)


def _cumsum_dot(x, tri2):
    hi = x.astype(MXU_DTYPE)
    lo = x - hi.astype(F32)
    return _dot(jnp.concatenate([hi, lo.astype(MXU_DTYPE)], axis=1), tri2, NN)


def _sb_logits(z, valid):
    l1p = jnp.log(1.0 + jnp.exp(-jnp.abs(z)))
    sp = jnp.maximum(z, 0.0) + l1p
    if valid is not None:
        sp = jnp.where(valid, sp, 0.0)
    return sp, jnp.minimum(z, 0.0) - l1p


def _sb_weights(ls, after, valid):
    a = jnp.exp(ls - after)
    return a if valid is None else jnp.where(valid, a, 0.0)


def _sb_setup(q_ref, k_ref, v_ref, n_sub, scale):
    qs = [q_ref[0, u * SB_BLK:(u + 1) * SB_BLK, :] * scale for u in range(n_sub)]
    row = lax.broadcasted_iota(jnp.int32, (SB_BLK, SB_BLK), 0)
    col = lax.broadcasted_iota(jnp.int32, (SB_BLK, SB_BLK), 1)

    def load(jj):
        start = pl.multiple_of(jj * SB_BLK, SB_BLK)
        return start, k_ref[0, pl.ds(start, SB_BLK), :], v_ref[0, pl.ds(start, SB_BLK), :]

    return qs, col < row, load


def _sb_fwd(q, k, v, name):
    nh, s, dh = q.shape
    tq = min(SB_TQ, s)
    n_sub = tq // SB_BLK
    scale = dh ** -0.5

    def body(q_ref, k_ref, v_ref, o_ref, ob_ref):
        base = pl.program_id(1) * n_sub
        qs, diag_valid, load = _sb_setup(q_ref, k_ref, v_ref, n_sub, scale)
        tri_after = _tri2(True)

        def key_block(jj, carry, first_sub, diagonal):
            _, kb, vb = load(jj)
            subs = range(first_sub, n_sub)
            valid = {u: diag_valid if diagonal and u == first_sub else None for u in subs}
            z = {u: _dot(qs[u], kb, NT) for u in subs}
            sp_ls = {u: _sb_logits(z[u], valid[u]) for u in subs}
            inside = {u: _cumsum_dot(sp_ls[u][0], tri_after) for u in subs}
            carry = list(carry)
            for u in subs:
                after_c, acc = carry[u]
                sp, ls = sp_ls[u]
                a = _sb_weights(ls, after_c + inside[u], valid[u])
                carry[u] = (after_c + jnp.sum(sp, axis=1, keepdims=True), acc + _dot(a, vb, NN))
            return tuple(carry)

        carry = tuple((jnp.zeros((SB_BLK, 1), F32), jnp.zeros((SB_BLK, dh), F32)) for _ in range(n_sub))
        for d in range(n_sub - 1, -1, -1):
            carry = key_block(base + d, carry, d, True)
        carry = lax.fori_loop(0, base, lambda t, c: key_block(base - 1 - t, c, 0, False), carry)
        for u in range(n_sub):
            rows = slice(u * SB_BLK, (u + 1) * SB_BLK)
            o_ref[0, rows, :] = carry[u][1]
            ob_ref[0, rows, :] = carry[u][1].astype(ob_ref.dtype)

    blk = pl.BlockSpec((1, tq, dh), lambda h, i: (h, i, 0))
    full = pl.BlockSpec((1, s, dh), lambda h, i: (h, 0, 0))
    return pl.pallas_call(
        body,
        name=name,
        grid=(nh, s // tq),
        in_specs=[blk, full, full],
        out_specs=[blk, blk],
        out_shape=[jax.ShapeDtypeStruct((nh, s, dh), F32), jax.ShapeDtypeStruct((nh, s, dh), MXU_DTYPE)],
        compiler_params=_params(("parallel", "parallel")),
    )(q, k, v)


def _sb_bwd(q, k, v, o, do, name):
    nh, s, dh = q.shape
    tq = min(SB_TQ, s)
    n_sub = tq // SB_BLK
    scale = dh ** -0.5

    def body(q_ref, k_ref, v_ref, o_ref, do_ref, dq_ref, dk_ref, dv_ref):
        i = pl.program_id(1)
        base = i * n_sub

        @pl.when(i == 0)
        def _():
            dk_ref[...] = jnp.zeros_like(dk_ref)
            dv_ref[...] = jnp.zeros_like(dv_ref)

        qs, diag_valid, load = _sb_setup(q_ref, k_ref, v_ref, n_sub, scale)
        dob = [do_ref[0, u * SB_BLK:(u + 1) * SB_BLK, :] for u in range(n_sub)]
        total = [jnp.sum(dob[u].astype(F32) * o_ref[0, u * SB_BLK:(u + 1) * SB_BLK, :], axis=1, keepdims=True) for u in range(n_sub)]
        tri_after = _tri2(True)
        tri_from = _tri2(False)

        def key_block(jj, carry, first_sub, diagonal):
            start, kb, vb = load(jj)
            subs = range(first_sub, n_sub)
            valid = {u: diag_valid if diagonal and u == first_sub else None for u in subs}
            z = {u: _dot(qs[u], kb, NT) for u in subs}
            da = {u: _dot(dob[u], vb, NT) for u in subs}
            sp_ls = {u: _sb_logits(z[u], valid[u]) for u in subs}
            inside = {u: _cumsum_dot(sp_ls[u][0], tri_after) for u in subs}
            ab, dl = {}, {}
            for u in subs:
                ab[u] = _sb_weights(sp_ls[u][1], carry[u][0] + inside[u], valid[u]).astype(MXU_DTYPE)
                dl[u] = ab[u].astype(F32) * da[u]
            from_in = {u: _cumsum_dot(dl[u], tri_from) for u in subs}
            carry = list(carry)
            dk = dv = None
            for u in subs:
                after_c, from_c, dq = carry[u]
                sp, ls = sp_ls[u]
                dz = dl[u] - jnp.exp(ls) * (dl[u] + total[u] - (from_c + from_in[u]))
                if valid[u] is not None:
                    dz = jnp.where(valid[u], dz, 0.0)
                dzb = dz.astype(MXU_DTYPE)
                carry[u] = (
                    after_c + jnp.sum(sp, axis=1, keepdims=True),
                    from_c + jnp.sum(dl[u], axis=1, keepdims=True),
                    dq + _dot(dzb, kb, NN),
                )
                dk_u, dv_u = _dot(dzb, qs[u], TN), _dot(ab[u], dob[u], TN)
                dk = dk_u if dk is None else dk + dk_u
                dv = dv_u if dv is None else dv + dv_u
            dk_ref[0, pl.ds(start, SB_BLK), :] += dk
            dv_ref[0, pl.ds(start, SB_BLK), :] += dv
            return tuple(carry)

        carry = tuple((jnp.zeros((SB_BLK, 1), F32), jnp.zeros((SB_BLK, 1), F32), jnp.zeros((SB_BLK, dh), F32)) for _ in range(n_sub))
        for d in range(n_sub - 1, -1, -1):
            carry = key_block(base + d, carry, d, True)
        carry = lax.fori_loop(0, base, lambda t, c: key_block(base - 1 - t, c, 0, False), carry)
        for u in range(n_sub):
            dq_ref[0, u * SB_BLK:(u + 1) * SB_BLK, :] = carry[u][2] * scale

    blk = pl.BlockSpec((1, tq, dh), lambda h, i: (h, i, 0))
    full = pl.BlockSpec((1, s, dh), lambda h, i: (h, 0, 0))
    shape = jax.ShapeDtypeStruct((nh, s, dh), F32)
    return pl.pallas_call(
        body,
        name=name,
        grid=(nh, s // tq),
        in_specs=[blk, full, full, blk, blk],
        out_specs=[blk, full, full],
        out_shape=[shape, shape, shape],
        compiler_params=_params(("parallel", "arbitrary")),
    )(q, k, v, o, do)


BAND_TQ = 512


def _band_scores(q_ref, k_ref, i, sub, tq, length, max_dist, scale):
    t0 = i * tq + sub * ATT_BLK
    ks = pl.multiple_of(jnp.minimum(jnp.maximum(t0 - ATT_BLK, 0), length - 2 * ATT_BLK), ATT_BLK)
    qs = q_ref[0, sub * ATT_BLK:(sub + 1) * ATT_BLK, :] * scale
    kw = k_ref[0, pl.ds(ks, 2 * ATT_BLK), :]
    sc = _dot(qs, kw, NT)
    diff = (t0 + lax.broadcasted_iota(jnp.int32, sc.shape, 0)) - (ks + lax.broadcasted_iota(jnp.int32, sc.shape, 1))
    valid = (diff >= 0) & (diff <= max_dist)
    return ks, qs, kw, jnp.where(valid, sc, NEG_INF)


def _band_fwd(q, k, v, sinks, max_dist, name):
    bq, length, dh = q.shape
    group = bq // k.shape[0]
    tq = min(BAND_TQ, length)
    scale = dh ** -0.5
    n_sink = 0 if sinks is None else sinks.shape[0]

    def body(*refs):
        if n_sink:
            sink_ref, q_ref, k_ref, v_ref, o_ref, lse_ref = refs
            sink = sink_ref[pl.program_id(0) % n_sink]
        else:
            q_ref, k_ref, v_ref, o_ref, lse_ref = refs
        i = pl.program_id(1)
        for sub in range(tq // ATT_BLK):
            ks, _, _, sc = _band_scores(q_ref, k_ref, i, sub, tq, length, max_dist, scale)
            m = jnp.max(sc, axis=1, keepdims=True)
            if n_sink:
                m = jnp.maximum(m, sink)
            e = jnp.exp(sc - m)
            den = jnp.sum(e, axis=1, keepdims=True)
            if n_sink:
                den = den + jnp.exp(sink - m)
            rows = slice(sub * ATT_BLK, (sub + 1) * ATT_BLK)
            o_ref[0, rows, :] = _dot(e / den, v_ref[0, pl.ds(ks, 2 * ATT_BLK), :], NN)
            lse_ref[0, rows, :] = m + jnp.log(den)

    qblk = pl.BlockSpec((1, tq, dh), lambda b, i: (b, i, 0))
    kfull = pl.BlockSpec((1, length, dh), lambda b, i: (b // group, 0, 0))
    in_specs = [qblk, kfull, kfull]
    args = [q, k, v]
    if n_sink:
        in_specs = [pl.BlockSpec(memory_space=pltpu.SMEM)] + in_specs
        args = [sinks] + args
    return pl.pallas_call(
        body,
        name=name,
        grid=(bq, length // tq),
        in_specs=in_specs,
        out_specs=[qblk, pl.BlockSpec((1, tq, 1), lambda b, i: (b, i, 0))],
        out_shape=[jax.ShapeDtypeStruct((bq, length, dh), F32), jax.ShapeDtypeStruct((bq, length, 1), F32)],
        compiler_params=_params(("parallel", "parallel")),
    )(*args)


def _band_bwd(q, k, v, do, lse, delta, sinks, max_dist, name):
    bq, length, dh = q.shape
    group = bq // k.shape[0]
    tq = min(BAND_TQ, length)
    scale = dh ** -0.5
    n_sink = 0 if sinks is None else sinks.shape[0]

    def body(*refs):
        if n_sink:
            sink_ref, q_ref, k_ref, v_ref, do_ref, lse_ref, dl_ref, dq_ref, dk_ref, dv_ref, ds_ref = refs
            sink = sink_ref[pl.program_id(0) % n_sink]
        else:
            q_ref, k_ref, v_ref, do_ref, lse_ref, dl_ref, dq_ref, dk_ref, dv_ref, ds_ref = refs
        i = pl.program_id(1)

        @pl.when((i == 0) & (pl.program_id(0) % group == 0))
        def _():
            dk_ref[...] = jnp.zeros_like(dk_ref)
            dv_ref[...] = jnp.zeros_like(dv_ref)

        @pl.when(i == 0)
        def _():
            ds_ref[...] = jnp.zeros_like(ds_ref)

        for sub in range(tq // ATT_BLK):
            ks, qs, kw, sc = _band_scores(q_ref, k_ref, i, sub, tq, length, max_dist, scale)
            rows = slice(sub * ATT_BLK, (sub + 1) * ATT_BLK)
            lse = lse_ref[0, rows, :]
            delta_r = dl_ref[0, rows, :]
            dob = do_ref[0, rows, :]
            p = jnp.exp(sc - lse)
            dp = _dot(dob, v_ref[0, pl.ds(ks, 2 * ATT_BLK), :], NT)
            dsb = (p * (dp - delta_r)).astype(MXU_DTYPE)
            dq_ref[0, rows, :] = _dot(dsb, kw, NN) * scale
            dk_ref[0, pl.ds(ks, 2 * ATT_BLK), :] += _dot(dsb, qs, TN)
            dv_ref[0, pl.ds(ks, 2 * ATT_BLK), :] += _dot(p, dob, TN)
            if n_sink:
                ds_ref[...] += jnp.sum(-jnp.exp(sink - lse) * delta_r, axis=0, keepdims=True)

    qblk = pl.BlockSpec((1, tq, dh), lambda b, i: (b, i, 0))
    qcol = pl.BlockSpec((1, tq, 1), lambda b, i: (b, i, 0))
    kfull = pl.BlockSpec((1, length, dh), lambda b, i: (b // group, 0, 0))
    in_specs = [qblk, kfull, kfull, qblk, qcol, qcol]
    args = [q, k, v, do, lse, delta]
    if n_sink:
        in_specs = [pl.BlockSpec(memory_space=pltpu.SMEM)] + in_specs
        args = [sinks] + args
    kshape = jax.ShapeDtypeStruct((k.shape[0], length, dh), F32)
    return pl.pallas_call(
        body,
        name=name,
        grid=(bq, length // tq),
        in_specs=in_specs,
        out_specs=[qblk, kfull, kfull, pl.BlockSpec((1, 8, 128), lambda b, i: (b, 0, 0))],
        out_shape=[jax.ShapeDtypeStruct((bq, length, dh), F32), kshape, kshape, jax.ShapeDtypeStruct((bq, 8, 128), F32)],
        compiler_params=_params(("arbitrary", "arbitrary")),
    )(*args)


def _merge_weights(lse_refs):
    lses = [r[0] for r in lse_refs]
    m = functools.reduce(jnp.maximum, lses)
    es = [jnp.exp(l - m) for l in lses]
    den = functools.reduce(lambda a, b: a + b, es)
    return [e / den for e in es]


def _merge_fwd(outs, lses, name):
    n = len(outs)
    nh, s, dh = outs[0].shape
    ts = _tile(s, 1024, 8)

    def body(*refs):
        ws = _merge_weights(refs[n:2 * n])
        o = functools.reduce(lambda a, b: a + b, [w * r[0] for w, r in zip(ws, refs[:n])])
        refs[2 * n][0] = o
        refs[2 * n + 1][0] = o.astype(MXU_DTYPE)

    blk = pl.BlockSpec((1, ts, dh), lambda h, i: (h, i, 0))
    col = pl.BlockSpec((1, ts, 1), lambda h, i: (h, i, 0))
    return pl.pallas_call(
        body,
        name=name,
        grid=(nh, s // ts),
        in_specs=[blk] * n + [col] * n,
        out_specs=[blk, blk],
        out_shape=[jax.ShapeDtypeStruct((nh, s, dh), F32), jax.ShapeDtypeStruct((nh, s, dh), MXU_DTYPE)],
        compiler_params=_params(("parallel", "parallel")),
    )(*outs, *lses)


def _merge_bwd(do, o, lses, name):
    n = len(lses)
    nh, s, dh = o.shape
    ts = _tile(s, 1024, 8)

    def body(*refs):
        do_ref, o_ref = refs[:2]
        ws = _merge_weights(refs[2:2 + n])
        dof = do_ref[0].astype(F32)
        base = jnp.sum(dof * o_ref[0], axis=1, keepdims=True)
        for g in range(n):
            refs[2 + n + g][0] = (ws[g] * dof).astype(MXU_DTYPE)
            refs[2 + 2 * n + g][0] = ws[g] * base

    blk = pl.BlockSpec((1, ts, dh), lambda h, i: (h, i, 0))
    col = pl.BlockSpec((1, ts, 1), lambda h, i: (h, i, 0))
    res = pl.pallas_call(
        body,
        name=name,
        grid=(nh, s // ts),
        in_specs=[blk, blk] + [col] * n,
        out_specs=[blk] * n + [col] * n,
        out_shape=[jax.ShapeDtypeStruct((nh, s, dh), MXU_DTYPE)] * n + [jax.ShapeDtypeStruct((nh, s, 1), F32)] * n,
        compiler_params=_params(("parallel", "parallel")),
    )(do, o, *lses)
    return res[:n], res[n:]


def _to_heads(a, dil=1):
    s, n = a.shape
    nh = n // HEAD_DIM
    a = a.reshape(s // dil, dil, nh, HEAD_DIM).transpose(1, 2, 0, 3)
    return a.reshape(dil * nh, s // dil, HEAD_DIM)


def _from_heads(a, dil=1):
    b, ls, c = a.shape
    nh = b // dil
    return a.reshape(dil, nh, ls, c).transpose(2, 0, 1, 3).reshape(ls * dil, nh * c)


def _restride(a, dil):
    nh, s, c = a.shape
    return a.reshape(nh, s // dil, dil, c).transpose(2, 0, 1, 3).reshape(dil * nh, s // dil, c)


def _unstride(a, dil):
    b, ls, c = a.shape
    nh = b // dil
    return a.reshape(dil, nh, ls, c).transpose(1, 2, 0, 3).reshape(nh, ls * dil, c)


ANY = pl.BlockSpec(memory_space=pl.ANY)


def _position():
    x, y, c = lax.axis_index("x"), lax.axis_index("y"), lax.axis_index("c")
    return x, y, c, [(1 - x, y), (x, 1 - y), (1 - x, 1 - y)]


def _remote(src, dst, send_sems, recv_sems, k, to):
    return pltpu.make_async_remote_copy(src_ref=src, dst_ref=dst, send_sem=send_sems.at[k], recv_sem=recv_sems.at[k], device_id=to, device_id_type=MESH)


def _gather_chips(arrs, name):
    n = len(arrs)

    def body(*refs):
        f_refs, g_refs = refs[:n], refs[n:2 * n]
        send_sems, recv_sems, local_sems = refs[2 * n:]
        x, y, c, chips = _position()
        me, sibling = (x, y, c), (x, y, 1 - c)

        def half(j, ref, sel):
            rows = f_refs[j].shape[1] // 2
            return ref.at[:, pl.ds(sel * rows, rows), :]

        def slot(j, chip, sel):
            return half(j, g_refs[j].at[2 * chip[0] + chip[1]], sel)

        mine = [pltpu.make_async_copy(f_refs[j], g_refs[j].at[2 * x + y], local_sems.at[j]) for j in range(n)]
        first = [_remote(half(j, f_refs[j], c), slot(j, (x, y), c), send_sems, recv_sems, 6 * j + k, (*chip, c)) for j in range(n) for k, chip in enumerate(chips)]
        for cp in mine + first:
            cp.start()
        passed = []
        for j in range(n):
            for k, chip in enumerate(chips):
                _remote(slot(j, chip, c), slot(j, chip, c), send_sems, recv_sems, 6 * j + k, me).wait_recv()
                passed.append(_remote(slot(j, chip, c), slot(j, chip, c), send_sems, recv_sems, 6 * j + 3 + k, sibling))
                passed[-1].start()
        for j in range(n):
            for k, chip in enumerate(chips):
                _remote(slot(j, chip, 1 - c), slot(j, chip, 1 - c), send_sems, recv_sems, 6 * j + 3 + k, me).wait_recv()
        for cp in first + passed:
            cp.wait_send()
        for cp in mine:
            cp.wait()

    return pl.pallas_call(
        body,
        name=name,
        in_specs=[ANY] * n,
        out_specs=[ANY] * n,
        out_shape=[jax.ShapeDtypeStruct((N_CHIPS,) + a.shape, a.dtype) for a in arrs],
        scratch_shapes=[pltpu.SemaphoreType.DMA((6 * n,)), pltpu.SemaphoreType.DMA((6 * n,)), pltpu.SemaphoreType.DMA((n,))],
    )(*arrs)


def _swap_sibling(arrs, name):
    n = len(arrs)

    def body(*refs):
        x, y, c, _ = _position()
        send_sems, recv_sems = refs[2 * n:]
        copies = [_remote(refs[j], refs[n + j], send_sems, recv_sems, j, (x, y, 1 - c)) for j in range(n)]
        for cp in copies:
            cp.start()
        for cp in copies:
            cp.wait()

    return pl.pallas_call(
        body,
        name=name,
        in_specs=[ANY] * n,
        out_specs=[ANY] * n,
        out_shape=[jax.ShapeDtypeStruct(a.shape, a.dtype) for a in arrs],
        scratch_shapes=[pltpu.SemaphoreType.DMA((n,)), pltpu.SemaphoreType.DMA((n,))],
    )(*arrs)


def _exchange_chips(parts, name):
    n = len(parts)

    def body(*refs):
        p_refs, q_refs = refs[:n], refs[n:2 * n]
        send_sems, recv_sems, local_sems = refs[2 * n:]
        x, y, c, chips = _position()
        my_slot = 2 * x + y
        mine = [pltpu.make_async_copy(p_refs[j].at[my_slot], q_refs[j].at[my_slot], local_sems.at[j]) for j in range(n)]
        sends = [_remote(p_refs[j].at[2 * px + py], q_refs[j].at[my_slot], send_sems, recv_sems, 3 * j + k, (px, py, c)) for j in range(n) for k, (px, py) in enumerate(chips)]
        for cp in mine + sends:
            cp.start()
        for j in range(n):
            for k, (px, py) in enumerate(chips):
                landed = q_refs[j].at[2 * px + py]
                _remote(landed, landed, send_sems, recv_sems, 3 * j + k, (x, y, c)).wait_recv()
        for cp in sends:
            cp.wait_send()
        for cp in mine:
            cp.wait()

    return pl.pallas_call(
        body,
        name=name,
        in_specs=[ANY] * n,
        out_specs=[ANY] * n,
        out_shape=[jax.ShapeDtypeStruct(a.shape, a.dtype) for a in parts],
        scratch_shapes=[pltpu.SemaphoreType.DMA((3 * n,)), pltpu.SemaphoreType.DMA((3 * n,)), pltpu.SemaphoreType.DMA((n,))],
    )(*parts)


def _share_sibling(halves, name):
    n = len(halves)

    def body(*refs):
        h_refs, o_refs = refs[:n], refs[n:2 * n]
        send_sems, recv_sems, local_sems = refs[2 * n:]
        x, y, c, _ = _position()

        def rows(j, sel):
            r = h_refs[j].shape[0]
            return o_refs[j].at[pl.ds(sel * r, r), :]

        mine = [pltpu.make_async_copy(h_refs[j], rows(j, c), local_sems.at[j]) for j in range(n)]
        sends = [_remote(h_refs[j], rows(j, c), send_sems, recv_sems, j, (x, y, 1 - c)) for j in range(n)]
        for cp in mine + sends:
            cp.start()
        for j in range(n):
            _remote(rows(j, 1 - c), rows(j, 1 - c), send_sems, recv_sems, j, (x, y, c)).wait_recv()
        for cp in sends:
            cp.wait_send()
        for cp in mine:
            cp.wait()

    return pl.pallas_call(
        body,
        name=name,
        in_specs=[ANY] * n,
        out_specs=[ANY] * n,
        out_shape=[jax.ShapeDtypeStruct((2 * a.shape[0],) + a.shape[1:], a.dtype) for a in halves],
        scratch_shapes=[pltpu.SemaphoreType.DMA((n,)), pltpu.SemaphoreType.DMA((n,)), pltpu.SemaphoreType.DMA((n,))],
    )(*halves)


def _sum_rows(r, c, n_in):
    return _tile(r, max(8, (1 << 20) // (c * (n_in + 1))), 8)


def _add2(a, b, name):
    n, r, c = a.shape
    tr = _sum_rows(r, c, 2)

    def body(a_ref, b_ref, o_ref):
        o_ref[...] = a_ref[...] + b_ref[...]

    blk = pl.BlockSpec((1, tr, c), lambda s, i: (s, i, 0))
    return pl.pallas_call(
        body,
        name=name,
        grid=(n, r // tr),
        in_specs=[blk, blk],
        out_specs=blk,
        out_shape=jax.ShapeDtypeStruct(a.shape, a.dtype),
        compiler_params=_params(("parallel", "parallel")),
    )(a, b)


def _sum_slots(q, name):
    n, r, c = q.shape
    tr = _sum_rows(r, c, n)

    def body(q_ref, o_ref):
        acc = q_ref[0]
        for s in range(1, n):
            acc = acc + q_ref[s]
        o_ref[...] = acc

    return pl.pallas_call(
        body,
        name=name,
        grid=(r // tr,),
        in_specs=[pl.BlockSpec((n, tr, c), lambda i: (0, i, 0))],
        out_specs=pl.BlockSpec((tr, c), lambda i: (i, 0)),
        out_shape=jax.ShapeDtypeStruct((r, c), q.dtype),
        compiler_params=_params(("parallel",)),
    )(q)


def _adamw(w, g, m, v, name):
    r, c = w.shape
    tr = _tile(r, 256, 8)
    c1 = 1.0 - ADAM_B1 ** ADAM_STEP
    c2 = 1.0 - ADAM_B2 ** ADAM_STEP

    def body(w_ref, g_ref, m_ref, v_ref, d_ref, nm_ref, nv_ref):
        g = g_ref[...]
        nm = ADAM_B1 * m_ref[...] + (1.0 - ADAM_B1) * g
        nv = ADAM_B2 * v_ref[...] + (1.0 - ADAM_B2) * (g * g)
        d_ref[...] = -ADAM_LR * ((nm / c1) / (jnp.sqrt(nv / c2) + ADAM_EPS) + ADAM_WD * w_ref[...])
        nm_ref[...] = nm
        nv_ref[...] = nv

    blk = pl.BlockSpec((tr, c), lambda i: (i, 0))
    shape = jax.ShapeDtypeStruct((r, c), F32)
    return pl.pallas_call(
        body,
        name=name,
        grid=(r // tr,),
        in_specs=[blk] * 4,
        out_specs=[blk] * 3,
        out_shape=[shape] * 3,
        compiler_params=_params(("parallel",)),
    )(w, g, m, v)


WEIGHTS = ("ffn1_w_in", "ffn1_w_out", "ffn2_w_in", "ffn2_w_out", "ln_g", "ln_b", "sb_w_in", "sb_w_out", "swa_w_in", "swa_sinks", "swa_w_out", "dil_w_in", "dil_w_out", "ple_w_proj", "ple_w_gate")
SHARD_AXIS = {"ffn1_w_in": 2, "ffn1_w_out": 1, "ffn2_w_in": 2, "ffn2_w_out": 1, "ln_g": 2, "ln_b": 2, "sb_w_in": 2, "sb_w_out": 1, "swa_w_in": 2, "swa_sinks": None, "swa_w_out": 1, "dil_w_in": 2, "dil_w_out": 1, "ple_w_proj": 2, "ple_w_gate": 1}
SMALL = ("ln_g", "ln_b", "swa_sinks")
SMALL_COLS = 128
SMALL_UNIT = 16 * SMALL_COLS


def _pack_small(pieces, lead):
    flat = jnp.concatenate([a.reshape(lead + (-1,)) for a in pieces], axis=-1)
    n = flat.shape[-1]
    flat = jnp.pad(flat, [(0, 0)] * len(lead) + [(0, -n % SMALL_UNIT)])
    return flat.reshape(lead + (-1, SMALL_COLS))


def _unpack_small(buf, shapes, lead):
    flat = buf.reshape(lead + (-1,))
    out, off = [], 0
    for shp in shapes:
        n = math.prod(shp)
        out.append(flat[..., off:off + n].reshape(lead + tuple(shp)))
        off += n
    return out


def _reduce_group(grads, c, tag):
    keep = [lax.dynamic_slice_in_dim(g, c * (g.shape[1] // 2), g.shape[1] // 2, axis=1) for g in grads]
    give = [lax.dynamic_slice_in_dim(g, (1 - c) * (g.shape[1] // 2), g.shape[1] // 2, axis=1) for g in grads]
    got = _swap_sibling(give, f"{tag}_pair")
    part = [_add2(k, g, f"{tag}_pair_sum{j}") for j, (k, g) in enumerate(zip(keep, got))]
    landed = _exchange_chips(part, f"{tag}_chips")
    halves = [_sum_slots(q, f"{tag}_chips_sum{j}") for j, q in enumerate(landed)]
    return _share_sibling(halves, f"{tag}_share")


def _full_from_shards(g, axis):
    g = jnp.moveaxis(g, 0, axis)
    return g.reshape(g.shape[:axis] + (g.shape[axis] * g.shape[axis + 1],) + g.shape[axis + 2:])


def _shards_from_full(a, axis):
    a = a.reshape(a.shape[:axis] + (N_CHIPS, a.shape[axis] // N_CHIPS) + a.shape[axis + 1:])
    return jnp.moveaxis(a, axis, 0)


def _ffn_forward(x, xb, w_in, w_out, g, b, alpha, tag):
    gate, up, act = _ffn_in(xb, w_in, f"{tag}_in")
    y = _mm(act, w_out, "nn", F32, f"{tag}_out")
    out, outb, xhat, rstd = _ln_fwd(x, y, g, b, alpha, 0.5, f"{tag}_ln")
    return (out, outb), dict(xb=xb, gate=gate, up=up, act=act, xhat=xhat, rstd=rstd)


def _ffn_backward(dout, saved, w_in, w_out, g, tag):
    dr, dyb, dg, db = _ln_bwd(*dout, saved["xhat"], saved["rstd"], g, 0.5, f"{tag}_ln_bwd")
    dh = _ffn_dact(dyb, w_out, saved["gate"], saved["up"], f"{tag}_dact")
    dw_out = _mm(saved["act"], dyb, "tn", F32, f"{tag}_dw_out")
    dw_in = _mm(saved["xb"], dh, "tn", F32, f"{tag}_dw_in", split_b=True)
    dxb = _mm(dh, w_in, "nt", F32, f"{tag}_dx", split_a=True)
    return dr, dxb, dw_in, dw_out, dg, db


def _sb_forward(xb, w_in, w_out, tag):
    nw = w_out.shape[0]
    h = _mm(xb, w_in, "nn", MXU_DTYPE, f"{tag}_qkv")
    q, k, v = (_to_heads(h[:, j * nw:(j + 1) * nw]) for j in range(3))
    o, ob = _sb_fwd(q, k, v, f"{tag}_att")
    ob = _from_heads(ob)
    return _mm(ob, w_out, "nn", F32, f"{tag}_proj"), dict(q=q, k=k, v=v, o=o, ob=ob)


def _sb_backward(dmix, saved, xb, w_in, w_out, tag):
    dw_out = _mm(saved["ob"], dmix, "tn", F32, f"{tag}_dw_out")
    do = _to_heads(_mm(dmix, w_out, "nt", MXU_DTYPE, f"{tag}_do"))
    dq, dk, dv = _sb_bwd(saved["q"], saved["k"], saved["v"], saved["o"], do, f"{tag}_att_bwd")
    dh = jnp.concatenate([_from_heads(t) for t in (dq, dk, dv)], axis=1).astype(MXU_DTYPE)
    dw_in = _mm(xb, dh, "tn", F32, f"{tag}_dw_in")
    return _mm(dh, w_in, "nt", F32, f"{tag}_dx"), dw_in, dw_out


def _swa_forward(xb, w_in, sinks, w_out, tables, tag):
    nq = w_out.shape[0]
    nkv = (w_in.shape[1] - nq) // 2
    h = _mm(xb, w_in, "nn", F32, f"{tag}_qkv")
    n_rot = (nq + nkv) // 128
    hb = _rope(h, *tables, lambda j: j // n_rot, MXU_DTYPE, f"{tag}_rope")
    q, k, v = _to_heads(hb[:, :nq]), _to_heads(hb[:, nq:nq + nkv]), _to_heads(hb[:, nq + nkv:])
    o, lse = _band_fwd(q, k, v, sinks, SWA_WINDOW - 1, f"{tag}_att")
    o, ob = _merge_fwd([o], [lse], f"{tag}_cast")
    ob = _from_heads(ob)
    return _mm(ob, w_out, "nn", F32, f"{tag}_proj"), dict(q=q, k=k, v=v, o=o, lse=lse, ob=ob, n_rot=n_rot)


def _swa_backward(dmix, saved, xb, w_in, sinks, w_out, tables, tag):
    dw_out = _mm(saved["ob"], dmix, "tn", F32, f"{tag}_dw_out")
    do = _to_heads(_mm(dmix, w_out, "nt", MXU_DTYPE, f"{tag}_do"))
    (dog,), (delta,) = _merge_bwd(do, saved["o"], [saved["lse"]], f"{tag}_delta")
    dq, dk, dv, dsink = _band_bwd(saved["q"], saved["k"], saved["v"], dog, saved["lse"], delta, sinks, SWA_WINDOW - 1, f"{tag}_att_bwd")
    dh = jnp.concatenate([_from_heads(t) for t in (dq, dk, dv)], axis=1)
    n_rot = saved["n_rot"]
    dhb = _rope(dh, tables[0], -tables[1], lambda j: j // n_rot, MXU_DTYPE, f"{tag}_rope_bwd")
    dw_in = _mm(xb, dhb, "tn", F32, f"{tag}_dw_in")
    return _mm(dhb, w_in, "nt", F32, f"{tag}_dx"), dw_in, dw_out, dsink[:, 0, 0]


def _dil_forward(xb, w_in, w_out, tables, tag):
    nw = w_out.shape[0]
    blocks = nw // 128
    h = _mm(xb, w_in, "nn", F32, f"{tag}_qkv")
    hb = _rope(h, *tables, lambda j: (j % (3 * blocks)) // (2 * blocks), MXU_DTYPE, f"{tag}_rope")
    qkv, outs, lses = [], [], []
    for gi, (win, dil) in enumerate(DIL_GROUPS):
        base = gi * 3 * nw
        q, k, v = (_to_heads(hb[:, base + j * nw:base + (j + 1) * nw], dil) for j in range(3))
        o, lse = _band_fwd(q, k, v, None, win // dil, f"{tag}_att{gi}")
        qkv.append((q, k, v))
        outs.append(_unstride(o, dil))
        lses.append(_unstride(lse, dil))
    o, ob = _merge_fwd(outs, lses, f"{tag}_merge")
    ob = _from_heads(ob)
    return _mm(ob, w_out, "nn", F32, f"{tag}_proj"), dict(qkv=qkv, o=o, lses=lses, ob=ob, blocks=blocks)


def _dil_backward(dmix, saved, xb, w_in, w_out, tables, tag):
    dw_out = _mm(saved["ob"], dmix, "tn", F32, f"{tag}_dw_out")
    do = _to_heads(_mm(dmix, w_out, "nt", MXU_DTYPE, f"{tag}_do"))
    dogs, deltas = _merge_bwd(do, saved["o"], saved["lses"], f"{tag}_merge_bwd")
    parts = []
    for gi, (win, dil) in enumerate(DIL_GROUPS):
        q, k, v = saved["qkv"][gi]
        dq, dk, dv, _ = _band_bwd(q, k, v, _restride(dogs[gi], dil), _restride(saved["lses"][gi], dil), _restride(deltas[gi], dil), None, win // dil, f"{tag}_att{gi}_bwd")
        parts += [_from_heads(t, dil) for t in (dq, dk, dv)]
    dh = jnp.concatenate(parts, axis=1)
    blocks = saved["blocks"]
    dhb = _rope(dh, tables[0], -tables[1], lambda j: (j % (3 * blocks)) // (2 * blocks), MXU_DTYPE, f"{tag}_rope_bwd")
    dw_in = _mm(xb, dhb, "tn", F32, f"{tag}_dw_in")
    return _mm(dhb, w_in, "nt", F32, f"{tag}_dx"), dw_in, dw_out


def kernel(x, p, ffn1_w_in, ffn1_w_out, ffn2_w_in, ffn2_w_out, ln_g, ln_b, sb_w_in, sb_w_out, swa_w_in, swa_sinks, swa_w_out, dil_w_in, dil_w_out, ple_w_proj, ple_w_gate, loss_target, m_ffn1_w_in, m_ffn1_w_out, m_ffn2_w_in, m_ffn2_w_out, m_ln_g, m_ln_b, m_sb_w_in, m_sb_w_out, m_swa_w_in, m_swa_sinks, m_swa_w_out, m_dil_w_in, m_dil_w_out, m_ple_w_proj, m_ple_w_gate, v_ffn1_w_in, v_ffn1_w_out, v_ffn2_w_in, v_ffn2_w_out, v_ln_g, v_ln_b, v_sb_w_in, v_sb_w_out, v_swa_w_in, v_swa_sinks, v_swa_w_out, v_dil_w_in, v_dil_w_out, v_ple_w_proj, v_ple_w_gate):
    shard = dict(ffn1_w_in=ffn1_w_in, ffn1_w_out=ffn1_w_out, ffn2_w_in=ffn2_w_in, ffn2_w_out=ffn2_w_out, ln_g=ln_g, ln_b=ln_b, sb_w_in=sb_w_in, sb_w_out=sb_w_out, swa_w_in=swa_w_in, swa_sinks=swa_sinks, swa_w_out=swa_w_out, dil_w_in=dil_w_in, dil_w_out=dil_w_out, ple_w_proj=ple_w_proj, ple_w_gate=ple_w_gate)
    mom_m = dict(ffn1_w_in=m_ffn1_w_in, ffn1_w_out=m_ffn1_w_out, ffn2_w_in=m_ffn2_w_in, ffn2_w_out=m_ffn2_w_out, ln_g=m_ln_g, ln_b=m_ln_b, sb_w_in=m_sb_w_in, sb_w_out=m_sb_w_out, swa_w_in=m_swa_w_in, swa_sinks=m_swa_sinks, swa_w_out=m_swa_w_out, dil_w_in=m_dil_w_in, dil_w_out=m_dil_w_out, ple_w_proj=m_ple_w_proj, ple_w_gate=m_ple_w_gate)
    mom_v = dict(ffn1_w_in=v_ffn1_w_in, ffn1_w_out=v_ffn1_w_out, ffn2_w_in=v_ffn2_w_in, ffn2_w_out=v_ffn2_w_out, ln_g=v_ln_g, ln_b=v_ln_b, sb_w_in=v_sb_w_in, sb_w_out=v_sb_w_out, swa_w_in=v_swa_w_in, swa_sinks=v_swa_sinks, swa_w_out=v_swa_w_out, dil_w_in=v_dil_w_in, dil_w_out=v_dil_w_out, ple_w_proj=v_ple_w_proj, ple_w_gate=v_ple_w_gate)
    depth = ffn1_w_in.shape[0]
    alpha = (2 * depth) ** 0.25
    c = lax.axis_index("c")

    mats = [n for n in WEIGHTS if n not in SMALL]
    got = _gather_chips([shard[n].astype(MXU_DTYPE) for n in mats] + [_pack_small([ln_g, ln_b], ())[None]], "gather_weights")
    full = {"swa_sinks": swa_sinks}
    for n, g in zip(mats, got):
        full[n] = _full_from_shards(g, SHARD_AXIS[n])
    for n, g in zip(("ln_g", "ln_b"), _unpack_small(got[-1][:, 0], [ln_g.shape, ln_b.shape], (N_CHIPS,))):
        full[n] = _full_from_shards(g, SHARD_AXIS[n])

    seq = x.shape[1]
    tables = _rope_tables(seq)
    xf = x[0]
    xb = xf.astype(MXU_DTYPE)
    saved = []
    for i in range(depth):
        kind, j = i % 3, i // 3
        sv = {}
        (x1, x1b), sv["ffn1"] = _ffn_forward(xf, xb, full["ffn1_w_in"][i], full["ffn1_w_out"][i], full["ln_g"][i, 0], full["ln_b"][i, 0], alpha, f"l{i}_ffn1")
        if kind == 0:
            mix, sv["mix"] = _sb_forward(x1b, full["sb_w_in"][j], full["sb_w_out"][j], f"l{i}_sb")
        elif kind == 1:
            mix, sv["mix"] = _swa_forward(x1b, full["swa_w_in"][j], swa_sinks[j], full["swa_w_out"][j], tables, f"l{i}_swa")
        else:
            mix, sv["mix"] = _dil_forward(x1b, full["dil_w_in"][j], full["dil_w_out"][j], tables, f"l{i}_dil")
        x2, x2b, sv["xhat2"], sv["rstd2"] = _ln_fwd(x1, mix, full["ln_g"][i, 1], full["ln_b"][i, 1], alpha, 1.0, f"l{i}_mix_ln")
        sv["x1b"] = x1b
        (x3, x3b), sv["ffn2"] = _ffn_forward(x2, x2b, full["ffn2_w_in"][i], full["ffn2_w_out"][i], full["ln_g"][i, 2], full["ln_b"][i, 2], alpha, f"l{i}_ffn2")
        xf, xb, sv["u"], sv["e"] = _ple_fwd(x3, x3b, p[i, 0], full["ple_w_gate"][i], full["ple_w_proj"][i], f"l{i}_ple")
        sv["x3b"] = x3b
        saved.append(sv)

    loss_part, dy = _loss(xf, loss_target[0], "loss")
    loss = lax.psum(loss_part[0, 0], ("x", "y", "c"))

    grads = {n: [None] * full[n].shape[0] for n in WEIGHTS if n not in ("ln_g", "ln_b")}
    gsum = {n: [None] * full[n].shape[0] for n in mats}
    dln_g = [[None] * 3 for _ in range(depth)]
    dln_b = [[None] * 3 for _ in range(depth)]
    dout = (dy, None, 1.0)
    for i in reversed(range(depth)):
        kind, j = i % 3, i // 3
        mixer = ("sb", "swa", "dil")[kind]
        sv = saved[i]
        dx4, dub, deb = _ple_bwd(*dout, sv["u"], sv["e"], f"l{i}_ple_bwd")
        grads["ple_w_gate"][i] = _mm(sv["x3b"], dub, "tn", F32, f"l{i}_ple_dw_gate")
        grads["ple_w_proj"][i] = _mm(p[i, 0], deb, "tn", F32, f"l{i}_ple_dw_proj")
        dxb = _mm(dub, full["ple_w_gate"][i], "nt", F32, f"l{i}_ple_dx")
        dr, dxb, grads["ffn2_w_in"][i], grads["ffn2_w_out"][i], dln_g[i][2], dln_b[i][2] = _ffn_backward((dx4, dxb, 1.0), sv["ffn2"], full["ffn2_w_in"][i], full["ffn2_w_out"][i], full["ln_g"][i, 2], f"l{i}_ffn2")
        dr, dmix, dln_g[i][1], dln_b[i][1] = _ln_bwd(dr, dxb, alpha, sv["xhat2"], sv["rstd2"], full["ln_g"][i, 1], 1.0, f"l{i}_mix_ln_bwd")
        if kind == 0:
            dxb, grads["sb_w_in"][j], grads["sb_w_out"][j] = _sb_backward(dmix, sv["mix"], sv["x1b"], full["sb_w_in"][j], full["sb_w_out"][j], f"l{i}_sb")
        elif kind == 1:
            dxb, grads["swa_w_in"][j], grads["swa_w_out"][j], grads["swa_sinks"][j] = _swa_backward(dmix, sv["mix"], sv["x1b"], full["swa_w_in"][j], swa_sinks[j], full["swa_w_out"][j], tables, f"l{i}_swa")
        else:
            dxb, grads["dil_w_in"][j], grads["dil_w_out"][j] = _dil_backward(dmix, sv["mix"], sv["x1b"], full["dil_w_in"][j], full["dil_w_out"][j], tables, f"l{i}_dil")
        dr, dxb, grads["ffn1_w_in"][i], grads["ffn1_w_out"][i], dln_g[i][0], dln_b[i][0] = _ffn_backward((dr, dxb, alpha), sv["ffn1"], full["ffn1_w_in"][i], full["ffn1_w_out"][i], full["ln_g"][i, 0], f"l{i}_ffn1")
        dout = (dr, dxb, alpha)
        layer = [("ffn1_w_in", i), ("ffn1_w_out", i), (f"{mixer}_w_in", j), (f"{mixer}_w_out", j), ("ffn2_w_in", i), ("ffn2_w_out", i), ("ple_w_proj", i), ("ple_w_gate", i)]
        summed = _reduce_group([_shards_from_full(grads[n][k], SHARD_AXIS[n] - 1) for n, k in layer], c, f"l{i}_reduce")
        for (n, k), g in zip(layer, summed):
            gsum[n][k] = g
    grad_x = _axpy(*dout, "grad_x")[None]

    gshard = {n: jnp.stack(g) for n, g in gsum.items()}
    small = [
        _shards_from_full(jnp.stack([jnp.concatenate(r, axis=0) for r in dln_g]), SHARD_AXIS["ln_g"]),
        _shards_from_full(jnp.stack([jnp.concatenate(r, axis=0) for r in dln_b]), SHARD_AXIS["ln_b"]),
        jnp.broadcast_to(jnp.stack(grads["swa_sinks"])[None], (N_CHIPS,) + swa_sinks.shape),
    ]
    (small_sum,) = _reduce_group([_pack_small(small, (N_CHIPS,))], c, "small_reduce")
    for n, g in zip(SMALL, _unpack_small(small_sum, [shard[n].shape for n in SMALL], ())):
        gshard[n] = g

    delta, new_m, new_v = {}, {}, {}
    for n in WEIGHTS:
        shp = shard[n].shape
        two_d = (-1, shp[-1])
        d, nm, nv = _adamw(shard[n].reshape(two_d), gshard[n].reshape(two_d), mom_m[n].reshape(two_d), mom_v[n].reshape(two_d), f"adamw_{n}")
        delta[n], new_m[n], new_v[n] = d.reshape(shp), nm.reshape(shp), nv.reshape(shp)

    return (loss, grad_x, *[gshard[n] for n in WEIGHTS], *[delta[n] for n in WEIGHTS], *[new_m[n] for n in WEIGHTS], *[new_v[n] for n in WEIGHTS])
```

```python
import functools
import math

import jax
import jax.numpy as jnp
from jax import lax
from jax.experimental import pallas as pl
from jax.experimental.pallas import tpu as pltpu

F32 = jnp.float32
MXU_DTYPE = jnp.bfloat16
MESH = pl.DeviceIdType.MESH

HEAD_DIM = 64
ATT_BLK = 128
SWA_WINDOW = 128
DIL_GROUPS = ((128, 1), (512, 4), (2048, 16))
LN_EPS = 1e-5
ROPE_THETA = 10000.0
NEG_INF = -1e30
ADAM_LR, ADAM_B1, ADAM_B2, ADAM_EPS, ADAM_WD, ADAM_STEP = 0.001, 0.9, 0.999, 1e-08, 0.01, 10

VMEM_LIMIT_BYTES = 56 * 1024 * 1024
N_CHIPS = 4


def _params(sem=None):
    return pltpu.CompilerParams(dimension_semantics=sem, vmem_limit_bytes=VMEM_LIMIT_BYTES)


def _tile(n, target, unit):
    t = (min(target, n) // unit) * unit
    while t >= unit:
        if n % t == 0:
            return t
        t -= unit
    return n


def _dot(a, b, dims):
    return lax.dot_general(a.astype(MXU_DTYPE), b.astype(MXU_DTYPE), (dims, ((), ())), preferred_element_type=F32)


NN = ((1,), (0,))
NT = ((1,), (1,))
TN = ((0,), (0,))


def _mm(a, b, mode, out_dtype, name, split_a=False, split_b=False, tm=1024, tn=1408, tk=1408):
    dims = {"nn": NN, "nt": NT, "tn": TN}[mode]
    if split_a:
        m, k = a.shape[1], 2 * a.shape[2]
    elif mode == "tn":
        k, m = a.shape
    else:
        m, k = a.shape
    if split_b:
        n = 2 * b.shape[2]
    elif mode == "nt":
        n = b.shape[0]
    else:
        n = b.shape[1]
    tm = _tile(m, tm, 128)
    tn = _tile(n // 2 if split_b else n, tn, 128)
    tk = _tile(k // 2 if split_a else k, tk, 128)
    nk = k // tk
    nk_half = nk // 2
    nn_half = (n // tn) // 2

    if split_a:
        a_spec = pl.BlockSpec((None, tm, tk), lambda i, j, kk: (kk // nk_half, i, kk % nk_half))
    elif mode == "tn":
        a_spec = pl.BlockSpec((tk, tm), lambda i, j, kk: (kk, i))
    else:
        a_spec = pl.BlockSpec((tm, tk), lambda i, j, kk: (i, kk))
    if split_b:
        b_spec = pl.BlockSpec((None, tk, tn), lambda i, j, kk: (j // nn_half, kk, j % nn_half))
    elif mode == "nt":
        b_spec = pl.BlockSpec((tn, tk), lambda i, j, kk: (j, kk))
    else:
        b_spec = pl.BlockSpec((tk, tn), lambda i, j, kk: (kk, j))

    def body(a_ref, b_ref, o_ref, acc_ref):
        kk = pl.program_id(2)

        @pl.when(kk == 0)
        def _():
            acc_ref[...] = jnp.zeros_like(acc_ref)

        acc_ref[...] += _dot(a_ref[...], b_ref[...], dims)

        @pl.when(kk == nk - 1)
        def _():
            o_ref[...] = acc_ref[...].astype(o_ref.dtype)

    return pl.pallas_call(
        body,
        name=name,
        grid=(m // tm, n // tn, nk),
        in_specs=[a_spec, b_spec],
        out_specs=pl.BlockSpec((tm, tn), lambda i, j, kk: (i, j)),
        out_shape=jax.ShapeDtypeStruct((m, n), out_dtype),
        scratch_shapes=[pltpu.VMEM((tm, tn), F32)],
        compiler_params=_params(("parallel", "parallel", "arbitrary")),
    )(a, b)


def _sigmoid(x):
    return 1.0 / (1.0 + jnp.exp(-x))


def _ffn_in(xb, w_in, name):
    s, d = xb.shape
    f = w_in.shape[1] // 2
    tm = _tile(s, 1024, 128)
    tn = _tile(f, 256, 128)
    nj = f // tn

    def body(x_ref, wg_ref, wu_ref, g_ref, u_ref, a_ref):
        x = x_ref[...]
        g = _dot(x, wg_ref[...], NN)
        u = _dot(x, wu_ref[...], NN)
        g_ref[...] = g
        u_ref[...] = u
        a_ref[...] = (g * _sigmoid(g) * u).astype(a_ref.dtype)

    out = pl.BlockSpec((tm, tn), lambda i, j: (i, j))
    return pl.pallas_call(
        body,
        name=name,
        grid=(s // tm, nj),
        in_specs=[
            pl.BlockSpec((tm, d), lambda i, j: (i, 0)),
            pl.BlockSpec((d, tn), lambda i, j: (0, j)),
            pl.BlockSpec((d, tn), lambda i, j: (0, j + nj)),
        ],
        out_specs=[out, out, out],
        out_shape=[
            jax.ShapeDtypeStruct((s, f), F32),
            jax.ShapeDtypeStruct((s, f), F32),
            jax.ShapeDtypeStruct((s, f), MXU_DTYPE),
        ],
        compiler_params=_params(("parallel", "parallel")),
    )(xb, w_in, w_in)


def _ffn_dact(dyb, w_out, gate, up, name):
    s, d = dyb.shape
    f = w_out.shape[0]
    tm = _tile(s, 1024, 128)
    tn = _tile(f, 256, 128)

    def body(dy_ref, w_ref, g_ref, u_ref, o_ref):
        dact = _dot(dy_ref[...], w_ref[...], NT)
        g = g_ref[...]
        sig = _sigmoid(g)
        o_ref[0] = (dact * u_ref[...] * (sig * (1.0 + g * (1.0 - sig)))).astype(o_ref.dtype)
        o_ref[1] = (dact * (g * sig)).astype(o_ref.dtype)

    tile = pl.BlockSpec((tm, tn), lambda i, j: (i, j))
    return pl.pallas_call(
        body,
        name=name,
        grid=(s // tm, f // tn),
        in_specs=[
            pl.BlockSpec((tm, d), lambda i, j: (i, 0)),
            pl.BlockSpec((tn, d), lambda i, j: (j, 0)),
            tile,
            tile,
        ],
        out_specs=pl.BlockSpec((2, tm, tn), lambda i, j: (0, i, j)),
        out_shape=jax.ShapeDtypeStruct((2, s, f), MXU_DTYPE),
        compiler_params=_params(("parallel", "parallel")),
    )(dyb, w_out, gate, up)


def _ple_fwd(x, xb, p, w_gate, w_proj, name):
    s, d = x.shape
    pd = p.shape[1]
    tm = _tile(s, 1024, 128)
    tn = _tile(d, 512, 128)

    def body(x_ref, xb_ref, p_ref, wg_ref, wp_ref, o_ref, ob_ref, u_ref, e_ref):
        u = _dot(xb_ref[...], wg_ref[...], NN)
        e = _dot(p_ref[...], wp_ref[...], NN)
        out = x_ref[...] + _sigmoid(u) * e
        o_ref[...] = out
        ob_ref[...] = out.astype(ob_ref.dtype)
        u_ref[...] = u
        e_ref[...] = e

    tile = pl.BlockSpec((tm, tn), lambda i, j: (i, j))
    return pl.pallas_call(
        body,
        name=name,
        grid=(s // tm, d // tn),
        in_specs=[
            tile,
            pl.BlockSpec((tm, d), lambda i, j: (i, 0)),
            pl.BlockSpec((tm, pd), lambda i, j: (i, 0)),
            pl.BlockSpec((d, tn), lambda i, j: (0, j)),
            pl.BlockSpec((pd, tn), lambda i, j: (0, j)),
        ],
        out_specs=[tile, tile, tile, tile],
        out_shape=[
            jax.ShapeDtypeStruct((s, d), F32),
            jax.ShapeDtypeStruct((s, d), MXU_DTYPE),
            jax.ShapeDtypeStruct((s, d), F32),
            jax.ShapeDtypeStruct((s, d), F32),
        ],
        compiler_params=_params(("parallel", "parallel")),
    )(x, xb, p, w_gate, w_proj)


def _rows_spec(ts, d):
    return pl.BlockSpec((ts, d), lambda i: (i, 0))


def _ln_fwd(x, y, g, b, alpha, beta, name):
    s, d = x.shape
    ts = _tile(s, 512, 8)

    def body(x_ref, y_ref, g_ref, b_ref, o_ref, ob_ref, xh_ref, rs_ref):
        r = alpha * x_ref[...] + beta * y_ref[...]
        mu = jnp.mean(r, axis=1, keepdims=True)
        cen = r - mu
        var = jnp.mean(cen * cen, axis=1, keepdims=True)
        rstd = lax.rsqrt(var + LN_EPS)
        xhat = cen * rstd
        out = xhat * g_ref[...] + b_ref[...]
        o_ref[...] = out
        ob_ref[...] = out.astype(ob_ref.dtype)
        xh_ref[...] = xhat
        rs_ref[...] = rstd

    vec = pl.BlockSpec((1, d), lambda i: (0, 0))
    return pl.pallas_call(
        body,
        name=name,
        grid=(s // ts,),
        in_specs=[_rows_spec(ts, d), _rows_spec(ts, d), vec, vec],
        out_specs=[_rows_spec(ts, d), _rows_spec(ts, d), _rows_spec(ts, d), _rows_spec(ts, 1)],
        out_shape=[
            jax.ShapeDtypeStruct((s, d), F32),
            jax.ShapeDtypeStruct((s, d), MXU_DTYPE),
            jax.ShapeDtypeStruct((s, d), F32),
            jax.ShapeDtypeStruct((s, 1), F32),
        ],
        compiler_params=_params(("parallel",)),
    )(x, y, g.reshape(1, d), b.reshape(1, d))


def _ln_bwd(ga, gb, ca, xhat, rstd, g, beta, name):
    s, d = xhat.shape
    ts = _tile(s, 512, 8)

    def body(ga_ref, gb_ref, xh_ref, rs_ref, g_ref, dr_ref, dyb_ref, dg_ref, db_ref):
        @pl.when(pl.program_id(0) == 0)
        def _():
            dg_ref[...] = jnp.zeros_like(dg_ref)
            db_ref[...] = jnp.zeros_like(db_ref)

        dout = ca * ga_ref[...] + gb_ref[...]
        xhat = xh_ref[...]
        dg_ref[...] += jnp.sum(dout * xhat, axis=0, keepdims=True)
        db_ref[...] += jnp.sum(dout, axis=0, keepdims=True)
        dxh = dout * g_ref[...]
        m1 = jnp.mean(dxh, axis=1, keepdims=True)
        m2 = jnp.mean(dxh * xhat, axis=1, keepdims=True)
        dr = rs_ref[...] * (dxh - m1 - xhat * m2)
        dr_ref[...] = dr
        dyb_ref[...] = (beta * dr).astype(dyb_ref.dtype)

    vec = pl.BlockSpec((1, d), lambda i: (0, 0))
    return pl.pallas_call(
        body,
        name=name,
        grid=(s // ts,),
        in_specs=[_rows_spec(ts, d), _rows_spec(ts, d), _rows_spec(ts, d), _rows_spec(ts, 1), vec],
        out_specs=[_rows_spec(ts, d), _rows_spec(ts, d), vec, vec],
        out_shape=[
            jax.ShapeDtypeStruct((s, d), F32),
            jax.ShapeDtypeStruct((s, d), MXU_DTYPE),
            jax.ShapeDtypeStruct((1, d), F32),
            jax.ShapeDtypeStruct((1, d), F32),
        ],
        compiler_params=_params(("arbitrary",)),
    )(ga, gb, xhat, rstd, g.reshape(1, d))


def _ple_bwd(ga, gb, ca, u, e, name):
    s, d = u.shape
    ts = _tile(s, 512, 8)
    grads = [ga] if gb is None else [ga, gb]

    def body(*refs):
        u_ref, e_ref, dx_ref, du_ref, de_ref = refs[len(grads):]
        dx = ca * refs[0][...]
        if gb is not None:
            dx = dx + refs[1][...]
        sig = _sigmoid(u_ref[...])
        dx_ref[...] = dx
        du_ref[...] = (dx * e_ref[...] * sig * (1.0 - sig)).astype(du_ref.dtype)
        de_ref[...] = (dx * sig).astype(de_ref.dtype)

    return pl.pallas_call(
        body,
        name=name,
        grid=(s // ts,),
        in_specs=[_rows_spec(ts, d)] * (len(grads) + 2),
        out_specs=[_rows_spec(ts, d)] * 3,
        out_shape=[
            jax.ShapeDtypeStruct((s, d), F32),
            jax.ShapeDtypeStruct((s, d), MXU_DTYPE),
            jax.ShapeDtypeStruct((s, d), MXU_DTYPE),
        ],
        compiler_params=_params(("parallel",)),
    )(*grads, u, e)


def _axpy(ga, gb, ca, name):
    s, d = ga.shape
    ts = _tile(s, 512, 8)

    def body(ga_ref, gb_ref, o_ref):
        o_ref[...] = ca * ga_ref[...] + gb_ref[...]

    return pl.pallas_call(
        body,
        name=name,
        grid=(s // ts,),
        in_specs=[_rows_spec(ts, d)] * 2,
        out_specs=_rows_spec(ts, d),
        out_shape=jax.ShapeDtypeStruct((s, d), F32),
        compiler_params=_params(("parallel",)),
    )(ga, gb)


def _loss(y, target, name):
    s, d = y.shape
    ts = _tile(s, 512, 8)

    def body(y_ref, t_ref, l_ref, dy_ref):
        @pl.when(pl.program_id(0) == 0)
        def _():
            l_ref[...] = jnp.zeros_like(l_ref)

        err = y_ref[...] - t_ref[...]
        l_ref[...] += (0.5 / d) * jnp.sum(jnp.sum(err * err, axis=1, keepdims=True), axis=0, keepdims=True)
        dy_ref[...] = err * (1.0 / d)

    return pl.pallas_call(
        body,
        name=name,
        grid=(s // ts,),
        in_specs=[_rows_spec(ts, d)] * 2,
        out_specs=[pl.BlockSpec((1, 1), lambda i: (0, 0)), _rows_spec(ts, d)],
        out_shape=[jax.ShapeDtypeStruct((1, 1), F32), jax.ShapeDtypeStruct((s, d), F32)],
        compiler_params=_params(("arbitrary",)),
    )(y, target)


def _rope_tables(seq):
    pos = jnp.arange(seq, dtype=F32)
    inv = ROPE_THETA ** (-jnp.arange(0, HEAD_DIM, 2, dtype=F32) / HEAD_DIM)
    ang = pos[:, None] * inv[None, :]
    cos, sin = jnp.cos(ang), jnp.sin(ang)
    cos2 = jnp.concatenate([cos, cos, cos, cos, jnp.ones((seq, 128), F32)], axis=1)
    sin2 = jnp.concatenate([-sin, sin, -sin, sin, jnp.zeros((seq, 128), F32)], axis=1)
    return cos2, sin2


def _rope(h, cos2, sin2, plain_block, out_dtype, name, width=128):
    s, n = h.shape
    ts = _tile(s, 512, 8)

    def body(h_ref, c_ref, s_ref, o_ref):
        cos, sin = c_ref[...], s_ref[...]
        lane = lax.broadcasted_iota(jnp.int32, cos.shape, 1)
        first_half = lane % HEAD_DIM < HEAD_DIM // 2
        for w in range(width // 128):
            cols = slice(w * 128, (w + 1) * 128)
            x = h_ref[:, cols].astype(F32)
            partner = jnp.where(first_half, pltpu.roll(x, 128 - HEAD_DIM // 2, 1), pltpu.roll(x, HEAD_DIM // 2, 1))
            o_ref[:, cols] = (x * cos + partner * sin).astype(o_ref.dtype)

    tile = pl.BlockSpec((ts, width), lambda i, j: (i, j))
    table = pl.BlockSpec((ts, 128), lambda i, j: (i, plain_block(j)))
    return pl.pallas_call(
        body,
        name=name,
        grid=(s // ts, n // width),
        in_specs=[tile, table, table],
        out_specs=tile,
        out_shape=jax.ShapeDtypeStruct((s, n), out_dtype),
        compiler_params=_params(("parallel", "parallel")),
    )(h, cos2, sin2)


SB_TQ = 512
SB_BLK = 128


def _tri2(strict):
    row = lax.broadcasted_iota(jnp.int32, (2 * SB_BLK, SB_BLK), 0) % SB_BLK
    col = lax.broadcasted_iota(jnp.int32, (2 * SB_BLK, SB_BLK), 1)
    return (row > col if strict else row >= col).astype(MXU_DTYPE)


def _cumsum_dot(x, tri2):
    hi = x.astype(MXU_DTYPE)
    lo = x - hi.astype(F32)
    return _dot(jnp.concatenate([hi, lo.astype(MXU_DTYPE)], axis=1), tri2, NN)


def _sb_logits(z, valid):
    l1p = jnp.log(1.0 + jnp.exp(-jnp.abs(z)))
    sp = jnp.maximum(z, 0.0) + l1p
    if valid is not None:
        sp = jnp.where(valid, sp, 0.0)
    return sp, jnp.minimum(z, 0.0) - l1p


def _sb_weights(ls, after, valid):
    a = jnp.exp(ls - after)
    return a if valid is None else jnp.where(valid, a, 0.0)


def _sb_setup(q_ref, k_ref, v_ref, n_sub, scale):
    qs = [q_ref[0, u * SB_BLK:(u + 1) * SB_BLK, :] * scale for u in range(n_sub)]
    row = lax.broadcasted_iota(jnp.int32, (SB_BLK, SB_BLK), 0)
    col = lax.broadcasted_iota(jnp.int32, (SB_BLK, SB_BLK), 1)

    def load(jj):
        start = pl.multiple_of(jj * SB_BLK, SB_BLK)
        return start, k_ref[0, pl.ds(start, SB_BLK), :], v_ref[0, pl.ds(start, SB_BLK), :]

    return qs, col < row, load


def _sb_fwd(q, k, v, name):
    nh, s, dh = q.shape
    tq = min(SB_TQ, s)
    n_sub = tq // SB_BLK
    scale = dh ** -0.5

    def body(q_ref, k_ref, v_ref, o_ref, ob_ref):
        base = pl.program_id(1) * n_sub
        qs, diag_valid, load = _sb_setup(q_ref, k_ref, v_ref, n_sub, scale)
        tri_after = _tri2(True)

        def key_blocks(first, carry, diagonal):
            blocks = list(range(n_sub - 1, -1, -1))
            kv = {d: load(first + d) for d in blocks}
            tiles = [(d, u) for d in blocks for u in range(d if diagonal else 0, n_sub)]
            valid = {t: diag_valid if diagonal and t[0] == t[1] else None for t in tiles}
            z = {t: _dot(qs[t[1]], kv[t[0]][1], NT) for t in tiles}
            sp_ls = {t: _sb_logits(z[t], valid[t]) for t in tiles}
            inside = {t: _cumsum_dot(sp_ls[t][0], tri_after) for t in tiles}
            carry = list(carry)
            for t in tiles:
                after_c, acc = carry[t[1]]
                sp, ls = sp_ls[t]
                a = _sb_weights(ls, after_c + inside[t], valid[t])
                carry[t[1]] = (after_c + jnp.sum(sp, axis=1, keepdims=True), acc + _dot(a, kv[t[0]][2], NN))
            return tuple(carry)

        carry = tuple((jnp.zeros((SB_BLK, 1), F32), jnp.zeros((SB_BLK, dh), F32)) for _ in range(n_sub))
        carry = key_blocks(base, carry, True)
        carry = lax.fori_loop(0, base // n_sub, lambda t, c: key_blocks(base - n_sub * (t + 1), c, False), carry)
        for u in range(n_sub):
            rows = slice(u * SB_BLK, (u + 1) * SB_BLK)
            o_ref[0, rows, :] = carry[u][1]
            ob_ref[0, rows, :] = carry[u][1].astype(ob_ref.dtype)

    blk = pl.BlockSpec((1, tq, dh), lambda h, i: (h, i, 0))
    full = pl.BlockSpec((1, s, dh), lambda h, i: (h, 0, 0))
    return pl.pallas_call(
        body,
        name=name,
        grid=(nh, s // tq),
        in_specs=[blk, full, full],
        out_specs=[blk, blk],
        out_shape=[jax.ShapeDtypeStruct((nh, s, dh), F32), jax.ShapeDtypeStruct((nh, s, dh), MXU_DTYPE)],
        compiler_params=_params(("parallel", "parallel")),
    )(q, k, v)


def _sb_bwd(q, k, v, o, do, name):
    nh, s, dh = q.shape
    tq = min(SB_TQ, s)
    n_sub = tq // SB_BLK
    scale = dh ** -0.5

    def body(q_ref, k_ref, v_ref, o_ref, do_ref, dq_ref, dk_ref, dv_ref):
        i = pl.program_id(1)
        base = i * n_sub

        @pl.when(i == 0)
        def _():
            dk_ref[...] = jnp.zeros_like(dk_ref)
            dv_ref[...] = jnp.zeros_like(dv_ref)

        qs, diag_valid, load = _sb_setup(q_ref, k_ref, v_ref, n_sub, scale)
        dob = [do_ref[0, u * SB_BLK:(u + 1) * SB_BLK, :] for u in range(n_sub)]
        total = [jnp.sum(dob[u].astype(F32) * o_ref[0, u * SB_BLK:(u + 1) * SB_BLK, :], axis=1, keepdims=True) for u in range(n_sub)]
        tri_after = _tri2(True)
        tri_from = _tri2(False)

        def key_blocks(first, carry, diagonal):
            blocks = list(range(n_sub - 1, -1, -1))
            kv = {d: load(first + d) for d in blocks}
            tiles = [(d, u) for d in blocks for u in range(d if diagonal else 0, n_sub)]
            valid = {t: diag_valid if diagonal and t[0] == t[1] else None for t in tiles}
            z = {t: _dot(qs[t[1]], kv[t[0]][1], NT) for t in tiles}
            da = {t: _dot(dob[t[1]], kv[t[0]][2], NT) for t in tiles}
            sp_ls = {t: _sb_logits(z[t], valid[t]) for t in tiles}
            inside = {t: _cumsum_dot(sp_ls[t][0], tri_after) for t in tiles}
            after_run = [c[0] for c in carry]
            ab, dl = {}, {}
            for t in tiles:
                sp, ls = sp_ls[t]
                ab[t] = _sb_weights(ls, after_run[t[1]] + inside[t], valid[t]).astype(MXU_DTYPE)
                dl[t] = ab[t].astype(F32) * da[t]
                after_run[t[1]] = after_run[t[1]] + jnp.sum(sp, axis=1, keepdims=True)
            from_in = {t: _cumsum_dot(dl[t], tri_from) for t in tiles}
            from_run = [c[1] for c in carry]
            dq = [c[2] for c in carry]
            for d in blocks:
                start, kb, _ = kv[d]
                dk = dv = None
                for u in range(d if diagonal else 0, n_sub):
                    t = (d, u)
                    dz = dl[t] - jnp.exp(sp_ls[t][1]) * (dl[t] + total[u] - (from_run[u] + from_in[t]))
                    if valid[t] is not None:
                        dz = jnp.where(valid[t], dz, 0.0)
                    dzb = dz.astype(MXU_DTYPE)
                    from_run[u] = from_run[u] + jnp.sum(dl[t], axis=1, keepdims=True)
                    dq[u] = dq[u] + _dot(dzb, kb, NN)
                    dk_u, dv_u = _dot(dzb, qs[u], TN), _dot(ab[t], dob[u], TN)
                    dk = dk_u if dk is None else dk + dk_u
                    dv = dv_u if dv is None else dv + dv_u
                dk_ref[0, pl.ds(start, SB_BLK), :] += dk
                dv_ref[0, pl.ds(start, SB_BLK), :] += dv
            return tuple(zip(after_run, from_run, dq))

        carry = tuple((jnp.zeros((SB_BLK, 1), F32), jnp.zeros((SB_BLK, 1), F32), jnp.zeros((SB_BLK, dh), F32)) for _ in range(n_sub))
        carry = key_blocks(base, carry, True)
        carry = lax.fori_loop(0, i, lambda t, c: key_blocks(base - n_sub * (t + 1), c, False), carry)
        for u in range(n_sub):
            dq_ref[0, u * SB_BLK:(u + 1) * SB_BLK, :] = carry[u][2] * scale

    blk = pl.BlockSpec((1, tq, dh), lambda h, i: (h, i, 0))
    full = pl.BlockSpec((1, s, dh), lambda h, i: (h, 0, 0))
    shape = jax.ShapeDtypeStruct((nh, s, dh), F32)
    return pl.pallas_call(
        body,
        name=name,
        grid=(nh, s // tq),
        in_specs=[blk, full, full, blk, blk],
        out_specs=[blk, full, full],
        out_shape=[shape, shape, shape],
        compiler_params=_params(("parallel", "arbitrary")),
    )(q, k, v, o, do)


BAND_TQ = 512


def _band_scores(q_ref, k_ref, i, sub, tq, length, max_dist, scale):
    t0 = i * tq + sub * ATT_BLK
    ks = pl.multiple_of(jnp.minimum(jnp.maximum(t0 - ATT_BLK, 0), length - 2 * ATT_BLK), ATT_BLK)
    qs = q_ref[0, sub * ATT_BLK:(sub + 1) * ATT_BLK, :] * scale
    kw = k_ref[0, pl.ds(ks, 2 * ATT_BLK), :]
    sc = _dot(qs, kw, NT)
    diff = (t0 + lax.broadcasted_iota(jnp.int32, sc.shape, 0)) - (ks + lax.broadcasted_iota(jnp.int32, sc.shape, 1))
    valid = (diff >= 0) & (diff <= max_dist)
    return ks, qs, kw, jnp.where(valid, sc, NEG_INF)


def _band_fwd(q, k, v, sinks, max_dist, name):
    bq, length, dh = q.shape
    group = bq // k.shape[0]
    tq = min(BAND_TQ, length)
    scale = dh ** -0.5
    n_sink = 0 if sinks is None else sinks.shape[0]

    def body(*refs):
        if n_sink:
            sink_ref, q_ref, k_ref, v_ref, o_ref, lse_ref = refs
            sink = sink_ref[pl.program_id(0) % n_sink]
        else:
            q_ref, k_ref, v_ref, o_ref, lse_ref = refs
        i = pl.program_id(1)
        for sub in range(tq // ATT_BLK):
            ks, _, _, sc = _band_scores(q_ref, k_ref, i, sub, tq, length, max_dist, scale)
            m = jnp.max(sc, axis=1, keepdims=True)
            if n_sink:
                m = jnp.maximum(m, sink)
            e = jnp.exp(sc - m)
            den = jnp.sum(e, axis=1, keepdims=True)
            if n_sink:
                den = den + jnp.exp(sink - m)
            rows = slice(sub * ATT_BLK, (sub + 1) * ATT_BLK)
            o_ref[0, rows, :] = _dot(e / den, v_ref[0, pl.ds(ks, 2 * ATT_BLK), :], NN)
            lse_ref[0, rows, :] = m + jnp.log(den)

    qblk = pl.BlockSpec((1, tq, dh), lambda b, i: (b, i, 0))
    kfull = pl.BlockSpec((1, length, dh), lambda b, i: (b // group, 0, 0))
    in_specs = [qblk, kfull, kfull]
    args = [q, k, v]
    if n_sink:
        in_specs = [pl.BlockSpec(memory_space=pltpu.SMEM)] + in_specs
        args = [sinks] + args
    return pl.pallas_call(
        body,
        name=name,
        grid=(bq, length // tq),
        in_specs=in_specs,
        out_specs=[qblk, pl.BlockSpec((1, tq, 1), lambda b, i: (b, i, 0))],
        out_shape=[jax.ShapeDtypeStruct((bq, length, dh), F32), jax.ShapeDtypeStruct((bq, length, 1), F32)],
        compiler_params=_params(("parallel", "parallel")),
    )(*args)


def _band_bwd(q, k, v, do, lse, delta, sinks, max_dist, name):
    bq, length, dh = q.shape
    group = bq // k.shape[0]
    tq = min(BAND_TQ, length)
    scale = dh ** -0.5
    n_sink = 0 if sinks is None else sinks.shape[0]

    def body(*refs):
        if n_sink:
            sink_ref, q_ref, k_ref, v_ref, do_ref, lse_ref, dl_ref, dq_ref, dk_ref, dv_ref, ds_ref = refs
            sink = sink_ref[pl.program_id(0) % n_sink]
        else:
            q_ref, k_ref, v_ref, do_ref, lse_ref, dl_ref, dq_ref, dk_ref, dv_ref, ds_ref = refs
        i = pl.program_id(1)

        @pl.when((i == 0) & (pl.program_id(0) % group == 0))
        def _():
            dk_ref[...] = jnp.zeros_like(dk_ref)
            dv_ref[...] = jnp.zeros_like(dv_ref)

        @pl.when(i == 0)
        def _():
            ds_ref[...] = jnp.zeros_like(ds_ref)

        for sub in range(tq // ATT_BLK):
            ks, qs, kw, sc = _band_scores(q_ref, k_ref, i, sub, tq, length, max_dist, scale)
            rows = slice(sub * ATT_BLK, (sub + 1) * ATT_BLK)
            lse = lse_ref[0, rows, :]
            delta_r = dl_ref[0, rows, :]
            dob = do_ref[0, rows, :]
            p = jnp.exp(sc - lse)
            dp = _dot(dob, v_ref[0, pl.ds(ks, 2 * ATT_BLK), :], NT)
            dsb = (p * (dp - delta_r)).astype(MXU_DTYPE)
            dq_ref[0, rows, :] = _dot(dsb, kw, NN) * scale
            dk_ref[0, pl.ds(ks, 2 * ATT_BLK), :] += _dot(dsb, qs, TN)
            dv_ref[0, pl.ds(ks, 2 * ATT_BLK), :] += _dot(p, dob, TN)
            if n_sink:
                ds_ref[...] += jnp.sum(-jnp.exp(sink - lse) * delta_r, axis=0, keepdims=True)

    qblk = pl.BlockSpec((1, tq, dh), lambda b, i: (b, i, 0))
    qcol = pl.BlockSpec((1, tq, 1), lambda b, i: (b, i, 0))
    kfull = pl.BlockSpec((1, length, dh), lambda b, i: (b // group, 0, 0))
    in_specs = [qblk, kfull, kfull, qblk, qcol, qcol]
    args = [q, k, v, do, lse, delta]
    if n_sink:
        in_specs = [pl.BlockSpec(memory_space=pltpu.SMEM)] + in_specs
        args = [sinks] + args
    kshape = jax.ShapeDtypeStruct((k.shape[0], length, dh), F32)
    return pl.pallas_call(
        body,
        name=name,
        grid=(bq, length // tq),
        in_specs=in_specs,
        out_specs=[qblk, kfull, kfull, pl.BlockSpec((1, 8, 128), lambda b, i: (b, 0, 0))],
        out_shape=[jax.ShapeDtypeStruct((bq, length, dh), F32), kshape, kshape, jax.ShapeDtypeStruct((bq, 8, 128), F32)],
        compiler_params=_params(("arbitrary", "arbitrary")),
    )(*args)


def _merge_weights(lse_refs):
    lses = [r[0] for r in lse_refs]
    m = functools.reduce(jnp.maximum, lses)
    es = [jnp.exp(l - m) for l in lses]
    den = functools.reduce(lambda a, b: a + b, es)
    return [e / den for e in es]


def _merge_fwd(outs, lses, name):
    n = len(outs)
    nh, s, dh = outs[0].shape
    ts = _tile(s, 1024, 8)

    def body(*refs):
        ws = _merge_weights(refs[n:2 * n])
        o = functools.reduce(lambda a, b: a + b, [w * r[0] for w, r in zip(ws, refs[:n])])
        refs[2 * n][0] = o
        refs[2 * n + 1][0] = o.astype(MXU_DTYPE)

    blk = pl.BlockSpec((1, ts, dh), lambda h, i: (h, i, 0))
    col = pl.BlockSpec((1, ts, 1), lambda h, i: (h, i, 0))
    return pl.pallas_call(
        body,
        name=name,
        grid=(nh, s // ts),
        in_specs=[blk] * n + [col] * n,
        out_specs=[blk, blk],
        out_shape=[jax.ShapeDtypeStruct((nh, s, dh), F32), jax.ShapeDtypeStruct((nh, s, dh), MXU_DTYPE)],
        compiler_params=_params(("parallel", "parallel")),
    )(*outs, *lses)


def _merge_bwd(do, o, lses, name):
    n = len(lses)
    nh, s, dh = o.shape
    ts = _tile(s, 1024, 8)

    def body(*refs):
        do_ref, o_ref = refs[:2]
        ws = _merge_weights(refs[2:2 + n])
        dof = do_ref[0].astype(F32)
        base = jnp.sum(dof * o_ref[0], axis=1, keepdims=True)
        for g in range(n):
            refs[2 + n + g][0] = (ws[g] * dof).astype(MXU_DTYPE)
            refs[2 + 2 * n + g][0] = ws[g] * base

    blk = pl.BlockSpec((1, ts, dh), lambda h, i: (h, i, 0))
    col = pl.BlockSpec((1, ts, 1), lambda h, i: (h, i, 0))
    res = pl.pallas_call(
        body,
        name=name,
        grid=(nh, s // ts),
        in_specs=[blk, blk] + [col] * n,
        out_specs=[blk] * n + [col] * n,
        out_shape=[jax.ShapeDtypeStruct((nh, s, dh), MXU_DTYPE)] * n + [jax.ShapeDtypeStruct((nh, s, 1), F32)] * n,
        compiler_params=_params(("parallel", "parallel")),
    )(do, o, *lses)
    return res[:n], res[n:]


def _to_heads(a, dil=1):
    s, n = a.shape
    nh = n // HEAD_DIM
    a = a.reshape(s // dil, dil, nh, HEAD_DIM).transpose(1, 2, 0, 3)
    return a.reshape(dil * nh, s // dil, HEAD_DIM)


def _from_heads(a, dil=1):
    b, ls, c = a.shape
    nh = b // dil
    return a.reshape(dil, nh, ls, c).transpose(2, 0, 1, 3).reshape(ls * dil, nh * c)


def _restride(a, dil):
    nh, s, c = a.shape
    return a.reshape(nh, s // dil, dil, c).transpose(2, 0, 1, 3).reshape(dil * nh, s // dil, c)


def _unstride(a, dil):
    b, ls, c = a.shape
    nh = b // dil
    return a.reshape(dil, nh, ls, c).transpose(1, 2, 0, 3).reshape(nh, ls * dil, c)


ANY = pl.BlockSpec(memory_space=pl.ANY)


def _position():
    x, y, c = lax.axis_index("x"), lax.axis_index("y"), lax.axis_index("c")
    return x, y, c, [(1 - x, y), (x, 1 - y), (1 - x, 1 - y)]


def _remote(src, dst, send_sems, recv_sems, k, to):
    return pltpu.make_async_remote_copy(src_ref=src, dst_ref=dst, send_sem=send_sems.at[k], recv_sem=recv_sems.at[k], device_id=to, device_id_type=MESH)


def _gather_chips(arrs, name):
    n = len(arrs)

    def body(*refs):
        f_refs, g_refs = refs[:n], refs[n:2 * n]
        send_sems, recv_sems, local_sems = refs[2 * n:]
        x, y, c, chips = _position()
        me, sibling = (x, y, c), (x, y, 1 - c)

        def half(j, ref, sel):
            rows = f_refs[j].shape[1] // 2
            return ref.at[:, pl.ds(sel * rows, rows), :]

        def slot(j, chip, sel):
            return half(j, g_refs[j].at[2 * chip[0] + chip[1]], sel)

        mine = [pltpu.make_async_copy(f_refs[j], g_refs[j].at[2 * x + y], local_sems.at[j]) for j in range(n)]
        first = [_remote(half(j, f_refs[j], c), slot(j, (x, y), c), send_sems, recv_sems, 6 * j + k, (*chip, c)) for j in range(n) for k, chip in enumerate(chips)]
        for cp in mine + first:
            cp.start()
        passed = []
        for j in range(n):
            for k, chip in enumerate(chips):
                _remote(slot(j, chip, c), slot(j, chip, c), send_sems, recv_sems, 6 * j + k, me).wait_recv()
                passed.append(_remote(slot(j, chip, c), slot(j, chip, c), send_sems, recv_sems, 6 * j + 3 + k, sibling))
                passed[-1].start()
        for j in range(n):
            for k, chip in enumerate(chips):
                _remote(slot(j, chip, 1 - c), slot(j, chip, 1 - c), send_sems, recv_sems, 6 * j + 3 + k, me).wait_recv()
        for cp in first + passed:
            cp.wait_send()
        for cp in mine:
            cp.wait()

    return pl.pallas_call(
        body,
        name=name,
        in_specs=[ANY] * n,
        out_specs=[ANY] * n,
        out_shape=[jax.ShapeDtypeStruct((N_CHIPS,) + a.shape, a.dtype) for a in arrs],
        scratch_shapes=[pltpu.SemaphoreType.DMA((6 * n,)), pltpu.SemaphoreType.DMA((6 * n,)), pltpu.SemaphoreType.DMA((n,))],
    )(*arrs)


def _swap_sibling(arrs, name):
    n = len(arrs)

    def body(*refs):
        x, y, c, _ = _position()
        send_sems, recv_sems = refs[2 * n:]
        copies = [_remote(refs[j], refs[n + j], send_sems, recv_sems, j, (x, y, 1 - c)) for j in range(n)]
        for cp in copies:
            cp.start()
        for cp in copies:
            cp.wait()

    return pl.pallas_call(
        body,
        name=name,
        in_specs=[ANY] * n,
        out_specs=[ANY] * n,
        out_shape=[jax.ShapeDtypeStruct(a.shape, a.dtype) for a in arrs],
        scratch_shapes=[pltpu.SemaphoreType.DMA((n,)), pltpu.SemaphoreType.DMA((n,))],
    )(*arrs)


def _exchange_chips(parts, name):
    n = len(parts)

    def body(*refs):
        p_refs, q_refs = refs[:n], refs[n:2 * n]
        send_sems, recv_sems, local_sems = refs[2 * n:]
        x, y, c, chips = _position()
        my_slot = 2 * x + y
        mine = [pltpu.make_async_copy(p_refs[j].at[my_slot], q_refs[j].at[my_slot], local_sems.at[j]) for j in range(n)]
        sends = [_remote(p_refs[j].at[2 * px + py], q_refs[j].at[my_slot], send_sems, recv_sems, 3 * j + k, (px, py, c)) for j in range(n) for k, (px, py) in enumerate(chips)]
        for cp in mine + sends:
            cp.start()
        for j in range(n):
            for k, (px, py) in enumerate(chips):
                landed = q_refs[j].at[2 * px + py]
                _remote(landed, landed, send_sems, recv_sems, 3 * j + k, (x, y, c)).wait_recv()
        for cp in sends:
            cp.wait_send()
        for cp in mine:
            cp.wait()

    return pl.pallas_call(
        body,
        name=name,
        in_specs=[ANY] * n,
        out_specs=[ANY] * n,
        out_shape=[jax.ShapeDtypeStruct(a.shape, a.dtype) for a in parts],
        scratch_shapes=[pltpu.SemaphoreType.DMA((3 * n,)), pltpu.SemaphoreType.DMA((3 * n,)), pltpu.SemaphoreType.DMA((n,))],
    )(*parts)


def _share_sibling(halves, name):
    n = len(halves)

    def body(*refs):
        h_refs, o_refs = refs[:n], refs[n:2 * n]
        send_sems, recv_sems, local_sems = refs[2 * n:]
        x, y, c, _ = _position()

        def rows(j, sel):
            r = h_refs[j].shape[0]
            return o_refs[j].at[pl.ds(sel * r, r), :]

        mine = [pltpu.make_async_copy(h_refs[j], rows(j, c), local_sems.at[j]) for j in range(n)]
        sends = [_remote(h_refs[j], rows(j, c), send_sems, recv_sems, j, (x, y, 1 - c)) for j in range(n)]
        for cp in mine + sends:
            cp.start()
        for j in range(n):
            _remote(rows(j, 1 - c), rows(j, 1 - c), send_sems, recv_sems, j, (x, y, c)).wait_recv()
        for cp in sends:
            cp.wait_send()
        for cp in mine:
            cp.wait()

    return pl.pallas_call(
        body,
        name=name,
        in_specs=[ANY] * n,
        out_specs=[ANY] * n,
        out_shape=[jax.ShapeDtypeStruct((2 * a.shape[0],) + a.shape[1:], a.dtype) for a in halves],
        scratch_shapes=[pltpu.SemaphoreType.DMA((n,)), pltpu.SemaphoreType.DMA((n,)), pltpu.SemaphoreType.DMA((n,))],
    )(*halves)


def _sum_rows(r, c, n_in):
    return _tile(r, max(8, (1 << 20) // (c * (n_in + 1))), 8)


def _add2(a, b, name):
    n, r, c = a.shape
    tr = _sum_rows(r, c, 2)

    def body(a_ref, b_ref, o_ref):
        o_ref[...] = a_ref[...] + b_ref[...]

    blk = pl.BlockSpec((1, tr, c), lambda s, i: (s, i, 0))
    return pl.pallas_call(
        body,
        name=name,
        grid=(n, r // tr),
        in_specs=[blk, blk],
        out_specs=blk,
        out_shape=jax.ShapeDtypeStruct(a.shape, a.dtype),
        compiler_params=_params(("parallel", "parallel")),
    )(a, b)


def _sum_slots(q, name):
    n, r, c = q.shape
    tr = _sum_rows(r, c, n)

    def body(q_ref, o_ref):
        acc = q_ref[0]
        for s in range(1, n):
            acc = acc + q_ref[s]
        o_ref[...] = acc

    return pl.pallas_call(
        body,
        name=name,
        grid=(r // tr,),
        in_specs=[pl.BlockSpec((n, tr, c), lambda i: (0, i, 0))],
        out_specs=pl.BlockSpec((tr, c), lambda i: (i, 0)),
        out_shape=jax.ShapeDtypeStruct((r, c), q.dtype),
        compiler_params=_params(("parallel",)),
    )(q)


def _adamw(w, g, m, v, name):
    r, c = w.shape
    tr = _tile(r, 256, 8)
    c1 = 1.0 - ADAM_B1 ** ADAM_STEP
    c2 = 1.0 - ADAM_B2 ** ADAM_STEP

    def body(w_ref, g_ref, m_ref, v_ref, d_ref, nm_ref, nv_ref):
        g = g_ref[...]
        nm = ADAM_B1 * m_ref[...] + (1.0 - ADAM_B1) * g
        nv = ADAM_B2 * v_ref[...] + (1.0 - ADAM_B2) * (g * g)
        d_ref[...] = -ADAM_LR * ((nm / c1) / (jnp.sqrt(nv / c2) + ADAM_EPS) + ADAM_WD * w_ref[...])
        nm_ref[...] = nm
        nv_ref[...] = nv

    blk = pl.BlockSpec((tr, c), lambda i: (i, 0))
    shape = jax.ShapeDtypeStruct((r, c), F32)
    return pl.pallas_call(
        body,
        name=name,
        grid=(r // tr,),
        in_specs=[blk] * 4,
        out_specs=[blk] * 3,
        out_shape=[shape] * 3,
        compiler_params=_params(("parallel",)),
    )(w, g, m, v)


WEIGHTS = ("ffn1_w_in", "ffn1_w_out", "ffn2_w_in", "ffn2_w_out", "ln_g", "ln_b", "sb_w_in", "sb_w_out", "swa_w_in", "swa_sinks", "swa_w_out", "dil_w_in", "dil_w_out", "ple_w_proj", "ple_w_gate")
SHARD_AXIS = {"ffn1_w_in": 2, "ffn1_w_out": 1, "ffn2_w_in": 2, "ffn2_w_out": 1, "ln_g": 2, "ln_b": 2, "sb_w_in": 2, "sb_w_out": 1, "swa_w_in": 2, "swa_sinks": None, "swa_w_out": 1, "dil_w_in": 2, "dil_w_out": 1, "ple_w_proj": 2, "ple_w_gate": 1}
SMALL = ("ln_g", "ln_b", "swa_sinks")
SMALL_COLS = 128
SMALL_UNIT = 16 * SMALL_COLS


def _pack_small(pieces, lead):
    flat = jnp.concatenate([a.reshape(lead + (-1,)) for a in pieces], axis=-1)
    n = flat.shape[-1]
    flat = jnp.pad(flat, [(0, 0)] * len(lead) + [(0, -n % SMALL_UNIT)])
    return flat.reshape(lead + (-1, SMALL_COLS))


def _unpack_small(buf, shapes, lead):
    flat = buf.reshape(lead + (-1,))
    out, off = [], 0
    for shp in shapes:
        n = math.prod(shp)
        out.append(flat[..., off:off + n].reshape(lead + tuple(shp)))
        off += n
    return out


def _reduce_group(grads, c, tag):
    keep = [lax.dynamic_slice_in_dim(g, c * (g.shape[1] // 2), g.shape[1] // 2, axis=1) for g in grads]
    give = [lax.dynamic_slice_in_dim(g, (1 - c) * (g.shape[1] // 2), g.shape[1] // 2, axis=1) for g in grads]
    got = _swap_sibling(give, f"{tag}_pair")
    part = [_add2(k, g, f"{tag}_pair_sum{j}") for j, (k, g) in enumerate(zip(keep, got))]
    landed = _exchange_chips(part, f"{tag}_chips")
    halves = [_sum_slots(q, f"{tag}_chips_sum{j}") for j, q in enumerate(landed)]
    return _share_sibling(halves, f"{tag}_share")


def _full_from_shards(g, axis):
    g = jnp.moveaxis(g, 0, axis)
    return g.reshape(g.shape[:axis] + (g.shape[axis] * g.shape[axis + 1],) + g.shape[axis + 2:])


def _shards_from_full(a, axis):
    a = a.reshape(a.shape[:axis] + (N_CHIPS, a.shape[axis] // N_CHIPS) + a.shape[axis + 1:])
    return jnp.moveaxis(a, axis, 0)


def _ffn_forward(x, xb, w_in, w_out, g, b, alpha, tag):
    gate, up, act = _ffn_in(xb, w_in, f"{tag}_in")
    y = _mm(act, w_out, "nn", F32, f"{tag}_out")
    out, outb, xhat, rstd = _ln_fwd(x, y, g, b, alpha, 0.5, f"{tag}_ln")
    return (out, outb), dict(xb=xb, gate=gate, up=up, act=act, xhat=xhat, rstd=rstd)


def _ffn_backward(dout, saved, w_in, w_out, g, tag):
    dr, dyb, dg, db = _ln_bwd(*dout, saved["xhat"], saved["rstd"], g, 0.5, f"{tag}_ln_bwd")
    dh = _ffn_dact(dyb, w_out, saved["gate"], saved["up"], f"{tag}_dact")
    dw_out = _mm(saved["act"], dyb, "tn", F32, f"{tag}_dw_out")
    dw_in = _mm(saved["xb"], dh, "tn", F32, f"{tag}_dw_in", split_b=True)
    dxb = _mm(dh, w_in, "nt", F32, f"{tag}_dx", split_a=True)
    return dr, dxb, dw_in, dw_out, dg, db


def _sb_forward(xb, w_in, w_out, tag):
    nw = w_out.shape[0]
    h = _mm(xb, w_in, "nn", MXU_DTYPE, f"{tag}_qkv")
    q, k, v = (_to_heads(h[:, j * nw:(j + 1) * nw]) for j in range(3))
    o, ob = _sb_fwd(q, k, v, f"{tag}_att")
    ob = _from_heads(ob)
    return _mm(ob, w_out, "nn", F32, f"{tag}_proj"), dict(q=q, k=k, v=v, o=o, ob=ob)


def _sb_backward(dmix, saved, xb, w_in, w_out, tag):
    dw_out = _mm(saved["ob"], dmix, "tn", F32, f"{tag}_dw_out")
    do = _to_heads(_mm(dmix, w_out, "nt", MXU_DTYPE, f"{tag}_do"))
    dq, dk, dv = _sb_bwd(saved["q"], saved["k"], saved["v"], saved["o"], do, f"{tag}_att_bwd")
    dh = jnp.concatenate([_from_heads(t) for t in (dq, dk, dv)], axis=1).astype(MXU_DTYPE)
    dw_in = _mm(xb, dh, "tn", F32, f"{tag}_dw_in")
    return _mm(dh, w_in, "nt", F32, f"{tag}_dx"), dw_in, dw_out


def _swa_forward(xb, w_in, sinks, w_out, tables, tag):
    nq = w_out.shape[0]
    nkv = (w_in.shape[1] - nq) // 2
    h = _mm(xb, w_in, "nn", F32, f"{tag}_qkv")
    n_rot = (nq + nkv) // 128
    hb = _rope(h, *tables, lambda j: j // n_rot, MXU_DTYPE, f"{tag}_rope")
    q, k, v = _to_heads(hb[:, :nq]), _to_heads(hb[:, nq:nq + nkv]), _to_heads(hb[:, nq + nkv:])
    o, lse = _band_fwd(q, k, v, sinks, SWA_WINDOW - 1, f"{tag}_att")
    o, ob = _merge_fwd([o], [lse], f"{tag}_cast")
    ob = _from_heads(ob)
    return _mm(ob, w_out, "nn", F32, f"{tag}_proj"), dict(q=q, k=k, v=v, o=o, lse=lse, ob=ob, n_rot=n_rot)


def _swa_backward(dmix, saved, xb, w_in, sinks, w_out, tables, tag):
    dw_out = _mm(saved["ob"], dmix, "tn", F32, f"{tag}_dw_out")
    do = _to_heads(_mm(dmix, w_out, "nt", MXU_DTYPE, f"{tag}_do"))
    (dog,), (delta,) = _merge_bwd(do, saved["o"], [saved["lse"]], f"{tag}_delta")
    dq, dk, dv, dsink = _band_bwd(saved["q"], saved["k"], saved["v"], dog, saved["lse"], delta, sinks, SWA_WINDOW - 1, f"{tag}_att_bwd")
    dh = jnp.concatenate([_from_heads(t) for t in (dq, dk, dv)], axis=1)
    n_rot = saved["n_rot"]
    dhb = _rope(dh, tables[0], -tables[1], lambda j: j // n_rot, MXU_DTYPE, f"{tag}_rope_bwd")
    dw_in = _mm(xb, dhb, "tn", F32, f"{tag}_dw_in")
    return _mm(dhb, w_in, "nt", F32, f"{tag}_dx"), dw_in, dw_out, dsink[:, 0, 0]


def _dil_forward(xb, w_in, w_out, tables, tag):
    nw = w_out.shape[0]
    h = _mm(xb, w_in, "nn", F32, f"{tag}_qkv")
    hb = _rope(h, *tables, lambda j: (j % 3) // 2, MXU_DTYPE, f"{tag}_rope", width=nw)
    qkv, outs, lses = [], [], []
    for gi, (win, dil) in enumerate(DIL_GROUPS):
        base = gi * 3 * nw
        q, k, v = (_to_heads(hb[:, base + j * nw:base + (j + 1) * nw], dil) for j in range(3))
        o, lse = _band_fwd(q, k, v, None, win // dil, f"{tag}_att{gi}")
        qkv.append((q, k, v))
        outs.append(_unstride(o, dil))
        lses.append(_unstride(lse, dil))
    o, ob = _merge_fwd(outs, lses, f"{tag}_merge")
    ob = _from_heads(ob)
    return _mm(ob, w_out, "nn", F32, f"{tag}_proj"), dict(qkv=qkv, o=o, lses=lses, ob=ob)


def _dil_backward(dmix, saved, xb, w_in, w_out, tables, tag):
    dw_out = _mm(saved["ob"], dmix, "tn", F32, f"{tag}_dw_out")
    do = _to_heads(_mm(dmix, w_out, "nt", MXU_DTYPE, f"{tag}_do"))
    dogs, deltas = _merge_bwd(do, saved["o"], saved["lses"], f"{tag}_merge_bwd")
    parts = []
    for gi, (win, dil) in enumerate(DIL_GROUPS):
        q, k, v = saved["qkv"][gi]
        dq, dk, dv, _ = _band_bwd(q, k, v, _restride(dogs[gi], dil), _restride(saved["lses"][gi], dil), _restride(deltas[gi], dil), None, win // dil, f"{tag}_att{gi}_bwd")
        parts += [_from_heads(t, dil) for t in (dq, dk, dv)]
    dh = jnp.concatenate(parts, axis=1)
    dhb = _rope(dh, tables[0], -tables[1], lambda j: (j % 3) // 2, MXU_DTYPE, f"{tag}_rope_bwd", width=w_out.shape[0])
    dw_in = _mm(xb, dhb, "tn", F32, f"{tag}_dw_in")
    return _mm(dhb, w_in, "nt", F32, f"{tag}_dx"), dw_in, dw_out


def kernel(x, p, ffn1_w_in, ffn1_w_out, ffn2_w_in, ffn2_w_out, ln_g, ln_b, sb_w_in, sb_w_out, swa_w_in, swa_sinks, swa_w_out, dil_w_in, dil_w_out, ple_w_proj, ple_w_gate, loss_target, m_ffn1_w_in, m_ffn1_w_out, m_ffn2_w_in, m_ffn2_w_out, m_ln_g, m_ln_b, m_sb_w_in, m_sb_w_out, m_swa_w_in, m_swa_sinks, m_swa_w_out, m_dil_w_in, m_dil_w_out, m_ple_w_proj, m_ple_w_gate, v_ffn1_w_in, v_ffn1_w_out, v_ffn2_w_in, v_ffn2_w_out, v_ln_g, v_ln_b, v_sb_w_in, v_sb_w_out, v_swa_w_in, v_swa_sinks, v_swa_w_out, v_dil_w_in, v_dil_w_out, v_ple_w_proj, v_ple_w_gate):
    shard = dict(ffn1_w_in=ffn1_w_in, ffn1_w_out=ffn1_w_out, ffn2_w_in=ffn2_w_in, ffn2_w_out=ffn2_w_out, ln_g=ln_g, ln_b=ln_b, sb_w_in=sb_w_in, sb_w_out=sb_w_out, swa_w_in=swa_w_in, swa_sinks=swa_sinks, swa_w_out=swa_w_out, dil_w_in=dil_w_in, dil_w_out=dil_w_out, ple_w_proj=ple_w_proj, ple_w_gate=ple_w_gate)
    mom_m = dict(ffn1_w_in=m_ffn1_w_in, ffn1_w_out=m_ffn1_w_out, ffn2_w_in=m_ffn2_w_in, ffn2_w_out=m_ffn2_w_out, ln_g=m_ln_g, ln_b=m_ln_b, sb_w_in=m_sb_w_in, sb_w_out=m_sb_w_out, swa_w_in=m_swa_w_in, swa_sinks=m_swa_sinks, swa_w_out=m_swa_w_out, dil_w_in=m_dil_w_in, dil_w_out=m_dil_w_out, ple_w_proj=m_ple_w_proj, ple_w_gate=m_ple_w_gate)
    mom_v = dict(ffn1_w_in=v_ffn1_w_in, ffn1_w_out=v_ffn1_w_out, ffn2_w_in=v_ffn2_w_in, ffn2_w_out=v_ffn2_w_out, ln_g=v_ln_g, ln_b=v_ln_b, sb_w_in=v_sb_w_in, sb_w_out=v_sb_w_out, swa_w_in=v_swa_w_in, swa_sinks=v_swa_sinks, swa_w_out=v_swa_w_out, dil_w_in=v_dil_w_in, dil_w_out=v_dil_w_out, ple_w_proj=v_ple_w_proj, ple_w_gate=v_ple_w_gate)
    depth = ffn1_w_in.shape[0]
    alpha = (2 * depth) ** 0.25
    c = lax.axis_index("c")

    mats = [n for n in WEIGHTS if n not in SMALL]
    got = _gather_chips([shard[n].astype(MXU_DTYPE) for n in mats] + [_pack_small([ln_g, ln_b], ())[None]], "gather_weights")
    full = {"swa_sinks": swa_sinks}
    for n, g in zip(mats, got):
        full[n] = _full_from_shards(g, SHARD_AXIS[n])
    for n, g in zip(("ln_g", "ln_b"), _unpack_small(got[-1][:, 0], [ln_g.shape, ln_b.shape], (N_CHIPS,))):
        full[n] = _full_from_shards(g, SHARD_AXIS[n])

    seq = x.shape[1]
    tables = _rope_tables(seq)
    xf = x[0]
    xb = xf.astype(MXU_DTYPE)
    saved = []
    for i in range(depth):
        kind, j = i % 3, i // 3
        sv = {}
        (x1, x1b), sv["ffn1"] = _ffn_forward(xf, xb, full["ffn1_w_in"][i], full["ffn1_w_out"][i], full["ln_g"][i, 0], full["ln_b"][i, 0], alpha, f"l{i}_ffn1")
        if kind == 0:
            mix, sv["mix"] = _sb_forward(x1b, full["sb_w_in"][j], full["sb_w_out"][j], f"l{i}_sb")
        elif kind == 1:
            mix, sv["mix"] = _swa_forward(x1b, full["swa_w_in"][j], swa_sinks[j], full["swa_w_out"][j], tables, f"l{i}_swa")
        else:
            mix, sv["mix"] = _dil_forward(x1b, full["dil_w_in"][j], full["dil_w_out"][j], tables, f"l{i}_dil")
        x2, x2b, sv["xhat2"], sv["rstd2"] = _ln_fwd(x1, mix, full["ln_g"][i, 1], full["ln_b"][i, 1], alpha, 1.0, f"l{i}_mix_ln")
        sv["x1b"] = x1b
        (x3, x3b), sv["ffn2"] = _ffn_forward(x2, x2b, full["ffn2_w_in"][i], full["ffn2_w_out"][i], full["ln_g"][i, 2], full["ln_b"][i, 2], alpha, f"l{i}_ffn2")
        xf, xb, sv["u"], sv["e"] = _ple_fwd(x3, x3b, p[i, 0], full["ple_w_gate"][i], full["ple_w_proj"][i], f"l{i}_ple")
        sv["x3b"] = x3b
        saved.append(sv)

    loss_part, dy = _loss(xf, loss_target[0], "loss")
    loss = lax.psum(loss_part[0, 0], ("x", "y", "c"))

    grads = {n: [None] * full[n].shape[0] for n in WEIGHTS if n not in ("ln_g", "ln_b")}
    gsum = {n: [None] * full[n].shape[0] for n in mats}
    dln_g = [[None] * 3 for _ in range(depth)]
    dln_b = [[None] * 3 for _ in range(depth)]
    dout = (dy, None, 1.0)
    for i in reversed(range(depth)):
        kind, j = i % 3, i // 3
        mixer = ("sb", "swa", "dil")[kind]
        sv = saved[i]
        dx4, dub, deb = _ple_bwd(*dout, sv["u"], sv["e"], f"l{i}_ple_bwd")
        grads["ple_w_gate"][i] = _mm(sv["x3b"], dub, "tn", F32, f"l{i}_ple_dw_gate")
        grads["ple_w_proj"][i] = _mm(p[i, 0], deb, "tn", F32, f"l{i}_ple_dw_proj")
        dxb = _mm(dub, full["ple_w_gate"][i], "nt", F32, f"l{i}_ple_dx")
        dr, dxb, grads["ffn2_w_in"][i], grads["ffn2_w_out"][i], dln_g[i][2], dln_b[i][2] = _ffn_backward((dx4, dxb, 1.0), sv["ffn2"], full["ffn2_w_in"][i], full["ffn2_w_out"][i], full["ln_g"][i, 2], f"l{i}_ffn2")
        dr, dmix, dln_g[i][1], dln_b[i][1] = _ln_bwd(dr, dxb, alpha, sv["xhat2"], sv["rstd2"], full["ln_g"][i, 1], 1.0, f"l{i}_mix_ln_bwd")
        if kind == 0:
            dxb, grads["sb_w_in"][j], grads["sb_w_out"][j] = _sb_backward(dmix, sv["mix"], sv["x1b"], full["sb_w_in"][j], full["sb_w_out"][j], f"l{i}_sb")
        elif kind == 1:
            dxb, grads["swa_w_in"][j], grads["swa_w_out"][j], grads["swa_sinks"][j] = _swa_backward(dmix, sv["mix"], sv["x1b"], full["swa_w_in"][j], swa_sinks[j], full["swa_w_out"][j], tables, f"l{i}_swa")
        else:
            dxb, grads["dil_w_in"][j], grads["dil_w_out"][j] = _dil_backward(dmix, sv["mix"], sv["x1b"], full["dil_w_in"][j], full["dil_w_out"][j], tables, f"l{i}_dil")
        dr, dxb, grads["ffn1_w_in"][i], grads["ffn1_w_out"][i], dln_g[i][0], dln_b[i][0] = _ffn_backward((dr, dxb, alpha), sv["ffn1"], full["ffn1_w_in"][i], full["ffn1_w_out"][i], full["ln_g"][i, 0], f"l{i}_ffn1")
        dout = (dr, dxb, alpha)
        layer = [("ffn1_w_in", i), ("ffn1_w_out", i), (f"{mixer}_w_in", j), (f"{mixer}_w_out", j), ("ffn2_w_in", i), ("ffn2_w_out", i), ("ple_w_proj", i), ("ple_w_gate", i)]
        summed = _reduce_group([_shards_from_full(grads[n][k], SHARD_AXIS[n] - 1) for n, k in layer], c, f"l{i}_reduce")
        for (n, k), g in zip(layer, summed):
            gsum[n][k] = g
    grad_x = _axpy(*dout, "grad_x")[None]

    gshard = {n: jnp.stack(g) for n, g in gsum.items()}
    small = [
        _shards_from_full(jnp.stack([jnp.concatenate(r, axis=0) for r in dln_g]), SHARD_AXIS["ln_g"]),
        _shards_from_full(jnp.stack([jnp.concatenate(r, axis=0) for r in dln_b]), SHARD_AXIS["ln_b"]),
        jnp.broadcast_to(jnp.stack(grads["swa_sinks"])[None], (N_CHIPS,) + swa_sinks.shape),
    ]
    (small_sum,) = _reduce_group([_pack_small(small, (N_CHIPS,))], c, "small_reduce")
    for n, g in zip(SMALL, _unpack_small(small_sum, [shard[n].shape for n in SMALL], ())):
        gshard[n] = g

    delta, new_m, new_v = {}, {}, {}
    for n in WEIGHTS:
        shp = shard[n].shape
        two_d = (-1, shp[-1])
        d, nm, nv = _adamw(shard[n].reshape(two_d), gshard[n].reshape(two_d), mom_m[n].reshape(two_d), mom_v[n].reshape(two_d), f"adamw_{n}")
        delta[n], new_m[n], new_v[n] = d.reshape(shp), nm.reshape(shp), nv.reshape(shp)

    return (loss, grad_x, *[gshard[n] for n in WEIGHTS], *[delta[n] for n in WEIGHTS], *[new_m[n] for n in WEIGHTS], *[new_v[n] for n in WEIGHTS])
```

```python
import functools
import math

import jax
import jax.numpy as jnp
from jax import lax
from jax.experimental import pallas as pl
from jax.experimental.pallas import tpu as pltpu

F32 = jnp.float32
MXU_DTYPE = jnp.bfloat16
MESH = pl.DeviceIdType.MESH

HEAD_DIM = 64
ATT_BLK = 128
SWA_WINDOW = 128
DIL_GROUPS = ((128, 1), (512, 4), (2048, 16))
LN_EPS = 1e-5
ROPE_THETA = 10000.0
NEG_INF = -1e30
ADAM_LR, ADAM_B1, ADAM_B2, ADAM_EPS, ADAM_WD, ADAM_STEP = 0.001, 0.9, 0.999, 1e-08, 0.01, 10

VMEM_LIMIT_BYTES = 56 * 1024 * 1024
N_CHIPS = 4


def _params(sem=None):
    return pltpu.CompilerParams(dimension_semantics=sem, vmem_limit_bytes=VMEM_LIMIT_BYTES)


def _tile(n, target, unit):
    t = (min(target, n) // unit) * unit
    while t >= unit:
        if n % t == 0:
            return t
        t -= unit
    return n


def _dot(a, b, dims):
    return lax.dot_general(a.astype(MXU_DTYPE), b.astype(MXU_DTYPE), (dims, ((), ())), preferred_element_type=F32)


NN = ((1,), (0,))
NT = ((1,), (1,))
TN = ((0,), (0,))


def _mm(a, b, mode, out_dtype, name, split_a=False, split_b=False, tm=1024, tn=1408, tk=1408):
    dims = {"nn": NN, "nt": NT, "tn": TN}[mode]
    if split_a:
        m, k = a.shape[1], 2 * a.shape[2]
    elif mode == "tn":
        k, m = a.shape
    else:
        m, k = a.shape
    if split_b:
        n = 2 * b.shape[2]
    elif mode == "nt":
        n = b.shape[0]
    else:
        n = b.shape[1]
    tm = _tile(m, tm, 128)
    tn = _tile(n // 2 if split_b else n, tn, 128)
    tk = _tile(k // 2 if split_a else k, tk, 128)
    nk = k // tk
    nk_half = nk // 2
    nn_half = (n // tn) // 2

    if split_a:
        a_spec = pl.BlockSpec((None, tm, tk), lambda i, j, kk: (kk // nk_half, i, kk % nk_half))
    elif mode == "tn":
        a_spec = pl.BlockSpec((tk, tm), lambda i, j, kk: (kk, i))
    else:
        a_spec = pl.BlockSpec((tm, tk), lambda i, j, kk: (i, kk))
    if split_b:
        b_spec = pl.BlockSpec((None, tk, tn), lambda i, j, kk: (j // nn_half, kk, j % nn_half))
    elif mode == "nt":
        b_spec = pl.BlockSpec((tn, tk), lambda i, j, kk: (j, kk))
    else:
        b_spec = pl.BlockSpec((tk, tn), lambda i, j, kk: (kk, j))

    def body(a_ref, b_ref, o_ref, acc_ref):
        kk = pl.program_id(2)

        @pl.when(kk == 0)
        def _():
            acc_ref[...] = jnp.zeros_like(acc_ref)

        acc_ref[...] += _dot(a_ref[...], b_ref[...], dims)

        @pl.when(kk == nk - 1)
        def _():
            o_ref[...] = acc_ref[...].astype(o_ref.dtype)

    return pl.pallas_call(
        body,
        name=name,
        grid=(m // tm, n // tn, nk),
        in_specs=[a_spec, b_spec],
        out_specs=pl.BlockSpec((tm, tn), lambda i, j, kk: (i, j)),
        out_shape=jax.ShapeDtypeStruct((m, n), out_dtype),
        scratch_shapes=[pltpu.VMEM((tm, tn), F32)],
        compiler_params=_params(("parallel", "parallel", "arbitrary")),
    )(a, b)


def _sigmoid(x):
    return 1.0 / (1.0 + jnp.exp(-x))


def _ffn_in(xb, w_in, name):
    s, d = xb.shape
    f = w_in.shape[1] // 2
    tm = _tile(s, 1024, 128)
    tn = _tile(f, 256, 128)
    nj = f // tn

    def body(x_ref, wg_ref, wu_ref, g_ref, u_ref, a_ref):
        x = x_ref[...]
        g = _dot(x, wg_ref[...], NN)
        u = _dot(x, wu_ref[...], NN)
        g_ref[...] = g
        u_ref[...] = u
        a_ref[...] = (g * _sigmoid(g) * u).astype(a_ref.dtype)

    out = pl.BlockSpec((tm, tn), lambda i, j: (i, j))
    return pl.pallas_call(
        body,
        name=name,
        grid=(s // tm, nj),
        in_specs=[
            pl.BlockSpec((tm, d), lambda i, j: (i, 0)),
            pl.BlockSpec((d, tn), lambda i, j: (0, j)),
            pl.BlockSpec((d, tn), lambda i, j: (0, j + nj)),
        ],
        out_specs=[out, out, out],
        out_shape=[
            jax.ShapeDtypeStruct((s, f), F32),
            jax.ShapeDtypeStruct((s, f), F32),
            jax.ShapeDtypeStruct((s, f), MXU_DTYPE),
        ],
        compiler_params=_params(("parallel", "parallel")),
    )(xb, w_in, w_in)


def _ffn_dact(dyb, w_out, gate, up, name):
    s, d = dyb.shape
    f = w_out.shape[0]
    tm = _tile(s, 1024, 128)
    tn = _tile(f, 256, 128)

    def body(dy_ref, w_ref, g_ref, u_ref, o_ref):
        dact = _dot(dy_ref[...], w_ref[...], NT)
        g = g_ref[...]
        sig = _sigmoid(g)
        o_ref[0] = (dact * u_ref[...] * (sig * (1.0 + g * (1.0 - sig)))).astype(o_ref.dtype)
        o_ref[1] = (dact * (g * sig)).astype(o_ref.dtype)

    tile = pl.BlockSpec((tm, tn), lambda i, j: (i, j))
    return pl.pallas_call(
        body,
        name=name,
        grid=(s // tm, f // tn),
        in_specs=[
            pl.BlockSpec((tm, d), lambda i, j: (i, 0)),
            pl.BlockSpec((tn, d), lambda i, j: (j, 0)),
            tile,
            tile,
        ],
        out_specs=pl.BlockSpec((2, tm, tn), lambda i, j: (0, i, j)),
        out_shape=jax.ShapeDtypeStruct((2, s, f), MXU_DTYPE),
        compiler_params=_params(("parallel", "parallel")),
    )(dyb, w_out, gate, up)


def _ple_fwd(x, xb, p, w_gate, w_proj, name):
    s, d = x.shape
    pd = p.shape[1]
    tm = _tile(s, 1024, 128)
    tn = _tile(d, 512, 128)

    def body(x_ref, xb_ref, p_ref, wg_ref, wp_ref, o_ref, ob_ref, u_ref, e_ref):
        u = _dot(xb_ref[...], wg_ref[...], NN)
        e = _dot(p_ref[...], wp_ref[...], NN)
        out = x_ref[...] + _sigmoid(u) * e
        o_ref[...] = out
        ob_ref[...] = out.astype(ob_ref.dtype)
        u_ref[...] = u
        e_ref[...] = e

    tile = pl.BlockSpec((tm, tn), lambda i, j: (i, j))
    return pl.pallas_call(
        body,
        name=name,
        grid=(s // tm, d // tn),
        in_specs=[
            tile,
            pl.BlockSpec((tm, d), lambda i, j: (i, 0)),
            pl.BlockSpec((tm, pd), lambda i, j: (i, 0)),
            pl.BlockSpec((d, tn), lambda i, j: (0, j)),
            pl.BlockSpec((pd, tn), lambda i, j: (0, j)),
        ],
        out_specs=[tile, tile, tile, tile],
        out_shape=[
            jax.ShapeDtypeStruct((s, d), F32),
            jax.ShapeDtypeStruct((s, d), MXU_DTYPE),
            jax.ShapeDtypeStruct((s, d), F32),
            jax.ShapeDtypeStruct((s, d), F32),
        ],
        compiler_params=_params(("parallel", "parallel")),
    )(x, xb, p, w_gate, w_proj)


def _rows_spec(ts, d):
    return pl.BlockSpec((ts, d), lambda i: (i, 0))


def _ln_fwd(x, y, g, b, alpha, beta, name):
    s, d = x.shape
    ts = _tile(s, 512, 8)

    def body(x_ref, y_ref, g_ref, b_ref, o_ref, ob_ref, xh_ref, rs_ref):
        r = alpha * x_ref[...] + beta * y_ref[...]
        mu = jnp.mean(r, axis=1, keepdims=True)
        cen = r - mu
        var = jnp.mean(cen * cen, axis=1, keepdims=True)
        rstd = lax.rsqrt(var + LN_EPS)
        xhat = cen * rstd
        out = xhat * g_ref[...] + b_ref[...]
        o_ref[...] = out
        ob_ref[...] = out.astype(ob_ref.dtype)
        xh_ref[...] = xhat
        rs_ref[...] = rstd

    vec = pl.BlockSpec((1, d), lambda i: (0, 0))
    return pl.pallas_call(
        body,
        name=name,
        grid=(s // ts,),
        in_specs=[_rows_spec(ts, d), _rows_spec(ts, d), vec, vec],
        out_specs=[_rows_spec(ts, d), _rows_spec(ts, d), _rows_spec(ts, d), _rows_spec(ts, 1)],
        out_shape=[
            jax.ShapeDtypeStruct((s, d), F32),
            jax.ShapeDtypeStruct((s, d), MXU_DTYPE),
            jax.ShapeDtypeStruct((s, d), F32),
            jax.ShapeDtypeStruct((s, 1), F32),
        ],
        compiler_params=_params(("parallel",)),
    )(x, y, g.reshape(1, d), b.reshape(1, d))


def _ln_bwd(ga, gb, ca, xhat, rstd, g, beta, name):
    s, d = xhat.shape
    ts = _tile(s, 512, 8)

    def body(ga_ref, gb_ref, xh_ref, rs_ref, g_ref, dr_ref, dyb_ref, dg_ref, db_ref):
        @pl.when(pl.program_id(0) == 0)
        def _():
            dg_ref[...] = jnp.zeros_like(dg_ref)
            db_ref[...] = jnp.zeros_like(db_ref)

        dout = ca * ga_ref[...] + gb_ref[...]
        xhat = xh_ref[...]
        dg_ref[...] += jnp.sum(dout * xhat, axis=0, keepdims=True)
        db_ref[...] += jnp.sum(dout, axis=0, keepdims=True)
        dxh = dout * g_ref[...]
        m1 = jnp.mean(dxh, axis=1, keepdims=True)
        m2 = jnp.mean(dxh * xhat, axis=1, keepdims=True)
        dr = rs_ref[...] * (dxh - m1 - xhat * m2)
        dr_ref[...] = dr
        dyb_ref[...] = (beta * dr).astype(dyb_ref.dtype)

    vec = pl.BlockSpec((1, d), lambda i: (0, 0))
    return pl.pallas_call(
        body,
        name=name,
        grid=(s // ts,),
        in_specs=[_rows_spec(ts, d), _rows_spec(ts, d), _rows_spec(ts, d), _rows_spec(ts, 1), vec],
        out_specs=[_rows_spec(ts, d), _rows_spec(ts, d), vec, vec],
        out_shape=[
            jax.ShapeDtypeStruct((s, d), F32),
            jax.ShapeDtypeStruct((s, d), MXU_DTYPE),
            jax.ShapeDtypeStruct((1, d), F32),
            jax.ShapeDtypeStruct((1, d), F32),
        ],
        compiler_params=_params(("arbitrary",)),
    )(ga, gb, xhat, rstd, g.reshape(1, d))


def _ple_bwd(ga, gb, ca, u, e, name):
    s, d = u.shape
    ts = _tile(s, 512, 8)
    grads = [ga] if gb is None else [ga, gb]

    def body(*refs):
        u_ref, e_ref, dx_ref, du_ref, de_ref = refs[len(grads):]
        dx = ca * refs[0][...]
        if gb is not None:
            dx = dx + refs[1][...]
        sig = _sigmoid(u_ref[...])
        dx_ref[...] = dx
        du_ref[...] = (dx * e_ref[...] * sig * (1.0 - sig)).astype(du_ref.dtype)
        de_ref[...] = (dx * sig).astype(de_ref.dtype)

    return pl.pallas_call(
        body,
        name=name,
        grid=(s // ts,),
        in_specs=[_rows_spec(ts, d)] * (len(grads) + 2),
        out_specs=[_rows_spec(ts, d)] * 3,
        out_shape=[
            jax.ShapeDtypeStruct((s, d), F32),
            jax.ShapeDtypeStruct((s, d), MXU_DTYPE),
            jax.ShapeDtypeStruct((s, d), MXU_DTYPE),
        ],
        compiler_params=_params(("parallel",)),
    )(*grads, u, e)


def _axpy(ga, gb, ca, name):
    s, d = ga.shape
    ts = _tile(s, 512, 8)

    def body(ga_ref, gb_ref, o_ref):
        o_ref[...] = ca * ga_ref[...] + gb_ref[...]

    return pl.pallas_call(
        body,
        name=name,
        grid=(s // ts,),
        in_specs=[_rows_spec(ts, d)] * 2,
        out_specs=_rows_spec(ts, d),
        out_shape=jax.ShapeDtypeStruct((s, d), F32),
        compiler_params=_params(("parallel",)),
    )(ga, gb)


def _loss(y, target, name):
    s, d = y.shape
    ts = _tile(s, 512, 8)

    def body(y_ref, t_ref, l_ref, dy_ref):
        @pl.when(pl.program_id(0) == 0)
        def _():
            l_ref[...] = jnp.zeros_like(l_ref)

        err = y_ref[...] - t_ref[...]
        l_ref[...] += (0.5 / d) * jnp.sum(jnp.sum(err * err, axis=1, keepdims=True), axis=0, keepdims=True)
        dy_ref[...] = err * (1.0 / d)

    return pl.pallas_call(
        body,
        name=name,
        grid=(s // ts,),
        in_specs=[_rows_spec(ts, d)] * 2,
        out_specs=[pl.BlockSpec((1, 1), lambda i: (0, 0)), _rows_spec(ts, d)],
        out_shape=[jax.ShapeDtypeStruct((1, 1), F32), jax.ShapeDtypeStruct((s, d), F32)],
        compiler_params=_params(("arbitrary",)),
    )(y, target)


def _rope_tables(seq):
    pos = jnp.arange(seq, dtype=F32)
    inv = ROPE_THETA ** (-jnp.arange(0, HEAD_DIM, 2, dtype=F32) / HEAD_DIM)
    ang = pos[:, None] * inv[None, :]
    cos, sin = jnp.cos(ang), jnp.sin(ang)
    cos2 = jnp.concatenate([cos, cos, cos, cos, jnp.ones((seq, 128), F32)], axis=1)
    sin2 = jnp.concatenate([-sin, sin, -sin, sin, jnp.zeros((seq, 128), F32)], axis=1)
    return cos2, sin2


def _rope(h, cos2, sin2, plain_block, out_dtype, name, width=128):
    s, n = h.shape
    ts = _tile(s, 512, 8)

    def body(h_ref, c_ref, s_ref, o_ref):
        cos, sin = c_ref[...], s_ref[...]
        lane = lax.broadcasted_iota(jnp.int32, cos.shape, 1)
        first_half = lane % HEAD_DIM < HEAD_DIM // 2
        for w in range(width // 128):
            cols = slice(w * 128, (w + 1) * 128)
            x = h_ref[:, cols].astype(F32)
            partner = jnp.where(first_half, pltpu.roll(x, 128 - HEAD_DIM // 2, 1), pltpu.roll(x, HEAD_DIM // 2, 1))
            o_ref[:, cols] = (x * cos + partner * sin).astype(o_ref.dtype)

    tile = pl.BlockSpec((ts, width), lambda i, j: (i, j))
    table = pl.BlockSpec((ts, 128), lambda i, j: (i, plain_block(j)))
    return pl.pallas_call(
        body,
        name=name,
        grid=(s // ts, n // width),
        in_specs=[tile, table, table],
        out_specs=tile,
        out_shape=jax.ShapeDtypeStruct((s, n), out_dtype),
        compiler_params=_params(("parallel", "parallel")),
    )(h, cos2, sin2)


SB_TQ = 512
SB_BLK = 128
SB_CLOSED = 110.0


def _sb_walk(n_trips, carry, key_blocks):
    def still_open(carry):
        lowest = functools.reduce(jnp.minimum, [c[0] for c in carry])
        return jnp.min(lowest) < SB_CLOSED

    def cond(state):
        t, go, _ = state
        return jnp.logical_and(t < n_trips, go)

    def body(state):
        t, _, carry = state
        carry = key_blocks(t, carry)
        return t + 1, still_open(carry), carry

    return lax.while_loop(cond, body, (jnp.int32(0), still_open(carry), carry))[2]


def _tri2(strict):
    row = lax.broadcasted_iota(jnp.int32, (2 * SB_BLK, SB_BLK), 0) % SB_BLK
    col = lax.broadcasted_iota(jnp.int32, (2 * SB_BLK, SB_BLK), 1)
    return (row > col if strict else row >= col).astype(MXU_DTYPE)


def _cumsum_dot(x, tri2):
    hi = x.astype(MXU_DTYPE)
    lo = x - hi.astype(F32)
    return _dot(jnp.concatenate([hi, lo.astype(MXU_DTYPE)], axis=1), tri2, NN)


def _sb_logits(z, valid):
    l1p = jnp.log(1.0 + jnp.exp(-jnp.abs(z)))
    sp = jnp.maximum(z, 0.0) + l1p
    if valid is not None:
        sp = jnp.where(valid, sp, 0.0)
    return sp, jnp.minimum(z, 0.0) - l1p


def _sb_weights(ls, after, valid):
    a = jnp.exp(ls - after)
    return a if valid is None else jnp.where(valid, a, 0.0)


def _sb_setup(q_ref, k_ref, v_ref, n_sub, scale):
    qs = [q_ref[0, u * SB_BLK:(u + 1) * SB_BLK, :] * scale for u in range(n_sub)]
    row = lax.broadcasted_iota(jnp.int32, (SB_BLK, SB_BLK), 0)
    col = lax.broadcasted_iota(jnp.int32, (SB_BLK, SB_BLK), 1)

    def load(jj):
        start = pl.multiple_of(jj * SB_BLK, SB_BLK)
        return start, k_ref[0, pl.ds(start, SB_BLK), :], v_ref[0, pl.ds(start, SB_BLK), :]

    return qs, col < row, load


def _sb_fwd(q, k, v, name):
    nh, s, dh = q.shape
    tq = min(SB_TQ, s)
    n_sub = tq // SB_BLK
    scale = dh ** -0.5

    def body(q_ref, k_ref, v_ref, o_ref, ob_ref):
        base = pl.program_id(1) * n_sub
        qs, diag_valid, load = _sb_setup(q_ref, k_ref, v_ref, n_sub, scale)
        tri_after = _tri2(True)

        def key_blocks(first, carry, diagonal):
            blocks = list(range(n_sub - 1, -1, -1))
            kv = {d: load(first + d) for d in blocks}
            tiles = [(d, u) for d in blocks for u in range(d if diagonal else 0, n_sub)]
            valid = {t: diag_valid if diagonal and t[0] == t[1] else None for t in tiles}
            z = {t: _dot(qs[t[1]], kv[t[0]][1], NT) for t in tiles}
            sp_ls = {t: _sb_logits(z[t], valid[t]) for t in tiles}
            inside = {t: _cumsum_dot(sp_ls[t][0], tri_after) for t in tiles}
            carry = list(carry)
            for t in tiles:
                after_c, acc = carry[t[1]]
                sp, ls = sp_ls[t]
                a = _sb_weights(ls, after_c + inside[t], valid[t])
                carry[t[1]] = (after_c + jnp.sum(sp, axis=1, keepdims=True), acc + _dot(a, kv[t[0]][2], NN))
            return tuple(carry)

        carry = tuple((jnp.zeros((SB_BLK, 1), F32), jnp.zeros((SB_BLK, dh), F32)) for _ in range(n_sub))
        carry = key_blocks(base, carry, True)
        carry = _sb_walk(pl.program_id(1), carry, lambda t, c: key_blocks(base - n_sub * (t + 1), c, False))
        for u in range(n_sub):
            rows = slice(u * SB_BLK, (u + 1) * SB_BLK)
            o_ref[0, rows, :] = carry[u][1]
            ob_ref[0, rows, :] = carry[u][1].astype(ob_ref.dtype)

    blk = pl.BlockSpec((1, tq, dh), lambda h, i: (h, i, 0))
    full = pl.BlockSpec((1, s, dh), lambda h, i: (h, 0, 0))
    return pl.pallas_call(
        body,
        name=name,
        grid=(nh, s // tq),
        in_specs=[blk, full, full],
        out_specs=[blk, blk],
        out_shape=[jax.ShapeDtypeStruct((nh, s, dh), F32), jax.ShapeDtypeStruct((nh, s, dh), MXU_DTYPE)],
        compiler_params=_params(("parallel", "parallel")),
    )(q, k, v)


def _sb_bwd(q, k, v, o, do, name):
    nh, s, dh = q.shape
    tq = min(SB_TQ, s)
    n_sub = tq // SB_BLK
    scale = dh ** -0.5

    def body(q_ref, k_ref, v_ref, o_ref, do_ref, dq_ref, dk_ref, dv_ref):
        i = pl.program_id(1)
        base = i * n_sub

        @pl.when(i == 0)
        def _():
            dk_ref[...] = jnp.zeros_like(dk_ref)
            dv_ref[...] = jnp.zeros_like(dv_ref)

        qs, diag_valid, load = _sb_setup(q_ref, k_ref, v_ref, n_sub, scale)
        dob = [do_ref[0, u * SB_BLK:(u + 1) * SB_BLK, :] for u in range(n_sub)]
        total = [jnp.sum(dob[u].astype(F32) * o_ref[0, u * SB_BLK:(u + 1) * SB_BLK, :], axis=1, keepdims=True) for u in range(n_sub)]
        tri_after = _tri2(True)
        tri_from = _tri2(False)

        def key_blocks(first, carry, diagonal):
            blocks = list(range(n_sub - 1, -1, -1))
            kv = {d: load(first + d) for d in blocks}
            tiles = [(d, u) for d in blocks for u in range(d if diagonal else 0, n_sub)]
            valid = {t: diag_valid if diagonal and t[0] == t[1] else None for t in tiles}
            z = {t: _dot(qs[t[1]], kv[t[0]][1], NT) for t in tiles}
            da = {t: _dot(dob[t[1]], kv[t[0]][2], NT) for t in tiles}
            sp_ls = {t: _sb_logits(z[t], valid[t]) for t in tiles}
            inside = {t: _cumsum_dot(sp_ls[t][0], tri_after) for t in tiles}
            after_run = [c[0] for c in carry]
            ab, dl = {}, {}
            for t in tiles:
                sp, ls = sp_ls[t]
                ab[t] = _sb_weights(ls, after_run[t[1]] + inside[t], valid[t]).astype(MXU_DTYPE)
                dl[t] = ab[t].astype(F32) * da[t]
                after_run[t[1]] = after_run[t[1]] + jnp.sum(sp, axis=1, keepdims=True)
            from_in = {t: _cumsum_dot(dl[t], tri_from) for t in tiles}
            from_run = [c[1] for c in carry]
            dq = [c[2] for c in carry]
            for d in blocks:
                start, kb, _ = kv[d]
                dk = dv = None
                for u in range(d if diagonal else 0, n_sub):
                    t = (d, u)
                    dz = dl[t] - jnp.exp(sp_ls[t][1]) * (dl[t] + total[u] - (from_run[u] + from_in[t]))
                    if valid[t] is not None:
                        dz = jnp.where(valid[t], dz, 0.0)
                    dzb = dz.astype(MXU_DTYPE)
                    from_run[u] = from_run[u] + jnp.sum(dl[t], axis=1, keepdims=True)
                    dq[u] = dq[u] + _dot(dzb, kb, NN)
                    dk_u, dv_u = _dot(dzb, qs[u], TN), _dot(ab[t], dob[u], TN)
                    dk = dk_u if dk is None else dk + dk_u
                    dv = dv_u if dv is None else dv + dv_u
                dk_ref[0, pl.ds(start, SB_BLK), :] += dk
                dv_ref[0, pl.ds(start, SB_BLK), :] += dv
            return tuple(zip(after_run, from_run, dq))

        carry = tuple((jnp.zeros((SB_BLK, 1), F32), jnp.zeros((SB_BLK, 1), F32), jnp.zeros((SB_BLK, dh), F32)) for _ in range(n_sub))
        carry = key_blocks(base, carry, True)
        carry = _sb_walk(i, carry, lambda t, c: key_blocks(base - n_sub * (t + 1), c, False))
        for u in range(n_sub):
            dq_ref[0, u * SB_BLK:(u + 1) * SB_BLK, :] = carry[u][2] * scale

    blk = pl.BlockSpec((1, tq, dh), lambda h, i: (h, i, 0))
    full = pl.BlockSpec((1, s, dh), lambda h, i: (h, 0, 0))
    shape = jax.ShapeDtypeStruct((nh, s, dh), F32)
    return pl.pallas_call(
        body,
        name=name,
        grid=(nh, s // tq),
        in_specs=[blk, full, full, blk, blk],
        out_specs=[blk, full, full],
        out_shape=[shape, shape, shape],
        compiler_params=_params(("parallel", "arbitrary")),
    )(q, k, v, o, do)


BAND_TQ = 512


def _band_scores(q_ref, k_ref, i, sub, tq, length, max_dist, scale):
    t0 = i * tq + sub * ATT_BLK
    ks = pl.multiple_of(jnp.minimum(jnp.maximum(t0 - ATT_BLK, 0), length - 2 * ATT_BLK), ATT_BLK)
    qs = q_ref[0, sub * ATT_BLK:(sub + 1) * ATT_BLK, :] * scale
    kw = k_ref[0, pl.ds(ks, 2 * ATT_BLK), :]
    sc = _dot(qs, kw, NT)
    diff = (t0 + lax.broadcasted_iota(jnp.int32, sc.shape, 0)) - (ks + lax.broadcasted_iota(jnp.int32, sc.shape, 1))
    valid = (diff >= 0) & (diff <= max_dist)
    return ks, qs, kw, jnp.where(valid, sc, NEG_INF)


def _band_fwd(q, k, v, sinks, max_dist, name):
    bq, length, dh = q.shape
    group = bq // k.shape[0]
    tq = min(BAND_TQ, length)
    scale = dh ** -0.5
    n_sink = 0 if sinks is None else sinks.shape[0]

    def body(*refs):
        if n_sink:
            sink_ref, q_ref, k_ref, v_ref, o_ref, lse_ref = refs
            sink = sink_ref[pl.program_id(0) % n_sink]
        else:
            q_ref, k_ref, v_ref, o_ref, lse_ref = refs
        i = pl.program_id(1)
        for sub in range(tq // ATT_BLK):
            ks, _, _, sc = _band_scores(q_ref, k_ref, i, sub, tq, length, max_dist, scale)
            m = jnp.max(sc, axis=1, keepdims=True)
            if n_sink:
                m = jnp.maximum(m, sink)
            e = jnp.exp(sc - m)
            den = jnp.sum(e, axis=1, keepdims=True)
            if n_sink:
                den = den + jnp.exp(sink - m)
            rows = slice(sub * ATT_BLK, (sub + 1) * ATT_BLK)
            o_ref[0, rows, :] = _dot(e / den, v_ref[0, pl.ds(ks, 2 * ATT_BLK), :], NN)
            lse_ref[0, rows, :] = m + jnp.log(den)

    qblk = pl.BlockSpec((1, tq, dh), lambda b, i: (b, i, 0))
    kfull = pl.BlockSpec((1, length, dh), lambda b, i: (b // group, 0, 0))
    in_specs = [qblk, kfull, kfull]
    args = [q, k, v]
    if n_sink:
        in_specs = [pl.BlockSpec(memory_space=pltpu.SMEM)] + in_specs
        args = [sinks] + args
    return pl.pallas_call(
        body,
        name=name,
        grid=(bq, length // tq),
        in_specs=in_specs,
        out_specs=[qblk, pl.BlockSpec((1, tq, 1), lambda b, i: (b, i, 0))],
        out_shape=[jax.ShapeDtypeStruct((bq, length, dh), F32), jax.ShapeDtypeStruct((bq, length, 1), F32)],
        compiler_params=_params(("parallel", "parallel")),
    )(*args)


def _band_bwd(q, k, v, do, lse, delta, sinks, max_dist, name):
    bq, length, dh = q.shape
    group = bq // k.shape[0]
    tq = min(BAND_TQ, length)
    scale = dh ** -0.5
    n_sink = 0 if sinks is None else sinks.shape[0]

    def body(*refs):
        if n_sink:
            sink_ref, q_ref, k_ref, v_ref, do_ref, lse_ref, dl_ref, dq_ref, dk_ref, dv_ref, ds_ref = refs
            sink = sink_ref[pl.program_id(0) % n_sink]
        else:
            q_ref, k_ref, v_ref, do_ref, lse_ref, dl_ref, dq_ref, dk_ref, dv_ref, ds_ref = refs
        i = pl.program_id(1)

        @pl.when((i == 0) & (pl.program_id(0) % group == 0))
        def _():
            dk_ref[...] = jnp.zeros_like(dk_ref)
            dv_ref[...] = jnp.zeros_like(dv_ref)

        @pl.when(i == 0)
        def _():
            ds_ref[...] = jnp.zeros_like(ds_ref)

        for sub in range(tq // ATT_BLK):
            ks, qs, kw, sc = _band_scores(q_ref, k_ref, i, sub, tq, length, max_dist, scale)
            rows = slice(sub * ATT_BLK, (sub + 1) * ATT_BLK)
            lse = lse_ref[0, rows, :]
            delta_r = dl_ref[0, rows, :]
            dob = do_ref[0, rows, :]
            p = jnp.exp(sc - lse)
            dp = _dot(dob, v_ref[0, pl.ds(ks, 2 * ATT_BLK), :], NT)
            dsb = (p * (dp - delta_r)).astype(MXU_DTYPE)
            dq_ref[0, rows, :] = _dot(dsb, kw, NN) * scale
            dk_ref[0, pl.ds(ks, 2 * ATT_BLK), :] += _dot(dsb, qs, TN)
            dv_ref[0, pl.ds(ks, 2 * ATT_BLK), :] += _dot(p, dob, TN)
            if n_sink:
                ds_ref[...] += jnp.sum(-jnp.exp(sink - lse) * delta_r, axis=0, keepdims=True)

    qblk = pl.BlockSpec((1, tq, dh), lambda b, i: (b, i, 0))
    qcol = pl.BlockSpec((1, tq, 1), lambda b, i: (b, i, 0))
    kfull = pl.BlockSpec((1, length, dh), lambda b, i: (b // group, 0, 0))
    in_specs = [qblk, kfull, kfull, qblk, qcol, qcol]
    args = [q, k, v, do, lse, delta]
    if n_sink:
        in_specs = [pl.BlockSpec(memory_space=pltpu.SMEM)] + in_specs
        args = [sinks] + args
    kshape = jax.ShapeDtypeStruct((k.shape[0], length, dh), F32)
    return pl.pallas_call(
        body,
        name=name,
        grid=(bq, length // tq),
        in_specs=in_specs,
        out_specs=[qblk, kfull, kfull, pl.BlockSpec((1, 8, 128), lambda b, i: (b, 0, 0))],
        out_shape=[jax.ShapeDtypeStruct((bq, length, dh), F32), kshape, kshape, jax.ShapeDtypeStruct((bq, 8, 128), F32)],
        compiler_params=_params(("arbitrary", "arbitrary")),
    )(*args)


def _merge_weights(lse_refs):
    lses = [r[0] for r in lse_refs]
    m = functools.reduce(jnp.maximum, lses)
    es = [jnp.exp(l - m) for l in lses]
    den = functools.reduce(lambda a, b: a + b, es)
    return [e / den for e in es]


def _merge_fwd(outs, lses, name):
    n = len(outs)
    nh, s, dh = outs[0].shape
    ts = _tile(s, 1024, 8)

    def body(*refs):
        ws = _merge_weights(refs[n:2 * n])
        o = functools.reduce(lambda a, b: a + b, [w * r[0] for w, r in zip(ws, refs[:n])])
        refs[2 * n][0] = o
        refs[2 * n + 1][0] = o.astype(MXU_DTYPE)

    blk = pl.BlockSpec((1, ts, dh), lambda h, i: (h, i, 0))
    col = pl.BlockSpec((1, ts, 1), lambda h, i: (h, i, 0))
    return pl.pallas_call(
        body,
        name=name,
        grid=(nh, s // ts),
        in_specs=[blk] * n + [col] * n,
        out_specs=[blk, blk],
        out_shape=[jax.ShapeDtypeStruct((nh, s, dh), F32), jax.ShapeDtypeStruct((nh, s, dh), MXU_DTYPE)],
        compiler_params=_params(("parallel", "parallel")),
    )(*outs, *lses)


def _merge_bwd(do, o, lses, name):
    n = len(lses)
    nh, s, dh = o.shape
    ts = _tile(s, 1024, 8)

    def body(*refs):
        do_ref, o_ref = refs[:2]
        ws = _merge_weights(refs[2:2 + n])
        dof = do_ref[0].astype(F32)
        base = jnp.sum(dof * o_ref[0], axis=1, keepdims=True)
        for g in range(n):
            refs[2 + n + g][0] = (ws[g] * dof).astype(MXU_DTYPE)
            refs[2 + 2 * n + g][0] = ws[g] * base

    blk = pl.BlockSpec((1, ts, dh), lambda h, i: (h, i, 0))
    col = pl.BlockSpec((1, ts, 1), lambda h, i: (h, i, 0))
    res = pl.pallas_call(
        body,
        name=name,
        grid=(nh, s // ts),
        in_specs=[blk, blk] + [col] * n,
        out_specs=[blk] * n + [col] * n,
        out_shape=[jax.ShapeDtypeStruct((nh, s, dh), MXU_DTYPE)] * n + [jax.ShapeDtypeStruct((nh, s, 1), F32)] * n,
        compiler_params=_params(("parallel", "parallel")),
    )(do, o, *lses)
    return res[:n], res[n:]


def _to_heads(a, dil=1):
    s, n = a.shape
    nh = n // HEAD_DIM
    a = a.reshape(s // dil, dil, nh, HEAD_DIM).transpose(1, 2, 0, 3)
    return a.reshape(dil * nh, s // dil, HEAD_DIM)


def _from_heads(a, dil=1):
    b, ls, c = a.shape
    nh = b // dil
    return a.reshape(dil, nh, ls, c).transpose(2, 0, 1, 3).reshape(ls * dil, nh * c)


def _restride(a, dil):
    nh, s, c = a.shape
    return a.reshape(nh, s // dil, dil, c).transpose(2, 0, 1, 3).reshape(dil * nh, s // dil, c)


def _unstride(a, dil):
    b, ls, c = a.shape
    nh = b // dil
    return a.reshape(dil, nh, ls, c).transpose(1, 2, 0, 3).reshape(nh, ls * dil, c)


ANY = pl.BlockSpec(memory_space=pl.ANY)


def _position():
    x, y, c = lax.axis_index("x"), lax.axis_index("y"), lax.axis_index("c")
    return x, y, c, [(1 - x, y), (x, 1 - y), (1 - x, 1 - y)]


def _remote(src, dst, send_sems, recv_sems, k, to):
    return pltpu.make_async_remote_copy(src_ref=src, dst_ref=dst, send_sem=send_sems.at[k], recv_sem=recv_sems.at[k], device_id=to, device_id_type=MESH)


def _gather_chips(arrs, name):
    n = len(arrs)

    def body(*refs):
        f_refs, g_refs = refs[:n], refs[n:2 * n]
        send_sems, recv_sems, local_sems = refs[2 * n:]
        x, y, c, chips = _position()
        me, sibling = (x, y, c), (x, y, 1 - c)

        def half(j, ref, sel):
            rows = f_refs[j].shape[1] // 2
            return ref.at[:, pl.ds(sel * rows, rows), :]

        def slot(j, chip, sel):
            return half(j, g_refs[j].at[2 * chip[0] + chip[1]], sel)

        mine = [pltpu.make_async_copy(f_refs[j], g_refs[j].at[2 * x + y], local_sems.at[j]) for j in range(n)]
        first = [_remote(half(j, f_refs[j], c), slot(j, (x, y), c), send_sems, recv_sems, 6 * j + k, (*chip, c)) for j in range(n) for k, chip in enumerate(chips)]
        for cp in mine + first:
            cp.start()
        passed = []
        for j in range(n):
            for k, chip in enumerate(chips):
                _remote(slot(j, chip, c), slot(j, chip, c), send_sems, recv_sems, 6 * j + k, me).wait_recv()
                passed.append(_remote(slot(j, chip, c), slot(j, chip, c), send_sems, recv_sems, 6 * j + 3 + k, sibling))
                passed[-1].start()
        for j in range(n):
            for k, chip in enumerate(chips):
                _remote(slot(j, chip, 1 - c), slot(j, chip, 1 - c), send_sems, recv_sems, 6 * j + 3 + k, me).wait_recv()
        for cp in first + passed:
            cp.wait_send()
        for cp in mine:
            cp.wait()

    return pl.pallas_call(
        body,
        name=name,
        in_specs=[ANY] * n,
        out_specs=[ANY] * n,
        out_shape=[jax.ShapeDtypeStruct((N_CHIPS,) + a.shape, a.dtype) for a in arrs],
        scratch_shapes=[pltpu.SemaphoreType.DMA((6 * n,)), pltpu.SemaphoreType.DMA((6 * n,)), pltpu.SemaphoreType.DMA((n,))],
    )(*arrs)


def _swap_sibling(arrs, name):
    n = len(arrs)

    def body(*refs):
        x, y, c, _ = _position()
        send_sems, recv_sems = refs[2 * n:]
        copies = [_remote(refs[j], refs[n + j], send_sems, recv_sems, j, (x, y, 1 - c)) for j in range(n)]
        for cp in copies:
            cp.start()
        for cp in copies:
            cp.wait()

    return pl.pallas_call(
        body,
        name=name,
        in_specs=[ANY] * n,
        out_specs=[ANY] * n,
        out_shape=[jax.ShapeDtypeStruct(a.shape, a.dtype) for a in arrs],
        scratch_shapes=[pltpu.SemaphoreType.DMA((n,)), pltpu.SemaphoreType.DMA((n,))],
    )(*arrs)


def _exchange_chips(parts, name):
    n = len(parts)

    def body(*refs):
        p_refs, q_refs = refs[:n], refs[n:2 * n]
        send_sems, recv_sems, local_sems = refs[2 * n:]
        x, y, c, chips = _position()
        my_slot = 2 * x + y
        mine = [pltpu.make_async_copy(p_refs[j].at[my_slot], q_refs[j].at[my_slot], local_sems.at[j]) for j in range(n)]
        sends = [_remote(p_refs[j].at[2 * px + py], q_refs[j].at[my_slot], send_sems, recv_sems, 3 * j + k, (px, py, c)) for j in range(n) for k, (px, py) in enumerate(chips)]
        for cp in mine + sends:
            cp.start()
        for j in range(n):
            for k, (px, py) in enumerate(chips):
                landed = q_refs[j].at[2 * px + py]
                _remote(landed, landed, send_sems, recv_sems, 3 * j + k, (x, y, c)).wait_recv()
        for cp in sends:
            cp.wait_send()
        for cp in mine:
            cp.wait()

    return pl.pallas_call(
        body,
        name=name,
        in_specs=[ANY] * n,
        out_specs=[ANY] * n,
        out_shape=[jax.ShapeDtypeStruct(a.shape, a.dtype) for a in parts],
        scratch_shapes=[pltpu.SemaphoreType.DMA((3 * n,)), pltpu.SemaphoreType.DMA((3 * n,)), pltpu.SemaphoreType.DMA((n,))],
    )(*parts)


def _share_sibling(halves, name):
    n = len(halves)

    def body(*refs):
        h_refs, o_refs = refs[:n], refs[n:2 * n]
        send_sems, recv_sems, local_sems = refs[2 * n:]
        x, y, c, _ = _position()

        def rows(j, sel):
            r = h_refs[j].shape[0]
            return o_refs[j].at[pl.ds(sel * r, r), :]

        mine = [pltpu.make_async_copy(h_refs[j], rows(j, c), local_sems.at[j]) for j in range(n)]
        sends = [_remote(h_refs[j], rows(j, c), send_sems, recv_sems, j, (x, y, 1 - c)) for j in range(n)]
        for cp in mine + sends:
            cp.start()
        for j in range(n):
            _remote(rows(j, 1 - c), rows(j, 1 - c), send_sems, recv_sems, j, (x, y, c)).wait_recv()
        for cp in sends:
            cp.wait_send()
        for cp in mine:
            cp.wait()

    return pl.pallas_call(
        body,
        name=name,
        in_specs=[ANY] * n,
        out_specs=[ANY] * n,
        out_shape=[jax.ShapeDtypeStruct((2 * a.shape[0],) + a.shape[1:], a.dtype) for a in halves],
        scratch_shapes=[pltpu.SemaphoreType.DMA((n,)), pltpu.SemaphoreType.DMA((n,)), pltpu.SemaphoreType.DMA((n,))],
    )(*halves)


def _sum_rows(r, c, n_in):
    return _tile(r, max(8, (1 << 20) // (c * (n_in + 1))), 8)


def _add2(a, b, name):
    n, r, c = a.shape
    tr = _sum_rows(r, c, 2)

    def body(a_ref, b_ref, o_ref):
        o_ref[...] = a_ref[...] + b_ref[...]

    blk = pl.BlockSpec((1, tr, c), lambda s, i: (s, i, 0))
    return pl.pallas_call(
        body,
        name=name,
        grid=(n, r // tr),
        in_specs=[blk, blk],
        out_specs=blk,
        out_shape=jax.ShapeDtypeStruct(a.shape, a.dtype),
        compiler_params=_params(("parallel", "parallel")),
    )(a, b)


def _sum_slots(q, name):
    n, r, c = q.shape
    tr = _sum_rows(r, c, n)

    def body(q_ref, o_ref):
        acc = q_ref[0]
        for s in range(1, n):
            acc = acc + q_ref[s]
        o_ref[...] = acc

    return pl.pallas_call(
        body,
        name=name,
        grid=(r // tr,),
        in_specs=[pl.BlockSpec((n, tr, c), lambda i: (0, i, 0))],
        out_specs=pl.BlockSpec((tr, c), lambda i: (i, 0)),
        out_shape=jax.ShapeDtypeStruct((r, c), q.dtype),
        compiler_params=_params(("parallel",)),
    )(q)


def _adamw(w, g, m, v, name):
    r, c = w.shape
    tr = _tile(r, 256, 8)
    c1 = 1.0 - ADAM_B1 ** ADAM_STEP
    c2 = 1.0 - ADAM_B2 ** ADAM_STEP

    def body(w_ref, g_ref, m_ref, v_ref, d_ref, nm_ref, nv_ref):
        g = g_ref[...]
        nm = ADAM_B1 * m_ref[...] + (1.0 - ADAM_B1) * g
        nv = ADAM_B2 * v_ref[...] + (1.0 - ADAM_B2) * (g * g)
        d_ref[...] = -ADAM_LR * ((nm / c1) / (jnp.sqrt(nv / c2) + ADAM_EPS) + ADAM_WD * w_ref[...])
        nm_ref[...] = nm
        nv_ref[...] = nv

    blk = pl.BlockSpec((tr, c), lambda i: (i, 0))
    shape = jax.ShapeDtypeStruct((r, c), F32)
    return pl.pallas_call(
        body,
        name=name,
        grid=(r // tr,),
        in_specs=[blk] * 4,
        out_specs=[blk] * 3,
        out_shape=[shape] * 3,
        compiler_params=_params(("parallel",)),
    )(w, g, m, v)


WEIGHTS = ("ffn1_w_in", "ffn1_w_out", "ffn2_w_in", "ffn2_w_out", "ln_g", "ln_b", "sb_w_in", "sb_w_out", "swa_w_in", "swa_sinks", "swa_w_out", "dil_w_in", "dil_w_out", "ple_w_proj", "ple_w_gate")
SHARD_AXIS = {"ffn1_w_in": 2, "ffn1_w_out": 1, "ffn2_w_in": 2, "ffn2_w_out": 1, "ln_g": 2, "ln_b": 2, "sb_w_in": 2, "sb_w_out": 1, "swa_w_in": 2, "swa_sinks": None, "swa_w_out": 1, "dil_w_in": 2, "dil_w_out": 1, "ple_w_proj": 2, "ple_w_gate": 1}
SMALL = ("ln_g", "ln_b", "swa_sinks")
SMALL_COLS = 128
SMALL_UNIT = 16 * SMALL_COLS


def _pack_small(pieces, lead):
    flat = jnp.concatenate([a.reshape(lead + (-1,)) for a in pieces], axis=-1)
    n = flat.shape[-1]
    flat = jnp.pad(flat, [(0, 0)] * len(lead) + [(0, -n % SMALL_UNIT)])
    return flat.reshape(lead + (-1, SMALL_COLS))


def _unpack_small(buf, shapes, lead):
    flat = buf.reshape(lead + (-1,))
    out, off = [], 0
    for shp in shapes:
        n = math.prod(shp)
        out.append(flat[..., off:off + n].reshape(lead + tuple(shp)))
        off += n
    return out


def _reduce_group(grads, c, tag):
    keep = [lax.dynamic_slice_in_dim(g, c * (g.shape[1] // 2), g.shape[1] // 2, axis=1) for g in grads]
    give = [lax.dynamic_slice_in_dim(g, (1 - c) * (g.shape[1] // 2), g.shape[1] // 2, axis=1) for g in grads]
    got = _swap_sibling(give, f"{tag}_pair")
    part = [_add2(k, g, f"{tag}_pair_sum{j}") for j, (k, g) in enumerate(zip(keep, got))]
    landed = _exchange_chips(part, f"{tag}_chips")
    halves = [_sum_slots(q, f"{tag}_chips_sum{j}") for j, q in enumerate(landed)]
    return _share_sibling(halves, f"{tag}_share")


def _full_from_shards(g, axis):
    g = jnp.moveaxis(g, 0, axis)
    return g.reshape(g.shape[:axis] + (g.shape[axis] * g.shape[axis + 1],) + g.shape[axis + 2:])


def _shards_from_full(a, axis):
    a = a.reshape(a.shape[:axis] + (N_CHIPS, a.shape[axis] // N_CHIPS) + a.shape[axis + 1:])
    return jnp.moveaxis(a, axis, 0)


def _ffn_forward(x, xb, w_in, w_out, g, b, alpha, tag):
    gate, up, act = _ffn_in(xb, w_in, f"{tag}_in")
    y = _mm(act, w_out, "nn", F32, f"{tag}_out")
    out, outb, xhat, rstd = _ln_fwd(x, y, g, b, alpha, 0.5, f"{tag}_ln")
    return (out, outb), dict(xb=xb, gate=gate, up=up, act=act, xhat=xhat, rstd=rstd)


def _ffn_backward(dout, saved, w_in, w_out, g, tag):
    dr, dyb, dg, db = _ln_bwd(*dout, saved["xhat"], saved["rstd"], g, 0.5, f"{tag}_ln_bwd")
    dh = _ffn_dact(dyb, w_out, saved["gate"], saved["up"], f"{tag}_dact")
    dw_out = _mm(saved["act"], dyb, "tn", F32, f"{tag}_dw_out")
    dw_in = _mm(saved["xb"], dh, "tn", F32, f"{tag}_dw_in", split_b=True)
    dxb = _mm(dh, w_in, "nt", F32, f"{tag}_dx", split_a=True)
    return dr, dxb, dw_in, dw_out, dg, db


def _sb_forward(xb, w_in, w_out, tag):
    nw = w_out.shape[0]
    h = _mm(xb, w_in, "nn", MXU_DTYPE, f"{tag}_qkv")
    q, k, v = (_to_heads(h[:, j * nw:(j + 1) * nw]) for j in range(3))
    o, ob = _sb_fwd(q, k, v, f"{tag}_att")
    ob = _from_heads(ob)
    return _mm(ob, w_out, "nn", F32, f"{tag}_proj"), dict(q=q, k=k, v=v, o=o, ob=ob)


def _sb_backward(dmix, saved, xb, w_in, w_out, tag):
    dw_out = _mm(saved["ob"], dmix, "tn", F32, f"{tag}_dw_out")
    do = _to_heads(_mm(dmix, w_out, "nt", MXU_DTYPE, f"{tag}_do"))
    dq, dk, dv = _sb_bwd(saved["q"], saved["k"], saved["v"], saved["o"], do, f"{tag}_att_bwd")
    dh = jnp.concatenate([_from_heads(t) for t in (dq, dk, dv)], axis=1).astype(MXU_DTYPE)
    dw_in = _mm(xb, dh, "tn", F32, f"{tag}_dw_in")
    return _mm(dh, w_in, "nt", F32, f"{tag}_dx"), dw_in, dw_out


def _swa_forward(xb, w_in, sinks, w_out, tables, tag):
    nq = w_out.shape[0]
    nkv = (w_in.shape[1] - nq) // 2
    h = _mm(xb, w_in, "nn", F32, f"{tag}_qkv")
    n_rot = (nq + nkv) // 128
    hb = _rope(h, *tables, lambda j: j // n_rot, MXU_DTYPE, f"{tag}_rope")
    q, k, v = _to_heads(hb[:, :nq]), _to_heads(hb[:, nq:nq + nkv]), _to_heads(hb[:, nq + nkv:])
    o, lse = _band_fwd(q, k, v, sinks, SWA_WINDOW - 1, f"{tag}_att")
    o, ob = _merge_fwd([o], [lse], f"{tag}_cast")
    ob = _from_heads(ob)
    return _mm(ob, w_out, "nn", F32, f"{tag}_proj"), dict(q=q, k=k, v=v, o=o, lse=lse, ob=ob, n_rot=n_rot)


def _swa_backward(dmix, saved, xb, w_in, sinks, w_out, tables, tag):
    dw_out = _mm(saved["ob"], dmix, "tn", F32, f"{tag}_dw_out")
    do = _to_heads(_mm(dmix, w_out, "nt", MXU_DTYPE, f"{tag}_do"))
    (dog,), (delta,) = _merge_bwd(do, saved["o"], [saved["lse"]], f"{tag}_delta")
    dq, dk, dv, dsink = _band_bwd(saved["q"], saved["k"], saved["v"], dog, saved["lse"], delta, sinks, SWA_WINDOW - 1, f"{tag}_att_bwd")
    dh = jnp.concatenate([_from_heads(t) for t in (dq, dk, dv)], axis=1)
    n_rot = saved["n_rot"]
    dhb = _rope(dh, tables[0], -tables[1], lambda j: j // n_rot, MXU_DTYPE, f"{tag}_rope_bwd")
    dw_in = _mm(xb, dhb, "tn", F32, f"{tag}_dw_in")
    return _mm(dhb, w_in, "nt", F32, f"{tag}_dx"), dw_in, dw_out, dsink[:, 0, 0]


def _dil_forward(xb, w_in, w_out, tables, tag):
    nw = w_out.shape[0]
    h = _mm(xb, w_in, "nn", F32, f"{tag}_qkv")
    hb = _rope(h, *tables, lambda j: (j % 3) // 2, MXU_DTYPE, f"{tag}_rope", width=nw)
    qkv, outs, lses = [], [], []
    for gi, (win, dil) in enumerate(DIL_GROUPS):
        base = gi * 3 * nw
        q, k, v = (_to_heads(hb[:, base + j * nw:base + (j + 1) * nw], dil) for j in range(3))
        o, lse = _band_fwd(q, k, v, None, win // dil, f"{tag}_att{gi}")
        qkv.append((q, k, v))
        outs.append(_unstride(o, dil))
        lses.append(_unstride(lse, dil))
    o, ob = _merge_fwd(outs, lses, f"{tag}_merge")
    ob = _from_heads(ob)
    return _mm(ob, w_out, "nn", F32, f"{tag}_proj"), dict(qkv=qkv, o=o, lses=lses, ob=ob)


def _dil_backward(dmix, saved, xb, w_in, w_out, tables, tag):
    dw_out = _mm(saved["ob"], dmix, "tn", F32, f"{tag}_dw_out")
    do = _to_heads(_mm(dmix, w_out, "nt", MXU_DTYPE, f"{tag}_do"))
    dogs, deltas = _merge_bwd(do, saved["o"], saved["lses"], f"{tag}_merge_bwd")
    parts = []
    for gi, (win, dil) in enumerate(DIL_GROUPS):
        q, k, v = saved["qkv"][gi]
        dq, dk, dv, _ = _band_bwd(q, k, v, _restride(dogs[gi], dil), _restride(saved["lses"][gi], dil), _restride(deltas[gi], dil), None, win // dil, f"{tag}_att{gi}_bwd")
        parts += [_from_heads(t, dil) for t in (dq, dk, dv)]
    dh = jnp.concatenate(parts, axis=1)
    dhb = _rope(dh, tables[0], -tables[1], lambda j: (j % 3) // 2, MXU_DTYPE, f"{tag}_rope_bwd", width=w_out.shape[0])
    dw_in = _mm(xb, dhb, "tn", F32, f"{tag}_dw_in")
    return _mm(dhb, w_in, "nt", F32, f"{tag}_dx"), dw_in, dw_out


def kernel(x, p, ffn1_w_in, ffn1_w_out, ffn2_w_in, ffn2_w_out, ln_g, ln_b, sb_w_in, sb_w_out, swa_w_in, swa_sinks, swa_w_out, dil_w_in, dil_w_out, ple_w_proj, ple_w_gate, loss_target, m_ffn1_w_in, m_ffn1_w_out, m_ffn2_w_in, m_ffn2_w_out, m_ln_g, m_ln_b, m_sb_w_in, m_sb_w_out, m_swa_w_in, m_swa_sinks, m_swa_w_out, m_dil_w_in, m_dil_w_out, m_ple_w_proj, m_ple_w_gate, v_ffn1_w_in, v_ffn1_w_out, v_ffn2_w_in, v_ffn2_w_out, v_ln_g, v_ln_b, v_sb_w_in, v_sb_w_out, v_swa_w_in, v_swa_sinks, v_swa_w_out, v_dil_w_in, v_dil_w_out, v_ple_w_proj, v_ple_w_gate):
    shard = dict(ffn1_w_in=ffn1_w_in, ffn1_w_out=ffn1_w_out, ffn2_w_in=ffn2_w_in, ffn2_w_out=ffn2_w_out, ln_g=ln_g, ln_b=ln_b, sb_w_in=sb_w_in, sb_w_out=sb_w_out, swa_w_in=swa_w_in, swa_sinks=swa_sinks, swa_w_out=swa_w_out, dil_w_in=dil_w_in, dil_w_out=dil_w_out, ple_w_proj=ple_w_proj, ple_w_gate=ple_w_gate)
    mom_m = dict(ffn1_w_in=m_ffn1_w_in, ffn1_w_out=m_ffn1_w_out, ffn2_w_in=m_ffn2_w_in, ffn2_w_out=m_ffn2_w_out, ln_g=m_ln_g, ln_b=m_ln_b, sb_w_in=m_sb_w_in, sb_w_out=m_sb_w_out, swa_w_in=m_swa_w_in, swa_sinks=m_swa_sinks, swa_w_out=m_swa_w_out, dil_w_in=m_dil_w_in, dil_w_out=m_dil_w_out, ple_w_proj=m_ple_w_proj, ple_w_gate=m_ple_w_gate)
    mom_v = dict(ffn1_w_in=v_ffn1_w_in, ffn1_w_out=v_ffn1_w_out, ffn2_w_in=v_ffn2_w_in, ffn2_w_out=v_ffn2_w_out, ln_g=v_ln_g, ln_b=v_ln_b, sb_w_in=v_sb_w_in, sb_w_out=v_sb_w_out, swa_w_in=v_swa_w_in, swa_sinks=v_swa_sinks, swa_w_out=v_swa_w_out, dil_w_in=v_dil_w_in, dil_w_out=v_dil_w_out, ple_w_proj=v_ple_w_proj, ple_w_gate=v_ple_w_gate)
    depth = ffn1_w_in.shape[0]
    alpha = (2 * depth) ** 0.25
    c = lax.axis_index("c")

    mats = [n for n in WEIGHTS if n not in SMALL]
    got = _gather_chips([shard[n].astype(MXU_DTYPE) for n in mats] + [_pack_small([ln_g, ln_b], ())[None]], "gather_weights")
    full = {"swa_sinks": swa_sinks}
    for n, g in zip(mats, got):
        full[n] = _full_from_shards(g, SHARD_AXIS[n])
    for n, g in zip(("ln_g", "ln_b"), _unpack_small(got[-1][:, 0], [ln_g.shape, ln_b.shape], (N_CHIPS,))):
        full[n] = _full_from_shards(g, SHARD_AXIS[n])

    seq = x.shape[1]
    tables = _rope_tables(seq)
    xf = x[0]
    xb = xf.astype(MXU_DTYPE)
    saved = []
    for i in range(depth):
        kind, j = i % 3, i // 3
        sv = {}
        (x1, x1b), sv["ffn1"] = _ffn_forward(xf, xb, full["ffn1_w_in"][i], full["ffn1_w_out"][i], full["ln_g"][i, 0], full["ln_b"][i, 0], alpha, f"l{i}_ffn1")
        if kind == 0:
            mix, sv["mix"] = _sb_forward(x1b, full["sb_w_in"][j], full["sb_w_out"][j], f"l{i}_sb")
        elif kind == 1:
            mix, sv["mix"] = _swa_forward(x1b, full["swa_w_in"][j], swa_sinks[j], full["swa_w_out"][j], tables, f"l{i}_swa")
        else:
            mix, sv["mix"] = _dil_forward(x1b, full["dil_w_in"][j], full["dil_w_out"][j], tables, f"l{i}_dil")
        x2, x2b, sv["xhat2"], sv["rstd2"] = _ln_fwd(x1, mix, full["ln_g"][i, 1], full["ln_b"][i, 1], alpha, 1.0, f"l{i}_mix_ln")
        sv["x1b"] = x1b
        (x3, x3b), sv["ffn2"] = _ffn_forward(x2, x2b, full["ffn2_w_in"][i], full["ffn2_w_out"][i], full["ln_g"][i, 2], full["ln_b"][i, 2], alpha, f"l{i}_ffn2")
        xf, xb, sv["u"], sv["e"] = _ple_fwd(x3, x3b, p[i, 0], full["ple_w_gate"][i], full["ple_w_proj"][i], f"l{i}_ple")
        sv["x3b"] = x3b
        saved.append(sv)

    loss_part, dy = _loss(xf, loss_target[0], "loss")
    loss = lax.psum(loss_part[0, 0], ("x", "y", "c"))

    grads = {n: [None] * full[n].shape[0] for n in WEIGHTS if n not in ("ln_g", "ln_b")}
    gsum = {n: [None] * full[n].shape[0] for n in mats}
    dln_g = [[None] * 3 for _ in range(depth)]
    dln_b = [[None] * 3 for _ in range(depth)]
    dout = (dy, None, 1.0)
    for i in reversed(range(depth)):
        kind, j = i % 3, i // 3
        mixer = ("sb", "swa", "dil")[kind]
        sv = saved[i]
        dx4, dub, deb = _ple_bwd(*dout, sv["u"], sv["e"], f"l{i}_ple_bwd")
        grads["ple_w_gate"][i] = _mm(sv["x3b"], dub, "tn", F32, f"l{i}_ple_dw_gate")
        grads["ple_w_proj"][i] = _mm(p[i, 0], deb, "tn", F32, f"l{i}_ple_dw_proj")
        dxb = _mm(dub, full["ple_w_gate"][i], "nt", F32, f"l{i}_ple_dx")
        dr, dxb, grads["ffn2_w_in"][i], grads["ffn2_w_out"][i], dln_g[i][2], dln_b[i][2] = _ffn_backward((dx4, dxb, 1.0), sv["ffn2"], full["ffn2_w_in"][i], full["ffn2_w_out"][i], full["ln_g"][i, 2], f"l{i}_ffn2")
        dr, dmix, dln_g[i][1], dln_b[i][1] = _ln_bwd(dr, dxb, alpha, sv["xhat2"], sv["rstd2"], full["ln_g"][i, 1], 1.0, f"l{i}_mix_ln_bwd")
        if kind == 0:
            dxb, grads["sb_w_in"][j], grads["sb_w_out"][j] = _sb_backward(dmix, sv["mix"], sv["x1b"], full["sb_w_in"][j], full["sb_w_out"][j], f"l{i}_sb")
        elif kind == 1:
            dxb, grads["swa_w_in"][j], grads["swa_w_out"][j], grads["swa_sinks"][j] = _swa_backward(dmix, sv["mix"], sv["x1b"], full["swa_w_in"][j], swa_sinks[j], full["swa_w_out"][j], tables, f"l{i}_swa")
        else:
            dxb, grads["dil_w_in"][j], grads["dil_w_out"][j] = _dil_backward(dmix, sv["mix"], sv["x1b"], full["dil_w_in"][j], full["dil_w_out"][j], tables, f"l{i}_dil")
        dr, dxb, grads["ffn1_w_in"][i], grads["ffn1_w_out"][i], dln_g[i][0], dln_b[i][0] = _ffn_backward((dr, dxb, alpha), sv["ffn1"], full["ffn1_w_in"][i], full["ffn1_w_out"][i], full["ln_g"][i, 0], f"l{i}_ffn1")
        dout = (dr, dxb, alpha)
        layer = [("ffn1_w_in", i), ("ffn1_w_out", i), (f"{mixer}_w_in", j), (f"{mixer}_w_out", j), ("ffn2_w_in", i), ("ffn2_w_out", i), ("ple_w_proj", i), ("ple_w_gate", i)]
        summed = _reduce_group([_shards_from_full(grads[n][k], SHARD_AXIS[n] - 1) for n, k in layer], c, f"l{i}_reduce")
        for (n, k), g in zip(layer, summed):
            gsum[n][k] = g
    grad_x = _axpy(*dout, "grad_x")[None]

    gshard = {n: jnp.stack(g) for n, g in gsum.items()}
    small = [
        _shards_from_full(jnp.stack([jnp.concatenate(r, axis=0) for r in dln_g]), SHARD_AXIS["ln_g"]),
        _shards_from_full(jnp.stack([jnp.concatenate(r, axis=0) for r in dln_b]), SHARD_AXIS["ln_b"]),
        jnp.broadcast_to(jnp.stack(grads["swa_sinks"])[None], (N_CHIPS,) + swa_sinks.shape),
    ]
    (small_sum,) = _reduce_group([_pack_small(small, (N_CHIPS,))], c, "small_reduce")
    for n, g in zip(SMALL, _unpack_small(small_sum, [shard[n].shape for n in SMALL], ())):
        gshard[n] = g

    delta, new_m, new_v = {}, {}, {}
    for n in WEIGHTS:
        shp = shard[n].shape
        two_d = (-1, shp[-1])
        d, nm, nv = _adamw(shard[n].reshape(two_d), gshard[n].reshape(two_d), mom_m[n].reshape(two_d), mom_v[n].reshape(two_d), f"adamw_{n}")
        delta[n], new_m[n], new_v[n] = d.reshape(shp), nm.reshape(shp), nv.reshape(shp)

    return (loss, grad_x, *[gshard[n] for n in WEIGHTS], *[delta[n] for n in WEIGHTS], *[new_m[n] for n in WEIGHTS], *[new_v[n] for n in WEIGHTS])
```

```python
import functools
import math

import jax
import jax.numpy as jnp
from jax import lax
from jax.experimental import pallas as pl
from jax.experimental.pallas import tpu as pltpu

F32 = jnp.float32
MXU_DTYPE = jnp.bfloat16
MESH = pl.DeviceIdType.MESH

HEAD_DIM = 64
ATT_BLK = 128
SWA_WINDOW = 128
DIL_GROUPS = ((128, 1), (512, 4), (2048, 16))
LN_EPS = 1e-5
ROPE_THETA = 10000.0
NEG_INF = -1e30
ADAM_LR, ADAM_B1, ADAM_B2, ADAM_EPS, ADAM_WD, ADAM_STEP = 0.001, 0.9, 0.999, 1e-08, 0.01, 10

VMEM_LIMIT_BYTES = 56 * 1024 * 1024
N_CHIPS = 4


def _params(sem=None):
    return pltpu.CompilerParams(dimension_semantics=sem, vmem_limit_bytes=VMEM_LIMIT_BYTES)


def _tile(n, target, unit):
    t = (min(target, n) // unit) * unit
    while t >= unit:
        if n % t == 0:
            return t
        t -= unit
    return n


def _dot(a, b, dims):
    return lax.dot_general(a.astype(MXU_DTYPE), b.astype(MXU_DTYPE), (dims, ((), ())), preferred_element_type=F32)


NN = ((1,), (0,))
NT = ((1,), (1,))
TN = ((0,), (0,))


def _mm(a, b, mode, out_dtype, name, split_a=False, split_b=False, tm=1024, tn=1408, tk=1408):
    dims = {"nn": NN, "nt": NT, "tn": TN}[mode]
    if split_a:
        m, k = a.shape[1], 2 * a.shape[2]
    elif mode == "tn":
        k, m = a.shape
    else:
        m, k = a.shape
    if split_b:
        n = 2 * b.shape[2]
    elif mode == "nt":
        n = b.shape[0]
    else:
        n = b.shape[1]
    tm = _tile(m, tm, 128)
    tn = _tile(n // 2 if split_b else n, tn, 128)
    tk = _tile(k // 2 if split_a else k, tk, 128)
    nk = k // tk
    nk_half = nk // 2
    nn_half = (n // tn) // 2

    if split_a:
        a_spec = pl.BlockSpec((None, tm, tk), lambda i, j, kk: (kk // nk_half, i, kk % nk_half))
    elif mode == "tn":
        a_spec = pl.BlockSpec((tk, tm), lambda i, j, kk: (kk, i))
    else:
        a_spec = pl.BlockSpec((tm, tk), lambda i, j, kk: (i, kk))
    if split_b:
        b_spec = pl.BlockSpec((None, tk, tn), lambda i, j, kk: (j // nn_half, kk, j % nn_half))
    elif mode == "nt":
        b_spec = pl.BlockSpec((tn, tk), lambda i, j, kk: (j, kk))
    else:
        b_spec = pl.BlockSpec((tk, tn), lambda i, j, kk: (kk, j))

    def body(a_ref, b_ref, o_ref, acc_ref):
        kk = pl.program_id(2)

        @pl.when(kk == 0)
        def _():
            acc_ref[...] = jnp.zeros_like(acc_ref)

        acc_ref[...] += _dot(a_ref[...], b_ref[...], dims)

        @pl.when(kk == nk - 1)
        def _():
            o_ref[...] = acc_ref[...].astype(o_ref.dtype)

    return pl.pallas_call(
        body,
        name=name,
        grid=(m // tm, n // tn, nk),
        in_specs=[a_spec, b_spec],
        out_specs=pl.BlockSpec((tm, tn), lambda i, j, kk: (i, j)),
        out_shape=jax.ShapeDtypeStruct((m, n), out_dtype),
        scratch_shapes=[pltpu.VMEM((tm, tn), F32)],
        compiler_params=_params(("parallel", "parallel", "arbitrary")),
    )(a, b)


def _sigmoid(x):
    return 1.0 / (1.0 + jnp.exp(-x))


def _ffn_in(xb, w_in, name):
    s, d = xb.shape
    f = w_in.shape[1] // 2
    tm = _tile(s, 512, 128)
    tn = _tile(f, 1408, 128)
    nj = f // tn

    def body(x_ref, wg_ref, wu_ref, g_ref, u_ref, a_ref):
        x = x_ref[...]
        g = _dot(x, wg_ref[...], NN)
        u = _dot(x, wu_ref[...], NN)
        g_ref[...] = g.astype(g_ref.dtype)
        u_ref[...] = u.astype(u_ref.dtype)
        a_ref[...] = (g * _sigmoid(g) * u).astype(a_ref.dtype)

    out = pl.BlockSpec((tm, tn), lambda i, j: (i, j))
    return pl.pallas_call(
        body,
        name=name,
        grid=(s // tm, nj),
        in_specs=[
            pl.BlockSpec((tm, d), lambda i, j: (i, 0)),
            pl.BlockSpec((d, tn), lambda i, j: (0, j)),
            pl.BlockSpec((d, tn), lambda i, j: (0, j + nj)),
        ],
        out_specs=[out, out, out],
        out_shape=[
            jax.ShapeDtypeStruct((s, f), MXU_DTYPE),
            jax.ShapeDtypeStruct((s, f), MXU_DTYPE),
            jax.ShapeDtypeStruct((s, f), MXU_DTYPE),
        ],
        compiler_params=_params(("parallel", "parallel")),
    )(xb, w_in, w_in)


def _ffn_dact(dyb, w_out, gate, up, name):
    s, d = dyb.shape
    f = w_out.shape[0]
    tm = _tile(s, 512, 128)
    tn = _tile(f, 1408, 128)

    def body(dy_ref, w_ref, g_ref, u_ref, o_ref):
        dact = _dot(dy_ref[...], w_ref[...], NT)
        g = g_ref[...].astype(F32)
        sig = _sigmoid(g)
        o_ref[0] = (dact * u_ref[...].astype(F32) * (sig * (1.0 + g * (1.0 - sig)))).astype(o_ref.dtype)
        o_ref[1] = (dact * (g * sig)).astype(o_ref.dtype)

    tile = pl.BlockSpec((tm, tn), lambda i, j: (i, j))
    return pl.pallas_call(
        body,
        name=name,
        grid=(s // tm, f // tn),
        in_specs=[
            pl.BlockSpec((tm, d), lambda i, j: (i, 0)),
            pl.BlockSpec((tn, d), lambda i, j: (j, 0)),
            tile,
            tile,
        ],
        out_specs=pl.BlockSpec((2, tm, tn), lambda i, j: (0, i, j)),
        out_shape=jax.ShapeDtypeStruct((2, s, f), MXU_DTYPE),
        compiler_params=_params(("parallel", "parallel")),
    )(dyb, w_out, gate, up)


def _ple_fwd(x, xb, p, w_gate, w_proj, name):
    s, d = x.shape
    pd = p.shape[1]
    tm = _tile(s, 1024, 128)
    tn = _tile(d, 512, 128)

    def body(x_ref, xb_ref, p_ref, wg_ref, wp_ref, o_ref, ob_ref, u_ref, e_ref):
        u = _dot(xb_ref[...], wg_ref[...], NN)
        e = _dot(p_ref[...], wp_ref[...], NN)
        out = x_ref[...] + _sigmoid(u) * e
        o_ref[...] = out
        ob_ref[...] = out.astype(ob_ref.dtype)
        u_ref[...] = u
        e_ref[...] = e

    tile = pl.BlockSpec((tm, tn), lambda i, j: (i, j))
    return pl.pallas_call(
        body,
        name=name,
        grid=(s // tm, d // tn),
        in_specs=[
            tile,
            pl.BlockSpec((tm, d), lambda i, j: (i, 0)),
            pl.BlockSpec((tm, pd), lambda i, j: (i, 0)),
            pl.BlockSpec((d, tn), lambda i, j: (0, j)),
            pl.BlockSpec((pd, tn), lambda i, j: (0, j)),
        ],
        out_specs=[tile, tile, tile, tile],
        out_shape=[
            jax.ShapeDtypeStruct((s, d), F32),
            jax.ShapeDtypeStruct((s, d), MXU_DTYPE),
            jax.ShapeDtypeStruct((s, d), F32),
            jax.ShapeDtypeStruct((s, d), F32),
        ],
        compiler_params=_params(("parallel", "parallel")),
    )(x, xb, p, w_gate, w_proj)


def _rows_spec(ts, d):
    return pl.BlockSpec((ts, d), lambda i: (i, 0))


def _ln_fwd(x, y, g, b, alpha, beta, name):
    s, d = x.shape
    ts = _tile(s, 512, 8)

    def body(x_ref, y_ref, g_ref, b_ref, o_ref, ob_ref, xh_ref, rs_ref):
        r = alpha * x_ref[...] + beta * y_ref[...]
        mu = jnp.mean(r, axis=1, keepdims=True)
        cen = r - mu
        var = jnp.mean(cen * cen, axis=1, keepdims=True)
        rstd = lax.rsqrt(var + LN_EPS)
        xhat = cen * rstd
        out = xhat * g_ref[...] + b_ref[...]
        o_ref[...] = out
        ob_ref[...] = out.astype(ob_ref.dtype)
        xh_ref[...] = xhat
        rs_ref[...] = rstd

    vec = pl.BlockSpec((1, d), lambda i: (0, 0))
    return pl.pallas_call(
        body,
        name=name,
        grid=(s // ts,),
        in_specs=[_rows_spec(ts, d), _rows_spec(ts, d), vec, vec],
        out_specs=[_rows_spec(ts, d), _rows_spec(ts, d), _rows_spec(ts, d), _rows_spec(ts, 1)],
        out_shape=[
            jax.ShapeDtypeStruct((s, d), F32),
            jax.ShapeDtypeStruct((s, d), MXU_DTYPE),
            jax.ShapeDtypeStruct((s, d), F32),
            jax.ShapeDtypeStruct((s, 1), F32),
        ],
        compiler_params=_params(("parallel",)),
    )(x, y, g.reshape(1, d), b.reshape(1, d))


def _ln_bwd(ga, gb, ca, xhat, rstd, g, beta, name):
    s, d = xhat.shape
    ts = _tile(s, 512, 8)

    def body(ga_ref, gb_ref, xh_ref, rs_ref, g_ref, dr_ref, dyb_ref, dg_ref, db_ref):
        @pl.when(pl.program_id(0) == 0)
        def _():
            dg_ref[...] = jnp.zeros_like(dg_ref)
            db_ref[...] = jnp.zeros_like(db_ref)

        dout = ca * ga_ref[...] + gb_ref[...]
        xhat = xh_ref[...]
        dg_ref[...] += jnp.sum(dout * xhat, axis=0, keepdims=True)
        db_ref[...] += jnp.sum(dout, axis=0, keepdims=True)
        dxh = dout * g_ref[...]
        m1 = jnp.mean(dxh, axis=1, keepdims=True)
        m2 = jnp.mean(dxh * xhat, axis=1, keepdims=True)
        dr = rs_ref[...] * (dxh - m1 - xhat * m2)
        dr_ref[...] = dr
        dyb_ref[...] = (beta * dr).astype(dyb_ref.dtype)

    vec = pl.BlockSpec((1, d), lambda i: (0, 0))
    return pl.pallas_call(
        body,
        name=name,
        grid=(s // ts,),
        in_specs=[_rows_spec(ts, d), _rows_spec(ts, d), _rows_spec(ts, d), _rows_spec(ts, 1), vec],
        out_specs=[_rows_spec(ts, d), _rows_spec(ts, d), vec, vec],
        out_shape=[
            jax.ShapeDtypeStruct((s, d), F32),
            jax.ShapeDtypeStruct((s, d), MXU_DTYPE),
            jax.ShapeDtypeStruct((1, d), F32),
            jax.ShapeDtypeStruct((1, d), F32),
        ],
        compiler_params=_params(("arbitrary",)),
    )(ga, gb, xhat, rstd, g.reshape(1, d))


def _ple_bwd(ga, gb, ca, u, e, name):
    s, d = u.shape
    ts = _tile(s, 512, 8)
    grads = [ga] if gb is None else [ga, gb]

    def body(*refs):
        u_ref, e_ref, dx_ref, du_ref, de_ref = refs[len(grads):]
        dx = ca * refs[0][...]
        if gb is not None:
            dx = dx + refs[1][...]
        sig = _sigmoid(u_ref[...])
        dx_ref[...] = dx
        du_ref[...] = (dx * e_ref[...] * sig * (1.0 - sig)).astype(du_ref.dtype)
        de_ref[...] = (dx * sig).astype(de_ref.dtype)

    return pl.pallas_call(
        body,
        name=name,
        grid=(s // ts,),
        in_specs=[_rows_spec(ts, d)] * (len(grads) + 2),
        out_specs=[_rows_spec(ts, d)] * 3,
        out_shape=[
            jax.ShapeDtypeStruct((s, d), F32),
            jax.ShapeDtypeStruct((s, d), MXU_DTYPE),
            jax.ShapeDtypeStruct((s, d), MXU_DTYPE),
        ],
        compiler_params=_params(("parallel",)),
    )(*grads, u, e)


def _axpy(ga, gb, ca, name):
    s, d = ga.shape
    ts = _tile(s, 512, 8)

    def body(ga_ref, gb_ref, o_ref):
        o_ref[...] = ca * ga_ref[...] + gb_ref[...]

    return pl.pallas_call(
        body,
        name=name,
        grid=(s // ts,),
        in_specs=[_rows_spec(ts, d)] * 2,
        out_specs=_rows_spec(ts, d),
        out_shape=jax.ShapeDtypeStruct((s, d), F32),
        compiler_params=_params(("parallel",)),
    )(ga, gb)


def _loss(y, target, name):
    s, d = y.shape
    ts = _tile(s, 512, 8)

    def body(y_ref, t_ref, l_ref, dy_ref):
        @pl.when(pl.program_id(0) == 0)
        def _():
            l_ref[...] = jnp.zeros_like(l_ref)

        err = y_ref[...] - t_ref[...]
        l_ref[...] += (0.5 / d) * jnp.sum(jnp.sum(err * err, axis=1, keepdims=True), axis=0, keepdims=True)
        dy_ref[...] = err * (1.0 / d)

    return pl.pallas_call(
        body,
        name=name,
        grid=(s // ts,),
        in_specs=[_rows_spec(ts, d)] * 2,
        out_specs=[pl.BlockSpec((1, 1), lambda i: (0, 0)), _rows_spec(ts, d)],
        out_shape=[jax.ShapeDtypeStruct((1, 1), F32), jax.ShapeDtypeStruct((s, d), F32)],
        compiler_params=_params(("arbitrary",)),
    )(y, target)


def _rope_tables(seq):
    pos = jnp.arange(seq, dtype=F32)
    inv = ROPE_THETA ** (-jnp.arange(0, HEAD_DIM, 2, dtype=F32) / HEAD_DIM)
    ang = pos[:, None] * inv[None, :]
    cos, sin = jnp.cos(ang), jnp.sin(ang)
    cos2 = jnp.concatenate([cos, cos, cos, cos, jnp.ones((seq, 128), F32)], axis=1)
    sin2 = jnp.concatenate([-sin, sin, -sin, sin, jnp.zeros((seq, 128), F32)], axis=1)
    return cos2, sin2


def _rope(h, cos2, sin2, plain_block, out_dtype, name, width=128):
    s, n = h.shape
    ts = _tile(s, 512, 8)

    def body(h_ref, c_ref, s_ref, o_ref):
        cos, sin = c_ref[...], s_ref[...]
        lane = lax.broadcasted_iota(jnp.int32, cos.shape, 1)
        first_half = lane % HEAD_DIM < HEAD_DIM // 2
        for w in range(width // 128):
            cols = slice(w * 128, (w + 1) * 128)
            x = h_ref[:, cols].astype(F32)
            partner = jnp.where(first_half, pltpu.roll(x, 128 - HEAD_DIM // 2, 1), pltpu.roll(x, HEAD_DIM // 2, 1))
            o_ref[:, cols] = (x * cos + partner * sin).astype(o_ref.dtype)

    tile = pl.BlockSpec((ts, width), lambda i, j: (i, j))
    table = pl.BlockSpec((ts, 128), lambda i, j: (i, plain_block(j)))
    return pl.pallas_call(
        body,
        name=name,
        grid=(s // ts, n // width),
        in_specs=[tile, table, table],
        out_specs=tile,
        out_shape=jax.ShapeDtypeStruct((s, n), out_dtype),
        compiler_params=_params(("parallel", "parallel")),
    )(h, cos2, sin2)


SB_TQ = 512
SB_BLK = 128
SB_CLOSED = 110.0


def _sb_walk(n_trips, carry, key_blocks):
    def still_open(carry):
        lowest = functools.reduce(jnp.minimum, [c[0] for c in carry])
        return jnp.min(lowest) < SB_CLOSED

    def cond(state):
        t, go, _ = state
        return jnp.logical_and(t < n_trips, go)

    def body(state):
        t, _, carry = state
        carry = key_blocks(t, carry)
        return t + 1, still_open(carry), carry

    return lax.while_loop(cond, body, (jnp.int32(0), still_open(carry), carry))[2]


def _tri2(strict):
    row = lax.broadcasted_iota(jnp.int32, (2 * SB_BLK, SB_BLK), 0) % SB_BLK
    col = lax.broadcasted_iota(jnp.int32, (2 * SB_BLK, SB_BLK), 1)
    return (row > col if strict else row >= col).astype(MXU_DTYPE)


def _cumsum_dot(x, tri2):
    hi = x.astype(MXU_DTYPE)
    lo = x - hi.astype(F32)
    return _dot(jnp.concatenate([hi, lo.astype(MXU_DTYPE)], axis=1), tri2, NN)


def _sb_logits(z, valid):
    l1p = jnp.log(1.0 + jnp.exp(-jnp.abs(z)))
    sp = jnp.maximum(z, 0.0) + l1p
    if valid is not None:
        sp = jnp.where(valid, sp, 0.0)
    return sp, jnp.minimum(z, 0.0) - l1p


def _sb_weights(ls, after, valid):
    a = jnp.exp(ls - after)
    return a if valid is None else jnp.where(valid, a, 0.0)


def _sb_setup(q_ref, k_ref, v_ref, n_sub, scale):
    qs = [q_ref[0, u * SB_BLK:(u + 1) * SB_BLK, :] * scale for u in range(n_sub)]
    row = lax.broadcasted_iota(jnp.int32, (SB_BLK, SB_BLK), 0)
    col = lax.broadcasted_iota(jnp.int32, (SB_BLK, SB_BLK), 1)

    def load(jj):
        start = pl.multiple_of(jj * SB_BLK, SB_BLK)
        return start, k_ref[0, pl.ds(start, SB_BLK), :], v_ref[0, pl.ds(start, SB_BLK), :]

    return qs, col < row, load


def _sb_fwd(q, k, v, name):
    nh, s, dh = q.shape
    tq = min(SB_TQ, s)
    n_sub = tq // SB_BLK
    scale = dh ** -0.5

    def body(q_ref, k_ref, v_ref, o_ref, ob_ref):
        base = pl.program_id(1) * n_sub
        qs, diag_valid, load = _sb_setup(q_ref, k_ref, v_ref, n_sub, scale)
        tri_after = _tri2(True)

        def key_blocks(first, carry, diagonal):
            blocks = list(range(n_sub - 1, -1, -1))
            kv = {d: load(first + d) for d in blocks}
            tiles = [(d, u) for d in blocks for u in range(d if diagonal else 0, n_sub)]
            valid = {t: diag_valid if diagonal and t[0] == t[1] else None for t in tiles}
            z = {t: _dot(qs[t[1]], kv[t[0]][1], NT) for t in tiles}
            sp_ls = {t: _sb_logits(z[t], valid[t]) for t in tiles}
            inside = {t: _cumsum_dot(sp_ls[t][0], tri_after) for t in tiles}
            carry = list(carry)
            for t in tiles:
                after_c, acc = carry[t[1]]
                sp, ls = sp_ls[t]
                a = _sb_weights(ls, after_c + inside[t], valid[t])
                carry[t[1]] = (after_c + jnp.sum(sp, axis=1, keepdims=True), acc + _dot(a, kv[t[0]][2], NN))
            return tuple(carry)

        carry = tuple((jnp.zeros((SB_BLK, 1), F32), jnp.zeros((SB_BLK, dh), F32)) for _ in range(n_sub))
        carry = key_blocks(base, carry, True)
        carry = _sb_walk(pl.program_id(1), carry, lambda t, c: key_blocks(base - n_sub * (t + 1), c, False))
        for u in range(n_sub):
            rows = slice(u * SB_BLK, (u + 1) * SB_BLK)
            o_ref[0, rows, :] = carry[u][1]
            ob_ref[0, rows, :] = carry[u][1].astype(ob_ref.dtype)

    blk = pl.BlockSpec((1, tq, dh), lambda h, i: (h, i, 0))
    full = pl.BlockSpec((1, s, dh), lambda h, i: (h, 0, 0))
    return pl.pallas_call(
        body,
        name=name,
        grid=(nh, s // tq),
        in_specs=[blk, full, full],
        out_specs=[blk, blk],
        out_shape=[jax.ShapeDtypeStruct((nh, s, dh), F32), jax.ShapeDtypeStruct((nh, s, dh), MXU_DTYPE)],
        compiler_params=_params(("parallel", "parallel")),
    )(q, k, v)


def _sb_bwd(q, k, v, o, do, name):
    nh, s, dh = q.shape
    tq = min(SB_TQ, s)
    n_sub = tq // SB_BLK
    scale = dh ** -0.5

    def body(q_ref, k_ref, v_ref, o_ref, do_ref, dq_ref, dk_ref, dv_ref):
        i = pl.program_id(1)
        base = i * n_sub

        @pl.when(i == 0)
        def _():
            dk_ref[...] = jnp.zeros_like(dk_ref)
            dv_ref[...] = jnp.zeros_like(dv_ref)

        qs, diag_valid, load = _sb_setup(q_ref, k_ref, v_ref, n_sub, scale)
        dob = [do_ref[0, u * SB_BLK:(u + 1) * SB_BLK, :] for u in range(n_sub)]
        total = [jnp.sum(dob[u].astype(F32) * o_ref[0, u * SB_BLK:(u + 1) * SB_BLK, :], axis=1, keepdims=True) for u in range(n_sub)]
        tri_after = _tri2(True)
        tri_from = _tri2(False)

        def key_blocks(first, carry, diagonal):
            blocks = list(range(n_sub - 1, -1, -1))
            kv = {d: load(first + d) for d in blocks}
            tiles = [(d, u) for d in blocks for u in range(d if diagonal else 0, n_sub)]
            valid = {t: diag_valid if diagonal and t[0] == t[1] else None for t in tiles}
            z = {t: _dot(qs[t[1]], kv[t[0]][1], NT) for t in tiles}
            da = {t: _dot(dob[t[1]], kv[t[0]][2], NT) for t in tiles}
            sp_ls = {t: _sb_logits(z[t], valid[t]) for t in tiles}
            inside = {t: _cumsum_dot(sp_ls[t][0], tri_after) for t in tiles}
            after_run = [c[0] for c in carry]
            ab, dl = {}, {}
            for t in tiles:
                sp, ls = sp_ls[t]
                ab[t] = _sb_weights(ls, after_run[t[1]] + inside[t], valid[t]).astype(MXU_DTYPE)
                dl[t] = ab[t].astype(F32) * da[t]
                after_run[t[1]] = after_run[t[1]] + jnp.sum(sp, axis=1, keepdims=True)
            from_in = {t: _cumsum_dot(dl[t], tri_from) for t in tiles}
            from_run = [c[1] for c in carry]
            dq = [c[2] for c in carry]
            for d in blocks:
                start, kb, _ = kv[d]
                dk = dv = None
                for u in range(d if diagonal else 0, n_sub):
                    t = (d, u)
                    dz = dl[t] - jnp.exp(sp_ls[t][1]) * (dl[t] + total[u] - (from_run[u] + from_in[t]))
                    if valid[t] is not None:
                        dz = jnp.where(valid[t], dz, 0.0)
                    dzb = dz.astype(MXU_DTYPE)
                    from_run[u] = from_run[u] + jnp.sum(dl[t], axis=1, keepdims=True)
                    dq[u] = dq[u] + _dot(dzb, kb, NN)
                    dk_u, dv_u = _dot(dzb, qs[u], TN), _dot(ab[t], dob[u], TN)
                    dk = dk_u if dk is None else dk + dk_u
                    dv = dv_u if dv is None else dv + dv_u
                dk_ref[0, pl.ds(start, SB_BLK), :] += dk
                dv_ref[0, pl.ds(start, SB_BLK), :] += dv
            return tuple(zip(after_run, from_run, dq))

        carry = tuple((jnp.zeros((SB_BLK, 1), F32), jnp.zeros((SB_BLK, 1), F32), jnp.zeros((SB_BLK, dh), F32)) for _ in range(n_sub))
        carry = key_blocks(base, carry, True)
        carry = _sb_walk(i, carry, lambda t, c: key_blocks(base - n_sub * (t + 1), c, False))
        for u in range(n_sub):
            dq_ref[0, u * SB_BLK:(u + 1) * SB_BLK, :] = carry[u][2] * scale

    blk = pl.BlockSpec((1, tq, dh), lambda h, i: (h, i, 0))
    full = pl.BlockSpec((1, s, dh), lambda h, i: (h, 0, 0))
    shape = jax.ShapeDtypeStruct((nh, s, dh), F32)
    return pl.pallas_call(
        body,
        name=name,
        grid=(nh, s // tq),
        in_specs=[blk, full, full, blk, blk],
        out_specs=[blk, full, full],
        out_shape=[shape, shape, shape],
        compiler_params=_params(("parallel", "arbitrary")),
    )(q, k, v, o, do)


BAND_TQ = 1024


def _band_scores(q_ref, k_ref, i, sub, tq, length, max_dist, scale):
    t0 = i * tq + sub * ATT_BLK
    ks = pl.multiple_of(jnp.minimum(jnp.maximum(t0 - ATT_BLK, 0), length - 2 * ATT_BLK), ATT_BLK)
    qs = q_ref[0, sub * ATT_BLK:(sub + 1) * ATT_BLK, :] * scale
    kw = k_ref[0, pl.ds(ks, 2 * ATT_BLK), :]
    sc = _dot(qs, kw, NT)
    diff = (t0 + lax.broadcasted_iota(jnp.int32, sc.shape, 0)) - (ks + lax.broadcasted_iota(jnp.int32, sc.shape, 1))
    valid = (diff >= 0) & (diff <= max_dist)
    return ks, qs, kw, jnp.where(valid, sc, NEG_INF)


def _band_fwd(q, k, v, sinks, max_dist, name):
    bq, length, dh = q.shape
    group = bq // k.shape[0]
    tq = min(BAND_TQ, length)
    scale = dh ** -0.5
    n_sink = 0 if sinks is None else sinks.shape[0]

    def body(*refs):
        if n_sink:
            sink_ref, q_ref, k_ref, v_ref, o_ref, lse_ref = refs
            sink = sink_ref[pl.program_id(0) % n_sink]
        else:
            q_ref, k_ref, v_ref, o_ref, lse_ref = refs
        i = pl.program_id(1)
        scores = [_band_scores(q_ref, k_ref, i, sub, tq, length, max_dist, scale) for sub in range(tq // ATT_BLK)]
        for sub, (ks, _, _, sc) in enumerate(scores):
            m = jnp.max(sc, axis=1, keepdims=True)
            if n_sink:
                m = jnp.maximum(m, sink)
            e = jnp.exp(sc - m)
            den = jnp.sum(e, axis=1, keepdims=True)
            if n_sink:
                den = den + jnp.exp(sink - m)
            rows = slice(sub * ATT_BLK, (sub + 1) * ATT_BLK)
            o_ref[0, rows, :] = _dot(e / den, v_ref[0, pl.ds(ks, 2 * ATT_BLK), :], NN)
            lse_ref[0, rows, :] = m + jnp.log(den)

    qblk = pl.BlockSpec((1, tq, dh), lambda b, i: (b, i, 0))
    kfull = pl.BlockSpec((1, length, dh), lambda b, i: (b // group, 0, 0))
    in_specs = [qblk, kfull, kfull]
    args = [q, k, v]
    if n_sink:
        in_specs = [pl.BlockSpec(memory_space=pltpu.SMEM)] + in_specs
        args = [sinks] + args
    return pl.pallas_call(
        body,
        name=name,
        grid=(bq, length // tq),
        in_specs=in_specs,
        out_specs=[qblk, pl.BlockSpec((1, tq, 1), lambda b, i: (b, i, 0))],
        out_shape=[jax.ShapeDtypeStruct((bq, length, dh), F32), jax.ShapeDtypeStruct((bq, length, 1), F32)],
        compiler_params=_params(("parallel", "parallel")),
    )(*args)


def _band_bwd(q, k, v, do, lse, delta, sinks, max_dist, name):
    bq, length, dh = q.shape
    group = bq // k.shape[0]
    tq = min(BAND_TQ, length)
    scale = dh ** -0.5
    n_sink = 0 if sinks is None else sinks.shape[0]

    def body(*refs):
        if n_sink:
            sink_ref, q_ref, k_ref, v_ref, do_ref, lse_ref, dl_ref, dq_ref, dk_ref, dv_ref, ds_ref = refs
            sink = sink_ref[pl.program_id(0) % n_sink]
        else:
            q_ref, k_ref, v_ref, do_ref, lse_ref, dl_ref, dq_ref, dk_ref, dv_ref, ds_ref = refs
        i = pl.program_id(1)

        @pl.when((i == 0) & (pl.program_id(0) % group == 0))
        def _():
            dk_ref[...] = jnp.zeros_like(dk_ref)
            dv_ref[...] = jnp.zeros_like(dv_ref)

        @pl.when(i == 0)
        def _():
            ds_ref[...] = jnp.zeros_like(ds_ref)

        scores = [_band_scores(q_ref, k_ref, i, sub, tq, length, max_dist, scale) for sub in range(tq // ATT_BLK)]
        dps = [_dot(do_ref[0, sub * ATT_BLK:(sub + 1) * ATT_BLK, :], v_ref[0, pl.ds(ks, 2 * ATT_BLK), :], NT) for sub, (ks, _, _, _) in enumerate(scores)]
        for sub, (ks, qs, kw, sc) in enumerate(scores):
            rows = slice(sub * ATT_BLK, (sub + 1) * ATT_BLK)
            lse = lse_ref[0, rows, :]
            delta_r = dl_ref[0, rows, :]
            dob = do_ref[0, rows, :]
            p = jnp.exp(sc - lse)
            dsb = (p * (dps[sub] - delta_r)).astype(MXU_DTYPE)
            dq_ref[0, rows, :] = _dot(dsb, kw, NN) * scale
            dk_ref[0, pl.ds(ks, 2 * ATT_BLK), :] += _dot(dsb, qs, TN)
            dv_ref[0, pl.ds(ks, 2 * ATT_BLK), :] += _dot(p, dob, TN)
            if n_sink:
                ds_ref[...] += jnp.sum(-jnp.exp(sink - lse) * delta_r, axis=0, keepdims=True)

    qblk = pl.BlockSpec((1, tq, dh), lambda b, i: (b, i, 0))
    qcol = pl.BlockSpec((1, tq, 1), lambda b, i: (b, i, 0))
    kfull = pl.BlockSpec((1, length, dh), lambda b, i: (b // group, 0, 0))
    in_specs = [qblk, kfull, kfull, qblk, qcol, qcol]
    args = [q, k, v, do, lse, delta]
    if n_sink:
        in_specs = [pl.BlockSpec(memory_space=pltpu.SMEM)] + in_specs
        args = [sinks] + args
    kshape = jax.ShapeDtypeStruct((k.shape[0], length, dh), F32)
    return pl.pallas_call(
        body,
        name=name,
        grid=(bq, length // tq),
        in_specs=in_specs,
        out_specs=[qblk, kfull, kfull, pl.BlockSpec((1, 8, 128), lambda b, i: (b, 0, 0))],
        out_shape=[jax.ShapeDtypeStruct((bq, length, dh), F32), kshape, kshape, jax.ShapeDtypeStruct((bq, 8, 128), F32)],
        compiler_params=_params(("arbitrary", "arbitrary")),
    )(*args)


def _merge_weights(lse_refs):
    lses = [r[0] for r in lse_refs]
    m = functools.reduce(jnp.maximum, lses)
    es = [jnp.exp(l - m) for l in lses]
    den = functools.reduce(lambda a, b: a + b, es)
    return [e / den for e in es]


def _merge_fwd(outs, lses, name):
    n = len(outs)
    nh, s, dh = outs[0].shape
    ts = _tile(s, 1024, 8)

    def body(*refs):
        ws = _merge_weights(refs[n:2 * n])
        o = functools.reduce(lambda a, b: a + b, [w * r[0] for w, r in zip(ws, refs[:n])])
        refs[2 * n][0] = o
        refs[2 * n + 1][0] = o.astype(MXU_DTYPE)

    blk = pl.BlockSpec((1, ts, dh), lambda h, i: (h, i, 0))
    col = pl.BlockSpec((1, ts, 1), lambda h, i: (h, i, 0))
    return pl.pallas_call(
        body,
        name=name,
        grid=(nh, s // ts),
        in_specs=[blk] * n + [col] * n,
        out_specs=[blk, blk],
        out_shape=[jax.ShapeDtypeStruct((nh, s, dh), F32), jax.ShapeDtypeStruct((nh, s, dh), MXU_DTYPE)],
        compiler_params=_params(("parallel", "parallel")),
    )(*outs, *lses)


def _merge_bwd(do, o, lses, name):
    n = len(lses)
    nh, s, dh = o.shape
    ts = _tile(s, 1024, 8)

    def body(*refs):
        do_ref, o_ref = refs[:2]
        ws = _merge_weights(refs[2:2 + n])
        dof = do_ref[0].astype(F32)
        base = jnp.sum(dof * o_ref[0], axis=1, keepdims=True)
        for g in range(n):
            refs[2 + n + g][0] = (ws[g] * dof).astype(MXU_DTYPE)
            refs[2 + 2 * n + g][0] = ws[g] * base

    blk = pl.BlockSpec((1, ts, dh), lambda h, i: (h, i, 0))
    col = pl.BlockSpec((1, ts, 1), lambda h, i: (h, i, 0))
    res = pl.pallas_call(
        body,
        name=name,
        grid=(nh, s // ts),
        in_specs=[blk, blk] + [col] * n,
        out_specs=[blk] * n + [col] * n,
        out_shape=[jax.ShapeDtypeStruct((nh, s, dh), MXU_DTYPE)] * n + [jax.ShapeDtypeStruct((nh, s, 1), F32)] * n,
        compiler_params=_params(("parallel", "parallel")),
    )(do, o, *lses)
    return res[:n], res[n:]


def _to_heads(a, dil=1):
    s, n = a.shape
    nh = n // HEAD_DIM
    a = a.reshape(s // dil, dil, nh, HEAD_DIM).transpose(1, 2, 0, 3)
    return a.reshape(dil * nh, s // dil, HEAD_DIM)


def _from_heads(a, dil=1):
    b, ls, c = a.shape
    nh = b // dil
    return a.reshape(dil, nh, ls, c).transpose(2, 0, 1, 3).reshape(ls * dil, nh * c)


def _restride(a, dil):
    nh, s, c = a.shape
    return a.reshape(nh, s // dil, dil, c).transpose(2, 0, 1, 3).reshape(dil * nh, s // dil, c)


def _unstride(a, dil):
    b, ls, c = a.shape
    nh = b // dil
    return a.reshape(dil, nh, ls, c).transpose(1, 2, 0, 3).reshape(nh, ls * dil, c)


ANY = pl.BlockSpec(memory_space=pl.ANY)


def _position():
    x, y, c = lax.axis_index("x"), lax.axis_index("y"), lax.axis_index("c")
    return x, y, c, [(1 - x, y), (x, 1 - y), (1 - x, 1 - y)]


def _remote(src, dst, send_sems, recv_sems, k, to):
    return pltpu.make_async_remote_copy(src_ref=src, dst_ref=dst, send_sem=send_sems.at[k], recv_sem=recv_sems.at[k], device_id=to, device_id_type=MESH)


def _gather_chips(arrs, name):
    n = len(arrs)

    def body(*refs):
        f_refs, g_refs = refs[:n], refs[n:2 * n]
        send_sems, recv_sems, local_sems = refs[2 * n:]
        x, y, c, chips = _position()
        me, sibling = (x, y, c), (x, y, 1 - c)

        def half(j, ref, sel):
            rows = f_refs[j].shape[1] // 2
            return ref.at[:, pl.ds(sel * rows, rows), :]

        def slot(j, chip, sel):
            return half(j, g_refs[j].at[2 * chip[0] + chip[1]], sel)

        mine = [pltpu.make_async_copy(f_refs[j], g_refs[j].at[2 * x + y], local_sems.at[j]) for j in range(n)]
        first = [_remote(half(j, f_refs[j], c), slot(j, (x, y), c), send_sems, recv_sems, 6 * j + k, (*chip, c)) for j in range(n) for k, chip in enumerate(chips)]
        for cp in mine + first:
            cp.start()
        passed = []
        for j in range(n):
            for k, chip in enumerate(chips):
                _remote(slot(j, chip, c), slot(j, chip, c), send_sems, recv_sems, 6 * j + k, me).wait_recv()
                passed.append(_remote(slot(j, chip, c), slot(j, chip, c), send_sems, recv_sems, 6 * j + 3 + k, sibling))
                passed[-1].start()
        for j in range(n):
            for k, chip in enumerate(chips):
                _remote(slot(j, chip, 1 - c), slot(j, chip, 1 - c), send_sems, recv_sems, 6 * j + 3 + k, me).wait_recv()
        for cp in first + passed:
            cp.wait_send()
        for cp in mine:
            cp.wait()

    return pl.pallas_call(
        body,
        name=name,
        in_specs=[ANY] * n,
        out_specs=[ANY] * n,
        out_shape=[jax.ShapeDtypeStruct((N_CHIPS,) + a.shape, a.dtype) for a in arrs],
        scratch_shapes=[pltpu.SemaphoreType.DMA((6 * n,)), pltpu.SemaphoreType.DMA((6 * n,)), pltpu.SemaphoreType.DMA((n,))],
    )(*arrs)


def _swap_sibling(arrs, name):
    n = len(arrs)

    def body(*refs):
        x, y, c, _ = _position()
        send_sems, recv_sems = refs[2 * n:]
        copies = [_remote(refs[j], refs[n + j], send_sems, recv_sems, j, (x, y, 1 - c)) for j in range(n)]
        for cp in copies:
            cp.start()
        for cp in copies:
            cp.wait()

    return pl.pallas_call(
        body,
        name=name,
        in_specs=[ANY] * n,
        out_specs=[ANY] * n,
        out_shape=[jax.ShapeDtypeStruct(a.shape, a.dtype) for a in arrs],
        scratch_shapes=[pltpu.SemaphoreType.DMA((n,)), pltpu.SemaphoreType.DMA((n,))],
    )(*arrs)


def _exchange_chips(parts, name):
    n = len(parts)

    def body(*refs):
        p_refs, q_refs = refs[:n], refs[n:2 * n]
        send_sems, recv_sems, local_sems = refs[2 * n:]
        x, y, c, chips = _position()
        my_slot = 2 * x + y
        mine = [pltpu.make_async_copy(p_refs[j].at[my_slot], q_refs[j].at[my_slot], local_sems.at[j]) for j in range(n)]
        sends = [_remote(p_refs[j].at[2 * px + py], q_refs[j].at[my_slot], send_sems, recv_sems, 3 * j + k, (px, py, c)) for j in range(n) for k, (px, py) in enumerate(chips)]
        for cp in mine + sends:
            cp.start()
        for j in range(n):
            for k, (px, py) in enumerate(chips):
                landed = q_refs[j].at[2 * px + py]
                _remote(landed, landed, send_sems, recv_sems, 3 * j + k, (x, y, c)).wait_recv()
        for cp in sends:
            cp.wait_send()
        for cp in mine:
            cp.wait()

    return pl.pallas_call(
        body,
        name=name,
        in_specs=[ANY] * n,
        out_specs=[ANY] * n,
        out_shape=[jax.ShapeDtypeStruct(a.shape, a.dtype) for a in parts],
        scratch_shapes=[pltpu.SemaphoreType.DMA((3 * n,)), pltpu.SemaphoreType.DMA((3 * n,)), pltpu.SemaphoreType.DMA((n,))],
    )(*parts)


def _share_sibling(halves, name):
    n = len(halves)

    def body(*refs):
        h_refs, o_refs = refs[:n], refs[n:2 * n]
        send_sems, recv_sems, local_sems = refs[2 * n:]
        x, y, c, _ = _position()

        def rows(j, sel):
            r = h_refs[j].shape[0]
            return o_refs[j].at[pl.ds(sel * r, r), :]

        mine = [pltpu.make_async_copy(h_refs[j], rows(j, c), local_sems.at[j]) for j in range(n)]
        sends = [_remote(h_refs[j], rows(j, c), send_sems, recv_sems, j, (x, y, 1 - c)) for j in range(n)]
        for cp in mine + sends:
            cp.start()
        for j in range(n):
            _remote(rows(j, 1 - c), rows(j, 1 - c), send_sems, recv_sems, j, (x, y, c)).wait_recv()
        for cp in sends:
            cp.wait_send()
        for cp in mine:
            cp.wait()

    return pl.pallas_call(
        body,
        name=name,
        in_specs=[ANY] * n,
        out_specs=[ANY] * n,
        out_shape=[jax.ShapeDtypeStruct((2 * a.shape[0],) + a.shape[1:], a.dtype) for a in halves],
        scratch_shapes=[pltpu.SemaphoreType.DMA((n,)), pltpu.SemaphoreType.DMA((n,)), pltpu.SemaphoreType.DMA((n,))],
    )(*halves)


def _sum_rows(r, c, n_in):
    return _tile(r, max(16, (1 << 20) // (c * (n_in + 1))), 16)


def _add2(a, b, out_dtype, name):
    n, r, c = a.shape
    tr = _sum_rows(r, c, 2)

    def body(a_ref, b_ref, o_ref):
        o_ref[...] = (a_ref[...].astype(F32) + b_ref[...].astype(F32)).astype(o_ref.dtype)

    blk = pl.BlockSpec((1, tr, c), lambda s, i: (s, i, 0))
    return pl.pallas_call(
        body,
        name=name,
        grid=(n, r // tr),
        in_specs=[blk, blk],
        out_specs=blk,
        out_shape=jax.ShapeDtypeStruct(a.shape, out_dtype),
        compiler_params=_params(("parallel", "parallel")),
    )(a, b)


def _sum_slots(q, name):
    n, r, c = q.shape
    tr = _sum_rows(r, c, n)

    def body(q_ref, o_ref):
        acc = q_ref[0].astype(F32)
        for s in range(1, n):
            acc = acc + q_ref[s].astype(F32)
        o_ref[...] = acc

    return pl.pallas_call(
        body,
        name=name,
        grid=(r // tr,),
        in_specs=[pl.BlockSpec((n, tr, c), lambda i: (0, i, 0))],
        out_specs=pl.BlockSpec((tr, c), lambda i: (i, 0)),
        out_shape=jax.ShapeDtypeStruct((r, c), F32),
        compiler_params=_params(("parallel",)),
    )(q)


def _adamw(w, g, m, v, name):
    r, c = w.shape
    tr = _tile(r, 256, 8)
    c1 = 1.0 - ADAM_B1 ** ADAM_STEP
    c2 = 1.0 - ADAM_B2 ** ADAM_STEP

    def body(w_ref, g_ref, m_ref, v_ref, d_ref, nm_ref, nv_ref):
        g = g_ref[...]
        nm = ADAM_B1 * m_ref[...] + (1.0 - ADAM_B1) * g
        nv = ADAM_B2 * v_ref[...] + (1.0 - ADAM_B2) * (g * g)
        d_ref[...] = -ADAM_LR * ((nm / c1) / (jnp.sqrt(nv / c2) + ADAM_EPS) + ADAM_WD * w_ref[...])
        nm_ref[...] = nm
        nv_ref[...] = nv

    blk = pl.BlockSpec((tr, c), lambda i: (i, 0))
    shape = jax.ShapeDtypeStruct((r, c), F32)
    return pl.pallas_call(
        body,
        name=name,
        grid=(r // tr,),
        in_specs=[blk] * 4,
        out_specs=[blk] * 3,
        out_shape=[shape] * 3,
        compiler_params=_params(("parallel",)),
    )(w, g, m, v)


WEIGHTS = ("ffn1_w_in", "ffn1_w_out", "ffn2_w_in", "ffn2_w_out", "ln_g", "ln_b", "sb_w_in", "sb_w_out", "swa_w_in", "swa_sinks", "swa_w_out", "dil_w_in", "dil_w_out", "ple_w_proj", "ple_w_gate")
SHARD_AXIS = {"ffn1_w_in": 2, "ffn1_w_out": 1, "ffn2_w_in": 2, "ffn2_w_out": 1, "ln_g": 2, "ln_b": 2, "sb_w_in": 2, "sb_w_out": 1, "swa_w_in": 2, "swa_sinks": None, "swa_w_out": 1, "dil_w_in": 2, "dil_w_out": 1, "ple_w_proj": 2, "ple_w_gate": 1}
SMALL = ("ln_g", "ln_b", "swa_sinks")
SMALL_COLS = 128
SMALL_UNIT = 16 * SMALL_COLS


def _pack_small(pieces, lead):
    flat = jnp.concatenate([a.reshape(lead + (-1,)) for a in pieces], axis=-1)
    n = flat.shape[-1]
    flat = jnp.pad(flat, [(0, 0)] * len(lead) + [(0, -n % SMALL_UNIT)])
    return flat.reshape(lead + (-1, SMALL_COLS))


def _unpack_small(buf, shapes, lead):
    flat = buf.reshape(lead + (-1,))
    out, off = [], 0
    for shp in shapes:
        n = math.prod(shp)
        out.append(flat[..., off:off + n].reshape(lead + tuple(shp)))
        off += n
    return out


def _reduce_group(grads, c, wire, tag):
    keep = [lax.dynamic_slice_in_dim(g, c * (g.shape[1] // 2), g.shape[1] // 2, axis=1) for g in grads]
    give = [lax.dynamic_slice_in_dim(g, (1 - c) * (g.shape[1] // 2), g.shape[1] // 2, axis=1).astype(wire) for g in grads]
    got = _swap_sibling(give, f"{tag}_pair")
    part = [_add2(k, g, wire, f"{tag}_pair_sum{j}") for j, (k, g) in enumerate(zip(keep, got))]
    landed = _exchange_chips(part, f"{tag}_chips")
    halves = [_sum_slots(q, f"{tag}_chips_sum{j}") for j, q in enumerate(landed)]
    return _share_sibling(halves, f"{tag}_share")


def _full_from_shards(g, axis):
    g = jnp.moveaxis(g, 0, axis)
    return g.reshape(g.shape[:axis] + (g.shape[axis] * g.shape[axis + 1],) + g.shape[axis + 2:])


def _shards_from_full(a, axis):
    a = a.reshape(a.shape[:axis] + (N_CHIPS, a.shape[axis] // N_CHIPS) + a.shape[axis + 1:])
    return jnp.moveaxis(a, axis, 0)


def _ffn_forward(x, xb, w_in, w_out, g, b, alpha, tag):
    gate, up, act = _ffn_in(xb, w_in, f"{tag}_in")
    y = _mm(act, w_out, "nn", F32, f"{tag}_out")
    out, outb, xhat, rstd = _ln_fwd(x, y, g, b, alpha, 0.5, f"{tag}_ln")
    return (out, outb), dict(xb=xb, gate=gate, up=up, act=act, xhat=xhat, rstd=rstd)


def _ffn_backward(dout, saved, w_in, w_out, g, tag):
    dr, dyb, dg, db = _ln_bwd(*dout, saved["xhat"], saved["rstd"], g, 0.5, f"{tag}_ln_bwd")
    dh = _ffn_dact(dyb, w_out, saved["gate"], saved["up"], f"{tag}_dact")
    dw_out = _mm(saved["act"], dyb, "tn", F32, f"{tag}_dw_out")
    dw_in = _mm(saved["xb"], dh, "tn", F32, f"{tag}_dw_in", split_b=True)
    dxb = _mm(dh, w_in, "nt", F32, f"{tag}_dx", split_a=True)
    return dr, dxb, dw_in, dw_out, dg, db


def _sb_forward(xb, w_in, w_out, tag):
    nw = w_out.shape[0]
    h = _mm(xb, w_in, "nn", MXU_DTYPE, f"{tag}_qkv")
    q, k, v = (_to_heads(h[:, j * nw:(j + 1) * nw]) for j in range(3))
    o, ob = _sb_fwd(q, k, v, f"{tag}_att")
    ob = _from_heads(ob)
    return _mm(ob, w_out, "nn", F32, f"{tag}_proj"), dict(q=q, k=k, v=v, o=o, ob=ob)


def _sb_backward(dmix, saved, xb, w_in, w_out, tag):
    dw_out = _mm(saved["ob"], dmix, "tn", F32, f"{tag}_dw_out")
    do = _to_heads(_mm(dmix, w_out, "nt", MXU_DTYPE, f"{tag}_do"))
    dq, dk, dv = _sb_bwd(saved["q"], saved["k"], saved["v"], saved["o"], do, f"{tag}_att_bwd")
    dh = jnp.concatenate([_from_heads(t) for t in (dq, dk, dv)], axis=1).astype(MXU_DTYPE)
    dw_in = _mm(xb, dh, "tn", F32, f"{tag}_dw_in")
    return _mm(dh, w_in, "nt", F32, f"{tag}_dx"), dw_in, dw_out


def _swa_forward(xb, w_in, sinks, w_out, tables, tag):
    nq = w_out.shape[0]
    nkv = (w_in.shape[1] - nq) // 2
    h = _mm(xb, w_in, "nn", F32, f"{tag}_qkv")
    n_rot = (nq + nkv) // 128
    hb = _rope(h, *tables, lambda j: j // n_rot, MXU_DTYPE, f"{tag}_rope")
    q, k, v = _to_heads(hb[:, :nq]), _to_heads(hb[:, nq:nq + nkv]), _to_heads(hb[:, nq + nkv:])
    o, lse = _band_fwd(q, k, v, sinks, SWA_WINDOW - 1, f"{tag}_att")
    o, ob = _merge_fwd([o], [lse], f"{tag}_cast")
    ob = _from_heads(ob)
    return _mm(ob, w_out, "nn", F32, f"{tag}_proj"), dict(q=q, k=k, v=v, o=o, lse=lse, ob=ob, n_rot=n_rot)


def _swa_backward(dmix, saved, xb, w_in, sinks, w_out, tables, tag):
    dw_out = _mm(saved["ob"], dmix, "tn", F32, f"{tag}_dw_out")
    do = _to_heads(_mm(dmix, w_out, "nt", MXU_DTYPE, f"{tag}_do"))
    (dog,), (delta,) = _merge_bwd(do, saved["o"], [saved["lse"]], f"{tag}_delta")
    dq, dk, dv, dsink = _band_bwd(saved["q"], saved["k"], saved["v"], dog, saved["lse"], delta, sinks, SWA_WINDOW - 1, f"{tag}_att_bwd")
    dh = jnp.concatenate([_from_heads(t) for t in (dq, dk, dv)], axis=1)
    n_rot = saved["n_rot"]
    dhb = _rope(dh, tables[0], -tables[1], lambda j: j // n_rot, MXU_DTYPE, f"{tag}_rope_bwd")
    dw_in = _mm(xb, dhb, "tn", F32, f"{tag}_dw_in")
    return _mm(dhb, w_in, "nt", F32, f"{tag}_dx"), dw_in, dw_out, dsink[:, 0, 0]


def _dil_forward(xb, w_in, w_out, tables, tag):
    nw = w_out.shape[0]
    h = _mm(xb, w_in, "nn", F32, f"{tag}_qkv")
    hb = _rope(h, *tables, lambda j: (j % 3) // 2, MXU_DTYPE, f"{tag}_rope", width=nw)
    qkv, outs, lses = [], [], []
    for gi, (win, dil) in enumerate(DIL_GROUPS):
        base = gi * 3 * nw
        q, k, v = (_to_heads(hb[:, base + j * nw:base + (j + 1) * nw], dil) for j in range(3))
        o, lse = _band_fwd(q, k, v, None, win // dil, f"{tag}_att{gi}")
        qkv.append((q, k, v))
        outs.append(_unstride(o, dil))
        lses.append(_unstride(lse, dil))
    o, ob = _merge_fwd(outs, lses, f"{tag}_merge")
    ob = _from_heads(ob)
    return _mm(ob, w_out, "nn", F32, f"{tag}_proj"), dict(qkv=qkv, o=o, lses=lses, ob=ob)


def _dil_backward(dmix, saved, xb, w_in, w_out, tables, tag):
    dw_out = _mm(saved["ob"], dmix, "tn", F32, f"{tag}_dw_out")
    do = _to_heads(_mm(dmix, w_out, "nt", MXU_DTYPE, f"{tag}_do"))
    dogs, deltas = _merge_bwd(do, saved["o"], saved["lses"], f"{tag}_merge_bwd")
    parts = []
    for gi, (win, dil) in enumerate(DIL_GROUPS):
        q, k, v = saved["qkv"][gi]
        dq, dk, dv, _ = _band_bwd(q, k, v, _restride(dogs[gi], dil), _restride(saved["lses"][gi], dil), _restride(deltas[gi], dil), None, win // dil, f"{tag}_att{gi}_bwd")
        parts += [_from_heads(t, dil) for t in (dq, dk, dv)]
    dh = jnp.concatenate(parts, axis=1)
    dhb = _rope(dh, tables[0], -tables[1], lambda j: (j % 3) // 2, MXU_DTYPE, f"{tag}_rope_bwd", width=w_out.shape[0])
    dw_in = _mm(xb, dhb, "tn", F32, f"{tag}_dw_in")
    return _mm(dhb, w_in, "nt", F32, f"{tag}_dx"), dw_in, dw_out


def kernel(x, p, ffn1_w_in, ffn1_w_out, ffn2_w_in, ffn2_w_out, ln_g, ln_b, sb_w_in, sb_w_out, swa_w_in, swa_sinks, swa_w_out, dil_w_in, dil_w_out, ple_w_proj, ple_w_gate, loss_target, m_ffn1_w_in, m_ffn1_w_out, m_ffn2_w_in, m_ffn2_w_out, m_ln_g, m_ln_b, m_sb_w_in, m_sb_w_out, m_swa_w_in, m_swa_sinks, m_swa_w_out, m_dil_w_in, m_dil_w_out, m_ple_w_proj, m_ple_w_gate, v_ffn1_w_in, v_ffn1_w_out, v_ffn2_w_in, v_ffn2_w_out, v_ln_g, v_ln_b, v_sb_w_in, v_sb_w_out, v_swa_w_in, v_swa_sinks, v_swa_w_out, v_dil_w_in, v_dil_w_out, v_ple_w_proj, v_ple_w_gate):
    shard = dict(ffn1_w_in=ffn1_w_in, ffn1_w_out=ffn1_w_out, ffn2_w_in=ffn2_w_in, ffn2_w_out=ffn2_w_out, ln_g=ln_g, ln_b=ln_b, sb_w_in=sb_w_in, sb_w_out=sb_w_out, swa_w_in=swa_w_in, swa_sinks=swa_sinks, swa_w_out=swa_w_out, dil_w_in=dil_w_in, dil_w_out=dil_w_out, ple_w_proj=ple_w_proj, ple_w_gate=ple_w_gate)
    mom_m = dict(ffn1_w_in=m_ffn1_w_in, ffn1_w_out=m_ffn1_w_out, ffn2_w_in=m_ffn2_w_in, ffn2_w_out=m_ffn2_w_out, ln_g=m_ln_g, ln_b=m_ln_b, sb_w_in=m_sb_w_in, sb_w_out=m_sb_w_out, swa_w_in=m_swa_w_in, swa_sinks=m_swa_sinks, swa_w_out=m_swa_w_out, dil_w_in=m_dil_w_in, dil_w_out=m_dil_w_out, ple_w_proj=m_ple_w_proj, ple_w_gate=m_ple_w_gate)
    mom_v = dict(ffn1_w_in=v_ffn1_w_in, ffn1_w_out=v_ffn1_w_out, ffn2_w_in=v_ffn2_w_in, ffn2_w_out=v_ffn2_w_out, ln_g=v_ln_g, ln_b=v_ln_b, sb_w_in=v_sb_w_in, sb_w_out=v_sb_w_out, swa_w_in=v_swa_w_in, swa_sinks=v_swa_sinks, swa_w_out=v_swa_w_out, dil_w_in=v_dil_w_in, dil_w_out=v_dil_w_out, ple_w_proj=v_ple_w_proj, ple_w_gate=v_ple_w_gate)
    depth = ffn1_w_in.shape[0]
    alpha = (2 * depth) ** 0.25
    c = lax.axis_index("c")

    mats = [n for n in WEIGHTS if n not in SMALL]
    got = _gather_chips([shard[n].astype(MXU_DTYPE) for n in mats] + [_pack_small([ln_g, ln_b], ())[None]], "gather_weights")
    full = {"swa_sinks": swa_sinks}
    for n, g in zip(mats, got):
        full[n] = _full_from_shards(g, SHARD_AXIS[n])
    for n, g in zip(("ln_g", "ln_b"), _unpack_small(got[-1][:, 0], [ln_g.shape, ln_b.shape], (N_CHIPS,))):
        full[n] = _full_from_shards(g, SHARD_AXIS[n])

    seq = x.shape[1]
    tables = _rope_tables(seq)
    xf = x[0]
    xb = xf.astype(MXU_DTYPE)
    saved = []
    for i in range(depth):
        kind, j = i % 3, i // 3
        sv = {}
        (x1, x1b), sv["ffn1"] = _ffn_forward(xf, xb, full["ffn1_w_in"][i], full["ffn1_w_out"][i], full["ln_g"][i, 0], full["ln_b"][i, 0], alpha, f"l{i}_ffn1")
        if kind == 0:
            mix, sv["mix"] = _sb_forward(x1b, full["sb_w_in"][j], full["sb_w_out"][j], f"l{i}_sb")
        elif kind == 1:
            mix, sv["mix"] = _swa_forward(x1b, full["swa_w_in"][j], swa_sinks[j], full["swa_w_out"][j], tables, f"l{i}_swa")
        else:
            mix, sv["mix"] = _dil_forward(x1b, full["dil_w_in"][j], full["dil_w_out"][j], tables, f"l{i}_dil")
        x2, x2b, sv["xhat2"], sv["rstd2"] = _ln_fwd(x1, mix, full["ln_g"][i, 1], full["ln_b"][i, 1], alpha, 1.0, f"l{i}_mix_ln")
        sv["x1b"] = x1b
        (x3, x3b), sv["ffn2"] = _ffn_forward(x2, x2b, full["ffn2_w_in"][i], full["ffn2_w_out"][i], full["ln_g"][i, 2], full["ln_b"][i, 2], alpha, f"l{i}_ffn2")
        xf, xb, sv["u"], sv["e"] = _ple_fwd(x3, x3b, p[i, 0], full["ple_w_gate"][i], full["ple_w_proj"][i], f"l{i}_ple")
        sv["x3b"] = x3b
        saved.append(sv)

    loss_part, dy = _loss(xf, loss_target[0], "loss")
    loss = lax.psum(loss_part[0, 0], ("x", "y", "c"))

    grads = {n: [None] * full[n].shape[0] for n in WEIGHTS if n not in ("ln_g", "ln_b")}
    gsum = {n: [None] * full[n].shape[0] for n in mats}
    dln_g = [[None] * 3 for _ in range(depth)]
    dln_b = [[None] * 3 for _ in range(depth)]
    dout = (dy, None, 1.0)
    for i in reversed(range(depth)):
        kind, j = i % 3, i // 3
        mixer = ("sb", "swa", "dil")[kind]
        sv = saved[i]
        dx4, dub, deb = _ple_bwd(*dout, sv["u"], sv["e"], f"l{i}_ple_bwd")
        grads["ple_w_gate"][i] = _mm(sv["x3b"], dub, "tn", F32, f"l{i}_ple_dw_gate")
        grads["ple_w_proj"][i] = _mm(p[i, 0], deb, "tn", F32, f"l{i}_ple_dw_proj")
        dxb = _mm(dub, full["ple_w_gate"][i], "nt", F32, f"l{i}_ple_dx")
        dr, dxb, grads["ffn2_w_in"][i], grads["ffn2_w_out"][i], dln_g[i][2], dln_b[i][2] = _ffn_backward((dx4, dxb, 1.0), sv["ffn2"], full["ffn2_w_in"][i], full["ffn2_w_out"][i], full["ln_g"][i, 2], f"l{i}_ffn2")
        dr, dmix, dln_g[i][1], dln_b[i][1] = _ln_bwd(dr, dxb, alpha, sv["xhat2"], sv["rstd2"], full["ln_g"][i, 1], 1.0, f"l{i}_mix_ln_bwd")
        if kind == 0:
            dxb, grads["sb_w_in"][j], grads["sb_w_out"][j] = _sb_backward(dmix, sv["mix"], sv["x1b"], full["sb_w_in"][j], full["sb_w_out"][j], f"l{i}_sb")
        elif kind == 1:
            dxb, grads["swa_w_in"][j], grads["swa_w_out"][j], grads["swa_sinks"][j] = _swa_backward(dmix, sv["mix"], sv["x1b"], full["swa_w_in"][j], swa_sinks[j], full["swa_w_out"][j], tables, f"l{i}_swa")
        else:
            dxb, grads["dil_w_in"][j], grads["dil_w_out"][j] = _dil_backward(dmix, sv["mix"], sv["x1b"], full["dil_w_in"][j], full["dil_w_out"][j], tables, f"l{i}_dil")
        dr, dxb, grads["ffn1_w_in"][i], grads["ffn1_w_out"][i], dln_g[i][0], dln_b[i][0] = _ffn_backward((dr, dxb, alpha), sv["ffn1"], full["ffn1_w_in"][i], full["ffn1_w_out"][i], full["ln_g"][i, 0], f"l{i}_ffn1")
        dout = (dr, dxb, alpha)
        layer = [("ffn1_w_in", i), ("ffn1_w_out", i), (f"{mixer}_w_in", j), (f"{mixer}_w_out", j), ("ffn2_w_in", i), ("ffn2_w_out", i), ("ple_w_proj", i), ("ple_w_gate", i)]
        summed = _reduce_group([_shards_from_full(grads[n][k], SHARD_AXIS[n] - 1) for n, k in layer], c, MXU_DTYPE, f"l{i}_reduce")
        for (n, k), g in zip(layer, summed):
            gsum[n][k] = g
    grad_x = _axpy(*dout, "grad_x")[None]

    gshard = {n: jnp.stack(g) for n, g in gsum.items()}
    small = [
        _shards_from_full(jnp.stack([jnp.concatenate(r, axis=0) for r in dln_g]), SHARD_AXIS["ln_g"]),
        _shards_from_full(jnp.stack([jnp.concatenate(r, axis=0) for r in dln_b]), SHARD_AXIS["ln_b"]),
        jnp.broadcast_to(jnp.stack(grads["swa_sinks"])[None], (N_CHIPS,) + swa_sinks.shape),
    ]
    (small_sum,) = _reduce_group([_pack_small(small, (N_CHIPS,))], c, F32, "small_reduce")
    for n, g in zip(SMALL, _unpack_small(small_sum, [shard[n].shape for n in SMALL], ())):
        gshard[n] = g

    delta, new_m, new_v = {}, {}, {}
    for n in WEIGHTS:
        shp = shard[n].shape
        two_d = (-1, shp[-1])
        d, nm, nv = _adamw(shard[n].reshape(two_d), gshard[n].reshape(two_d), mom_m[n].reshape(two_d), mom_v[n].reshape(two_d), f"adamw_{n}")
        delta[n], new_m[n], new_v[n] = d.reshape(shp), nm.reshape(shp), nv.reshape(shp)

    return (loss, grad_x, *[gshard[n] for n in WEIGHTS], *[delta[n] for n in WEIGHTS], *[new_m[n] for n in WEIGHTS], *[new_v[n] for n in WEIGHTS])
```

```python
import functools
import math

import jax
import jax.numpy as jnp
from jax import lax
from jax.experimental import pallas as pl
from jax.experimental.pallas import tpu as pltpu

F32 = jnp.float32
MXU_DTYPE = jnp.bfloat16
MESH = pl.DeviceIdType.MESH

HEAD_DIM = 64
ATT_BLK = 128
SWA_WINDOW = 128
DIL_GROUPS = ((128, 1), (512, 4), (2048, 16))
LN_EPS = 1e-5
ROPE_THETA = 10000.0
NEG_INF = -1e30
ADAM_LR, ADAM_B1, ADAM_B2, ADAM_EPS, ADAM_WD, ADAM_STEP = 0.001, 0.9, 0.999, 1e-08, 0.01, 10

VMEM_LIMIT_BYTES = 56 * 1024 * 1024
N_CHIPS = 4


def _params(sem=None):
    return pltpu.CompilerParams(dimension_semantics=sem, vmem_limit_bytes=VMEM_LIMIT_BYTES)


def _tile(n, target, unit):
    t = (min(target, n) // unit) * unit
    while t >= unit:
        if n % t == 0:
            return t
        t -= unit
    return n


def _dot(a, b, dims):
    return lax.dot_general(a.astype(MXU_DTYPE), b.astype(MXU_DTYPE), (dims, ((), ())), preferred_element_type=F32)


NN = ((1,), (0,))
NT = ((1,), (1,))
TN = ((0,), (0,))


def _mm(a, b, mode, out_dtype, name, split_a=False, split_b=False, tm=1024, tn=1408, tk=1408):
    dims = {"nn": NN, "nt": NT, "tn": TN}[mode]
    if split_a:
        m, k = a.shape[1], 2 * a.shape[2]
    elif mode == "tn":
        k, m = a.shape
    else:
        m, k = a.shape
    if split_b:
        n = 2 * b.shape[2]
    elif mode == "nt":
        n = b.shape[0]
    else:
        n = b.shape[1]
    tm = _tile(m, tm, 128)
    tn = _tile(n // 2 if split_b else n, tn, 128)
    tk = _tile(k // 2 if split_a else k, tk, 128)
    nk = k // tk
    nk_half = nk // 2
    nn_half = (n // tn) // 2

    if split_a:
        a_spec = pl.BlockSpec((None, tm, tk), lambda i, j, kk: (kk // nk_half, i, kk % nk_half))
    elif mode == "tn":
        a_spec = pl.BlockSpec((tk, tm), lambda i, j, kk: (kk, i))
    else:
        a_spec = pl.BlockSpec((tm, tk), lambda i, j, kk: (i, kk))
    if split_b:
        b_spec = pl.BlockSpec((None, tk, tn), lambda i, j, kk: (j // nn_half, kk, j % nn_half))
    elif mode == "nt":
        b_spec = pl.BlockSpec((tn, tk), lambda i, j, kk: (j, kk))
    else:
        b_spec = pl.BlockSpec((tk, tn), lambda i, j, kk: (kk, j))

    def body(a_ref, b_ref, o_ref, acc_ref):
        kk = pl.program_id(2)

        @pl.when(kk == 0)
        def _():
            acc_ref[...] = jnp.zeros_like(acc_ref)

        acc_ref[...] += _dot(a_ref[...], b_ref[...], dims)

        @pl.when(kk == nk - 1)
        def _():
            o_ref[...] = acc_ref[...].astype(o_ref.dtype)

    return pl.pallas_call(
        body,
        name=name,
        grid=(m // tm, n // tn, nk),
        in_specs=[a_spec, b_spec],
        out_specs=pl.BlockSpec((tm, tn), lambda i, j, kk: (i, j)),
        out_shape=jax.ShapeDtypeStruct((m, n), out_dtype),
        scratch_shapes=[pltpu.VMEM((tm, tn), F32)],
        compiler_params=_params(("parallel", "parallel", "arbitrary")),
    )(a, b)


def _sigmoid(x):
    return 1.0 / (1.0 + jnp.exp(-x))


def _ffn_in(xb, w_in, name):
    s, d = xb.shape
    f = w_in.shape[1] // 2
    tm = _tile(s, 512, 128)
    tn = _tile(f, 1408, 128)
    nj = f // tn

    def body(x_ref, wg_ref, wu_ref, g_ref, u_ref, a_ref):
        x = x_ref[...]
        g = _dot(x, wg_ref[...], NN)
        u = _dot(x, wu_ref[...], NN)
        g_ref[...] = g.astype(g_ref.dtype)
        u_ref[...] = u.astype(u_ref.dtype)
        a_ref[...] = (g * _sigmoid(g) * u).astype(a_ref.dtype)

    out = pl.BlockSpec((tm, tn), lambda i, j: (i, j))
    return pl.pallas_call(
        body,
        name=name,
        grid=(s // tm, nj),
        in_specs=[
            pl.BlockSpec((tm, d), lambda i, j: (i, 0)),
            pl.BlockSpec((d, tn), lambda i, j: (0, j)),
            pl.BlockSpec((d, tn), lambda i, j: (0, j + nj)),
        ],
        out_specs=[out, out, out],
        out_shape=[
            jax.ShapeDtypeStruct((s, f), MXU_DTYPE),
            jax.ShapeDtypeStruct((s, f), MXU_DTYPE),
            jax.ShapeDtypeStruct((s, f), MXU_DTYPE),
        ],
        compiler_params=_params(("parallel", "parallel")),
    )(xb, w_in, w_in)


def _ffn_dact(dyb, w_out, gate, up, name):
    s, d = dyb.shape
    f = w_out.shape[0]
    tm = _tile(s, 512, 128)
    tn = _tile(f, 1408, 128)

    def body(dy_ref, w_ref, g_ref, u_ref, o_ref):
        dact = _dot(dy_ref[...], w_ref[...], NT)
        g = g_ref[...].astype(F32)
        sig = _sigmoid(g)
        o_ref[0] = (dact * u_ref[...].astype(F32) * (sig * (1.0 + g * (1.0 - sig)))).astype(o_ref.dtype)
        o_ref[1] = (dact * (g * sig)).astype(o_ref.dtype)

    tile = pl.BlockSpec((tm, tn), lambda i, j: (i, j))
    return pl.pallas_call(
        body,
        name=name,
        grid=(s // tm, f // tn),
        in_specs=[
            pl.BlockSpec((tm, d), lambda i, j: (i, 0)),
            pl.BlockSpec((tn, d), lambda i, j: (j, 0)),
            tile,
            tile,
        ],
        out_specs=pl.BlockSpec((2, tm, tn), lambda i, j: (0, i, j)),
        out_shape=jax.ShapeDtypeStruct((2, s, f), MXU_DTYPE),
        compiler_params=_params(("parallel", "parallel")),
    )(dyb, w_out, gate, up)


def _ple_fwd(x, xb, p, w_gate, w_proj, name):
    s, d = x.shape
    pd = p.shape[1]
    tm = _tile(s, 1024, 128)
    tn = _tile(d, 512, 128)

    def body(x_ref, xb_ref, p_ref, wg_ref, wp_ref, o_ref, ob_ref, u_ref, e_ref):
        u = _dot(xb_ref[...], wg_ref[...], NN)
        e = _dot(p_ref[...], wp_ref[...], NN)
        out = x_ref[...] + _sigmoid(u) * e
        o_ref[...] = out
        ob_ref[...] = out.astype(ob_ref.dtype)
        u_ref[...] = u
        e_ref[...] = e

    tile = pl.BlockSpec((tm, tn), lambda i, j: (i, j))
    return pl.pallas_call(
        body,
        name=name,
        grid=(s // tm, d // tn),
        in_specs=[
            tile,
            pl.BlockSpec((tm, d), lambda i, j: (i, 0)),
            pl.BlockSpec((tm, pd), lambda i, j: (i, 0)),
            pl.BlockSpec((d, tn), lambda i, j: (0, j)),
            pl.BlockSpec((pd, tn), lambda i, j: (0, j)),
        ],
        out_specs=[tile, tile, tile, tile],
        out_shape=[
            jax.ShapeDtypeStruct((s, d), F32),
            jax.ShapeDtypeStruct((s, d), MXU_DTYPE),
            jax.ShapeDtypeStruct((s, d), F32),
            jax.ShapeDtypeStruct((s, d), F32),
        ],
        compiler_params=_params(("parallel", "parallel")),
    )(x, xb, p, w_gate, w_proj)


def _rows_spec(ts, d):
    return pl.BlockSpec((ts, d), lambda i: (i, 0))


def _ln_fwd(x, y, g, b, alpha, beta, name):
    s, d = x.shape
    ts = _tile(s, 512, 8)

    def body(x_ref, y_ref, g_ref, b_ref, o_ref, ob_ref, xh_ref, rs_ref):
        r = alpha * x_ref[...] + beta * y_ref[...]
        mu = jnp.mean(r, axis=1, keepdims=True)
        cen = r - mu
        var = jnp.mean(cen * cen, axis=1, keepdims=True)
        rstd = lax.rsqrt(var + LN_EPS)
        xhat = cen * rstd
        out = xhat * g_ref[...] + b_ref[...]
        o_ref[...] = out
        ob_ref[...] = out.astype(ob_ref.dtype)
        xh_ref[...] = xhat
        rs_ref[...] = rstd

    vec = pl.BlockSpec((1, d), lambda i: (0, 0))
    return pl.pallas_call(
        body,
        name=name,
        grid=(s // ts,),
        in_specs=[_rows_spec(ts, d), _rows_spec(ts, d), vec, vec],
        out_specs=[_rows_spec(ts, d), _rows_spec(ts, d), _rows_spec(ts, d), _rows_spec(ts, 1)],
        out_shape=[
            jax.ShapeDtypeStruct((s, d), F32),
            jax.ShapeDtypeStruct((s, d), MXU_DTYPE),
            jax.ShapeDtypeStruct((s, d), F32),
            jax.ShapeDtypeStruct((s, 1), F32),
        ],
        compiler_params=_params(("parallel",)),
    )(x, y, g.reshape(1, d), b.reshape(1, d))


def _ln_bwd(ga, gb, ca, xhat, rstd, g, beta, name):
    s, d = xhat.shape
    ts = _tile(s, 512, 8)

    def body(ga_ref, gb_ref, xh_ref, rs_ref, g_ref, dr_ref, dyb_ref, dg_ref, db_ref):
        @pl.when(pl.program_id(0) == 0)
        def _():
            dg_ref[...] = jnp.zeros_like(dg_ref)
            db_ref[...] = jnp.zeros_like(db_ref)

        dout = ca * ga_ref[...] + gb_ref[...]
        xhat = xh_ref[...]
        dg_ref[...] += jnp.sum(dout * xhat, axis=0, keepdims=True)
        db_ref[...] += jnp.sum(dout, axis=0, keepdims=True)
        dxh = dout * g_ref[...]
        m1 = jnp.mean(dxh, axis=1, keepdims=True)
        m2 = jnp.mean(dxh * xhat, axis=1, keepdims=True)
        dr = rs_ref[...] * (dxh - m1 - xhat * m2)
        dr_ref[...] = dr
        dyb_ref[...] = (beta * dr).astype(dyb_ref.dtype)

    vec = pl.BlockSpec((1, d), lambda i: (0, 0))
    return pl.pallas_call(
        body,
        name=name,
        grid=(s // ts,),
        in_specs=[_rows_spec(ts, d), _rows_spec(ts, d), _rows_spec(ts, d), _rows_spec(ts, 1), vec],
        out_specs=[_rows_spec(ts, d), _rows_spec(ts, d), vec, vec],
        out_shape=[
            jax.ShapeDtypeStruct((s, d), F32),
            jax.ShapeDtypeStruct((s, d), MXU_DTYPE),
            jax.ShapeDtypeStruct((1, d), F32),
            jax.ShapeDtypeStruct((1, d), F32),
        ],
        compiler_params=_params(("arbitrary",)),
    )(ga, gb, xhat, rstd, g.reshape(1, d))


def _ple_bwd(ga, gb, ca, u, e, name):
    s, d = u.shape
    ts = _tile(s, 512, 8)
    grads = [ga] if gb is None else [ga, gb]

    def body(*refs):
        u_ref, e_ref, dx_ref, du_ref, de_ref = refs[len(grads):]
        dx = ca * refs[0][...]
        if gb is not None:
            dx = dx + refs[1][...]
        sig = _sigmoid(u_ref[...])
        dx_ref[...] = dx
        du_ref[...] = (dx * e_ref[...] * sig * (1.0 - sig)).astype(du_ref.dtype)
        de_ref[...] = (dx * sig).astype(de_ref.dtype)

    return pl.pallas_call(
        body,
        name=name,
        grid=(s // ts,),
        in_specs=[_rows_spec(ts, d)] * (len(grads) + 2),
        out_specs=[_rows_spec(ts, d)] * 3,
        out_shape=[
            jax.ShapeDtypeStruct((s, d), F32),
            jax.ShapeDtypeStruct((s, d), MXU_DTYPE),
            jax.ShapeDtypeStruct((s, d), MXU_DTYPE),
        ],
        compiler_params=_params(("parallel",)),
    )(*grads, u, e)


def _axpy(ga, gb, ca, name):
    s, d = ga.shape
    ts = _tile(s, 512, 8)

    def body(ga_ref, gb_ref, o_ref):
        o_ref[...] = ca * ga_ref[...] + gb_ref[...]

    return pl.pallas_call(
        body,
        name=name,
        grid=(s // ts,),
        in_specs=[_rows_spec(ts, d)] * 2,
        out_specs=_rows_spec(ts, d),
        out_shape=jax.ShapeDtypeStruct((s, d), F32),
        compiler_params=_params(("parallel",)),
    )(ga, gb)


def _loss(y, target, name):
    s, d = y.shape
    ts = _tile(s, 512, 8)

    def body(y_ref, t_ref, l_ref, dy_ref):
        @pl.when(pl.program_id(0) == 0)
        def _():
            l_ref[...] = jnp.zeros_like(l_ref)

        err = y_ref[...] - t_ref[...]
        l_ref[...] += (0.5 / d) * jnp.sum(jnp.sum(err * err, axis=1, keepdims=True), axis=0, keepdims=True)
        dy_ref[...] = err * (1.0 / d)

    return pl.pallas_call(
        body,
        name=name,
        grid=(s // ts,),
        in_specs=[_rows_spec(ts, d)] * 2,
        out_specs=[pl.BlockSpec((1, 1), lambda i: (0, 0)), _rows_spec(ts, d)],
        out_shape=[jax.ShapeDtypeStruct((1, 1), F32), jax.ShapeDtypeStruct((s, d), F32)],
        compiler_params=_params(("arbitrary",)),
    )(y, target)


def _rope_tables(seq):
    pos = jnp.arange(seq, dtype=F32)
    inv = ROPE_THETA ** (-jnp.arange(0, HEAD_DIM, 2, dtype=F32) / HEAD_DIM)
    ang = pos[:, None] * inv[None, :]
    cos, sin = jnp.cos(ang), jnp.sin(ang)
    cos2 = jnp.concatenate([cos, cos, cos, cos, jnp.ones((seq, 128), F32)], axis=1)
    sin2 = jnp.concatenate([-sin, sin, -sin, sin, jnp.zeros((seq, 128), F32)], axis=1)
    return cos2, sin2


def _rope(h, cos2, sin2, plain_block, out_dtype, name, width=128):
    s, n = h.shape
    ts = _tile(s, 512, 8)

    def body(h_ref, c_ref, s_ref, o_ref):
        cos, sin = c_ref[...], s_ref[...]
        lane = lax.broadcasted_iota(jnp.int32, cos.shape, 1)
        first_half = lane % HEAD_DIM < HEAD_DIM // 2
        for w in range(width // 128):
            cols = slice(w * 128, (w + 1) * 128)
            x = h_ref[:, cols].astype(F32)
            partner = jnp.where(first_half, pltpu.roll(x, 128 - HEAD_DIM // 2, 1), pltpu.roll(x, HEAD_DIM // 2, 1))
            o_ref[:, cols] = (x * cos + partner * sin).astype(o_ref.dtype)

    tile = pl.BlockSpec((ts, width), lambda i, j: (i, j))
    table = pl.BlockSpec((ts, 128), lambda i, j: (i, plain_block(j)))
    return pl.pallas_call(
        body,
        name=name,
        grid=(s // ts, n // width),
        in_specs=[tile, table, table],
        out_specs=tile,
        out_shape=jax.ShapeDtypeStruct((s, n), out_dtype),
        compiler_params=_params(("parallel", "parallel")),
    )(h, cos2, sin2)


SB_TQ = 512
SB_BLK = 128
SB_WALK = 2
SB_CLOSED = 110.0


def _sb_walk(n_trips, carry, key_blocks):
    def still_open(carry):
        lowest = functools.reduce(jnp.minimum, [c[0] for c in carry])
        return jnp.min(lowest) < SB_CLOSED

    def cond(state):
        t, go, _ = state
        return jnp.logical_and(t < n_trips, go)

    def body(state):
        t, _, carry = state
        carry = key_blocks(t, carry)
        return t + 1, still_open(carry), carry

    return lax.while_loop(cond, body, (jnp.int32(0), still_open(carry), carry))[2]


def _tri2(strict):
    row = lax.broadcasted_iota(jnp.int32, (2 * SB_BLK, SB_BLK), 0) % SB_BLK
    col = lax.broadcasted_iota(jnp.int32, (2 * SB_BLK, SB_BLK), 1)
    return (row > col if strict else row >= col).astype(MXU_DTYPE)


def _cumsum_dot(x, tri2):
    hi = x.astype(MXU_DTYPE)
    lo = x - hi.astype(F32)
    return _dot(jnp.concatenate([hi, lo.astype(MXU_DTYPE)], axis=1), tri2, NN)


def _sb_logits(z, valid):
    l1p = jnp.log(1.0 + jnp.exp(-jnp.abs(z)))
    sp = jnp.maximum(z, 0.0) + l1p
    if valid is not None:
        sp = jnp.where(valid, sp, 0.0)
    return sp, jnp.minimum(z, 0.0) - l1p


def _sb_weights(ls, after, valid):
    a = jnp.exp(ls - after)
    return a if valid is None else jnp.where(valid, a, 0.0)


def _sb_setup(q_ref, k_ref, v_ref, n_sub, scale):
    qs = [q_ref[0, u * SB_BLK:(u + 1) * SB_BLK, :] * scale for u in range(n_sub)]
    row = lax.broadcasted_iota(jnp.int32, (SB_BLK, SB_BLK), 0)
    col = lax.broadcasted_iota(jnp.int32, (SB_BLK, SB_BLK), 1)

    def load(jj):
        start = pl.multiple_of(jj * SB_BLK, SB_BLK)
        return start, k_ref[0, pl.ds(start, SB_BLK), :], v_ref[0, pl.ds(start, SB_BLK), :]

    return qs, col < row, load


def _sb_fwd(q, k, v, name):
    nh, s, dh = q.shape
    tq = min(SB_TQ, s)
    n_sub = tq // SB_BLK
    scale = dh ** -0.5

    def body(q_ref, k_ref, v_ref, o_ref, ob_ref):
        base = pl.program_id(1) * n_sub
        qs, diag_valid, load = _sb_setup(q_ref, k_ref, v_ref, n_sub, scale)
        tri_after = _tri2(True)

        def key_blocks(first, carry, diagonal):
            blocks = list(range((n_sub if diagonal else SB_WALK) - 1, -1, -1))
            kv = {d: load(first + d) for d in blocks}
            tiles = [(d, u) for d in blocks for u in range(d if diagonal else 0, n_sub)]
            valid = {t: diag_valid if diagonal and t[0] == t[1] else None for t in tiles}
            z = {t: _dot(qs[t[1]], kv[t[0]][1], NT) for t in tiles}
            sp_ls = {t: _sb_logits(z[t], valid[t]) for t in tiles}
            inside = {t: _cumsum_dot(sp_ls[t][0], tri_after) for t in tiles}
            carry = list(carry)
            for t in tiles:
                after_c, acc = carry[t[1]]
                sp, ls = sp_ls[t]
                a = _sb_weights(ls, after_c + inside[t], valid[t])
                carry[t[1]] = (after_c + jnp.sum(sp, axis=1, keepdims=True), acc + _dot(a, kv[t[0]][2], NN))
            return tuple(carry)

        carry = tuple((jnp.zeros((SB_BLK, 1), F32), jnp.zeros((SB_BLK, dh), F32)) for _ in range(n_sub))
        carry = key_blocks(base, carry, True)
        carry = _sb_walk(base // SB_WALK, carry, lambda t, c: key_blocks(base - SB_WALK * (t + 1), c, False))
        for u in range(n_sub):
            rows = slice(u * SB_BLK, (u + 1) * SB_BLK)
            o_ref[0, rows, :] = carry[u][1]
            ob_ref[0, rows, :] = carry[u][1].astype(ob_ref.dtype)

    blk = pl.BlockSpec((1, tq, dh), lambda h, i: (h, i, 0))
    full = pl.BlockSpec((1, s, dh), lambda h, i: (h, 0, 0))
    return pl.pallas_call(
        body,
        name=name,
        grid=(nh, s // tq),
        in_specs=[blk, full, full],
        out_specs=[blk, blk],
        out_shape=[jax.ShapeDtypeStruct((nh, s, dh), F32), jax.ShapeDtypeStruct((nh, s, dh), MXU_DTYPE)],
        compiler_params=_params(("parallel", "parallel")),
    )(q, k, v)


def _sb_bwd(q, k, v, o, do, name):
    nh, s, dh = q.shape
    tq = min(SB_TQ, s)
    n_sub = tq // SB_BLK
    scale = dh ** -0.5

    def body(q_ref, k_ref, v_ref, o_ref, do_ref, dq_ref, dk_ref, dv_ref):
        i = pl.program_id(1)
        base = i * n_sub

        @pl.when(i == 0)
        def _():
            dk_ref[...] = jnp.zeros_like(dk_ref)
            dv_ref[...] = jnp.zeros_like(dv_ref)

        qs, diag_valid, load = _sb_setup(q_ref, k_ref, v_ref, n_sub, scale)
        dob = [do_ref[0, u * SB_BLK:(u + 1) * SB_BLK, :] for u in range(n_sub)]
        total = [jnp.sum(dob[u].astype(F32) * o_ref[0, u * SB_BLK:(u + 1) * SB_BLK, :], axis=1, keepdims=True) for u in range(n_sub)]
        tri_after = _tri2(True)
        tri_from = _tri2(False)

        def key_blocks(first, carry, diagonal):
            blocks = list(range((n_sub if diagonal else SB_WALK) - 1, -1, -1))
            kv = {d: load(first + d) for d in blocks}
            tiles = [(d, u) for d in blocks for u in range(d if diagonal else 0, n_sub)]
            valid = {t: diag_valid if diagonal and t[0] == t[1] else None for t in tiles}
            z = {t: _dot(qs[t[1]], kv[t[0]][1], NT) for t in tiles}
            da = {t: _dot(dob[t[1]], kv[t[0]][2], NT) for t in tiles}
            sp_ls = {t: _sb_logits(z[t], valid[t]) for t in tiles}
            inside = {t: _cumsum_dot(sp_ls[t][0], tri_after) for t in tiles}
            after_run = [c[0] for c in carry]
            ab, dl = {}, {}
            for t in tiles:
                sp, ls = sp_ls[t]
                ab[t] = _sb_weights(ls, after_run[t[1]] + inside[t], valid[t]).astype(MXU_DTYPE)
                dl[t] = ab[t].astype(F32) * da[t]
                after_run[t[1]] = after_run[t[1]] + jnp.sum(sp, axis=1, keepdims=True)
            from_in = {t: _cumsum_dot(dl[t], tri_from) for t in tiles}
            from_run = [c[1] for c in carry]
            dq = [c[2] for c in carry]
            for d in blocks:
                start, kb, _ = kv[d]
                dk = dv = None
                for u in range(d if diagonal else 0, n_sub):
                    t = (d, u)
                    dz = dl[t] - jnp.exp(sp_ls[t][1]) * (dl[t] + total[u] - (from_run[u] + from_in[t]))
                    if valid[t] is not None:
                        dz = jnp.where(valid[t], dz, 0.0)
                    dzb = dz.astype(MXU_DTYPE)
                    from_run[u] = from_run[u] + jnp.sum(dl[t], axis=1, keepdims=True)
                    dq[u] = dq[u] + _dot(dzb, kb, NN)
                    dk_u, dv_u = _dot(dzb, qs[u], TN), _dot(ab[t], dob[u], TN)
                    dk = dk_u if dk is None else dk + dk_u
                    dv = dv_u if dv is None else dv + dv_u
                dk_ref[0, pl.ds(start, SB_BLK), :] += dk
                dv_ref[0, pl.ds(start, SB_BLK), :] += dv
            return tuple(zip(after_run, from_run, dq))

        carry = tuple((jnp.zeros((SB_BLK, 1), F32), jnp.zeros((SB_BLK, 1), F32), jnp.zeros((SB_BLK, dh), F32)) for _ in range(n_sub))
        carry = key_blocks(base, carry, True)
        carry = _sb_walk(base // SB_WALK, carry, lambda t, c: key_blocks(base - SB_WALK * (t + 1), c, False))
        for u in range(n_sub):
            dq_ref[0, u * SB_BLK:(u + 1) * SB_BLK, :] = carry[u][2] * scale

    blk = pl.BlockSpec((1, tq, dh), lambda h, i: (h, i, 0))
    full = pl.BlockSpec((1, s, dh), lambda h, i: (h, 0, 0))
    shape = jax.ShapeDtypeStruct((nh, s, dh), F32)
    return pl.pallas_call(
        body,
        name=name,
        grid=(nh, s // tq),
        in_specs=[blk, full, full, blk, blk],
        out_specs=[blk, full, full],
        out_shape=[shape, shape, shape],
        compiler_params=_params(("parallel", "arbitrary")),
    )(q, k, v, o, do)


BAND_TQ = 1024


def _band_scores(q_ref, k_ref, i, sub, tq, length, max_dist, scale):
    t0 = i * tq + sub * ATT_BLK
    ks = pl.multiple_of(jnp.minimum(jnp.maximum(t0 - ATT_BLK, 0), length - 2 * ATT_BLK), ATT_BLK)
    qs = q_ref[0, sub * ATT_BLK:(sub + 1) * ATT_BLK, :] * scale
    kw = k_ref[0, pl.ds(ks, 2 * ATT_BLK), :]
    sc = _dot(qs, kw, NT)
    diff = (t0 + lax.broadcasted_iota(jnp.int32, sc.shape, 0)) - (ks + lax.broadcasted_iota(jnp.int32, sc.shape, 1))
    valid = (diff >= 0) & (diff <= max_dist)
    return ks, qs, kw, jnp.where(valid, sc, NEG_INF)


def _band_fwd(q, k, v, sinks, max_dist, name):
    bq, length, dh = q.shape
    group = bq // k.shape[0]
    tq = min(BAND_TQ, length)
    scale = dh ** -0.5
    n_sink = 0 if sinks is None else sinks.shape[0]

    def body(*refs):
        if n_sink:
            sink_ref, q_ref, k_ref, v_ref, o_ref, lse_ref = refs
            sink = sink_ref[pl.program_id(0) % n_sink]
        else:
            q_ref, k_ref, v_ref, o_ref, lse_ref = refs
        i = pl.program_id(1)
        scores = [_band_scores(q_ref, k_ref, i, sub, tq, length, max_dist, scale) for sub in range(tq // ATT_BLK)]
        for sub, (ks, _, _, sc) in enumerate(scores):
            m = jnp.max(sc, axis=1, keepdims=True)
            if n_sink:
                m = jnp.maximum(m, sink)
            e = jnp.exp(sc - m)
            den = jnp.sum(e, axis=1, keepdims=True)
            if n_sink:
                den = den + jnp.exp(sink - m)
            rows = slice(sub * ATT_BLK, (sub + 1) * ATT_BLK)
            o_ref[0, rows, :] = _dot(e / den, v_ref[0, pl.ds(ks, 2 * ATT_BLK), :], NN)
            lse_ref[0, rows, :] = m + jnp.log(den)

    qblk = pl.BlockSpec((1, tq, dh), lambda b, i: (b, i, 0))
    kfull = pl.BlockSpec((1, length, dh), lambda b, i: (b // group, 0, 0))
    in_specs = [qblk, kfull, kfull]
    args = [q, k, v]
    if n_sink:
        in_specs = [pl.BlockSpec(memory_space=pltpu.SMEM)] + in_specs
        args = [sinks] + args
    return pl.pallas_call(
        body,
        name=name,
        grid=(bq, length // tq),
        in_specs=in_specs,
        out_specs=[qblk, pl.BlockSpec((1, tq, 1), lambda b, i: (b, i, 0))],
        out_shape=[jax.ShapeDtypeStruct((bq, length, dh), F32), jax.ShapeDtypeStruct((bq, length, 1), F32)],
        compiler_params=_params(("parallel", "parallel")),
    )(*args)


def _band_bwd(q, k, v, do, lse, delta, sinks, max_dist, name):
    bq, length, dh = q.shape
    group = bq // k.shape[0]
    tq = min(BAND_TQ, length)
    scale = dh ** -0.5
    n_sink = 0 if sinks is None else sinks.shape[0]

    def body(*refs):
        if n_sink:
            sink_ref, q_ref, k_ref, v_ref, do_ref, lse_ref, dl_ref, dq_ref, dk_ref, dv_ref, ds_ref = refs
            sink = sink_ref[pl.program_id(0) % n_sink]
        else:
            q_ref, k_ref, v_ref, do_ref, lse_ref, dl_ref, dq_ref, dk_ref, dv_ref, ds_ref = refs
        i = pl.program_id(1)

        @pl.when((i == 0) & (pl.program_id(0) % group == 0))
        def _():
            dk_ref[...] = jnp.zeros_like(dk_ref)
            dv_ref[...] = jnp.zeros_like(dv_ref)

        @pl.when(i == 0)
        def _():
            ds_ref[...] = jnp.zeros_like(ds_ref)

        scores = [_band_scores(q_ref, k_ref, i, sub, tq, length, max_dist, scale) for sub in range(tq // ATT_BLK)]
        dps = [_dot(do_ref[0, sub * ATT_BLK:(sub + 1) * ATT_BLK, :], v_ref[0, pl.ds(ks, 2 * ATT_BLK), :], NT) for sub, (ks, _, _, _) in enumerate(scores)]
        for sub, (ks, qs, kw, sc) in enumerate(scores):
            rows = slice(sub * ATT_BLK, (sub + 1) * ATT_BLK)
            lse = lse_ref[0, rows, :]
            delta_r = dl_ref[0, rows, :]
            dob = do_ref[0, rows, :]
            p = jnp.exp(sc - lse)
            dsb = (p * (dps[sub] - delta_r)).astype(MXU_DTYPE)
            dq_ref[0, rows, :] = _dot(dsb, kw, NN) * scale
            dk_ref[0, pl.ds(ks, 2 * ATT_BLK), :] += _dot(dsb, qs, TN)
            dv_ref[0, pl.ds(ks, 2 * ATT_BLK), :] += _dot(p, dob, TN)
            if n_sink:
                ds_ref[...] += jnp.sum(-jnp.exp(sink - lse) * delta_r, axis=0, keepdims=True)

    qblk = pl.BlockSpec((1, tq, dh), lambda b, i: (b, i, 0))
    qcol = pl.BlockSpec((1, tq, 1), lambda b, i: (b, i, 0))
    kfull = pl.BlockSpec((1, length, dh), lambda b, i: (b // group, 0, 0))
    in_specs = [qblk, kfull, kfull, qblk, qcol, qcol]
    args = [q, k, v, do, lse, delta]
    if n_sink:
        in_specs = [pl.BlockSpec(memory_space=pltpu.SMEM)] + in_specs
        args = [sinks] + args
    kshape = jax.ShapeDtypeStruct((k.shape[0], length, dh), F32)
    return pl.pallas_call(
        body,
        name=name,
        grid=(bq, length // tq),
        in_specs=in_specs,
        out_specs=[qblk, kfull, kfull, pl.BlockSpec((1, 8, 128), lambda b, i: (b, 0, 0))],
        out_shape=[jax.ShapeDtypeStruct((bq, length, dh), F32), kshape, kshape, jax.ShapeDtypeStruct((bq, 8, 128), F32)],
        compiler_params=_params(("arbitrary", "arbitrary")),
    )(*args)


def _merge_weights(lse_refs):
    lses = [r[0] for r in lse_refs]
    m = functools.reduce(jnp.maximum, lses)
    es = [jnp.exp(l - m) for l in lses]
    den = functools.reduce(lambda a, b: a + b, es)
    return [e / den for e in es]


def _merge_fwd(outs, lses, name):
    n = len(outs)
    nh, s, dh = outs[0].shape
    ts = _tile(s, 1024, 8)

    def body(*refs):
        ws = _merge_weights(refs[n:2 * n])
        o = functools.reduce(lambda a, b: a + b, [w * r[0] for w, r in zip(ws, refs[:n])])
        refs[2 * n][0] = o
        refs[2 * n + 1][0] = o.astype(MXU_DTYPE)

    blk = pl.BlockSpec((1, ts, dh), lambda h, i: (h, i, 0))
    col = pl.BlockSpec((1, ts, 1), lambda h, i: (h, i, 0))
    return pl.pallas_call(
        body,
        name=name,
        grid=(nh, s // ts),
        in_specs=[blk] * n + [col] * n,
        out_specs=[blk, blk],
        out_shape=[jax.ShapeDtypeStruct((nh, s, dh), F32), jax.ShapeDtypeStruct((nh, s, dh), MXU_DTYPE)],
        compiler_params=_params(("parallel", "parallel")),
    )(*outs, *lses)


def _merge_bwd(do, o, lses, name):
    n = len(lses)
    nh, s, dh = o.shape
    ts = _tile(s, 1024, 8)

    def body(*refs):
        do_ref, o_ref = refs[:2]
        ws = _merge_weights(refs[2:2 + n])
        dof = do_ref[0].astype(F32)
        base = jnp.sum(dof * o_ref[0], axis=1, keepdims=True)
        for g in range(n):
            refs[2 + n + g][0] = (ws[g] * dof).astype(MXU_DTYPE)
            refs[2 + 2 * n + g][0] = ws[g] * base

    blk = pl.BlockSpec((1, ts, dh), lambda h, i: (h, i, 0))
    col = pl.BlockSpec((1, ts, 1), lambda h, i: (h, i, 0))
    res = pl.pallas_call(
        body,
        name=name,
        grid=(nh, s // ts),
        in_specs=[blk, blk] + [col] * n,
        out_specs=[blk] * n + [col] * n,
        out_shape=[jax.ShapeDtypeStruct((nh, s, dh), MXU_DTYPE)] * n + [jax.ShapeDtypeStruct((nh, s, 1), F32)] * n,
        compiler_params=_params(("parallel", "parallel")),
    )(do, o, *lses)
    return res[:n], res[n:]


def _to_heads(a, dil=1):
    s, n = a.shape
    nh = n // HEAD_DIM
    a = a.reshape(s // dil, dil, nh, HEAD_DIM).transpose(1, 2, 0, 3)
    return a.reshape(dil * nh, s // dil, HEAD_DIM)


def _from_heads(a, dil=1):
    b, ls, c = a.shape
    nh = b // dil
    return a.reshape(dil, nh, ls, c).transpose(2, 0, 1, 3).reshape(ls * dil, nh * c)


def _restride(a, dil):
    nh, s, c = a.shape
    return a.reshape(nh, s // dil, dil, c).transpose(2, 0, 1, 3).reshape(dil * nh, s // dil, c)


def _unstride(a, dil):
    b, ls, c = a.shape
    nh = b // dil
    return a.reshape(dil, nh, ls, c).transpose(1, 2, 0, 3).reshape(nh, ls * dil, c)


ANY = pl.BlockSpec(memory_space=pl.ANY)


def _position():
    x, y, c = lax.axis_index("x"), lax.axis_index("y"), lax.axis_index("c")
    return x, y, c, [(1 - x, y), (x, 1 - y), (1 - x, 1 - y)]


def _remote(src, dst, send_sems, recv_sems, k, to):
    return pltpu.make_async_remote_copy(src_ref=src, dst_ref=dst, send_sem=send_sems.at[k], recv_sem=recv_sems.at[k], device_id=to, device_id_type=MESH)


def _gather_chips(arrs, name):
    n = len(arrs)

    def body(*refs):
        f_refs, g_refs = refs[:n], refs[n:2 * n]
        send_sems, recv_sems = refs[2 * n:]
        x, y, c, chips = _position()
        me, sibling = (x, y, c), (x, y, 1 - c)

        def half(j, ref, sel):
            rows = f_refs[j].shape[1] // 2
            return ref.at[:, pl.ds(sel * rows, rows), :]

        def slot(j, chip, sel):
            return half(j, g_refs[j].at[2 * chip[0] + chip[1]], sel)

        first = [_remote(half(j, f_refs[j], c), slot(j, (x, y), c), send_sems, recv_sems, 6 * j + k, (*chip, c)) for j in range(n) for k, chip in enumerate(chips)]
        for cp in first:
            cp.start()
        passed = []
        for j in range(n):
            for k, chip in enumerate(chips):
                _remote(slot(j, chip, c), slot(j, chip, c), send_sems, recv_sems, 6 * j + k, me).wait_recv()
                passed.append(_remote(slot(j, chip, c), slot(j, chip, c), send_sems, recv_sems, 6 * j + 3 + k, sibling))
                passed[-1].start()
        for j in range(n):
            for k, chip in enumerate(chips):
                _remote(slot(j, chip, 1 - c), slot(j, chip, 1 - c), send_sems, recv_sems, 6 * j + 3 + k, me).wait_recv()
        for cp in first + passed:
            cp.wait_send()

    return pl.pallas_call(
        body,
        name=name,
        in_specs=[ANY] * n,
        out_specs=[ANY] * n,
        out_shape=[jax.ShapeDtypeStruct((N_CHIPS,) + a.shape, a.dtype) for a in arrs],
        scratch_shapes=[pltpu.SemaphoreType.DMA((6 * n,)), pltpu.SemaphoreType.DMA((6 * n,))],
    )(*arrs)


def _swap_sibling(arrs, name):
    n = len(arrs)

    def body(*refs):
        x, y, c, _ = _position()
        send_sems, recv_sems = refs[2 * n:]
        copies = [_remote(refs[j], refs[n + j], send_sems, recv_sems, j, (x, y, 1 - c)) for j in range(n)]
        for cp in copies:
            cp.start()
        for cp in copies:
            cp.wait()

    return pl.pallas_call(
        body,
        name=name,
        in_specs=[ANY] * n,
        out_specs=[ANY] * n,
        out_shape=[jax.ShapeDtypeStruct(a.shape, a.dtype) for a in arrs],
        scratch_shapes=[pltpu.SemaphoreType.DMA((n,)), pltpu.SemaphoreType.DMA((n,))],
    )(*arrs)


def _exchange_chips(parts, name):
    n = len(parts)

    def body(*refs):
        p_refs, q_refs = refs[:n], refs[n:2 * n]
        send_sems, recv_sems = refs[2 * n:]
        x, y, c, chips = _position()
        my_slot = 2 * x + y
        sends = [_remote(p_refs[j].at[2 * px + py], q_refs[j].at[my_slot], send_sems, recv_sems, 3 * j + k, (px, py, c)) for j in range(n) for k, (px, py) in enumerate(chips)]
        for cp in sends:
            cp.start()
        for j in range(n):
            for k, (px, py) in enumerate(chips):
                landed = q_refs[j].at[2 * px + py]
                _remote(landed, landed, send_sems, recv_sems, 3 * j + k, (x, y, c)).wait_recv()
        for cp in sends:
            cp.wait_send()

    return pl.pallas_call(
        body,
        name=name,
        in_specs=[ANY] * n,
        out_specs=[ANY] * n,
        out_shape=[jax.ShapeDtypeStruct(a.shape, a.dtype) for a in parts],
        scratch_shapes=[pltpu.SemaphoreType.DMA((3 * n,)), pltpu.SemaphoreType.DMA((3 * n,))],
    )(*parts)


def _share_sibling(halves, name):
    n = len(halves)

    def body(*refs):
        h_refs, o_refs = refs[:n], refs[n:2 * n]
        send_sems, recv_sems = refs[2 * n:]
        x, y, c, _ = _position()

        def rows(j, sel):
            r = h_refs[j].shape[0]
            return o_refs[j].at[pl.ds(sel * r, r), :]

        sends = [_remote(h_refs[j], rows(j, c), send_sems, recv_sems, j, (x, y, 1 - c)) for j in range(n)]
        for cp in sends:
            cp.start()
        for j in range(n):
            _remote(rows(j, 1 - c), rows(j, 1 - c), send_sems, recv_sems, j, (x, y, c)).wait_recv()
        for cp in sends:
            cp.wait_send()

    return pl.pallas_call(
        body,
        name=name,
        in_specs=[ANY] * n,
        out_specs=[ANY] * n,
        out_shape=[jax.ShapeDtypeStruct((2 * a.shape[0],) + a.shape[1:], a.dtype) for a in halves],
        scratch_shapes=[pltpu.SemaphoreType.DMA((n,)), pltpu.SemaphoreType.DMA((n,))],
    )(*halves)


def _sum_rows(r, c, n_in):
    return _tile(r, max(16, (1 << 20) // (c * (n_in + 1))), 16)


def _add2(a, b, out_dtype, name):
    n, r, c = a.shape
    tr = _sum_rows(r, c, 2)

    def body(a_ref, b_ref, o_ref):
        o_ref[...] = (a_ref[...].astype(F32) + b_ref[...].astype(F32)).astype(o_ref.dtype)

    blk = pl.BlockSpec((1, tr, c), lambda s, i: (s, i, 0))
    return pl.pallas_call(
        body,
        name=name,
        grid=(n, r // tr),
        in_specs=[blk, blk],
        out_specs=blk,
        out_shape=jax.ShapeDtypeStruct(a.shape, out_dtype),
        compiler_params=_params(("parallel", "parallel")),
    )(a, b)


def _sum_slots(q, name):
    n, r, c = q.shape
    tr = _sum_rows(r, c, n)

    def body(q_ref, o_ref):
        acc = q_ref[0].astype(F32)
        for s in range(1, n):
            acc = acc + q_ref[s].astype(F32)
        o_ref[...] = acc

    return pl.pallas_call(
        body,
        name=name,
        grid=(r // tr,),
        in_specs=[pl.BlockSpec((n, tr, c), lambda i: (0, i, 0))],
        out_specs=pl.BlockSpec((tr, c), lambda i: (i, 0)),
        out_shape=jax.ShapeDtypeStruct((r, c), F32),
        compiler_params=_params(("parallel",)),
    )(q)


def _adamw(w, g, m, v, name):
    r, c = w.shape
    tr = _tile(r, 256, 8)
    c1 = 1.0 - ADAM_B1 ** ADAM_STEP
    c2 = 1.0 - ADAM_B2 ** ADAM_STEP

    def body(w_ref, g_ref, m_ref, v_ref, d_ref, nm_ref, nv_ref):
        g = g_ref[...]
        nm = ADAM_B1 * m_ref[...] + (1.0 - ADAM_B1) * g
        nv = ADAM_B2 * v_ref[...] + (1.0 - ADAM_B2) * (g * g)
        d_ref[...] = -ADAM_LR * ((nm / c1) / (jnp.sqrt(nv / c2) + ADAM_EPS) + ADAM_WD * w_ref[...])
        nm_ref[...] = nm
        nv_ref[...] = nv

    blk = pl.BlockSpec((tr, c), lambda i: (i, 0))
    shape = jax.ShapeDtypeStruct((r, c), F32)
    return pl.pallas_call(
        body,
        name=name,
        grid=(r // tr,),
        in_specs=[blk] * 4,
        out_specs=[blk] * 3,
        out_shape=[shape] * 3,
        compiler_params=_params(("parallel",)),
    )(w, g, m, v)


WEIGHTS = ("ffn1_w_in", "ffn1_w_out", "ffn2_w_in", "ffn2_w_out", "ln_g", "ln_b", "sb_w_in", "sb_w_out", "swa_w_in", "swa_sinks", "swa_w_out", "dil_w_in", "dil_w_out", "ple_w_proj", "ple_w_gate")
SHARD_AXIS = {"ffn1_w_in": 2, "ffn1_w_out": 1, "ffn2_w_in": 2, "ffn2_w_out": 1, "ln_g": 2, "ln_b": 2, "sb_w_in": 2, "sb_w_out": 1, "swa_w_in": 2, "swa_sinks": None, "swa_w_out": 1, "dil_w_in": 2, "dil_w_out": 1, "ple_w_proj": 2, "ple_w_gate": 1}
SMALL = ("ln_g", "ln_b", "swa_sinks")
SMALL_COLS = 128
SMALL_UNIT = 16 * SMALL_COLS


def _pack_small(pieces, lead):
    flat = jnp.concatenate([a.reshape(lead + (-1,)) for a in pieces], axis=-1)
    n = flat.shape[-1]
    flat = jnp.pad(flat, [(0, 0)] * len(lead) + [(0, -n % SMALL_UNIT)])
    return flat.reshape(lead + (-1, SMALL_COLS))


def _unpack_small(buf, shapes, lead):
    flat = buf.reshape(lead + (-1,))
    out, off = [], 0
    for shp in shapes:
        n = math.prod(shp)
        out.append(flat[..., off:off + n].reshape(lead + tuple(shp)))
        off += n
    return out


def _reduce_group(grads, c, wire, tag):
    keep = [lax.dynamic_slice_in_dim(g, c * (g.shape[1] // 2), g.shape[1] // 2, axis=1) for g in grads]
    give = [lax.dynamic_slice_in_dim(g, (1 - c) * (g.shape[1] // 2), g.shape[1] // 2, axis=1).astype(wire) for g in grads]
    got = _swap_sibling(give, f"{tag}_pair")
    part = [_add2(k, g, wire, f"{tag}_pair_sum{j}") for j, (k, g) in enumerate(zip(keep, got))]
    landed = [_fill_slot(q, lax.dynamic_slice_in_dim(p, _my_slot(), 1, axis=0)) for q, p in zip(_exchange_chips(part, f"{tag}_chips"), part)]
    halves = [_sum_slots(q, f"{tag}_chips_sum{j}") for j, q in enumerate(landed)]
    shared = _share_sibling(halves, f"{tag}_share")
    return [lax.dynamic_update_slice_in_dim(s, h, c * h.shape[0], axis=0) for s, h in zip(shared, halves)]


def _my_slot():
    return 2 * lax.axis_index("x") + lax.axis_index("y")


def _fill_slot(slots, mine):
    return lax.dynamic_update_slice_in_dim(slots, mine, _my_slot(), axis=0)


def _full_from_shards(g, axis):
    g = jnp.moveaxis(g, 0, axis)
    return g.reshape(g.shape[:axis] + (g.shape[axis] * g.shape[axis + 1],) + g.shape[axis + 2:])


def _shards_from_full(a, axis):
    a = a.reshape(a.shape[:axis] + (N_CHIPS, a.shape[axis] // N_CHIPS) + a.shape[axis + 1:])
    return jnp.moveaxis(a, axis, 0)


def _ffn_forward(x, xb, w_in, w_out, g, b, alpha, tag):
    gate, up, act = _ffn_in(xb, w_in, f"{tag}_in")
    y = _mm(act, w_out, "nn", F32, f"{tag}_out")
    out, outb, xhat, rstd = _ln_fwd(x, y, g, b, alpha, 0.5, f"{tag}_ln")
    return (out, outb), dict(xb=xb, gate=gate, up=up, act=act, xhat=xhat, rstd=rstd)


def _ffn_backward(dout, saved, w_in, w_out, g, tag):
    dr, dyb, dg, db = _ln_bwd(*dout, saved["xhat"], saved["rstd"], g, 0.5, f"{tag}_ln_bwd")
    dh = _ffn_dact(dyb, w_out, saved["gate"], saved["up"], f"{tag}_dact")
    dw_out = _mm(saved["act"], dyb, "tn", F32, f"{tag}_dw_out")
    dw_in = _mm(saved["xb"], dh, "tn", F32, f"{tag}_dw_in", split_b=True)
    dxb = _mm(dh, w_in, "nt", F32, f"{tag}_dx", split_a=True)
    return dr, dxb, dw_in, dw_out, dg, db


def _sb_forward(xb, w_in, w_out, tag):
    nw = w_out.shape[0]
    h = _mm(xb, w_in, "nn", MXU_DTYPE, f"{tag}_qkv")
    q, k, v = (_to_heads(h[:, j * nw:(j + 1) * nw]) for j in range(3))
    o, ob = _sb_fwd(q, k, v, f"{tag}_att")
    ob = _from_heads(ob)
    return _mm(ob, w_out, "nn", F32, f"{tag}_proj"), dict(q=q, k=k, v=v, o=o, ob=ob)


def _sb_backward(dmix, saved, xb, w_in, w_out, tag):
    dw_out = _mm(saved["ob"], dmix, "tn", F32, f"{tag}_dw_out")
    do = _to_heads(_mm(dmix, w_out, "nt", MXU_DTYPE, f"{tag}_do"))
    dq, dk, dv = _sb_bwd(saved["q"], saved["k"], saved["v"], saved["o"], do, f"{tag}_att_bwd")
    dh = jnp.concatenate([_from_heads(t) for t in (dq, dk, dv)], axis=1).astype(MXU_DTYPE)
    dw_in = _mm(xb, dh, "tn", F32, f"{tag}_dw_in")
    return _mm(dh, w_in, "nt", F32, f"{tag}_dx"), dw_in, dw_out


def _swa_forward(xb, w_in, sinks, w_out, tables, tag):
    nq = w_out.shape[0]
    nkv = (w_in.shape[1] - nq) // 2
    h = _mm(xb, w_in, "nn", F32, f"{tag}_qkv")
    n_rot = (nq + nkv) // 128
    hb = _rope(h, *tables, lambda j: j // n_rot, MXU_DTYPE, f"{tag}_rope")
    q, k, v = _to_heads(hb[:, :nq]), _to_heads(hb[:, nq:nq + nkv]), _to_heads(hb[:, nq + nkv:])
    o, lse = _band_fwd(q, k, v, sinks, SWA_WINDOW - 1, f"{tag}_att")
    o, ob = _merge_fwd([o], [lse], f"{tag}_cast")
    ob = _from_heads(ob)
    return _mm(ob, w_out, "nn", F32, f"{tag}_proj"), dict(q=q, k=k, v=v, o=o, lse=lse, ob=ob, n_rot=n_rot)


def _swa_backward(dmix, saved, xb, w_in, sinks, w_out, tables, tag):
    dw_out = _mm(saved["ob"], dmix, "tn", F32, f"{tag}_dw_out")
    do = _to_heads(_mm(dmix, w_out, "nt", MXU_DTYPE, f"{tag}_do"))
    (dog,), (delta,) = _merge_bwd(do, saved["o"], [saved["lse"]], f"{tag}_delta")
    dq, dk, dv, dsink = _band_bwd(saved["q"], saved["k"], saved["v"], dog, saved["lse"], delta, sinks, SWA_WINDOW - 1, f"{tag}_att_bwd")
    dh = jnp.concatenate([_from_heads(t) for t in (dq, dk, dv)], axis=1)
    n_rot = saved["n_rot"]
    dhb = _rope(dh, tables[0], -tables[1], lambda j: j // n_rot, MXU_DTYPE, f"{tag}_rope_bwd")
    dw_in = _mm(xb, dhb, "tn", F32, f"{tag}_dw_in")
    return _mm(dhb, w_in, "nt", F32, f"{tag}_dx"), dw_in, dw_out, dsink[:, 0, 0]


def _dil_forward(xb, w_in, w_out, tables, tag):
    nw = w_out.shape[0]
    h = _mm(xb, w_in, "nn", F32, f"{tag}_qkv")
    hb = _rope(h, *tables, lambda j: (j % 3) // 2, MXU_DTYPE, f"{tag}_rope", width=nw)
    qkv, outs, lses = [], [], []
    for gi, (win, dil) in enumerate(DIL_GROUPS):
        base = gi * 3 * nw
        q, k, v = (_to_heads(hb[:, base + j * nw:base + (j + 1) * nw], dil) for j in range(3))
        o, lse = _band_fwd(q, k, v, None, win // dil, f"{tag}_att{gi}")
        qkv.append((q, k, v))
        outs.append(_unstride(o, dil))
        lses.append(_unstride(lse, dil))
    o, ob = _merge_fwd(outs, lses, f"{tag}_merge")
    ob = _from_heads(ob)
    return _mm(ob, w_out, "nn", F32, f"{tag}_proj"), dict(qkv=qkv, o=o, lses=lses, ob=ob)


def _dil_backward(dmix, saved, xb, w_in, w_out, tables, tag):
    dw_out = _mm(saved["ob"], dmix, "tn", F32, f"{tag}_dw_out")
    do = _to_heads(_mm(dmix, w_out, "nt", MXU_DTYPE, f"{tag}_do"))
    dogs, deltas = _merge_bwd(do, saved["o"], saved["lses"], f"{tag}_merge_bwd")
    parts = []
    for gi, (win, dil) in enumerate(DIL_GROUPS):
        q, k, v = saved["qkv"][gi]
        dq, dk, dv, _ = _band_bwd(q, k, v, _restride(dogs[gi], dil), _restride(saved["lses"][gi], dil), _restride(deltas[gi], dil), None, win // dil, f"{tag}_att{gi}_bwd")
        parts += [_from_heads(t, dil) for t in (dq, dk, dv)]
    dh = jnp.concatenate(parts, axis=1)
    dhb = _rope(dh, tables[0], -tables[1], lambda j: (j % 3) // 2, MXU_DTYPE, f"{tag}_rope_bwd", width=w_out.shape[0])
    dw_in = _mm(xb, dhb, "tn", F32, f"{tag}_dw_in")
    return _mm(dhb, w_in, "nt", F32, f"{tag}_dx"), dw_in, dw_out


def kernel(x, p, ffn1_w_in, ffn1_w_out, ffn2_w_in, ffn2_w_out, ln_g, ln_b, sb_w_in, sb_w_out, swa_w_in, swa_sinks, swa_w_out, dil_w_in, dil_w_out, ple_w_proj, ple_w_gate, loss_target, m_ffn1_w_in, m_ffn1_w_out, m_ffn2_w_in, m_ffn2_w_out, m_ln_g, m_ln_b, m_sb_w_in, m_sb_w_out, m_swa_w_in, m_swa_sinks, m_swa_w_out, m_dil_w_in, m_dil_w_out, m_ple_w_proj, m_ple_w_gate, v_ffn1_w_in, v_ffn1_w_out, v_ffn2_w_in, v_ffn2_w_out, v_ln_g, v_ln_b, v_sb_w_in, v_sb_w_out, v_swa_w_in, v_swa_sinks, v_swa_w_out, v_dil_w_in, v_dil_w_out, v_ple_w_proj, v_ple_w_gate):
    shard = dict(ffn1_w_in=ffn1_w_in, ffn1_w_out=ffn1_w_out, ffn2_w_in=ffn2_w_in, ffn2_w_out=ffn2_w_out, ln_g=ln_g, ln_b=ln_b, sb_w_in=sb_w_in, sb_w_out=sb_w_out, swa_w_in=swa_w_in, swa_sinks=swa_sinks, swa_w_out=swa_w_out, dil_w_in=dil_w_in, dil_w_out=dil_w_out, ple_w_proj=ple_w_proj, ple_w_gate=ple_w_gate)
    mom_m = dict(ffn1_w_in=m_ffn1_w_in, ffn1_w_out=m_ffn1_w_out, ffn2_w_in=m_ffn2_w_in, ffn2_w_out=m_ffn2_w_out, ln_g=m_ln_g, ln_b=m_ln_b, sb_w_in=m_sb_w_in, sb_w_out=m_sb_w_out, swa_w_in=m_swa_w_in, swa_sinks=m_swa_sinks, swa_w_out=m_swa_w_out, dil_w_in=m_dil_w_in, dil_w_out=m_dil_w_out, ple_w_proj=m_ple_w_proj, ple_w_gate=m_ple_w_gate)
    mom_v = dict(ffn1_w_in=v_ffn1_w_in, ffn1_w_out=v_ffn1_w_out, ffn2_w_in=v_ffn2_w_in, ffn2_w_out=v_ffn2_w_out, ln_g=v_ln_g, ln_b=v_ln_b, sb_w_in=v_sb_w_in, sb_w_out=v_sb_w_out, swa_w_in=v_swa_w_in, swa_sinks=v_swa_sinks, swa_w_out=v_swa_w_out, dil_w_in=v_dil_w_in, dil_w_out=v_dil_w_out, ple_w_proj=v_ple_w_proj, ple_w_gate=v_ple_w_gate)
    depth = ffn1_w_in.shape[0]
    alpha = (2 * depth) ** 0.25
    c = lax.axis_index("c")

    mats = [n for n in WEIGHTS if n not in SMALL]
    local = [shard[n].astype(MXU_DTYPE) for n in mats] + [_pack_small([ln_g, ln_b], ())[None]]
    got = [_fill_slot(g, a[None]) for g, a in zip(_gather_chips(local, "gather_weights"), local)]
    full = {"swa_sinks": swa_sinks}
    for n, g in zip(mats, got):
        full[n] = _full_from_shards(g, SHARD_AXIS[n])
    for n, g in zip(("ln_g", "ln_b"), _unpack_small(got[-1][:, 0], [ln_g.shape, ln_b.shape], (N_CHIPS,))):
        full[n] = _full_from_shards(g, SHARD_AXIS[n])

    seq = x.shape[1]
    tables = _rope_tables(seq)
    xf = x[0]
    xb = xf.astype(MXU_DTYPE)
    saved = []
    for i in range(depth):
        kind, j = i % 3, i // 3
        sv = {}
        (x1, x1b), sv["ffn1"] = _ffn_forward(xf, xb, full["ffn1_w_in"][i], full["ffn1_w_out"][i], full["ln_g"][i, 0], full["ln_b"][i, 0], alpha, f"l{i}_ffn1")
        if kind == 0:
            mix, sv["mix"] = _sb_forward(x1b, full["sb_w_in"][j], full["sb_w_out"][j], f"l{i}_sb")
        elif kind == 1:
            mix, sv["mix"] = _swa_forward(x1b, full["swa_w_in"][j], swa_sinks[j], full["swa_w_out"][j], tables, f"l{i}_swa")
        else:
            mix, sv["mix"] = _dil_forward(x1b, full["dil_w_in"][j], full["dil_w_out"][j], tables, f"l{i}_dil")
        x2, x2b, sv["xhat2"], sv["rstd2"] = _ln_fwd(x1, mix, full["ln_g"][i, 1], full["ln_b"][i, 1], alpha, 1.0, f"l{i}_mix_ln")
        sv["x1b"] = x1b
        (x3, x3b), sv["ffn2"] = _ffn_forward(x2, x2b, full["ffn2_w_in"][i], full["ffn2_w_out"][i], full["ln_g"][i, 2], full["ln_b"][i, 2], alpha, f"l{i}_ffn2")
        xf, xb, sv["u"], sv["e"] = _ple_fwd(x3, x3b, p[i, 0], full["ple_w_gate"][i], full["ple_w_proj"][i], f"l{i}_ple")
        sv["x3b"] = x3b
        saved.append(sv)

    loss_part, dy = _loss(xf, loss_target[0], "loss")
    loss = lax.psum(loss_part[0, 0], ("x", "y", "c"))

    grads = {n: [None] * full[n].shape[0] for n in WEIGHTS if n not in ("ln_g", "ln_b")}
    gsum = {n: [None] * full[n].shape[0] for n in mats}
    dln_g = [[None] * 3 for _ in range(depth)]
    dln_b = [[None] * 3 for _ in range(depth)]
    dout = (dy, None, 1.0)
    for i in reversed(range(depth)):
        kind, j = i % 3, i // 3
        mixer = ("sb", "swa", "dil")[kind]
        sv = saved[i]
        dx4, dub, deb = _ple_bwd(*dout, sv["u"], sv["e"], f"l{i}_ple_bwd")
        grads["ple_w_gate"][i] = _mm(sv["x3b"], dub, "tn", F32, f"l{i}_ple_dw_gate")
        grads["ple_w_proj"][i] = _mm(p[i, 0], deb, "tn", F32, f"l{i}_ple_dw_proj")
        dxb = _mm(dub, full["ple_w_gate"][i], "nt", F32, f"l{i}_ple_dx")
        dr, dxb, grads["ffn2_w_in"][i], grads["ffn2_w_out"][i], dln_g[i][2], dln_b[i][2] = _ffn_backward((dx4, dxb, 1.0), sv["ffn2"], full["ffn2_w_in"][i], full["ffn2_w_out"][i], full["ln_g"][i, 2], f"l{i}_ffn2")
        dr, dmix, dln_g[i][1], dln_b[i][1] = _ln_bwd(dr, dxb, alpha, sv["xhat2"], sv["rstd2"], full["ln_g"][i, 1], 1.0, f"l{i}_mix_ln_bwd")
        if kind == 0:
            dxb, grads["sb_w_in"][j], grads["sb_w_out"][j] = _sb_backward(dmix, sv["mix"], sv["x1b"], full["sb_w_in"][j], full["sb_w_out"][j], f"l{i}_sb")
        elif kind == 1:
            dxb, grads["swa_w_in"][j], grads["swa_w_out"][j], grads["swa_sinks"][j] = _swa_backward(dmix, sv["mix"], sv["x1b"], full["swa_w_in"][j], swa_sinks[j], full["swa_w_out"][j], tables, f"l{i}_swa")
        else:
            dxb, grads["dil_w_in"][j], grads["dil_w_out"][j] = _dil_backward(dmix, sv["mix"], sv["x1b"], full["dil_w_in"][j], full["dil_w_out"][j], tables, f"l{i}_dil")
        dr, dxb, grads["ffn1_w_in"][i], grads["ffn1_w_out"][i], dln_g[i][0], dln_b[i][0] = _ffn_backward((dr, dxb, alpha), sv["ffn1"], full["ffn1_w_in"][i], full["ffn1_w_out"][i], full["ln_g"][i, 0], f"l{i}_ffn1")
        dout = (dr, dxb, alpha)
        layer = [("ffn1_w_in", i), ("ffn1_w_out", i), (f"{mixer}_w_in", j), (f"{mixer}_w_out", j), ("ffn2_w_in", i), ("ffn2_w_out", i), ("ple_w_proj", i), ("ple_w_gate", i)]
        summed = _reduce_group([_shards_from_full(grads[n][k], SHARD_AXIS[n] - 1) for n, k in layer], c, MXU_DTYPE, f"l{i}_reduce")
        for (n, k), g in zip(layer, summed):
            gsum[n][k] = g
    grad_x = _axpy(*dout, "grad_x")[None]

    gshard = {n: jnp.stack(g) for n, g in gsum.items()}
    small = [
        _shards_from_full(jnp.stack([jnp.concatenate(r, axis=0) for r in dln_g]), SHARD_AXIS["ln_g"]),
        _shards_from_full(jnp.stack([jnp.concatenate(r, axis=0) for r in dln_b]), SHARD_AXIS["ln_b"]),
        jnp.broadcast_to(jnp.stack(grads["swa_sinks"])[None], (N_CHIPS,) + swa_sinks.shape),
    ]
    (small_sum,) = _reduce_group([_pack_small(small, (N_CHIPS,))], c, F32, "small_reduce")
    for n, g in zip(SMALL, _unpack_small(small_sum, [shard[n].shape for n in SMALL], ())):
        gshard[n] = g

    delta, new_m, new_v = {}, {}, {}
    for n in WEIGHTS:
        shp = shard[n].shape
        two_d = (-1, shp[-1])
        d, nm, nv = _adamw(shard[n].reshape(two_d), gshard[n].reshape(two_d), mom_m[n].reshape(two_d), mom_v[n].reshape(two_d), f"adamw_{n}")
        delta[n], new_m[n], new_v[n] = d.reshape(shp), nm.reshape(shp), nv.reshape(shp)

    return (loss, grad_x, *[gshard[n] for n in WEIGHTS], *[delta[n] for n in WEIGHTS], *[new_m[n] for n in WEIGHTS], *[new_v[n] for n in WEIGHTS])
```

```python
import functools
import math

import jax
import jax.numpy as jnp
from jax import lax
from jax.experimental import pallas as pl
from jax.experimental.pallas import tpu as pltpu

F32 = jnp.float32
MXU_DTYPE = jnp.bfloat16
MESH = pl.DeviceIdType.MESH

HEAD_DIM = 64
ATT_BLK = 128
SWA_WINDOW = 128
DIL_GROUPS = ((128, 1), (512, 4), (2048, 16))
LN_EPS = 1e-5
ROPE_THETA = 10000.0
NEG_INF = -1e30
ADAM_LR, ADAM_B1, ADAM_B2, ADAM_EPS, ADAM_WD, ADAM_STEP = 0.001, 0.9, 0.999, 1e-08, 0.01, 10

VMEM_LIMIT_BYTES = 56 * 1024 * 1024
N_CHIPS = 4


def _params(sem=None):
    return pltpu.CompilerParams(dimension_semantics=sem, vmem_limit_bytes=VMEM_LIMIT_BYTES)


def _tile(n, target, unit):
    t = (min(target, n) // unit) * unit
    while t >= unit:
        if n % t == 0:
            return t
        t -= unit
    return n


def _dot(a, b, dims):
    return lax.dot_general(a.astype(MXU_DTYPE), b.astype(MXU_DTYPE), (dims, ((), ())), preferred_element_type=F32)


NN = ((1,), (0,))
NT = ((1,), (1,))
TN = ((0,), (0,))


def _accumulate(part, acc_ref, kk, nk, finish):
    if nk == 1:
        finish(part)
        return

    @pl.when(kk == 0)
    def _():
        acc_ref[...] = part

    @pl.when((kk > 0) & (kk < nk - 1))
    def _():
        acc_ref[...] += part

    @pl.when(kk == nk - 1)
    def _():
        finish(acc_ref[...] + part)


def _mm(a, b, mode, out_dtype, name, split_a=False, split_b=False, tm=1408, tn=1408, tk=1408):
    dims = {"nn": NN, "nt": NT, "tn": TN}[mode]
    if split_a:
        m, k = a.shape[1], 2 * a.shape[2]
    elif mode == "tn":
        k, m = a.shape
    else:
        m, k = a.shape
    if split_b:
        n = 2 * b.shape[2]
    elif mode == "nt":
        n = b.shape[0]
    else:
        n = b.shape[1]
    tm = _tile(m, tm, 128)
    tn = _tile(n // 2 if split_b else n, tn, 128)
    tk = _tile(k // 2 if split_a else k, tk, 128)
    nk = k // tk
    nk_half = nk // 2
    nn_half = (n // tn) // 2

    if split_a:
        a_spec = pl.BlockSpec((None, tm, tk), lambda i, j, kk: (kk // nk_half, i, kk % nk_half))
    elif mode == "tn":
        a_spec = pl.BlockSpec((tk, tm), lambda i, j, kk: (kk, i))
    else:
        a_spec = pl.BlockSpec((tm, tk), lambda i, j, kk: (i, kk))
    if split_b:
        b_spec = pl.BlockSpec((None, tk, tn), lambda i, j, kk: (j // nn_half, kk, j % nn_half))
    elif mode == "nt":
        b_spec = pl.BlockSpec((tn, tk), lambda i, j, kk: (j, kk))
    else:
        b_spec = pl.BlockSpec((tk, tn), lambda i, j, kk: (kk, j))

    def body(a_ref, b_ref, o_ref, acc_ref):
        def finish(total):
            o_ref[...] = total.astype(o_ref.dtype)

        _accumulate(_dot(a_ref[...], b_ref[...], dims), acc_ref, pl.program_id(2), nk, finish)

    return pl.pallas_call(
        body,
        name=name,
        grid=(m // tm, n // tn, nk),
        in_specs=[a_spec, b_spec],
        out_specs=pl.BlockSpec((tm, tn), lambda i, j, kk: (i, j)),
        out_shape=jax.ShapeDtypeStruct((m, n), out_dtype),
        scratch_shapes=[pltpu.VMEM((tm, tn), F32)],
        compiler_params=_params(("parallel", "parallel", "arbitrary")),
    )(a, b)


def _sigmoid(x):
    return 1.0 / (1.0 + jnp.exp(-x))


def _ffn_in(xb, w_in, name):
    s, d = xb.shape
    f = w_in.shape[1] // 2
    tm = _tile(s, 512, 128)
    tn = _tile(f, 1408, 128)
    nj = f // tn

    def body(x_ref, wg_ref, wu_ref, g_ref, u_ref, a_ref):
        x = x_ref[...]
        g = _dot(x, wg_ref[...], NN)
        u = _dot(x, wu_ref[...], NN)
        g_ref[...] = g.astype(g_ref.dtype)
        u_ref[...] = u.astype(u_ref.dtype)
        a_ref[...] = (g * _sigmoid(g) * u).astype(a_ref.dtype)

    out = pl.BlockSpec((tm, tn), lambda i, j: (i, j))
    return pl.pallas_call(
        body,
        name=name,
        grid=(s // tm, nj),
        in_specs=[
            pl.BlockSpec((tm, d), lambda i, j: (i, 0)),
            pl.BlockSpec((d, tn), lambda i, j: (0, j)),
            pl.BlockSpec((d, tn), lambda i, j: (0, j + nj)),
        ],
        out_specs=[out, out, out],
        out_shape=[
            jax.ShapeDtypeStruct((s, f), MXU_DTYPE),
            jax.ShapeDtypeStruct((s, f), MXU_DTYPE),
            jax.ShapeDtypeStruct((s, f), MXU_DTYPE),
        ],
        compiler_params=_params(("parallel", "parallel")),
    )(xb, w_in, w_in)


def _ffn_dact(dyb, w_out, gate, up, name):
    s, d = dyb.shape
    f = w_out.shape[0]
    tm = _tile(s, 512, 128)
    tn = _tile(f, 1408, 128)

    def body(dy_ref, w_ref, g_ref, u_ref, o_ref):
        dact = _dot(dy_ref[...], w_ref[...], NT)
        g = g_ref[...].astype(F32)
        sig = _sigmoid(g)
        o_ref[0] = (dact * u_ref[...].astype(F32) * (sig * (1.0 + g * (1.0 - sig)))).astype(o_ref.dtype)
        o_ref[1] = (dact * (g * sig)).astype(o_ref.dtype)

    tile = pl.BlockSpec((tm, tn), lambda i, j: (i, j))
    return pl.pallas_call(
        body,
        name=name,
        grid=(s // tm, f // tn),
        in_specs=[
            pl.BlockSpec((tm, d), lambda i, j: (i, 0)),
            pl.BlockSpec((tn, d), lambda i, j: (j, 0)),
            tile,
            tile,
        ],
        out_specs=pl.BlockSpec((2, tm, tn), lambda i, j: (0, i, j)),
        out_shape=jax.ShapeDtypeStruct((2, s, f), MXU_DTYPE),
        compiler_params=_params(("parallel", "parallel")),
    )(dyb, w_out, gate, up)


def _ple_fwd(x, xb, p, w_gate, w_proj, name):
    s, d = x.shape
    pd = p.shape[1]
    tm = _tile(s, 1024, 128)
    tn = _tile(d, 512, 128)

    def body(x_ref, xb_ref, p_ref, wg_ref, wp_ref, o_ref, ob_ref, u_ref, e_ref):
        u = _dot(xb_ref[...], wg_ref[...], NN)
        e = _dot(p_ref[...], wp_ref[...], NN)
        out = x_ref[...] + _sigmoid(u) * e
        o_ref[...] = out
        ob_ref[...] = out.astype(ob_ref.dtype)
        u_ref[...] = u
        e_ref[...] = e

    tile = pl.BlockSpec((tm, tn), lambda i, j: (i, j))
    return pl.pallas_call(
        body,
        name=name,
        grid=(s // tm, d // tn),
        in_specs=[
            tile,
            pl.BlockSpec((tm, d), lambda i, j: (i, 0)),
            pl.BlockSpec((tm, pd), lambda i, j: (i, 0)),
            pl.BlockSpec((d, tn), lambda i, j: (0, j)),
            pl.BlockSpec((pd, tn), lambda i, j: (0, j)),
        ],
        out_specs=[tile, tile, tile, tile],
        out_shape=[
            jax.ShapeDtypeStruct((s, d), F32),
            jax.ShapeDtypeStruct((s, d), MXU_DTYPE),
            jax.ShapeDtypeStruct((s, d), F32),
            jax.ShapeDtypeStruct((s, d), F32),
        ],
        compiler_params=_params(("parallel", "parallel")),
    )(x, xb, p, w_gate, w_proj)


def _rows_spec(ts, d):
    return pl.BlockSpec((ts, d), lambda i: (i, 0))


def _mm_ln(a, w, x, g, b, alpha, beta, name):
    s, k = a.shape
    d = w.shape[1]
    tm = _tile(s, 512, 128)
    tk = _tile(k, 1408, 128)
    nk = k // tk

    def body(a_ref, w_ref, x_ref, g_ref, b_ref, o_ref, ob_ref, xh_ref, rs_ref, acc_ref):
        def finish(y):
            r = alpha * x_ref[...] + beta * y
            mu = jnp.mean(r, axis=1, keepdims=True)
            cen = r - mu
            var = jnp.mean(cen * cen, axis=1, keepdims=True)
            rstd = lax.rsqrt(var + LN_EPS)
            xhat = cen * rstd
            out = xhat * g_ref[...] + b_ref[...]
            o_ref[...] = out
            ob_ref[...] = out.astype(ob_ref.dtype)
            xh_ref[...] = xhat
            rs_ref[...] = rstd

        _accumulate(_dot(a_ref[...], w_ref[...], NN), acc_ref, pl.program_id(1), nk, finish)

    rows = pl.BlockSpec((tm, d), lambda i, kk: (i, 0))
    vec = pl.BlockSpec((1, d), lambda i, kk: (0, 0))
    return pl.pallas_call(
        body,
        name=name,
        grid=(s // tm, nk),
        in_specs=[pl.BlockSpec((tm, tk), lambda i, kk: (i, kk)), pl.BlockSpec((tk, d), lambda i, kk: (kk, 0)), rows, vec, vec],
        out_specs=[rows, rows, rows, pl.BlockSpec((tm, 1), lambda i, kk: (i, 0))],
        out_shape=[
            jax.ShapeDtypeStruct((s, d), F32),
            jax.ShapeDtypeStruct((s, d), MXU_DTYPE),
            jax.ShapeDtypeStruct((s, d), F32),
            jax.ShapeDtypeStruct((s, 1), F32),
        ],
        scratch_shapes=[pltpu.VMEM((tm, d), F32)],
        compiler_params=_params(("parallel", "arbitrary")),
    )(a, w, x, g.reshape(1, d), b.reshape(1, d))


def _ln_bwd(ga, gb, ca, xhat, rstd, g, beta, name):
    s, d = xhat.shape
    ts = _tile(s, 512, 8)

    def body(ga_ref, gb_ref, xh_ref, rs_ref, g_ref, dr_ref, dyb_ref, dg_ref, db_ref):
        @pl.when(pl.program_id(0) == 0)
        def _():
            dg_ref[...] = jnp.zeros_like(dg_ref)
            db_ref[...] = jnp.zeros_like(db_ref)

        dout = ca * ga_ref[...] + gb_ref[...]
        xhat = xh_ref[...]
        dg_ref[...] += jnp.sum(dout * xhat, axis=0, keepdims=True)
        db_ref[...] += jnp.sum(dout, axis=0, keepdims=True)
        dxh = dout * g_ref[...]
        m1 = jnp.mean(dxh, axis=1, keepdims=True)
        m2 = jnp.mean(dxh * xhat, axis=1, keepdims=True)
        dr = rs_ref[...] * (dxh - m1 - xhat * m2)
        dr_ref[...] = dr
        dyb_ref[...] = (beta * dr).astype(dyb_ref.dtype)

    vec = pl.BlockSpec((1, d), lambda i: (0, 0))
    return pl.pallas_call(
        body,
        name=name,
        grid=(s // ts,),
        in_specs=[_rows_spec(ts, d), _rows_spec(ts, d), _rows_spec(ts, d), _rows_spec(ts, 1), vec],
        out_specs=[_rows_spec(ts, d), _rows_spec(ts, d), vec, vec],
        out_shape=[
            jax.ShapeDtypeStruct((s, d), F32),
            jax.ShapeDtypeStruct((s, d), MXU_DTYPE),
            jax.ShapeDtypeStruct((1, d), F32),
            jax.ShapeDtypeStruct((1, d), F32),
        ],
        compiler_params=_params(("arbitrary",)),
    )(ga, gb, xhat, rstd, g.reshape(1, d))


def _ple_bwd(ga, gb, ca, u, e, name):
    s, d = u.shape
    ts = _tile(s, 512, 8)
    grads = [ga] if gb is None else [ga, gb]

    def body(*refs):
        u_ref, e_ref, dx_ref, du_ref, de_ref = refs[len(grads):]
        dx = ca * refs[0][...]
        if gb is not None:
            dx = dx + refs[1][...]
        sig = _sigmoid(u_ref[...])
        dx_ref[...] = dx
        du_ref[...] = (dx * e_ref[...] * sig * (1.0 - sig)).astype(du_ref.dtype)
        de_ref[...] = (dx * sig).astype(de_ref.dtype)

    return pl.pallas_call(
        body,
        name=name,
        grid=(s // ts,),
        in_specs=[_rows_spec(ts, d)] * (len(grads) + 2),
        out_specs=[_rows_spec(ts, d)] * 3,
        out_shape=[
            jax.ShapeDtypeStruct((s, d), F32),
            jax.ShapeDtypeStruct((s, d), MXU_DTYPE),
            jax.ShapeDtypeStruct((s, d), MXU_DTYPE),
        ],
        compiler_params=_params(("parallel",)),
    )(*grads, u, e)


def _axpy(ga, gb, ca, name):
    s, d = ga.shape
    ts = _tile(s, 512, 8)

    def body(ga_ref, gb_ref, o_ref):
        o_ref[...] = ca * ga_ref[...] + gb_ref[...]

    return pl.pallas_call(
        body,
        name=name,
        grid=(s // ts,),
        in_specs=[_rows_spec(ts, d)] * 2,
        out_specs=_rows_spec(ts, d),
        out_shape=jax.ShapeDtypeStruct((s, d), F32),
        compiler_params=_params(("parallel",)),
    )(ga, gb)


def _loss(y, target, name):
    s, d = y.shape
    ts = _tile(s, 512, 8)

    def body(y_ref, t_ref, l_ref, dy_ref):
        @pl.when(pl.program_id(0) == 0)
        def _():
            l_ref[...] = jnp.zeros_like(l_ref)

        err = y_ref[...] - t_ref[...]
        l_ref[...] += (0.5 / d) * jnp.sum(jnp.sum(err * err, axis=1, keepdims=True), axis=0, keepdims=True)
        dy_ref[...] = err * (1.0 / d)

    return pl.pallas_call(
        body,
        name=name,
        grid=(s // ts,),
        in_specs=[_rows_spec(ts, d)] * 2,
        out_specs=[pl.BlockSpec((1, 1), lambda i: (0, 0)), _rows_spec(ts, d)],
        out_shape=[jax.ShapeDtypeStruct((1, 1), F32), jax.ShapeDtypeStruct((s, d), F32)],
        compiler_params=_params(("arbitrary",)),
    )(y, target)


def _rope_tables(seq):
    pos = jnp.arange(seq, dtype=F32)
    inv = ROPE_THETA ** (-jnp.arange(0, HEAD_DIM, 2, dtype=F32) / HEAD_DIM)
    ang = pos[:, None] * inv[None, :]
    cos, sin = jnp.cos(ang), jnp.sin(ang)
    cos2 = jnp.concatenate([cos, cos, cos, cos, jnp.ones((seq, 128), F32)], axis=1)
    sin2 = jnp.concatenate([-sin, sin, -sin, sin, jnp.zeros((seq, 128), F32)], axis=1)
    return cos2, sin2


def _rope(h, cos2, sin2, plain_block, out_dtype, name, width=128):
    s, n = h.shape
    ts = _tile(s, 512, 8)

    def body(h_ref, c_ref, s_ref, o_ref):
        cos, sin = c_ref[...], s_ref[...]
        lane = lax.broadcasted_iota(jnp.int32, cos.shape, 1)
        first_half = lane % HEAD_DIM < HEAD_DIM // 2
        for w in range(width // 128):
            cols = slice(w * 128, (w + 1) * 128)
            x = h_ref[:, cols].astype(F32)
            partner = jnp.where(first_half, pltpu.roll(x, 128 - HEAD_DIM // 2, 1), pltpu.roll(x, HEAD_DIM // 2, 1))
            o_ref[:, cols] = (x * cos + partner * sin).astype(o_ref.dtype)

    tile = pl.BlockSpec((ts, width), lambda i, j: (i, j))
    table = pl.BlockSpec((ts, 128), lambda i, j: (i, plain_block(j)))
    return pl.pallas_call(
        body,
        name=name,
        grid=(s // ts, n // width),
        in_specs=[tile, table, table],
        out_specs=tile,
        out_shape=jax.ShapeDtypeStruct((s, n), out_dtype),
        compiler_params=_params(("parallel", "parallel")),
    )(h, cos2, sin2)


SB_TQ = 512
SB_BLK = 128
SB_WALK = 2
SB_CLOSED = 110.0


def _sb_walk(n_trips, carry, key_blocks):
    def still_open(carry):
        lowest = functools.reduce(jnp.minimum, [c[0] for c in carry])
        return jnp.min(lowest) < SB_CLOSED

    def cond(state):
        t, go, _ = state
        return jnp.logical_and(t < n_trips, go)

    def body(state):
        t, _, carry = state
        carry = key_blocks(t, carry)
        return t + 1, still_open(carry), carry

    return lax.while_loop(cond, body, (jnp.int32(0), still_open(carry), carry))[2]


def _tri2(strict):
    row = lax.broadcasted_iota(jnp.int32, (2 * SB_BLK, SB_BLK), 0) % SB_BLK
    col = lax.broadcasted_iota(jnp.int32, (2 * SB_BLK, SB_BLK), 1)
    return (row > col if strict else row >= col).astype(MXU_DTYPE)


def _cumsum_dot(x, tri2):
    hi = x.astype(MXU_DTYPE)
    lo = x - hi.astype(F32)
    return _dot(jnp.concatenate([hi, lo.astype(MXU_DTYPE)], axis=1), tri2, NN)


def _sb_logits(z, valid):
    l1p = jnp.log(1.0 + jnp.exp(-jnp.abs(z)))
    sp = jnp.maximum(z, 0.0) + l1p
    ls = z - sp
    if valid is not None:
        sp = jnp.where(valid, sp, 0.0)
    return sp, ls


def _sb_weights(ls, after, valid):
    a = jnp.exp(ls - after)
    return a if valid is None else jnp.where(valid, a, 0.0)


def _sb_setup(q_ref, k_ref, v_ref, n_sub, scale):
    qs = [q_ref[0, u * SB_BLK:(u + 1) * SB_BLK, :] * scale for u in range(n_sub)]
    row = lax.broadcasted_iota(jnp.int32, (SB_BLK, SB_BLK), 0)
    col = lax.broadcasted_iota(jnp.int32, (SB_BLK, SB_BLK), 1)

    def load(jj):
        start = pl.multiple_of(jj * SB_BLK, SB_BLK)
        return start, k_ref[0, pl.ds(start, SB_BLK), :], v_ref[0, pl.ds(start, SB_BLK), :]

    return qs, col < row, load


def _sb_fwd(q, k, v, name):
    nh, s, dh = q.shape
    tq = min(SB_TQ, s)
    n_sub = tq // SB_BLK
    scale = dh ** -0.5

    def body(q_ref, k_ref, v_ref, o_ref, ob_ref):
        base = pl.program_id(1) * n_sub
        qs, diag_valid, load = _sb_setup(q_ref, k_ref, v_ref, n_sub, scale)
        tri_after = _tri2(True)

        def key_blocks(first, carry, diagonal):
            blocks = list(range((n_sub if diagonal else SB_WALK) - 1, -1, -1))
            kv = {d: load(first + d) for d in blocks}
            tiles = [(d, u) for d in blocks for u in range(d if diagonal else 0, n_sub)]
            valid = {t: diag_valid if diagonal and t[0] == t[1] else None for t in tiles}
            z = {t: _dot(qs[t[1]], kv[t[0]][1], NT) for t in tiles}
            sp_ls = {t: _sb_logits(z[t], valid[t]) for t in tiles}
            inside = {t: _cumsum_dot(sp_ls[t][0], tri_after) for t in tiles}
            carry = list(carry)
            for t in tiles:
                after_c, acc = carry[t[1]]
                sp, ls = sp_ls[t]
                a = _sb_weights(ls, after_c + inside[t], valid[t])
                carry[t[1]] = (after_c + jnp.sum(sp, axis=1, keepdims=True), acc + _dot(a, kv[t[0]][2], NN))
            return tuple(carry)

        carry = tuple((jnp.zeros((SB_BLK, 1), F32), jnp.zeros((SB_BLK, dh), F32)) for _ in range(n_sub))
        carry = key_blocks(base, carry, True)
        carry = _sb_walk(base // SB_WALK, carry, lambda t, c: key_blocks(base - SB_WALK * (t + 1), c, False))
        for u in range(n_sub):
            rows = slice(u * SB_BLK, (u + 1) * SB_BLK)
            o_ref[0, rows, :] = carry[u][1]
            ob_ref[0, rows, :] = carry[u][1].astype(ob_ref.dtype)

    blk = pl.BlockSpec((1, tq, dh), lambda h, i: (h, i, 0))
    full = pl.BlockSpec((1, s, dh), lambda h, i: (h, 0, 0))
    return pl.pallas_call(
        body,
        name=name,
        grid=(nh, s // tq),
        in_specs=[blk, full, full],
        out_specs=[blk, blk],
        out_shape=[jax.ShapeDtypeStruct((nh, s, dh), F32), jax.ShapeDtypeStruct((nh, s, dh), MXU_DTYPE)],
        compiler_params=_params(("parallel", "parallel")),
    )(q, k, v)


def _sb_bwd(q, k, v, o, do, name):
    nh, s, dh = q.shape
    tq = min(SB_TQ, s)
    n_sub = tq // SB_BLK
    scale = dh ** -0.5

    def body(q_ref, k_ref, v_ref, o_ref, do_ref, dq_ref, dk_ref, dv_ref):
        i = pl.program_id(1)
        base = i * n_sub

        @pl.when(i == 0)
        def _():
            dk_ref[...] = jnp.zeros_like(dk_ref)
            dv_ref[...] = jnp.zeros_like(dv_ref)

        qs, diag_valid, load = _sb_setup(q_ref, k_ref, v_ref, n_sub, scale)
        dob = [do_ref[0, u * SB_BLK:(u + 1) * SB_BLK, :] for u in range(n_sub)]
        total = [jnp.sum(dob[u].astype(F32) * o_ref[0, u * SB_BLK:(u + 1) * SB_BLK, :], axis=1, keepdims=True) for u in range(n_sub)]
        tri_after = _tri2(True)
        tri_from = _tri2(False)

        def key_blocks(first, carry, diagonal):
            blocks = list(range((n_sub if diagonal else SB_WALK) - 1, -1, -1))
            kv = {d: load(first + d) for d in blocks}
            tiles = [(d, u) for d in blocks for u in range(d if diagonal else 0, n_sub)]
            valid = {t: diag_valid if diagonal and t[0] == t[1] else None for t in tiles}
            z = {t: _dot(qs[t[1]], kv[t[0]][1], NT) for t in tiles}
            da = {t: _dot(dob[t[1]], kv[t[0]][2], NT) for t in tiles}
            sp_ls = {t: _sb_logits(z[t], valid[t]) for t in tiles}
            inside = {t: _cumsum_dot(sp_ls[t][0], tri_after) for t in tiles}
            after_run = [c[0] for c in carry]
            ab, dl = {}, {}
            for t in tiles:
                sp, ls = sp_ls[t]
                ab[t] = _sb_weights(ls, after_run[t[1]] + inside[t], valid[t]).astype(MXU_DTYPE)
                dl[t] = ab[t].astype(F32) * da[t]
                after_run[t[1]] = after_run[t[1]] + jnp.sum(sp, axis=1, keepdims=True)
            from_in = {t: _cumsum_dot(dl[t], tri_from) for t in tiles}
            from_run = [c[1] for c in carry]
            dq = [c[2] for c in carry]
            for d in blocks:
                start, kb, _ = kv[d]
                dk = dv = None
                for u in range(d if diagonal else 0, n_sub):
                    t = (d, u)
                    dz = dl[t] - jnp.exp(sp_ls[t][1]) * (dl[t] + total[u] - (from_run[u] + from_in[t]))
                    if valid[t] is not None:
                        dz = jnp.where(valid[t], dz, 0.0)
                    dzb = dz.astype(MXU_DTYPE)
                    from_run[u] = from_run[u] + jnp.sum(dl[t], axis=1, keepdims=True)
                    dq[u] = dq[u] + _dot(dzb, kb, NN)
                    dk_u, dv_u = _dot(dzb, qs[u], TN), _dot(ab[t], dob[u], TN)
                    dk = dk_u if dk is None else dk + dk_u
                    dv = dv_u if dv is None else dv + dv_u
                dk_ref[0, pl.ds(start, SB_BLK), :] += dk
                dv_ref[0, pl.ds(start, SB_BLK), :] += dv
            return tuple(zip(after_run, from_run, dq))

        carry = tuple((jnp.zeros((SB_BLK, 1), F32), jnp.zeros((SB_BLK, 1), F32), jnp.zeros((SB_BLK, dh), F32)) for _ in range(n_sub))
        carry = key_blocks(base, carry, True)
        carry = _sb_walk(base // SB_WALK, carry, lambda t, c: key_blocks(base - SB_WALK * (t + 1), c, False))
        for u in range(n_sub):
            dq_ref[0, u * SB_BLK:(u + 1) * SB_BLK, :] = carry[u][2] * scale

    blk = pl.BlockSpec((1, tq, dh), lambda h, i: (h, i, 0))
    full = pl.BlockSpec((1, s, dh), lambda h, i: (h, 0, 0))
    shape = jax.ShapeDtypeStruct((nh, s, dh), F32)
    return pl.pallas_call(
        body,
        name=name,
        grid=(nh, s // tq),
        in_specs=[blk, full, full, blk, blk],
        out_specs=[blk, full, full],
        out_shape=[shape, shape, shape],
        compiler_params=_params(("parallel", "arbitrary")),
    )(q, k, v, o, do)


BAND_TQ = 1024


def _band_scores(q_ref, k_ref, i, sub, tq, length, max_dist, scale):
    t0 = i * tq + sub * ATT_BLK
    ks = pl.multiple_of(jnp.minimum(jnp.maximum(t0 - ATT_BLK, 0), length - 2 * ATT_BLK), ATT_BLK)
    qs = q_ref[0, sub * ATT_BLK:(sub + 1) * ATT_BLK, :] * scale
    kw = k_ref[0, pl.ds(ks, 2 * ATT_BLK), :]
    sc = _dot(qs, kw, NT)
    diff = (t0 + lax.broadcasted_iota(jnp.int32, sc.shape, 0)) - (ks + lax.broadcasted_iota(jnp.int32, sc.shape, 1))
    valid = (diff >= 0) & (diff <= max_dist)
    return ks, qs, kw, jnp.where(valid, sc, NEG_INF)


def _band_fwd(q, k, v, sinks, max_dist, name):
    bq, length, dh = q.shape
    group = bq // k.shape[0]
    tq = min(BAND_TQ, length)
    scale = dh ** -0.5
    n_sink = 0 if sinks is None else sinks.shape[0]

    def body(*refs):
        if n_sink:
            sink_ref, q_ref, k_ref, v_ref, o_ref, lse_ref = refs
            sink = sink_ref[pl.program_id(0) % n_sink]
        else:
            q_ref, k_ref, v_ref, o_ref, lse_ref = refs
        i = pl.program_id(1)
        scores = [_band_scores(q_ref, k_ref, i, sub, tq, length, max_dist, scale) for sub in range(tq // ATT_BLK)]
        for sub, (ks, _, _, sc) in enumerate(scores):
            m = jnp.max(sc, axis=1, keepdims=True)
            if n_sink:
                m = jnp.maximum(m, sink)
            e = jnp.exp(sc - m)
            den = jnp.sum(e, axis=1, keepdims=True)
            if n_sink:
                den = den + jnp.exp(sink - m)
            rows = slice(sub * ATT_BLK, (sub + 1) * ATT_BLK)
            o_ref[0, rows, :] = _dot(e / den, v_ref[0, pl.ds(ks, 2 * ATT_BLK), :], NN)
            lse_ref[0, rows, :] = m + jnp.log(den)

    qblk = pl.BlockSpec((1, tq, dh), lambda b, i: (b, i, 0))
    kfull = pl.BlockSpec((1, length, dh), lambda b, i: (b // group, 0, 0))
    in_specs = [qblk, kfull, kfull]
    args = [q, k, v]
    if n_sink:
        in_specs = [pl.BlockSpec(memory_space=pltpu.SMEM)] + in_specs
        args = [sinks] + args
    return pl.pallas_call(
        body,
        name=name,
        grid=(bq, length // tq),
        in_specs=in_specs,
        out_specs=[qblk, pl.BlockSpec((1, tq, 1), lambda b, i: (b, i, 0))],
        out_shape=[jax.ShapeDtypeStruct((bq, length, dh), F32), jax.ShapeDtypeStruct((bq, length, 1), F32)],
        compiler_params=_params(("parallel", "parallel")),
    )(*args)


def _band_bwd(q, k, v, do, lse, delta, sinks, max_dist, name):
    bq, length, dh = q.shape
    group = bq // k.shape[0]
    tq = min(BAND_TQ, length)
    scale = dh ** -0.5
    n_sink = 0 if sinks is None else sinks.shape[0]

    def body(*refs):
        if n_sink:
            sink_ref, q_ref, k_ref, v_ref, do_ref, lse_ref, dl_ref, dq_ref, dk_ref, dv_ref, ds_ref = refs
            sink = sink_ref[pl.program_id(0) % n_sink]
        else:
            q_ref, k_ref, v_ref, do_ref, lse_ref, dl_ref, dq_ref, dk_ref, dv_ref, ds_ref = refs
        i = pl.program_id(1)

        @pl.when((i == 0) & (pl.program_id(0) % group == 0))
        def _():
            dk_ref[...] = jnp.zeros_like(dk_ref)
            dv_ref[...] = jnp.zeros_like(dv_ref)

        @pl.when(i == 0)
        def _():
            ds_ref[...] = jnp.zeros_like(ds_ref)

        scores = [_band_scores(q_ref, k_ref, i, sub, tq, length, max_dist, scale) for sub in range(tq // ATT_BLK)]
        dps = [_dot(do_ref[0, sub * ATT_BLK:(sub + 1) * ATT_BLK, :], v_ref[0, pl.ds(ks, 2 * ATT_BLK), :], NT) for sub, (ks, _, _, _) in enumerate(scores)]
        for sub, (ks, qs, kw, sc) in enumerate(scores):
            rows = slice(sub * ATT_BLK, (sub + 1) * ATT_BLK)
            lse = lse_ref[0, rows, :]
            delta_r = dl_ref[0, rows, :]
            dob = do_ref[0, rows, :]
            p = jnp.exp(sc - lse)
            dsb = (p * (dps[sub] - delta_r)).astype(MXU_DTYPE)
            dq_ref[0, rows, :] = _dot(dsb, kw, NN) * scale
            dk_ref[0, pl.ds(ks, 2 * ATT_BLK), :] += _dot(dsb, qs, TN)
            dv_ref[0, pl.ds(ks, 2 * ATT_BLK), :] += _dot(p, dob, TN)
            if n_sink:
                ds_ref[...] += jnp.sum(-jnp.exp(sink - lse) * delta_r, axis=0, keepdims=True)

    qblk = pl.BlockSpec((1, tq, dh), lambda b, i: (b, i, 0))
    qcol = pl.BlockSpec((1, tq, 1), lambda b, i: (b, i, 0))
    kfull = pl.BlockSpec((1, length, dh), lambda b, i: (b // group, 0, 0))
    in_specs = [qblk, kfull, kfull, qblk, qcol, qcol]
    args = [q, k, v, do, lse, delta]
    if n_sink:
        in_specs = [pl.BlockSpec(memory_space=pltpu.SMEM)] + in_specs
        args = [sinks] + args
    kshape = jax.ShapeDtypeStruct((k.shape[0], length, dh), F32)
    return pl.pallas_call(
        body,
        name=name,
        grid=(bq, length // tq),
        in_specs=in_specs,
        out_specs=[qblk, kfull, kfull, pl.BlockSpec((1, 8, 128), lambda b, i: (b, 0, 0))],
        out_shape=[jax.ShapeDtypeStruct((bq, length, dh), F32), kshape, kshape, jax.ShapeDtypeStruct((bq, 8, 128), F32)],
        compiler_params=_params(("arbitrary", "arbitrary")),
    )(*args)


def _merge_weights(lse_refs):
    lses = [r[0] for r in lse_refs]
    m = functools.reduce(jnp.maximum, lses)
    es = [jnp.exp(l - m) for l in lses]
    den = functools.reduce(lambda a, b: a + b, es)
    return [e / den for e in es]


def _merge_fwd(outs, lses, name):
    n = len(outs)
    nh, s, dh = outs[0].shape
    ts = _tile(s, 1024, 8)

    def body(*refs):
        ws = _merge_weights(refs[n:2 * n])
        o = functools.reduce(lambda a, b: a + b, [w * r[0] for w, r in zip(ws, refs[:n])])
        refs[2 * n][0] = o
        refs[2 * n + 1][0] = o.astype(MXU_DTYPE)

    blk = pl.BlockSpec((1, ts, dh), lambda h, i: (h, i, 0))
    col = pl.BlockSpec((1, ts, 1), lambda h, i: (h, i, 0))
    return pl.pallas_call(
        body,
        name=name,
        grid=(nh, s // ts),
        in_specs=[blk] * n + [col] * n,
        out_specs=[blk, blk],
        out_shape=[jax.ShapeDtypeStruct((nh, s, dh), F32), jax.ShapeDtypeStruct((nh, s, dh), MXU_DTYPE)],
        compiler_params=_params(("parallel", "parallel")),
    )(*outs, *lses)


def _merge_bwd(do, o, lses, name):
    n = len(lses)
    nh, s, dh = o.shape
    ts = _tile(s, 1024, 8)

    def body(*refs):
        do_ref, o_ref = refs[:2]
        ws = _merge_weights(refs[2:2 + n])
        dof = do_ref[0].astype(F32)
        base = jnp.sum(dof * o_ref[0], axis=1, keepdims=True)
        for g in range(n):
            refs[2 + n + g][0] = (ws[g] * dof).astype(MXU_DTYPE)
            refs[2 + 2 * n + g][0] = ws[g] * base

    blk = pl.BlockSpec((1, ts, dh), lambda h, i: (h, i, 0))
    col = pl.BlockSpec((1, ts, 1), lambda h, i: (h, i, 0))
    res = pl.pallas_call(
        body,
        name=name,
        grid=(nh, s // ts),
        in_specs=[blk, blk] + [col] * n,
        out_specs=[blk] * n + [col] * n,
        out_shape=[jax.ShapeDtypeStruct((nh, s, dh), MXU_DTYPE)] * n + [jax.ShapeDtypeStruct((nh, s, 1), F32)] * n,
        compiler_params=_params(("parallel", "parallel")),
    )(do, o, *lses)
    return res[:n], res[n:]


def _to_heads(a, dil=1):
    s, n = a.shape
    nh = n // HEAD_DIM
    a = a.reshape(s // dil, dil, nh, HEAD_DIM).transpose(1, 2, 0, 3)
    return a.reshape(dil * nh, s // dil, HEAD_DIM)


def _from_heads(a, dil=1):
    b, ls, c = a.shape
    nh = b // dil
    return a.reshape(dil, nh, ls, c).transpose(2, 0, 1, 3).reshape(ls * dil, nh * c)


def _restride(a, dil):
    nh, s, c = a.shape
    return a.reshape(nh, s // dil, dil, c).transpose(2, 0, 1, 3).reshape(dil * nh, s // dil, c)


def _unstride(a, dil):
    b, ls, c = a.shape
    nh = b // dil
    return a.reshape(dil, nh, ls, c).transpose(1, 2, 0, 3).reshape(nh, ls * dil, c)


ANY = pl.BlockSpec(memory_space=pl.ANY)


def _position():
    x, y, c = lax.axis_index("x"), lax.axis_index("y"), lax.axis_index("c")
    return x, y, c, [(1 - x, y), (x, 1 - y), (1 - x, 1 - y)]


def _remote(src, dst, send_sems, recv_sems, k, to):
    return pltpu.make_async_remote_copy(src_ref=src, dst_ref=dst, send_sem=send_sems.at[k], recv_sem=recv_sems.at[k], device_id=to, device_id_type=MESH)


def _gather_chips(arrs, name):
    n = len(arrs)

    def body(*refs):
        f_refs, g_refs = refs[:n], refs[n:2 * n]
        send_sems, recv_sems = refs[2 * n:]
        x, y, c, chips = _position()
        me, sibling = (x, y, c), (x, y, 1 - c)

        def half(j, ref, sel):
            rows = f_refs[j].shape[1] // 2
            return ref.at[:, pl.ds(sel * rows, rows), :]

        def slot(j, chip, sel):
            return half(j, g_refs[j].at[2 * chip[0] + chip[1]], sel)

        first = [_remote(half(j, f_refs[j], c), slot(j, (x, y), c), send_sems, recv_sems, 6 * j + k, (*chip, c)) for j in range(n) for k, chip in enumerate(chips)]
        for cp in first:
            cp.start()
        passed = []
        for j in range(n):
            for k, chip in enumerate(chips):
                _remote(slot(j, chip, c), slot(j, chip, c), send_sems, recv_sems, 6 * j + k, me).wait_recv()
                passed.append(_remote(slot(j, chip, c), slot(j, chip, c), send_sems, recv_sems, 6 * j + 3 + k, sibling))
                passed[-1].start()
        for j in range(n):
            for k, chip in enumerate(chips):
                _remote(slot(j, chip, 1 - c), slot(j, chip, 1 - c), send_sems, recv_sems, 6 * j + 3 + k, me).wait_recv()
        for cp in first + passed:
            cp.wait_send()

    return pl.pallas_call(
        body,
        name=name,
        in_specs=[ANY] * n,
        out_specs=[ANY] * n,
        out_shape=[jax.ShapeDtypeStruct((N_CHIPS,) + a.shape, a.dtype) for a in arrs],
        scratch_shapes=[pltpu.SemaphoreType.DMA((6 * n,)), pltpu.SemaphoreType.DMA((6 * n,))],
    )(*arrs)


def _swap_sibling(arrs, name):
    n = len(arrs)

    def body(*refs):
        x, y, c, _ = _position()
        send_sems, recv_sems = refs[2 * n:]
        copies = [_remote(refs[j], refs[n + j], send_sems, recv_sems, j, (x, y, 1 - c)) for j in range(n)]
        for cp in copies:
            cp.start()
        for cp in copies:
            cp.wait()

    return pl.pallas_call(
        body,
        name=name,
        in_specs=[ANY] * n,
        out_specs=[ANY] * n,
        out_shape=[jax.ShapeDtypeStruct(a.shape, a.dtype) for a in arrs],
        scratch_shapes=[pltpu.SemaphoreType.DMA((n,)), pltpu.SemaphoreType.DMA((n,))],
    )(*arrs)


def _exchange_chips(parts, name):
    n = len(parts)

    def body(*refs):
        p_refs, q_refs = refs[:n], refs[n:2 * n]
        send_sems, recv_sems = refs[2 * n:]
        x, y, c, chips = _position()
        my_slot = 2 * x + y
        sends = [_remote(p_refs[j].at[2 * px + py], q_refs[j].at[my_slot], send_sems, recv_sems, 3 * j + k, (px, py, c)) for j in range(n) for k, (px, py) in enumerate(chips)]
        for cp in sends:
            cp.start()
        for j in range(n):
            for k, (px, py) in enumerate(chips):
                landed = q_refs[j].at[2 * px + py]
                _remote(landed, landed, send_sems, recv_sems, 3 * j + k, (x, y, c)).wait_recv()
        for cp in sends:
            cp.wait_send()

    return pl.pallas_call(
        body,
        name=name,
        in_specs=[ANY] * n,
        out_specs=[ANY] * n,
        out_shape=[jax.ShapeDtypeStruct(a.shape, a.dtype) for a in parts],
        scratch_shapes=[pltpu.SemaphoreType.DMA((3 * n,)), pltpu.SemaphoreType.DMA((3 * n,))],
    )(*parts)


def _share_sibling(halves, name):
    n = len(halves)

    def body(*refs):
        h_refs, o_refs = refs[:n], refs[n:2 * n]
        send_sems, recv_sems = refs[2 * n:]
        x, y, c, _ = _position()

        def rows(j, sel):
            r = h_refs[j].shape[0]
            return o_refs[j].at[pl.ds(sel * r, r), :]

        sends = [_remote(h_refs[j], rows(j, c), send_sems, recv_sems, j, (x, y, 1 - c)) for j in range(n)]
        for cp in sends:
            cp.start()
        for j in range(n):
            _remote(rows(j, 1 - c), rows(j, 1 - c), send_sems, recv_sems, j, (x, y, c)).wait_recv()
        for cp in sends:
            cp.wait_send()

    return pl.pallas_call(
        body,
        name=name,
        in_specs=[ANY] * n,
        out_specs=[ANY] * n,
        out_shape=[jax.ShapeDtypeStruct((2 * a.shape[0],) + a.shape[1:], a.dtype) for a in halves],
        scratch_shapes=[pltpu.SemaphoreType.DMA((n,)), pltpu.SemaphoreType.DMA((n,))],
    )(*halves)


def _sum_rows(r, c, n_in):
    return _tile(r, max(16, (1 << 20) // (c * (n_in + 1))), 16)


def _add2(a, b, out_dtype, name):
    n, r, c = a.shape
    tr = _sum_rows(r, c, 2)

    def body(a_ref, b_ref, o_ref):
        o_ref[...] = (a_ref[...].astype(F32) + b_ref[...].astype(F32)).astype(o_ref.dtype)

    blk = pl.BlockSpec((1, tr, c), lambda s, i: (s, i, 0))
    return pl.pallas_call(
        body,
        name=name,
        grid=(n, r // tr),
        in_specs=[blk, blk],
        out_specs=blk,
        out_shape=jax.ShapeDtypeStruct(a.shape, out_dtype),
        compiler_params=_params(("parallel", "parallel")),
    )(a, b)


def _sum_slots(q, name):
    n, r, c = q.shape
    tr = _sum_rows(r, c, n)

    def body(q_ref, o_ref):
        acc = q_ref[0].astype(F32)
        for s in range(1, n):
            acc = acc + q_ref[s].astype(F32)
        o_ref[...] = acc

    return pl.pallas_call(
        body,
        name=name,
        grid=(r // tr,),
        in_specs=[pl.BlockSpec((n, tr, c), lambda i: (0, i, 0))],
        out_specs=pl.BlockSpec((tr, c), lambda i: (i, 0)),
        out_shape=jax.ShapeDtypeStruct((r, c), F32),
        compiler_params=_params(("parallel",)),
    )(q)


def _adamw(w, g, m, v, name):
    r, c = w.shape
    tr = _tile(r, 256, 8)
    c1 = 1.0 - ADAM_B1 ** ADAM_STEP
    c2 = 1.0 - ADAM_B2 ** ADAM_STEP

    def body(w_ref, g_ref, m_ref, v_ref, d_ref, nm_ref, nv_ref):
        g = g_ref[...]
        nm = ADAM_B1 * m_ref[...] + (1.0 - ADAM_B1) * g
        nv = ADAM_B2 * v_ref[...] + (1.0 - ADAM_B2) * (g * g)
        d_ref[...] = -ADAM_LR * ((nm / c1) / (jnp.sqrt(nv / c2) + ADAM_EPS) + ADAM_WD * w_ref[...])
        nm_ref[...] = nm
        nv_ref[...] = nv

    blk = pl.BlockSpec((tr, c), lambda i: (i, 0))
    shape = jax.ShapeDtypeStruct((r, c), F32)
    return pl.pallas_call(
        body,
        name=name,
        grid=(r // tr,),
        in_specs=[blk] * 4,
        out_specs=[blk] * 3,
        out_shape=[shape] * 3,
        compiler_params=_params(("parallel",)),
    )(w, g, m, v)


WEIGHTS = ("ffn1_w_in", "ffn1_w_out", "ffn2_w_in", "ffn2_w_out", "ln_g", "ln_b", "sb_w_in", "sb_w_out", "swa_w_in", "swa_sinks", "swa_w_out", "dil_w_in", "dil_w_out", "ple_w_proj", "ple_w_gate")
SHARD_AXIS = {"ffn1_w_in": 2, "ffn1_w_out": 1, "ffn2_w_in": 2, "ffn2_w_out": 1, "ln_g": 2, "ln_b": 2, "sb_w_in": 2, "sb_w_out": 1, "swa_w_in": 2, "swa_sinks": None, "swa_w_out": 1, "dil_w_in": 2, "dil_w_out": 1, "ple_w_proj": 2, "ple_w_gate": 1}
SMALL = ("ln_g", "ln_b", "swa_sinks")
SMALL_COLS = 128
SMALL_UNIT = 16 * SMALL_COLS


def _pack_small(pieces, lead):
    flat = jnp.concatenate([a.reshape(lead + (-1,)) for a in pieces], axis=-1)
    n = flat.shape[-1]
    flat = jnp.pad(flat, [(0, 0)] * len(lead) + [(0, -n % SMALL_UNIT)])
    return flat.reshape(lead + (-1, SMALL_COLS))


def _unpack_small(buf, shapes, lead):
    flat = buf.reshape(lead + (-1,))
    out, off = [], 0
    for shp in shapes:
        n = math.prod(shp)
        out.append(flat[..., off:off + n].reshape(lead + tuple(shp)))
        off += n
    return out


def _reduce_group(grads, c, wire, tag):
    keep = [lax.dynamic_slice_in_dim(g, c * (g.shape[1] // 2), g.shape[1] // 2, axis=1) for g in grads]
    give = [lax.dynamic_slice_in_dim(g, (1 - c) * (g.shape[1] // 2), g.shape[1] // 2, axis=1).astype(wire) for g in grads]
    got = _swap_sibling(give, f"{tag}_pair")
    part = [_add2(k, g, wire, f"{tag}_pair_sum{j}") for j, (k, g) in enumerate(zip(keep, got))]
    landed = [_fill_slot(q, lax.dynamic_slice_in_dim(p, _my_slot(), 1, axis=0)) for q, p in zip(_exchange_chips(part, f"{tag}_chips"), part)]
    halves = [_sum_slots(q, f"{tag}_chips_sum{j}") for j, q in enumerate(landed)]
    shared = _share_sibling(halves, f"{tag}_share")
    return [lax.dynamic_update_slice_in_dim(s, h, c * h.shape[0], axis=0) for s, h in zip(shared, halves)]


def _my_slot():
    return 2 * lax.axis_index("x") + lax.axis_index("y")


def _fill_slot(slots, mine):
    return lax.dynamic_update_slice_in_dim(slots, mine, _my_slot(), axis=0)


def _full_from_shards(g, axis):
    g = jnp.moveaxis(g, 0, axis)
    return g.reshape(g.shape[:axis] + (g.shape[axis] * g.shape[axis + 1],) + g.shape[axis + 2:])


def _shards_from_full(a, axis):
    a = a.reshape(a.shape[:axis] + (N_CHIPS, a.shape[axis] // N_CHIPS) + a.shape[axis + 1:])
    return jnp.moveaxis(a, axis, 0)


def _ffn_forward(x, xb, w_in, w_out, g, b, alpha, tag):
    gate, up, act = _ffn_in(xb, w_in, f"{tag}_in")
    out, outb, xhat, rstd = _mm_ln(act, w_out, x, g, b, alpha, 0.5, f"{tag}_out_ln")
    return (out, outb), dict(xb=xb, gate=gate, up=up, act=act, xhat=xhat, rstd=rstd)


def _ffn_backward(dout, saved, w_in, w_out, g, tag):
    dr, dyb, dg, db = _ln_bwd(*dout, saved["xhat"], saved["rstd"], g, 0.5, f"{tag}_ln_bwd")
    dh = _ffn_dact(dyb, w_out, saved["gate"], saved["up"], f"{tag}_dact")
    dw_out = _mm(saved["act"], dyb, "tn", F32, f"{tag}_dw_out")
    dw_in = _mm(saved["xb"], dh, "tn", F32, f"{tag}_dw_in", split_b=True)
    dxb = _mm(dh, w_in, "nt", F32, f"{tag}_dx", split_a=True)
    return dr, dxb, dw_in, dw_out, dg, db


def _sb_forward(xb, w_in, w_out, tag):
    nw = w_out.shape[0]
    h = _mm(xb, w_in, "nn", MXU_DTYPE, f"{tag}_qkv")
    q, k, v = (_to_heads(h[:, j * nw:(j + 1) * nw]) for j in range(3))
    o, ob = _sb_fwd(q, k, v, f"{tag}_att")
    ob = _from_heads(ob)
    return ob, dict(q=q, k=k, v=v, o=o, ob=ob)


def _sb_backward(dmix, saved, xb, w_in, w_out, tag):
    dw_out = _mm(saved["ob"], dmix, "tn", F32, f"{tag}_dw_out")
    do = _to_heads(_mm(dmix, w_out, "nt", MXU_DTYPE, f"{tag}_do"))
    dq, dk, dv = _sb_bwd(saved["q"], saved["k"], saved["v"], saved["o"], do, f"{tag}_att_bwd")
    dh = jnp.concatenate([_from_heads(t) for t in (dq, dk, dv)], axis=1).astype(MXU_DTYPE)
    dw_in = _mm(xb, dh, "tn", F32, f"{tag}_dw_in")
    return _mm(dh, w_in, "nt", F32, f"{tag}_dx"), dw_in, dw_out


def _swa_forward(xb, w_in, sinks, w_out, tables, tag):
    nq = w_out.shape[0]
    nkv = (w_in.shape[1] - nq) // 2
    h = _mm(xb, w_in, "nn", F32, f"{tag}_qkv")
    n_rot = (nq + nkv) // 128
    hb = _rope(h, *tables, lambda j: j // n_rot, MXU_DTYPE, f"{tag}_rope")
    q, k, v = _to_heads(hb[:, :nq]), _to_heads(hb[:, nq:nq + nkv]), _to_heads(hb[:, nq + nkv:])
    o, lse = _band_fwd(q, k, v, sinks, SWA_WINDOW - 1, f"{tag}_att")
    o, ob = _merge_fwd([o], [lse], f"{tag}_cast")
    ob = _from_heads(ob)
    return ob, dict(q=q, k=k, v=v, o=o, lse=lse, ob=ob, n_rot=n_rot)


def _swa_backward(dmix, saved, xb, w_in, sinks, w_out, tables, tag):
    dw_out = _mm(saved["ob"], dmix, "tn", F32, f"{tag}_dw_out")
    do = _to_heads(_mm(dmix, w_out, "nt", MXU_DTYPE, f"{tag}_do"))
    (dog,), (delta,) = _merge_bwd(do, saved["o"], [saved["lse"]], f"{tag}_delta")
    dq, dk, dv, dsink = _band_bwd(saved["q"], saved["k"], saved["v"], dog, saved["lse"], delta, sinks, SWA_WINDOW - 1, f"{tag}_att_bwd")
    dh = jnp.concatenate([_from_heads(t) for t in (dq, dk, dv)], axis=1)
    n_rot = saved["n_rot"]
    dhb = _rope(dh, tables[0], -tables[1], lambda j: j // n_rot, MXU_DTYPE, f"{tag}_rope_bwd")
    dw_in = _mm(xb, dhb, "tn", F32, f"{tag}_dw_in")
    return _mm(dhb, w_in, "nt", F32, f"{tag}_dx"), dw_in, dw_out, dsink[:, 0, 0]


def _dil_forward(xb, w_in, w_out, tables, tag):
    nw = w_out.shape[0]
    h = _mm(xb, w_in, "nn", F32, f"{tag}_qkv")
    hb = _rope(h, *tables, lambda j: (j % 3) // 2, MXU_DTYPE, f"{tag}_rope", width=nw)
    qkv, outs, lses = [], [], []
    for gi, (win, dil) in enumerate(DIL_GROUPS):
        base = gi * 3 * nw
        q, k, v = (_to_heads(hb[:, base + j * nw:base + (j + 1) * nw], dil) for j in range(3))
        o, lse = _band_fwd(q, k, v, None, win // dil, f"{tag}_att{gi}")
        qkv.append((q, k, v))
        outs.append(_unstride(o, dil))
        lses.append(_unstride(lse, dil))
    o, ob = _merge_fwd(outs, lses, f"{tag}_merge")
    ob = _from_heads(ob)
    return ob, dict(qkv=qkv, o=o, lses=lses, ob=ob)


def _dil_backward(dmix, saved, xb, w_in, w_out, tables, tag):
    dw_out = _mm(saved["ob"], dmix, "tn", F32, f"{tag}_dw_out")
    do = _to_heads(_mm(dmix, w_out, "nt", MXU_DTYPE, f"{tag}_do"))
    dogs, deltas = _merge_bwd(do, saved["o"], saved["lses"], f"{tag}_merge_bwd")
    parts = []
    for gi, (win, dil) in enumerate(DIL_GROUPS):
        q, k, v = saved["qkv"][gi]
        dq, dk, dv, _ = _band_bwd(q, k, v, _restride(dogs[gi], dil), _restride(saved["lses"][gi], dil), _restride(deltas[gi], dil), None, win // dil, f"{tag}_att{gi}_bwd")
        parts += [_from_heads(t, dil) for t in (dq, dk, dv)]
    dh = jnp.concatenate(parts, axis=1)
    dhb = _rope(dh, tables[0], -tables[1], lambda j: (j % 3) // 2, MXU_DTYPE, f"{tag}_rope_bwd", width=w_out.shape[0])
    dw_in = _mm(xb, dhb, "tn", F32, f"{tag}_dw_in")
    return _mm(dhb, w_in, "nt", F32, f"{tag}_dx"), dw_in, dw_out


def kernel(x, p, ffn1_w_in, ffn1_w_out, ffn2_w_in, ffn2_w_out, ln_g, ln_b, sb_w_in, sb_w_out, swa_w_in, swa_sinks, swa_w_out, dil_w_in, dil_w_out, ple_w_proj, ple_w_gate, loss_target, m_ffn1_w_in, m_ffn1_w_out, m_ffn2_w_in, m_ffn2_w_out, m_ln_g, m_ln_b, m_sb_w_in, m_sb_w_out, m_swa_w_in, m_swa_sinks, m_swa_w_out, m_dil_w_in, m_dil_w_out, m_ple_w_proj, m_ple_w_gate, v_ffn1_w_in, v_ffn1_w_out, v_ffn2_w_in, v_ffn2_w_out, v_ln_g, v_ln_b, v_sb_w_in, v_sb_w_out, v_swa_w_in, v_swa_sinks, v_swa_w_out, v_dil_w_in, v_dil_w_out, v_ple_w_proj, v_ple_w_gate):
    shard = dict(ffn1_w_in=ffn1_w_in, ffn1_w_out=ffn1_w_out, ffn2_w_in=ffn2_w_in, ffn2_w_out=ffn2_w_out, ln_g=ln_g, ln_b=ln_b, sb_w_in=sb_w_in, sb_w_out=sb_w_out, swa_w_in=swa_w_in, swa_sinks=swa_sinks, swa_w_out=swa_w_out, dil_w_in=dil_w_in, dil_w_out=dil_w_out, ple_w_proj=ple_w_proj, ple_w_gate=ple_w_gate)
    mom_m = dict(ffn1_w_in=m_ffn1_w_in, ffn1_w_out=m_ffn1_w_out, ffn2_w_in=m_ffn2_w_in, ffn2_w_out=m_ffn2_w_out, ln_g=m_ln_g, ln_b=m_ln_b, sb_w_in=m_sb_w_in, sb_w_out=m_sb_w_out, swa_w_in=m_swa_w_in, swa_sinks=m_swa_sinks, swa_w_out=m_swa_w_out, dil_w_in=m_dil_w_in, dil_w_out=m_dil_w_out, ple_w_proj=m_ple_w_proj, ple_w_gate=m_ple_w_gate)
    mom_v = dict(ffn1_w_in=v_ffn1_w_in, ffn1_w_out=v_ffn1_w_out, ffn2_w_in=v_ffn2_w_in, ffn2_w_out=v_ffn2_w_out, ln_g=v_ln_g, ln_b=v_ln_b, sb_w_in=v_sb_w_in, sb_w_out=v_sb_w_out, swa_w_in=v_swa_w_in, swa_sinks=v_swa_sinks, swa_w_out=v_swa_w_out, dil_w_in=v_dil_w_in, dil_w_out=v_dil_w_out, ple_w_proj=v_ple_w_proj, ple_w_gate=v_ple_w_gate)
    depth = ffn1_w_in.shape[0]
    alpha = (2 * depth) ** 0.25
    c = lax.axis_index("c")

    mats = [n for n in WEIGHTS if n not in SMALL]
    local = [shard[n].astype(MXU_DTYPE) for n in mats] + [_pack_small([ln_g, ln_b], ())[None]]
    got = [_fill_slot(g, a[None]) for g, a in zip(_gather_chips(local, "gather_weights"), local)]
    full = {"swa_sinks": swa_sinks}
    for n, g in zip(mats, got):
        full[n] = _full_from_shards(g, SHARD_AXIS[n])
    for n, g in zip(("ln_g", "ln_b"), _unpack_small(got[-1][:, 0], [ln_g.shape, ln_b.shape], (N_CHIPS,))):
        full[n] = _full_from_shards(g, SHARD_AXIS[n])

    seq = x.shape[1]
    tables = _rope_tables(seq)
    xf = x[0]
    xb = xf.astype(MXU_DTYPE)
    saved = []
    for i in range(depth):
        kind, j = i % 3, i // 3
        mixer = ("sb", "swa", "dil")[kind]
        sv = {}
        (x1, x1b), sv["ffn1"] = _ffn_forward(xf, xb, full["ffn1_w_in"][i], full["ffn1_w_out"][i], full["ln_g"][i, 0], full["ln_b"][i, 0], alpha, f"l{i}_ffn1")
        if kind == 0:
            mix, sv["mix"] = _sb_forward(x1b, full["sb_w_in"][j], full["sb_w_out"][j], f"l{i}_sb")
        elif kind == 1:
            mix, sv["mix"] = _swa_forward(x1b, full["swa_w_in"][j], swa_sinks[j], full["swa_w_out"][j], tables, f"l{i}_swa")
        else:
            mix, sv["mix"] = _dil_forward(x1b, full["dil_w_in"][j], full["dil_w_out"][j], tables, f"l{i}_dil")
        x2, x2b, sv["xhat2"], sv["rstd2"] = _mm_ln(mix, full[f"{mixer}_w_out"][j], x1, full["ln_g"][i, 1], full["ln_b"][i, 1], alpha, 1.0, f"l{i}_{mixer}_proj_ln")
        sv["x1b"] = x1b
        (x3, x3b), sv["ffn2"] = _ffn_forward(x2, x2b, full["ffn2_w_in"][i], full["ffn2_w_out"][i], full["ln_g"][i, 2], full["ln_b"][i, 2], alpha, f"l{i}_ffn2")
        xf, xb, sv["u"], sv["e"] = _ple_fwd(x3, x3b, p[i, 0], full["ple_w_gate"][i], full["ple_w_proj"][i], f"l{i}_ple")
        sv["x3b"] = x3b
        saved.append(sv)

    loss_part, dy = _loss(xf, loss_target[0], "loss")
    loss = lax.psum(loss_part[0, 0], ("x", "y", "c"))

    grads = {n: [None] * full[n].shape[0] for n in WEIGHTS if n not in ("ln_g", "ln_b")}
    gsum = {n: [None] * full[n].shape[0] for n in mats}
    dln_g = [[None] * 3 for _ in range(depth)]
    dln_b = [[None] * 3 for _ in range(depth)]
    dout = (dy, None, 1.0)
    for i in reversed(range(depth)):
        kind, j = i % 3, i // 3
        mixer = ("sb", "swa", "dil")[kind]
        sv = saved[i]
        dx4, dub, deb = _ple_bwd(*dout, sv["u"], sv["e"], f"l{i}_ple_bwd")
        grads["ple_w_gate"][i] = _mm(sv["x3b"], dub, "tn", F32, f"l{i}_ple_dw_gate")
        grads["ple_w_proj"][i] = _mm(p[i, 0], deb, "tn", F32, f"l{i}_ple_dw_proj")
        dxb = _mm(dub, full["ple_w_gate"][i], "nt", F32, f"l{i}_ple_dx")
        dr, dxb, grads["ffn2_w_in"][i], grads["ffn2_w_out"][i], dln_g[i][2], dln_b[i][2] = _ffn_backward((dx4, dxb, 1.0), sv["ffn2"], full["ffn2_w_in"][i], full["ffn2_w_out"][i], full["ln_g"][i, 2], f"l{i}_ffn2")
        dr, dmix, dln_g[i][1], dln_b[i][1] = _ln_bwd(dr, dxb, alpha, sv["xhat2"], sv["rstd2"], full["ln_g"][i, 1], 1.0, f"l{i}_mix_ln_bwd")
        if kind == 0:
            dxb, grads["sb_w_in"][j], grads["sb_w_out"][j] = _sb_backward(dmix, sv["mix"], sv["x1b"], full["sb_w_in"][j], full["sb_w_out"][j], f"l{i}_sb")
        elif kind == 1:
            dxb, grads["swa_w_in"][j], grads["swa_w_out"][j], grads["swa_sinks"][j] = _swa_backward(dmix, sv["mix"], sv["x1b"], full["swa_w_in"][j], swa_sinks[j], full["swa_w_out"][j], tables, f"l{i}_swa")
        else:
            dxb, grads["dil_w_in"][j], grads["dil_w_out"][j] = _dil_backward(dmix, sv["mix"], sv["x1b"], full["dil_w_in"][j], full["dil_w_out"][j], tables, f"l{i}_dil")
        dr, dxb, grads["ffn1_w_in"][i], grads["ffn1_w_out"][i], dln_g[i][0], dln_b[i][0] = _ffn_backward((dr, dxb, alpha), sv["ffn1"], full["ffn1_w_in"][i], full["ffn1_w_out"][i], full["ln_g"][i, 0], f"l{i}_ffn1")
        dout = (dr, dxb, alpha)
        layer = [("ffn1_w_in", i), ("ffn1_w_out", i), (f"{mixer}_w_in", j), (f"{mixer}_w_out", j), ("ffn2_w_in", i), ("ffn2_w_out", i), ("ple_w_proj", i), ("ple_w_gate", i)]
        summed = _reduce_group([_shards_from_full(grads[n][k], SHARD_AXIS[n] - 1) for n, k in layer], c, MXU_DTYPE, f"l{i}_reduce")
        for (n, k), g in zip(layer, summed):
            gsum[n][k] = g
    grad_x = _axpy(*dout, "grad_x")[None]

    gshard = {n: jnp.stack(g) for n, g in gsum.items()}
    small = [
        _shards_from_full(jnp.stack([jnp.concatenate(r, axis=0) for r in dln_g]), SHARD_AXIS["ln_g"]),
        _shards_from_full(jnp.stack([jnp.concatenate(r, axis=0) for r in dln_b]), SHARD_AXIS["ln_b"]),
        jnp.broadcast_to(jnp.stack(grads["swa_sinks"])[None], (N_CHIPS,) + swa_sinks.shape),
    ]
    (small_sum,) = _reduce_group([_pack_small(small, (N_CHIPS,))], c, F32, "small_reduce")
    for n, g in zip(SMALL, _unpack_small(small_sum, [shard[n].shape for n in SMALL], ())):
        gshard[n] = g

    delta, new_m, new_v = {}, {}, {}
    for n in WEIGHTS:
        shp = shard[n].shape
        two_d = (-1, shp[-1])
        d, nm, nv = _adamw(shard[n].reshape(two_d), gshard[n].reshape(two_d), mom_m[n].reshape(two_d), mom_v[n].reshape(two_d), f"adamw_{n}")
        delta[n], new_m[n], new_v[n] = d.reshape(shp), nm.reshape(shp), nv.reshape(shp)

    return (loss, grad_x, *[gshard[n] for n in WEIGHTS], *[delta[n] for n in WEIGHTS], *[new_m[n] for n in WEIGHTS], *[new_v[n] for n in WEIGHTS])
```

```python
import functools
import math

import jax
import jax.numpy as jnp
from jax import lax
from jax.experimental import pallas as pl
from jax.experimental.pallas import tpu as pltpu

F32 = jnp.float32
MXU_DTYPE = jnp.bfloat16
MESH = pl.DeviceIdType.MESH

HEAD_DIM = 64
ATT_BLK = 128
SWA_WINDOW = 128
DIL_GROUPS = ((128, 1), (512, 4), (2048, 16))
LN_EPS = 1e-5
ROPE_THETA = 10000.0
NEG_INF = -1e30
ADAM_LR, ADAM_B1, ADAM_B2, ADAM_EPS, ADAM_WD, ADAM_STEP = 0.001, 0.9, 0.999, 1e-08, 0.01, 10

VMEM_LIMIT_BYTES = 56 * 1024 * 1024
N_CHIPS = 4


def _params(sem=None):
    return pltpu.CompilerParams(dimension_semantics=sem, vmem_limit_bytes=VMEM_LIMIT_BYTES)


def _tile(n, target, unit):
    t = (min(target, n) // unit) * unit
    while t >= unit:
        if n % t == 0:
            return t
        t -= unit
    return n


def _dot(a, b, dims):
    return lax.dot_general(a.astype(MXU_DTYPE), b.astype(MXU_DTYPE), (dims, ((), ())), preferred_element_type=F32)


NN = ((1,), (0,))
NT = ((1,), (1,))
TN = ((0,), (0,))


def _accumulate(part, acc_ref, kk, nk, finish):
    if nk == 1:
        finish(part)
        return

    @pl.when(kk == 0)
    def _():
        acc_ref[...] = jnp.zeros_like(acc_ref)

    acc_ref[...] += part

    @pl.when(kk == nk - 1)
    def _():
        finish(acc_ref[...])


def _mm(a, b, mode, out_dtype, name, split_a=False, split_b=False, exchange=None, tm=1408, tn=1408, tk=1408):
    dims = {"nn": NN, "nt": NT, "tn": TN}[mode]
    if split_a:
        m, k = a.shape[1], 2 * a.shape[2]
    elif mode == "tn":
        k, m = a.shape
    else:
        m, k = a.shape
    if split_b:
        n = 2 * b.shape[2]
    elif mode == "nt":
        n = b.shape[0]
    else:
        n = b.shape[1]
    tm = _tile(m, tm, 128)
    tn = _tile(n // 2 if split_b else n, tn, 128)
    tk = _tile(k // 2 if split_a else k, tk, 128)
    nk = k // tk
    nk_half = nk // 2
    nn_half = (n // tn) // 2

    if split_a:
        a_spec = pl.BlockSpec((None, tm, tk), lambda i, j, kk: (kk // nk_half, i, kk % nk_half))
    elif mode == "tn":
        a_spec = pl.BlockSpec((tk, tm), lambda i, j, kk: (kk, i))
    else:
        a_spec = pl.BlockSpec((tm, tk), lambda i, j, kk: (i, kk))
    if split_b:
        b_spec = pl.BlockSpec((None, tk, tn), lambda i, j, kk: (j // nn_half, kk, j % nn_half))
    elif mode == "nt":
        b_spec = pl.BlockSpec((tn, tk), lambda i, j, kk: (j, kk))
    else:
        b_spec = pl.BlockSpec((tk, tn), lambda i, j, kk: (kk, j))

    parts = [] if exchange is None else list(exchange)
    n_ex = len(parts)
    grid = (m // tm, n // tn, nk)

    def body(*refs):
        a_ref, b_ref = refs[:2]
        p_refs, o_ref, q_refs, acc_ref = refs[2:2 + n_ex], refs[2 + n_ex], refs[3 + n_ex:3 + 2 * n_ex], refs[3 + 2 * n_ex]
        if n_ex:
            step = (pl.program_id(0) * grid[1] + pl.program_id(1)) * nk + pl.program_id(2)

            @pl.when(step == 0)
            def _():
                for cp in _chips_copies(p_refs, q_refs, *refs[4 + 2 * n_ex:])[0]:
                    cp.start()

        def finish(total):
            o_ref[...] = total.astype(o_ref.dtype)

        _accumulate(_dot(a_ref[...], b_ref[...], dims), acc_ref, pl.program_id(2), nk, finish)

        if n_ex:

            @pl.when(step == grid[0] * grid[1] * nk - 1)
            def _():
                _chips_finish(*_chips_copies(p_refs, q_refs, *refs[4 + 2 * n_ex:]))

    out = pl.pallas_call(
        body,
        name=name,
        grid=grid,
        in_specs=[a_spec, b_spec] + [ANY] * n_ex,
        out_specs=[pl.BlockSpec((tm, tn), lambda i, j, kk: (i, j))] + [ANY] * n_ex,
        out_shape=[jax.ShapeDtypeStruct((m, n), out_dtype)] + [jax.ShapeDtypeStruct(p.shape, p.dtype) for p in parts],
        scratch_shapes=[pltpu.VMEM((tm, tn), F32)] + [pltpu.SemaphoreType.DMA((3 * n_ex,))] * (2 if n_ex else 0),
        compiler_params=_params(("arbitrary", "arbitrary", "arbitrary") if n_ex else ("parallel", "parallel", "arbitrary")),
    )(a, b, *parts)
    return (out[0], out[1:]) if n_ex else out[0]


def _sigmoid(x):
    return 1.0 / (1.0 + jnp.exp(-x))


def _ffn_in(xb, w_in, name):
    s, d = xb.shape
    f = w_in.shape[1] // 2
    tm = _tile(s, 512, 128)
    tn = _tile(f, 1408, 128)
    nj = f // tn

    def body(x_ref, wg_ref, wu_ref, g_ref, u_ref, a_ref):
        x = x_ref[...]
        g = _dot(x, wg_ref[...], NN)
        u = _dot(x, wu_ref[...], NN)
        g_ref[...] = g.astype(g_ref.dtype)
        u_ref[...] = u.astype(u_ref.dtype)
        a_ref[...] = (g * _sigmoid(g) * u).astype(a_ref.dtype)

    out = pl.BlockSpec((tm, tn), lambda i, j: (i, j))
    return pl.pallas_call(
        body,
        name=name,
        grid=(s // tm, nj),
        in_specs=[
            pl.BlockSpec((tm, d), lambda i, j: (i, 0)),
            pl.BlockSpec((d, tn), lambda i, j: (0, j)),
            pl.BlockSpec((d, tn), lambda i, j: (0, j + nj)),
        ],
        out_specs=[out, out, out],
        out_shape=[
            jax.ShapeDtypeStruct((s, f), MXU_DTYPE),
            jax.ShapeDtypeStruct((s, f), MXU_DTYPE),
            jax.ShapeDtypeStruct((s, f), MXU_DTYPE),
        ],
        compiler_params=_params(("parallel", "parallel")),
    )(xb, w_in, w_in)


def _ffn_dact(dyb, w_out, gate, up, name):
    s, d = dyb.shape
    f = w_out.shape[0]
    tm = _tile(s, 512, 128)
    tn = _tile(f, 1408, 128)

    def body(dy_ref, w_ref, g_ref, u_ref, o_ref):
        dact = _dot(dy_ref[...], w_ref[...], NT)
        g = g_ref[...].astype(F32)
        sig = _sigmoid(g)
        o_ref[0] = (dact * u_ref[...].astype(F32) * (sig * (1.0 + g * (1.0 - sig)))).astype(o_ref.dtype)
        o_ref[1] = (dact * (g * sig)).astype(o_ref.dtype)

    tile = pl.BlockSpec((tm, tn), lambda i, j: (i, j))
    return pl.pallas_call(
        body,
        name=name,
        grid=(s // tm, f // tn),
        in_specs=[
            pl.BlockSpec((tm, d), lambda i, j: (i, 0)),
            pl.BlockSpec((tn, d), lambda i, j: (j, 0)),
            tile,
            tile,
        ],
        out_specs=pl.BlockSpec((2, tm, tn), lambda i, j: (0, i, j)),
        out_shape=jax.ShapeDtypeStruct((2, s, f), MXU_DTYPE),
        compiler_params=_params(("parallel", "parallel")),
    )(dyb, w_out, gate, up)


def _ple_fwd(x, xb, p, w_gate, w_proj, name):
    s, d = x.shape
    pd = p.shape[1]
    tm = _tile(s, 1024, 128)
    tn = _tile(d, 512, 128)

    def body(x_ref, xb_ref, p_ref, wg_ref, wp_ref, o_ref, ob_ref, u_ref, e_ref):
        u = _dot(xb_ref[...], wg_ref[...], NN)
        e = _dot(p_ref[...], wp_ref[...], NN)
        out = x_ref[...] + _sigmoid(u) * e
        o_ref[...] = out
        ob_ref[...] = out.astype(ob_ref.dtype)
        u_ref[...] = u
        e_ref[...] = e

    tile = pl.BlockSpec((tm, tn), lambda i, j: (i, j))
    return pl.pallas_call(
        body,
        name=name,
        grid=(s // tm, d // tn),
        in_specs=[
            tile,
            pl.BlockSpec((tm, d), lambda i, j: (i, 0)),
            pl.BlockSpec((tm, pd), lambda i, j: (i, 0)),
            pl.BlockSpec((d, tn), lambda i, j: (0, j)),
            pl.BlockSpec((pd, tn), lambda i, j: (0, j)),
        ],
        out_specs=[tile, tile, tile, tile],
        out_shape=[
            jax.ShapeDtypeStruct((s, d), F32),
            jax.ShapeDtypeStruct((s, d), MXU_DTYPE),
            jax.ShapeDtypeStruct((s, d), F32),
            jax.ShapeDtypeStruct((s, d), F32),
        ],
        compiler_params=_params(("parallel", "parallel")),
    )(x, xb, p, w_gate, w_proj)


def _rows_spec(ts, d):
    return pl.BlockSpec((ts, d), lambda i: (i, 0))


def _mm_ln(a, w, x, g, b, alpha, beta, name):
    s, k = a.shape
    d = w.shape[1]
    tm = _tile(s, 512, 128)
    tk = _tile(k, 1408, 128)
    nk = k // tk

    def body(a_ref, w_ref, x_ref, g_ref, b_ref, o_ref, ob_ref, xh_ref, rs_ref, acc_ref):
        def finish(y):
            r = alpha * x_ref[...] + beta * y
            mu = jnp.mean(r, axis=1, keepdims=True)
            cen = r - mu
            var = jnp.mean(cen * cen, axis=1, keepdims=True)
            rstd = lax.rsqrt(var + LN_EPS)
            xhat = cen * rstd
            out = xhat * g_ref[...] + b_ref[...]
            o_ref[...] = out
            ob_ref[...] = out.astype(ob_ref.dtype)
            xh_ref[...] = xhat
            rs_ref[...] = rstd

        _accumulate(_dot(a_ref[...], w_ref[...], NN), acc_ref, pl.program_id(1), nk, finish)

    rows = pl.BlockSpec((tm, d), lambda i, kk: (i, 0))
    vec = pl.BlockSpec((1, d), lambda i, kk: (0, 0))
    return pl.pallas_call(
        body,
        name=name,
        grid=(s // tm, nk),
        in_specs=[pl.BlockSpec((tm, tk), lambda i, kk: (i, kk)), pl.BlockSpec((tk, d), lambda i, kk: (kk, 0)), rows, vec, vec],
        out_specs=[rows, rows, rows, pl.BlockSpec((tm, 1), lambda i, kk: (i, 0))],
        out_shape=[
            jax.ShapeDtypeStruct((s, d), F32),
            jax.ShapeDtypeStruct((s, d), MXU_DTYPE),
            jax.ShapeDtypeStruct((s, d), F32),
            jax.ShapeDtypeStruct((s, 1), F32),
        ],
        scratch_shapes=[pltpu.VMEM((tm, d), F32)],
        compiler_params=_params(("parallel", "arbitrary")),
    )(a, w, x, g.reshape(1, d), b.reshape(1, d))


def _ln_bwd(ga, gb, ca, xhat, rstd, g, beta, name):
    s, d = xhat.shape
    ts = _tile(s, 512, 8)

    def body(ga_ref, gb_ref, xh_ref, rs_ref, g_ref, dr_ref, dyb_ref, dg_ref, db_ref):
        @pl.when(pl.program_id(0) == 0)
        def _():
            dg_ref[...] = jnp.zeros_like(dg_ref)
            db_ref[...] = jnp.zeros_like(db_ref)

        dout = ca * ga_ref[...] + gb_ref[...]
        xhat = xh_ref[...]
        dg_ref[...] += jnp.sum(dout * xhat, axis=0, keepdims=True)
        db_ref[...] += jnp.sum(dout, axis=0, keepdims=True)
        dxh = dout * g_ref[...]
        m1 = jnp.mean(dxh, axis=1, keepdims=True)
        m2 = jnp.mean(dxh * xhat, axis=1, keepdims=True)
        dr = rs_ref[...] * (dxh - m1 - xhat * m2)
        dr_ref[...] = dr
        dyb_ref[...] = (beta * dr).astype(dyb_ref.dtype)

    vec = pl.BlockSpec((1, d), lambda i: (0, 0))
    return pl.pallas_call(
        body,
        name=name,
        grid=(s // ts,),
        in_specs=[_rows_spec(ts, d), _rows_spec(ts, d), _rows_spec(ts, d), _rows_spec(ts, 1), vec],
        out_specs=[_rows_spec(ts, d), _rows_spec(ts, d), vec, vec],
        out_shape=[
            jax.ShapeDtypeStruct((s, d), F32),
            jax.ShapeDtypeStruct((s, d), MXU_DTYPE),
            jax.ShapeDtypeStruct((1, d), F32),
            jax.ShapeDtypeStruct((1, d), F32),
        ],
        compiler_params=_params(("arbitrary",)),
    )(ga, gb, xhat, rstd, g.reshape(1, d))


def _ple_bwd(ga, gb, ca, u, e, name):
    s, d = u.shape
    ts = _tile(s, 512, 8)
    grads = [ga] if gb is None else [ga, gb]

    def body(*refs):
        u_ref, e_ref, dx_ref, du_ref, de_ref = refs[len(grads):]
        dx = ca * refs[0][...]
        if gb is not None:
            dx = dx + refs[1][...]
        sig = _sigmoid(u_ref[...])
        dx_ref[...] = dx
        du_ref[...] = (dx * e_ref[...] * sig * (1.0 - sig)).astype(du_ref.dtype)
        de_ref[...] = (dx * sig).astype(de_ref.dtype)

    return pl.pallas_call(
        body,
        name=name,
        grid=(s // ts,),
        in_specs=[_rows_spec(ts, d)] * (len(grads) + 2),
        out_specs=[_rows_spec(ts, d)] * 3,
        out_shape=[
            jax.ShapeDtypeStruct((s, d), F32),
            jax.ShapeDtypeStruct((s, d), MXU_DTYPE),
            jax.ShapeDtypeStruct((s, d), MXU_DTYPE),
        ],
        compiler_params=_params(("parallel",)),
    )(*grads, u, e)


def _axpy(ga, gb, ca, name):
    s, d = ga.shape
    ts = _tile(s, 512, 8)

    def body(ga_ref, gb_ref, o_ref):
        o_ref[...] = ca * ga_ref[...] + gb_ref[...]

    return pl.pallas_call(
        body,
        name=name,
        grid=(s // ts,),
        in_specs=[_rows_spec(ts, d)] * 2,
        out_specs=_rows_spec(ts, d),
        out_shape=jax.ShapeDtypeStruct((s, d), F32),
        compiler_params=_params(("parallel",)),
    )(ga, gb)


def _loss(y, target, name):
    s, d = y.shape
    ts = _tile(s, 512, 8)

    def body(y_ref, t_ref, l_ref, dy_ref):
        @pl.when(pl.program_id(0) == 0)
        def _():
            l_ref[...] = jnp.zeros_like(l_ref)

        err = y_ref[...] - t_ref[...]
        l_ref[...] += (0.5 / d) * jnp.sum(jnp.sum(err * err, axis=1, keepdims=True), axis=0, keepdims=True)
        dy_ref[...] = err * (1.0 / d)

    return pl.pallas_call(
        body,
        name=name,
        grid=(s // ts,),
        in_specs=[_rows_spec(ts, d)] * 2,
        out_specs=[pl.BlockSpec((1, 1), lambda i: (0, 0)), _rows_spec(ts, d)],
        out_shape=[jax.ShapeDtypeStruct((1, 1), F32), jax.ShapeDtypeStruct((s, d), F32)],
        compiler_params=_params(("arbitrary",)),
    )(y, target)


def _rope_tables(seq):
    pos = jnp.arange(seq, dtype=F32)
    inv = ROPE_THETA ** (-jnp.arange(0, HEAD_DIM, 2, dtype=F32) / HEAD_DIM)
    ang = pos[:, None] * inv[None, :]
    cos, sin = jnp.cos(ang), jnp.sin(ang)
    cos2 = jnp.concatenate([cos, cos, cos, cos, jnp.ones((seq, 128), F32)], axis=1)
    sin2 = jnp.concatenate([-sin, sin, -sin, sin, jnp.zeros((seq, 128), F32)], axis=1)
    return cos2, sin2


def _rope(h, cos2, sin2, plain_block, out_dtype, name, width=128):
    s, n = h.shape
    ts = _tile(s, 512, 8)

    def body(h_ref, c_ref, s_ref, o_ref):
        cos, sin = c_ref[...], s_ref[...]
        lane = lax.broadcasted_iota(jnp.int32, cos.shape, 1)
        first_half = lane % HEAD_DIM < HEAD_DIM // 2
        for w in range(width // 128):
            cols = slice(w * 128, (w + 1) * 128)
            x = h_ref[:, cols].astype(F32)
            partner = jnp.where(first_half, pltpu.roll(x, 128 - HEAD_DIM // 2, 1), pltpu.roll(x, HEAD_DIM // 2, 1))
            o_ref[:, cols] = (x * cos + partner * sin).astype(o_ref.dtype)

    tile = pl.BlockSpec((ts, width), lambda i, j: (i, j))
    table = pl.BlockSpec((ts, 128), lambda i, j: (i, plain_block(j)))
    return pl.pallas_call(
        body,
        name=name,
        grid=(s // ts, n // width),
        in_specs=[tile, table, table],
        out_specs=tile,
        out_shape=jax.ShapeDtypeStruct((s, n), out_dtype),
        compiler_params=_params(("parallel", "parallel")),
    )(h, cos2, sin2)


SB_TQ = 512
SB_BLK = 128
SB_WALK = 2
SB_CLOSED = 110.0


def _sb_walk(n_trips, carry, key_blocks):
    def still_open(carry):
        lowest = functools.reduce(jnp.minimum, [c[0] for c in carry])
        return jnp.min(lowest) < SB_CLOSED

    def cond(state):
        t, go, _ = state
        return jnp.logical_and(t < n_trips, go)

    def body(state):
        t, _, carry = state
        carry = key_blocks(t, carry)
        return t + 1, still_open(carry), carry

    return lax.while_loop(cond, body, (jnp.int32(0), still_open(carry), carry))[2]


def _tri2(strict):
    row = lax.broadcasted_iota(jnp.int32, (2 * SB_BLK, SB_BLK), 0) % SB_BLK
    col = lax.broadcasted_iota(jnp.int32, (2 * SB_BLK, SB_BLK), 1)
    return (row > col if strict else row >= col).astype(MXU_DTYPE)


def _cumsum_dot(x, tri2):
    hi = x.astype(MXU_DTYPE)
    lo = x - hi.astype(F32)
    return _dot(jnp.concatenate([hi, lo.astype(MXU_DTYPE)], axis=1), tri2, NN)


def _sb_logits(z, valid):
    l1p = jnp.log(1.0 + jnp.exp(-jnp.abs(z)))
    sp = jnp.maximum(z, 0.0) + l1p
    ls = z - sp
    if valid is not None:
        sp = jnp.where(valid, sp, 0.0)
    return sp, ls


def _sb_weights(ls, after, valid):
    a = jnp.exp(ls - after)
    return a if valid is None else jnp.where(valid, a, 0.0)


def _sb_setup(q_ref, k_ref, v_ref, n_sub, scale):
    qs = [q_ref[0, u * SB_BLK:(u + 1) * SB_BLK, :] * scale for u in range(n_sub)]
    row = lax.broadcasted_iota(jnp.int32, (SB_BLK, SB_BLK), 0)
    col = lax.broadcasted_iota(jnp.int32, (SB_BLK, SB_BLK), 1)

    def load(jj):
        start = pl.multiple_of(jj * SB_BLK, SB_BLK)
        return start, k_ref[0, pl.ds(start, SB_BLK), :], v_ref[0, pl.ds(start, SB_BLK), :]

    return qs, col < row, load


def _sb_fwd(q, k, v, name):
    nh, s, dh = q.shape
    tq = min(SB_TQ, s)
    n_sub = tq // SB_BLK
    scale = dh ** -0.5

    def body(q_ref, k_ref, v_ref, o_ref, ob_ref):
        base = pl.program_id(1) * n_sub
        qs, diag_valid, load = _sb_setup(q_ref, k_ref, v_ref, n_sub, scale)
        tri_after = _tri2(True)

        def key_blocks(first, carry, diagonal):
            blocks = list(range((n_sub if diagonal else SB_WALK) - 1, -1, -1))
            kv = {d: load(first + d) for d in blocks}
            tiles = [(d, u) for d in blocks for u in range(d if diagonal else 0, n_sub)]
            valid = {t: diag_valid if diagonal and t[0] == t[1] else None for t in tiles}
            z = {t: _dot(qs[t[1]], kv[t[0]][1], NT) for t in tiles}
            sp_ls = {t: _sb_logits(z[t], valid[t]) for t in tiles}
            inside = {t: _cumsum_dot(sp_ls[t][0], tri_after) for t in tiles}
            carry = list(carry)
            for t in tiles:
                after_c, acc = carry[t[1]]
                sp, ls = sp_ls[t]
                a = _sb_weights(ls, after_c + inside[t], valid[t])
                carry[t[1]] = (after_c + jnp.sum(sp, axis=1, keepdims=True), acc + _dot(a, kv[t[0]][2], NN))
            return tuple(carry)

        carry = tuple((jnp.zeros((SB_BLK, 1), F32), jnp.zeros((SB_BLK, dh), F32)) for _ in range(n_sub))
        carry = key_blocks(base, carry, True)
        carry = _sb_walk(base // SB_WALK, carry, lambda t, c: key_blocks(base - SB_WALK * (t + 1), c, False))
        for u in range(n_sub):
            rows = slice(u * SB_BLK, (u + 1) * SB_BLK)
            o_ref[0, rows, :] = carry[u][1]
            ob_ref[0, rows, :] = carry[u][1].astype(ob_ref.dtype)

    blk = pl.BlockSpec((1, tq, dh), lambda h, i: (h, i, 0))
    full = pl.BlockSpec((1, s, dh), lambda h, i: (h, 0, 0))
    return pl.pallas_call(
        body,
        name=name,
        grid=(nh, s // tq),
        in_specs=[blk, full, full],
        out_specs=[blk, blk],
        out_shape=[jax.ShapeDtypeStruct((nh, s, dh), F32), jax.ShapeDtypeStruct((nh, s, dh), MXU_DTYPE)],
        compiler_params=_params(("parallel", "parallel")),
    )(q, k, v)


def _sb_bwd(q, k, v, o, do, name):
    nh, s, dh = q.shape
    tq = min(SB_TQ, s)
    n_sub = tq // SB_BLK
    scale = dh ** -0.5

    def body(q_ref, k_ref, v_ref, o_ref, do_ref, dq_ref, dk_ref, dv_ref):
        i = pl.program_id(1)
        base = i * n_sub

        @pl.when(i == 0)
        def _():
            dk_ref[...] = jnp.zeros_like(dk_ref)
            dv_ref[...] = jnp.zeros_like(dv_ref)

        qs, diag_valid, load = _sb_setup(q_ref, k_ref, v_ref, n_sub, scale)
        dob = [do_ref[0, u * SB_BLK:(u + 1) * SB_BLK, :] for u in range(n_sub)]
        total = [jnp.sum(dob[u].astype(F32) * o_ref[0, u * SB_BLK:(u + 1) * SB_BLK, :], axis=1, keepdims=True) for u in range(n_sub)]
        tri_after = _tri2(True)
        tri_from = _tri2(False)

        def key_blocks(first, carry, diagonal):
            blocks = list(range((n_sub if diagonal else SB_WALK) - 1, -1, -1))
            kv = {d: load(first + d) for d in blocks}
            tiles = [(d, u) for d in blocks for u in range(d if diagonal else 0, n_sub)]
            valid = {t: diag_valid if diagonal and t[0] == t[1] else None for t in tiles}
            z = {t: _dot(qs[t[1]], kv[t[0]][1], NT) for t in tiles}
            da = {t: _dot(dob[t[1]], kv[t[0]][2], NT) for t in tiles}
            sp_ls = {t: _sb_logits(z[t], valid[t]) for t in tiles}
            inside = {t: _cumsum_dot(sp_ls[t][0], tri_after) for t in tiles}
            after_run = [c[0] for c in carry]
            ab, dl = {}, {}
            for t in tiles:
                sp, ls = sp_ls[t]
                ab[t] = _sb_weights(ls, after_run[t[1]] + inside[t], valid[t]).astype(MXU_DTYPE)
                dl[t] = ab[t].astype(F32) * da[t]
                after_run[t[1]] = after_run[t[1]] + jnp.sum(sp, axis=1, keepdims=True)
            from_in = {t: _cumsum_dot(dl[t], tri_from) for t in tiles}
            from_run = [c[1] for c in carry]
            dq = [c[2] for c in carry]
            for d in blocks:
                start, kb, _ = kv[d]
                dk = dv = None
                for u in range(d if diagonal else 0, n_sub):
                    t = (d, u)
                    dz = dl[t] - jnp.exp(sp_ls[t][1]) * (dl[t] + total[u] - (from_run[u] + from_in[t]))
                    if valid[t] is not None:
                        dz = jnp.where(valid[t], dz, 0.0)
                    dzb = dz.astype(MXU_DTYPE)
                    from_run[u] = from_run[u] + jnp.sum(dl[t], axis=1, keepdims=True)
                    dq[u] = dq[u] + _dot(dzb, kb, NN)
                    dk_u, dv_u = _dot(dzb, qs[u], TN), _dot(ab[t], dob[u], TN)
                    dk = dk_u if dk is None else dk + dk_u
                    dv = dv_u if dv is None else dv + dv_u
                dk_ref[0, pl.ds(start, SB_BLK), :] += dk
                dv_ref[0, pl.ds(start, SB_BLK), :] += dv
            return tuple(zip(after_run, from_run, dq))

        carry = tuple((jnp.zeros((SB_BLK, 1), F32), jnp.zeros((SB_BLK, 1), F32), jnp.zeros((SB_BLK, dh), F32)) for _ in range(n_sub))
        carry = key_blocks(base, carry, True)
        carry = _sb_walk(base // SB_WALK, carry, lambda t, c: key_blocks(base - SB_WALK * (t + 1), c, False))
        for u in range(n_sub):
            dq_ref[0, u * SB_BLK:(u + 1) * SB_BLK, :] = carry[u][2] * scale

    blk = pl.BlockSpec((1, tq, dh), lambda h, i: (h, i, 0))
    full = pl.BlockSpec((1, s, dh), lambda h, i: (h, 0, 0))
    shape = jax.ShapeDtypeStruct((nh, s, dh), F32)
    return pl.pallas_call(
        body,
        name=name,
        grid=(nh, s // tq),
        in_specs=[blk, full, full, blk, blk],
        out_specs=[blk, full, full],
        out_shape=[shape, shape, shape],
        compiler_params=_params(("parallel", "arbitrary")),
    )(q, k, v, o, do)


BAND_TQ = 1024


def _band_scores(q_ref, k_ref, i, sub, tq, length, max_dist, scale):
    t0 = i * tq + sub * ATT_BLK
    ks = pl.multiple_of(jnp.minimum(jnp.maximum(t0 - ATT_BLK, 0), length - 2 * ATT_BLK), ATT_BLK)
    qs = q_ref[0, sub * ATT_BLK:(sub + 1) * ATT_BLK, :] * scale
    kw = k_ref[0, pl.ds(ks, 2 * ATT_BLK), :]
    sc = _dot(qs, kw, NT)
    diff = (t0 + lax.broadcasted_iota(jnp.int32, sc.shape, 0)) - (ks + lax.broadcasted_iota(jnp.int32, sc.shape, 1))
    valid = (diff >= 0) & (diff <= max_dist)
    return ks, qs, kw, jnp.where(valid, sc, NEG_INF)


def _band_fwd(q, k, v, sinks, max_dist, name):
    bq, length, dh = q.shape
    group = bq // k.shape[0]
    tq = min(BAND_TQ, length)
    scale = dh ** -0.5
    n_sink = 0 if sinks is None else sinks.shape[0]

    def body(*refs):
        if n_sink:
            sink_ref, q_ref, k_ref, v_ref, o_ref, lse_ref = refs
            sink = sink_ref[pl.program_id(0) % n_sink]
        else:
            q_ref, k_ref, v_ref, o_ref, lse_ref = refs
        i = pl.program_id(1)
        scores = [_band_scores(q_ref, k_ref, i, sub, tq, length, max_dist, scale) for sub in range(tq // ATT_BLK)]
        for sub, (ks, _, _, sc) in enumerate(scores):
            m = jnp.max(sc, axis=1, keepdims=True)
            if n_sink:
                m = jnp.maximum(m, sink)
            e = jnp.exp(sc - m)
            den = jnp.sum(e, axis=1, keepdims=True)
            if n_sink:
                den = den + jnp.exp(sink - m)
            rows = slice(sub * ATT_BLK, (sub + 1) * ATT_BLK)
            o_ref[0, rows, :] = _dot(e / den, v_ref[0, pl.ds(ks, 2 * ATT_BLK), :], NN)
            lse_ref[0, rows, :] = m + jnp.log(den)

    qblk = pl.BlockSpec((1, tq, dh), lambda b, i: (b, i, 0))
    kfull = pl.BlockSpec((1, length, dh), lambda b, i: (b // group, 0, 0))
    in_specs = [qblk, kfull, kfull]
    args = [q, k, v]
    if n_sink:
        in_specs = [pl.BlockSpec(memory_space=pltpu.SMEM)] + in_specs
        args = [sinks] + args
    return pl.pallas_call(
        body,
        name=name,
        grid=(bq, length // tq),
        in_specs=in_specs,
        out_specs=[qblk, pl.BlockSpec((1, tq, 1), lambda b, i: (b, i, 0))],
        out_shape=[jax.ShapeDtypeStruct((bq, length, dh), F32), jax.ShapeDtypeStruct((bq, length, 1), F32)],
        compiler_params=_params(("parallel", "parallel")),
    )(*args)


def _band_bwd(q, k, v, do, lse, delta, sinks, max_dist, name):
    bq, length, dh = q.shape
    group = bq // k.shape[0]
    tq = min(BAND_TQ, length)
    scale = dh ** -0.5
    n_sink = 0 if sinks is None else sinks.shape[0]

    def body(*refs):
        if n_sink:
            sink_ref, q_ref, k_ref, v_ref, do_ref, lse_ref, dl_ref, dq_ref, dk_ref, dv_ref, ds_ref = refs
            sink = sink_ref[pl.program_id(0) % n_sink]
        else:
            q_ref, k_ref, v_ref, do_ref, lse_ref, dl_ref, dq_ref, dk_ref, dv_ref, ds_ref = refs
        i = pl.program_id(1)

        @pl.when((i == 0) & (pl.program_id(0) % group == 0))
        def _():
            dk_ref[...] = jnp.zeros_like(dk_ref)
            dv_ref[...] = jnp.zeros_like(dv_ref)

        @pl.when(i == 0)
        def _():
            ds_ref[...] = jnp.zeros_like(ds_ref)

        scores = [_band_scores(q_ref, k_ref, i, sub, tq, length, max_dist, scale) for sub in range(tq // ATT_BLK)]
        dps = [_dot(do_ref[0, sub * ATT_BLK:(sub + 1) * ATT_BLK, :], v_ref[0, pl.ds(ks, 2 * ATT_BLK), :], NT) for sub, (ks, _, _, _) in enumerate(scores)]
        for sub, (ks, qs, kw, sc) in enumerate(scores):
            rows = slice(sub * ATT_BLK, (sub + 1) * ATT_BLK)
            lse = lse_ref[0, rows, :]
            delta_r = dl_ref[0, rows, :]
            dob = do_ref[0, rows, :]
            p = jnp.exp(sc - lse)
            dsb = (p * (dps[sub] - delta_r)).astype(MXU_DTYPE)
            dq_ref[0, rows, :] = _dot(dsb, kw, NN) * scale
            dk_ref[0, pl.ds(ks, 2 * ATT_BLK), :] += _dot(dsb, qs, TN)
            dv_ref[0, pl.ds(ks, 2 * ATT_BLK), :] += _dot(p, dob, TN)
            if n_sink:
                ds_ref[...] += jnp.sum(-jnp.exp(sink - lse) * delta_r, axis=0, keepdims=True)

    qblk = pl.BlockSpec((1, tq, dh), lambda b, i: (b, i, 0))
    qcol = pl.BlockSpec((1, tq, 1), lambda b, i: (b, i, 0))
    kfull = pl.BlockSpec((1, length, dh), lambda b, i: (b // group, 0, 0))
    in_specs = [qblk, kfull, kfull, qblk, qcol, qcol]
    args = [q, k, v, do, lse, delta]
    if n_sink:
        in_specs = [pl.BlockSpec(memory_space=pltpu.SMEM)] + in_specs
        args = [sinks] + args
    kshape = jax.ShapeDtypeStruct((k.shape[0], length, dh), F32)
    return pl.pallas_call(
        body,
        name=name,
        grid=(bq, length // tq),
        in_specs=in_specs,
        out_specs=[qblk, kfull, kfull, pl.BlockSpec((1, 8, 128), lambda b, i: (b, 0, 0))],
        out_shape=[jax.ShapeDtypeStruct((bq, length, dh), F32), kshape, kshape, jax.ShapeDtypeStruct((bq, 8, 128), F32)],
        compiler_params=_params(("arbitrary", "arbitrary")),
    )(*args)


def _merge_weights(lse_refs):
    lses = [r[0] for r in lse_refs]
    m = functools.reduce(jnp.maximum, lses)
    es = [jnp.exp(l - m) for l in lses]
    den = functools.reduce(lambda a, b: a + b, es)
    return [e / den for e in es]


def _merge_fwd(outs, lses, name):
    n = len(outs)
    nh, s, dh = outs[0].shape
    ts = _tile(s, 1024, 8)

    def body(*refs):
        ws = _merge_weights(refs[n:2 * n])
        o = functools.reduce(lambda a, b: a + b, [w * r[0] for w, r in zip(ws, refs[:n])])
        refs[2 * n][0] = o
        refs[2 * n + 1][0] = o.astype(MXU_DTYPE)

    blk = pl.BlockSpec((1, ts, dh), lambda h, i: (h, i, 0))
    col = pl.BlockSpec((1, ts, 1), lambda h, i: (h, i, 0))
    return pl.pallas_call(
        body,
        name=name,
        grid=(nh, s // ts),
        in_specs=[blk] * n + [col] * n,
        out_specs=[blk, blk],
        out_shape=[jax.ShapeDtypeStruct((nh, s, dh), F32), jax.ShapeDtypeStruct((nh, s, dh), MXU_DTYPE)],
        compiler_params=_params(("parallel", "parallel")),
    )(*outs, *lses)


def _merge_bwd(do, o, lses, name):
    n = len(lses)
    nh, s, dh = o.shape
    ts = _tile(s, 1024, 8)

    def body(*refs):
        do_ref, o_ref = refs[:2]
        ws = _merge_weights(refs[2:2 + n])
        dof = do_ref[0].astype(F32)
        base = jnp.sum(dof * o_ref[0], axis=1, keepdims=True)
        for g in range(n):
            refs[2 + n + g][0] = (ws[g] * dof).astype(MXU_DTYPE)
            refs[2 + 2 * n + g][0] = ws[g] * base

    blk = pl.BlockSpec((1, ts, dh), lambda h, i: (h, i, 0))
    col = pl.BlockSpec((1, ts, 1), lambda h, i: (h, i, 0))
    res = pl.pallas_call(
        body,
        name=name,
        grid=(nh, s // ts),
        in_specs=[blk, blk] + [col] * n,
        out_specs=[blk] * n + [col] * n,
        out_shape=[jax.ShapeDtypeStruct((nh, s, dh), MXU_DTYPE)] * n + [jax.ShapeDtypeStruct((nh, s, 1), F32)] * n,
        compiler_params=_params(("parallel", "parallel")),
    )(do, o, *lses)
    return res[:n], res[n:]


def _to_heads(a, dil=1):
    s, n = a.shape
    nh = n // HEAD_DIM
    a = a.reshape(s // dil, dil, nh, HEAD_DIM).transpose(1, 2, 0, 3)
    return a.reshape(dil * nh, s // dil, HEAD_DIM)


def _from_heads(a, dil=1):
    b, ls, c = a.shape
    nh = b // dil
    return a.reshape(dil, nh, ls, c).transpose(2, 0, 1, 3).reshape(ls * dil, nh * c)


def _restride(a, dil):
    nh, s, c = a.shape
    return a.reshape(nh, s // dil, dil, c).transpose(2, 0, 1, 3).reshape(dil * nh, s // dil, c)


def _unstride(a, dil):
    b, ls, c = a.shape
    nh = b // dil
    return a.reshape(dil, nh, ls, c).transpose(1, 2, 0, 3).reshape(nh, ls * dil, c)


ANY = pl.BlockSpec(memory_space=pl.ANY)


def _position():
    x, y, c = lax.axis_index("x"), lax.axis_index("y"), lax.axis_index("c")
    return x, y, c, [(1 - x, y), (x, 1 - y), (1 - x, 1 - y)]


def _remote(src, dst, send_sems, recv_sems, k, to):
    return pltpu.make_async_remote_copy(src_ref=src, dst_ref=dst, send_sem=send_sems.at[k], recv_sem=recv_sems.at[k], device_id=to, device_id_type=MESH)


def _gather_chips(arrs, name):
    n = len(arrs)

    def body(*refs):
        f_refs, g_refs = refs[:n], refs[n:2 * n]
        send_sems, recv_sems = refs[2 * n:]
        x, y, c, chips = _position()
        me, sibling = (x, y, c), (x, y, 1 - c)

        def half(j, ref, sel):
            rows = f_refs[j].shape[1] // 2
            return ref.at[:, pl.ds(sel * rows, rows), :]

        def slot(j, chip, sel):
            return half(j, g_refs[j].at[2 * chip[0] + chip[1]], sel)

        first = [_remote(half(j, f_refs[j], c), slot(j, (x, y), c), send_sems, recv_sems, 6 * j + k, (*chip, c)) for j in range(n) for k, chip in enumerate(chips)]
        for cp in first:
            cp.start()
        passed = []
        for j in range(n):
            for k, chip in enumerate(chips):
                _remote(slot(j, chip, c), slot(j, chip, c), send_sems, recv_sems, 6 * j + k, me).wait_recv()
                passed.append(_remote(slot(j, chip, c), slot(j, chip, c), send_sems, recv_sems, 6 * j + 3 + k, sibling))
                passed[-1].start()
        for j in range(n):
            for k, chip in enumerate(chips):
                _remote(slot(j, chip, 1 - c), slot(j, chip, 1 - c), send_sems, recv_sems, 6 * j + 3 + k, me).wait_recv()
        for cp in first + passed:
            cp.wait_send()

    return pl.pallas_call(
        body,
        name=name,
        in_specs=[ANY] * n,
        out_specs=[ANY] * n,
        out_shape=[jax.ShapeDtypeStruct((N_CHIPS,) + a.shape, a.dtype) for a in arrs],
        scratch_shapes=[pltpu.SemaphoreType.DMA((6 * n,)), pltpu.SemaphoreType.DMA((6 * n,))],
    )(*arrs)


def _swap_sibling(arrs, name):
    n = len(arrs)

    def body(*refs):
        x, y, c, _ = _position()
        send_sems, recv_sems = refs[2 * n:]
        copies = [_remote(refs[j], refs[n + j], send_sems, recv_sems, j, (x, y, 1 - c)) for j in range(n)]
        for cp in copies:
            cp.start()
        for cp in copies:
            cp.wait()

    return pl.pallas_call(
        body,
        name=name,
        in_specs=[ANY] * n,
        out_specs=[ANY] * n,
        out_shape=[jax.ShapeDtypeStruct(a.shape, a.dtype) for a in arrs],
        scratch_shapes=[pltpu.SemaphoreType.DMA((n,)), pltpu.SemaphoreType.DMA((n,))],
    )(*arrs)


def _chips_copies(p_refs, q_refs, send_sems, recv_sems):
    x, y, c, chips = _position()
    my_slot = 2 * x + y
    n = len(p_refs)
    sends = [_remote(p_refs[j].at[2 * px + py], q_refs[j].at[my_slot], send_sems, recv_sems, 3 * j + k, (px, py, c)) for j in range(n) for k, (px, py) in enumerate(chips)]
    arrivals = [_remote(q_refs[j].at[2 * px + py], q_refs[j].at[2 * px + py], send_sems, recv_sems, 3 * j + k, (x, y, c)) for j in range(n) for k, (px, py) in enumerate(chips)]
    return sends, arrivals


def _chips_finish(sends, arrivals):
    for cp in arrivals:
        cp.wait_recv()
    for cp in sends:
        cp.wait_send()


def _exchange_chips(parts, name):
    n = len(parts)

    def body(*refs):
        sends, arrivals = _chips_copies(refs[:n], refs[n:2 * n], *refs[2 * n:])
        for cp in sends:
            cp.start()
        _chips_finish(sends, arrivals)

    return pl.pallas_call(
        body,
        name=name,
        in_specs=[ANY] * n,
        out_specs=[ANY] * n,
        out_shape=[jax.ShapeDtypeStruct(a.shape, a.dtype) for a in parts],
        scratch_shapes=[pltpu.SemaphoreType.DMA((3 * n,)), pltpu.SemaphoreType.DMA((3 * n,))],
    )(*parts)


def _share_sibling(halves, name):
    n = len(halves)

    def body(*refs):
        h_refs, o_refs = refs[:n], refs[n:2 * n]
        send_sems, recv_sems = refs[2 * n:]
        x, y, c, _ = _position()

        def rows(j, sel):
            r = h_refs[j].shape[0]
            return o_refs[j].at[pl.ds(sel * r, r), :]

        sends = [_remote(h_refs[j], rows(j, c), send_sems, recv_sems, j, (x, y, 1 - c)) for j in range(n)]
        for cp in sends:
            cp.start()
        for j in range(n):
            _remote(rows(j, 1 - c), rows(j, 1 - c), send_sems, recv_sems, j, (x, y, c)).wait_recv()
        for cp in sends:
            cp.wait_send()

    return pl.pallas_call(
        body,
        name=name,
        in_specs=[ANY] * n,
        out_specs=[ANY] * n,
        out_shape=[jax.ShapeDtypeStruct((2 * a.shape[0],) + a.shape[1:], a.dtype) for a in halves],
        scratch_shapes=[pltpu.SemaphoreType.DMA((n,)), pltpu.SemaphoreType.DMA((n,))],
    )(*halves)


def _sum_rows(r, c, n_in):
    return _tile(r, max(16, (1 << 20) // (c * (n_in + 1))), 16)


def _add2(a, b, out_dtype, name):
    n, r, c = a.shape
    tr = _sum_rows(r, c, 2)

    def body(a_ref, b_ref, o_ref):
        o_ref[...] = (a_ref[...].astype(F32) + b_ref[...].astype(F32)).astype(o_ref.dtype)

    blk = pl.BlockSpec((1, tr, c), lambda s, i: (s, i, 0))
    return pl.pallas_call(
        body,
        name=name,
        grid=(n, r // tr),
        in_specs=[blk, blk],
        out_specs=blk,
        out_shape=jax.ShapeDtypeStruct(a.shape, out_dtype),
        compiler_params=_params(("parallel", "parallel")),
    )(a, b)


def _sum_slots(q, name):
    n, r, c = q.shape
    tr = _sum_rows(r, c, n)

    def body(q_ref, o_ref):
        acc = q_ref[0].astype(F32)
        for s in range(1, n):
            acc = acc + q_ref[s].astype(F32)
        o_ref[...] = acc

    return pl.pallas_call(
        body,
        name=name,
        grid=(r // tr,),
        in_specs=[pl.BlockSpec((n, tr, c), lambda i: (0, i, 0))],
        out_specs=pl.BlockSpec((tr, c), lambda i: (i, 0)),
        out_shape=jax.ShapeDtypeStruct((r, c), F32),
        compiler_params=_params(("parallel",)),
    )(q)


def _adamw(w, g, m, v, name):
    r, c = w.shape
    tr = _tile(r, 256, 8)
    c1 = 1.0 - ADAM_B1 ** ADAM_STEP
    c2 = 1.0 - ADAM_B2 ** ADAM_STEP

    def body(w_ref, g_ref, m_ref, v_ref, d_ref, nm_ref, nv_ref):
        g = g_ref[...]
        nm = ADAM_B1 * m_ref[...] + (1.0 - ADAM_B1) * g
        nv = ADAM_B2 * v_ref[...] + (1.0 - ADAM_B2) * (g * g)
        d_ref[...] = -ADAM_LR * ((nm / c1) / (jnp.sqrt(nv / c2) + ADAM_EPS) + ADAM_WD * w_ref[...])
        nm_ref[...] = nm
        nv_ref[...] = nv

    blk = pl.BlockSpec((tr, c), lambda i: (i, 0))
    shape = jax.ShapeDtypeStruct((r, c), F32)
    return pl.pallas_call(
        body,
        name=name,
        grid=(r // tr,),
        in_specs=[blk] * 4,
        out_specs=[blk] * 3,
        out_shape=[shape] * 3,
        compiler_params=_params(("parallel",)),
    )(w, g, m, v)


WEIGHTS = ("ffn1_w_in", "ffn1_w_out", "ffn2_w_in", "ffn2_w_out", "ln_g", "ln_b", "sb_w_in", "sb_w_out", "swa_w_in", "swa_sinks", "swa_w_out", "dil_w_in", "dil_w_out", "ple_w_proj", "ple_w_gate")
SHARD_AXIS = {"ffn1_w_in": 2, "ffn1_w_out": 1, "ffn2_w_in": 2, "ffn2_w_out": 1, "ln_g": 2, "ln_b": 2, "sb_w_in": 2, "sb_w_out": 1, "swa_w_in": 2, "swa_sinks": None, "swa_w_out": 1, "dil_w_in": 2, "dil_w_out": 1, "ple_w_proj": 2, "ple_w_gate": 1}
SMALL = ("ln_g", "ln_b", "swa_sinks")
SMALL_COLS = 128
SMALL_UNIT = 16 * SMALL_COLS


def _pack_small(pieces, lead):
    flat = jnp.concatenate([a.reshape(lead + (-1,)) for a in pieces], axis=-1)
    n = flat.shape[-1]
    flat = jnp.pad(flat, [(0, 0)] * len(lead) + [(0, -n % SMALL_UNIT)])
    return flat.reshape(lead + (-1, SMALL_COLS))


def _unpack_small(buf, shapes, lead):
    flat = buf.reshape(lead + (-1,))
    out, off = [], 0
    for shp in shapes:
        n = math.prod(shp)
        out.append(flat[..., off:off + n].reshape(lead + tuple(shp)))
        off += n
    return out


def _reduce_pair(grads, c, wire, tag):
    keep = [lax.dynamic_slice_in_dim(g, c * (g.shape[1] // 2), g.shape[1] // 2, axis=1) for g in grads]
    give = [lax.dynamic_slice_in_dim(g, (1 - c) * (g.shape[1] // 2), g.shape[1] // 2, axis=1).astype(wire) for g in grads]
    got = _swap_sibling(give, f"{tag}_pair")
    return [_add2(k, g, wire, f"{tag}_pair_sum{j}") for j, (k, g) in enumerate(zip(keep, got))]


def _reduce_finish(part, landed, c, tag):
    landed = [_fill_slot(q, lax.dynamic_slice_in_dim(p, _my_slot(), 1, axis=0)) for q, p in zip(landed, part)]
    halves = [_sum_slots(q, f"{tag}_chips_sum{j}") for j, q in enumerate(landed)]
    shared = _share_sibling(halves, f"{tag}_share")
    return [lax.dynamic_update_slice_in_dim(s, h, c * h.shape[0], axis=0) for s, h in zip(shared, halves)]


def _reduce_group(grads, c, wire, tag):
    part = _reduce_pair(grads, c, wire, tag)
    return _reduce_finish(part, _exchange_chips(part, f"{tag}_chips"), c, tag)


def _my_slot():
    return 2 * lax.axis_index("x") + lax.axis_index("y")


def _fill_slot(slots, mine):
    return lax.dynamic_update_slice_in_dim(slots, mine, _my_slot(), axis=0)


def _full_from_shards(g, axis):
    g = jnp.moveaxis(g, 0, axis)
    return g.reshape(g.shape[:axis] + (g.shape[axis] * g.shape[axis + 1],) + g.shape[axis + 2:])


def _shards_from_full(a, axis):
    a = a.reshape(a.shape[:axis] + (N_CHIPS, a.shape[axis] // N_CHIPS) + a.shape[axis + 1:])
    return jnp.moveaxis(a, axis, 0)


def _ffn_forward(x, xb, w_in, w_out, g, b, alpha, tag):
    gate, up, act = _ffn_in(xb, w_in, f"{tag}_in")
    out, outb, xhat, rstd = _mm_ln(act, w_out, x, g, b, alpha, 0.5, f"{tag}_out_ln")
    return (out, outb), dict(xb=xb, gate=gate, up=up, act=act, xhat=xhat, rstd=rstd)


def _ffn_backward(dout, saved, w_in, w_out, g, tag, exchange=None):
    dr, dyb, dg, db = _ln_bwd(*dout, saved["xhat"], saved["rstd"], g, 0.5, f"{tag}_ln_bwd")
    dh = _ffn_dact(dyb, w_out, saved["gate"], saved["up"], f"{tag}_dact")
    dw_out = _mm(saved["act"], dyb, "tn", F32, f"{tag}_dw_out")
    dw_in = _mm(saved["xb"], dh, "tn", F32, f"{tag}_dw_in", split_b=True)
    dxb = _mm(dh, w_in, "nt", F32, f"{tag}_dx", split_a=True, exchange=exchange)
    dxb, landed = dxb if exchange is not None else (dxb, None)
    return dr, dxb, dw_in, dw_out, dg, db, landed


def _sb_forward(xb, w_in, w_out, tag):
    nw = w_out.shape[0]
    h = _mm(xb, w_in, "nn", MXU_DTYPE, f"{tag}_qkv")
    q, k, v = (_to_heads(h[:, j * nw:(j + 1) * nw]) for j in range(3))
    o, ob = _sb_fwd(q, k, v, f"{tag}_att")
    ob = _from_heads(ob)
    return ob, dict(q=q, k=k, v=v, o=o, ob=ob)


def _sb_backward(dmix, saved, xb, w_in, w_out, tag):
    dw_out = _mm(saved["ob"], dmix, "tn", F32, f"{tag}_dw_out")
    do = _to_heads(_mm(dmix, w_out, "nt", MXU_DTYPE, f"{tag}_do"))
    dq, dk, dv = _sb_bwd(saved["q"], saved["k"], saved["v"], saved["o"], do, f"{tag}_att_bwd")
    dh = jnp.concatenate([_from_heads(t) for t in (dq, dk, dv)], axis=1).astype(MXU_DTYPE)
    dw_in = _mm(xb, dh, "tn", F32, f"{tag}_dw_in")
    return _mm(dh, w_in, "nt", F32, f"{tag}_dx"), dw_in, dw_out


def _swa_forward(xb, w_in, sinks, w_out, tables, tag):
    nq = w_out.shape[0]
    nkv = (w_in.shape[1] - nq) // 2
    h = _mm(xb, w_in, "nn", F32, f"{tag}_qkv")
    n_rot = (nq + nkv) // 128
    hb = _rope(h, *tables, lambda j: j // n_rot, MXU_DTYPE, f"{tag}_rope")
    q, k, v = _to_heads(hb[:, :nq]), _to_heads(hb[:, nq:nq + nkv]), _to_heads(hb[:, nq + nkv:])
    o, lse = _band_fwd(q, k, v, sinks, SWA_WINDOW - 1, f"{tag}_att")
    o, ob = _merge_fwd([o], [lse], f"{tag}_cast")
    ob = _from_heads(ob)
    return ob, dict(q=q, k=k, v=v, o=o, lse=lse, ob=ob, n_rot=n_rot)


def _swa_backward(dmix, saved, xb, w_in, sinks, w_out, tables, tag):
    dw_out = _mm(saved["ob"], dmix, "tn", F32, f"{tag}_dw_out")
    do = _to_heads(_mm(dmix, w_out, "nt", MXU_DTYPE, f"{tag}_do"))
    (dog,), (delta,) = _merge_bwd(do, saved["o"], [saved["lse"]], f"{tag}_delta")
    dq, dk, dv, dsink = _band_bwd(saved["q"], saved["k"], saved["v"], dog, saved["lse"], delta, sinks, SWA_WINDOW - 1, f"{tag}_att_bwd")
    dh = jnp.concatenate([_from_heads(t) for t in (dq, dk, dv)], axis=1)
    n_rot = saved["n_rot"]
    dhb = _rope(dh, tables[0], -tables[1], lambda j: j // n_rot, MXU_DTYPE, f"{tag}_rope_bwd")
    dw_in = _mm(xb, dhb, "tn", F32, f"{tag}_dw_in")
    return _mm(dhb, w_in, "nt", F32, f"{tag}_dx"), dw_in, dw_out, dsink[:, 0, 0]


def _dil_forward(xb, w_in, w_out, tables, tag):
    nw = w_out.shape[0]
    h = _mm(xb, w_in, "nn", F32, f"{tag}_qkv")
    hb = _rope(h, *tables, lambda j: (j % 3) // 2, MXU_DTYPE, f"{tag}_rope", width=nw)
    qkv, outs, lses = [], [], []
    for gi, (win, dil) in enumerate(DIL_GROUPS):
        base = gi * 3 * nw
        q, k, v = (_to_heads(hb[:, base + j * nw:base + (j + 1) * nw], dil) for j in range(3))
        o, lse = _band_fwd(q, k, v, None, win // dil, f"{tag}_att{gi}")
        qkv.append((q, k, v))
        outs.append(_unstride(o, dil))
        lses.append(_unstride(lse, dil))
    o, ob = _merge_fwd(outs, lses, f"{tag}_merge")
    ob = _from_heads(ob)
    return ob, dict(qkv=qkv, o=o, lses=lses, ob=ob)


def _dil_backward(dmix, saved, xb, w_in, w_out, tables, tag):
    dw_out = _mm(saved["ob"], dmix, "tn", F32, f"{tag}_dw_out")
    do = _to_heads(_mm(dmix, w_out, "nt", MXU_DTYPE, f"{tag}_do"))
    dogs, deltas = _merge_bwd(do, saved["o"], saved["lses"], f"{tag}_merge_bwd")
    parts = []
    for gi, (win, dil) in enumerate(DIL_GROUPS):
        q, k, v = saved["qkv"][gi]
        dq, dk, dv, _ = _band_bwd(q, k, v, _restride(dogs[gi], dil), _restride(saved["lses"][gi], dil), _restride(deltas[gi], dil), None, win // dil, f"{tag}_att{gi}_bwd")
        parts += [_from_heads(t, dil) for t in (dq, dk, dv)]
    dh = jnp.concatenate(parts, axis=1)
    dhb = _rope(dh, tables[0], -tables[1], lambda j: (j % 3) // 2, MXU_DTYPE, f"{tag}_rope_bwd", width=w_out.shape[0])
    dw_in = _mm(xb, dhb, "tn", F32, f"{tag}_dw_in")
    return _mm(dhb, w_in, "nt", F32, f"{tag}_dx"), dw_in, dw_out


def kernel(x, p, ffn1_w_in, ffn1_w_out, ffn2_w_in, ffn2_w_out, ln_g, ln_b, sb_w_in, sb_w_out, swa_w_in, swa_sinks, swa_w_out, dil_w_in, dil_w_out, ple_w_proj, ple_w_gate, loss_target, m_ffn1_w_in, m_ffn1_w_out, m_ffn2_w_in, m_ffn2_w_out, m_ln_g, m_ln_b, m_sb_w_in, m_sb_w_out, m_swa_w_in, m_swa_sinks, m_swa_w_out, m_dil_w_in, m_dil_w_out, m_ple_w_proj, m_ple_w_gate, v_ffn1_w_in, v_ffn1_w_out, v_ffn2_w_in, v_ffn2_w_out, v_ln_g, v_ln_b, v_sb_w_in, v_sb_w_out, v_swa_w_in, v_swa_sinks, v_swa_w_out, v_dil_w_in, v_dil_w_out, v_ple_w_proj, v_ple_w_gate):
    shard = dict(ffn1_w_in=ffn1_w_in, ffn1_w_out=ffn1_w_out, ffn2_w_in=ffn2_w_in, ffn2_w_out=ffn2_w_out, ln_g=ln_g, ln_b=ln_b, sb_w_in=sb_w_in, sb_w_out=sb_w_out, swa_w_in=swa_w_in, swa_sinks=swa_sinks, swa_w_out=swa_w_out, dil_w_in=dil_w_in, dil_w_out=dil_w_out, ple_w_proj=ple_w_proj, ple_w_gate=ple_w_gate)
    mom_m = dict(ffn1_w_in=m_ffn1_w_in, ffn1_w_out=m_ffn1_w_out, ffn2_w_in=m_ffn2_w_in, ffn2_w_out=m_ffn2_w_out, ln_g=m_ln_g, ln_b=m_ln_b, sb_w_in=m_sb_w_in, sb_w_out=m_sb_w_out, swa_w_in=m_swa_w_in, swa_sinks=m_swa_sinks, swa_w_out=m_swa_w_out, dil_w_in=m_dil_w_in, dil_w_out=m_dil_w_out, ple_w_proj=m_ple_w_proj, ple_w_gate=m_ple_w_gate)
    mom_v = dict(ffn1_w_in=v_ffn1_w_in, ffn1_w_out=v_ffn1_w_out, ffn2_w_in=v_ffn2_w_in, ffn2_w_out=v_ffn2_w_out, ln_g=v_ln_g, ln_b=v_ln_b, sb_w_in=v_sb_w_in, sb_w_out=v_sb_w_out, swa_w_in=v_swa_w_in, swa_sinks=v_swa_sinks, swa_w_out=v_swa_w_out, dil_w_in=v_dil_w_in, dil_w_out=v_dil_w_out, ple_w_proj=v_ple_w_proj, ple_w_gate=v_ple_w_gate)
    depth = ffn1_w_in.shape[0]
    alpha = (2 * depth) ** 0.25
    c = lax.axis_index("c")

    mats = [n for n in WEIGHTS if n not in SMALL]
    local = [shard[n].astype(MXU_DTYPE) for n in mats] + [_pack_small([ln_g, ln_b], ())[None]]
    got = [_fill_slot(g, a[None]) for g, a in zip(_gather_chips(local, "gather_weights"), local)]
    full = {"swa_sinks": swa_sinks}
    for n, g in zip(mats, got):
        full[n] = _full_from_shards(g, SHARD_AXIS[n])
    for n, g in zip(("ln_g", "ln_b"), _unpack_small(got[-1][:, 0], [ln_g.shape, ln_b.shape], (N_CHIPS,))):
        full[n] = _full_from_shards(g, SHARD_AXIS[n])

    seq = x.shape[1]
    tables = _rope_tables(seq)
    xf = x[0]
    xb = xf.astype(MXU_DTYPE)
    saved = []
    for i in range(depth):
        kind, j = i % 3, i // 3
        mixer = ("sb", "swa", "dil")[kind]
        sv = {}
        (x1, x1b), sv["ffn1"] = _ffn_forward(xf, xb, full["ffn1_w_in"][i], full["ffn1_w_out"][i], full["ln_g"][i, 0], full["ln_b"][i, 0], alpha, f"l{i}_ffn1")
        if kind == 0:
            mix, sv["mix"] = _sb_forward(x1b, full["sb_w_in"][j], full["sb_w_out"][j], f"l{i}_sb")
        elif kind == 1:
            mix, sv["mix"] = _swa_forward(x1b, full["swa_w_in"][j], swa_sinks[j], full["swa_w_out"][j], tables, f"l{i}_swa")
        else:
            mix, sv["mix"] = _dil_forward(x1b, full["dil_w_in"][j], full["dil_w_out"][j], tables, f"l{i}_dil")
        x2, x2b, sv["xhat2"], sv["rstd2"] = _mm_ln(mix, full[f"{mixer}_w_out"][j], x1, full["ln_g"][i, 1], full["ln_b"][i, 1], alpha, 1.0, f"l{i}_{mixer}_proj_ln")
        sv["x1b"] = x1b
        (x3, x3b), sv["ffn2"] = _ffn_forward(x2, x2b, full["ffn2_w_in"][i], full["ffn2_w_out"][i], full["ln_g"][i, 2], full["ln_b"][i, 2], alpha, f"l{i}_ffn2")
        xf, xb, sv["u"], sv["e"] = _ple_fwd(x3, x3b, p[i, 0], full["ple_w_gate"][i], full["ple_w_proj"][i], f"l{i}_ple")
        sv["x3b"] = x3b
        saved.append(sv)

    loss_part, dy = _loss(xf, loss_target[0], "loss")
    loss = lax.psum(loss_part[0, 0], ("x", "y", "c"))

    grads = {n: [None] * full[n].shape[0] for n in WEIGHTS if n not in ("ln_g", "ln_b")}
    gsum = {n: [None] * full[n].shape[0] for n in mats}
    dln_g = [[None] * 3 for _ in range(depth)]
    dln_b = [[None] * 3 for _ in range(depth)]
    dout = (dy, None, 1.0)
    pending = None

    def finish_reduce(layer, part, tag, landed):
        for (n, k), g in zip(layer, _reduce_finish(part, landed, c, tag)):
            gsum[n][k] = g

    for i in reversed(range(depth)):
        kind, j = i % 3, i // 3
        mixer = ("sb", "swa", "dil")[kind]
        sv = saved[i]
        dx4, dub, deb = _ple_bwd(*dout, sv["u"], sv["e"], f"l{i}_ple_bwd")
        grads["ple_w_gate"][i] = _mm(sv["x3b"], dub, "tn", F32, f"l{i}_ple_dw_gate")
        grads["ple_w_proj"][i] = _mm(p[i, 0], deb, "tn", F32, f"l{i}_ple_dw_proj")
        dxb = _mm(dub, full["ple_w_gate"][i], "nt", F32, f"l{i}_ple_dx")
        dr, dxb, grads["ffn2_w_in"][i], grads["ffn2_w_out"][i], dln_g[i][2], dln_b[i][2], landed = _ffn_backward((dx4, dxb, 1.0), sv["ffn2"], full["ffn2_w_in"][i], full["ffn2_w_out"][i], full["ln_g"][i, 2], f"l{i}_ffn2", exchange=pending and pending[1])
        if pending:
            finish_reduce(*pending, landed)
        dr, dmix, dln_g[i][1], dln_b[i][1] = _ln_bwd(dr, dxb, alpha, sv["xhat2"], sv["rstd2"], full["ln_g"][i, 1], 1.0, f"l{i}_mix_ln_bwd")
        if kind == 0:
            dxb, grads["sb_w_in"][j], grads["sb_w_out"][j] = _sb_backward(dmix, sv["mix"], sv["x1b"], full["sb_w_in"][j], full["sb_w_out"][j], f"l{i}_sb")
        elif kind == 1:
            dxb, grads["swa_w_in"][j], grads["swa_w_out"][j], grads["swa_sinks"][j] = _swa_backward(dmix, sv["mix"], sv["x1b"], full["swa_w_in"][j], swa_sinks[j], full["swa_w_out"][j], tables, f"l{i}_swa")
        else:
            dxb, grads["dil_w_in"][j], grads["dil_w_out"][j] = _dil_backward(dmix, sv["mix"], sv["x1b"], full["dil_w_in"][j], full["dil_w_out"][j], tables, f"l{i}_dil")
        dr, dxb, grads["ffn1_w_in"][i], grads["ffn1_w_out"][i], dln_g[i][0], dln_b[i][0], _ = _ffn_backward((dr, dxb, alpha), sv["ffn1"], full["ffn1_w_in"][i], full["ffn1_w_out"][i], full["ln_g"][i, 0], f"l{i}_ffn1")
        dout = (dr, dxb, alpha)
        layer = [("ffn1_w_in", i), ("ffn1_w_out", i), (f"{mixer}_w_in", j), (f"{mixer}_w_out", j), ("ffn2_w_in", i), ("ffn2_w_out", i), ("ple_w_proj", i), ("ple_w_gate", i)]
        pending = (layer, _reduce_pair([_shards_from_full(grads[n][k], SHARD_AXIS[n] - 1) for n, k in layer], c, MXU_DTYPE, f"l{i}_reduce"), f"l{i}_reduce")
    layer, part, tag = pending
    finish_reduce(layer, part, tag, _exchange_chips(part, f"{tag}_chips"))
    grad_x = _axpy(*dout, "grad_x")[None]

    gshard = {n: jnp.stack(g) for n, g in gsum.items()}
    small = [
        _shards_from_full(jnp.stack([jnp.concatenate(r, axis=0) for r in dln_g]), SHARD_AXIS["ln_g"]),
        _shards_from_full(jnp.stack([jnp.concatenate(r, axis=0) for r in dln_b]), SHARD_AXIS["ln_b"]),
        jnp.broadcast_to(jnp.stack(grads["swa_sinks"])[None], (N_CHIPS,) + swa_sinks.shape),
    ]
    (small_sum,) = _reduce_group([_pack_small(small, (N_CHIPS,))], c, F32, "small_reduce")
    for n, g in zip(SMALL, _unpack_small(small_sum, [shard[n].shape for n in SMALL], ())):
        gshard[n] = g

    delta, new_m, new_v = {}, {}, {}
    for n in WEIGHTS:
        shp = shard[n].shape
        two_d = (-1, shp[-1])
        d, nm, nv = _adamw(shard[n].reshape(two_d), gshard[n].reshape(two_d), mom_m[n].reshape(two_d), mom_v[n].reshape(two_d), f"adamw_{n}")
        delta[n], new_m[n], new_v[n] = d.reshape(shp), nm.reshape(shp), nv.reshape(shp)

    return (loss, grad_x, *[gshard[n] for n in WEIGHTS], *[delta[n] for n in WEIGHTS], *[new_m[n] for n in WEIGHTS], *[new_v[n] for n in WEIGHTS])
```

```python
import functools
import math

import jax
import jax.numpy as jnp
from jax import lax
from jax.experimental import pallas as pl
from jax.experimental.pallas import tpu as pltpu

F32 = jnp.float32
MXU_DTYPE = jnp.bfloat16
MESH = pl.DeviceIdType.MESH

HEAD_DIM = 64
ATT_BLK = 128
SWA_WINDOW = 128
DIL_GROUPS = ((128, 1), (512, 4), (2048, 16))
LN_EPS = 1e-5
ROPE_THETA = 10000.0
NEG_INF = -1e30
ADAM_LR, ADAM_B1, ADAM_B2, ADAM_EPS, ADAM_WD, ADAM_STEP = 0.001, 0.9, 0.999, 1e-08, 0.01, 10

VMEM_LIMIT_BYTES = 56 * 1024 * 1024
N_CHIPS = 4


def _params(sem=None):
    return pltpu.CompilerParams(dimension_semantics=sem, vmem_limit_bytes=VMEM_LIMIT_BYTES)


def _tile(n, target, unit):
    t = (min(target, n) // unit) * unit
    while t >= unit:
        if n % t == 0:
            return t
        t -= unit
    return n


def _dot(a, b, dims):
    return lax.dot_general(a.astype(MXU_DTYPE), b.astype(MXU_DTYPE), (dims, ((), ())), preferred_element_type=F32)


NN = ((1,), (0,))
NT = ((1,), (1,))
TN = ((0,), (0,))


def _accumulate(part, acc_ref, kk, nk, finish):
    if nk == 1:
        finish(part)
        return

    @pl.when(kk == 0)
    def _():
        acc_ref[...] = jnp.zeros_like(acc_ref)

    acc_ref[...] += part

    @pl.when(kk == nk - 1)
    def _():
        finish(acc_ref[...])


def _mm(a, b, mode, out_dtype, name, split_a=False, split_b=False, exchange=None, tm=1408, tn=1408, tk=1408):
    dims = {"nn": NN, "nt": NT, "tn": TN}[mode]
    if split_a:
        m, k = a.shape[1], 2 * a.shape[2]
    elif mode == "tn":
        k, m = a.shape
    else:
        m, k = a.shape
    if split_b:
        n = 2 * b.shape[2]
    elif mode == "nt":
        n = b.shape[0]
    else:
        n = b.shape[1]
    tm = _tile(m, tm, 128)
    tn = _tile(n // 2 if split_b else n, tn, 128)
    tk = _tile(k // 2 if split_a else k, tk, 128)
    nk = k // tk
    nk_half = nk // 2
    nn_half = (n // tn) // 2

    if split_a:
        a_spec = pl.BlockSpec((None, tm, tk), lambda i, j, kk: (kk // nk_half, i, kk % nk_half))
    elif mode == "tn":
        a_spec = pl.BlockSpec((tk, tm), lambda i, j, kk: (kk, i))
    else:
        a_spec = pl.BlockSpec((tm, tk), lambda i, j, kk: (i, kk))
    if split_b:
        b_spec = pl.BlockSpec((None, tk, tn), lambda i, j, kk: (j // nn_half, kk, j % nn_half))
    elif mode == "nt":
        b_spec = pl.BlockSpec((tn, tk), lambda i, j, kk: (j, kk))
    else:
        b_spec = pl.BlockSpec((tk, tn), lambda i, j, kk: (kk, j))

    parts = [] if exchange is None else list(exchange)
    n_ex = len(parts)
    grid = (m // tm, n // tn, nk)

    def body(*refs):
        a_ref, b_ref = refs[:2]
        p_refs, o_ref, q_refs, acc_ref = refs[2:2 + n_ex], refs[2 + n_ex], refs[3 + n_ex:3 + 2 * n_ex], refs[3 + 2 * n_ex]
        if n_ex:
            step = (pl.program_id(0) * grid[1] + pl.program_id(1)) * nk + pl.program_id(2)

            @pl.when(step == 0)
            def _():
                for cp in _chips_copies(p_refs, q_refs, *refs[4 + 2 * n_ex:])[0]:
                    cp.start()

        def finish(total):
            o_ref[...] = total.astype(o_ref.dtype)

        _accumulate(_dot(a_ref[...], b_ref[...], dims), acc_ref, pl.program_id(2), nk, finish)

        if n_ex:

            @pl.when(step == grid[0] * grid[1] * nk - 1)
            def _():
                _chips_finish(*_chips_copies(p_refs, q_refs, *refs[4 + 2 * n_ex:]))

    out = pl.pallas_call(
        body,
        name=name,
        grid=grid,
        in_specs=[a_spec, b_spec] + [ANY] * n_ex,
        out_specs=[pl.BlockSpec((tm, tn), lambda i, j, kk: (i, j))] + [ANY] * n_ex,
        out_shape=[jax.ShapeDtypeStruct((m, n), out_dtype)] + [jax.ShapeDtypeStruct(p.shape, p.dtype) for p in parts],
        scratch_shapes=[pltpu.VMEM((tm, tn), F32)] + [pltpu.SemaphoreType.DMA((3 * n_ex,))] * (2 if n_ex else 0),
        compiler_params=_params(("arbitrary", "arbitrary", "arbitrary") if n_ex else ("parallel", "parallel", "arbitrary")),
    )(a, b, *parts)
    return (out[0], out[1:]) if n_ex else out[0]


def _sigmoid(x):
    return 1.0 / (1.0 + jnp.exp(-x))


def _ffn_in(xb, w_in, name):
    s, d = xb.shape
    f = w_in.shape[1] // 2
    tm = _tile(s, 512, 128)
    tn = _tile(f, 1408, 128)
    nj = f // tn

    def body(x_ref, wg_ref, wu_ref, g_ref, u_ref, a_ref):
        x = x_ref[...]
        g = _dot(x, wg_ref[...], NN)
        u = _dot(x, wu_ref[...], NN)
        g_ref[...] = g.astype(g_ref.dtype)
        u_ref[...] = u.astype(u_ref.dtype)
        a_ref[...] = (g * _sigmoid(g) * u).astype(a_ref.dtype)

    out = pl.BlockSpec((tm, tn), lambda i, j: (i, j))
    return pl.pallas_call(
        body,
        name=name,
        grid=(s // tm, nj),
        in_specs=[
            pl.BlockSpec((tm, d), lambda i, j: (i, 0)),
            pl.BlockSpec((d, tn), lambda i, j: (0, j)),
            pl.BlockSpec((d, tn), lambda i, j: (0, j + nj)),
        ],
        out_specs=[out, out, out],
        out_shape=[
            jax.ShapeDtypeStruct((s, f), MXU_DTYPE),
            jax.ShapeDtypeStruct((s, f), MXU_DTYPE),
            jax.ShapeDtypeStruct((s, f), MXU_DTYPE),
        ],
        compiler_params=_params(("parallel", "parallel")),
    )(xb, w_in, w_in)


def _ffn_dact(dyb, w_out, gate, up, name):
    s, d = dyb.shape
    f = w_out.shape[0]
    tm = _tile(s, 512, 128)
    tn = _tile(f, 1408, 128)

    def body(dy_ref, w_ref, g_ref, u_ref, o_ref):
        dact = _dot(dy_ref[...], w_ref[...], NT)
        g = g_ref[...].astype(F32)
        sig = _sigmoid(g)
        o_ref[0] = (dact * u_ref[...].astype(F32) * (sig * (1.0 + g * (1.0 - sig)))).astype(o_ref.dtype)
        o_ref[1] = (dact * (g * sig)).astype(o_ref.dtype)

    tile = pl.BlockSpec((tm, tn), lambda i, j: (i, j))
    return pl.pallas_call(
        body,
        name=name,
        grid=(s // tm, f // tn),
        in_specs=[
            pl.BlockSpec((tm, d), lambda i, j: (i, 0)),
            pl.BlockSpec((tn, d), lambda i, j: (j, 0)),
            tile,
            tile,
        ],
        out_specs=pl.BlockSpec((2, tm, tn), lambda i, j: (0, i, j)),
        out_shape=jax.ShapeDtypeStruct((2, s, f), MXU_DTYPE),
        compiler_params=_params(("parallel", "parallel")),
    )(dyb, w_out, gate, up)


def _ple_fwd(x, xb, p, w_gate, w_proj, name):
    s, d = x.shape
    pd = p.shape[1]
    tm = _tile(s, 1024, 128)
    tn = _tile(d, 512, 128)

    def body(x_ref, xb_ref, p_ref, wg_ref, wp_ref, o_ref, ob_ref, u_ref, e_ref):
        u = _dot(xb_ref[...], wg_ref[...], NN)
        e = _dot(p_ref[...], wp_ref[...], NN)
        out = x_ref[...] + _sigmoid(u) * e
        o_ref[...] = out
        ob_ref[...] = out.astype(ob_ref.dtype)
        u_ref[...] = u
        e_ref[...] = e

    tile = pl.BlockSpec((tm, tn), lambda i, j: (i, j))
    return pl.pallas_call(
        body,
        name=name,
        grid=(s // tm, d // tn),
        in_specs=[
            tile,
            pl.BlockSpec((tm, d), lambda i, j: (i, 0)),
            pl.BlockSpec((tm, pd), lambda i, j: (i, 0)),
            pl.BlockSpec((d, tn), lambda i, j: (0, j)),
            pl.BlockSpec((pd, tn), lambda i, j: (0, j)),
        ],
        out_specs=[tile, tile, tile, tile],
        out_shape=[
            jax.ShapeDtypeStruct((s, d), F32),
            jax.ShapeDtypeStruct((s, d), MXU_DTYPE),
            jax.ShapeDtypeStruct((s, d), F32),
            jax.ShapeDtypeStruct((s, d), F32),
        ],
        compiler_params=_params(("parallel", "parallel")),
    )(x, xb, p, w_gate, w_proj)


def _rows_spec(ts, d):
    return pl.BlockSpec((ts, d), lambda i: (i, 0))


def _mm_ln(a, w, x, g, b, alpha, beta, name):
    s, k = a.shape
    d = w.shape[1]
    tm = _tile(s, 512, 128)
    tk = _tile(k, 1408, 128)
    nk = k // tk

    def body(a_ref, w_ref, x_ref, g_ref, b_ref, o_ref, ob_ref, xh_ref, rs_ref, acc_ref):
        def finish(y):
            r = alpha * x_ref[...] + beta * y
            mu = jnp.mean(r, axis=1, keepdims=True)
            cen = r - mu
            var = jnp.mean(cen * cen, axis=1, keepdims=True)
            rstd = lax.rsqrt(var + LN_EPS)
            xhat = cen * rstd
            out = xhat * g_ref[...] + b_ref[...]
            o_ref[...] = out
            ob_ref[...] = out.astype(ob_ref.dtype)
            xh_ref[...] = xhat
            rs_ref[...] = rstd

        _accumulate(_dot(a_ref[...], w_ref[...], NN), acc_ref, pl.program_id(1), nk, finish)

    rows = pl.BlockSpec((tm, d), lambda i, kk: (i, 0))
    vec = pl.BlockSpec((1, d), lambda i, kk: (0, 0))
    return pl.pallas_call(
        body,
        name=name,
        grid=(s // tm, nk),
        in_specs=[pl.BlockSpec((tm, tk), lambda i, kk: (i, kk)), pl.BlockSpec((tk, d), lambda i, kk: (kk, 0)), rows, vec, vec],
        out_specs=[rows, rows, rows, pl.BlockSpec((tm, 1), lambda i, kk: (i, 0))],
        out_shape=[
            jax.ShapeDtypeStruct((s, d), F32),
            jax.ShapeDtypeStruct((s, d), MXU_DTYPE),
            jax.ShapeDtypeStruct((s, d), F32),
            jax.ShapeDtypeStruct((s, 1), F32),
        ],
        scratch_shapes=[pltpu.VMEM((tm, d), F32)],
        compiler_params=_params(("parallel", "arbitrary")),
    )(a, w, x, g.reshape(1, d), b.reshape(1, d))


def _ln_bwd(ga, gb, ca, xhat, rstd, g, beta, name):
    s, d = xhat.shape
    ts = _tile(s, 512, 8)

    def body(ga_ref, gb_ref, xh_ref, rs_ref, g_ref, dr_ref, dyb_ref, dg_ref, db_ref):
        @pl.when(pl.program_id(0) == 0)
        def _():
            dg_ref[...] = jnp.zeros_like(dg_ref)
            db_ref[...] = jnp.zeros_like(db_ref)

        dout = ca * ga_ref[...] + gb_ref[...]
        xhat = xh_ref[...]
        dg_ref[...] += jnp.sum(dout * xhat, axis=0, keepdims=True)
        db_ref[...] += jnp.sum(dout, axis=0, keepdims=True)
        dxh = dout * g_ref[...]
        m1 = jnp.mean(dxh, axis=1, keepdims=True)
        m2 = jnp.mean(dxh * xhat, axis=1, keepdims=True)
        dr = rs_ref[...] * (dxh - m1 - xhat * m2)
        dr_ref[...] = dr
        dyb_ref[...] = (beta * dr).astype(dyb_ref.dtype)

    vec = pl.BlockSpec((1, d), lambda i: (0, 0))
    return pl.pallas_call(
        body,
        name=name,
        grid=(s // ts,),
        in_specs=[_rows_spec(ts, d), _rows_spec(ts, d), _rows_spec(ts, d), _rows_spec(ts, 1), vec],
        out_specs=[_rows_spec(ts, d), _rows_spec(ts, d), vec, vec],
        out_shape=[
            jax.ShapeDtypeStruct((s, d), F32),
            jax.ShapeDtypeStruct((s, d), MXU_DTYPE),
            jax.ShapeDtypeStruct((1, d), F32),
            jax.ShapeDtypeStruct((1, d), F32),
        ],
        compiler_params=_params(("arbitrary",)),
    )(ga, gb, xhat, rstd, g.reshape(1, d))


def _ple_bwd(ga, gb, ca, u, e, name):
    s, d = u.shape
    ts = _tile(s, 512, 8)
    grads = [ga] if gb is None else [ga, gb]

    def body(*refs):
        u_ref, e_ref, dx_ref, du_ref, de_ref = refs[len(grads):]
        dx = ca * refs[0][...]
        if gb is not None:
            dx = dx + refs[1][...]
        sig = _sigmoid(u_ref[...])
        dx_ref[...] = dx
        du_ref[...] = (dx * e_ref[...] * sig * (1.0 - sig)).astype(du_ref.dtype)
        de_ref[...] = (dx * sig).astype(de_ref.dtype)

    return pl.pallas_call(
        body,
        name=name,
        grid=(s // ts,),
        in_specs=[_rows_spec(ts, d)] * (len(grads) + 2),
        out_specs=[_rows_spec(ts, d)] * 3,
        out_shape=[
            jax.ShapeDtypeStruct((s, d), F32),
            jax.ShapeDtypeStruct((s, d), MXU_DTYPE),
            jax.ShapeDtypeStruct((s, d), MXU_DTYPE),
        ],
        compiler_params=_params(("parallel",)),
    )(*grads, u, e)


def _axpy(ga, gb, ca, name):
    s, d = ga.shape
    ts = _tile(s, 512, 8)

    def body(ga_ref, gb_ref, o_ref):
        o_ref[...] = ca * ga_ref[...] + gb_ref[...]

    return pl.pallas_call(
        body,
        name=name,
        grid=(s // ts,),
        in_specs=[_rows_spec(ts, d)] * 2,
        out_specs=_rows_spec(ts, d),
        out_shape=jax.ShapeDtypeStruct((s, d), F32),
        compiler_params=_params(("parallel",)),
    )(ga, gb)


def _loss(y, target, name):
    s, d = y.shape
    ts = _tile(s, 512, 8)

    def body(y_ref, t_ref, l_ref, dy_ref):
        @pl.when(pl.program_id(0) == 0)
        def _():
            l_ref[...] = jnp.zeros_like(l_ref)

        err = y_ref[...] - t_ref[...]
        l_ref[...] += (0.5 / d) * jnp.sum(jnp.sum(err * err, axis=1, keepdims=True), axis=0, keepdims=True)
        dy_ref[...] = err * (1.0 / d)

    return pl.pallas_call(
        body,
        name=name,
        grid=(s // ts,),
        in_specs=[_rows_spec(ts, d)] * 2,
        out_specs=[pl.BlockSpec((1, 1), lambda i: (0, 0)), _rows_spec(ts, d)],
        out_shape=[jax.ShapeDtypeStruct((1, 1), F32), jax.ShapeDtypeStruct((s, d), F32)],
        compiler_params=_params(("arbitrary",)),
    )(y, target)


def _rope_tables(seq):
    pos = jnp.arange(seq, dtype=F32)
    inv = ROPE_THETA ** (-jnp.arange(0, HEAD_DIM, 2, dtype=F32) / HEAD_DIM)
    ang = pos[:, None] * inv[None, :]
    cos, sin = jnp.cos(ang), jnp.sin(ang)
    cos2 = jnp.concatenate([cos, cos, cos, cos, jnp.ones((seq, 128), F32)], axis=1)
    sin2 = jnp.concatenate([-sin, sin, -sin, sin, jnp.zeros((seq, 128), F32)], axis=1)
    return cos2, sin2


def _rope(h, cos2, sin2, plain_block, out_dtype, name, width=128):
    s, n = h.shape
    ts = _tile(s, 512, 8)

    def body(h_ref, c_ref, s_ref, o_ref):
        cos, sin = c_ref[...], s_ref[...]
        lane = lax.broadcasted_iota(jnp.int32, cos.shape, 1)
        first_half = lane % HEAD_DIM < HEAD_DIM // 2
        for w in range(width // 128):
            cols = slice(w * 128, (w + 1) * 128)
            x = h_ref[:, cols].astype(F32)
            partner = jnp.where(first_half, pltpu.roll(x, 128 - HEAD_DIM // 2, 1), pltpu.roll(x, HEAD_DIM // 2, 1))
            o_ref[:, cols] = (x * cos + partner * sin).astype(o_ref.dtype)

    tile = pl.BlockSpec((ts, width), lambda i, j: (i, j))
    table = pl.BlockSpec((ts, 128), lambda i, j: (i, plain_block(j)))
    return pl.pallas_call(
        body,
        name=name,
        grid=(s // ts, n // width),
        in_specs=[tile, table, table],
        out_specs=tile,
        out_shape=jax.ShapeDtypeStruct((s, n), out_dtype),
        compiler_params=_params(("parallel", "parallel")),
    )(h, cos2, sin2)


SB_TQ = 512
SB_BLK = 128
SB_WALK = 2
SB_CLOSED = 110.0


def _sb_walk(n_trips, carry, key_blocks):
    def still_open(carry):
        lowest = functools.reduce(jnp.minimum, [c[0] for c in carry])
        return jnp.min(lowest) < SB_CLOSED

    def cond(state):
        t, go, _ = state
        return jnp.logical_and(t < n_trips, go)

    def body(state):
        t, _, carry = state
        carry = key_blocks(t, carry)
        return t + 1, still_open(carry), carry

    return lax.while_loop(cond, body, (jnp.int32(0), still_open(carry), carry))[2]


def _tri2(strict):
    row = lax.broadcasted_iota(jnp.int32, (2 * SB_BLK, SB_BLK), 0) % SB_BLK
    col = lax.broadcasted_iota(jnp.int32, (2 * SB_BLK, SB_BLK), 1)
    return (row > col if strict else row >= col).astype(MXU_DTYPE)


def _cumsum_dot(x, tri2):
    hi = x.astype(MXU_DTYPE)
    lo = x - hi.astype(F32)
    return _dot(jnp.concatenate([hi, lo.astype(MXU_DTYPE)], axis=1), tri2, NN)


def _sb_logits(z, valid):
    l1p = jnp.log(1.0 + jnp.exp(-jnp.abs(z)))
    sp = jnp.maximum(z, 0.0) + l1p
    ls = z - sp
    if valid is not None:
        sp = jnp.where(valid, sp, 0.0)
    return sp, ls


def _sb_weights(ls, after, valid):
    a = jnp.exp(ls - after)
    return a if valid is None else jnp.where(valid, a, 0.0)


def _sb_setup(q_ref, k_ref, v_ref, n_sub, scale):
    qs = [q_ref[0, u * SB_BLK:(u + 1) * SB_BLK, :] * scale for u in range(n_sub)]
    row = lax.broadcasted_iota(jnp.int32, (SB_BLK, SB_BLK), 0)
    col = lax.broadcasted_iota(jnp.int32, (SB_BLK, SB_BLK), 1)

    def load(jj):
        start = pl.multiple_of(jj * SB_BLK, SB_BLK)
        return start, k_ref[0, pl.ds(start, SB_BLK), :], v_ref[0, pl.ds(start, SB_BLK), :]

    return qs, col < row, load


def _sb_fwd(q, k, v, name):
    nh, s, dh = q.shape
    tq = min(SB_TQ, s)
    n_sub = tq // SB_BLK
    scale = dh ** -0.5

    def body(q_ref, k_ref, v_ref, o_ref, ob_ref):
        base = pl.program_id(1) * n_sub
        qs, diag_valid, load = _sb_setup(q_ref, k_ref, v_ref, n_sub, scale)
        tri_after = _tri2(True)

        def key_blocks(first, carry, diagonal):
            blocks = list(range((n_sub if diagonal else SB_WALK) - 1, -1, -1))
            kv = {d: load(first + d) for d in blocks}
            tiles = [(d, u) for d in blocks for u in range(d if diagonal else 0, n_sub)]
            valid = {t: diag_valid if diagonal and t[0] == t[1] else None for t in tiles}
            z = {t: _dot(qs[t[1]], kv[t[0]][1], NT) for t in tiles}
            sp_ls = {t: _sb_logits(z[t], valid[t]) for t in tiles}
            inside = {t: _cumsum_dot(sp_ls[t][0], tri_after) for t in tiles}
            carry = list(carry)
            for t in tiles:
                after_c, acc = carry[t[1]]
                sp, ls = sp_ls[t]
                a = _sb_weights(ls, after_c + inside[t], valid[t])
                carry[t[1]] = (after_c + jnp.sum(sp, axis=1, keepdims=True), acc + _dot(a, kv[t[0]][2], NN))
            return tuple(carry)

        carry = tuple((jnp.zeros((SB_BLK, 1), F32), jnp.zeros((SB_BLK, dh), F32)) for _ in range(n_sub))
        carry = key_blocks(base, carry, True)
        carry = _sb_walk(base // SB_WALK, carry, lambda t, c: key_blocks(base - SB_WALK * (t + 1), c, False))
        for u in range(n_sub):
            rows = slice(u * SB_BLK, (u + 1) * SB_BLK)
            o_ref[0, rows, :] = carry[u][1]
            ob_ref[0, rows, :] = carry[u][1].astype(ob_ref.dtype)

    blk = pl.BlockSpec((1, tq, dh), lambda h, i: (h, i, 0))
    full = pl.BlockSpec((1, s, dh), lambda h, i: (h, 0, 0))
    return pl.pallas_call(
        body,
        name=name,
        grid=(nh, s // tq),
        in_specs=[blk, full, full],
        out_specs=[blk, blk],
        out_shape=[jax.ShapeDtypeStruct((nh, s, dh), F32), jax.ShapeDtypeStruct((nh, s, dh), MXU_DTYPE)],
        compiler_params=_params(("parallel", "parallel")),
    )(q, k, v)


def _sb_bwd(q, k, v, o, do, name):
    nh, s, dh = q.shape
    tq = min(SB_TQ, s)
    n_sub = tq // SB_BLK
    scale = dh ** -0.5

    def body(q_ref, k_ref, v_ref, o_ref, do_ref, dq_ref, dk_ref, dv_ref):
        i = pl.program_id(1)
        base = i * n_sub

        @pl.when(i == 0)
        def _():
            dk_ref[...] = jnp.zeros_like(dk_ref)
            dv_ref[...] = jnp.zeros_like(dv_ref)

        qs, diag_valid, load = _sb_setup(q_ref, k_ref, v_ref, n_sub, scale)
        dob = [do_ref[0, u * SB_BLK:(u + 1) * SB_BLK, :] for u in range(n_sub)]
        total = [jnp.sum(dob[u].astype(F32) * o_ref[0, u * SB_BLK:(u + 1) * SB_BLK, :], axis=1, keepdims=True) for u in range(n_sub)]
        tri_after = _tri2(True)
        tri_from = _tri2(False)

        def key_blocks(first, carry, diagonal):
            blocks = list(range((n_sub if diagonal else SB_WALK) - 1, -1, -1))
            kv = {d: load(first + d) for d in blocks}
            tiles = [(d, u) for d in blocks for u in range(d if diagonal else 0, n_sub)]
            valid = {t: diag_valid if diagonal and t[0] == t[1] else None for t in tiles}
            z = {t: _dot(qs[t[1]], kv[t[0]][1], NT) for t in tiles}
            da = {t: _dot(dob[t[1]], kv[t[0]][2], NT) for t in tiles}
            sp_ls = {t: _sb_logits(z[t], valid[t]) for t in tiles}
            inside = {t: _cumsum_dot(sp_ls[t][0], tri_after) for t in tiles}
            after_run = [c[0] for c in carry]
            ab, dl = {}, {}
            for t in tiles:
                sp, ls = sp_ls[t]
                ab[t] = _sb_weights(ls, after_run[t[1]] + inside[t], valid[t]).astype(MXU_DTYPE)
                dl[t] = ab[t].astype(F32) * da[t]
                after_run[t[1]] = after_run[t[1]] + jnp.sum(sp, axis=1, keepdims=True)
            from_in = {t: _cumsum_dot(dl[t], tri_from) for t in tiles}
            from_run = [c[1] for c in carry]
            dq = [c[2] for c in carry]
            for d in blocks:
                start, kb, _ = kv[d]
                dk = dv = None
                for u in range(d if diagonal else 0, n_sub):
                    t = (d, u)
                    dz = dl[t] - jnp.exp(sp_ls[t][1]) * (dl[t] + total[u] - (from_run[u] + from_in[t]))
                    if valid[t] is not None:
                        dz = jnp.where(valid[t], dz, 0.0)
                    dzb = dz.astype(MXU_DTYPE)
                    from_run[u] = from_run[u] + jnp.sum(dl[t], axis=1, keepdims=True)
                    dq[u] = dq[u] + _dot(dzb, kb, NN)
                    dk_u, dv_u = _dot(dzb, qs[u], TN), _dot(ab[t], dob[u], TN)
                    dk = dk_u if dk is None else dk + dk_u
                    dv = dv_u if dv is None else dv + dv_u
                dk_ref[0, pl.ds(start, SB_BLK), :] += dk
                dv_ref[0, pl.ds(start, SB_BLK), :] += dv
            return tuple(zip(after_run, from_run, dq))

        carry = tuple((jnp.zeros((SB_BLK, 1), F32), jnp.zeros((SB_BLK, 1), F32), jnp.zeros((SB_BLK, dh), F32)) for _ in range(n_sub))
        carry = key_blocks(base, carry, True)
        carry = _sb_walk(base // SB_WALK, carry, lambda t, c: key_blocks(base - SB_WALK * (t + 1), c, False))
        for u in range(n_sub):
            dq_ref[0, u * SB_BLK:(u + 1) * SB_BLK, :] = carry[u][2] * scale

    blk = pl.BlockSpec((1, tq, dh), lambda h, i: (h, i, 0))
    full = pl.BlockSpec((1, s, dh), lambda h, i: (h, 0, 0))
    shape = jax.ShapeDtypeStruct((nh, s, dh), F32)
    return pl.pallas_call(
        body,
        name=name,
        grid=(nh, s // tq),
        in_specs=[blk, full, full, blk, blk],
        out_specs=[blk, full, full],
        out_shape=[shape, shape, shape],
        compiler_params=_params(("parallel", "arbitrary")),
    )(q, k, v, o, do)


BAND_TQ = 1024


def _band_scores(q_ref, k_ref, i, sub, tq, length, max_dist, scale):
    t0 = i * tq + sub * ATT_BLK
    ks = pl.multiple_of(jnp.minimum(jnp.maximum(t0 - ATT_BLK, 0), length - 2 * ATT_BLK), ATT_BLK)
    qs = q_ref[0, sub * ATT_BLK:(sub + 1) * ATT_BLK, :] * scale
    kw = k_ref[0, pl.ds(ks, 2 * ATT_BLK), :]
    sc = _dot(qs, kw, NT)
    diff = (t0 + lax.broadcasted_iota(jnp.int32, sc.shape, 0)) - (ks + lax.broadcasted_iota(jnp.int32, sc.shape, 1))
    valid = (diff >= 0) & (diff <= max_dist)
    return ks, qs, kw, jnp.where(valid, sc, NEG_INF)


def _band_fwd(q, k, v, sinks, max_dist, name):
    bq, length, dh = q.shape
    group = bq // k.shape[0]
    tq = min(BAND_TQ, length)
    scale = dh ** -0.5
    n_sink = 0 if sinks is None else sinks.shape[0]

    def body(*refs):
        if n_sink:
            sink_ref, q_ref, k_ref, v_ref, o_ref, lse_ref = refs
            sink = sink_ref[pl.program_id(0) % n_sink]
        else:
            q_ref, k_ref, v_ref, o_ref, lse_ref = refs
        i = pl.program_id(1)
        scores = [_band_scores(q_ref, k_ref, i, sub, tq, length, max_dist, scale) for sub in range(tq // ATT_BLK)]
        for sub, (ks, _, _, sc) in enumerate(scores):
            m = jnp.max(sc, axis=1, keepdims=True)
            if n_sink:
                m = jnp.maximum(m, sink)
            e = jnp.exp(sc - m)
            den = jnp.sum(e, axis=1, keepdims=True)
            if n_sink:
                den = den + jnp.exp(sink - m)
            rows = slice(sub * ATT_BLK, (sub + 1) * ATT_BLK)
            o_ref[0, rows, :] = _dot(e / den, v_ref[0, pl.ds(ks, 2 * ATT_BLK), :], NN)
            lse_ref[0, rows, :] = m + jnp.log(den)

    qblk = pl.BlockSpec((1, tq, dh), lambda b, i: (b, i, 0))
    kfull = pl.BlockSpec((1, length, dh), lambda b, i: (b // group, 0, 0))
    in_specs = [qblk, kfull, kfull]
    args = [q, k, v]
    if n_sink:
        in_specs = [pl.BlockSpec(memory_space=pltpu.SMEM)] + in_specs
        args = [sinks] + args
    return pl.pallas_call(
        body,
        name=name,
        grid=(bq, length // tq),
        in_specs=in_specs,
        out_specs=[qblk, pl.BlockSpec((1, tq, 1), lambda b, i: (b, i, 0))],
        out_shape=[jax.ShapeDtypeStruct((bq, length, dh), F32), jax.ShapeDtypeStruct((bq, length, 1), F32)],
        compiler_params=_params(("parallel", "parallel")),
    )(*args)


def _band_bwd(q, k, v, do, lse, delta, sinks, max_dist, name):
    bq, length, dh = q.shape
    group = bq // k.shape[0]
    tq = min(BAND_TQ, length)
    scale = dh ** -0.5
    n_sink = 0 if sinks is None else sinks.shape[0]

    def body(*refs):
        if n_sink:
            sink_ref, q_ref, k_ref, v_ref, do_ref, lse_ref, dl_ref, dq_ref, dk_ref, dv_ref, ds_ref = refs
            sink = sink_ref[pl.program_id(0) % n_sink]
        else:
            q_ref, k_ref, v_ref, do_ref, lse_ref, dl_ref, dq_ref, dk_ref, dv_ref, ds_ref = refs
        i = pl.program_id(1)

        @pl.when((i == 0) & (pl.program_id(0) % group == 0))
        def _():
            dk_ref[...] = jnp.zeros_like(dk_ref)
            dv_ref[...] = jnp.zeros_like(dv_ref)

        @pl.when(i == 0)
        def _():
            ds_ref[...] = jnp.zeros_like(ds_ref)

        scores = [_band_scores(q_ref, k_ref, i, sub, tq, length, max_dist, scale) for sub in range(tq // ATT_BLK)]
        dps = [_dot(do_ref[0, sub * ATT_BLK:(sub + 1) * ATT_BLK, :], v_ref[0, pl.ds(ks, 2 * ATT_BLK), :], NT) for sub, (ks, _, _, _) in enumerate(scores)]
        for sub, (ks, qs, kw, sc) in enumerate(scores):
            rows = slice(sub * ATT_BLK, (sub + 1) * ATT_BLK)
            lse = lse_ref[0, rows, :]
            delta_r = dl_ref[0, rows, :]
            dob = do_ref[0, rows, :]
            p = jnp.exp(sc - lse)
            dsb = (p * (dps[sub] - delta_r)).astype(MXU_DTYPE)
            dq_ref[0, rows, :] = _dot(dsb, kw, NN) * scale
            dk_ref[0, pl.ds(ks, 2 * ATT_BLK), :] += _dot(dsb, qs, TN)
            dv_ref[0, pl.ds(ks, 2 * ATT_BLK), :] += _dot(p, dob, TN)
            if n_sink:
                ds_ref[...] += jnp.sum(-jnp.exp(sink - lse) * delta_r, axis=0, keepdims=True)

    qblk = pl.BlockSpec((1, tq, dh), lambda b, i: (b, i, 0))
    qcol = pl.BlockSpec((1, tq, 1), lambda b, i: (b, i, 0))
    kfull = pl.BlockSpec((1, length, dh), lambda b, i: (b // group, 0, 0))
    in_specs = [qblk, kfull, kfull, qblk, qcol, qcol]
    args = [q, k, v, do, lse, delta]
    if n_sink:
        in_specs = [pl.BlockSpec(memory_space=pltpu.SMEM)] + in_specs
        args = [sinks] + args
    kshape = jax.ShapeDtypeStruct((k.shape[0], length, dh), F32)
    return pl.pallas_call(
        body,
        name=name,
        grid=(bq, length // tq),
        in_specs=in_specs,
        out_specs=[qblk, kfull, kfull, pl.BlockSpec((1, 8, 128), lambda b, i: (b, 0, 0))],
        out_shape=[jax.ShapeDtypeStruct((bq, length, dh), F32), kshape, kshape, jax.ShapeDtypeStruct((bq, 8, 128), F32)],
        compiler_params=_params(("arbitrary", "arbitrary")),
    )(*args)


def _merge_weights(lse_refs):
    lses = [r[0] for r in lse_refs]
    m = functools.reduce(jnp.maximum, lses)
    es = [jnp.exp(l - m) for l in lses]
    den = functools.reduce(lambda a, b: a + b, es)
    return [e / den for e in es]


def _merge_fwd(outs, lses, name):
    n = len(outs)
    nh, s, dh = outs[0].shape
    ts = _tile(s, 1024, 8)

    def body(*refs):
        ws = _merge_weights(refs[n:2 * n])
        o = functools.reduce(lambda a, b: a + b, [w * r[0] for w, r in zip(ws, refs[:n])])
        refs[2 * n][0] = o
        refs[2 * n + 1][0] = o.astype(MXU_DTYPE)

    blk = pl.BlockSpec((1, ts, dh), lambda h, i: (h, i, 0))
    col = pl.BlockSpec((1, ts, 1), lambda h, i: (h, i, 0))
    return pl.pallas_call(
        body,
        name=name,
        grid=(nh, s // ts),
        in_specs=[blk] * n + [col] * n,
        out_specs=[blk, blk],
        out_shape=[jax.ShapeDtypeStruct((nh, s, dh), F32), jax.ShapeDtypeStruct((nh, s, dh), MXU_DTYPE)],
        compiler_params=_params(("parallel", "parallel")),
    )(*outs, *lses)


def _merge_bwd(do, o, lses, name):
    n = len(lses)
    nh, s, dh = o.shape
    ts = _tile(s, 1024, 8)

    def body(*refs):
        do_ref, o_ref = refs[:2]
        ws = _merge_weights(refs[2:2 + n])
        dof = do_ref[0].astype(F32)
        base = jnp.sum(dof * o_ref[0], axis=1, keepdims=True)
        for g in range(n):
            refs[2 + n + g][0] = (ws[g] * dof).astype(MXU_DTYPE)
            refs[2 + 2 * n + g][0] = ws[g] * base

    blk = pl.BlockSpec((1, ts, dh), lambda h, i: (h, i, 0))
    col = pl.BlockSpec((1, ts, 1), lambda h, i: (h, i, 0))
    res = pl.pallas_call(
        body,
        name=name,
        grid=(nh, s // ts),
        in_specs=[blk, blk] + [col] * n,
        out_specs=[blk] * n + [col] * n,
        out_shape=[jax.ShapeDtypeStruct((nh, s, dh), MXU_DTYPE)] * n + [jax.ShapeDtypeStruct((nh, s, 1), F32)] * n,
        compiler_params=_params(("parallel", "parallel")),
    )(do, o, *lses)
    return res[:n], res[n:]


def _to_heads(a, dil=1):
    s, n = a.shape
    nh = n // HEAD_DIM
    a = a.reshape(s // dil, dil, nh, HEAD_DIM).transpose(1, 2, 0, 3)
    return a.reshape(dil * nh, s // dil, HEAD_DIM)


def _from_heads(a, dil=1):
    b, ls, c = a.shape
    nh = b // dil
    return a.reshape(dil, nh, ls, c).transpose(2, 0, 1, 3).reshape(ls * dil, nh * c)


def _restride(a, dil):
    nh, s, c = a.shape
    return a.reshape(nh, s // dil, dil, c).transpose(2, 0, 1, 3).reshape(dil * nh, s // dil, c)


def _unstride(a, dil):
    b, ls, c = a.shape
    nh = b // dil
    return a.reshape(dil, nh, ls, c).transpose(1, 2, 0, 3).reshape(nh, ls * dil, c)


ANY = pl.BlockSpec(memory_space=pl.ANY)


def _position():
    x, y, c = lax.axis_index("x"), lax.axis_index("y"), lax.axis_index("c")
    return x, y, c, [(1 - x, y), (x, 1 - y), (1 - x, 1 - y)]


def _remote(src, dst, send_sems, recv_sems, k, to):
    return pltpu.make_async_remote_copy(src_ref=src, dst_ref=dst, send_sem=send_sems.at[k], recv_sem=recv_sems.at[k], device_id=to, device_id_type=MESH)


def _gather_chips(arrs, name):
    n = len(arrs)

    def body(*refs):
        f_refs, g_refs = refs[:n], refs[n:2 * n]
        send_sems, recv_sems = refs[2 * n:]
        x, y, c, chips = _position()
        me, sibling = (x, y, c), (x, y, 1 - c)

        def half(j, ref, sel):
            rows = f_refs[j].shape[1] // 2
            return ref.at[:, pl.ds(sel * rows, rows), :]

        def slot(j, chip, sel):
            return half(j, g_refs[j].at[2 * chip[0] + chip[1]], sel)

        first = [_remote(half(j, f_refs[j], c), slot(j, (x, y), c), send_sems, recv_sems, 6 * j + k, (*chip, c)) for j in range(n) for k, chip in enumerate(chips)]
        for cp in first:
            cp.start()
        passed = []
        for j in range(n):
            for k, chip in enumerate(chips):
                _remote(slot(j, chip, c), slot(j, chip, c), send_sems, recv_sems, 6 * j + k, me).wait_recv()
                passed.append(_remote(slot(j, chip, c), slot(j, chip, c), send_sems, recv_sems, 6 * j + 3 + k, sibling))
                passed[-1].start()
        for j in range(n):
            for k, chip in enumerate(chips):
                _remote(slot(j, chip, 1 - c), slot(j, chip, 1 - c), send_sems, recv_sems, 6 * j + 3 + k, me).wait_recv()
        for cp in first + passed:
            cp.wait_send()

    return pl.pallas_call(
        body,
        name=name,
        in_specs=[ANY] * n,
        out_specs=[ANY] * n,
        out_shape=[jax.ShapeDtypeStruct((N_CHIPS,) + a.shape, a.dtype) for a in arrs],
        scratch_shapes=[pltpu.SemaphoreType.DMA((6 * n,)), pltpu.SemaphoreType.DMA((6 * n,))],
    )(*arrs)


def _swap_sibling(arrs, name):
    n = len(arrs)

    def body(*refs):
        x, y, c, _ = _position()
        send_sems, recv_sems = refs[2 * n:]
        copies = [_remote(refs[j], refs[n + j], send_sems, recv_sems, j, (x, y, 1 - c)) for j in range(n)]
        for cp in copies:
            cp.start()
        for cp in copies:
            cp.wait()

    return pl.pallas_call(
        body,
        name=name,
        in_specs=[ANY] * n,
        out_specs=[ANY] * n,
        out_shape=[jax.ShapeDtypeStruct(a.shape, a.dtype) for a in arrs],
        scratch_shapes=[pltpu.SemaphoreType.DMA((n,)), pltpu.SemaphoreType.DMA((n,))],
    )(*arrs)


def _chips_copies(p_refs, q_refs, send_sems, recv_sems):
    x, y, c, chips = _position()
    my_slot = 2 * x + y
    n = len(p_refs)
    sends = [_remote(p_refs[j].at[2 * px + py], q_refs[j].at[my_slot], send_sems, recv_sems, 3 * j + k, (px, py, c)) for j in range(n) for k, (px, py) in enumerate(chips)]
    arrivals = [_remote(q_refs[j].at[2 * px + py], q_refs[j].at[2 * px + py], send_sems, recv_sems, 3 * j + k, (x, y, c)) for j in range(n) for k, (px, py) in enumerate(chips)]
    return sends, arrivals


def _chips_finish(sends, arrivals):
    for cp in arrivals:
        cp.wait_recv()
    for cp in sends:
        cp.wait_send()


def _exchange_chips(parts, name):
    n = len(parts)

    def body(*refs):
        sends, arrivals = _chips_copies(refs[:n], refs[n:2 * n], *refs[2 * n:])
        for cp in sends:
            cp.start()
        _chips_finish(sends, arrivals)

    return pl.pallas_call(
        body,
        name=name,
        in_specs=[ANY] * n,
        out_specs=[ANY] * n,
        out_shape=[jax.ShapeDtypeStruct(a.shape, a.dtype) for a in parts],
        scratch_shapes=[pltpu.SemaphoreType.DMA((3 * n,)), pltpu.SemaphoreType.DMA((3 * n,))],
    )(*parts)


def _share_sibling(halves, name):
    n = len(halves)

    def body(*refs):
        h_refs, o_refs = refs[:n], refs[n:2 * n]
        send_sems, recv_sems = refs[2 * n:]
        x, y, c, _ = _position()

        def rows(j, sel):
            r = h_refs[j].shape[0]
            return o_refs[j].at[pl.ds(sel * r, r), :]

        sends = [_remote(h_refs[j], rows(j, c), send_sems, recv_sems, j, (x, y, 1 - c)) for j in range(n)]
        for cp in sends:
            cp.start()
        for j in range(n):
            _remote(rows(j, 1 - c), rows(j, 1 - c), send_sems, recv_sems, j, (x, y, c)).wait_recv()
        for cp in sends:
            cp.wait_send()

    return pl.pallas_call(
        body,
        name=name,
        in_specs=[ANY] * n,
        out_specs=[ANY] * n,
        out_shape=[jax.ShapeDtypeStruct((2 * a.shape[0],) + a.shape[1:], a.dtype) for a in halves],
        scratch_shapes=[pltpu.SemaphoreType.DMA((n,)), pltpu.SemaphoreType.DMA((n,))],
    )(*halves)


def _sum_rows(r, c, n_in):
    return _tile(r, max(16, (1 << 20) // (c * (n_in + 1))), 16)


def _add2(a, b, out_dtype, name):
    n, r, c = a.shape
    tr = _sum_rows(r, c, 2)

    def body(a_ref, b_ref, o_ref):
        o_ref[...] = (a_ref[...].astype(F32) + b_ref[...].astype(F32)).astype(o_ref.dtype)

    blk = pl.BlockSpec((1, tr, c), lambda s, i: (s, i, 0))
    return pl.pallas_call(
        body,
        name=name,
        grid=(n, r // tr),
        in_specs=[blk, blk],
        out_specs=blk,
        out_shape=jax.ShapeDtypeStruct(a.shape, out_dtype),
        compiler_params=_params(("parallel", "parallel")),
    )(a, b)


def _sum_slots(q, name):
    n, r, c = q.shape
    tr = _sum_rows(r, c, n)

    def body(q_ref, o_ref):
        acc = q_ref[0].astype(F32)
        for s in range(1, n):
            acc = acc + q_ref[s].astype(F32)
        o_ref[...] = acc

    return pl.pallas_call(
        body,
        name=name,
        grid=(r // tr,),
        in_specs=[pl.BlockSpec((n, tr, c), lambda i: (0, i, 0))],
        out_specs=pl.BlockSpec((tr, c), lambda i: (i, 0)),
        out_shape=jax.ShapeDtypeStruct((r, c), F32),
        compiler_params=_params(("parallel",)),
    )(q)


def _adamw(w, g, m, v, name):
    r, c = w.shape
    tr = _tile(r, 256, 8)
    c1 = 1.0 - ADAM_B1 ** ADAM_STEP
    c2 = 1.0 - ADAM_B2 ** ADAM_STEP

    def body(w_ref, g_ref, m_ref, v_ref, d_ref, nm_ref, nv_ref):
        g = g_ref[...]
        nm = ADAM_B1 * m_ref[...] + (1.0 - ADAM_B1) * g
        nv = ADAM_B2 * v_ref[...] + (1.0 - ADAM_B2) * (g * g)
        d_ref[...] = -ADAM_LR * ((nm / c1) / (jnp.sqrt(nv / c2) + ADAM_EPS) + ADAM_WD * w_ref[...])
        nm_ref[...] = nm
        nv_ref[...] = nv

    blk = pl.BlockSpec((tr, c), lambda i: (i, 0))
    shape = jax.ShapeDtypeStruct((r, c), F32)
    return pl.pallas_call(
        body,
        name=name,
        grid=(r // tr,),
        in_specs=[blk] * 4,
        out_specs=[blk] * 3,
        out_shape=[shape] * 3,
        compiler_params=_params(("parallel",)),
    )(w, g, m, v)


WEIGHTS = ("ffn1_w_in", "ffn1_w_out", "ffn2_w_in", "ffn2_w_out", "ln_g", "ln_b", "sb_w_in", "sb_w_out", "swa_w_in", "swa_sinks", "swa_w_out", "dil_w_in", "dil_w_out", "ple_w_proj", "ple_w_gate")
SHARD_AXIS = {"ffn1_w_in": 2, "ffn1_w_out": 1, "ffn2_w_in": 2, "ffn2_w_out": 1, "ln_g": 2, "ln_b": 2, "sb_w_in": 2, "sb_w_out": 1, "swa_w_in": 2, "swa_sinks": None, "swa_w_out": 1, "dil_w_in": 2, "dil_w_out": 1, "ple_w_proj": 2, "ple_w_gate": 1}
SMALL = ("ln_g", "ln_b", "swa_sinks")
SMALL_COLS = 128
SMALL_UNIT = 16 * SMALL_COLS


def _pack_small(pieces, lead):
    flat = jnp.concatenate([a.reshape(lead + (-1,)) for a in pieces], axis=-1)
    n = flat.shape[-1]
    flat = jnp.pad(flat, [(0, 0)] * len(lead) + [(0, -n % SMALL_UNIT)])
    return flat.reshape(lead + (-1, SMALL_COLS))


def _unpack_small(buf, shapes, lead):
    flat = buf.reshape(lead + (-1,))
    out, off = [], 0
    for shp in shapes:
        n = math.prod(shp)
        out.append(flat[..., off:off + n].reshape(lead + tuple(shp)))
        off += n
    return out


def _reduce_pair(grads, c, wire, tag):
    keep = [lax.dynamic_slice_in_dim(g, c * (g.shape[1] // 2), g.shape[1] // 2, axis=1) for g in grads]
    give = [lax.dynamic_slice_in_dim(g, (1 - c) * (g.shape[1] // 2), g.shape[1] // 2, axis=1).astype(wire) for g in grads]
    got = _swap_sibling(give, f"{tag}_pair")
    return [_add2(k, g, wire, f"{tag}_pair_sum{j}") for j, (k, g) in enumerate(zip(keep, got))]


def _reduce_finish(part, landed, c, tag):
    landed = [_fill_slot(q, lax.dynamic_slice_in_dim(p, _my_slot(), 1, axis=0)) for q, p in zip(landed, part)]
    halves = [_sum_slots(q, f"{tag}_chips_sum{j}") for j, q in enumerate(landed)]
    shared = _share_sibling(halves, f"{tag}_share")
    return [lax.dynamic_update_slice_in_dim(s, h, c * h.shape[0], axis=0) for s, h in zip(shared, halves)]


def _reduce_group(grads, c, wire, tag):
    part = _reduce_pair(grads, c, wire, tag)
    return _reduce_finish(part, _exchange_chips(part, f"{tag}_chips"), c, tag)


def _my_slot():
    return 2 * lax.axis_index("x") + lax.axis_index("y")


def _fill_slot(slots, mine):
    return lax.dynamic_update_slice_in_dim(slots, mine, _my_slot(), axis=0)


def _full_from_shards(g, axis):
    g = jnp.moveaxis(g, 0, axis)
    return g.reshape(g.shape[:axis] + (g.shape[axis] * g.shape[axis + 1],) + g.shape[axis + 2:])


def _shards_from_full(a, axis):
    a = a.reshape(a.shape[:axis] + (N_CHIPS, a.shape[axis] // N_CHIPS) + a.shape[axis + 1:])
    return jnp.moveaxis(a, axis, 0)


def _ffn_forward(x, xb, w_in, w_out, g, b, alpha, tag):
    gate, up, act = _ffn_in(xb, w_in, f"{tag}_in")
    out, outb, xhat, rstd = _mm_ln(act, w_out, x, g, b, alpha, 0.5, f"{tag}_out_ln")
    return (out, outb), dict(xb=xb, gate=gate, up=up, act=act, xhat=xhat, rstd=rstd)


EXCHANGE_HOSTS = ("dw_out", "dw_in", "dx")


def _exchange_host(name):
    if name in ("ffn1_w_in", "ffn2_w_in", "ffn1_w_out"):
        return "dw_in"
    return "dx" if name.endswith("_w_in") else "dw_out"


def _ffn_backward(dout, saved, w_in, w_out, g, tag, exchange=None):
    exchange = exchange or {}
    dr, dyb, dg, db = _ln_bwd(*dout, saved["xhat"], saved["rstd"], g, 0.5, f"{tag}_ln_bwd")
    dh = _ffn_dact(dyb, w_out, saved["gate"], saved["up"], f"{tag}_dact")
    out = {
        "dw_out": _mm(saved["act"], dyb, "tn", F32, f"{tag}_dw_out", exchange=exchange.get("dw_out") or None),
        "dw_in": _mm(saved["xb"], dh, "tn", F32, f"{tag}_dw_in", split_b=True, exchange=exchange.get("dw_in") or None),
        "dx": _mm(dh, w_in, "nt", F32, f"{tag}_dx", split_a=True, exchange=exchange.get("dx") or None),
    }
    landed = {h: out[h][1] for h in EXCHANGE_HOSTS if exchange.get(h)}
    res = {h: out[h][0] if exchange.get(h) else out[h] for h in EXCHANGE_HOSTS}
    return dr, res["dx"], res["dw_in"], res["dw_out"], dg, db, landed


def _sb_forward(xb, w_in, w_out, tag):
    nw = w_out.shape[0]
    h = _mm(xb, w_in, "nn", MXU_DTYPE, f"{tag}_qkv")
    q, k, v = (_to_heads(h[:, j * nw:(j + 1) * nw]) for j in range(3))
    o, ob = _sb_fwd(q, k, v, f"{tag}_att")
    ob = _from_heads(ob)
    return ob, dict(q=q, k=k, v=v, o=o, ob=ob)


def _sb_backward(dmix, saved, xb, w_in, w_out, tag):
    dw_out = _mm(saved["ob"], dmix, "tn", F32, f"{tag}_dw_out")
    do = _to_heads(_mm(dmix, w_out, "nt", MXU_DTYPE, f"{tag}_do"))
    dq, dk, dv = _sb_bwd(saved["q"], saved["k"], saved["v"], saved["o"], do, f"{tag}_att_bwd")
    dh = jnp.concatenate([_from_heads(t) for t in (dq, dk, dv)], axis=1).astype(MXU_DTYPE)
    dw_in = _mm(xb, dh, "tn", F32, f"{tag}_dw_in")
    return _mm(dh, w_in, "nt", F32, f"{tag}_dx"), dw_in, dw_out


def _swa_forward(xb, w_in, sinks, w_out, tables, tag):
    nq = w_out.shape[0]
    nkv = (w_in.shape[1] - nq) // 2
    h = _mm(xb, w_in, "nn", F32, f"{tag}_qkv")
    n_rot = (nq + nkv) // 128
    hb = _rope(h, *tables, lambda j: j // n_rot, MXU_DTYPE, f"{tag}_rope")
    q, k, v = _to_heads(hb[:, :nq]), _to_heads(hb[:, nq:nq + nkv]), _to_heads(hb[:, nq + nkv:])
    o, lse = _band_fwd(q, k, v, sinks, SWA_WINDOW - 1, f"{tag}_att")
    o, ob = _merge_fwd([o], [lse], f"{tag}_cast")
    ob = _from_heads(ob)
    return ob, dict(q=q, k=k, v=v, o=o, lse=lse, ob=ob, n_rot=n_rot)


def _swa_backward(dmix, saved, xb, w_in, sinks, w_out, tables, tag):
    dw_out = _mm(saved["ob"], dmix, "tn", F32, f"{tag}_dw_out")
    do = _to_heads(_mm(dmix, w_out, "nt", MXU_DTYPE, f"{tag}_do"))
    (dog,), (delta,) = _merge_bwd(do, saved["o"], [saved["lse"]], f"{tag}_delta")
    dq, dk, dv, dsink = _band_bwd(saved["q"], saved["k"], saved["v"], dog, saved["lse"], delta, sinks, SWA_WINDOW - 1, f"{tag}_att_bwd")
    dh = jnp.concatenate([_from_heads(t) for t in (dq, dk, dv)], axis=1)
    n_rot = saved["n_rot"]
    dhb = _rope(dh, tables[0], -tables[1], lambda j: j // n_rot, MXU_DTYPE, f"{tag}_rope_bwd")
    dw_in = _mm(xb, dhb, "tn", F32, f"{tag}_dw_in")
    return _mm(dhb, w_in, "nt", F32, f"{tag}_dx"), dw_in, dw_out, dsink[:, 0, 0]


def _dil_forward(xb, w_in, w_out, tables, tag):
    nw = w_out.shape[0]
    h = _mm(xb, w_in, "nn", F32, f"{tag}_qkv")
    hb = _rope(h, *tables, lambda j: (j % 3) // 2, MXU_DTYPE, f"{tag}_rope", width=nw)
    qkv, outs, lses = [], [], []
    for gi, (win, dil) in enumerate(DIL_GROUPS):
        base = gi * 3 * nw
        q, k, v = (_to_heads(hb[:, base + j * nw:base + (j + 1) * nw], dil) for j in range(3))
        o, lse = _band_fwd(q, k, v, None, win // dil, f"{tag}_att{gi}")
        qkv.append((q, k, v))
        outs.append(_unstride(o, dil))
        lses.append(_unstride(lse, dil))
    o, ob = _merge_fwd(outs, lses, f"{tag}_merge")
    ob = _from_heads(ob)
    return ob, dict(qkv=qkv, o=o, lses=lses, ob=ob)


def _dil_backward(dmix, saved, xb, w_in, w_out, tables, tag):
    dw_out = _mm(saved["ob"], dmix, "tn", F32, f"{tag}_dw_out")
    do = _to_heads(_mm(dmix, w_out, "nt", MXU_DTYPE, f"{tag}_do"))
    dogs, deltas = _merge_bwd(do, saved["o"], saved["lses"], f"{tag}_merge_bwd")
    parts = []
    for gi, (win, dil) in enumerate(DIL_GROUPS):
        q, k, v = saved["qkv"][gi]
        dq, dk, dv, _ = _band_bwd(q, k, v, _restride(dogs[gi], dil), _restride(saved["lses"][gi], dil), _restride(deltas[gi], dil), None, win // dil, f"{tag}_att{gi}_bwd")
        parts += [_from_heads(t, dil) for t in (dq, dk, dv)]
    dh = jnp.concatenate(parts, axis=1)
    dhb = _rope(dh, tables[0], -tables[1], lambda j: (j % 3) // 2, MXU_DTYPE, f"{tag}_rope_bwd", width=w_out.shape[0])
    dw_in = _mm(xb, dhb, "tn", F32, f"{tag}_dw_in")
    return _mm(dhb, w_in, "nt", F32, f"{tag}_dx"), dw_in, dw_out


def kernel(x, p, ffn1_w_in, ffn1_w_out, ffn2_w_in, ffn2_w_out, ln_g, ln_b, sb_w_in, sb_w_out, swa_w_in, swa_sinks, swa_w_out, dil_w_in, dil_w_out, ple_w_proj, ple_w_gate, loss_target, m_ffn1_w_in, m_ffn1_w_out, m_ffn2_w_in, m_ffn2_w_out, m_ln_g, m_ln_b, m_sb_w_in, m_sb_w_out, m_swa_w_in, m_swa_sinks, m_swa_w_out, m_dil_w_in, m_dil_w_out, m_ple_w_proj, m_ple_w_gate, v_ffn1_w_in, v_ffn1_w_out, v_ffn2_w_in, v_ffn2_w_out, v_ln_g, v_ln_b, v_sb_w_in, v_sb_w_out, v_swa_w_in, v_swa_sinks, v_swa_w_out, v_dil_w_in, v_dil_w_out, v_ple_w_proj, v_ple_w_gate):
    shard = dict(ffn1_w_in=ffn1_w_in, ffn1_w_out=ffn1_w_out, ffn2_w_in=ffn2_w_in, ffn2_w_out=ffn2_w_out, ln_g=ln_g, ln_b=ln_b, sb_w_in=sb_w_in, sb_w_out=sb_w_out, swa_w_in=swa_w_in, swa_sinks=swa_sinks, swa_w_out=swa_w_out, dil_w_in=dil_w_in, dil_w_out=dil_w_out, ple_w_proj=ple_w_proj, ple_w_gate=ple_w_gate)
    mom_m = dict(ffn1_w_in=m_ffn1_w_in, ffn1_w_out=m_ffn1_w_out, ffn2_w_in=m_ffn2_w_in, ffn2_w_out=m_ffn2_w_out, ln_g=m_ln_g, ln_b=m_ln_b, sb_w_in=m_sb_w_in, sb_w_out=m_sb_w_out, swa_w_in=m_swa_w_in, swa_sinks=m_swa_sinks, swa_w_out=m_swa_w_out, dil_w_in=m_dil_w_in, dil_w_out=m_dil_w_out, ple_w_proj=m_ple_w_proj, ple_w_gate=m_ple_w_gate)
    mom_v = dict(ffn1_w_in=v_ffn1_w_in, ffn1_w_out=v_ffn1_w_out, ffn2_w_in=v_ffn2_w_in, ffn2_w_out=v_ffn2_w_out, ln_g=v_ln_g, ln_b=v_ln_b, sb_w_in=v_sb_w_in, sb_w_out=v_sb_w_out, swa_w_in=v_swa_w_in, swa_sinks=v_swa_sinks, swa_w_out=v_swa_w_out, dil_w_in=v_dil_w_in, dil_w_out=v_dil_w_out, ple_w_proj=v_ple_w_proj, ple_w_gate=v_ple_w_gate)
    depth = ffn1_w_in.shape[0]
    alpha = (2 * depth) ** 0.25
    c = lax.axis_index("c")

    mats = [n for n in WEIGHTS if n not in SMALL]
    local = [shard[n].astype(MXU_DTYPE) for n in mats] + [_pack_small([ln_g, ln_b], ())[None]]
    got = [_fill_slot(g, a[None]) for g, a in zip(_gather_chips(local, "gather_weights"), local)]
    full = {"swa_sinks": swa_sinks}
    for n, g in zip(mats, got):
        full[n] = _full_from_shards(g, SHARD_AXIS[n])
    for n, g in zip(("ln_g", "ln_b"), _unpack_small(got[-1][:, 0], [ln_g.shape, ln_b.shape], (N_CHIPS,))):
        full[n] = _full_from_shards(g, SHARD_AXIS[n])

    seq = x.shape[1]
    tables = _rope_tables(seq)
    xf = x[0]
    xb = xf.astype(MXU_DTYPE)
    saved = []
    for i in range(depth):
        kind, j = i % 3, i // 3
        mixer = ("sb", "swa", "dil")[kind]
        sv = {}
        (x1, x1b), sv["ffn1"] = _ffn_forward(xf, xb, full["ffn1_w_in"][i], full["ffn1_w_out"][i], full["ln_g"][i, 0], full["ln_b"][i, 0], alpha, f"l{i}_ffn1")
        if kind == 0:
            mix, sv["mix"] = _sb_forward(x1b, full["sb_w_in"][j], full["sb_w_out"][j], f"l{i}_sb")
        elif kind == 1:
            mix, sv["mix"] = _swa_forward(x1b, full["swa_w_in"][j], swa_sinks[j], full["swa_w_out"][j], tables, f"l{i}_swa")
        else:
            mix, sv["mix"] = _dil_forward(x1b, full["dil_w_in"][j], full["dil_w_out"][j], tables, f"l{i}_dil")
        x2, x2b, sv["xhat2"], sv["rstd2"] = _mm_ln(mix, full[f"{mixer}_w_out"][j], x1, full["ln_g"][i, 1], full["ln_b"][i, 1], alpha, 1.0, f"l{i}_{mixer}_proj_ln")
        sv["x1b"] = x1b
        (x3, x3b), sv["ffn2"] = _ffn_forward(x2, x2b, full["ffn2_w_in"][i], full["ffn2_w_out"][i], full["ln_g"][i, 2], full["ln_b"][i, 2], alpha, f"l{i}_ffn2")
        xf, xb, sv["u"], sv["e"] = _ple_fwd(x3, x3b, p[i, 0], full["ple_w_gate"][i], full["ple_w_proj"][i], f"l{i}_ple")
        sv["x3b"] = x3b
        saved.append(sv)

    loss_part, dy = _loss(xf, loss_target[0], "loss")
    loss = lax.psum(loss_part[0, 0], ("x", "y", "c"))

    grads = {n: [None] * full[n].shape[0] for n in WEIGHTS if n not in ("ln_g", "ln_b")}
    gsum = {n: [None] * full[n].shape[0] for n in mats}
    dln_g = [[None] * 3 for _ in range(depth)]
    dln_b = [[None] * 3 for _ in range(depth)]
    dout = (dy, None, 1.0)
    pending = None

    def finish_reduce(layer, part, tag, landed):
        for (n, k), g in zip(layer, _reduce_finish(part, landed, c, tag)):
            gsum[n][k] = g

    for i in reversed(range(depth)):
        kind, j = i % 3, i // 3
        mixer = ("sb", "swa", "dil")[kind]
        sv = saved[i]
        dx4, dub, deb = _ple_bwd(*dout, sv["u"], sv["e"], f"l{i}_ple_bwd")
        grads["ple_w_gate"][i] = _mm(sv["x3b"], dub, "tn", F32, f"l{i}_ple_dw_gate")
        grads["ple_w_proj"][i] = _mm(p[i, 0], deb, "tn", F32, f"l{i}_ple_dw_proj")
        dxb = _mm(dub, full["ple_w_gate"][i], "nt", F32, f"l{i}_ple_dx")
        dr, dxb, grads["ffn2_w_in"][i], grads["ffn2_w_out"][i], dln_g[i][2], dln_b[i][2], landed = _ffn_backward((dx4, dxb, 1.0), sv["ffn2"], full["ffn2_w_in"][i], full["ffn2_w_out"][i], full["ln_g"][i, 2], f"l{i}_ffn2", exchange=pending and {h: [q for (n, _), q in zip(pending[0], pending[1]) if _exchange_host(n) == h] for h in EXCHANGE_HOSTS})
        if pending:
            arrived = {h: iter(v) for h, v in landed.items()}
            finish_reduce(*pending, [next(arrived[_exchange_host(n)]) for n, _ in pending[0]])
        dr, dmix, dln_g[i][1], dln_b[i][1] = _ln_bwd(dr, dxb, alpha, sv["xhat2"], sv["rstd2"], full["ln_g"][i, 1], 1.0, f"l{i}_mix_ln_bwd")
        if kind == 0:
            dxb, grads["sb_w_in"][j], grads["sb_w_out"][j] = _sb_backward(dmix, sv["mix"], sv["x1b"], full["sb_w_in"][j], full["sb_w_out"][j], f"l{i}_sb")
        elif kind == 1:
            dxb, grads["swa_w_in"][j], grads["swa_w_out"][j], grads["swa_sinks"][j] = _swa_backward(dmix, sv["mix"], sv["x1b"], full["swa_w_in"][j], swa_sinks[j], full["swa_w_out"][j], tables, f"l{i}_swa")
        else:
            dxb, grads["dil_w_in"][j], grads["dil_w_out"][j] = _dil_backward(dmix, sv["mix"], sv["x1b"], full["dil_w_in"][j], full["dil_w_out"][j], tables, f"l{i}_dil")
        dr, dxb, grads["ffn1_w_in"][i], grads["ffn1_w_out"][i], dln_g[i][0], dln_b[i][0], _ = _ffn_backward((dr, dxb, alpha), sv["ffn1"], full["ffn1_w_in"][i], full["ffn1_w_out"][i], full["ln_g"][i, 0], f"l{i}_ffn1")
        dout = (dr, dxb, alpha)
        layer = [("ffn1_w_in", i), ("ffn1_w_out", i), (f"{mixer}_w_in", j), (f"{mixer}_w_out", j), ("ffn2_w_in", i), ("ffn2_w_out", i), ("ple_w_proj", i), ("ple_w_gate", i)]
        pending = (layer, _reduce_pair([_shards_from_full(grads[n][k], SHARD_AXIS[n] - 1) for n, k in layer], c, MXU_DTYPE, f"l{i}_reduce"), f"l{i}_reduce")
    layer, part, tag = pending
    finish_reduce(layer, part, tag, _exchange_chips(part, f"{tag}_chips"))
    grad_x = _axpy(*dout, "grad_x")[None]

    gshard = {n: jnp.stack(g) for n, g in gsum.items()}
    small = [
        _shards_from_full(jnp.stack([jnp.concatenate(r, axis=0) for r in dln_g]), SHARD_AXIS["ln_g"]),
        _shards_from_full(jnp.stack([jnp.concatenate(r, axis=0) for r in dln_b]), SHARD_AXIS["ln_b"]),
        jnp.broadcast_to(jnp.stack(grads["swa_sinks"])[None], (N_CHIPS,) + swa_sinks.shape),
    ]
    (small_sum,) = _reduce_group([_pack_small(small, (N_CHIPS,))], c, F32, "small_reduce")
    for n, g in zip(SMALL, _unpack_small(small_sum, [shard[n].shape for n in SMALL], ())):
        gshard[n] = g

    delta, new_m, new_v = {}, {}, {}
    for n in WEIGHTS:
        shp = shard[n].shape
        two_d = (-1, shp[-1])
        d, nm, nv = _adamw(shard[n].reshape(two_d), gshard[n].reshape(two_d), mom_m[n].reshape(two_d), mom_v[n].reshape(two_d), f"adamw_{n}")
        delta[n], new_m[n], new_v[n] = d.reshape(shp), nm.reshape(shp), nv.reshape(shp)

    return (loss, grad_x, *[gshard[n] for n in WEIGHTS], *[delta[n] for n in WEIGHTS], *[new_m[n] for n in WEIGHTS], *[new_v[n] for n in WEIGHTS])
```

```python
import functools
import math

import jax
import jax.numpy as jnp
from jax import lax
from jax.experimental import pallas as pl
from jax.experimental.pallas import tpu as pltpu

F32 = jnp.float32
MXU_DTYPE = jnp.bfloat16
MESH = pl.DeviceIdType.MESH

HEAD_DIM = 64
ATT_BLK = 128
SWA_WINDOW = 128
DIL_GROUPS = ((128, 1), (512, 4), (2048, 16))
LN_EPS = 1e-5
ROPE_THETA = 10000.0
NEG_INF = -1e30
ADAM_LR, ADAM_B1, ADAM_B2, ADAM_EPS, ADAM_WD, ADAM_STEP = 0.001, 0.9, 0.999, 1e-08, 0.01, 10

VMEM_LIMIT_BYTES = 56 * 1024 * 1024
N_CHIPS = 4


def _params(sem=None):
    return pltpu.CompilerParams(dimension_semantics=sem, vmem_limit_bytes=VMEM_LIMIT_BYTES)


def _tile(n, target, unit):
    t = (min(target, n) // unit) * unit
    while t >= unit:
        if n % t == 0:
            return t
        t -= unit
    return n


def _dot(a, b, dims):
    return lax.dot_general(a.astype(MXU_DTYPE), b.astype(MXU_DTYPE), (dims, ((), ())), preferred_element_type=F32)


NN = ((1,), (0,))
NT = ((1,), (1,))
TN = ((0,), (0,))


def _accumulate(part, acc_ref, kk, nk, finish):
    if nk == 1:
        finish(part)
        return

    @pl.when(kk == 0)
    def _():
        acc_ref[...] = jnp.zeros_like(acc_ref)

    acc_ref[...] += part

    @pl.when(kk == nk - 1)
    def _():
        finish(acc_ref[...])


def _mm(a, b, mode, out_dtype, name, split_a=False, split_b=False, exchange=None, tm=1408, tn=1408, tk=1408):
    dims = {"nn": NN, "nt": NT, "tn": TN}[mode]
    if split_a:
        m, k = a.shape[1], 2 * a.shape[2]
    elif mode == "tn":
        k, m = a.shape
    else:
        m, k = a.shape
    if split_b:
        n = 2 * b.shape[2]
    elif mode == "nt":
        n = b.shape[0]
    else:
        n = b.shape[1]
    tm = _tile(m, tm, 128)
    tn = _tile(n // 2 if split_b else n, tn, 128)
    tk = _tile(k // 2 if split_a else k, tk, 128)
    nk = k // tk
    nk_half = nk // 2
    nn_half = (n // tn) // 2

    if split_a:
        a_spec = pl.BlockSpec((None, tm, tk), lambda i, j, kk: (kk // nk_half, i, kk % nk_half))
    elif mode == "tn":
        a_spec = pl.BlockSpec((tk, tm), lambda i, j, kk: (kk, i))
    else:
        a_spec = pl.BlockSpec((tm, tk), lambda i, j, kk: (i, kk))
    if split_b:
        b_spec = pl.BlockSpec((None, tk, tn), lambda i, j, kk: (j // nn_half, kk, j % nn_half))
    elif mode == "nt":
        b_spec = pl.BlockSpec((tn, tk), lambda i, j, kk: (j, kk))
    else:
        b_spec = pl.BlockSpec((tk, tn), lambda i, j, kk: (kk, j))

    parts = [] if exchange is None else list(exchange)
    n_ex = len(parts)
    grid = (m // tm, n // tn, nk)

    def body(*refs):
        a_ref, b_ref = refs[:2]
        p_refs, o_ref, q_refs, acc_ref = refs[2:2 + n_ex], refs[2 + n_ex], refs[3 + n_ex:3 + 2 * n_ex], refs[3 + 2 * n_ex]
        if n_ex:
            step = (pl.program_id(0) * grid[1] + pl.program_id(1)) * nk + pl.program_id(2)

            @pl.when(step == 0)
            def _():
                for cp in _chips_copies(p_refs, q_refs, *refs[4 + 2 * n_ex:])[0]:
                    cp.start()

        def finish(total):
            o_ref[...] = total.astype(o_ref.dtype)

        _accumulate(_dot(a_ref[...], b_ref[...], dims), acc_ref, pl.program_id(2), nk, finish)

        if n_ex:

            @pl.when(step == grid[0] * grid[1] * nk - 1)
            def _():
                _chips_finish(*_chips_copies(p_refs, q_refs, *refs[4 + 2 * n_ex:]))

    out = pl.pallas_call(
        body,
        name=name,
        grid=grid,
        in_specs=[a_spec, b_spec] + [ANY] * n_ex,
        out_specs=[pl.BlockSpec((tm, tn), lambda i, j, kk: (i, j))] + [ANY] * n_ex,
        out_shape=[jax.ShapeDtypeStruct((m, n), out_dtype)] + [jax.ShapeDtypeStruct(p.shape, p.dtype) for p in parts],
        scratch_shapes=[pltpu.VMEM((tm, tn), F32)] + [pltpu.SemaphoreType.DMA((3 * n_ex,))] * (2 if n_ex else 0),
        compiler_params=_params(("arbitrary", "arbitrary", "arbitrary") if n_ex else ("parallel", "parallel", "arbitrary")),
    )(a, b, *parts)
    return (out[0], out[1:]) if n_ex else out[0]


def _sigmoid(x):
    return 1.0 / (1.0 + jnp.exp(-x))


def _ffn_in(xb, w_in, name):
    s, d = xb.shape
    f = w_in.shape[1] // 2
    tm = _tile(s, 512, 128)
    tn = _tile(f, 1408, 128)
    nj = f // tn

    def body(x_ref, wg_ref, wu_ref, g_ref, u_ref, a_ref):
        x = x_ref[...]
        g = _dot(x, wg_ref[...], NN)
        u = _dot(x, wu_ref[...], NN)
        g_ref[...] = g.astype(g_ref.dtype)
        u_ref[...] = u.astype(u_ref.dtype)
        a_ref[...] = (g * _sigmoid(g) * u).astype(a_ref.dtype)

    out = pl.BlockSpec((tm, tn), lambda i, j: (i, j))
    return pl.pallas_call(
        body,
        name=name,
        grid=(s // tm, nj),
        in_specs=[
            pl.BlockSpec((tm, d), lambda i, j: (i, 0)),
            pl.BlockSpec((d, tn), lambda i, j: (0, j)),
            pl.BlockSpec((d, tn), lambda i, j: (0, j + nj)),
        ],
        out_specs=[out, out, out],
        out_shape=[
            jax.ShapeDtypeStruct((s, f), MXU_DTYPE),
            jax.ShapeDtypeStruct((s, f), MXU_DTYPE),
            jax.ShapeDtypeStruct((s, f), MXU_DTYPE),
        ],
        compiler_params=_params(("parallel", "parallel")),
    )(xb, w_in, w_in)


def _ffn_dact(dyb, w_out, gate, up, name):
    s, d = dyb.shape
    f = w_out.shape[0]
    tm = _tile(s, 512, 128)
    tn = _tile(f, 1408, 128)

    def body(dy_ref, w_ref, g_ref, u_ref, o_ref):
        dact = _dot(dy_ref[...], w_ref[...], NT)
        g = g_ref[...].astype(F32)
        sig = _sigmoid(g)
        o_ref[0] = (dact * u_ref[...].astype(F32) * (sig * (1.0 + g * (1.0 - sig)))).astype(o_ref.dtype)
        o_ref[1] = (dact * (g * sig)).astype(o_ref.dtype)

    tile = pl.BlockSpec((tm, tn), lambda i, j: (i, j))
    return pl.pallas_call(
        body,
        name=name,
        grid=(s // tm, f // tn),
        in_specs=[
            pl.BlockSpec((tm, d), lambda i, j: (i, 0)),
            pl.BlockSpec((tn, d), lambda i, j: (j, 0)),
            tile,
            tile,
        ],
        out_specs=pl.BlockSpec((2, tm, tn), lambda i, j: (0, i, j)),
        out_shape=jax.ShapeDtypeStruct((2, s, f), MXU_DTYPE),
        compiler_params=_params(("parallel", "parallel")),
    )(dyb, w_out, gate, up)


def _ple_fwd(x, xb, p, w_gate, w_proj, name):
    s, d = x.shape
    pd = p.shape[1]
    tm = _tile(s, 1024, 128)
    tn = _tile(d, 512, 128)

    def body(x_ref, xb_ref, p_ref, wg_ref, wp_ref, o_ref, ob_ref, u_ref, e_ref):
        u = _dot(xb_ref[...], wg_ref[...], NN)
        e = _dot(p_ref[...], wp_ref[...], NN)
        out = x_ref[...] + _sigmoid(u) * e
        o_ref[...] = out
        ob_ref[...] = out.astype(ob_ref.dtype)
        u_ref[...] = u
        e_ref[...] = e

    tile = pl.BlockSpec((tm, tn), lambda i, j: (i, j))
    return pl.pallas_call(
        body,
        name=name,
        grid=(s // tm, d // tn),
        in_specs=[
            tile,
            pl.BlockSpec((tm, d), lambda i, j: (i, 0)),
            pl.BlockSpec((tm, pd), lambda i, j: (i, 0)),
            pl.BlockSpec((d, tn), lambda i, j: (0, j)),
            pl.BlockSpec((pd, tn), lambda i, j: (0, j)),
        ],
        out_specs=[tile, tile, tile, tile],
        out_shape=[
            jax.ShapeDtypeStruct((s, d), F32),
            jax.ShapeDtypeStruct((s, d), MXU_DTYPE),
            jax.ShapeDtypeStruct((s, d), F32),
            jax.ShapeDtypeStruct((s, d), F32),
        ],
        compiler_params=_params(("parallel", "parallel")),
    )(x, xb, p, w_gate, w_proj)


def _rows_spec(ts, d):
    return pl.BlockSpec((ts, d), lambda i: (i, 0))


def _mm_ln(a, w, x, g, b, alpha, beta, name):
    s, k = a.shape
    d = w.shape[1]
    tm = _tile(s, 512, 128)
    tk = _tile(k, 1408, 128)
    nk = k // tk

    def body(a_ref, w_ref, x_ref, g_ref, b_ref, o_ref, ob_ref, xh_ref, rs_ref, acc_ref):
        def finish(y):
            r = alpha * x_ref[...] + beta * y
            mu = jnp.mean(r, axis=1, keepdims=True)
            cen = r - mu
            var = jnp.mean(cen * cen, axis=1, keepdims=True)
            rstd = lax.rsqrt(var + LN_EPS)
            xhat = cen * rstd
            out = xhat * g_ref[...] + b_ref[...]
            o_ref[...] = out
            ob_ref[...] = out.astype(ob_ref.dtype)
            xh_ref[...] = xhat
            rs_ref[...] = rstd

        _accumulate(_dot(a_ref[...], w_ref[...], NN), acc_ref, pl.program_id(1), nk, finish)

    rows = pl.BlockSpec((tm, d), lambda i, kk: (i, 0))
    vec = pl.BlockSpec((1, d), lambda i, kk: (0, 0))
    return pl.pallas_call(
        body,
        name=name,
        grid=(s // tm, nk),
        in_specs=[pl.BlockSpec((tm, tk), lambda i, kk: (i, kk)), pl.BlockSpec((tk, d), lambda i, kk: (kk, 0)), rows, vec, vec],
        out_specs=[rows, rows, rows, pl.BlockSpec((tm, 1), lambda i, kk: (i, 0))],
        out_shape=[
            jax.ShapeDtypeStruct((s, d), F32),
            jax.ShapeDtypeStruct((s, d), MXU_DTYPE),
            jax.ShapeDtypeStruct((s, d), F32),
            jax.ShapeDtypeStruct((s, 1), F32),
        ],
        scratch_shapes=[pltpu.VMEM((tm, d), F32)],
        compiler_params=_params(("parallel", "arbitrary")),
    )(a, w, x, g.reshape(1, d), b.reshape(1, d))


def _ln_bwd(ga, gb, ca, xhat, rstd, g, beta, name):
    s, d = xhat.shape
    ts = _tile(s, 512, 8)

    def body(ga_ref, gb_ref, xh_ref, rs_ref, g_ref, dr_ref, dyb_ref, dg_ref, db_ref):
        @pl.when(pl.program_id(0) == 0)
        def _():
            dg_ref[...] = jnp.zeros_like(dg_ref)
            db_ref[...] = jnp.zeros_like(db_ref)

        dout = ca * ga_ref[...] + gb_ref[...]
        xhat = xh_ref[...]
        dg_ref[...] += jnp.sum(dout * xhat, axis=0, keepdims=True)
        db_ref[...] += jnp.sum(dout, axis=0, keepdims=True)
        dxh = dout * g_ref[...]
        m1 = jnp.mean(dxh, axis=1, keepdims=True)
        m2 = jnp.mean(dxh * xhat, axis=1, keepdims=True)
        dr = rs_ref[...] * (dxh - m1 - xhat * m2)
        dr_ref[...] = dr
        dyb_ref[...] = (beta * dr).astype(dyb_ref.dtype)

    vec = pl.BlockSpec((1, d), lambda i: (0, 0))
    return pl.pallas_call(
        body,
        name=name,
        grid=(s // ts,),
        in_specs=[_rows_spec(ts, d), _rows_spec(ts, d), _rows_spec(ts, d), _rows_spec(ts, 1), vec],
        out_specs=[_rows_spec(ts, d), _rows_spec(ts, d), vec, vec],
        out_shape=[
            jax.ShapeDtypeStruct((s, d), F32),
            jax.ShapeDtypeStruct((s, d), MXU_DTYPE),
            jax.ShapeDtypeStruct((1, d), F32),
            jax.ShapeDtypeStruct((1, d), F32),
        ],
        compiler_params=_params(("arbitrary",)),
    )(ga, gb, xhat, rstd, g.reshape(1, d))


def _ple_bwd(ga, gb, ca, u, e, name):
    s, d = u.shape
    ts = _tile(s, 512, 8)
    grads = [ga] if gb is None else [ga, gb]

    def body(*refs):
        u_ref, e_ref, dx_ref, du_ref, de_ref = refs[len(grads):]
        dx = ca * refs[0][...]
        if gb is not None:
            dx = dx + refs[1][...]
        sig = _sigmoid(u_ref[...])
        dx_ref[...] = dx
        du_ref[...] = (dx * e_ref[...] * sig * (1.0 - sig)).astype(du_ref.dtype)
        de_ref[...] = (dx * sig).astype(de_ref.dtype)

    return pl.pallas_call(
        body,
        name=name,
        grid=(s // ts,),
        in_specs=[_rows_spec(ts, d)] * (len(grads) + 2),
        out_specs=[_rows_spec(ts, d)] * 3,
        out_shape=[
            jax.ShapeDtypeStruct((s, d), F32),
            jax.ShapeDtypeStruct((s, d), MXU_DTYPE),
            jax.ShapeDtypeStruct((s, d), MXU_DTYPE),
        ],
        compiler_params=_params(("parallel",)),
    )(*grads, u, e)


def _axpy(ga, gb, ca, name):
    s, d = ga.shape
    ts = _tile(s, 512, 8)

    def body(ga_ref, gb_ref, o_ref):
        o_ref[...] = ca * ga_ref[...] + gb_ref[...]

    return pl.pallas_call(
        body,
        name=name,
        grid=(s // ts,),
        in_specs=[_rows_spec(ts, d)] * 2,
        out_specs=_rows_spec(ts, d),
        out_shape=jax.ShapeDtypeStruct((s, d), F32),
        compiler_params=_params(("parallel",)),
    )(ga, gb)


def _loss(y, target, name):
    s, d = y.shape
    ts = _tile(s, 512, 8)

    def body(y_ref, t_ref, l_ref, dy_ref):
        @pl.when(pl.program_id(0) == 0)
        def _():
            l_ref[...] = jnp.zeros_like(l_ref)

        err = y_ref[...] - t_ref[...]
        l_ref[...] += (0.5 / d) * jnp.sum(jnp.sum(err * err, axis=1, keepdims=True), axis=0, keepdims=True)
        dy_ref[...] = err * (1.0 / d)

    return pl.pallas_call(
        body,
        name=name,
        grid=(s // ts,),
        in_specs=[_rows_spec(ts, d)] * 2,
        out_specs=[pl.BlockSpec((1, 1), lambda i: (0, 0)), _rows_spec(ts, d)],
        out_shape=[jax.ShapeDtypeStruct((1, 1), F32), jax.ShapeDtypeStruct((s, d), F32)],
        compiler_params=_params(("arbitrary",)),
    )(y, target)


def _rope_tables(seq):
    pos = jnp.arange(seq, dtype=F32)
    inv = ROPE_THETA ** (-jnp.arange(0, HEAD_DIM, 2, dtype=F32) / HEAD_DIM)
    ang = pos[:, None] * inv[None, :]
    cos, sin = jnp.cos(ang), jnp.sin(ang)
    cos2 = jnp.concatenate([cos, cos, cos, cos, jnp.ones((seq, 128), F32)], axis=1)
    sin2 = jnp.concatenate([-sin, sin, -sin, sin, jnp.zeros((seq, 128), F32)], axis=1)
    return cos2, sin2


def _rope(h, cos2, sin2, plain_block, out_dtype, name, width=128):
    s, n = h.shape
    ts = _tile(s, 512, 8)

    def body(h_ref, c_ref, s_ref, o_ref):
        cos, sin = c_ref[...], s_ref[...]
        lane = lax.broadcasted_iota(jnp.int32, cos.shape, 1)
        first_half = lane % HEAD_DIM < HEAD_DIM // 2
        for w in range(width // 128):
            cols = slice(w * 128, (w + 1) * 128)
            x = h_ref[:, cols].astype(F32)
            partner = jnp.where(first_half, pltpu.roll(x, 128 - HEAD_DIM // 2, 1), pltpu.roll(x, HEAD_DIM // 2, 1))
            o_ref[:, cols] = (x * cos + partner * sin).astype(o_ref.dtype)

    tile = pl.BlockSpec((ts, width), lambda i, j: (i, j))
    table = pl.BlockSpec((ts, 128), lambda i, j: (i, plain_block(j)))
    return pl.pallas_call(
        body,
        name=name,
        grid=(s // ts, n // width),
        in_specs=[tile, table, table],
        out_specs=tile,
        out_shape=jax.ShapeDtypeStruct((s, n), out_dtype),
        compiler_params=_params(("parallel", "parallel")),
    )(h, cos2, sin2)


SB_TQ = 512
SB_BLK = 128
SB_WALK = 2
SB_CLOSED = 110.0


def _sb_walk(n_trips, carry, key_blocks):
    def still_open(carry):
        lowest = functools.reduce(jnp.minimum, [c[0] for c in carry])
        return jnp.min(lowest) < SB_CLOSED

    def cond(state):
        t, go, _ = state
        return jnp.logical_and(t < n_trips, go)

    def body(state):
        t, _, carry = state
        carry = key_blocks(t, carry)
        return t + 1, still_open(carry), carry

    return lax.while_loop(cond, body, (jnp.int32(0), still_open(carry), carry))[2]


def _tri2(strict):
    row = lax.broadcasted_iota(jnp.int32, (2 * SB_BLK, SB_BLK), 0) % SB_BLK
    col = lax.broadcasted_iota(jnp.int32, (2 * SB_BLK, SB_BLK), 1)
    return (row > col if strict else row >= col).astype(MXU_DTYPE)


def _cumsum_dot(x, tri2):
    hi = x.astype(MXU_DTYPE)
    lo = x - hi.astype(F32)
    return _dot(jnp.concatenate([hi, lo.astype(MXU_DTYPE)], axis=1), tri2, NN)


def _sb_logits(z, valid):
    l1p = jnp.log(1.0 + jnp.exp(-jnp.abs(z)))
    sp = jnp.maximum(z, 0.0) + l1p
    ls = z - sp
    if valid is not None:
        sp = jnp.where(valid, sp, 0.0)
    return sp, ls


def _sb_weights(ls, after, valid):
    a = jnp.exp(ls - after)
    return a if valid is None else jnp.where(valid, a, 0.0)


def _sb_setup(q_ref, k_ref, v_ref, n_sub, scale):
    qs = [q_ref[0, u * SB_BLK:(u + 1) * SB_BLK, :] * scale for u in range(n_sub)]
    row = lax.broadcasted_iota(jnp.int32, (SB_BLK, SB_BLK), 0)
    col = lax.broadcasted_iota(jnp.int32, (SB_BLK, SB_BLK), 1)

    def load(jj):
        start = pl.multiple_of(jj * SB_BLK, SB_BLK)
        return start, k_ref[0, pl.ds(start, SB_BLK), :], v_ref[0, pl.ds(start, SB_BLK), :]

    return qs, col < row, load


def _sb_fwd(q, k, v, name):
    nh, s, dh = q.shape
    tq = min(SB_TQ, s)
    n_sub = tq // SB_BLK
    scale = dh ** -0.5

    def body(q_ref, k_ref, v_ref, o_ref, ob_ref):
        base = pl.program_id(1) * n_sub
        qs, diag_valid, load = _sb_setup(q_ref, k_ref, v_ref, n_sub, scale)
        tri_after = _tri2(True)

        def key_blocks(first, carry, diagonal):
            blocks = list(range((n_sub if diagonal else SB_WALK) - 1, -1, -1))
            kv = {d: load(first + d) for d in blocks}
            tiles = [(d, u) for d in blocks for u in range(d if diagonal else 0, n_sub)]
            valid = {t: diag_valid if diagonal and t[0] == t[1] else None for t in tiles}
            z = {t: _dot(qs[t[1]], kv[t[0]][1], NT) for t in tiles}
            sp_ls = {t: _sb_logits(z[t], valid[t]) for t in tiles}
            inside = {t: _cumsum_dot(sp_ls[t][0], tri_after) for t in tiles}
            carry = list(carry)
            for t in tiles:
                after_c, acc = carry[t[1]]
                sp, ls = sp_ls[t]
                a = _sb_weights(ls, after_c + inside[t], valid[t])
                carry[t[1]] = (after_c + jnp.sum(sp, axis=1, keepdims=True), acc + _dot(a, kv[t[0]][2], NN))
            return tuple(carry)

        carry = tuple((jnp.zeros((SB_BLK, 1), F32), jnp.zeros((SB_BLK, dh), F32)) for _ in range(n_sub))
        carry = key_blocks(base, carry, True)
        carry = _sb_walk(base // SB_WALK, carry, lambda t, c: key_blocks(base - SB_WALK * (t + 1), c, False))
        for u in range(n_sub):
            rows = slice(u * SB_BLK, (u + 1) * SB_BLK)
            o_ref[0, rows, :] = carry[u][1]
            ob_ref[0, rows, :] = carry[u][1].astype(ob_ref.dtype)

    blk = pl.BlockSpec((1, tq, dh), lambda h, i: (h, i, 0))
    full = pl.BlockSpec((1, s, dh), lambda h, i: (h, 0, 0))
    return pl.pallas_call(
        body,
        name=name,
        grid=(nh, s // tq),
        in_specs=[blk, full, full],
        out_specs=[blk, blk],
        out_shape=[jax.ShapeDtypeStruct((nh, s, dh), F32), jax.ShapeDtypeStruct((nh, s, dh), MXU_DTYPE)],
        compiler_params=_params(("parallel", "parallel")),
    )(q, k, v)


def _sb_bwd(q, k, v, o, do, name):
    nh, s, dh = q.shape
    tq = min(SB_TQ, s)
    n_sub = tq // SB_BLK
    scale = dh ** -0.5

    def body(q_ref, k_ref, v_ref, o_ref, do_ref, dq_ref, dk_ref, dv_ref):
        i = pl.program_id(1)
        base = i * n_sub

        @pl.when(i == 0)
        def _():
            dk_ref[...] = jnp.zeros_like(dk_ref)
            dv_ref[...] = jnp.zeros_like(dv_ref)

        qs, diag_valid, load = _sb_setup(q_ref, k_ref, v_ref, n_sub, scale)
        dob = [do_ref[0, u * SB_BLK:(u + 1) * SB_BLK, :] for u in range(n_sub)]
        total = [jnp.sum(dob[u].astype(F32) * o_ref[0, u * SB_BLK:(u + 1) * SB_BLK, :], axis=1, keepdims=True) for u in range(n_sub)]
        tri_after = _tri2(True)
        tri_from = _tri2(False)

        def key_blocks(first, carry, diagonal):
            blocks = list(range((n_sub if diagonal else SB_WALK) - 1, -1, -1))
            kv = {d: load(first + d) for d in blocks}
            tiles = [(d, u) for d in blocks for u in range(d if diagonal else 0, n_sub)]
            valid = {t: diag_valid if diagonal and t[0] == t[1] else None for t in tiles}
            z = {t: _dot(qs[t[1]], kv[t[0]][1], NT) for t in tiles}
            da = {t: _dot(dob[t[1]], kv[t[0]][2], NT) for t in tiles}
            sp_ls = {t: _sb_logits(z[t], valid[t]) for t in tiles}
            inside = {t: _cumsum_dot(sp_ls[t][0], tri_after) for t in tiles}
            after_run = [c[0] for c in carry]
            ab, dl = {}, {}
            for t in tiles:
                sp, ls = sp_ls[t]
                ab[t] = _sb_weights(ls, after_run[t[1]] + inside[t], valid[t]).astype(MXU_DTYPE)
                dl[t] = ab[t].astype(F32) * da[t]
                after_run[t[1]] = after_run[t[1]] + jnp.sum(sp, axis=1, keepdims=True)
            from_in = {t: _cumsum_dot(dl[t], tri_from) for t in tiles}
            from_run = [c[1] for c in carry]
            dq = [c[2] for c in carry]
            for d in blocks:
                start, kb, _ = kv[d]
                dk = dv = None
                for u in range(d if diagonal else 0, n_sub):
                    t = (d, u)
                    dz = dl[t] - jnp.exp(sp_ls[t][1]) * (dl[t] + total[u] - (from_run[u] + from_in[t]))
                    if valid[t] is not None:
                        dz = jnp.where(valid[t], dz, 0.0)
                    dzb = dz.astype(MXU_DTYPE)
                    from_run[u] = from_run[u] + jnp.sum(dl[t], axis=1, keepdims=True)
                    dq[u] = dq[u] + _dot(dzb, kb, NN)
                    dk_u, dv_u = _dot(dzb, qs[u], TN), _dot(ab[t], dob[u], TN)
                    dk = dk_u if dk is None else dk + dk_u
                    dv = dv_u if dv is None else dv + dv_u
                dk_ref[0, pl.ds(start, SB_BLK), :] += dk
                dv_ref[0, pl.ds(start, SB_BLK), :] += dv
            return tuple(zip(after_run, from_run, dq))

        carry = tuple((jnp.zeros((SB_BLK, 1), F32), jnp.zeros((SB_BLK, 1), F32), jnp.zeros((SB_BLK, dh), F32)) for _ in range(n_sub))
        carry = key_blocks(base, carry, True)
        carry = _sb_walk(base // SB_WALK, carry, lambda t, c: key_blocks(base - SB_WALK * (t + 1), c, False))
        for u in range(n_sub):
            dq_ref[0, u * SB_BLK:(u + 1) * SB_BLK, :] = carry[u][2] * scale

    blk = pl.BlockSpec((1, tq, dh), lambda h, i: (h, i, 0))
    full = pl.BlockSpec((1, s, dh), lambda h, i: (h, 0, 0))
    shape = jax.ShapeDtypeStruct((nh, s, dh), F32)
    return pl.pallas_call(
        body,
        name=name,
        grid=(nh, s // tq),
        in_specs=[blk, full, full, blk, blk],
        out_specs=[blk, full, full],
        out_shape=[shape, shape, shape],
        compiler_params=_params(("parallel", "arbitrary")),
    )(q, k, v, o, do)


BAND_TQ = 2048


def _band_scores(q_ref, k_ref, i, sub, tq, length, max_dist, scale):
    t0 = i * tq + sub * ATT_BLK
    ks = pl.multiple_of(jnp.minimum(jnp.maximum(t0 - ATT_BLK, 0), length - 2 * ATT_BLK), ATT_BLK)
    qs = q_ref[0, sub * ATT_BLK:(sub + 1) * ATT_BLK, :] * scale
    kw = k_ref[0, pl.ds(ks, 2 * ATT_BLK), :]
    sc = _dot(qs, kw, NT)
    diff = (t0 + lax.broadcasted_iota(jnp.int32, sc.shape, 0)) - (ks + lax.broadcasted_iota(jnp.int32, sc.shape, 1))
    valid = (diff >= 0) & (diff <= max_dist)
    return ks, qs, kw, jnp.where(valid, sc, NEG_INF)


def _band_fwd(q, k, v, sinks, max_dist, name):
    bq, length, dh = q.shape
    group = bq // k.shape[0]
    tq = min(BAND_TQ, length)
    scale = dh ** -0.5
    n_sink = 0 if sinks is None else sinks.shape[0]

    def body(*refs):
        if n_sink:
            sink_ref, q_ref, k_ref, v_ref, o_ref, lse_ref = refs
            sink = sink_ref[pl.program_id(0) % n_sink]
        else:
            q_ref, k_ref, v_ref, o_ref, lse_ref = refs
        i = pl.program_id(1)
        scores = [_band_scores(q_ref, k_ref, i, sub, tq, length, max_dist, scale) for sub in range(tq // ATT_BLK)]
        for sub, (ks, _, _, sc) in enumerate(scores):
            m = jnp.max(sc, axis=1, keepdims=True)
            if n_sink:
                m = jnp.maximum(m, sink)
            e = jnp.exp(sc - m)
            den = jnp.sum(e, axis=1, keepdims=True)
            if n_sink:
                den = den + jnp.exp(sink - m)
            rows = slice(sub * ATT_BLK, (sub + 1) * ATT_BLK)
            o_ref[0, rows, :] = _dot(e / den, v_ref[0, pl.ds(ks, 2 * ATT_BLK), :], NN)
            lse_ref[0, rows, :] = m + jnp.log(den)

    qblk = pl.BlockSpec((1, tq, dh), lambda b, i: (b, i, 0))
    kfull = pl.BlockSpec((1, length, dh), lambda b, i: (b // group, 0, 0))
    in_specs = [qblk, kfull, kfull]
    args = [q, k, v]
    if n_sink:
        in_specs = [pl.BlockSpec(memory_space=pltpu.SMEM)] + in_specs
        args = [sinks] + args
    return pl.pallas_call(
        body,
        name=name,
        grid=(bq, length // tq),
        in_specs=in_specs,
        out_specs=[qblk, pl.BlockSpec((1, tq, 1), lambda b, i: (b, i, 0))],
        out_shape=[jax.ShapeDtypeStruct((bq, length, dh), F32), jax.ShapeDtypeStruct((bq, length, 1), F32)],
        compiler_params=_params(("parallel", "parallel")),
    )(*args)


def _band_bwd(q, k, v, do, lse, delta, sinks, max_dist, name):
    bq, length, dh = q.shape
    group = bq // k.shape[0]
    tq = min(BAND_TQ, length)
    scale = dh ** -0.5
    n_sink = 0 if sinks is None else sinks.shape[0]

    def body(*refs):
        if n_sink:
            sink_ref, q_ref, k_ref, v_ref, do_ref, lse_ref, dl_ref, dq_ref, dk_ref, dv_ref, ds_ref = refs
            sink = sink_ref[pl.program_id(0) % n_sink]
        else:
            q_ref, k_ref, v_ref, do_ref, lse_ref, dl_ref, dq_ref, dk_ref, dv_ref, ds_ref = refs
        i = pl.program_id(1)

        @pl.when((i == 0) & (pl.program_id(0) % group == 0))
        def _():
            dk_ref[...] = jnp.zeros_like(dk_ref)
            dv_ref[...] = jnp.zeros_like(dv_ref)

        @pl.when(i == 0)
        def _():
            ds_ref[...] = jnp.zeros_like(ds_ref)

        scores = [_band_scores(q_ref, k_ref, i, sub, tq, length, max_dist, scale) for sub in range(tq // ATT_BLK)]
        dps = [_dot(do_ref[0, sub * ATT_BLK:(sub + 1) * ATT_BLK, :], v_ref[0, pl.ds(ks, 2 * ATT_BLK), :], NT) for sub, (ks, _, _, _) in enumerate(scores)]
        for sub, (ks, qs, kw, sc) in enumerate(scores):
            rows = slice(sub * ATT_BLK, (sub + 1) * ATT_BLK)
            lse = lse_ref[0, rows, :]
            delta_r = dl_ref[0, rows, :]
            dob = do_ref[0, rows, :]
            p = jnp.exp(sc - lse)
            dsb = (p * (dps[sub] - delta_r)).astype(MXU_DTYPE)
            dq_ref[0, rows, :] = _dot(dsb, kw, NN) * scale
            dk_ref[0, pl.ds(ks, 2 * ATT_BLK), :] += _dot(dsb, qs, TN)
            dv_ref[0, pl.ds(ks, 2 * ATT_BLK), :] += _dot(p, dob, TN)
            if n_sink:
                ds_ref[...] += jnp.sum(-jnp.exp(sink - lse) * delta_r, axis=0, keepdims=True)

    qblk = pl.BlockSpec((1, tq, dh), lambda b, i: (b, i, 0))
    qcol = pl.BlockSpec((1, tq, 1), lambda b, i: (b, i, 0))
    kfull = pl.BlockSpec((1, length, dh), lambda b, i: (b // group, 0, 0))
    in_specs = [qblk, kfull, kfull, qblk, qcol, qcol]
    args = [q, k, v, do, lse, delta]
    if n_sink:
        in_specs = [pl.BlockSpec(memory_space=pltpu.SMEM)] + in_specs
        args = [sinks] + args
    kshape = jax.ShapeDtypeStruct((k.shape[0], length, dh), F32)
    return pl.pallas_call(
        body,
        name=name,
        grid=(bq, length // tq),
        in_specs=in_specs,
        out_specs=[qblk, kfull, kfull, pl.BlockSpec((1, 8, 128), lambda b, i: (b, 0, 0))],
        out_shape=[jax.ShapeDtypeStruct((bq, length, dh), F32), kshape, kshape, jax.ShapeDtypeStruct((bq, 8, 128), F32)],
        compiler_params=_params(("arbitrary", "arbitrary")),
    )(*args)


def _merge_weights(lse_refs):
    lses = [r[0] for r in lse_refs]
    m = functools.reduce(jnp.maximum, lses)
    es = [jnp.exp(l - m) for l in lses]
    den = functools.reduce(lambda a, b: a + b, es)
    return [e / den for e in es]


def _merge_fwd(outs, lses, name):
    n = len(outs)
    nh, s, dh = outs[0].shape
    ts = _tile(s, 1024, 8)

    def body(*refs):
        ws = _merge_weights(refs[n:2 * n])
        o = functools.reduce(lambda a, b: a + b, [w * r[0] for w, r in zip(ws, refs[:n])])
        refs[2 * n][0] = o
        refs[2 * n + 1][0] = o.astype(MXU_DTYPE)

    blk = pl.BlockSpec((1, ts, dh), lambda h, i: (h, i, 0))
    col = pl.BlockSpec((1, ts, 1), lambda h, i: (h, i, 0))
    return pl.pallas_call(
        body,
        name=name,
        grid=(nh, s // ts),
        in_specs=[blk] * n + [col] * n,
        out_specs=[blk, blk],
        out_shape=[jax.ShapeDtypeStruct((nh, s, dh), F32), jax.ShapeDtypeStruct((nh, s, dh), MXU_DTYPE)],
        compiler_params=_params(("parallel", "parallel")),
    )(*outs, *lses)


def _merge_bwd(do, o, lses, name):
    n = len(lses)
    nh, s, dh = o.shape
    ts = _tile(s, 1024, 8)

    def body(*refs):
        do_ref, o_ref = refs[:2]
        ws = _merge_weights(refs[2:2 + n])
        dof = do_ref[0].astype(F32)
        base = jnp.sum(dof * o_ref[0], axis=1, keepdims=True)
        for g in range(n):
            refs[2 + n + g][0] = (ws[g] * dof).astype(MXU_DTYPE)
            refs[2 + 2 * n + g][0] = ws[g] * base

    blk = pl.BlockSpec((1, ts, dh), lambda h, i: (h, i, 0))
    col = pl.BlockSpec((1, ts, 1), lambda h, i: (h, i, 0))
    res = pl.pallas_call(
        body,
        name=name,
        grid=(nh, s // ts),
        in_specs=[blk, blk] + [col] * n,
        out_specs=[blk] * n + [col] * n,
        out_shape=[jax.ShapeDtypeStruct((nh, s, dh), MXU_DTYPE)] * n + [jax.ShapeDtypeStruct((nh, s, 1), F32)] * n,
        compiler_params=_params(("parallel", "parallel")),
    )(do, o, *lses)
    return res[:n], res[n:]


def _to_heads(a, dil=1):
    s, n = a.shape
    nh = n // HEAD_DIM
    a = a.reshape(s // dil, dil, nh, HEAD_DIM).transpose(1, 2, 0, 3)
    return a.reshape(dil * nh, s // dil, HEAD_DIM)


def _from_heads(a, dil=1):
    b, ls, c = a.shape
    nh = b // dil
    return a.reshape(dil, nh, ls, c).transpose(2, 0, 1, 3).reshape(ls * dil, nh * c)


def _restride(a, dil):
    nh, s, c = a.shape
    return a.reshape(nh, s // dil, dil, c).transpose(2, 0, 1, 3).reshape(dil * nh, s // dil, c)


def _unstride(a, dil):
    b, ls, c = a.shape
    nh = b // dil
    return a.reshape(dil, nh, ls, c).transpose(1, 2, 0, 3).reshape(nh, ls * dil, c)


ANY = pl.BlockSpec(memory_space=pl.ANY)


def _position():
    x, y, c = lax.axis_index("x"), lax.axis_index("y"), lax.axis_index("c")
    return x, y, c, [(1 - x, y), (x, 1 - y), (1 - x, 1 - y)]


def _remote(src, dst, send_sems, recv_sems, k, to):
    return pltpu.make_async_remote_copy(src_ref=src, dst_ref=dst, send_sem=send_sems.at[k], recv_sem=recv_sems.at[k], device_id=to, device_id_type=MESH)


def _gather_chips(arrs, name):
    n = len(arrs)

    def body(*refs):
        f_refs, g_refs = refs[:n], refs[n:2 * n]
        send_sems, recv_sems = refs[2 * n:]
        x, y, c, chips = _position()
        me, sibling = (x, y, c), (x, y, 1 - c)

        def half(j, ref, sel):
            rows = f_refs[j].shape[1] // 2
            return ref.at[:, pl.ds(sel * rows, rows), :]

        def slot(j, chip, sel):
            return half(j, g_refs[j].at[2 * chip[0] + chip[1]], sel)

        first = [_remote(half(j, f_refs[j], c), slot(j, (x, y), c), send_sems, recv_sems, 6 * j + k, (*chip, c)) for j in range(n) for k, chip in enumerate(chips)]
        for cp in first:
            cp.start()
        passed = []
        for j in range(n):
            for k, chip in enumerate(chips):
                _remote(slot(j, chip, c), slot(j, chip, c), send_sems, recv_sems, 6 * j + k, me).wait_recv()
                passed.append(_remote(slot(j, chip, c), slot(j, chip, c), send_sems, recv_sems, 6 * j + 3 + k, sibling))
                passed[-1].start()
        for j in range(n):
            for k, chip in enumerate(chips):
                _remote(slot(j, chip, 1 - c), slot(j, chip, 1 - c), send_sems, recv_sems, 6 * j + 3 + k, me).wait_recv()
        for cp in first + passed:
            cp.wait_send()

    return pl.pallas_call(
        body,
        name=name,
        in_specs=[ANY] * n,
        out_specs=[ANY] * n,
        out_shape=[jax.ShapeDtypeStruct((N_CHIPS,) + a.shape, a.dtype) for a in arrs],
        scratch_shapes=[pltpu.SemaphoreType.DMA((6 * n,)), pltpu.SemaphoreType.DMA((6 * n,))],
    )(*arrs)


def _swap_sibling(arrs, name):
    n = len(arrs)

    def body(*refs):
        x, y, c, _ = _position()
        send_sems, recv_sems = refs[2 * n:]
        copies = [_remote(refs[j], refs[n + j], send_sems, recv_sems, j, (x, y, 1 - c)) for j in range(n)]
        for cp in copies:
            cp.start()
        for cp in copies:
            cp.wait()

    return pl.pallas_call(
        body,
        name=name,
        in_specs=[ANY] * n,
        out_specs=[ANY] * n,
        out_shape=[jax.ShapeDtypeStruct(a.shape, a.dtype) for a in arrs],
        scratch_shapes=[pltpu.SemaphoreType.DMA((n,)), pltpu.SemaphoreType.DMA((n,))],
    )(*arrs)


def _chips_copies(p_refs, q_refs, send_sems, recv_sems):
    x, y, c, chips = _position()
    my_slot = 2 * x + y
    n = len(p_refs)
    sends = [_remote(p_refs[j].at[2 * px + py], q_refs[j].at[my_slot], send_sems, recv_sems, 3 * j + k, (px, py, c)) for j in range(n) for k, (px, py) in enumerate(chips)]
    arrivals = [_remote(q_refs[j].at[2 * px + py], q_refs[j].at[2 * px + py], send_sems, recv_sems, 3 * j + k, (x, y, c)) for j in range(n) for k, (px, py) in enumerate(chips)]
    return sends, arrivals


def _chips_finish(sends, arrivals):
    for cp in arrivals:
        cp.wait_recv()
    for cp in sends:
        cp.wait_send()


def _exchange_chips(parts, name):
    n = len(parts)

    def body(*refs):
        sends, arrivals = _chips_copies(refs[:n], refs[n:2 * n], *refs[2 * n:])
        for cp in sends:
            cp.start()
        _chips_finish(sends, arrivals)

    return pl.pallas_call(
        body,
        name=name,
        in_specs=[ANY] * n,
        out_specs=[ANY] * n,
        out_shape=[jax.ShapeDtypeStruct(a.shape, a.dtype) for a in parts],
        scratch_shapes=[pltpu.SemaphoreType.DMA((3 * n,)), pltpu.SemaphoreType.DMA((3 * n,))],
    )(*parts)


def _share_sibling(halves, name):
    n = len(halves)

    def body(*refs):
        h_refs, o_refs = refs[:n], refs[n:2 * n]
        send_sems, recv_sems = refs[2 * n:]
        x, y, c, _ = _position()

        def rows(j, sel):
            r = h_refs[j].shape[0]
            return o_refs[j].at[pl.ds(sel * r, r), :]

        sends = [_remote(h_refs[j], rows(j, c), send_sems, recv_sems, j, (x, y, 1 - c)) for j in range(n)]
        for cp in sends:
            cp.start()
        for j in range(n):
            _remote(rows(j, 1 - c), rows(j, 1 - c), send_sems, recv_sems, j, (x, y, c)).wait_recv()
        for cp in sends:
            cp.wait_send()

    return pl.pallas_call(
        body,
        name=name,
        in_specs=[ANY] * n,
        out_specs=[ANY] * n,
        out_shape=[jax.ShapeDtypeStruct((2 * a.shape[0],) + a.shape[1:], a.dtype) for a in halves],
        scratch_shapes=[pltpu.SemaphoreType.DMA((n,)), pltpu.SemaphoreType.DMA((n,))],
    )(*halves)


def _sum_rows(r, c, n_in):
    return _tile(r, max(16, (1 << 20) // (c * (n_in + 1))), 16)


def _add2(a, b, out_dtype, name):
    n, r, c = a.shape
    tr = _sum_rows(r, c, 2)

    def body(a_ref, b_ref, o_ref):
        o_ref[...] = (a_ref[...].astype(F32) + b_ref[...].astype(F32)).astype(o_ref.dtype)

    blk = pl.BlockSpec((1, tr, c), lambda s, i: (s, i, 0))
    return pl.pallas_call(
        body,
        name=name,
        grid=(n, r // tr),
        in_specs=[blk, blk],
        out_specs=blk,
        out_shape=jax.ShapeDtypeStruct(a.shape, out_dtype),
        compiler_params=_params(("parallel", "parallel")),
    )(a, b)


def _sum_slots(q, name):
    n, r, c = q.shape
    tr = _sum_rows(r, c, n)

    def body(q_ref, o_ref):
        acc = q_ref[0].astype(F32)
        for s in range(1, n):
            acc = acc + q_ref[s].astype(F32)
        o_ref[...] = acc

    return pl.pallas_call(
        body,
        name=name,
        grid=(r // tr,),
        in_specs=[pl.BlockSpec((n, tr, c), lambda i: (0, i, 0))],
        out_specs=pl.BlockSpec((tr, c), lambda i: (i, 0)),
        out_shape=jax.ShapeDtypeStruct((r, c), F32),
        compiler_params=_params(("parallel",)),
    )(q)


def _adamw(w, g, m, v, name):
    r, c = w.shape
    tr = _tile(r, 256, 8)
    c1 = 1.0 - ADAM_B1 ** ADAM_STEP
    c2 = 1.0 - ADAM_B2 ** ADAM_STEP

    def body(w_ref, g_ref, m_ref, v_ref, d_ref, nm_ref, nv_ref):
        g = g_ref[...]
        nm = ADAM_B1 * m_ref[...] + (1.0 - ADAM_B1) * g
        nv = ADAM_B2 * v_ref[...] + (1.0 - ADAM_B2) * (g * g)
        d_ref[...] = -ADAM_LR * ((nm / c1) / (jnp.sqrt(nv / c2) + ADAM_EPS) + ADAM_WD * w_ref[...])
        nm_ref[...] = nm
        nv_ref[...] = nv

    blk = pl.BlockSpec((tr, c), lambda i: (i, 0))
    shape = jax.ShapeDtypeStruct((r, c), F32)
    return pl.pallas_call(
        body,
        name=name,
        grid=(r // tr,),
        in_specs=[blk] * 4,
        out_specs=[blk] * 3,
        out_shape=[shape] * 3,
        compiler_params=_params(("parallel",)),
    )(w, g, m, v)


WEIGHTS = ("ffn1_w_in", "ffn1_w_out", "ffn2_w_in", "ffn2_w_out", "ln_g", "ln_b", "sb_w_in", "sb_w_out", "swa_w_in", "swa_sinks", "swa_w_out", "dil_w_in", "dil_w_out", "ple_w_proj", "ple_w_gate")
SHARD_AXIS = {"ffn1_w_in": 2, "ffn1_w_out": 1, "ffn2_w_in": 2, "ffn2_w_out": 1, "ln_g": 2, "ln_b": 2, "sb_w_in": 2, "sb_w_out": 1, "swa_w_in": 2, "swa_sinks": None, "swa_w_out": 1, "dil_w_in": 2, "dil_w_out": 1, "ple_w_proj": 2, "ple_w_gate": 1}
SMALL = ("ln_g", "ln_b", "swa_sinks")
SMALL_COLS = 128
SMALL_UNIT = 16 * SMALL_COLS


def _pack_small(pieces, lead):
    flat = jnp.concatenate([a.reshape(lead + (-1,)) for a in pieces], axis=-1)
    n = flat.shape[-1]
    flat = jnp.pad(flat, [(0, 0)] * len(lead) + [(0, -n % SMALL_UNIT)])
    return flat.reshape(lead + (-1, SMALL_COLS))


def _unpack_small(buf, shapes, lead):
    flat = buf.reshape(lead + (-1,))
    out, off = [], 0
    for shp in shapes:
        n = math.prod(shp)
        out.append(flat[..., off:off + n].reshape(lead + tuple(shp)))
        off += n
    return out


def _reduce_pair(grads, c, wire, tag):
    keep = [lax.dynamic_slice_in_dim(g, c * (g.shape[1] // 2), g.shape[1] // 2, axis=1) for g in grads]
    give = [lax.dynamic_slice_in_dim(g, (1 - c) * (g.shape[1] // 2), g.shape[1] // 2, axis=1).astype(wire) for g in grads]
    got = _swap_sibling(give, f"{tag}_pair")
    return [_add2(k, g, wire, f"{tag}_pair_sum{j}") for j, (k, g) in enumerate(zip(keep, got))]


def _reduce_finish(part, landed, c, tag):
    landed = [_fill_slot(q, lax.dynamic_slice_in_dim(p, _my_slot(), 1, axis=0)) for q, p in zip(landed, part)]
    halves = [_sum_slots(q, f"{tag}_chips_sum{j}") for j, q in enumerate(landed)]
    shared = _share_sibling(halves, f"{tag}_share")
    return [lax.dynamic_update_slice_in_dim(s, h, c * h.shape[0], axis=0) for s, h in zip(shared, halves)]


def _reduce_group(grads, c, wire, tag):
    part = _reduce_pair(grads, c, wire, tag)
    return _reduce_finish(part, _exchange_chips(part, f"{tag}_chips"), c, tag)


def _my_slot():
    return 2 * lax.axis_index("x") + lax.axis_index("y")


def _fill_slot(slots, mine):
    return lax.dynamic_update_slice_in_dim(slots, mine, _my_slot(), axis=0)


def _full_from_shards(g, axis):
    g = jnp.moveaxis(g, 0, axis)
    return g.reshape(g.shape[:axis] + (g.shape[axis] * g.shape[axis + 1],) + g.shape[axis + 2:])


def _shards_from_full(a, axis):
    a = a.reshape(a.shape[:axis] + (N_CHIPS, a.shape[axis] // N_CHIPS) + a.shape[axis + 1:])
    return jnp.moveaxis(a, axis, 0)


def _ffn_forward(x, xb, w_in, w_out, g, b, alpha, tag):
    gate, up, act = _ffn_in(xb, w_in, f"{tag}_in")
    out, outb, xhat, rstd = _mm_ln(act, w_out, x, g, b, alpha, 0.5, f"{tag}_out_ln")
    return (out, outb), dict(xb=xb, gate=gate, up=up, act=act, xhat=xhat, rstd=rstd)


EXCHANGE_HOSTS = ("dw_out", "dw_in", "dx")


def _exchange_host(name):
    if name in ("ffn1_w_in", "ffn2_w_in", "ffn1_w_out"):
        return "dw_in"
    return "dx" if name.endswith("_w_in") else "dw_out"


def _ffn_backward(dout, saved, w_in, w_out, g, tag, exchange=None):
    exchange = exchange or {}
    dr, dyb, dg, db = _ln_bwd(*dout, saved["xhat"], saved["rstd"], g, 0.5, f"{tag}_ln_bwd")
    dh = _ffn_dact(dyb, w_out, saved["gate"], saved["up"], f"{tag}_dact")
    out = {
        "dw_out": _mm(saved["act"], dyb, "tn", F32, f"{tag}_dw_out", exchange=exchange.get("dw_out") or None),
        "dw_in": _mm(saved["xb"], dh, "tn", F32, f"{tag}_dw_in", split_b=True, exchange=exchange.get("dw_in") or None),
        "dx": _mm(dh, w_in, "nt", F32, f"{tag}_dx", split_a=True, exchange=exchange.get("dx") or None),
    }
    landed = {h: out[h][1] for h in EXCHANGE_HOSTS if exchange.get(h)}
    res = {h: out[h][0] if exchange.get(h) else out[h] for h in EXCHANGE_HOSTS}
    return dr, res["dx"], res["dw_in"], res["dw_out"], dg, db, landed


def _sb_forward(xb, w_in, w_out, tag):
    nw = w_out.shape[0]
    h = _mm(xb, w_in, "nn", MXU_DTYPE, f"{tag}_qkv")
    q, k, v = (_to_heads(h[:, j * nw:(j + 1) * nw]) for j in range(3))
    o, ob = _sb_fwd(q, k, v, f"{tag}_att")
    ob = _from_heads(ob)
    return ob, dict(q=q, k=k, v=v, o=o, ob=ob)


def _sb_backward(dmix, saved, xb, w_in, w_out, tag):
    dw_out = _mm(saved["ob"], dmix, "tn", F32, f"{tag}_dw_out")
    do = _to_heads(_mm(dmix, w_out, "nt", MXU_DTYPE, f"{tag}_do"))
    dq, dk, dv = _sb_bwd(saved["q"], saved["k"], saved["v"], saved["o"], do, f"{tag}_att_bwd")
    dh = jnp.concatenate([_from_heads(t) for t in (dq, dk, dv)], axis=1).astype(MXU_DTYPE)
    dw_in = _mm(xb, dh, "tn", F32, f"{tag}_dw_in")
    return _mm(dh, w_in, "nt", F32, f"{tag}_dx"), dw_in, dw_out


def _swa_forward(xb, w_in, sinks, w_out, tables, tag):
    nq = w_out.shape[0]
    nkv = (w_in.shape[1] - nq) // 2
    h = _mm(xb, w_in, "nn", F32, f"{tag}_qkv")
    n_rot = (nq + nkv) // 128
    hb = _rope(h, *tables, lambda j: j // n_rot, MXU_DTYPE, f"{tag}_rope")
    q, k, v = _to_heads(hb[:, :nq]), _to_heads(hb[:, nq:nq + nkv]), _to_heads(hb[:, nq + nkv:])
    o, lse = _band_fwd(q, k, v, sinks, SWA_WINDOW - 1, f"{tag}_att")
    o, ob = _merge_fwd([o], [lse], f"{tag}_cast")
    ob = _from_heads(ob)
    return ob, dict(q=q, k=k, v=v, o=o, lse=lse, ob=ob, n_rot=n_rot)


def _swa_backward(dmix, saved, xb, w_in, sinks, w_out, tables, tag):
    dw_out = _mm(saved["ob"], dmix, "tn", F32, f"{tag}_dw_out")
    do = _to_heads(_mm(dmix, w_out, "nt", MXU_DTYPE, f"{tag}_do"))
    (dog,), (delta,) = _merge_bwd(do, saved["o"], [saved["lse"]], f"{tag}_delta")
    dq, dk, dv, dsink = _band_bwd(saved["q"], saved["k"], saved["v"], dog, saved["lse"], delta, sinks, SWA_WINDOW - 1, f"{tag}_att_bwd")
    dh = jnp.concatenate([_from_heads(t) for t in (dq, dk, dv)], axis=1)
    n_rot = saved["n_rot"]
    dhb = _rope(dh, tables[0], -tables[1], lambda j: j // n_rot, MXU_DTYPE, f"{tag}_rope_bwd")
    dw_in = _mm(xb, dhb, "tn", F32, f"{tag}_dw_in")
    return _mm(dhb, w_in, "nt", F32, f"{tag}_dx"), dw_in, dw_out, dsink[:, 0, 0]


def _dil_forward(xb, w_in, w_out, tables, tag):
    nw = w_out.shape[0]
    h = _mm(xb, w_in, "nn", F32, f"{tag}_qkv")
    hb = _rope(h, *tables, lambda j: (j % 3) // 2, MXU_DTYPE, f"{tag}_rope", width=nw)
    qkv, outs, lses = [], [], []
    for gi, (win, dil) in enumerate(DIL_GROUPS):
        base = gi * 3 * nw
        q, k, v = (_to_heads(hb[:, base + j * nw:base + (j + 1) * nw], dil) for j in range(3))
        o, lse = _band_fwd(q, k, v, None, win // dil, f"{tag}_att{gi}")
        qkv.append((q, k, v))
        outs.append(_unstride(o, dil))
        lses.append(_unstride(lse, dil))
    o, ob = _merge_fwd(outs, lses, f"{tag}_merge")
    ob = _from_heads(ob)
    return ob, dict(qkv=qkv, o=o, lses=lses, ob=ob)


def _dil_backward(dmix, saved, xb, w_in, w_out, tables, tag):
    dw_out = _mm(saved["ob"], dmix, "tn", F32, f"{tag}_dw_out")
    do = _to_heads(_mm(dmix, w_out, "nt", MXU_DTYPE, f"{tag}_do"))
    dogs, deltas = _merge_bwd(do, saved["o"], saved["lses"], f"{tag}_merge_bwd")
    parts = []
    for gi, (win, dil) in enumerate(DIL_GROUPS):
        q, k, v = saved["qkv"][gi]
        dq, dk, dv, _ = _band_bwd(q, k, v, _restride(dogs[gi], dil), _restride(saved["lses"][gi], dil), _restride(deltas[gi], dil), None, win // dil, f"{tag}_att{gi}_bwd")
        parts += [_from_heads(t, dil) for t in (dq, dk, dv)]
    dh = jnp.concatenate(parts, axis=1)
    dhb = _rope(dh, tables[0], -tables[1], lambda j: (j % 3) // 2, MXU_DTYPE, f"{tag}_rope_bwd", width=w_out.shape[0])
    dw_in = _mm(xb, dhb, "tn", F32, f"{tag}_dw_in")
    return _mm(dhb, w_in, "nt", F32, f"{tag}_dx"), dw_in, dw_out


def kernel(x, p, ffn1_w_in, ffn1_w_out, ffn2_w_in, ffn2_w_out, ln_g, ln_b, sb_w_in, sb_w_out, swa_w_in, swa_sinks, swa_w_out, dil_w_in, dil_w_out, ple_w_proj, ple_w_gate, loss_target, m_ffn1_w_in, m_ffn1_w_out, m_ffn2_w_in, m_ffn2_w_out, m_ln_g, m_ln_b, m_sb_w_in, m_sb_w_out, m_swa_w_in, m_swa_sinks, m_swa_w_out, m_dil_w_in, m_dil_w_out, m_ple_w_proj, m_ple_w_gate, v_ffn1_w_in, v_ffn1_w_out, v_ffn2_w_in, v_ffn2_w_out, v_ln_g, v_ln_b, v_sb_w_in, v_sb_w_out, v_swa_w_in, v_swa_sinks, v_swa_w_out, v_dil_w_in, v_dil_w_out, v_ple_w_proj, v_ple_w_gate):
    shard = dict(ffn1_w_in=ffn1_w_in, ffn1_w_out=ffn1_w_out, ffn2_w_in=ffn2_w_in, ffn2_w_out=ffn2_w_out, ln_g=ln_g, ln_b=ln_b, sb_w_in=sb_w_in, sb_w_out=sb_w_out, swa_w_in=swa_w_in, swa_sinks=swa_sinks, swa_w_out=swa_w_out, dil_w_in=dil_w_in, dil_w_out=dil_w_out, ple_w_proj=ple_w_proj, ple_w_gate=ple_w_gate)
    mom_m = dict(ffn1_w_in=m_ffn1_w_in, ffn1_w_out=m_ffn1_w_out, ffn2_w_in=m_ffn2_w_in, ffn2_w_out=m_ffn2_w_out, ln_g=m_ln_g, ln_b=m_ln_b, sb_w_in=m_sb_w_in, sb_w_out=m_sb_w_out, swa_w_in=m_swa_w_in, swa_sinks=m_swa_sinks, swa_w_out=m_swa_w_out, dil_w_in=m_dil_w_in, dil_w_out=m_dil_w_out, ple_w_proj=m_ple_w_proj, ple_w_gate=m_ple_w_gate)
    mom_v = dict(ffn1_w_in=v_ffn1_w_in, ffn1_w_out=v_ffn1_w_out, ffn2_w_in=v_ffn2_w_in, ffn2_w_out=v_ffn2_w_out, ln_g=v_ln_g, ln_b=v_ln_b, sb_w_in=v_sb_w_in, sb_w_out=v_sb_w_out, swa_w_in=v_swa_w_in, swa_sinks=v_swa_sinks, swa_w_out=v_swa_w_out, dil_w_in=v_dil_w_in, dil_w_out=v_dil_w_out, ple_w_proj=v_ple_w_proj, ple_w_gate=v_ple_w_gate)
    depth = ffn1_w_in.shape[0]
    alpha = (2 * depth) ** 0.25
    c = lax.axis_index("c")

    mats = [n for n in WEIGHTS if n not in SMALL]
    local = [shard[n].astype(MXU_DTYPE) for n in mats] + [_pack_small([ln_g, ln_b], ())[None]]
    got = [_fill_slot(g, a[None]) for g, a in zip(_gather_chips(local, "gather_weights"), local)]
    full = {"swa_sinks": swa_sinks}
    for n, g in zip(mats, got):
        full[n] = _full_from_shards(g, SHARD_AXIS[n])
    for n, g in zip(("ln_g", "ln_b"), _unpack_small(got[-1][:, 0], [ln_g.shape, ln_b.shape], (N_CHIPS,))):
        full[n] = _full_from_shards(g, SHARD_AXIS[n])

    seq = x.shape[1]
    tables = _rope_tables(seq)
    xf = x[0]
    xb = xf.astype(MXU_DTYPE)
    saved = []
    for i in range(depth):
        kind, j = i % 3, i // 3
        mixer = ("sb", "swa", "dil")[kind]
        sv = {}
        (x1, x1b), sv["ffn1"] = _ffn_forward(xf, xb, full["ffn1_w_in"][i], full["ffn1_w_out"][i], full["ln_g"][i, 0], full["ln_b"][i, 0], alpha, f"l{i}_ffn1")
        if kind == 0:
            mix, sv["mix"] = _sb_forward(x1b, full["sb_w_in"][j], full["sb_w_out"][j], f"l{i}_sb")
        elif kind == 1:
            mix, sv["mix"] = _swa_forward(x1b, full["swa_w_in"][j], swa_sinks[j], full["swa_w_out"][j], tables, f"l{i}_swa")
        else:
            mix, sv["mix"] = _dil_forward(x1b, full["dil_w_in"][j], full["dil_w_out"][j], tables, f"l{i}_dil")
        x2, x2b, sv["xhat2"], sv["rstd2"] = _mm_ln(mix, full[f"{mixer}_w_out"][j], x1, full["ln_g"][i, 1], full["ln_b"][i, 1], alpha, 1.0, f"l{i}_{mixer}_proj_ln")
        sv["x1b"] = x1b
        (x3, x3b), sv["ffn2"] = _ffn_forward(x2, x2b, full["ffn2_w_in"][i], full["ffn2_w_out"][i], full["ln_g"][i, 2], full["ln_b"][i, 2], alpha, f"l{i}_ffn2")
        xf, xb, sv["u"], sv["e"] = _ple_fwd(x3, x3b, p[i, 0], full["ple_w_gate"][i], full["ple_w_proj"][i], f"l{i}_ple")
        sv["x3b"] = x3b
        saved.append(sv)

    loss_part, dy = _loss(xf, loss_target[0], "loss")
    loss = lax.psum(loss_part[0, 0], ("x", "y", "c"))

    grads = {n: [None] * full[n].shape[0] for n in WEIGHTS if n not in ("ln_g", "ln_b")}
    gsum = {n: [None] * full[n].shape[0] for n in mats}
    dln_g = [[None] * 3 for _ in range(depth)]
    dln_b = [[None] * 3 for _ in range(depth)]
    dout = (dy, None, 1.0)
    pending = None

    def finish_reduce(layer, part, tag, landed):
        for (n, k), g in zip(layer, _reduce_finish(part, landed, c, tag)):
            gsum[n][k] = g

    for i in reversed(range(depth)):
        kind, j = i % 3, i // 3
        mixer = ("sb", "swa", "dil")[kind]
        sv = saved[i]
        dx4, dub, deb = _ple_bwd(*dout, sv["u"], sv["e"], f"l{i}_ple_bwd")
        grads["ple_w_gate"][i] = _mm(sv["x3b"], dub, "tn", F32, f"l{i}_ple_dw_gate")
        grads["ple_w_proj"][i] = _mm(p[i, 0], deb, "tn", F32, f"l{i}_ple_dw_proj")
        dxb = _mm(dub, full["ple_w_gate"][i], "nt", F32, f"l{i}_ple_dx")
        dr, dxb, grads["ffn2_w_in"][i], grads["ffn2_w_out"][i], dln_g[i][2], dln_b[i][2], landed = _ffn_backward((dx4, dxb, 1.0), sv["ffn2"], full["ffn2_w_in"][i], full["ffn2_w_out"][i], full["ln_g"][i, 2], f"l{i}_ffn2", exchange=pending and {h: [q for (n, _), q in zip(pending[0], pending[1]) if _exchange_host(n) == h] for h in EXCHANGE_HOSTS})
        if pending:
            arrived = {h: iter(v) for h, v in landed.items()}
            finish_reduce(*pending, [next(arrived[_exchange_host(n)]) for n, _ in pending[0]])
        dr, dmix, dln_g[i][1], dln_b[i][1] = _ln_bwd(dr, dxb, alpha, sv["xhat2"], sv["rstd2"], full["ln_g"][i, 1], 1.0, f"l{i}_mix_ln_bwd")
        if kind == 0:
            dxb, grads["sb_w_in"][j], grads["sb_w_out"][j] = _sb_backward(dmix, sv["mix"], sv["x1b"], full["sb_w_in"][j], full["sb_w_out"][j], f"l{i}_sb")
        elif kind == 1:
            dxb, grads["swa_w_in"][j], grads["swa_w_out"][j], grads["swa_sinks"][j] = _swa_backward(dmix, sv["mix"], sv["x1b"], full["swa_w_in"][j], swa_sinks[j], full["swa_w_out"][j], tables, f"l{i}_swa")
        else:
            dxb, grads["dil_w_in"][j], grads["dil_w_out"][j] = _dil_backward(dmix, sv["mix"], sv["x1b"], full["dil_w_in"][j], full["dil_w_out"][j], tables, f"l{i}_dil")
        dr, dxb, grads["ffn1_w_in"][i], grads["ffn1_w_out"][i], dln_g[i][0], dln_b[i][0], _ = _ffn_backward((dr, dxb, alpha), sv["ffn1"], full["ffn1_w_in"][i], full["ffn1_w_out"][i], full["ln_g"][i, 0], f"l{i}_ffn1")
        dout = (dr, dxb, alpha)
        layer = [("ffn1_w_in", i), ("ffn1_w_out", i), (f"{mixer}_w_in", j), (f"{mixer}_w_out", j), ("ffn2_w_in", i), ("ffn2_w_out", i), ("ple_w_proj", i), ("ple_w_gate", i)]
        pending = (layer, _reduce_pair([_shards_from_full(grads[n][k], SHARD_AXIS[n] - 1) for n, k in layer], c, MXU_DTYPE, f"l{i}_reduce"), f"l{i}_reduce")
    layer, part, tag = pending
    finish_reduce(layer, part, tag, _exchange_chips(part, f"{tag}_chips"))
    grad_x = _axpy(*dout, "grad_x")[None]

    gshard = {n: jnp.stack(g) for n, g in gsum.items()}
    small = [
        _shards_from_full(jnp.stack([jnp.concatenate(r, axis=0) for r in dln_g]), SHARD_AXIS["ln_g"]),
        _shards_from_full(jnp.stack([jnp.concatenate(r, axis=0) for r in dln_b]), SHARD_AXIS["ln_b"]),
        jnp.broadcast_to(jnp.stack(grads["swa_sinks"])[None], (N_CHIPS,) + swa_sinks.shape),
    ]
    (small_sum,) = _reduce_group([_pack_small(small, (N_CHIPS,))], c, F32, "small_reduce")
    for n, g in zip(SMALL, _unpack_small(small_sum, [shard[n].shape for n in SMALL], ())):
        gshard[n] = g

    delta, new_m, new_v = {}, {}, {}
    for n in WEIGHTS:
        shp = shard[n].shape
        two_d = (-1, shp[-1])
        d, nm, nv = _adamw(shard[n].reshape(two_d), gshard[n].reshape(two_d), mom_m[n].reshape(two_d), mom_v[n].reshape(two_d), f"adamw_{n}")
        delta[n], new_m[n], new_v[n] = d.reshape(shp), nm.reshape(shp), nv.reshape(shp)

    return (loss, grad_x, *[gshard[n] for n in WEIGHTS], *[delta[n] for n in WEIGHTS], *[new_m[n] for n in WEIGHTS], *[new_v[n] for n in WEIGHTS])
```

```python
import functools
import math

import jax
import jax.numpy as jnp
from jax import lax
from jax.experimental import pallas as pl
from jax.experimental.pallas import tpu as pltpu

F32 = jnp.float32
MXU_DTYPE = jnp.bfloat16
MESH = pl.DeviceIdType.MESH

HEAD_DIM = 64
ATT_BLK = 128
SWA_WINDOW = 128
DIL_GROUPS = ((128, 1), (512, 4), (2048, 16))
LN_EPS = 1e-5
ROPE_THETA = 10000.0
NEG_INF = -1e30
ADAM_LR, ADAM_B1, ADAM_B2, ADAM_EPS, ADAM_WD, ADAM_STEP = 0.001, 0.9, 0.999, 1e-08, 0.01, 10

VMEM_LIMIT_BYTES = 56 * 1024 * 1024
N_CHIPS = 4


def _params(sem=None):
    return pltpu.CompilerParams(dimension_semantics=sem, vmem_limit_bytes=VMEM_LIMIT_BYTES)


def _tile(n, target, unit):
    t = (min(target, n) // unit) * unit
    while t >= unit:
        if n % t == 0:
            return t
        t -= unit
    return n


def _dot(a, b, dims):
    return lax.dot_general(a.astype(MXU_DTYPE), b.astype(MXU_DTYPE), (dims, ((), ())), preferred_element_type=F32)


NN = ((1,), (0,))
NT = ((1,), (1,))
TN = ((0,), (0,))


def _accumulate(part, acc_ref, kk, nk, finish):
    if nk == 1:
        finish(part)
        return

    @pl.when(kk == 0)
    def _():
        acc_ref[...] = jnp.zeros_like(acc_ref)

    acc_ref[...] += part

    @pl.when(kk == nk - 1)
    def _():
        finish(acc_ref[...])


def _mm(a, b, mode, out_dtype, name, split_a=False, split_b=False, exchange=None, tm=1408, tn=1408, tk=1408):
    dims = {"nn": NN, "nt": NT, "tn": TN}[mode]
    if split_a:
        m, k = a.shape[1], 2 * a.shape[2]
    elif mode == "tn":
        k, m = a.shape
    else:
        m, k = a.shape
    if split_b:
        n = 2 * b.shape[2]
    elif mode == "nt":
        n = b.shape[0]
    else:
        n = b.shape[1]
    tm = _tile(m, tm, 128)
    tn = _tile(n // 2 if split_b else n, tn, 128)
    tk = _tile(k // 2 if split_a else k, tk, 128)
    nk = k // tk
    nk_half = nk // 2
    nn_half = (n // tn) // 2

    if split_a:
        a_spec = pl.BlockSpec((None, tm, tk), lambda i, j, kk: (kk // nk_half, i, kk % nk_half))
    elif mode == "tn":
        a_spec = pl.BlockSpec((tk, tm), lambda i, j, kk: (kk, i))
    else:
        a_spec = pl.BlockSpec((tm, tk), lambda i, j, kk: (i, kk))
    if split_b:
        b_spec = pl.BlockSpec((None, tk, tn), lambda i, j, kk: (j // nn_half, kk, j % nn_half))
    elif mode == "nt":
        b_spec = pl.BlockSpec((tn, tk), lambda i, j, kk: (j, kk))
    else:
        b_spec = pl.BlockSpec((tk, tn), lambda i, j, kk: (kk, j))

    parts = [] if exchange is None else list(exchange)
    n_ex = len(parts)
    grid = (m // tm, n // tn, nk)

    def body(*refs):
        a_ref, b_ref = refs[:2]
        p_refs, o_ref, q_refs, acc_ref = refs[2:2 + n_ex], refs[2 + n_ex], refs[3 + n_ex:3 + 2 * n_ex], refs[3 + 2 * n_ex]
        if n_ex:
            step = (pl.program_id(0) * grid[1] + pl.program_id(1)) * nk + pl.program_id(2)

            @pl.when(step == 0)
            def _():
                for cp in _chips_copies(p_refs, q_refs, *refs[4 + 2 * n_ex:])[0]:
                    cp.start()

        def finish(total):
            o_ref[...] = total.astype(o_ref.dtype)

        _accumulate(_dot(a_ref[...], b_ref[...], dims), acc_ref, pl.program_id(2), nk, finish)

        if n_ex:

            @pl.when(step == grid[0] * grid[1] * nk - 1)
            def _():
                _chips_finish(*_chips_copies(p_refs, q_refs, *refs[4 + 2 * n_ex:]))

    out = pl.pallas_call(
        body,
        name=name,
        grid=grid,
        in_specs=[a_spec, b_spec] + [ANY] * n_ex,
        out_specs=[pl.BlockSpec((tm, tn), lambda i, j, kk: (i, j))] + [ANY] * n_ex,
        out_shape=[jax.ShapeDtypeStruct((m, n), out_dtype)] + [jax.ShapeDtypeStruct(p.shape, p.dtype) for p in parts],
        scratch_shapes=[pltpu.VMEM((tm, tn), F32)] + [pltpu.SemaphoreType.DMA((3 * n_ex,))] * (2 if n_ex else 0),
        compiler_params=_params(("arbitrary", "arbitrary", "arbitrary") if n_ex else ("parallel", "parallel", "arbitrary")),
    )(a, b, *parts)
    return (out[0], out[1:]) if n_ex else out[0]


def _sigmoid(x):
    return 1.0 / (1.0 + jnp.exp(-x))


def _ffn_in(xb, w_in, name):
    s, d = xb.shape
    f = w_in.shape[1] // 2
    tm = _tile(s, 512, 128)
    tn = _tile(f, 1408, 128)
    nj = f // tn

    def body(x_ref, wg_ref, wu_ref, g_ref, u_ref, a_ref):
        x = x_ref[...]
        g = _dot(x, wg_ref[...], NN)
        u = _dot(x, wu_ref[...], NN)
        g_ref[...] = g.astype(g_ref.dtype)
        u_ref[...] = u.astype(u_ref.dtype)
        a_ref[...] = (g * _sigmoid(g) * u).astype(a_ref.dtype)

    out = pl.BlockSpec((tm, tn), lambda i, j: (i, j))
    return pl.pallas_call(
        body,
        name=name,
        grid=(s // tm, nj),
        in_specs=[
            pl.BlockSpec((tm, d), lambda i, j: (i, 0)),
            pl.BlockSpec((d, tn), lambda i, j: (0, j)),
            pl.BlockSpec((d, tn), lambda i, j: (0, j + nj)),
        ],
        out_specs=[out, out, out],
        out_shape=[
            jax.ShapeDtypeStruct((s, f), MXU_DTYPE),
            jax.ShapeDtypeStruct((s, f), MXU_DTYPE),
            jax.ShapeDtypeStruct((s, f), MXU_DTYPE),
        ],
        compiler_params=_params(("parallel", "parallel")),
    )(xb, w_in, w_in)


def _ffn_dact(dyb, w_out, gate, up, name):
    s, d = dyb.shape
    f = w_out.shape[0]
    tm = _tile(s, 512, 128)
    tn = _tile(f, 1408, 128)

    def body(dy_ref, w_ref, g_ref, u_ref, o_ref):
        dact = _dot(dy_ref[...], w_ref[...], NT)
        g = g_ref[...].astype(F32)
        sig = _sigmoid(g)
        o_ref[0] = (dact * u_ref[...].astype(F32) * (sig * (1.0 + g * (1.0 - sig)))).astype(o_ref.dtype)
        o_ref[1] = (dact * (g * sig)).astype(o_ref.dtype)

    tile = pl.BlockSpec((tm, tn), lambda i, j: (i, j))
    return pl.pallas_call(
        body,
        name=name,
        grid=(s // tm, f // tn),
        in_specs=[
            pl.BlockSpec((tm, d), lambda i, j: (i, 0)),
            pl.BlockSpec((tn, d), lambda i, j: (j, 0)),
            tile,
            tile,
        ],
        out_specs=pl.BlockSpec((2, tm, tn), lambda i, j: (0, i, j)),
        out_shape=jax.ShapeDtypeStruct((2, s, f), MXU_DTYPE),
        compiler_params=_params(("parallel", "parallel")),
    )(dyb, w_out, gate, up)


def _ple_fwd(x, xb, p, w_gate, w_proj, name):
    s, d = x.shape
    pd = p.shape[1]
    tm = _tile(s, 1024, 128)
    tn = _tile(d, 512, 128)

    def body(x_ref, xb_ref, p_ref, wg_ref, wp_ref, o_ref, ob_ref, u_ref, e_ref):
        u = _dot(xb_ref[...], wg_ref[...], NN)
        e = _dot(p_ref[...], wp_ref[...], NN)
        out = x_ref[...] + _sigmoid(u) * e
        o_ref[...] = out
        ob_ref[...] = out.astype(ob_ref.dtype)
        u_ref[...] = u
        e_ref[...] = e

    tile = pl.BlockSpec((tm, tn), lambda i, j: (i, j))
    return pl.pallas_call(
        body,
        name=name,
        grid=(s // tm, d // tn),
        in_specs=[
            tile,
            pl.BlockSpec((tm, d), lambda i, j: (i, 0)),
            pl.BlockSpec((tm, pd), lambda i, j: (i, 0)),
            pl.BlockSpec((d, tn), lambda i, j: (0, j)),
            pl.BlockSpec((pd, tn), lambda i, j: (0, j)),
        ],
        out_specs=[tile, tile, tile, tile],
        out_shape=[
            jax.ShapeDtypeStruct((s, d), F32),
            jax.ShapeDtypeStruct((s, d), MXU_DTYPE),
            jax.ShapeDtypeStruct((s, d), F32),
            jax.ShapeDtypeStruct((s, d), F32),
        ],
        compiler_params=_params(("parallel", "parallel")),
    )(x, xb, p, w_gate, w_proj)


def _rows_spec(ts, d):
    return pl.BlockSpec((ts, d), lambda i: (i, 0))


def _mm_ln(a, w, x, g, b, alpha, beta, name):
    s, k = a.shape
    d = w.shape[1]
    tm = _tile(s, 512, 128)
    tk = _tile(k, 1408, 128)
    nk = k // tk

    def body(a_ref, w_ref, x_ref, g_ref, b_ref, o_ref, ob_ref, xh_ref, rs_ref, acc_ref):
        def finish(y):
            r = alpha * x_ref[...] + beta * y
            mu = jnp.mean(r, axis=1, keepdims=True)
            cen = r - mu
            var = jnp.mean(cen * cen, axis=1, keepdims=True)
            rstd = lax.rsqrt(var + LN_EPS)
            xhat = cen * rstd
            out = xhat * g_ref[...] + b_ref[...]
            o_ref[...] = out
            ob_ref[...] = out.astype(ob_ref.dtype)
            xh_ref[...] = xhat
            rs_ref[...] = rstd

        _accumulate(_dot(a_ref[...], w_ref[...], NN), acc_ref, pl.program_id(1), nk, finish)

    rows = pl.BlockSpec((tm, d), lambda i, kk: (i, 0))
    vec = pl.BlockSpec((1, d), lambda i, kk: (0, 0))
    return pl.pallas_call(
        body,
        name=name,
        grid=(s // tm, nk),
        in_specs=[pl.BlockSpec((tm, tk), lambda i, kk: (i, kk)), pl.BlockSpec((tk, d), lambda i, kk: (kk, 0)), rows, vec, vec],
        out_specs=[rows, rows, rows, pl.BlockSpec((tm, 1), lambda i, kk: (i, 0))],
        out_shape=[
            jax.ShapeDtypeStruct((s, d), F32),
            jax.ShapeDtypeStruct((s, d), MXU_DTYPE),
            jax.ShapeDtypeStruct((s, d), F32),
            jax.ShapeDtypeStruct((s, 1), F32),
        ],
        scratch_shapes=[pltpu.VMEM((tm, d), F32)],
        compiler_params=_params(("parallel", "arbitrary")),
    )(a, w, x, g.reshape(1, d), b.reshape(1, d))


def _ln_bwd(ga, gb, ca, xhat, rstd, g, beta, name):
    s, d = xhat.shape
    ts = _tile(s, 512, 8)

    def body(ga_ref, gb_ref, xh_ref, rs_ref, g_ref, dr_ref, dyb_ref, dg_ref, db_ref):
        @pl.when(pl.program_id(0) == 0)
        def _():
            dg_ref[...] = jnp.zeros_like(dg_ref)
            db_ref[...] = jnp.zeros_like(db_ref)

        dout = ca * ga_ref[...] + gb_ref[...]
        xhat = xh_ref[...]
        dg_ref[...] += jnp.sum(dout * xhat, axis=0, keepdims=True)
        db_ref[...] += jnp.sum(dout, axis=0, keepdims=True)
        dxh = dout * g_ref[...]
        m1 = jnp.mean(dxh, axis=1, keepdims=True)
        m2 = jnp.mean(dxh * xhat, axis=1, keepdims=True)
        dr = rs_ref[...] * (dxh - m1 - xhat * m2)
        dr_ref[...] = dr
        dyb_ref[...] = (beta * dr).astype(dyb_ref.dtype)

    vec = pl.BlockSpec((1, d), lambda i: (0, 0))
    return pl.pallas_call(
        body,
        name=name,
        grid=(s // ts,),
        in_specs=[_rows_spec(ts, d), _rows_spec(ts, d), _rows_spec(ts, d), _rows_spec(ts, 1), vec],
        out_specs=[_rows_spec(ts, d), _rows_spec(ts, d), vec, vec],
        out_shape=[
            jax.ShapeDtypeStruct((s, d), F32),
            jax.ShapeDtypeStruct((s, d), MXU_DTYPE),
            jax.ShapeDtypeStruct((1, d), F32),
            jax.ShapeDtypeStruct((1, d), F32),
        ],
        compiler_params=_params(("arbitrary",)),
    )(ga, gb, xhat, rstd, g.reshape(1, d))


def _ple_bwd(ga, gb, ca, u, e, name):
    s, d = u.shape
    ts = _tile(s, 512, 8)
    grads = [ga] if gb is None else [ga, gb]

    def body(*refs):
        u_ref, e_ref, dx_ref, du_ref, de_ref = refs[len(grads):]
        dx = ca * refs[0][...]
        if gb is not None:
            dx = dx + refs[1][...]
        sig = _sigmoid(u_ref[...])
        dx_ref[...] = dx
        du_ref[...] = (dx * e_ref[...] * sig * (1.0 - sig)).astype(du_ref.dtype)
        de_ref[...] = (dx * sig).astype(de_ref.dtype)

    return pl.pallas_call(
        body,
        name=name,
        grid=(s // ts,),
        in_specs=[_rows_spec(ts, d)] * (len(grads) + 2),
        out_specs=[_rows_spec(ts, d)] * 3,
        out_shape=[
            jax.ShapeDtypeStruct((s, d), F32),
            jax.ShapeDtypeStruct((s, d), MXU_DTYPE),
            jax.ShapeDtypeStruct((s, d), MXU_DTYPE),
        ],
        compiler_params=_params(("parallel",)),
    )(*grads, u, e)


def _axpy(ga, gb, ca, name):
    s, d = ga.shape
    ts = _tile(s, 512, 8)

    def body(ga_ref, gb_ref, o_ref):
        o_ref[...] = ca * ga_ref[...] + gb_ref[...]

    return pl.pallas_call(
        body,
        name=name,
        grid=(s // ts,),
        in_specs=[_rows_spec(ts, d)] * 2,
        out_specs=_rows_spec(ts, d),
        out_shape=jax.ShapeDtypeStruct((s, d), F32),
        compiler_params=_params(("parallel",)),
    )(ga, gb)


def _loss(y, target, name):
    s, d = y.shape
    ts = _tile(s, 512, 8)

    def body(y_ref, t_ref, l_ref, dy_ref):
        @pl.when(pl.program_id(0) == 0)
        def _():
            l_ref[...] = jnp.zeros_like(l_ref)

        err = y_ref[...] - t_ref[...]
        l_ref[...] += (0.5 / d) * jnp.sum(jnp.sum(err * err, axis=1, keepdims=True), axis=0, keepdims=True)
        dy_ref[...] = err * (1.0 / d)

    return pl.pallas_call(
        body,
        name=name,
        grid=(s // ts,),
        in_specs=[_rows_spec(ts, d)] * 2,
        out_specs=[pl.BlockSpec((1, 1), lambda i: (0, 0)), _rows_spec(ts, d)],
        out_shape=[jax.ShapeDtypeStruct((1, 1), F32), jax.ShapeDtypeStruct((s, d), F32)],
        compiler_params=_params(("arbitrary",)),
    )(y, target)


def _rope_tables(seq):
    pos = jnp.arange(seq, dtype=F32)
    inv = ROPE_THETA ** (-jnp.arange(0, HEAD_DIM, 2, dtype=F32) / HEAD_DIM)
    ang = pos[:, None] * inv[None, :]
    cos, sin = jnp.cos(ang), jnp.sin(ang)
    cos2 = jnp.concatenate([cos, cos, cos, cos, jnp.ones((seq, 128), F32)], axis=1)
    sin2 = jnp.concatenate([-sin, sin, -sin, sin, jnp.zeros((seq, 128), F32)], axis=1)
    return cos2, sin2


def _rope(h, cos2, sin2, plain_block, out_dtype, name, width=128):
    s, n = h.shape
    ts = _tile(s, 512, 8)

    def body(h_ref, c_ref, s_ref, o_ref):
        cos, sin = c_ref[...], s_ref[...]
        lane = lax.broadcasted_iota(jnp.int32, cos.shape, 1)
        first_half = lane % HEAD_DIM < HEAD_DIM // 2
        for w in range(width // 128):
            cols = slice(w * 128, (w + 1) * 128)
            x = h_ref[:, cols].astype(F32)
            partner = jnp.where(first_half, pltpu.roll(x, 128 - HEAD_DIM // 2, 1), pltpu.roll(x, HEAD_DIM // 2, 1))
            o_ref[:, cols] = (x * cos + partner * sin).astype(o_ref.dtype)

    tile = pl.BlockSpec((ts, width), lambda i, j: (i, j))
    table = pl.BlockSpec((ts, 128), lambda i, j: (i, plain_block(j)))
    return pl.pallas_call(
        body,
        name=name,
        grid=(s // ts, n // width),
        in_specs=[tile, table, table],
        out_specs=tile,
        out_shape=jax.ShapeDtypeStruct((s, n), out_dtype),
        compiler_params=_params(("parallel", "parallel")),
    )(h, cos2, sin2)


SB_TQ = 512
SB_BLK = 128
SB_WALK = 2
SB_CLOSED = 110.0


def _sb_walk(n_trips, carry, key_blocks):
    def still_open(carry):
        lowest = functools.reduce(jnp.minimum, [c[0] for c in carry])
        return jnp.min(lowest) < SB_CLOSED

    def cond(state):
        t, go, _ = state
        return jnp.logical_and(t < n_trips, go)

    def body(state):
        t, _, carry = state
        carry = key_blocks(t, carry)
        return t + 1, still_open(carry), carry

    return lax.while_loop(cond, body, (jnp.int32(0), still_open(carry), carry))[2]


def _tri2(strict):
    row = lax.broadcasted_iota(jnp.int32, (2 * SB_BLK, SB_BLK), 0) % SB_BLK
    col = lax.broadcasted_iota(jnp.int32, (2 * SB_BLK, SB_BLK), 1)
    return (row > col if strict else row >= col).astype(MXU_DTYPE)


def _cumsum_dot(x, tri2):
    hi = x.astype(MXU_DTYPE)
    lo = x - hi.astype(F32)
    return _dot(jnp.concatenate([hi, lo.astype(MXU_DTYPE)], axis=1), tri2, NN)


def _sb_logits(z, valid):
    l1p = jnp.log(1.0 + jnp.exp(-jnp.abs(z)))
    sp = jnp.maximum(z, 0.0) + l1p
    ls = z - sp
    if valid is not None:
        sp = jnp.where(valid, sp, 0.0)
    return sp, ls


def _sb_weights(ls, after, valid):
    a = jnp.exp(ls - after)
    return a if valid is None else jnp.where(valid, a, 0.0)


def _sb_setup(q_ref, k_ref, v_ref, n_sub, scale):
    qs = [q_ref[0, u * SB_BLK:(u + 1) * SB_BLK, :] * scale for u in range(n_sub)]
    row = lax.broadcasted_iota(jnp.int32, (SB_BLK, SB_BLK), 0)
    col = lax.broadcasted_iota(jnp.int32, (SB_BLK, SB_BLK), 1)

    def load(jj):
        start = pl.multiple_of(jj * SB_BLK, SB_BLK)
        return start, k_ref[0, pl.ds(start, SB_BLK), :], v_ref[0, pl.ds(start, SB_BLK), :]

    return qs, col < row, load


def _sb_fwd(q, k, v, name):
    nh, s, dh = q.shape
    tq = min(SB_TQ, s)
    n_sub = tq // SB_BLK
    scale = dh ** -0.5

    def body(q_ref, k_ref, v_ref, o_ref, ob_ref):
        base = pl.program_id(1) * n_sub
        qs, diag_valid, load = _sb_setup(q_ref, k_ref, v_ref, n_sub, scale)
        tri_after = _tri2(True)

        def key_blocks(first, carry, diagonal):
            blocks = list(range((n_sub if diagonal else SB_WALK) - 1, -1, -1))
            kv = {d: load(first + d) for d in blocks}
            tiles = [(d, u) for d in blocks for u in range(d if diagonal else 0, n_sub)]
            valid = {t: diag_valid if diagonal and t[0] == t[1] else None for t in tiles}
            z = {t: _dot(qs[t[1]], kv[t[0]][1], NT) for t in tiles}
            sp_ls = {t: _sb_logits(z[t], valid[t]) for t in tiles}
            inside = {t: _cumsum_dot(sp_ls[t][0], tri_after) for t in tiles}
            carry = list(carry)
            for t in tiles:
                after_c, acc = carry[t[1]]
                sp, ls = sp_ls[t]
                a = _sb_weights(ls, after_c + inside[t], valid[t])
                carry[t[1]] = (after_c + jnp.sum(sp, axis=1, keepdims=True), acc + _dot(a, kv[t[0]][2], NN))
            return tuple(carry)

        carry = tuple((jnp.zeros((SB_BLK, 1), F32), jnp.zeros((SB_BLK, dh), F32)) for _ in range(n_sub))
        carry = key_blocks(base, carry, True)
        carry = _sb_walk(base // SB_WALK, carry, lambda t, c: key_blocks(base - SB_WALK * (t + 1), c, False))
        for u in range(n_sub):
            rows = slice(u * SB_BLK, (u + 1) * SB_BLK)
            o_ref[0, rows, :] = carry[u][1]
            ob_ref[0, rows, :] = carry[u][1].astype(ob_ref.dtype)

    blk = pl.BlockSpec((1, tq, dh), lambda h, i: (h, i, 0))
    full = pl.BlockSpec((1, s, dh), lambda h, i: (h, 0, 0))
    return pl.pallas_call(
        body,
        name=name,
        grid=(nh, s // tq),
        in_specs=[blk, full, full],
        out_specs=[blk, blk],
        out_shape=[jax.ShapeDtypeStruct((nh, s, dh), F32), jax.ShapeDtypeStruct((nh, s, dh), MXU_DTYPE)],
        compiler_params=_params(("parallel", "parallel")),
    )(q, k, v)


def _sb_bwd(q, k, v, o, do, name):
    nh, s, dh = q.shape
    tq = min(SB_TQ, s)
    n_sub = tq // SB_BLK
    scale = dh ** -0.5

    def body(q_ref, k_ref, v_ref, o_ref, do_ref, dq_ref, dk_ref, dv_ref):
        i = pl.program_id(1)
        base = i * n_sub

        @pl.when(i == 0)
        def _():
            dk_ref[...] = jnp.zeros_like(dk_ref)
            dv_ref[...] = jnp.zeros_like(dv_ref)

        qs, diag_valid, load = _sb_setup(q_ref, k_ref, v_ref, n_sub, scale)
        dob = [do_ref[0, u * SB_BLK:(u + 1) * SB_BLK, :] for u in range(n_sub)]
        total = [jnp.sum(dob[u].astype(F32) * o_ref[0, u * SB_BLK:(u + 1) * SB_BLK, :], axis=1, keepdims=True) for u in range(n_sub)]
        tri_after = _tri2(True)
        tri_from = _tri2(False)

        def key_blocks(first, carry, diagonal):
            blocks = list(range((n_sub if diagonal else SB_WALK) - 1, -1, -1))
            kv = {d: load(first + d) for d in blocks}
            tiles = [(d, u) for d in blocks for u in range(d if diagonal else 0, n_sub)]
            valid = {t: diag_valid if diagonal and t[0] == t[1] else None for t in tiles}
            z = {t: _dot(qs[t[1]], kv[t[0]][1], NT) for t in tiles}
            da = {t: _dot(dob[t[1]], kv[t[0]][2], NT) for t in tiles}
            sp_ls = {t: _sb_logits(z[t], valid[t]) for t in tiles}
            inside = {t: _cumsum_dot(sp_ls[t][0], tri_after) for t in tiles}
            after_run = [c[0] for c in carry]
            ab, dl = {}, {}
            for t in tiles:
                sp, ls = sp_ls[t]
                ab[t] = _sb_weights(ls, after_run[t[1]] + inside[t], valid[t]).astype(MXU_DTYPE)
                dl[t] = ab[t].astype(F32) * da[t]
                after_run[t[1]] = after_run[t[1]] + jnp.sum(sp, axis=1, keepdims=True)
            from_in = {t: _cumsum_dot(dl[t], tri_from) for t in tiles}
            from_run = [c[1] for c in carry]
            dq = [c[2] for c in carry]
            for d in blocks:
                start, kb, _ = kv[d]
                dk = dv = None
                for u in range(d if diagonal else 0, n_sub):
                    t = (d, u)
                    dz = dl[t] - jnp.exp(sp_ls[t][1]) * (dl[t] + total[u] - (from_run[u] + from_in[t]))
                    if valid[t] is not None:
                        dz = jnp.where(valid[t], dz, 0.0)
                    dzb = dz.astype(MXU_DTYPE)
                    from_run[u] = from_run[u] + jnp.sum(dl[t], axis=1, keepdims=True)
                    dq[u] = dq[u] + _dot(dzb, kb, NN)
                    dk_u, dv_u = _dot(dzb, qs[u], TN), _dot(ab[t], dob[u], TN)
                    dk = dk_u if dk is None else dk + dk_u
                    dv = dv_u if dv is None else dv + dv_u
                dk_ref[0, pl.ds(start, SB_BLK), :] += dk
                dv_ref[0, pl.ds(start, SB_BLK), :] += dv
            return tuple(zip(after_run, from_run, dq))

        carry = tuple((jnp.zeros((SB_BLK, 1), F32), jnp.zeros((SB_BLK, 1), F32), jnp.zeros((SB_BLK, dh), F32)) for _ in range(n_sub))
        carry = key_blocks(base, carry, True)
        carry = _sb_walk(base // SB_WALK, carry, lambda t, c: key_blocks(base - SB_WALK * (t + 1), c, False))
        for u in range(n_sub):
            dq_ref[0, u * SB_BLK:(u + 1) * SB_BLK, :] = carry[u][2] * scale

    blk = pl.BlockSpec((1, tq, dh), lambda h, i: (h, i, 0))
    full = pl.BlockSpec((1, s, dh), lambda h, i: (h, 0, 0))
    shape = jax.ShapeDtypeStruct((nh, s, dh), F32)
    return pl.pallas_call(
        body,
        name=name,
        grid=(nh, s // tq),
        in_specs=[blk, full, full, blk, blk],
        out_specs=[blk, full, full],
        out_shape=[shape, shape, shape],
        compiler_params=_params(("parallel", "arbitrary")),
    )(q, k, v, o, do)


BAND_TQ = 4096


def _band_scores(q_ref, k_ref, i, sub, tq, length, max_dist, scale):
    t0 = i * tq + sub * ATT_BLK
    ks = pl.multiple_of(jnp.minimum(jnp.maximum(t0 - ATT_BLK, 0), length - 2 * ATT_BLK), ATT_BLK)
    qs = q_ref[0, sub * ATT_BLK:(sub + 1) * ATT_BLK, :] * scale
    kw = k_ref[0, pl.ds(ks, 2 * ATT_BLK), :]
    sc = _dot(qs, kw, NT)
    diff = (t0 + lax.broadcasted_iota(jnp.int32, sc.shape, 0)) - (ks + lax.broadcasted_iota(jnp.int32, sc.shape, 1))
    valid = (diff >= 0) & (diff <= max_dist)
    return ks, qs, kw, jnp.where(valid, sc, NEG_INF)


def _band_fwd(q, k, v, sinks, max_dist, name):
    bq, length, dh = q.shape
    group = bq // k.shape[0]
    tq = min(BAND_TQ, length)
    scale = dh ** -0.5
    n_sink = 0 if sinks is None else sinks.shape[0]

    def body(*refs):
        if n_sink:
            sink_ref, q_ref, k_ref, v_ref, o_ref, lse_ref = refs
            sink = sink_ref[pl.program_id(0) % n_sink]
        else:
            q_ref, k_ref, v_ref, o_ref, lse_ref = refs
        i = pl.program_id(1)
        scores = [_band_scores(q_ref, k_ref, i, sub, tq, length, max_dist, scale) for sub in range(tq // ATT_BLK)]
        for sub, (ks, _, _, sc) in enumerate(scores):
            m = jnp.max(sc, axis=1, keepdims=True)
            if n_sink:
                m = jnp.maximum(m, sink)
            e = jnp.exp(sc - m)
            den = jnp.sum(e, axis=1, keepdims=True)
            if n_sink:
                den = den + jnp.exp(sink - m)
            rows = slice(sub * ATT_BLK, (sub + 1) * ATT_BLK)
            o_ref[0, rows, :] = _dot(e / den, v_ref[0, pl.ds(ks, 2 * ATT_BLK), :], NN)
            lse_ref[0, rows, :] = m + jnp.log(den)

    qblk = pl.BlockSpec((1, tq, dh), lambda b, i: (b, i, 0))
    kfull = pl.BlockSpec((1, length, dh), lambda b, i: (b // group, 0, 0))
    in_specs = [qblk, kfull, kfull]
    args = [q, k, v]
    if n_sink:
        in_specs = [pl.BlockSpec(memory_space=pltpu.SMEM)] + in_specs
        args = [sinks] + args
    return pl.pallas_call(
        body,
        name=name,
        grid=(bq, length // tq),
        in_specs=in_specs,
        out_specs=[qblk, pl.BlockSpec((1, tq, 1), lambda b, i: (b, i, 0))],
        out_shape=[jax.ShapeDtypeStruct((bq, length, dh), F32), jax.ShapeDtypeStruct((bq, length, 1), F32)],
        compiler_params=_params(("parallel", "parallel")),
    )(*args)


def _band_bwd(q, k, v, do, lse, delta, sinks, max_dist, name):
    bq, length, dh = q.shape
    group = bq // k.shape[0]
    tq = min(BAND_TQ, length)
    scale = dh ** -0.5
    n_sink = 0 if sinks is None else sinks.shape[0]

    def body(*refs):
        if n_sink:
            sink_ref, q_ref, k_ref, v_ref, do_ref, lse_ref, dl_ref, dq_ref, dk_ref, dv_ref, ds_ref = refs
            sink = sink_ref[pl.program_id(0) % n_sink]
        else:
            q_ref, k_ref, v_ref, do_ref, lse_ref, dl_ref, dq_ref, dk_ref, dv_ref, ds_ref = refs
        i = pl.program_id(1)

        @pl.when((i == 0) & (pl.program_id(0) % group == 0))
        def _():
            dk_ref[...] = jnp.zeros_like(dk_ref)
            dv_ref[...] = jnp.zeros_like(dv_ref)

        @pl.when(i == 0)
        def _():
            ds_ref[...] = jnp.zeros_like(ds_ref)

        scores = [_band_scores(q_ref, k_ref, i, sub, tq, length, max_dist, scale) for sub in range(tq // ATT_BLK)]
        dps = [_dot(do_ref[0, sub * ATT_BLK:(sub + 1) * ATT_BLK, :], v_ref[0, pl.ds(ks, 2 * ATT_BLK), :], NT) for sub, (ks, _, _, _) in enumerate(scores)]
        for sub, (ks, qs, kw, sc) in enumerate(scores):
            rows = slice(sub * ATT_BLK, (sub + 1) * ATT_BLK)
            lse = lse_ref[0, rows, :]
            delta_r = dl_ref[0, rows, :]
            dob = do_ref[0, rows, :]
            p = jnp.exp(sc - lse)
            dsb = (p * (dps[sub] - delta_r)).astype(MXU_DTYPE)
            dq_ref[0, rows, :] = _dot(dsb, kw, NN) * scale
            dk_ref[0, pl.ds(ks, 2 * ATT_BLK), :] += _dot(dsb, qs, TN)
            dv_ref[0, pl.ds(ks, 2 * ATT_BLK), :] += _dot(p, dob, TN)
            if n_sink:
                ds_ref[...] += jnp.sum(-jnp.exp(sink - lse) * delta_r, axis=0, keepdims=True)

    qblk = pl.BlockSpec((1, tq, dh), lambda b, i: (b, i, 0))
    qcol = pl.BlockSpec((1, tq, 1), lambda b, i: (b, i, 0))
    kfull = pl.BlockSpec((1, length, dh), lambda b, i: (b // group, 0, 0))
    in_specs = [qblk, kfull, kfull, qblk, qcol, qcol]
    args = [q, k, v, do, lse, delta]
    if n_sink:
        in_specs = [pl.BlockSpec(memory_space=pltpu.SMEM)] + in_specs
        args = [sinks] + args
    kshape = jax.ShapeDtypeStruct((k.shape[0], length, dh), F32)
    return pl.pallas_call(
        body,
        name=name,
        grid=(bq, length // tq),
        in_specs=in_specs,
        out_specs=[qblk, kfull, kfull, pl.BlockSpec((1, 8, 128), lambda b, i: (b, 0, 0))],
        out_shape=[jax.ShapeDtypeStruct((bq, length, dh), F32), kshape, kshape, jax.ShapeDtypeStruct((bq, 8, 128), F32)],
        compiler_params=_params(("arbitrary", "arbitrary")),
    )(*args)


def _merge_weights(lse_refs):
    lses = [r[0] for r in lse_refs]
    m = functools.reduce(jnp.maximum, lses)
    es = [jnp.exp(l - m) for l in lses]
    den = functools.reduce(lambda a, b: a + b, es)
    return [e / den for e in es]


def _merge_fwd(outs, lses, name):
    n = len(outs)
    nh, s, dh = outs[0].shape
    ts = _tile(s, 1024, 8)

    def body(*refs):
        ws = _merge_weights(refs[n:2 * n])
        o = functools.reduce(lambda a, b: a + b, [w * r[0] for w, r in zip(ws, refs[:n])])
        refs[2 * n][0] = o
        refs[2 * n + 1][0] = o.astype(MXU_DTYPE)

    blk = pl.BlockSpec((1, ts, dh), lambda h, i: (h, i, 0))
    col = pl.BlockSpec((1, ts, 1), lambda h, i: (h, i, 0))
    return pl.pallas_call(
        body,
        name=name,
        grid=(nh, s // ts),
        in_specs=[blk] * n + [col] * n,
        out_specs=[blk, blk],
        out_shape=[jax.ShapeDtypeStruct((nh, s, dh), F32), jax.ShapeDtypeStruct((nh, s, dh), MXU_DTYPE)],
        compiler_params=_params(("parallel", "parallel")),
    )(*outs, *lses)


def _merge_bwd(do, o, lses, name):
    n = len(lses)
    nh, s, dh = o.shape
    ts = _tile(s, 1024, 8)

    def body(*refs):
        do_ref, o_ref = refs[:2]
        ws = _merge_weights(refs[2:2 + n])
        dof = do_ref[0].astype(F32)
        base = jnp.sum(dof * o_ref[0], axis=1, keepdims=True)
        for g in range(n):
            refs[2 + n + g][0] = (ws[g] * dof).astype(MXU_DTYPE)
            refs[2 + 2 * n + g][0] = ws[g] * base

    blk = pl.BlockSpec((1, ts, dh), lambda h, i: (h, i, 0))
    col = pl.BlockSpec((1, ts, 1), lambda h, i: (h, i, 0))
    res = pl.pallas_call(
        body,
        name=name,
        grid=(nh, s // ts),
        in_specs=[blk, blk] + [col] * n,
        out_specs=[blk] * n + [col] * n,
        out_shape=[jax.ShapeDtypeStruct((nh, s, dh), MXU_DTYPE)] * n + [jax.ShapeDtypeStruct((nh, s, 1), F32)] * n,
        compiler_params=_params(("parallel", "parallel")),
    )(do, o, *lses)
    return res[:n], res[n:]


def _to_heads(a, dil=1):
    s, n = a.shape
    nh = n // HEAD_DIM
    a = a.reshape(s // dil, dil, nh, HEAD_DIM).transpose(1, 2, 0, 3)
    return a.reshape(dil * nh, s // dil, HEAD_DIM)


def _from_heads(a, dil=1):
    b, ls, c = a.shape
    nh = b // dil
    return a.reshape(dil, nh, ls, c).transpose(2, 0, 1, 3).reshape(ls * dil, nh * c)


def _restride(a, dil):
    nh, s, c = a.shape
    return a.reshape(nh, s // dil, dil, c).transpose(2, 0, 1, 3).reshape(dil * nh, s // dil, c)


def _unstride(a, dil):
    b, ls, c = a.shape
    nh = b // dil
    return a.reshape(dil, nh, ls, c).transpose(1, 2, 0, 3).reshape(nh, ls * dil, c)


ANY = pl.BlockSpec(memory_space=pl.ANY)


def _position():
    x, y, c = lax.axis_index("x"), lax.axis_index("y"), lax.axis_index("c")
    return x, y, c, [(1 - x, y), (x, 1 - y), (1 - x, 1 - y)]


def _remote(src, dst, send_sems, recv_sems, k, to):
    return pltpu.make_async_remote_copy(src_ref=src, dst_ref=dst, send_sem=send_sems.at[k], recv_sem=recv_sems.at[k], device_id=to, device_id_type=MESH)


def _gather_chips(arrs, name):
    n = len(arrs)

    def body(*refs):
        f_refs, g_refs = refs[:n], refs[n:2 * n]
        send_sems, recv_sems = refs[2 * n:]
        x, y, c, chips = _position()
        me, sibling = (x, y, c), (x, y, 1 - c)

        def half(j, ref, sel):
            rows = f_refs[j].shape[1] // 2
            return ref.at[:, pl.ds(sel * rows, rows), :]

        def slot(j, chip, sel):
            return half(j, g_refs[j].at[2 * chip[0] + chip[1]], sel)

        first = [_remote(half(j, f_refs[j], c), slot(j, (x, y), c), send_sems, recv_sems, 6 * j + k, (*chip, c)) for j in range(n) for k, chip in enumerate(chips)]
        for cp in first:
            cp.start()
        passed = []
        for j in range(n):
            for k, chip in enumerate(chips):
                _remote(slot(j, chip, c), slot(j, chip, c), send_sems, recv_sems, 6 * j + k, me).wait_recv()
                passed.append(_remote(slot(j, chip, c), slot(j, chip, c), send_sems, recv_sems, 6 * j + 3 + k, sibling))
                passed[-1].start()
        for j in range(n):
            for k, chip in enumerate(chips):
                _remote(slot(j, chip, 1 - c), slot(j, chip, 1 - c), send_sems, recv_sems, 6 * j + 3 + k, me).wait_recv()
        for cp in first + passed:
            cp.wait_send()

    return pl.pallas_call(
        body,
        name=name,
        in_specs=[ANY] * n,
        out_specs=[ANY] * n,
        out_shape=[jax.ShapeDtypeStruct((N_CHIPS,) + a.shape, a.dtype) for a in arrs],
        scratch_shapes=[pltpu.SemaphoreType.DMA((6 * n,)), pltpu.SemaphoreType.DMA((6 * n,))],
    )(*arrs)


def _swap_sibling(arrs, name):
    n = len(arrs)

    def body(*refs):
        x, y, c, _ = _position()
        send_sems, recv_sems = refs[2 * n:]
        copies = [_remote(refs[j], refs[n + j], send_sems, recv_sems, j, (x, y, 1 - c)) for j in range(n)]
        for cp in copies:
            cp.start()
        for cp in copies:
            cp.wait()

    return pl.pallas_call(
        body,
        name=name,
        in_specs=[ANY] * n,
        out_specs=[ANY] * n,
        out_shape=[jax.ShapeDtypeStruct(a.shape, a.dtype) for a in arrs],
        scratch_shapes=[pltpu.SemaphoreType.DMA((n,)), pltpu.SemaphoreType.DMA((n,))],
    )(*arrs)


def _chips_copies(p_refs, q_refs, send_sems, recv_sems):
    x, y, c, chips = _position()
    my_slot = 2 * x + y
    n = len(p_refs)
    sends = [_remote(p_refs[j].at[2 * px + py], q_refs[j].at[my_slot], send_sems, recv_sems, 3 * j + k, (px, py, c)) for j in range(n) for k, (px, py) in enumerate(chips)]
    arrivals = [_remote(q_refs[j].at[2 * px + py], q_refs[j].at[2 * px + py], send_sems, recv_sems, 3 * j + k, (x, y, c)) for j in range(n) for k, (px, py) in enumerate(chips)]
    return sends, arrivals


def _chips_finish(sends, arrivals):
    for cp in arrivals:
        cp.wait_recv()
    for cp in sends:
        cp.wait_send()


def _exchange_chips(parts, name):
    n = len(parts)

    def body(*refs):
        sends, arrivals = _chips_copies(refs[:n], refs[n:2 * n], *refs[2 * n:])
        for cp in sends:
            cp.start()
        _chips_finish(sends, arrivals)

    return pl.pallas_call(
        body,
        name=name,
        in_specs=[ANY] * n,
        out_specs=[ANY] * n,
        out_shape=[jax.ShapeDtypeStruct(a.shape, a.dtype) for a in parts],
        scratch_shapes=[pltpu.SemaphoreType.DMA((3 * n,)), pltpu.SemaphoreType.DMA((3 * n,))],
    )(*parts)


def _share_sibling(halves, name):
    n = len(halves)

    def body(*refs):
        h_refs, o_refs = refs[:n], refs[n:2 * n]
        send_sems, recv_sems = refs[2 * n:]
        x, y, c, _ = _position()

        def rows(j, sel):
            r = h_refs[j].shape[0]
            return o_refs[j].at[pl.ds(sel * r, r), :]

        sends = [_remote(h_refs[j], rows(j, c), send_sems, recv_sems, j, (x, y, 1 - c)) for j in range(n)]
        for cp in sends:
            cp.start()
        for j in range(n):
            _remote(rows(j, 1 - c), rows(j, 1 - c), send_sems, recv_sems, j, (x, y, c)).wait_recv()
        for cp in sends:
            cp.wait_send()

    return pl.pallas_call(
        body,
        name=name,
        in_specs=[ANY] * n,
        out_specs=[ANY] * n,
        out_shape=[jax.ShapeDtypeStruct((2 * a.shape[0],) + a.shape[1:], a.dtype) for a in halves],
        scratch_shapes=[pltpu.SemaphoreType.DMA((n,)), pltpu.SemaphoreType.DMA((n,))],
    )(*halves)


def _sum_rows(r, c, n_in):
    return _tile(r, max(16, (1 << 20) // (c * (n_in + 1))), 16)


def _add2(a, b, out_dtype, name):
    n, r, c = a.shape
    tr = _sum_rows(r, c, 2)

    def body(a_ref, b_ref, o_ref):
        o_ref[...] = (a_ref[...].astype(F32) + b_ref[...].astype(F32)).astype(o_ref.dtype)

    blk = pl.BlockSpec((1, tr, c), lambda s, i: (s, i, 0))
    return pl.pallas_call(
        body,
        name=name,
        grid=(n, r // tr),
        in_specs=[blk, blk],
        out_specs=blk,
        out_shape=jax.ShapeDtypeStruct(a.shape, out_dtype),
        compiler_params=_params(("parallel", "parallel")),
    )(a, b)


def _sum_slots(q, name):
    n, r, c = q.shape
    tr = _sum_rows(r, c, n)

    def body(q_ref, o_ref):
        acc = q_ref[0].astype(F32)
        for s in range(1, n):
            acc = acc + q_ref[s].astype(F32)
        o_ref[...] = acc

    return pl.pallas_call(
        body,
        name=name,
        grid=(r // tr,),
        in_specs=[pl.BlockSpec((n, tr, c), lambda i: (0, i, 0))],
        out_specs=pl.BlockSpec((tr, c), lambda i: (i, 0)),
        out_shape=jax.ShapeDtypeStruct((r, c), F32),
        compiler_params=_params(("parallel",)),
    )(q)


def _adamw(w, g, m, v, name):
    r, c = w.shape
    tr = _tile(r, 256, 8)
    c1 = 1.0 - ADAM_B1 ** ADAM_STEP
    c2 = 1.0 - ADAM_B2 ** ADAM_STEP

    def body(w_ref, g_ref, m_ref, v_ref, d_ref, nm_ref, nv_ref):
        g = g_ref[...]
        nm = ADAM_B1 * m_ref[...] + (1.0 - ADAM_B1) * g
        nv = ADAM_B2 * v_ref[...] + (1.0 - ADAM_B2) * (g * g)
        d_ref[...] = -ADAM_LR * ((nm / c1) / (jnp.sqrt(nv / c2) + ADAM_EPS) + ADAM_WD * w_ref[...])
        nm_ref[...] = nm
        nv_ref[...] = nv

    blk = pl.BlockSpec((tr, c), lambda i: (i, 0))
    shape = jax.ShapeDtypeStruct((r, c), F32)
    return pl.pallas_call(
        body,
        name=name,
        grid=(r // tr,),
        in_specs=[blk] * 4,
        out_specs=[blk] * 3,
        out_shape=[shape] * 3,
        compiler_params=_params(("parallel",)),
    )(w, g, m, v)


WEIGHTS = ("ffn1_w_in", "ffn1_w_out", "ffn2_w_in", "ffn2_w_out", "ln_g", "ln_b", "sb_w_in", "sb_w_out", "swa_w_in", "swa_sinks", "swa_w_out", "dil_w_in", "dil_w_out", "ple_w_proj", "ple_w_gate")
SHARD_AXIS = {"ffn1_w_in": 2, "ffn1_w_out": 1, "ffn2_w_in": 2, "ffn2_w_out": 1, "ln_g": 2, "ln_b": 2, "sb_w_in": 2, "sb_w_out": 1, "swa_w_in": 2, "swa_sinks": None, "swa_w_out": 1, "dil_w_in": 2, "dil_w_out": 1, "ple_w_proj": 2, "ple_w_gate": 1}
SMALL = ("ln_g", "ln_b", "swa_sinks")
SMALL_COLS = 128
SMALL_UNIT = 16 * SMALL_COLS


def _pack_small(pieces, lead):
    flat = jnp.concatenate([a.reshape(lead + (-1,)) for a in pieces], axis=-1)
    n = flat.shape[-1]
    flat = jnp.pad(flat, [(0, 0)] * len(lead) + [(0, -n % SMALL_UNIT)])
    return flat.reshape(lead + (-1, SMALL_COLS))


def _unpack_small(buf, shapes, lead):
    flat = buf.reshape(lead + (-1,))
    out, off = [], 0
    for shp in shapes:
        n = math.prod(shp)
        out.append(flat[..., off:off + n].reshape(lead + tuple(shp)))
        off += n
    return out


def _reduce_pair(grads, c, wire, tag):
    keep = [lax.dynamic_slice_in_dim(g, c * (g.shape[1] // 2), g.shape[1] // 2, axis=1) for g in grads]
    give = [lax.dynamic_slice_in_dim(g, (1 - c) * (g.shape[1] // 2), g.shape[1] // 2, axis=1).astype(wire) for g in grads]
    got = _swap_sibling(give, f"{tag}_pair")
    return [_add2(k, g, wire, f"{tag}_pair_sum{j}") for j, (k, g) in enumerate(zip(keep, got))]


def _reduce_finish(part, landed, c, tag):
    landed = [_fill_slot(q, lax.dynamic_slice_in_dim(p, _my_slot(), 1, axis=0)) for q, p in zip(landed, part)]
    halves = [_sum_slots(q, f"{tag}_chips_sum{j}") for j, q in enumerate(landed)]
    shared = _share_sibling(halves, f"{tag}_share")
    return [lax.dynamic_update_slice_in_dim(s, h, c * h.shape[0], axis=0) for s, h in zip(shared, halves)]


def _reduce_group(grads, c, wire, tag):
    part = _reduce_pair(grads, c, wire, tag)
    return _reduce_finish(part, _exchange_chips(part, f"{tag}_chips"), c, tag)


def _my_slot():
    return 2 * lax.axis_index("x") + lax.axis_index("y")


def _fill_slot(slots, mine):
    return lax.dynamic_update_slice_in_dim(slots, mine, _my_slot(), axis=0)


def _full_from_shards(g, axis):
    g = jnp.moveaxis(g, 0, axis)
    return g.reshape(g.shape[:axis] + (g.shape[axis] * g.shape[axis + 1],) + g.shape[axis + 2:])


def _shards_from_full(a, axis):
    a = a.reshape(a.shape[:axis] + (N_CHIPS, a.shape[axis] // N_CHIPS) + a.shape[axis + 1:])
    return jnp.moveaxis(a, axis, 0)


def _ffn_forward(x, xb, w_in, w_out, g, b, alpha, tag):
    gate, up, act = _ffn_in(xb, w_in, f"{tag}_in")
    out, outb, xhat, rstd = _mm_ln(act, w_out, x, g, b, alpha, 0.5, f"{tag}_out_ln")
    return (out, outb), dict(xb=xb, gate=gate, up=up, act=act, xhat=xhat, rstd=rstd)


EXCHANGE_HOSTS = ("dw_out", "dw_in", "dx")


def _exchange_host(name):
    if name in ("ffn1_w_in", "ffn2_w_in", "ffn1_w_out"):
        return "dw_in"
    return "dx" if name.endswith("_w_in") else "dw_out"


def _ffn_backward(dout, saved, w_in, w_out, g, tag, exchange=None):
    exchange = exchange or {}
    dr, dyb, dg, db = _ln_bwd(*dout, saved["xhat"], saved["rstd"], g, 0.5, f"{tag}_ln_bwd")
    dh = _ffn_dact(dyb, w_out, saved["gate"], saved["up"], f"{tag}_dact")
    out = {
        "dw_out": _mm(saved["act"], dyb, "tn", F32, f"{tag}_dw_out", exchange=exchange.get("dw_out") or None),
        "dw_in": _mm(saved["xb"], dh, "tn", F32, f"{tag}_dw_in", split_b=True, exchange=exchange.get("dw_in") or None),
        "dx": _mm(dh, w_in, "nt", F32, f"{tag}_dx", split_a=True, exchange=exchange.get("dx") or None),
    }
    landed = {h: out[h][1] for h in EXCHANGE_HOSTS if exchange.get(h)}
    res = {h: out[h][0] if exchange.get(h) else out[h] for h in EXCHANGE_HOSTS}
    return dr, res["dx"], res["dw_in"], res["dw_out"], dg, db, landed


def _sb_forward(xb, w_in, w_out, tag):
    nw = w_out.shape[0]
    h = _mm(xb, w_in, "nn", MXU_DTYPE, f"{tag}_qkv")
    q, k, v = (_to_heads(h[:, j * nw:(j + 1) * nw]) for j in range(3))
    o, ob = _sb_fwd(q, k, v, f"{tag}_att")
    ob = _from_heads(ob)
    return ob, dict(q=q, k=k, v=v, o=o, ob=ob)


def _sb_backward(dmix, saved, xb, w_in, w_out, tag):
    dw_out = _mm(saved["ob"], dmix, "tn", F32, f"{tag}_dw_out")
    do = _to_heads(_mm(dmix, w_out, "nt", MXU_DTYPE, f"{tag}_do"))
    dq, dk, dv = _sb_bwd(saved["q"], saved["k"], saved["v"], saved["o"], do, f"{tag}_att_bwd")
    dh = jnp.concatenate([_from_heads(t) for t in (dq, dk, dv)], axis=1).astype(MXU_DTYPE)
    dw_in = _mm(xb, dh, "tn", F32, f"{tag}_dw_in")
    return _mm(dh, w_in, "nt", F32, f"{tag}_dx"), dw_in, dw_out


def _swa_forward(xb, w_in, sinks, w_out, tables, tag):
    nq = w_out.shape[0]
    nkv = (w_in.shape[1] - nq) // 2
    h = _mm(xb, w_in, "nn", F32, f"{tag}_qkv")
    n_rot = (nq + nkv) // 128
    hb = _rope(h, *tables, lambda j: j // n_rot, MXU_DTYPE, f"{tag}_rope")
    q, k, v = _to_heads(hb[:, :nq]), _to_heads(hb[:, nq:nq + nkv]), _to_heads(hb[:, nq + nkv:])
    o, lse = _band_fwd(q, k, v, sinks, SWA_WINDOW - 1, f"{tag}_att")
    o, ob = _merge_fwd([o], [lse], f"{tag}_cast")
    ob = _from_heads(ob)
    return ob, dict(q=q, k=k, v=v, o=o, lse=lse, ob=ob, n_rot=n_rot)


def _swa_backward(dmix, saved, xb, w_in, sinks, w_out, tables, tag):
    dw_out = _mm(saved["ob"], dmix, "tn", F32, f"{tag}_dw_out")
    do = _to_heads(_mm(dmix, w_out, "nt", MXU_DTYPE, f"{tag}_do"))
    (dog,), (delta,) = _merge_bwd(do, saved["o"], [saved["lse"]], f"{tag}_delta")
    dq, dk, dv, dsink = _band_bwd(saved["q"], saved["k"], saved["v"], dog, saved["lse"], delta, sinks, SWA_WINDOW - 1, f"{tag}_att_bwd")
    dh = jnp.concatenate([_from_heads(t) for t in (dq, dk, dv)], axis=1)
    n_rot = saved["n_rot"]
    dhb = _rope(dh, tables[0], -tables[1], lambda j: j // n_rot, MXU_DTYPE, f"{tag}_rope_bwd")
    dw_in = _mm(xb, dhb, "tn", F32, f"{tag}_dw_in")
    return _mm(dhb, w_in, "nt", F32, f"{tag}_dx"), dw_in, dw_out, dsink[:, 0, 0]


def _dil_forward(xb, w_in, w_out, tables, tag):
    nw = w_out.shape[0]
    h = _mm(xb, w_in, "nn", F32, f"{tag}_qkv")
    hb = _rope(h, *tables, lambda j: (j % 3) // 2, MXU_DTYPE, f"{tag}_rope", width=nw)
    qkv, outs, lses = [], [], []
    for gi, (win, dil) in enumerate(DIL_GROUPS):
        base = gi * 3 * nw
        q, k, v = (_to_heads(hb[:, base + j * nw:base + (j + 1) * nw], dil) for j in range(3))
        o, lse = _band_fwd(q, k, v, None, win // dil, f"{tag}_att{gi}")
        qkv.append((q, k, v))
        outs.append(_unstride(o, dil))
        lses.append(_unstride(lse, dil))
    o, ob = _merge_fwd(outs, lses, f"{tag}_merge")
    ob = _from_heads(ob)
    return ob, dict(qkv=qkv, o=o, lses=lses, ob=ob)


def _dil_backward(dmix, saved, xb, w_in, w_out, tables, tag):
    dw_out = _mm(saved["ob"], dmix, "tn", F32, f"{tag}_dw_out")
    do = _to_heads(_mm(dmix, w_out, "nt", MXU_DTYPE, f"{tag}_do"))
    dogs, deltas = _merge_bwd(do, saved["o"], saved["lses"], f"{tag}_merge_bwd")
    parts = []
    for gi, (win, dil) in enumerate(DIL_GROUPS):
        q, k, v = saved["qkv"][gi]
        dq, dk, dv, _ = _band_bwd(q, k, v, _restride(dogs[gi], dil), _restride(saved["lses"][gi], dil), _restride(deltas[gi], dil), None, win // dil, f"{tag}_att{gi}_bwd")
        parts += [_from_heads(t, dil) for t in (dq, dk, dv)]
    dh = jnp.concatenate(parts, axis=1)
    dhb = _rope(dh, tables[0], -tables[1], lambda j: (j % 3) // 2, MXU_DTYPE, f"{tag}_rope_bwd", width=w_out.shape[0])
    dw_in = _mm(xb, dhb, "tn", F32, f"{tag}_dw_in")
    return _mm(dhb, w_in, "nt", F32, f"{tag}_dx"), dw_in, dw_out


def kernel(x, p, ffn1_w_in, ffn1_w_out, ffn2_w_in, ffn2_w_out, ln_g, ln_b, sb_w_in, sb_w_out, swa_w_in, swa_sinks, swa_w_out, dil_w_in, dil_w_out, ple_w_proj, ple_w_gate, loss_target, m_ffn1_w_in, m_ffn1_w_out, m_ffn2_w_in, m_ffn2_w_out, m_ln_g, m_ln_b, m_sb_w_in, m_sb_w_out, m_swa_w_in, m_swa_sinks, m_swa_w_out, m_dil_w_in, m_dil_w_out, m_ple_w_proj, m_ple_w_gate, v_ffn1_w_in, v_ffn1_w_out, v_ffn2_w_in, v_ffn2_w_out, v_ln_g, v_ln_b, v_sb_w_in, v_sb_w_out, v_swa_w_in, v_swa_sinks, v_swa_w_out, v_dil_w_in, v_dil_w_out, v_ple_w_proj, v_ple_w_gate):
    shard = dict(ffn1_w_in=ffn1_w_in, ffn1_w_out=ffn1_w_out, ffn2_w_in=ffn2_w_in, ffn2_w_out=ffn2_w_out, ln_g=ln_g, ln_b=ln_b, sb_w_in=sb_w_in, sb_w_out=sb_w_out, swa_w_in=swa_w_in, swa_sinks=swa_sinks, swa_w_out=swa_w_out, dil_w_in=dil_w_in, dil_w_out=dil_w_out, ple_w_proj=ple_w_proj, ple_w_gate=ple_w_gate)
    mom_m = dict(ffn1_w_in=m_ffn1_w_in, ffn1_w_out=m_ffn1_w_out, ffn2_w_in=m_ffn2_w_in, ffn2_w_out=m_ffn2_w_out, ln_g=m_ln_g, ln_b=m_ln_b, sb_w_in=m_sb_w_in, sb_w_out=m_sb_w_out, swa_w_in=m_swa_w_in, swa_sinks=m_swa_sinks, swa_w_out=m_swa_w_out, dil_w_in=m_dil_w_in, dil_w_out=m_dil_w_out, ple_w_proj=m_ple_w_proj, ple_w_gate=m_ple_w_gate)
    mom_v = dict(ffn1_w_in=v_ffn1_w_in, ffn1_w_out=v_ffn1_w_out, ffn2_w_in=v_ffn2_w_in, ffn2_w_out=v_ffn2_w_out, ln_g=v_ln_g, ln_b=v_ln_b, sb_w_in=v_sb_w_in, sb_w_out=v_sb_w_out, swa_w_in=v_swa_w_in, swa_sinks=v_swa_sinks, swa_w_out=v_swa_w_out, dil_w_in=v_dil_w_in, dil_w_out=v_dil_w_out, ple_w_proj=v_ple_w_proj, ple_w_gate=v_ple_w_gate)
    depth = ffn1_w_in.shape[0]
    alpha = (2 * depth) ** 0.25
    c = lax.axis_index("c")

    mats = [n for n in WEIGHTS if n not in SMALL]
    local = [shard[n].astype(MXU_DTYPE) for n in mats] + [_pack_small([ln_g, ln_b], ())[None]]
    got = [_fill_slot(g, a[None]) for g, a in zip(_gather_chips(local, "gather_weights"), local)]
    full = {"swa_sinks": swa_sinks}
    for n, g in zip(mats, got):
        full[n] = _full_from_shards(g, SHARD_AXIS[n])
    for n, g in zip(("ln_g", "ln_b"), _unpack_small(got[-1][:, 0], [ln_g.shape, ln_b.shape], (N_CHIPS,))):
        full[n] = _full_from_shards(g, SHARD_AXIS[n])

    seq = x.shape[1]
    tables = _rope_tables(seq)
    xf = x[0]
    xb = xf.astype(MXU_DTYPE)
    saved = []
    for i in range(depth):
        kind, j = i % 3, i // 3
        mixer = ("sb", "swa", "dil")[kind]
        sv = {}
        (x1, x1b), sv["ffn1"] = _ffn_forward(xf, xb, full["ffn1_w_in"][i], full["ffn1_w_out"][i], full["ln_g"][i, 0], full["ln_b"][i, 0], alpha, f"l{i}_ffn1")
        if kind == 0:
            mix, sv["mix"] = _sb_forward(x1b, full["sb_w_in"][j], full["sb_w_out"][j], f"l{i}_sb")
        elif kind == 1:
            mix, sv["mix"] = _swa_forward(x1b, full["swa_w_in"][j], swa_sinks[j], full["swa_w_out"][j], tables, f"l{i}_swa")
        else:
            mix, sv["mix"] = _dil_forward(x1b, full["dil_w_in"][j], full["dil_w_out"][j], tables, f"l{i}_dil")
        x2, x2b, sv["xhat2"], sv["rstd2"] = _mm_ln(mix, full[f"{mixer}_w_out"][j], x1, full["ln_g"][i, 1], full["ln_b"][i, 1], alpha, 1.0, f"l{i}_{mixer}_proj_ln")
        sv["x1b"] = x1b
        (x3, x3b), sv["ffn2"] = _ffn_forward(x2, x2b, full["ffn2_w_in"][i], full["ffn2_w_out"][i], full["ln_g"][i, 2], full["ln_b"][i, 2], alpha, f"l{i}_ffn2")
        xf, xb, sv["u"], sv["e"] = _ple_fwd(x3, x3b, p[i, 0], full["ple_w_gate"][i], full["ple_w_proj"][i], f"l{i}_ple")
        sv["x3b"] = x3b
        saved.append(sv)

    loss_part, dy = _loss(xf, loss_target[0], "loss")
    loss = lax.psum(loss_part[0, 0], ("x", "y", "c"))

    grads = {n: [None] * full[n].shape[0] for n in WEIGHTS if n not in ("ln_g", "ln_b")}
    gsum = {n: [None] * full[n].shape[0] for n in mats}
    dln_g = [[None] * 3 for _ in range(depth)]
    dln_b = [[None] * 3 for _ in range(depth)]
    dout = (dy, None, 1.0)
    pending = None

    def finish_reduce(layer, part, tag, landed):
        for (n, k), g in zip(layer, _reduce_finish(part, landed, c, tag)):
            gsum[n][k] = g

    for i in reversed(range(depth)):
        kind, j = i % 3, i // 3
        mixer = ("sb", "swa", "dil")[kind]
        sv = saved[i]
        dx4, dub, deb = _ple_bwd(*dout, sv["u"], sv["e"], f"l{i}_ple_bwd")
        grads["ple_w_gate"][i] = _mm(sv["x3b"], dub, "tn", F32, f"l{i}_ple_dw_gate")
        grads["ple_w_proj"][i] = _mm(p[i, 0], deb, "tn", F32, f"l{i}_ple_dw_proj")
        dxb = _mm(dub, full["ple_w_gate"][i], "nt", F32, f"l{i}_ple_dx")
        dr, dxb, grads["ffn2_w_in"][i], grads["ffn2_w_out"][i], dln_g[i][2], dln_b[i][2], landed = _ffn_backward((dx4, dxb, 1.0), sv["ffn2"], full["ffn2_w_in"][i], full["ffn2_w_out"][i], full["ln_g"][i, 2], f"l{i}_ffn2", exchange=pending and {h: [q for (n, _), q in zip(pending[0], pending[1]) if _exchange_host(n) == h] for h in EXCHANGE_HOSTS})
        if pending:
            arrived = {h: iter(v) for h, v in landed.items()}
            finish_reduce(*pending, [next(arrived[_exchange_host(n)]) for n, _ in pending[0]])
        dr, dmix, dln_g[i][1], dln_b[i][1] = _ln_bwd(dr, dxb, alpha, sv["xhat2"], sv["rstd2"], full["ln_g"][i, 1], 1.0, f"l{i}_mix_ln_bwd")
        if kind == 0:
            dxb, grads["sb_w_in"][j], grads["sb_w_out"][j] = _sb_backward(dmix, sv["mix"], sv["x1b"], full["sb_w_in"][j], full["sb_w_out"][j], f"l{i}_sb")
        elif kind == 1:
            dxb, grads["swa_w_in"][j], grads["swa_w_out"][j], grads["swa_sinks"][j] = _swa_backward(dmix, sv["mix"], sv["x1b"], full["swa_w_in"][j], swa_sinks[j], full["swa_w_out"][j], tables, f"l{i}_swa")
        else:
            dxb, grads["dil_w_in"][j], grads["dil_w_out"][j] = _dil_backward(dmix, sv["mix"], sv["x1b"], full["dil_w_in"][j], full["dil_w_out"][j], tables, f"l{i}_dil")
        dr, dxb, grads["ffn1_w_in"][i], grads["ffn1_w_out"][i], dln_g[i][0], dln_b[i][0], _ = _ffn_backward((dr, dxb, alpha), sv["ffn1"], full["ffn1_w_in"][i], full["ffn1_w_out"][i], full["ln_g"][i, 0], f"l{i}_ffn1")
        dout = (dr, dxb, alpha)
        layer = [("ffn1_w_in", i), ("ffn1_w_out", i), (f"{mixer}_w_in", j), (f"{mixer}_w_out", j), ("ffn2_w_in", i), ("ffn2_w_out", i), ("ple_w_proj", i), ("ple_w_gate", i)]
        pending = (layer, _reduce_pair([_shards_from_full(grads[n][k], SHARD_AXIS[n] - 1) for n, k in layer], c, MXU_DTYPE, f"l{i}_reduce"), f"l{i}_reduce")
    layer, part, tag = pending
    finish_reduce(layer, part, tag, _exchange_chips(part, f"{tag}_chips"))
    grad_x = _axpy(*dout, "grad_x")[None]

    gshard = {n: jnp.stack(g) for n, g in gsum.items()}
    small = [
        _shards_from_full(jnp.stack([jnp.concatenate(r, axis=0) for r in dln_g]), SHARD_AXIS["ln_g"]),
        _shards_from_full(jnp.stack([jnp.concatenate(r, axis=0) for r in dln_b]), SHARD_AXIS["ln_b"]),
        jnp.broadcast_to(jnp.stack(grads["swa_sinks"])[None], (N_CHIPS,) + swa_sinks.shape),
    ]
    (small_sum,) = _reduce_group([_pack_small(small, (N_CHIPS,))], c, F32, "small_reduce")
    for n, g in zip(SMALL, _unpack_small(small_sum, [shard[n].shape for n in SMALL], ())):
        gshard[n] = g

    delta, new_m, new_v = {}, {}, {}
    for n in WEIGHTS:
        shp = shard[n].shape
        two_d = (-1, shp[-1])
        d, nm, nv = _adamw(shard[n].reshape(two_d), gshard[n].reshape(two_d), mom_m[n].reshape(two_d), mom_v[n].reshape(two_d), f"adamw_{n}")
        delta[n], new_m[n], new_v[n] = d.reshape(shp), nm.reshape(shp), nv.reshape(shp)

    return (loss, grad_x, *[gshard[n] for n in WEIGHTS], *[delta[n] for n in WEIGHTS], *[new_m[n] for n in WEIGHTS], *[new_v[n] for n in WEIGHTS])
```

```python
import functools
import math

import jax
import jax.numpy as jnp
from jax import lax
from jax.experimental import pallas as pl
from jax.experimental.pallas import tpu as pltpu

F32 = jnp.float32
MXU_DTYPE = jnp.bfloat16
MESH = pl.DeviceIdType.MESH

HEAD_DIM = 64
ATT_BLK = 128
SWA_WINDOW = 128
DIL_GROUPS = ((128, 1), (512, 4), (2048, 16))
LN_EPS = 1e-5
ROPE_THETA = 10000.0
NEG_INF = -1e30
ADAM_LR, ADAM_B1, ADAM_B2, ADAM_EPS, ADAM_WD, ADAM_STEP = 0.001, 0.9, 0.999, 1e-08, 0.01, 10

VMEM_LIMIT_BYTES = 56 * 1024 * 1024
N_CHIPS = 4


def _params(sem=None):
    return pltpu.CompilerParams(dimension_semantics=sem, vmem_limit_bytes=VMEM_LIMIT_BYTES)


def _tile(n, target, unit):
    t = (min(target, n) // unit) * unit
    while t >= unit:
        if n % t == 0:
            return t
        t -= unit
    return n


def _dot(a, b, dims):
    return lax.dot_general(a.astype(MXU_DTYPE), b.astype(MXU_DTYPE), (dims, ((), ())), preferred_element_type=F32)


NN = ((1,), (0,))
NT = ((1,), (1,))
TN = ((0,), (0,))


def _accumulate(part, acc_ref, kk, nk, finish):
    if nk == 1:
        finish(part)
        return

    @pl.when(kk == 0)
    def _():
        acc_ref[...] = jnp.zeros_like(acc_ref)

    acc_ref[...] += part

    @pl.when(kk == nk - 1)
    def _():
        finish(acc_ref[...])


def _mm(a, b, mode, out_dtype, name, split_a=False, split_b=False, exchange=None, tm=1408, tn=1408, tk=2048):
    dims = {"nn": NN, "nt": NT, "tn": TN}[mode]
    if split_a:
        m, k = a.shape[1], 2 * a.shape[2]
    elif mode == "tn":
        k, m = a.shape
    else:
        m, k = a.shape
    if split_b:
        n = 2 * b.shape[2]
    elif mode == "nt":
        n = b.shape[0]
    else:
        n = b.shape[1]
    tm = _tile(m, tm, 128)
    tn = _tile(n // 2 if split_b else n, tn, 128)
    tk = _tile(k // 2 if split_a else k, tk, 128)
    nk = k // tk
    nk_half = nk // 2
    nn_half = (n // tn) // 2

    if split_a:
        a_spec = pl.BlockSpec((None, tm, tk), lambda i, j, kk: (kk // nk_half, i, kk % nk_half))
    elif mode == "tn":
        a_spec = pl.BlockSpec((tk, tm), lambda i, j, kk: (kk, i))
    else:
        a_spec = pl.BlockSpec((tm, tk), lambda i, j, kk: (i, kk))
    if split_b:
        b_spec = pl.BlockSpec((None, tk, tn), lambda i, j, kk: (j // nn_half, kk, j % nn_half))
    elif mode == "nt":
        b_spec = pl.BlockSpec((tn, tk), lambda i, j, kk: (j, kk))
    else:
        b_spec = pl.BlockSpec((tk, tn), lambda i, j, kk: (kk, j))

    parts = [] if exchange is None else list(exchange)
    n_ex = len(parts)
    grid = (m // tm, n // tn, nk)

    def body(*refs):
        a_ref, b_ref = refs[:2]
        p_refs, o_ref, q_refs, acc_ref = refs[2:2 + n_ex], refs[2 + n_ex], refs[3 + n_ex:3 + 2 * n_ex], refs[3 + 2 * n_ex]
        if n_ex:
            step = (pl.program_id(0) * grid[1] + pl.program_id(1)) * nk + pl.program_id(2)

            @pl.when(step == 0)
            def _():
                for cp in _chips_copies(p_refs, q_refs, *refs[4 + 2 * n_ex:])[0]:
                    cp.start()

        def finish(total):
            o_ref[...] = total.astype(o_ref.dtype)

        _accumulate(_dot(a_ref[...], b_ref[...], dims), acc_ref, pl.program_id(2), nk, finish)

        if n_ex:

            @pl.when(step == grid[0] * grid[1] * nk - 1)
            def _():
                _chips_finish(*_chips_copies(p_refs, q_refs, *refs[4 + 2 * n_ex:]))

    out = pl.pallas_call(
        body,
        name=name,
        grid=grid,
        in_specs=[a_spec, b_spec] + [ANY] * n_ex,
        out_specs=[pl.BlockSpec((tm, tn), lambda i, j, kk: (i, j))] + [ANY] * n_ex,
        out_shape=[jax.ShapeDtypeStruct((m, n), out_dtype)] + [jax.ShapeDtypeStruct(p.shape, p.dtype) for p in parts],
        scratch_shapes=[pltpu.VMEM((tm, tn), F32)] + [pltpu.SemaphoreType.DMA((3 * n_ex,))] * (2 if n_ex else 0),
        compiler_params=_params(("arbitrary", "arbitrary", "arbitrary") if n_ex else ("parallel", "parallel", "arbitrary")),
    )(a, b, *parts)
    return (out[0], out[1:]) if n_ex else out[0]


def _sigmoid(x):
    return 1.0 / (1.0 + jnp.exp(-x))


def _ffn_in(xb, w_in, name):
    s, d = xb.shape
    f = w_in.shape[1] // 2
    tm = _tile(s, 512, 128)
    tn = _tile(f, 1408, 128)
    nj = f // tn

    def body(x_ref, wg_ref, wu_ref, g_ref, u_ref, a_ref):
        x = x_ref[...]
        g = _dot(x, wg_ref[...], NN)
        u = _dot(x, wu_ref[...], NN)
        g_ref[...] = g.astype(g_ref.dtype)
        u_ref[...] = u.astype(u_ref.dtype)
        a_ref[...] = (g * _sigmoid(g) * u).astype(a_ref.dtype)

    out = pl.BlockSpec((tm, tn), lambda i, j: (i, j))
    return pl.pallas_call(
        body,
        name=name,
        grid=(s // tm, nj),
        in_specs=[
            pl.BlockSpec((tm, d), lambda i, j: (i, 0)),
            pl.BlockSpec((d, tn), lambda i, j: (0, j)),
            pl.BlockSpec((d, tn), lambda i, j: (0, j + nj)),
        ],
        out_specs=[out, out, out],
        out_shape=[
            jax.ShapeDtypeStruct((s, f), MXU_DTYPE),
            jax.ShapeDtypeStruct((s, f), MXU_DTYPE),
            jax.ShapeDtypeStruct((s, f), MXU_DTYPE),
        ],
        compiler_params=_params(("parallel", "parallel")),
    )(xb, w_in, w_in)


def _ffn_dact(dyb, w_out, gate, up, name):
    s, d = dyb.shape
    f = w_out.shape[0]
    tm = _tile(s, 512, 128)
    tn = _tile(f, 1408, 128)

    def body(dy_ref, w_ref, g_ref, u_ref, o_ref):
        dact = _dot(dy_ref[...], w_ref[...], NT)
        g = g_ref[...].astype(F32)
        sig = _sigmoid(g)
        o_ref[0] = (dact * u_ref[...].astype(F32) * (sig * (1.0 + g * (1.0 - sig)))).astype(o_ref.dtype)
        o_ref[1] = (dact * (g * sig)).astype(o_ref.dtype)

    tile = pl.BlockSpec((tm, tn), lambda i, j: (i, j))
    return pl.pallas_call(
        body,
        name=name,
        grid=(s // tm, f // tn),
        in_specs=[
            pl.BlockSpec((tm, d), lambda i, j: (i, 0)),
            pl.BlockSpec((tn, d), lambda i, j: (j, 0)),
            tile,
            tile,
        ],
        out_specs=pl.BlockSpec((2, tm, tn), lambda i, j: (0, i, j)),
        out_shape=jax.ShapeDtypeStruct((2, s, f), MXU_DTYPE),
        compiler_params=_params(("parallel", "parallel")),
    )(dyb, w_out, gate, up)


def _ple_fwd(x, xb, p, w_gate, w_proj, name):
    s, d = x.shape
    pd = p.shape[1]
    tm = _tile(s, 1024, 128)
    tn = _tile(d, 512, 128)

    def body(x_ref, xb_ref, p_ref, wg_ref, wp_ref, o_ref, ob_ref, u_ref, e_ref):
        u = _dot(xb_ref[...], wg_ref[...], NN)
        e = _dot(p_ref[...], wp_ref[...], NN)
        out = x_ref[...] + _sigmoid(u) * e
        o_ref[...] = out
        ob_ref[...] = out.astype(ob_ref.dtype)
        u_ref[...] = u
        e_ref[...] = e

    tile = pl.BlockSpec((tm, tn), lambda i, j: (i, j))
    return pl.pallas_call(
        body,
        name=name,
        grid=(s // tm, d // tn),
        in_specs=[
            tile,
            pl.BlockSpec((tm, d), lambda i, j: (i, 0)),
            pl.BlockSpec((tm, pd), lambda i, j: (i, 0)),
            pl.BlockSpec((d, tn), lambda i, j: (0, j)),
            pl.BlockSpec((pd, tn), lambda i, j: (0, j)),
        ],
        out_specs=[tile, tile, tile, tile],
        out_shape=[
            jax.ShapeDtypeStruct((s, d), F32),
            jax.ShapeDtypeStruct((s, d), MXU_DTYPE),
            jax.ShapeDtypeStruct((s, d), F32),
            jax.ShapeDtypeStruct((s, d), F32),
        ],
        compiler_params=_params(("parallel", "parallel")),
    )(x, xb, p, w_gate, w_proj)


def _rows_spec(ts, d):
    return pl.BlockSpec((ts, d), lambda i: (i, 0))


def _mm_ln(a, w, x, g, b, alpha, beta, name):
    s, k = a.shape
    d = w.shape[1]
    tm = _tile(s, 512, 128)
    tk = _tile(k, 1408, 128)
    nk = k // tk

    def body(a_ref, w_ref, x_ref, g_ref, b_ref, o_ref, ob_ref, xh_ref, rs_ref, acc_ref):
        def finish(y):
            r = alpha * x_ref[...] + beta * y
            mu = jnp.mean(r, axis=1, keepdims=True)
            cen = r - mu
            var = jnp.mean(cen * cen, axis=1, keepdims=True)
            rstd = lax.rsqrt(var + LN_EPS)
            xhat = cen * rstd
            out = xhat * g_ref[...] + b_ref[...]
            o_ref[...] = out
            ob_ref[...] = out.astype(ob_ref.dtype)
            xh_ref[...] = xhat
            rs_ref[...] = rstd

        _accumulate(_dot(a_ref[...], w_ref[...], NN), acc_ref, pl.program_id(1), nk, finish)

    rows = pl.BlockSpec((tm, d), lambda i, kk: (i, 0))
    vec = pl.BlockSpec((1, d), lambda i, kk: (0, 0))
    return pl.pallas_call(
        body,
        name=name,
        grid=(s // tm, nk),
        in_specs=[pl.BlockSpec((tm, tk), lambda i, kk: (i, kk)), pl.BlockSpec((tk, d), lambda i, kk: (kk, 0)), rows, vec, vec],
        out_specs=[rows, rows, rows, pl.BlockSpec((tm, 1), lambda i, kk: (i, 0))],
        out_shape=[
            jax.ShapeDtypeStruct((s, d), F32),
            jax.ShapeDtypeStruct((s, d), MXU_DTYPE),
            jax.ShapeDtypeStruct((s, d), F32),
            jax.ShapeDtypeStruct((s, 1), F32),
        ],
        scratch_shapes=[pltpu.VMEM((tm, d), F32)],
        compiler_params=_params(("parallel", "arbitrary")),
    )(a, w, x, g.reshape(1, d), b.reshape(1, d))


def _ln_bwd(ga, gb, ca, xhat, rstd, g, beta, name):
    s, d = xhat.shape
    ts = _tile(s, 512, 8)

    def body(ga_ref, gb_ref, xh_ref, rs_ref, g_ref, dr_ref, dyb_ref, dg_ref, db_ref):
        @pl.when(pl.program_id(0) == 0)
        def _():
            dg_ref[...] = jnp.zeros_like(dg_ref)
            db_ref[...] = jnp.zeros_like(db_ref)

        dout = ca * ga_ref[...] + gb_ref[...]
        xhat = xh_ref[...]
        dg_ref[...] += jnp.sum(dout * xhat, axis=0, keepdims=True)
        db_ref[...] += jnp.sum(dout, axis=0, keepdims=True)
        dxh = dout * g_ref[...]
        m1 = jnp.mean(dxh, axis=1, keepdims=True)
        m2 = jnp.mean(dxh * xhat, axis=1, keepdims=True)
        dr = rs_ref[...] * (dxh - m1 - xhat * m2)
        dr_ref[...] = dr
        dyb_ref[...] = (beta * dr).astype(dyb_ref.dtype)

    vec = pl.BlockSpec((1, d), lambda i: (0, 0))
    return pl.pallas_call(
        body,
        name=name,
        grid=(s // ts,),
        in_specs=[_rows_spec(ts, d), _rows_spec(ts, d), _rows_spec(ts, d), _rows_spec(ts, 1), vec],
        out_specs=[_rows_spec(ts, d), _rows_spec(ts, d), vec, vec],
        out_shape=[
            jax.ShapeDtypeStruct((s, d), F32),
            jax.ShapeDtypeStruct((s, d), MXU_DTYPE),
            jax.ShapeDtypeStruct((1, d), F32),
            jax.ShapeDtypeStruct((1, d), F32),
        ],
        compiler_params=_params(("arbitrary",)),
    )(ga, gb, xhat, rstd, g.reshape(1, d))


def _ple_bwd(ga, gb, ca, u, e, name):
    s, d = u.shape
    ts = _tile(s, 512, 8)
    grads = [ga] if gb is None else [ga, gb]

    def body(*refs):
        u_ref, e_ref, dx_ref, du_ref, de_ref = refs[len(grads):]
        dx = ca * refs[0][...]
        if gb is not None:
            dx = dx + refs[1][...]
        sig = _sigmoid(u_ref[...])
        dx_ref[...] = dx
        du_ref[...] = (dx * e_ref[...] * sig * (1.0 - sig)).astype(du_ref.dtype)
        de_ref[...] = (dx * sig).astype(de_ref.dtype)

    return pl.pallas_call(
        body,
        name=name,
        grid=(s // ts,),
        in_specs=[_rows_spec(ts, d)] * (len(grads) + 2),
        out_specs=[_rows_spec(ts, d)] * 3,
        out_shape=[
            jax.ShapeDtypeStruct((s, d), F32),
            jax.ShapeDtypeStruct((s, d), MXU_DTYPE),
            jax.ShapeDtypeStruct((s, d), MXU_DTYPE),
        ],
        compiler_params=_params(("parallel",)),
    )(*grads, u, e)


def _axpy(ga, gb, ca, name):
    s, d = ga.shape
    ts = _tile(s, 512, 8)

    def body(ga_ref, gb_ref, o_ref):
        o_ref[...] = ca * ga_ref[...] + gb_ref[...]

    return pl.pallas_call(
        body,
        name=name,
        grid=(s // ts,),
        in_specs=[_rows_spec(ts, d)] * 2,
        out_specs=_rows_spec(ts, d),
        out_shape=jax.ShapeDtypeStruct((s, d), F32),
        compiler_params=_params(("parallel",)),
    )(ga, gb)


def _loss(y, target, name):
    s, d = y.shape
    ts = _tile(s, 512, 8)

    def body(y_ref, t_ref, l_ref, dy_ref):
        @pl.when(pl.program_id(0) == 0)
        def _():
            l_ref[...] = jnp.zeros_like(l_ref)

        err = y_ref[...] - t_ref[...]
        l_ref[...] += (0.5 / d) * jnp.sum(jnp.sum(err * err, axis=1, keepdims=True), axis=0, keepdims=True)
        dy_ref[...] = err * (1.0 / d)

    return pl.pallas_call(
        body,
        name=name,
        grid=(s // ts,),
        in_specs=[_rows_spec(ts, d)] * 2,
        out_specs=[pl.BlockSpec((1, 1), lambda i: (0, 0)), _rows_spec(ts, d)],
        out_shape=[jax.ShapeDtypeStruct((1, 1), F32), jax.ShapeDtypeStruct((s, d), F32)],
        compiler_params=_params(("arbitrary",)),
    )(y, target)


def _rope_tables(seq):
    pos = jnp.arange(seq, dtype=F32)
    inv = ROPE_THETA ** (-jnp.arange(0, HEAD_DIM, 2, dtype=F32) / HEAD_DIM)
    ang = pos[:, None] * inv[None, :]
    cos, sin = jnp.cos(ang), jnp.sin(ang)
    cos2 = jnp.concatenate([cos, cos, cos, cos, jnp.ones((seq, 128), F32)], axis=1)
    sin2 = jnp.concatenate([-sin, sin, -sin, sin, jnp.zeros((seq, 128), F32)], axis=1)
    return cos2, sin2


def _rope(h, cos2, sin2, plain_block, out_dtype, name, width=128):
    s, n = h.shape
    ts = _tile(s, 512, 8)

    def body(h_ref, c_ref, s_ref, o_ref):
        cos, sin = c_ref[...], s_ref[...]
        lane = lax.broadcasted_iota(jnp.int32, cos.shape, 1)
        first_half = lane % HEAD_DIM < HEAD_DIM // 2
        for w in range(width // 128):
            cols = slice(w * 128, (w + 1) * 128)
            x = h_ref[:, cols].astype(F32)
            partner = jnp.where(first_half, pltpu.roll(x, 128 - HEAD_DIM // 2, 1), pltpu.roll(x, HEAD_DIM // 2, 1))
            o_ref[:, cols] = (x * cos + partner * sin).astype(o_ref.dtype)

    tile = pl.BlockSpec((ts, width), lambda i, j: (i, j))
    table = pl.BlockSpec((ts, 128), lambda i, j: (i, plain_block(j)))
    return pl.pallas_call(
        body,
        name=name,
        grid=(s // ts, n // width),
        in_specs=[tile, table, table],
        out_specs=tile,
        out_shape=jax.ShapeDtypeStruct((s, n), out_dtype),
        compiler_params=_params(("parallel", "parallel")),
    )(h, cos2, sin2)


SB_TQ = 512
SB_BLK = 128
SB_WALK = 2
SB_CLOSED = 110.0


def _sb_walk(n_trips, carry, key_blocks):
    def still_open(carry):
        lowest = functools.reduce(jnp.minimum, [c[0] for c in carry])
        return jnp.min(lowest) < SB_CLOSED

    def cond(state):
        t, go, _ = state
        return jnp.logical_and(t < n_trips, go)

    def body(state):
        t, _, carry = state
        carry = key_blocks(t, carry)
        return t + 1, still_open(carry), carry

    return lax.while_loop(cond, body, (jnp.int32(0), still_open(carry), carry))[2]


def _tri2(strict):
    row = lax.broadcasted_iota(jnp.int32, (2 * SB_BLK, SB_BLK), 0) % SB_BLK
    col = lax.broadcasted_iota(jnp.int32, (2 * SB_BLK, SB_BLK), 1)
    return (row > col if strict else row >= col).astype(MXU_DTYPE)


def _cumsum_dot(x, tri2):
    hi = x.astype(MXU_DTYPE)
    lo = x - hi.astype(F32)
    return _dot(jnp.concatenate([hi, lo.astype(MXU_DTYPE)], axis=1), tri2, NN)


def _sb_logits(z, valid):
    l1p = jnp.log(1.0 + jnp.exp(-jnp.abs(z)))
    sp = jnp.maximum(z, 0.0) + l1p
    ls = z - sp
    if valid is not None:
        sp = jnp.where(valid, sp, 0.0)
    return sp, ls


def _sb_weights(ls, after, valid):
    a = jnp.exp(ls - after)
    return a if valid is None else jnp.where(valid, a, 0.0)


def _sb_setup(q_ref, k_ref, v_ref, n_sub, scale):
    qs = [q_ref[0, u * SB_BLK:(u + 1) * SB_BLK, :] * scale for u in range(n_sub)]
    row = lax.broadcasted_iota(jnp.int32, (SB_BLK, SB_BLK), 0)
    col = lax.broadcasted_iota(jnp.int32, (SB_BLK, SB_BLK), 1)

    def load(jj):
        start = pl.multiple_of(jj * SB_BLK, SB_BLK)
        return start, k_ref[0, pl.ds(start, SB_BLK), :], v_ref[0, pl.ds(start, SB_BLK), :]

    return qs, col < row, load


def _sb_fwd(q, k, v, name):
    nh, s, dh = q.shape
    tq = min(SB_TQ, s)
    n_sub = tq // SB_BLK
    scale = dh ** -0.5

    def body(q_ref, k_ref, v_ref, o_ref, ob_ref):
        base = pl.program_id(1) * n_sub
        qs, diag_valid, load = _sb_setup(q_ref, k_ref, v_ref, n_sub, scale)
        tri_after = _tri2(True)

        def key_blocks(first, carry, diagonal):
            blocks = list(range((n_sub if diagonal else SB_WALK) - 1, -1, -1))
            kv = {d: load(first + d) for d in blocks}
            tiles = [(d, u) for d in blocks for u in range(d if diagonal else 0, n_sub)]
            valid = {t: diag_valid if diagonal and t[0] == t[1] else None for t in tiles}
            z = {t: _dot(qs[t[1]], kv[t[0]][1], NT) for t in tiles}
            sp_ls = {t: _sb_logits(z[t], valid[t]) for t in tiles}
            inside = {t: _cumsum_dot(sp_ls[t][0], tri_after) for t in tiles}
            carry = list(carry)
            for t in tiles:
                after_c, acc = carry[t[1]]
                sp, ls = sp_ls[t]
                a = _sb_weights(ls, after_c + inside[t], valid[t])
                carry[t[1]] = (after_c + jnp.sum(sp, axis=1, keepdims=True), acc + _dot(a, kv[t[0]][2], NN))
            return tuple(carry)

        carry = tuple((jnp.zeros((SB_BLK, 1), F32), jnp.zeros((SB_BLK, dh), F32)) for _ in range(n_sub))
        carry = key_blocks(base, carry, True)
        carry = _sb_walk(base // SB_WALK, carry, lambda t, c: key_blocks(base - SB_WALK * (t + 1), c, False))
        for u in range(n_sub):
            rows = slice(u * SB_BLK, (u + 1) * SB_BLK)
            o_ref[0, rows, :] = carry[u][1]
            ob_ref[0, rows, :] = carry[u][1].astype(ob_ref.dtype)

    blk = pl.BlockSpec((1, tq, dh), lambda h, i: (h, i, 0))
    full = pl.BlockSpec((1, s, dh), lambda h, i: (h, 0, 0))
    return pl.pallas_call(
        body,
        name=name,
        grid=(nh, s // tq),
        in_specs=[blk, full, full],
        out_specs=[blk, blk],
        out_shape=[jax.ShapeDtypeStruct((nh, s, dh), F32), jax.ShapeDtypeStruct((nh, s, dh), MXU_DTYPE)],
        compiler_params=_params(("parallel", "parallel")),
    )(q, k, v)


def _sb_bwd(q, k, v, o, do, name):
    nh, s, dh = q.shape
    tq = min(SB_TQ, s)
    n_sub = tq // SB_BLK
    scale = dh ** -0.5

    def body(q_ref, k_ref, v_ref, o_ref, do_ref, dq_ref, dk_ref, dv_ref):
        i = pl.program_id(1)
        base = i * n_sub

        @pl.when(i == 0)
        def _():
            dk_ref[...] = jnp.zeros_like(dk_ref)
            dv_ref[...] = jnp.zeros_like(dv_ref)

        qs, diag_valid, load = _sb_setup(q_ref, k_ref, v_ref, n_sub, scale)
        dob = [do_ref[0, u * SB_BLK:(u + 1) * SB_BLK, :] for u in range(n_sub)]
        total = [jnp.sum(dob[u].astype(F32) * o_ref[0, u * SB_BLK:(u + 1) * SB_BLK, :], axis=1, keepdims=True) for u in range(n_sub)]
        tri_after = _tri2(True)
        tri_from = _tri2(False)

        def key_blocks(first, carry, diagonal):
            blocks = list(range((n_sub if diagonal else SB_WALK) - 1, -1, -1))
            kv = {d: load(first + d) for d in blocks}
            tiles = [(d, u) for d in blocks for u in range(d if diagonal else 0, n_sub)]
            valid = {t: diag_valid if diagonal and t[0] == t[1] else None for t in tiles}
            z = {t: _dot(qs[t[1]], kv[t[0]][1], NT) for t in tiles}
            da = {t: _dot(dob[t[1]], kv[t[0]][2], NT) for t in tiles}
            sp_ls = {t: _sb_logits(z[t], valid[t]) for t in tiles}
            inside = {t: _cumsum_dot(sp_ls[t][0], tri_after) for t in tiles}
            after_run = [c[0] for c in carry]
            ab, dl = {}, {}
            for t in tiles:
                sp, ls = sp_ls[t]
                ab[t] = _sb_weights(ls, after_run[t[1]] + inside[t], valid[t]).astype(MXU_DTYPE)
                dl[t] = ab[t].astype(F32) * da[t]
                after_run[t[1]] = after_run[t[1]] + jnp.sum(sp, axis=1, keepdims=True)
            from_in = {t: _cumsum_dot(dl[t], tri_from) for t in tiles}
            from_run = [c[1] for c in carry]
            dq = [c[2] for c in carry]
            for d in blocks:
                start, kb, _ = kv[d]
                dk = dv = None
                for u in range(d if diagonal else 0, n_sub):
                    t = (d, u)
                    dz = dl[t] - jnp.exp(sp_ls[t][1]) * (dl[t] + total[u] - (from_run[u] + from_in[t]))
                    if valid[t] is not None:
                        dz = jnp.where(valid[t], dz, 0.0)
                    dzb = dz.astype(MXU_DTYPE)
                    from_run[u] = from_run[u] + jnp.sum(dl[t], axis=1, keepdims=True)
                    dq[u] = dq[u] + _dot(dzb, kb, NN)
                    dk_u, dv_u = _dot(dzb, qs[u], TN), _dot(ab[t], dob[u], TN)
                    dk = dk_u if dk is None else dk + dk_u
                    dv = dv_u if dv is None else dv + dv_u
                dk_ref[0, pl.ds(start, SB_BLK), :] += dk
                dv_ref[0, pl.ds(start, SB_BLK), :] += dv
            return tuple(zip(after_run, from_run, dq))

        carry = tuple((jnp.zeros((SB_BLK, 1), F32), jnp.zeros((SB_BLK, 1), F32), jnp.zeros((SB_BLK, dh), F32)) for _ in range(n_sub))
        carry = key_blocks(base, carry, True)
        carry = _sb_walk(base // SB_WALK, carry, lambda t, c: key_blocks(base - SB_WALK * (t + 1), c, False))
        for u in range(n_sub):
            dq_ref[0, u * SB_BLK:(u + 1) * SB_BLK, :] = carry[u][2] * scale

    blk = pl.BlockSpec((1, tq, dh), lambda h, i: (h, i, 0))
    full = pl.BlockSpec((1, s, dh), lambda h, i: (h, 0, 0))
    shape = jax.ShapeDtypeStruct((nh, s, dh), F32)
    return pl.pallas_call(
        body,
        name=name,
        grid=(nh, s // tq),
        in_specs=[blk, full, full, blk, blk],
        out_specs=[blk, full, full],
        out_shape=[shape, shape, shape],
        compiler_params=_params(("parallel", "arbitrary")),
    )(q, k, v, o, do)


BAND_TQ = 2048


def _band_scores(q_ref, k_ref, i, sub, tq, length, max_dist, scale):
    t0 = i * tq + sub * ATT_BLK
    ks = pl.multiple_of(jnp.minimum(jnp.maximum(t0 - ATT_BLK, 0), length - 2 * ATT_BLK), ATT_BLK)
    qs = q_ref[0, sub * ATT_BLK:(sub + 1) * ATT_BLK, :] * scale
    kw = k_ref[0, pl.ds(ks, 2 * ATT_BLK), :]
    sc = _dot(qs, kw, NT)
    diff = (t0 + lax.broadcasted_iota(jnp.int32, sc.shape, 0)) - (ks + lax.broadcasted_iota(jnp.int32, sc.shape, 1))
    valid = (diff >= 0) & (diff <= max_dist)
    return ks, qs, kw, jnp.where(valid, sc, NEG_INF)


def _band_fwd(q, k, v, sinks, max_dist, name):
    bq, length, dh = q.shape
    group = bq // k.shape[0]
    tq = min(BAND_TQ, length)
    scale = dh ** -0.5
    n_sink = 0 if sinks is None else sinks.shape[0]

    def body(*refs):
        if n_sink:
            sink_ref, q_ref, k_ref, v_ref, o_ref, lse_ref = refs
            sink = sink_ref[pl.program_id(0) % n_sink]
        else:
            q_ref, k_ref, v_ref, o_ref, lse_ref = refs
        i = pl.program_id(1)
        scores = [_band_scores(q_ref, k_ref, i, sub, tq, length, max_dist, scale) for sub in range(tq // ATT_BLK)]
        for sub, (ks, _, _, sc) in enumerate(scores):
            m = jnp.max(sc, axis=1, keepdims=True)
            if n_sink:
                m = jnp.maximum(m, sink)
            e = jnp.exp(sc - m)
            den = jnp.sum(e, axis=1, keepdims=True)
            if n_sink:
                den = den + jnp.exp(sink - m)
            rows = slice(sub * ATT_BLK, (sub + 1) * ATT_BLK)
            o_ref[0, rows, :] = _dot(e / den, v_ref[0, pl.ds(ks, 2 * ATT_BLK), :], NN)
            lse_ref[0, rows, :] = m + jnp.log(den)

    qblk = pl.BlockSpec((1, tq, dh), lambda b, i: (b, i, 0))
    kfull = pl.BlockSpec((1, length, dh), lambda b, i: (b // group, 0, 0))
    in_specs = [qblk, kfull, kfull]
    args = [q, k, v]
    if n_sink:
        in_specs = [pl.BlockSpec(memory_space=pltpu.SMEM)] + in_specs
        args = [sinks] + args
    return pl.pallas_call(
        body,
        name=name,
        grid=(bq, length // tq),
        in_specs=in_specs,
        out_specs=[qblk, pl.BlockSpec((1, tq, 1), lambda b, i: (b, i, 0))],
        out_shape=[jax.ShapeDtypeStruct((bq, length, dh), F32), jax.ShapeDtypeStruct((bq, length, 1), F32)],
        compiler_params=_params(("parallel", "parallel")),
    )(*args)


def _band_bwd(q, k, v, do, lse, delta, sinks, max_dist, name):
    bq, length, dh = q.shape
    group = bq // k.shape[0]
    tq = min(BAND_TQ, length)
    scale = dh ** -0.5
    n_sink = 0 if sinks is None else sinks.shape[0]

    def body(*refs):
        if n_sink:
            sink_ref, q_ref, k_ref, v_ref, do_ref, lse_ref, dl_ref, dq_ref, dk_ref, dv_ref, ds_ref = refs
            sink = sink_ref[pl.program_id(0) % n_sink]
        else:
            q_ref, k_ref, v_ref, do_ref, lse_ref, dl_ref, dq_ref, dk_ref, dv_ref, ds_ref = refs
        i = pl.program_id(1)

        @pl.when((i == 0) & (pl.program_id(0) % group == 0))
        def _():
            dk_ref[...] = jnp.zeros_like(dk_ref)
            dv_ref[...] = jnp.zeros_like(dv_ref)

        @pl.when(i == 0)
        def _():
            ds_ref[...] = jnp.zeros_like(ds_ref)

        scores = [_band_scores(q_ref, k_ref, i, sub, tq, length, max_dist, scale) for sub in range(tq // ATT_BLK)]
        dps = [_dot(do_ref[0, sub * ATT_BLK:(sub + 1) * ATT_BLK, :], v_ref[0, pl.ds(ks, 2 * ATT_BLK), :], NT) for sub, (ks, _, _, _) in enumerate(scores)]
        for sub, (ks, qs, kw, sc) in enumerate(scores):
            rows = slice(sub * ATT_BLK, (sub + 1) * ATT_BLK)
            lse = lse_ref[0, rows, :]
            delta_r = dl_ref[0, rows, :]
            dob = do_ref[0, rows, :]
            p = jnp.exp(sc - lse)
            dsb = (p * (dps[sub] - delta_r)).astype(MXU_DTYPE)
            dq_ref[0, rows, :] = _dot(dsb, kw, NN) * scale
            dk_ref[0, pl.ds(ks, 2 * ATT_BLK), :] += _dot(dsb, qs, TN)
            dv_ref[0, pl.ds(ks, 2 * ATT_BLK), :] += _dot(p, dob, TN)
            if n_sink:
                ds_ref[...] += jnp.sum(-jnp.exp(sink - lse) * delta_r, axis=0, keepdims=True)

    qblk = pl.BlockSpec((1, tq, dh), lambda b, i: (b, i, 0))
    qcol = pl.BlockSpec((1, tq, 1), lambda b, i: (b, i, 0))
    kfull = pl.BlockSpec((1, length, dh), lambda b, i: (b // group, 0, 0))
    in_specs = [qblk, kfull, kfull, qblk, qcol, qcol]
    args = [q, k, v, do, lse, delta]
    if n_sink:
        in_specs = [pl.BlockSpec(memory_space=pltpu.SMEM)] + in_specs
        args = [sinks] + args
    kshape = jax.ShapeDtypeStruct((k.shape[0], length, dh), F32)
    return pl.pallas_call(
        body,
        name=name,
        grid=(bq, length // tq),
        in_specs=in_specs,
        out_specs=[qblk, kfull, kfull, pl.BlockSpec((1, 8, 128), lambda b, i: (b, 0, 0))],
        out_shape=[jax.ShapeDtypeStruct((bq, length, dh), F32), kshape, kshape, jax.ShapeDtypeStruct((bq, 8, 128), F32)],
        compiler_params=_params(("arbitrary", "arbitrary")),
    )(*args)


def _merge_weights(lse_refs):
    lses = [r[0] for r in lse_refs]
    m = functools.reduce(jnp.maximum, lses)
    es = [jnp.exp(l - m) for l in lses]
    den = functools.reduce(lambda a, b: a + b, es)
    return [e / den for e in es]


def _merge_fwd(outs, lses, name):
    n = len(outs)
    nh, s, dh = outs[0].shape
    ts = _tile(s, 1024, 8)

    def body(*refs):
        ws = _merge_weights(refs[n:2 * n])
        o = functools.reduce(lambda a, b: a + b, [w * r[0] for w, r in zip(ws, refs[:n])])
        refs[2 * n][0] = o
        refs[2 * n + 1][0] = o.astype(MXU_DTYPE)

    blk = pl.BlockSpec((1, ts, dh), lambda h, i: (h, i, 0))
    col = pl.BlockSpec((1, ts, 1), lambda h, i: (h, i, 0))
    return pl.pallas_call(
        body,
        name=name,
        grid=(nh, s // ts),
        in_specs=[blk] * n + [col] * n,
        out_specs=[blk, blk],
        out_shape=[jax.ShapeDtypeStruct((nh, s, dh), F32), jax.ShapeDtypeStruct((nh, s, dh), MXU_DTYPE)],
        compiler_params=_params(("parallel", "parallel")),
    )(*outs, *lses)


def _merge_bwd(do, o, lses, name):
    n = len(lses)
    nh, s, dh = o.shape
    ts = _tile(s, 1024, 8)

    def body(*refs):
        do_ref, o_ref = refs[:2]
        ws = _merge_weights(refs[2:2 + n])
        dof = do_ref[0].astype(F32)
        base = jnp.sum(dof * o_ref[0], axis=1, keepdims=True)
        for g in range(n):
            refs[2 + n + g][0] = (ws[g] * dof).astype(MXU_DTYPE)
            refs[2 + 2 * n + g][0] = ws[g] * base

    blk = pl.BlockSpec((1, ts, dh), lambda h, i: (h, i, 0))
    col = pl.BlockSpec((1, ts, 1), lambda h, i: (h, i, 0))
    res = pl.pallas_call(
        body,
        name=name,
        grid=(nh, s // ts),
        in_specs=[blk, blk] + [col] * n,
        out_specs=[blk] * n + [col] * n,
        out_shape=[jax.ShapeDtypeStruct((nh, s, dh), MXU_DTYPE)] * n + [jax.ShapeDtypeStruct((nh, s, 1), F32)] * n,
        compiler_params=_params(("parallel", "parallel")),
    )(do, o, *lses)
    return res[:n], res[n:]


def _to_heads(a, dil=1):
    s, n = a.shape
    nh = n // HEAD_DIM
    a = a.reshape(s // dil, dil, nh, HEAD_DIM).transpose(1, 2, 0, 3)
    return a.reshape(dil * nh, s // dil, HEAD_DIM)


def _from_heads(a, dil=1):
    b, ls, c = a.shape
    nh = b // dil
    return a.reshape(dil, nh, ls, c).transpose(2, 0, 1, 3).reshape(ls * dil, nh * c)


def _restride(a, dil):
    nh, s, c = a.shape
    return a.reshape(nh, s // dil, dil, c).transpose(2, 0, 1, 3).reshape(dil * nh, s // dil, c)


def _unstride(a, dil):
    b, ls, c = a.shape
    nh = b // dil
    return a.reshape(dil, nh, ls, c).transpose(1, 2, 0, 3).reshape(nh, ls * dil, c)


ANY = pl.BlockSpec(memory_space=pl.ANY)


def _position():
    x, y, c = lax.axis_index("x"), lax.axis_index("y"), lax.axis_index("c")
    return x, y, c, [(1 - x, y), (x, 1 - y), (1 - x, 1 - y)]


def _remote(src, dst, send_sems, recv_sems, k, to):
    return pltpu.make_async_remote_copy(src_ref=src, dst_ref=dst, send_sem=send_sems.at[k], recv_sem=recv_sems.at[k], device_id=to, device_id_type=MESH)


def _gather_chips(arrs, name):
    n = len(arrs)

    def body(*refs):
        f_refs, g_refs = refs[:n], refs[n:2 * n]
        send_sems, recv_sems = refs[2 * n:]
        x, y, c, chips = _position()
        me, sibling = (x, y, c), (x, y, 1 - c)

        def half(j, ref, sel):
            rows = f_refs[j].shape[1] // 2
            return ref.at[:, pl.ds(sel * rows, rows), :]

        def slot(j, chip, sel):
            return half(j, g_refs[j].at[2 * chip[0] + chip[1]], sel)

        first = [_remote(half(j, f_refs[j], c), slot(j, (x, y), c), send_sems, recv_sems, 6 * j + k, (*chip, c)) for j in range(n) for k, chip in enumerate(chips)]
        for cp in first:
            cp.start()
        passed = []
        for j in range(n):
            for k, chip in enumerate(chips):
                _remote(slot(j, chip, c), slot(j, chip, c), send_sems, recv_sems, 6 * j + k, me).wait_recv()
                passed.append(_remote(slot(j, chip, c), slot(j, chip, c), send_sems, recv_sems, 6 * j + 3 + k, sibling))
                passed[-1].start()
        for j in range(n):
            for k, chip in enumerate(chips):
                _remote(slot(j, chip, 1 - c), slot(j, chip, 1 - c), send_sems, recv_sems, 6 * j + 3 + k, me).wait_recv()
        for cp in first + passed:
            cp.wait_send()

    return pl.pallas_call(
        body,
        name=name,
        in_specs=[ANY] * n,
        out_specs=[ANY] * n,
        out_shape=[jax.ShapeDtypeStruct((N_CHIPS,) + a.shape, a.dtype) for a in arrs],
        scratch_shapes=[pltpu.SemaphoreType.DMA((6 * n,)), pltpu.SemaphoreType.DMA((6 * n,))],
    )(*arrs)


def _swap_sibling(arrs, name):
    n = len(arrs)

    def body(*refs):
        x, y, c, _ = _position()
        send_sems, recv_sems = refs[2 * n:]
        copies = [_remote(refs[j], refs[n + j], send_sems, recv_sems, j, (x, y, 1 - c)) for j in range(n)]
        for cp in copies:
            cp.start()
        for cp in copies:
            cp.wait()

    return pl.pallas_call(
        body,
        name=name,
        in_specs=[ANY] * n,
        out_specs=[ANY] * n,
        out_shape=[jax.ShapeDtypeStruct(a.shape, a.dtype) for a in arrs],
        scratch_shapes=[pltpu.SemaphoreType.DMA((n,)), pltpu.SemaphoreType.DMA((n,))],
    )(*arrs)


def _chips_copies(p_refs, q_refs, send_sems, recv_sems):
    x, y, c, chips = _position()
    my_slot = 2 * x + y
    n = len(p_refs)
    sends = [_remote(p_refs[j].at[2 * px + py], q_refs[j].at[my_slot], send_sems, recv_sems, 3 * j + k, (px, py, c)) for j in range(n) for k, (px, py) in enumerate(chips)]
    arrivals = [_remote(q_refs[j].at[2 * px + py], q_refs[j].at[2 * px + py], send_sems, recv_sems, 3 * j + k, (x, y, c)) for j in range(n) for k, (px, py) in enumerate(chips)]
    return sends, arrivals


def _chips_finish(sends, arrivals):
    for cp in arrivals:
        cp.wait_recv()
    for cp in sends:
        cp.wait_send()


def _exchange_chips(parts, name):
    n = len(parts)

    def body(*refs):
        sends, arrivals = _chips_copies(refs[:n], refs[n:2 * n], *refs[2 * n:])
        for cp in sends:
            cp.start()
        _chips_finish(sends, arrivals)

    return pl.pallas_call(
        body,
        name=name,
        in_specs=[ANY] * n,
        out_specs=[ANY] * n,
        out_shape=[jax.ShapeDtypeStruct(a.shape, a.dtype) for a in parts],
        scratch_shapes=[pltpu.SemaphoreType.DMA((3 * n,)), pltpu.SemaphoreType.DMA((3 * n,))],
    )(*parts)


def _share_sibling(halves, name):
    n = len(halves)

    def body(*refs):
        h_refs, o_refs = refs[:n], refs[n:2 * n]
        send_sems, recv_sems = refs[2 * n:]
        x, y, c, _ = _position()

        def rows(j, sel):
            r = h_refs[j].shape[0]
            return o_refs[j].at[pl.ds(sel * r, r), :]

        sends = [_remote(h_refs[j], rows(j, c), send_sems, recv_sems, j, (x, y, 1 - c)) for j in range(n)]
        for cp in sends:
            cp.start()
        for j in range(n):
            _remote(rows(j, 1 - c), rows(j, 1 - c), send_sems, recv_sems, j, (x, y, c)).wait_recv()
        for cp in sends:
            cp.wait_send()

    return pl.pallas_call(
        body,
        name=name,
        in_specs=[ANY] * n,
        out_specs=[ANY] * n,
        out_shape=[jax.ShapeDtypeStruct((2 * a.shape[0],) + a.shape[1:], a.dtype) for a in halves],
        scratch_shapes=[pltpu.SemaphoreType.DMA((n,)), pltpu.SemaphoreType.DMA((n,))],
    )(*halves)


def _sum_rows(r, c, n_in):
    return _tile(r, max(16, (1 << 20) // (c * (n_in + 1))), 16)


def _add2(a, b, out_dtype, name):
    n, r, c = a.shape
    tr = _sum_rows(r, c, 2)

    def body(a_ref, b_ref, o_ref):
        o_ref[...] = (a_ref[...].astype(F32) + b_ref[...].astype(F32)).astype(o_ref.dtype)

    blk = pl.BlockSpec((1, tr, c), lambda s, i: (s, i, 0))
    return pl.pallas_call(
        body,
        name=name,
        grid=(n, r // tr),
        in_specs=[blk, blk],
        out_specs=blk,
        out_shape=jax.ShapeDtypeStruct(a.shape, out_dtype),
        compiler_params=_params(("parallel", "parallel")),
    )(a, b)


def _sum_slots(q, name):
    n, r, c = q.shape
    tr = _sum_rows(r, c, n)

    def body(q_ref, o_ref):
        acc = q_ref[0].astype(F32)
        for s in range(1, n):
            acc = acc + q_ref[s].astype(F32)
        o_ref[...] = acc

    return pl.pallas_call(
        body,
        name=name,
        grid=(r // tr,),
        in_specs=[pl.BlockSpec((n, tr, c), lambda i: (0, i, 0))],
        out_specs=pl.BlockSpec((tr, c), lambda i: (i, 0)),
        out_shape=jax.ShapeDtypeStruct((r, c), F32),
        compiler_params=_params(("parallel",)),
    )(q)


def _adamw(w, g, m, v, name):
    r, c = w.shape
    tr = _tile(r, 256, 8)
    c1 = 1.0 - ADAM_B1 ** ADAM_STEP
    c2 = 1.0 - ADAM_B2 ** ADAM_STEP

    def body(w_ref, g_ref, m_ref, v_ref, d_ref, nm_ref, nv_ref):
        g = g_ref[...]
        nm = ADAM_B1 * m_ref[...] + (1.0 - ADAM_B1) * g
        nv = ADAM_B2 * v_ref[...] + (1.0 - ADAM_B2) * (g * g)
        d_ref[...] = -ADAM_LR * ((nm / c1) / (jnp.sqrt(nv / c2) + ADAM_EPS) + ADAM_WD * w_ref[...])
        nm_ref[...] = nm
        nv_ref[...] = nv

    blk = pl.BlockSpec((tr, c), lambda i: (i, 0))
    shape = jax.ShapeDtypeStruct((r, c), F32)
    return pl.pallas_call(
        body,
        name=name,
        grid=(r // tr,),
        in_specs=[blk] * 4,
        out_specs=[blk] * 3,
        out_shape=[shape] * 3,
        compiler_params=_params(("parallel",)),
    )(w, g, m, v)


WEIGHTS = ("ffn1_w_in", "ffn1_w_out", "ffn2_w_in", "ffn2_w_out", "ln_g", "ln_b", "sb_w_in", "sb_w_out", "swa_w_in", "swa_sinks", "swa_w_out", "dil_w_in", "dil_w_out", "ple_w_proj", "ple_w_gate")
SHARD_AXIS = {"ffn1_w_in": 2, "ffn1_w_out": 1, "ffn2_w_in": 2, "ffn2_w_out": 1, "ln_g": 2, "ln_b": 2, "sb_w_in": 2, "sb_w_out": 1, "swa_w_in": 2, "swa_sinks": None, "swa_w_out": 1, "dil_w_in": 2, "dil_w_out": 1, "ple_w_proj": 2, "ple_w_gate": 1}
SMALL = ("ln_g", "ln_b", "swa_sinks")
SMALL_COLS = 128
SMALL_UNIT = 16 * SMALL_COLS


def _pack_small(pieces, lead):
    flat = jnp.concatenate([a.reshape(lead + (-1,)) for a in pieces], axis=-1)
    n = flat.shape[-1]
    flat = jnp.pad(flat, [(0, 0)] * len(lead) + [(0, -n % SMALL_UNIT)])
    return flat.reshape(lead + (-1, SMALL_COLS))


def _unpack_small(buf, shapes, lead):
    flat = buf.reshape(lead + (-1,))
    out, off = [], 0
    for shp in shapes:
        n = math.prod(shp)
        out.append(flat[..., off:off + n].reshape(lead + tuple(shp)))
        off += n
    return out


def _reduce_pair(grads, c, wire, tag):
    keep = [lax.dynamic_slice_in_dim(g, c * (g.shape[1] // 2), g.shape[1] // 2, axis=1) for g in grads]
    give = [lax.dynamic_slice_in_dim(g, (1 - c) * (g.shape[1] // 2), g.shape[1] // 2, axis=1).astype(wire) for g in grads]
    got = _swap_sibling(give, f"{tag}_pair")
    return [_add2(k, g, wire, f"{tag}_pair_sum{j}") for j, (k, g) in enumerate(zip(keep, got))]


def _reduce_finish(part, landed, c, tag):
    landed = [_fill_slot(q, lax.dynamic_slice_in_dim(p, _my_slot(), 1, axis=0)) for q, p in zip(landed, part)]
    halves = [_sum_slots(q, f"{tag}_chips_sum{j}") for j, q in enumerate(landed)]
    shared = _share_sibling(halves, f"{tag}_share")
    return [lax.dynamic_update_slice_in_dim(s, h, c * h.shape[0], axis=0) for s, h in zip(shared, halves)]


def _reduce_group(grads, c, wire, tag):
    part = _reduce_pair(grads, c, wire, tag)
    return _reduce_finish(part, _exchange_chips(part, f"{tag}_chips"), c, tag)


def _my_slot():
    return 2 * lax.axis_index("x") + lax.axis_index("y")


def _fill_slot(slots, mine):
    return lax.dynamic_update_slice_in_dim(slots, mine, _my_slot(), axis=0)


def _full_from_shards(g, axis):
    g = jnp.moveaxis(g, 0, axis)
    return g.reshape(g.shape[:axis] + (g.shape[axis] * g.shape[axis + 1],) + g.shape[axis + 2:])


def _shards_from_full(a, axis):
    a = a.reshape(a.shape[:axis] + (N_CHIPS, a.shape[axis] // N_CHIPS) + a.shape[axis + 1:])
    return jnp.moveaxis(a, axis, 0)


def _ffn_forward(x, xb, w_in, w_out, g, b, alpha, tag):
    gate, up, act = _ffn_in(xb, w_in, f"{tag}_in")
    out, outb, xhat, rstd = _mm_ln(act, w_out, x, g, b, alpha, 0.5, f"{tag}_out_ln")
    return (out, outb), dict(xb=xb, gate=gate, up=up, act=act, xhat=xhat, rstd=rstd)


EXCHANGE_HOSTS = ("dw_out", "dw_in", "dx")


def _exchange_host(name):
    if name in ("ffn1_w_in", "ffn2_w_in", "ffn1_w_out"):
        return "dw_in"
    return "dx" if name.endswith("_w_in") else "dw_out"


def _ffn_backward(dout, saved, w_in, w_out, g, tag, exchange=None):
    exchange = exchange or {}
    dr, dyb, dg, db = _ln_bwd(*dout, saved["xhat"], saved["rstd"], g, 0.5, f"{tag}_ln_bwd")
    dh = _ffn_dact(dyb, w_out, saved["gate"], saved["up"], f"{tag}_dact")
    out = {
        "dw_out": _mm(saved["act"], dyb, "tn", F32, f"{tag}_dw_out", exchange=exchange.get("dw_out") or None),
        "dw_in": _mm(saved["xb"], dh, "tn", F32, f"{tag}_dw_in", split_b=True, exchange=exchange.get("dw_in") or None),
        "dx": _mm(dh, w_in, "nt", F32, f"{tag}_dx", split_a=True, exchange=exchange.get("dx") or None),
    }
    landed = {h: out[h][1] for h in EXCHANGE_HOSTS if exchange.get(h)}
    res = {h: out[h][0] if exchange.get(h) else out[h] for h in EXCHANGE_HOSTS}
    return dr, res["dx"], res["dw_in"], res["dw_out"], dg, db, landed


def _sb_forward(xb, w_in, w_out, tag):
    nw = w_out.shape[0]
    h = _mm(xb, w_in, "nn", MXU_DTYPE, f"{tag}_qkv")
    q, k, v = (_to_heads(h[:, j * nw:(j + 1) * nw]) for j in range(3))
    o, ob = _sb_fwd(q, k, v, f"{tag}_att")
    ob = _from_heads(ob)
    return ob, dict(q=q, k=k, v=v, o=o, ob=ob)


def _sb_backward(dmix, saved, xb, w_in, w_out, tag):
    dw_out = _mm(saved["ob"], dmix, "tn", F32, f"{tag}_dw_out")
    do = _to_heads(_mm(dmix, w_out, "nt", MXU_DTYPE, f"{tag}_do"))
    dq, dk, dv = _sb_bwd(saved["q"], saved["k"], saved["v"], saved["o"], do, f"{tag}_att_bwd")
    dh = jnp.concatenate([_from_heads(t) for t in (dq, dk, dv)], axis=1).astype(MXU_DTYPE)
    dw_in = _mm(xb, dh, "tn", F32, f"{tag}_dw_in")
    return _mm(dh, w_in, "nt", F32, f"{tag}_dx"), dw_in, dw_out


def _swa_forward(xb, w_in, sinks, w_out, tables, tag):
    nq = w_out.shape[0]
    nkv = (w_in.shape[1] - nq) // 2
    h = _mm(xb, w_in, "nn", F32, f"{tag}_qkv")
    n_rot = (nq + nkv) // 128
    hb = _rope(h, *tables, lambda j: j // n_rot, MXU_DTYPE, f"{tag}_rope")
    q, k, v = _to_heads(hb[:, :nq]), _to_heads(hb[:, nq:nq + nkv]), _to_heads(hb[:, nq + nkv:])
    o, lse = _band_fwd(q, k, v, sinks, SWA_WINDOW - 1, f"{tag}_att")
    o, ob = _merge_fwd([o], [lse], f"{tag}_cast")
    ob = _from_heads(ob)
    return ob, dict(q=q, k=k, v=v, o=o, lse=lse, ob=ob, n_rot=n_rot)


def _swa_backward(dmix, saved, xb, w_in, sinks, w_out, tables, tag):
    dw_out = _mm(saved["ob"], dmix, "tn", F32, f"{tag}_dw_out")
    do = _to_heads(_mm(dmix, w_out, "nt", MXU_DTYPE, f"{tag}_do"))
    (dog,), (delta,) = _merge_bwd(do, saved["o"], [saved["lse"]], f"{tag}_delta")
    dq, dk, dv, dsink = _band_bwd(saved["q"], saved["k"], saved["v"], dog, saved["lse"], delta, sinks, SWA_WINDOW - 1, f"{tag}_att_bwd")
    dh = jnp.concatenate([_from_heads(t) for t in (dq, dk, dv)], axis=1)
    n_rot = saved["n_rot"]
    dhb = _rope(dh, tables[0], -tables[1], lambda j: j // n_rot, MXU_DTYPE, f"{tag}_rope_bwd")
    dw_in = _mm(xb, dhb, "tn", F32, f"{tag}_dw_in")
    return _mm(dhb, w_in, "nt", F32, f"{tag}_dx"), dw_in, dw_out, dsink[:, 0, 0]


def _dil_forward(xb, w_in, w_out, tables, tag):
    nw = w_out.shape[0]
    h = _mm(xb, w_in, "nn", F32, f"{tag}_qkv")
    hb = _rope(h, *tables, lambda j: (j % 3) // 2, MXU_DTYPE, f"{tag}_rope", width=nw)
    qkv, outs, lses = [], [], []
    for gi, (win, dil) in enumerate(DIL_GROUPS):
        base = gi * 3 * nw
        q, k, v = (_to_heads(hb[:, base + j * nw:base + (j + 1) * nw], dil) for j in range(3))
        o, lse = _band_fwd(q, k, v, None, win // dil, f"{tag}_att{gi}")
        qkv.append((q, k, v))
        outs.append(_unstride(o, dil))
        lses.append(_unstride(lse, dil))
    o, ob = _merge_fwd(outs, lses, f"{tag}_merge")
    ob = _from_heads(ob)
    return ob, dict(qkv=qkv, o=o, lses=lses, ob=ob)


def _dil_backward(dmix, saved, xb, w_in, w_out, tables, tag):
    dw_out = _mm(saved["ob"], dmix, "tn", F32, f"{tag}_dw_out")
    do = _to_heads(_mm(dmix, w_out, "nt", MXU_DTYPE, f"{tag}_do"))
    dogs, deltas = _merge_bwd(do, saved["o"], saved["lses"], f"{tag}_merge_bwd")
    parts = []
    for gi, (win, dil) in enumerate(DIL_GROUPS):
        q, k, v = saved["qkv"][gi]
        dq, dk, dv, _ = _band_bwd(q, k, v, _restride(dogs[gi], dil), _restride(saved["lses"][gi], dil), _restride(deltas[gi], dil), None, win // dil, f"{tag}_att{gi}_bwd")
        parts += [_from_heads(t, dil) for t in (dq, dk, dv)]
    dh = jnp.concatenate(parts, axis=1)
    dhb = _rope(dh, tables[0], -tables[1], lambda j: (j % 3) // 2, MXU_DTYPE, f"{tag}_rope_bwd", width=w_out.shape[0])
    dw_in = _mm(xb, dhb, "tn", F32, f"{tag}_dw_in")
    return _mm(dhb, w_in, "nt", F32, f"{tag}_dx"), dw_in, dw_out


def kernel(x, p, ffn1_w_in, ffn1_w_out, ffn2_w_in, ffn2_w_out, ln_g, ln_b, sb_w_in, sb_w_out, swa_w_in, swa_sinks, swa_w_out, dil_w_in, dil_w_out, ple_w_proj, ple_w_gate, loss_target, m_ffn1_w_in, m_ffn1_w_out, m_ffn2_w_in, m_ffn2_w_out, m_ln_g, m_ln_b, m_sb_w_in, m_sb_w_out, m_swa_w_in, m_swa_sinks, m_swa_w_out, m_dil_w_in, m_dil_w_out, m_ple_w_proj, m_ple_w_gate, v_ffn1_w_in, v_ffn1_w_out, v_ffn2_w_in, v_ffn2_w_out, v_ln_g, v_ln_b, v_sb_w_in, v_sb_w_out, v_swa_w_in, v_swa_sinks, v_swa_w_out, v_dil_w_in, v_dil_w_out, v_ple_w_proj, v_ple_w_gate):
    shard = dict(ffn1_w_in=ffn1_w_in, ffn1_w_out=ffn1_w_out, ffn2_w_in=ffn2_w_in, ffn2_w_out=ffn2_w_out, ln_g=ln_g, ln_b=ln_b, sb_w_in=sb_w_in, sb_w_out=sb_w_out, swa_w_in=swa_w_in, swa_sinks=swa_sinks, swa_w_out=swa_w_out, dil_w_in=dil_w_in, dil_w_out=dil_w_out, ple_w_proj=ple_w_proj, ple_w_gate=ple_w_gate)
    mom_m = dict(ffn1_w_in=m_ffn1_w_in, ffn1_w_out=m_ffn1_w_out, ffn2_w_in=m_ffn2_w_in, ffn2_w_out=m_ffn2_w_out, ln_g=m_ln_g, ln_b=m_ln_b, sb_w_in=m_sb_w_in, sb_w_out=m_sb_w_out, swa_w_in=m_swa_w_in, swa_sinks=m_swa_sinks, swa_w_out=m_swa_w_out, dil_w_in=m_dil_w_in, dil_w_out=m_dil_w_out, ple_w_proj=m_ple_w_proj, ple_w_gate=m_ple_w_gate)
    mom_v = dict(ffn1_w_in=v_ffn1_w_in, ffn1_w_out=v_ffn1_w_out, ffn2_w_in=v_ffn2_w_in, ffn2_w_out=v_ffn2_w_out, ln_g=v_ln_g, ln_b=v_ln_b, sb_w_in=v_sb_w_in, sb_w_out=v_sb_w_out, swa_w_in=v_swa_w_in, swa_sinks=v_swa_sinks, swa_w_out=v_swa_w_out, dil_w_in=v_dil_w_in, dil_w_out=v_dil_w_out, ple_w_proj=v_ple_w_proj, ple_w_gate=v_ple_w_gate)
    depth = ffn1_w_in.shape[0]
    alpha = (2 * depth) ** 0.25
    c = lax.axis_index("c")

    mats = [n for n in WEIGHTS if n not in SMALL]
    local = [shard[n].astype(MXU_DTYPE) for n in mats] + [_pack_small([ln_g, ln_b], ())[None]]
    got = [_fill_slot(g, a[None]) for g, a in zip(_gather_chips(local, "gather_weights"), local)]
    full = {"swa_sinks": swa_sinks}
    for n, g in zip(mats, got):
        full[n] = _full_from_shards(g, SHARD_AXIS[n])
    for n, g in zip(("ln_g", "ln_b"), _unpack_small(got[-1][:, 0], [ln_g.shape, ln_b.shape], (N_CHIPS,))):
        full[n] = _full_from_shards(g, SHARD_AXIS[n])

    seq = x.shape[1]
    tables = _rope_tables(seq)
    xf = x[0]
    xb = xf.astype(MXU_DTYPE)
    saved = []
    for i in range(depth):
        kind, j = i % 3, i // 3
        mixer = ("sb", "swa", "dil")[kind]
        sv = {}
        (x1, x1b), sv["ffn1"] = _ffn_forward(xf, xb, full["ffn1_w_in"][i], full["ffn1_w_out"][i], full["ln_g"][i, 0], full["ln_b"][i, 0], alpha, f"l{i}_ffn1")
        if kind == 0:
            mix, sv["mix"] = _sb_forward(x1b, full["sb_w_in"][j], full["sb_w_out"][j], f"l{i}_sb")
        elif kind == 1:
            mix, sv["mix"] = _swa_forward(x1b, full["swa_w_in"][j], swa_sinks[j], full["swa_w_out"][j], tables, f"l{i}_swa")
        else:
            mix, sv["mix"] = _dil_forward(x1b, full["dil_w_in"][j], full["dil_w_out"][j], tables, f"l{i}_dil")
        x2, x2b, sv["xhat2"], sv["rstd2"] = _mm_ln(mix, full[f"{mixer}_w_out"][j], x1, full["ln_g"][i, 1], full["ln_b"][i, 1], alpha, 1.0, f"l{i}_{mixer}_proj_ln")
        sv["x1b"] = x1b
        (x3, x3b), sv["ffn2"] = _ffn_forward(x2, x2b, full["ffn2_w_in"][i], full["ffn2_w_out"][i], full["ln_g"][i, 2], full["ln_b"][i, 2], alpha, f"l{i}_ffn2")
        xf, xb, sv["u"], sv["e"] = _ple_fwd(x3, x3b, p[i, 0], full["ple_w_gate"][i], full["ple_w_proj"][i], f"l{i}_ple")
        sv["x3b"] = x3b
        saved.append(sv)

    loss_part, dy = _loss(xf, loss_target[0], "loss")
    loss = lax.psum(loss_part[0, 0], ("x", "y", "c"))

    grads = {n: [None] * full[n].shape[0] for n in WEIGHTS if n not in ("ln_g", "ln_b")}
    gsum = {n: [None] * full[n].shape[0] for n in mats}
    dln_g = [[None] * 3 for _ in range(depth)]
    dln_b = [[None] * 3 for _ in range(depth)]
    dout = (dy, None, 1.0)
    pending = None

    def finish_reduce(layer, part, tag, landed):
        for (n, k), g in zip(layer, _reduce_finish(part, landed, c, tag)):
            gsum[n][k] = g

    for i in reversed(range(depth)):
        kind, j = i % 3, i // 3
        mixer = ("sb", "swa", "dil")[kind]
        sv = saved[i]
        dx4, dub, deb = _ple_bwd(*dout, sv["u"], sv["e"], f"l{i}_ple_bwd")
        grads["ple_w_gate"][i] = _mm(sv["x3b"], dub, "tn", F32, f"l{i}_ple_dw_gate")
        grads["ple_w_proj"][i] = _mm(p[i, 0], deb, "tn", F32, f"l{i}_ple_dw_proj")
        dxb = _mm(dub, full["ple_w_gate"][i], "nt", F32, f"l{i}_ple_dx")
        dr, dxb, grads["ffn2_w_in"][i], grads["ffn2_w_out"][i], dln_g[i][2], dln_b[i][2], landed = _ffn_backward((dx4, dxb, 1.0), sv["ffn2"], full["ffn2_w_in"][i], full["ffn2_w_out"][i], full["ln_g"][i, 2], f"l{i}_ffn2", exchange=pending and {h: [q for (n, _), q in zip(pending[0], pending[1]) if _exchange_host(n) == h] for h in EXCHANGE_HOSTS})
        if pending:
            arrived = {h: iter(v) for h, v in landed.items()}
            finish_reduce(*pending, [next(arrived[_exchange_host(n)]) for n, _ in pending[0]])
        dr, dmix, dln_g[i][1], dln_b[i][1] = _ln_bwd(dr, dxb, alpha, sv["xhat2"], sv["rstd2"], full["ln_g"][i, 1], 1.0, f"l{i}_mix_ln_bwd")
        if kind == 0:
            dxb, grads["sb_w_in"][j], grads["sb_w_out"][j] = _sb_backward(dmix, sv["mix"], sv["x1b"], full["sb_w_in"][j], full["sb_w_out"][j], f"l{i}_sb")
        elif kind == 1:
            dxb, grads["swa_w_in"][j], grads["swa_w_out"][j], grads["swa_sinks"][j] = _swa_backward(dmix, sv["mix"], sv["x1b"], full["swa_w_in"][j], swa_sinks[j], full["swa_w_out"][j], tables, f"l{i}_swa")
        else:
            dxb, grads["dil_w_in"][j], grads["dil_w_out"][j] = _dil_backward(dmix, sv["mix"], sv["x1b"], full["dil_w_in"][j], full["dil_w_out"][j], tables, f"l{i}_dil")
        dr, dxb, grads["ffn1_w_in"][i], grads["ffn1_w_out"][i], dln_g[i][0], dln_b[i][0], _ = _ffn_backward((dr, dxb, alpha), sv["ffn1"], full["ffn1_w_in"][i], full["ffn1_w_out"][i], full["ln_g"][i, 0], f"l{i}_ffn1")
        dout = (dr, dxb, alpha)
        layer = [("ffn1_w_in", i), ("ffn1_w_out", i), (f"{mixer}_w_in", j), (f"{mixer}_w_out", j), ("ffn2_w_in", i), ("ffn2_w_out", i), ("ple_w_proj", i), ("ple_w_gate", i)]
        pending = (layer, _reduce_pair([_shards_from_full(grads[n][k], SHARD_AXIS[n] - 1) for n, k in layer], c, MXU_DTYPE, f"l{i}_reduce"), f"l{i}_reduce")
    layer, part, tag = pending
    finish_reduce(layer, part, tag, _exchange_chips(part, f"{tag}_chips"))
    grad_x = _axpy(*dout, "grad_x")[None]

    gshard = {n: jnp.stack(g) for n, g in gsum.items()}
    small = [
        _shards_from_full(jnp.stack([jnp.concatenate(r, axis=0) for r in dln_g]), SHARD_AXIS["ln_g"]),
        _shards_from_full(jnp.stack([jnp.concatenate(r, axis=0) for r in dln_b]), SHARD_AXIS["ln_b"]),
        jnp.broadcast_to(jnp.stack(grads["swa_sinks"])[None], (N_CHIPS,) + swa_sinks.shape),
    ]
    (small_sum,) = _reduce_group([_pack_small(small, (N_CHIPS,))], c, F32, "small_reduce")
    for n, g in zip(SMALL, _unpack_small(small_sum, [shard[n].shape for n in SMALL], ())):
        gshard[n] = g

    delta, new_m, new_v = {}, {}, {}
    for n in WEIGHTS:
        shp = shard[n].shape
        two_d = (-1, shp[-1])
        d, nm, nv = _adamw(shard[n].reshape(two_d), gshard[n].reshape(two_d), mom_m[n].reshape(two_d), mom_v[n].reshape(two_d), f"adamw_{n}")
        delta[n], new_m[n], new_v[n] = d.reshape(shp), nm.reshape(shp), nv.reshape(shp)

    return (loss, grad_x, *[gshard[n] for n in WEIGHTS], *[delta[n] for n in WEIGHTS], *[new_m[n] for n in WEIGHTS], *[new_v[n] for n in WEIGHTS])
```

```python
import functools
import math

import jax
import jax.numpy as jnp
from jax import lax
from jax.experimental import pallas as pl
from jax.experimental.pallas import tpu as pltpu

F32 = jnp.float32
MXU_DTYPE = jnp.bfloat16
MESH = pl.DeviceIdType.MESH

HEAD_DIM = 64
ATT_BLK = 128
SWA_WINDOW = 128
DIL_GROUPS = ((128, 1), (512, 4), (2048, 16))
LN_EPS = 1e-5
ROPE_THETA = 10000.0
NEG_INF = -1e30
ADAM_LR, ADAM_B1, ADAM_B2, ADAM_EPS, ADAM_WD, ADAM_STEP = 0.001, 0.9, 0.999, 1e-08, 0.01, 10

VMEM_LIMIT_BYTES = 56 * 1024 * 1024
N_CHIPS = 4


def _params(sem=None):
    return pltpu.CompilerParams(dimension_semantics=sem, vmem_limit_bytes=VMEM_LIMIT_BYTES)


def _tile(n, target, unit):
    t = (min(target, n) // unit) * unit
    while t >= unit:
        if n % t == 0:
            return t
        t -= unit
    return n


def _dot(a, b, dims):
    return lax.dot_general(a.astype(MXU_DTYPE), b.astype(MXU_DTYPE), (dims, ((), ())), preferred_element_type=F32)


NN = ((1,), (0,))
NT = ((1,), (1,))
TN = ((0,), (0,))


def _accumulate(part, acc_ref, kk, nk, finish):
    if nk == 1:
        finish(part)
        return

    @pl.when(kk == 0)
    def _():
        acc_ref[...] = jnp.zeros_like(acc_ref)

    acc_ref[...] += part

    @pl.when(kk == nk - 1)
    def _():
        finish(acc_ref[...])


def _mm(a, b, mode, out_dtype, name, split_a=False, split_b=False, exchange=None, tm=1408, tn=1408, tk=2048):
    dims = {"nn": NN, "nt": NT, "tn": TN}[mode]
    if split_a:
        m, k = a.shape[1], 2 * a.shape[2]
    elif mode == "tn":
        k, m = a.shape
    else:
        m, k = a.shape
    if split_b:
        n = 2 * b.shape[2]
    elif mode == "nt":
        n = b.shape[0]
    else:
        n = b.shape[1]
    tm = _tile(m, tm, 128)
    tn = _tile(n // 2 if split_b else n, tn, 128)
    tk = _tile(k // 2 if split_a else k, tk, 128)
    nk = k // tk
    nk_half = nk // 2
    nn_half = (n // tn) // 2

    if split_a:
        a_spec = pl.BlockSpec((None, tm, tk), lambda i, j, kk: (kk // nk_half, i, kk % nk_half))
    elif mode == "tn":
        a_spec = pl.BlockSpec((tk, tm), lambda i, j, kk: (kk, i))
    else:
        a_spec = pl.BlockSpec((tm, tk), lambda i, j, kk: (i, kk))
    if split_b:
        b_spec = pl.BlockSpec((None, tk, tn), lambda i, j, kk: (j // nn_half, kk, j % nn_half))
    elif mode == "nt":
        b_spec = pl.BlockSpec((tn, tk), lambda i, j, kk: (j, kk))
    else:
        b_spec = pl.BlockSpec((tk, tn), lambda i, j, kk: (kk, j))

    parts = [] if exchange is None else list(exchange)
    n_ex = len(parts)
    grid = (m // tm, n // tn, nk)

    def body(*refs):
        a_ref, b_ref = refs[:2]
        p_refs, o_ref, q_refs, acc_ref = refs[2:2 + n_ex], refs[2 + n_ex], refs[3 + n_ex:3 + 2 * n_ex], refs[3 + 2 * n_ex]
        if n_ex:
            step = (pl.program_id(0) * grid[1] + pl.program_id(1)) * nk + pl.program_id(2)

            @pl.when(step == 0)
            def _():
                for cp in _chips_copies(p_refs, q_refs, *refs[4 + 2 * n_ex:])[0]:
                    cp.start()

        def finish(total):
            o_ref[...] = total.astype(o_ref.dtype)

        _accumulate(_dot(a_ref[...], b_ref[...], dims), acc_ref, pl.program_id(2), nk, finish)

        if n_ex:

            @pl.when(step == grid[0] * grid[1] * nk - 1)
            def _():
                _chips_finish(*_chips_copies(p_refs, q_refs, *refs[4 + 2 * n_ex:]))

    out = pl.pallas_call(
        body,
        name=name,
        grid=grid,
        in_specs=[a_spec, b_spec] + [ANY] * n_ex,
        out_specs=[pl.BlockSpec((tm, tn), lambda i, j, kk: (i, j))] + [ANY] * n_ex,
        out_shape=[jax.ShapeDtypeStruct((m, n), out_dtype)] + [jax.ShapeDtypeStruct(p.shape, p.dtype) for p in parts],
        scratch_shapes=[pltpu.VMEM((tm, tn), F32)] + [pltpu.SemaphoreType.DMA((3 * n_ex,))] * (2 if n_ex else 0),
        compiler_params=_params(("arbitrary", "arbitrary", "arbitrary") if n_ex else ("parallel", "parallel", "arbitrary")),
    )(a, b, *parts)
    return (out[0], out[1:]) if n_ex else out[0]


def _sigmoid(x):
    return 1.0 / (1.0 + jnp.exp(-x))


def _ffn_in(xb, w_in, name):
    s, d = xb.shape
    f = w_in.shape[1] // 2
    tm = _tile(s, 512, 128)
    tn = _tile(f, 1408, 128)
    nj = f // tn

    def body(x_ref, wg_ref, wu_ref, g_ref, u_ref, a_ref):
        x = x_ref[...]
        g = _dot(x, wg_ref[...], NN)
        u = _dot(x, wu_ref[...], NN)
        g_ref[...] = g.astype(g_ref.dtype)
        u_ref[...] = u.astype(u_ref.dtype)
        a_ref[...] = (g * _sigmoid(g) * u).astype(a_ref.dtype)

    out = pl.BlockSpec((tm, tn), lambda i, j: (i, j))
    return pl.pallas_call(
        body,
        name=name,
        grid=(s // tm, nj),
        in_specs=[
            pl.BlockSpec((tm, d), lambda i, j: (i, 0)),
            pl.BlockSpec((d, tn), lambda i, j: (0, j)),
            pl.BlockSpec((d, tn), lambda i, j: (0, j + nj)),
        ],
        out_specs=[out, out, out],
        out_shape=[
            jax.ShapeDtypeStruct((s, f), MXU_DTYPE),
            jax.ShapeDtypeStruct((s, f), MXU_DTYPE),
            jax.ShapeDtypeStruct((s, f), MXU_DTYPE),
        ],
        compiler_params=_params(("parallel", "parallel")),
    )(xb, w_in, w_in)


def _ffn_dact(dyb, w_out, gate, up, name):
    s, d = dyb.shape
    f = w_out.shape[0]
    tm = _tile(s, 512, 128)
    tn = _tile(f, 1408, 128)

    def body(dy_ref, w_ref, g_ref, u_ref, o_ref):
        dact = _dot(dy_ref[...], w_ref[...], NT)
        g = g_ref[...].astype(F32)
        sig = _sigmoid(g)
        o_ref[0] = (dact * u_ref[...].astype(F32) * (sig * (1.0 + g * (1.0 - sig)))).astype(o_ref.dtype)
        o_ref[1] = (dact * (g * sig)).astype(o_ref.dtype)

    tile = pl.BlockSpec((tm, tn), lambda i, j: (i, j))
    return pl.pallas_call(
        body,
        name=name,
        grid=(s // tm, f // tn),
        in_specs=[
            pl.BlockSpec((tm, d), lambda i, j: (i, 0)),
            pl.BlockSpec((tn, d), lambda i, j: (j, 0)),
            tile,
            tile,
        ],
        out_specs=pl.BlockSpec((2, tm, tn), lambda i, j: (0, i, j)),
        out_shape=jax.ShapeDtypeStruct((2, s, f), MXU_DTYPE),
        compiler_params=_params(("parallel", "parallel")),
    )(dyb, w_out, gate, up)


def _ple_fwd(x, xb, p, w_gate, w_proj, name):
    s, d = x.shape
    pd = p.shape[1]
    tm = _tile(s, 1024, 128)
    tn = _tile(d, 512, 128)

    def body(x_ref, xb_ref, p_ref, wg_ref, wp_ref, o_ref, ob_ref, u_ref, e_ref):
        u = _dot(xb_ref[...], wg_ref[...], NN)
        e = _dot(p_ref[...], wp_ref[...], NN)
        out = x_ref[...] + _sigmoid(u) * e
        o_ref[...] = out
        ob_ref[...] = out.astype(ob_ref.dtype)
        u_ref[...] = u
        e_ref[...] = e

    tile = pl.BlockSpec((tm, tn), lambda i, j: (i, j))
    return pl.pallas_call(
        body,
        name=name,
        grid=(s // tm, d // tn),
        in_specs=[
            tile,
            pl.BlockSpec((tm, d), lambda i, j: (i, 0)),
            pl.BlockSpec((tm, pd), lambda i, j: (i, 0)),
            pl.BlockSpec((d, tn), lambda i, j: (0, j)),
            pl.BlockSpec((pd, tn), lambda i, j: (0, j)),
        ],
        out_specs=[tile, tile, tile, tile],
        out_shape=[
            jax.ShapeDtypeStruct((s, d), F32),
            jax.ShapeDtypeStruct((s, d), MXU_DTYPE),
            jax.ShapeDtypeStruct((s, d), F32),
            jax.ShapeDtypeStruct((s, d), F32),
        ],
        compiler_params=_params(("parallel", "parallel")),
    )(x, xb, p, w_gate, w_proj)


def _rows_spec(ts, d):
    return pl.BlockSpec((ts, d), lambda i: (i, 0))


def _mm_ln(a, w, x, g, b, alpha, beta, name):
    s, k = a.shape
    d = w.shape[1]
    tm = _tile(s, 512, 128)
    tk = _tile(k, 2816, 128)
    nk = k // tk

    def body(a_ref, w_ref, x_ref, g_ref, b_ref, o_ref, ob_ref, xh_ref, rs_ref, acc_ref):
        def finish(y):
            r = alpha * x_ref[...] + beta * y
            mu = jnp.mean(r, axis=1, keepdims=True)
            cen = r - mu
            var = jnp.mean(cen * cen, axis=1, keepdims=True)
            rstd = lax.rsqrt(var + LN_EPS)
            xhat = cen * rstd
            out = xhat * g_ref[...] + b_ref[...]
            o_ref[...] = out
            ob_ref[...] = out.astype(ob_ref.dtype)
            xh_ref[...] = xhat
            rs_ref[...] = rstd

        _accumulate(_dot(a_ref[...], w_ref[...], NN), acc_ref, pl.program_id(1), nk, finish)

    rows = pl.BlockSpec((tm, d), lambda i, kk: (i, 0))
    vec = pl.BlockSpec((1, d), lambda i, kk: (0, 0))
    return pl.pallas_call(
        body,
        name=name,
        grid=(s // tm, nk),
        in_specs=[pl.BlockSpec((tm, tk), lambda i, kk: (i, kk)), pl.BlockSpec((tk, d), lambda i, kk: (kk, 0)), rows, vec, vec],
        out_specs=[rows, rows, rows, pl.BlockSpec((tm, 1), lambda i, kk: (i, 0))],
        out_shape=[
            jax.ShapeDtypeStruct((s, d), F32),
            jax.ShapeDtypeStruct((s, d), MXU_DTYPE),
            jax.ShapeDtypeStruct((s, d), F32),
            jax.ShapeDtypeStruct((s, 1), F32),
        ],
        scratch_shapes=[pltpu.VMEM((tm, d), F32)],
        compiler_params=_params(("parallel", "arbitrary")),
    )(a, w, x, g.reshape(1, d), b.reshape(1, d))


def _ln_bwd(ga, gb, ca, xhat, rstd, g, beta, name):
    s, d = xhat.shape
    ts = _tile(s, 512, 8)

    def body(ga_ref, gb_ref, xh_ref, rs_ref, g_ref, dr_ref, dyb_ref, dg_ref, db_ref):
        @pl.when(pl.program_id(0) == 0)
        def _():
            dg_ref[...] = jnp.zeros_like(dg_ref)
            db_ref[...] = jnp.zeros_like(db_ref)

        dout = ca * ga_ref[...] + gb_ref[...]
        xhat = xh_ref[...]
        dg_ref[...] += jnp.sum(dout * xhat, axis=0, keepdims=True)
        db_ref[...] += jnp.sum(dout, axis=0, keepdims=True)
        dxh = dout * g_ref[...]
        m1 = jnp.mean(dxh, axis=1, keepdims=True)
        m2 = jnp.mean(dxh * xhat, axis=1, keepdims=True)
        dr = rs_ref[...] * (dxh - m1 - xhat * m2)
        dr_ref[...] = dr
        dyb_ref[...] = (beta * dr).astype(dyb_ref.dtype)

    vec = pl.BlockSpec((1, d), lambda i: (0, 0))
    return pl.pallas_call(
        body,
        name=name,
        grid=(s // ts,),
        in_specs=[_rows_spec(ts, d), _rows_spec(ts, d), _rows_spec(ts, d), _rows_spec(ts, 1), vec],
        out_specs=[_rows_spec(ts, d), _rows_spec(ts, d), vec, vec],
        out_shape=[
            jax.ShapeDtypeStruct((s, d), F32),
            jax.ShapeDtypeStruct((s, d), MXU_DTYPE),
            jax.ShapeDtypeStruct((1, d), F32),
            jax.ShapeDtypeStruct((1, d), F32),
        ],
        compiler_params=_params(("arbitrary",)),
    )(ga, gb, xhat, rstd, g.reshape(1, d))


def _ple_bwd(ga, gb, ca, u, e, name):
    s, d = u.shape
    ts = _tile(s, 512, 8)
    grads = [ga] if gb is None else [ga, gb]

    def body(*refs):
        u_ref, e_ref, dx_ref, du_ref, de_ref = refs[len(grads):]
        dx = ca * refs[0][...]
        if gb is not None:
            dx = dx + refs[1][...]
        sig = _sigmoid(u_ref[...])
        dx_ref[...] = dx
        du_ref[...] = (dx * e_ref[...] * sig * (1.0 - sig)).astype(du_ref.dtype)
        de_ref[...] = (dx * sig).astype(de_ref.dtype)

    return pl.pallas_call(
        body,
        name=name,
        grid=(s // ts,),
        in_specs=[_rows_spec(ts, d)] * (len(grads) + 2),
        out_specs=[_rows_spec(ts, d)] * 3,
        out_shape=[
            jax.ShapeDtypeStruct((s, d), F32),
            jax.ShapeDtypeStruct((s, d), MXU_DTYPE),
            jax.ShapeDtypeStruct((s, d), MXU_DTYPE),
        ],
        compiler_params=_params(("parallel",)),
    )(*grads, u, e)


def _axpy(ga, gb, ca, name):
    s, d = ga.shape
    ts = _tile(s, 512, 8)

    def body(ga_ref, gb_ref, o_ref):
        o_ref[...] = ca * ga_ref[...] + gb_ref[...]

    return pl.pallas_call(
        body,
        name=name,
        grid=(s // ts,),
        in_specs=[_rows_spec(ts, d)] * 2,
        out_specs=_rows_spec(ts, d),
        out_shape=jax.ShapeDtypeStruct((s, d), F32),
        compiler_params=_params(("parallel",)),
    )(ga, gb)


def _loss(y, target, name):
    s, d = y.shape
    ts = _tile(s, 512, 8)

    def body(y_ref, t_ref, l_ref, dy_ref):
        @pl.when(pl.program_id(0) == 0)
        def _():
            l_ref[...] = jnp.zeros_like(l_ref)

        err = y_ref[...] - t_ref[...]
        l_ref[...] += (0.5 / d) * jnp.sum(jnp.sum(err * err, axis=1, keepdims=True), axis=0, keepdims=True)
        dy_ref[...] = err * (1.0 / d)

    return pl.pallas_call(
        body,
        name=name,
        grid=(s // ts,),
        in_specs=[_rows_spec(ts, d)] * 2,
        out_specs=[pl.BlockSpec((1, 1), lambda i: (0, 0)), _rows_spec(ts, d)],
        out_shape=[jax.ShapeDtypeStruct((1, 1), F32), jax.ShapeDtypeStruct((s, d), F32)],
        compiler_params=_params(("arbitrary",)),
    )(y, target)


def _rope_tables(seq):
    pos = jnp.arange(seq, dtype=F32)
    inv = ROPE_THETA ** (-jnp.arange(0, HEAD_DIM, 2, dtype=F32) / HEAD_DIM)
    ang = pos[:, None] * inv[None, :]
    cos, sin = jnp.cos(ang), jnp.sin(ang)
    cos2 = jnp.concatenate([cos, cos, cos, cos, jnp.ones((seq, 128), F32)], axis=1)
    sin2 = jnp.concatenate([-sin, sin, -sin, sin, jnp.zeros((seq, 128), F32)], axis=1)
    return cos2, sin2


def _rope(h, cos2, sin2, plain_block, out_dtype, name, width=128):
    s, n = h.shape
    ts = _tile(s, 512, 8)

    def body(h_ref, c_ref, s_ref, o_ref):
        cos, sin = c_ref[...], s_ref[...]
        lane = lax.broadcasted_iota(jnp.int32, cos.shape, 1)
        first_half = lane % HEAD_DIM < HEAD_DIM // 2
        for w in range(width // 128):
            cols = slice(w * 128, (w + 1) * 128)
            x = h_ref[:, cols].astype(F32)
            partner = jnp.where(first_half, pltpu.roll(x, 128 - HEAD_DIM // 2, 1), pltpu.roll(x, HEAD_DIM // 2, 1))
            o_ref[:, cols] = (x * cos + partner * sin).astype(o_ref.dtype)

    tile = pl.BlockSpec((ts, width), lambda i, j: (i, j))
    table = pl.BlockSpec((ts, 128), lambda i, j: (i, plain_block(j)))
    return pl.pallas_call(
        body,
        name=name,
        grid=(s // ts, n // width),
        in_specs=[tile, table, table],
        out_specs=tile,
        out_shape=jax.ShapeDtypeStruct((s, n), out_dtype),
        compiler_params=_params(("parallel", "parallel")),
    )(h, cos2, sin2)


SB_TQ = 512
SB_BLK = 128
SB_WALK = 2
SB_CLOSED = 110.0


def _sb_walk(n_trips, carry, key_blocks):
    def still_open(carry):
        lowest = functools.reduce(jnp.minimum, [c[0] for c in carry])
        return jnp.min(lowest) < SB_CLOSED

    def cond(state):
        t, go, _ = state
        return jnp.logical_and(t < n_trips, go)

    def body(state):
        t, _, carry = state
        carry = key_blocks(t, carry)
        return t + 1, still_open(carry), carry

    return lax.while_loop(cond, body, (jnp.int32(0), still_open(carry), carry))[2]


def _tri2(strict):
    row = lax.broadcasted_iota(jnp.int32, (2 * SB_BLK, SB_BLK), 0) % SB_BLK
    col = lax.broadcasted_iota(jnp.int32, (2 * SB_BLK, SB_BLK), 1)
    return (row > col if strict else row >= col).astype(MXU_DTYPE)


def _cumsum_dot(x, tri2):
    hi = x.astype(MXU_DTYPE)
    lo = x - hi.astype(F32)
    return _dot(jnp.concatenate([hi, lo.astype(MXU_DTYPE)], axis=1), tri2, NN)


def _sb_logits(z, valid):
    l1p = jnp.log(1.0 + jnp.exp(-jnp.abs(z)))
    sp = jnp.maximum(z, 0.0) + l1p
    ls = z - sp
    if valid is not None:
        sp = jnp.where(valid, sp, 0.0)
    return sp, ls


def _sb_weights(ls, after, valid):
    a = jnp.exp(ls - after)
    return a if valid is None else jnp.where(valid, a, 0.0)


def _sb_setup(q_ref, k_ref, v_ref, n_sub, scale):
    qs = [q_ref[0, u * SB_BLK:(u + 1) * SB_BLK, :] * scale for u in range(n_sub)]
    row = lax.broadcasted_iota(jnp.int32, (SB_BLK, SB_BLK), 0)
    col = lax.broadcasted_iota(jnp.int32, (SB_BLK, SB_BLK), 1)

    def load(jj):
        start = pl.multiple_of(jj * SB_BLK, SB_BLK)
        return start, k_ref[0, pl.ds(start, SB_BLK), :], v_ref[0, pl.ds(start, SB_BLK), :]

    return qs, col < row, load


def _sb_fwd(q, k, v, name):
    nh, s, dh = q.shape
    tq = min(SB_TQ, s)
    n_sub = tq // SB_BLK
    scale = dh ** -0.5

    def body(q_ref, k_ref, v_ref, o_ref, ob_ref):
        base = pl.program_id(1) * n_sub
        qs, diag_valid, load = _sb_setup(q_ref, k_ref, v_ref, n_sub, scale)
        tri_after = _tri2(True)

        def key_blocks(first, carry, diagonal):
            blocks = list(range((n_sub if diagonal else SB_WALK) - 1, -1, -1))
            kv = {d: load(first + d) for d in blocks}
            tiles = [(d, u) for d in blocks for u in range(d if diagonal else 0, n_sub)]
            valid = {t: diag_valid if diagonal and t[0] == t[1] else None for t in tiles}
            z = {t: _dot(qs[t[1]], kv[t[0]][1], NT) for t in tiles}
            sp_ls = {t: _sb_logits(z[t], valid[t]) for t in tiles}
            inside = {t: _cumsum_dot(sp_ls[t][0], tri_after) for t in tiles}
            carry = list(carry)
            for t in tiles:
                after_c, acc = carry[t[1]]
                sp, ls = sp_ls[t]
                a = _sb_weights(ls, after_c + inside[t], valid[t])
                carry[t[1]] = (after_c + jnp.sum(sp, axis=1, keepdims=True), acc + _dot(a, kv[t[0]][2], NN))
            return tuple(carry)

        carry = tuple((jnp.zeros((SB_BLK, 1), F32), jnp.zeros((SB_BLK, dh), F32)) for _ in range(n_sub))
        carry = key_blocks(base, carry, True)
        carry = _sb_walk(base // SB_WALK, carry, lambda t, c: key_blocks(base - SB_WALK * (t + 1), c, False))
        for u in range(n_sub):
            rows = slice(u * SB_BLK, (u + 1) * SB_BLK)
            o_ref[0, rows, :] = carry[u][1]
            ob_ref[0, rows, :] = carry[u][1].astype(ob_ref.dtype)

    blk = pl.BlockSpec((1, tq, dh), lambda h, i: (h, i, 0))
    full = pl.BlockSpec((1, s, dh), lambda h, i: (h, 0, 0))
    return pl.pallas_call(
        body,
        name=name,
        grid=(nh, s // tq),
        in_specs=[blk, full, full],
        out_specs=[blk, blk],
        out_shape=[jax.ShapeDtypeStruct((nh, s, dh), F32), jax.ShapeDtypeStruct((nh, s, dh), MXU_DTYPE)],
        compiler_params=_params(("parallel", "parallel")),
    )(q, k, v)


def _sb_bwd(q, k, v, o, do, name):
    nh, s, dh = q.shape
    tq = min(SB_TQ, s)
    n_sub = tq // SB_BLK
    scale = dh ** -0.5

    def body(q_ref, k_ref, v_ref, o_ref, do_ref, dq_ref, dk_ref, dv_ref):
        i = pl.program_id(1)
        base = i * n_sub

        @pl.when(i == 0)
        def _():
            dk_ref[...] = jnp.zeros_like(dk_ref)
            dv_ref[...] = jnp.zeros_like(dv_ref)

        qs, diag_valid, load = _sb_setup(q_ref, k_ref, v_ref, n_sub, scale)
        dob = [do_ref[0, u * SB_BLK:(u + 1) * SB_BLK, :] for u in range(n_sub)]
        total = [jnp.sum(dob[u].astype(F32) * o_ref[0, u * SB_BLK:(u + 1) * SB_BLK, :], axis=1, keepdims=True) for u in range(n_sub)]
        tri_after = _tri2(True)
        tri_from = _tri2(False)

        def key_blocks(first, carry, diagonal):
            blocks = list(range((n_sub if diagonal else SB_WALK) - 1, -1, -1))
            kv = {d: load(first + d) for d in blocks}
            tiles = [(d, u) for d in blocks for u in range(d if diagonal else 0, n_sub)]
            valid = {t: diag_valid if diagonal and t[0] == t[1] else None for t in tiles}
            z = {t: _dot(qs[t[1]], kv[t[0]][1], NT) for t in tiles}
            da = {t: _dot(dob[t[1]], kv[t[0]][2], NT) for t in tiles}
            sp_ls = {t: _sb_logits(z[t], valid[t]) for t in tiles}
            inside = {t: _cumsum_dot(sp_ls[t][0], tri_after) for t in tiles}
            after_run = [c[0] for c in carry]
            ab, dl = {}, {}
            for t in tiles:
                sp, ls = sp_ls[t]
                ab[t] = _sb_weights(ls, after_run[t[1]] + inside[t], valid[t]).astype(MXU_DTYPE)
                dl[t] = ab[t].astype(F32) * da[t]
                after_run[t[1]] = after_run[t[1]] + jnp.sum(sp, axis=1, keepdims=True)
            from_in = {t: _cumsum_dot(dl[t], tri_from) for t in tiles}
            from_run = [c[1] for c in carry]
            dq = [c[2] for c in carry]
            for d in blocks:
                start, kb, _ = kv[d]
                dk = dv = None
                for u in range(d if diagonal else 0, n_sub):
                    t = (d, u)
                    dz = dl[t] - jnp.exp(sp_ls[t][1]) * (dl[t] + total[u] - (from_run[u] + from_in[t]))
                    if valid[t] is not None:
                        dz = jnp.where(valid[t], dz, 0.0)
                    dzb = dz.astype(MXU_DTYPE)
                    from_run[u] = from_run[u] + jnp.sum(dl[t], axis=1, keepdims=True)
                    dq[u] = dq[u] + _dot(dzb, kb, NN)
                    dk_u, dv_u = _dot(dzb, qs[u], TN), _dot(ab[t], dob[u], TN)
                    dk = dk_u if dk is None else dk + dk_u
                    dv = dv_u if dv is None else dv + dv_u
                dk_ref[0, pl.ds(start, SB_BLK), :] += dk
                dv_ref[0, pl.ds(start, SB_BLK), :] += dv
            return tuple(zip(after_run, from_run, dq))

        carry = tuple((jnp.zeros((SB_BLK, 1), F32), jnp.zeros((SB_BLK, 1), F32), jnp.zeros((SB_BLK, dh), F32)) for _ in range(n_sub))
        carry = key_blocks(base, carry, True)
        carry = _sb_walk(base // SB_WALK, carry, lambda t, c: key_blocks(base - SB_WALK * (t + 1), c, False))
        for u in range(n_sub):
            dq_ref[0, u * SB_BLK:(u + 1) * SB_BLK, :] = carry[u][2] * scale

    blk = pl.BlockSpec((1, tq, dh), lambda h, i: (h, i, 0))
    full = pl.BlockSpec((1, s, dh), lambda h, i: (h, 0, 0))
    shape = jax.ShapeDtypeStruct((nh, s, dh), F32)
    return pl.pallas_call(
        body,
        name=name,
        grid=(nh, s // tq),
        in_specs=[blk, full, full, blk, blk],
        out_specs=[blk, full, full],
        out_shape=[shape, shape, shape],
        compiler_params=_params(("parallel", "arbitrary")),
    )(q, k, v, o, do)


BAND_TQ = 2048


def _band_scores(q_ref, k_ref, i, sub, tq, length, max_dist, scale):
    t0 = i * tq + sub * ATT_BLK
    ks = pl.multiple_of(jnp.minimum(jnp.maximum(t0 - ATT_BLK, 0), length - 2 * ATT_BLK), ATT_BLK)
    qs = q_ref[0, sub * ATT_BLK:(sub + 1) * ATT_BLK, :] * scale
    kw = k_ref[0, pl.ds(ks, 2 * ATT_BLK), :]
    sc = _dot(qs, kw, NT)
    diff = (t0 + lax.broadcasted_iota(jnp.int32, sc.shape, 0)) - (ks + lax.broadcasted_iota(jnp.int32, sc.shape, 1))
    valid = (diff >= 0) & (diff <= max_dist)
    return ks, qs, kw, jnp.where(valid, sc, NEG_INF)


def _band_fwd(q, k, v, sinks, max_dist, name):
    bq, length, dh = q.shape
    group = bq // k.shape[0]
    tq = min(BAND_TQ, length)
    scale = dh ** -0.5
    n_sink = 0 if sinks is None else sinks.shape[0]

    def body(*refs):
        if n_sink:
            sink_ref, q_ref, k_ref, v_ref, o_ref, lse_ref = refs
            sink = sink_ref[pl.program_id(0) % n_sink]
        else:
            q_ref, k_ref, v_ref, o_ref, lse_ref = refs
        i = pl.program_id(1)
        scores = [_band_scores(q_ref, k_ref, i, sub, tq, length, max_dist, scale) for sub in range(tq // ATT_BLK)]
        for sub, (ks, _, _, sc) in enumerate(scores):
            m = jnp.max(sc, axis=1, keepdims=True)
            if n_sink:
                m = jnp.maximum(m, sink)
            e = jnp.exp(sc - m)
            den = jnp.sum(e, axis=1, keepdims=True)
            if n_sink:
                den = den + jnp.exp(sink - m)
            rows = slice(sub * ATT_BLK, (sub + 1) * ATT_BLK)
            o_ref[0, rows, :] = _dot(e / den, v_ref[0, pl.ds(ks, 2 * ATT_BLK), :], NN)
            lse_ref[0, rows, :] = m + jnp.log(den)

    qblk = pl.BlockSpec((1, tq, dh), lambda b, i: (b, i, 0))
    kfull = pl.BlockSpec((1, length, dh), lambda b, i: (b // group, 0, 0))
    in_specs = [qblk, kfull, kfull]
    args = [q, k, v]
    if n_sink:
        in_specs = [pl.BlockSpec(memory_space=pltpu.SMEM)] + in_specs
        args = [sinks] + args
    return pl.pallas_call(
        body,
        name=name,
        grid=(bq, length // tq),
        in_specs=in_specs,
        out_specs=[qblk, pl.BlockSpec((1, tq, 1), lambda b, i: (b, i, 0))],
        out_shape=[jax.ShapeDtypeStruct((bq, length, dh), F32), jax.ShapeDtypeStruct((bq, length, 1), F32)],
        compiler_params=_params(("parallel", "parallel")),
    )(*args)


def _band_bwd(q, k, v, do, lse, delta, sinks, max_dist, name):
    bq, length, dh = q.shape
    group = bq // k.shape[0]
    tq = min(BAND_TQ, length)
    scale = dh ** -0.5
    n_sink = 0 if sinks is None else sinks.shape[0]

    def body(*refs):
        if n_sink:
            sink_ref, q_ref, k_ref, v_ref, do_ref, lse_ref, dl_ref, dq_ref, dk_ref, dv_ref, ds_ref = refs
            sink = sink_ref[pl.program_id(0) % n_sink]
        else:
            q_ref, k_ref, v_ref, do_ref, lse_ref, dl_ref, dq_ref, dk_ref, dv_ref, ds_ref = refs
        i = pl.program_id(1)

        @pl.when((i == 0) & (pl.program_id(0) % group == 0))
        def _():
            dk_ref[...] = jnp.zeros_like(dk_ref)
            dv_ref[...] = jnp.zeros_like(dv_ref)

        @pl.when(i == 0)
        def _():
            ds_ref[...] = jnp.zeros_like(ds_ref)

        scores = [_band_scores(q_ref, k_ref, i, sub, tq, length, max_dist, scale) for sub in range(tq // ATT_BLK)]
        dps = [_dot(do_ref[0, sub * ATT_BLK:(sub + 1) * ATT_BLK, :], v_ref[0, pl.ds(ks, 2 * ATT_BLK), :], NT) for sub, (ks, _, _, _) in enumerate(scores)]
        for sub, (ks, qs, kw, sc) in enumerate(scores):
            rows = slice(sub * ATT_BLK, (sub + 1) * ATT_BLK)
            lse = lse_ref[0, rows, :]
            delta_r = dl_ref[0, rows, :]
            dob = do_ref[0, rows, :]
            p = jnp.exp(sc - lse)
            dsb = (p * (dps[sub] - delta_r)).astype(MXU_DTYPE)
            dq_ref[0, rows, :] = _dot(dsb, kw, NN) * scale
            dk_ref[0, pl.ds(ks, 2 * ATT_BLK), :] += _dot(dsb, qs, TN)
            dv_ref[0, pl.ds(ks, 2 * ATT_BLK), :] += _dot(p, dob, TN)
            if n_sink:
                ds_ref[...] += jnp.sum(-jnp.exp(sink - lse) * delta_r, axis=0, keepdims=True)

    qblk = pl.BlockSpec((1, tq, dh), lambda b, i: (b, i, 0))
    qcol = pl.BlockSpec((1, tq, 1), lambda b, i: (b, i, 0))
    kfull = pl.BlockSpec((1, length, dh), lambda b, i: (b // group, 0, 0))
    in_specs = [qblk, kfull, kfull, qblk, qcol, qcol]
    args = [q, k, v, do, lse, delta]
    if n_sink:
        in_specs = [pl.BlockSpec(memory_space=pltpu.SMEM)] + in_specs
        args = [sinks] + args
    kshape = jax.ShapeDtypeStruct((k.shape[0], length, dh), F32)
    return pl.pallas_call(
        body,
        name=name,
        grid=(bq, length // tq),
        in_specs=in_specs,
        out_specs=[qblk, kfull, kfull, pl.BlockSpec((1, 8, 128), lambda b, i: (b, 0, 0))],
        out_shape=[jax.ShapeDtypeStruct((bq, length, dh), F32), kshape, kshape, jax.ShapeDtypeStruct((bq, 8, 128), F32)],
        compiler_params=_params(("arbitrary", "arbitrary")),
    )(*args)


def _merge_weights(lse_refs):
    lses = [r[0] for r in lse_refs]
    m = functools.reduce(jnp.maximum, lses)
    es = [jnp.exp(l - m) for l in lses]
    den = functools.reduce(lambda a, b: a + b, es)
    return [e / den for e in es]


def _merge_fwd(outs, lses, name):
    n = len(outs)
    nh, s, dh = outs[0].shape
    ts = _tile(s, 1024, 8)

    def body(*refs):
        ws = _merge_weights(refs[n:2 * n])
        o = functools.reduce(lambda a, b: a + b, [w * r[0] for w, r in zip(ws, refs[:n])])
        refs[2 * n][0] = o
        refs[2 * n + 1][0] = o.astype(MXU_DTYPE)

    blk = pl.BlockSpec((1, ts, dh), lambda h, i: (h, i, 0))
    col = pl.BlockSpec((1, ts, 1), lambda h, i: (h, i, 0))
    return pl.pallas_call(
        body,
        name=name,
        grid=(nh, s // ts),
        in_specs=[blk] * n + [col] * n,
        out_specs=[blk, blk],
        out_shape=[jax.ShapeDtypeStruct((nh, s, dh), F32), jax.ShapeDtypeStruct((nh, s, dh), MXU_DTYPE)],
        compiler_params=_params(("parallel", "parallel")),
    )(*outs, *lses)


def _merge_bwd(do, o, lses, name):
    n = len(lses)
    nh, s, dh = o.shape
    ts = _tile(s, 1024, 8)

    def body(*refs):
        do_ref, o_ref = refs[:2]
        ws = _merge_weights(refs[2:2 + n])
        dof = do_ref[0].astype(F32)
        base = jnp.sum(dof * o_ref[0], axis=1, keepdims=True)
        for g in range(n):
            refs[2 + n + g][0] = (ws[g] * dof).astype(MXU_DTYPE)
            refs[2 + 2 * n + g][0] = ws[g] * base

    blk = pl.BlockSpec((1, ts, dh), lambda h, i: (h, i, 0))
    col = pl.BlockSpec((1, ts, 1), lambda h, i: (h, i, 0))
    res = pl.pallas_call(
        body,
        name=name,
        grid=(nh, s // ts),
        in_specs=[blk, blk] + [col] * n,
        out_specs=[blk] * n + [col] * n,
        out_shape=[jax.ShapeDtypeStruct((nh, s, dh), MXU_DTYPE)] * n + [jax.ShapeDtypeStruct((nh, s, 1), F32)] * n,
        compiler_params=_params(("parallel", "parallel")),
    )(do, o, *lses)
    return res[:n], res[n:]


def _to_heads(a, dil=1):
    s, n = a.shape
    nh = n // HEAD_DIM
    a = a.reshape(s // dil, dil, nh, HEAD_DIM).transpose(1, 2, 0, 3)
    return a.reshape(dil * nh, s // dil, HEAD_DIM)


def _from_heads(a, dil=1):
    b, ls, c = a.shape
    nh = b // dil
    return a.reshape(dil, nh, ls, c).transpose(2, 0, 1, 3).reshape(ls * dil, nh * c)


def _restride(a, dil):
    nh, s, c = a.shape
    return a.reshape(nh, s // dil, dil, c).transpose(2, 0, 1, 3).reshape(dil * nh, s // dil, c)


def _unstride(a, dil):
    b, ls, c = a.shape
    nh = b // dil
    return a.reshape(dil, nh, ls, c).transpose(1, 2, 0, 3).reshape(nh, ls * dil, c)


ANY = pl.BlockSpec(memory_space=pl.ANY)


def _position():
    x, y, c = lax.axis_index("x"), lax.axis_index("y"), lax.axis_index("c")
    return x, y, c, [(1 - x, y), (x, 1 - y), (1 - x, 1 - y)]


def _remote(src, dst, send_sems, recv_sems, k, to):
    return pltpu.make_async_remote_copy(src_ref=src, dst_ref=dst, send_sem=send_sems.at[k], recv_sem=recv_sems.at[k], device_id=to, device_id_type=MESH)


def _gather_chips(arrs, name):
    n = len(arrs)

    def body(*refs):
        f_refs, g_refs = refs[:n], refs[n:2 * n]
        send_sems, recv_sems = refs[2 * n:]
        x, y, c, chips = _position()
        me, sibling = (x, y, c), (x, y, 1 - c)

        def half(j, ref, sel):
            rows = f_refs[j].shape[1] // 2
            return ref.at[:, pl.ds(sel * rows, rows), :]

        def slot(j, chip, sel):
            return half(j, g_refs[j].at[2 * chip[0] + chip[1]], sel)

        first = [_remote(half(j, f_refs[j], c), slot(j, (x, y), c), send_sems, recv_sems, 6 * j + k, (*chip, c)) for j in range(n) for k, chip in enumerate(chips)]
        for cp in first:
            cp.start()
        passed = []
        for j in range(n):
            for k, chip in enumerate(chips):
                _remote(slot(j, chip, c), slot(j, chip, c), send_sems, recv_sems, 6 * j + k, me).wait_recv()
                passed.append(_remote(slot(j, chip, c), slot(j, chip, c), send_sems, recv_sems, 6 * j + 3 + k, sibling))
                passed[-1].start()
        for j in range(n):
            for k, chip in enumerate(chips):
                _remote(slot(j, chip, 1 - c), slot(j, chip, 1 - c), send_sems, recv_sems, 6 * j + 3 + k, me).wait_recv()
        for cp in first + passed:
            cp.wait_send()

    return pl.pallas_call(
        body,
        name=name,
        in_specs=[ANY] * n,
        out_specs=[ANY] * n,
        out_shape=[jax.ShapeDtypeStruct((N_CHIPS,) + a.shape, a.dtype) for a in arrs],
        scratch_shapes=[pltpu.SemaphoreType.DMA((6 * n,)), pltpu.SemaphoreType.DMA((6 * n,))],
    )(*arrs)


def _swap_sibling(arrs, name):
    n = len(arrs)

    def body(*refs):
        x, y, c, _ = _position()
        send_sems, recv_sems = refs[2 * n:]
        copies = [_remote(refs[j], refs[n + j], send_sems, recv_sems, j, (x, y, 1 - c)) for j in range(n)]
        for cp in copies:
            cp.start()
        for cp in copies:
            cp.wait()

    return pl.pallas_call(
        body,
        name=name,
        in_specs=[ANY] * n,
        out_specs=[ANY] * n,
        out_shape=[jax.ShapeDtypeStruct(a.shape, a.dtype) for a in arrs],
        scratch_shapes=[pltpu.SemaphoreType.DMA((n,)), pltpu.SemaphoreType.DMA((n,))],
    )(*arrs)


def _chips_copies(p_refs, q_refs, send_sems, recv_sems):
    x, y, c, chips = _position()
    my_slot = 2 * x + y
    n = len(p_refs)
    sends = [_remote(p_refs[j].at[2 * px + py], q_refs[j].at[my_slot], send_sems, recv_sems, 3 * j + k, (px, py, c)) for j in range(n) for k, (px, py) in enumerate(chips)]
    arrivals = [_remote(q_refs[j].at[2 * px + py], q_refs[j].at[2 * px + py], send_sems, recv_sems, 3 * j + k, (x, y, c)) for j in range(n) for k, (px, py) in enumerate(chips)]
    return sends, arrivals


def _chips_finish(sends, arrivals):
    for cp in arrivals:
        cp.wait_recv()
    for cp in sends:
        cp.wait_send()


def _exchange_chips(parts, name):
    n = len(parts)

    def body(*refs):
        sends, arrivals = _chips_copies(refs[:n], refs[n:2 * n], *refs[2 * n:])
        for cp in sends:
            cp.start()
        _chips_finish(sends, arrivals)

    return pl.pallas_call(
        body,
        name=name,
        in_specs=[ANY] * n,
        out_specs=[ANY] * n,
        out_shape=[jax.ShapeDtypeStruct(a.shape, a.dtype) for a in parts],
        scratch_shapes=[pltpu.SemaphoreType.DMA((3 * n,)), pltpu.SemaphoreType.DMA((3 * n,))],
    )(*parts)


def _share_sibling(halves, name):
    n = len(halves)

    def body(*refs):
        h_refs, o_refs = refs[:n], refs[n:2 * n]
        send_sems, recv_sems = refs[2 * n:]
        x, y, c, _ = _position()

        def rows(j, sel):
            r = h_refs[j].shape[0]
            return o_refs[j].at[pl.ds(sel * r, r), :]

        sends = [_remote(h_refs[j], rows(j, c), send_sems, recv_sems, j, (x, y, 1 - c)) for j in range(n)]
        for cp in sends:
            cp.start()
        for j in range(n):
            _remote(rows(j, 1 - c), rows(j, 1 - c), send_sems, recv_sems, j, (x, y, c)).wait_recv()
        for cp in sends:
            cp.wait_send()

    return pl.pallas_call(
        body,
        name=name,
        in_specs=[ANY] * n,
        out_specs=[ANY] * n,
        out_shape=[jax.ShapeDtypeStruct((2 * a.shape[0],) + a.shape[1:], a.dtype) for a in halves],
        scratch_shapes=[pltpu.SemaphoreType.DMA((n,)), pltpu.SemaphoreType.DMA((n,))],
    )(*halves)


def _sum_rows(r, c, n_in):
    return _tile(r, max(16, (1 << 20) // (c * (n_in + 1))), 16)


def _add2(a, b, out_dtype, name):
    n, r, c = a.shape
    tr = _sum_rows(r, c, 2)

    def body(a_ref, b_ref, o_ref):
        o_ref[...] = (a_ref[...].astype(F32) + b_ref[...].astype(F32)).astype(o_ref.dtype)

    blk = pl.BlockSpec((1, tr, c), lambda s, i: (s, i, 0))
    return pl.pallas_call(
        body,
        name=name,
        grid=(n, r // tr),
        in_specs=[blk, blk],
        out_specs=blk,
        out_shape=jax.ShapeDtypeStruct(a.shape, out_dtype),
        compiler_params=_params(("parallel", "parallel")),
    )(a, b)


def _sum_slots(q, name):
    n, r, c = q.shape
    tr = _sum_rows(r, c, n)

    def body(q_ref, o_ref):
        acc = q_ref[0].astype(F32)
        for s in range(1, n):
            acc = acc + q_ref[s].astype(F32)
        o_ref[...] = acc

    return pl.pallas_call(
        body,
        name=name,
        grid=(r // tr,),
        in_specs=[pl.BlockSpec((n, tr, c), lambda i: (0, i, 0))],
        out_specs=pl.BlockSpec((tr, c), lambda i: (i, 0)),
        out_shape=jax.ShapeDtypeStruct((r, c), F32),
        compiler_params=_params(("parallel",)),
    )(q)


def _adamw(w, g, m, v, name):
    r, c = w.shape
    tr = _tile(r, 256, 8)
    c1 = 1.0 - ADAM_B1 ** ADAM_STEP
    c2 = 1.0 - ADAM_B2 ** ADAM_STEP

    def body(w_ref, g_ref, m_ref, v_ref, d_ref, nm_ref, nv_ref):
        g = g_ref[...]
        nm = ADAM_B1 * m_ref[...] + (1.0 - ADAM_B1) * g
        nv = ADAM_B2 * v_ref[...] + (1.0 - ADAM_B2) * (g * g)
        d_ref[...] = -ADAM_LR * ((nm / c1) / (jnp.sqrt(nv / c2) + ADAM_EPS) + ADAM_WD * w_ref[...])
        nm_ref[...] = nm
        nv_ref[...] = nv

    blk = pl.BlockSpec((tr, c), lambda i: (i, 0))
    shape = jax.ShapeDtypeStruct((r, c), F32)
    return pl.pallas_call(
        body,
        name=name,
        grid=(r // tr,),
        in_specs=[blk] * 4,
        out_specs=[blk] * 3,
        out_shape=[shape] * 3,
        compiler_params=_params(("parallel",)),
    )(w, g, m, v)


WEIGHTS = ("ffn1_w_in", "ffn1_w_out", "ffn2_w_in", "ffn2_w_out", "ln_g", "ln_b", "sb_w_in", "sb_w_out", "swa_w_in", "swa_sinks", "swa_w_out", "dil_w_in", "dil_w_out", "ple_w_proj", "ple_w_gate")
SHARD_AXIS = {"ffn1_w_in": 2, "ffn1_w_out": 1, "ffn2_w_in": 2, "ffn2_w_out": 1, "ln_g": 2, "ln_b": 2, "sb_w_in": 2, "sb_w_out": 1, "swa_w_in": 2, "swa_sinks": None, "swa_w_out": 1, "dil_w_in": 2, "dil_w_out": 1, "ple_w_proj": 2, "ple_w_gate": 1}
SMALL = ("ln_g", "ln_b", "swa_sinks")
SMALL_COLS = 128
SMALL_UNIT = 16 * SMALL_COLS


def _pack_small(pieces, lead):
    flat = jnp.concatenate([a.reshape(lead + (-1,)) for a in pieces], axis=-1)
    n = flat.shape[-1]
    flat = jnp.pad(flat, [(0, 0)] * len(lead) + [(0, -n % SMALL_UNIT)])
    return flat.reshape(lead + (-1, SMALL_COLS))


def _unpack_small(buf, shapes, lead):
    flat = buf.reshape(lead + (-1,))
    out, off = [], 0
    for shp in shapes:
        n = math.prod(shp)
        out.append(flat[..., off:off + n].reshape(lead + tuple(shp)))
        off += n
    return out


def _reduce_pair(grads, c, wire, tag):
    keep = [lax.dynamic_slice_in_dim(g, c * (g.shape[1] // 2), g.shape[1] // 2, axis=1) for g in grads]
    give = [lax.dynamic_slice_in_dim(g, (1 - c) * (g.shape[1] // 2), g.shape[1] // 2, axis=1).astype(wire) for g in grads]
    got = _swap_sibling(give, f"{tag}_pair")
    return [_add2(k, g, wire, f"{tag}_pair_sum{j}") for j, (k, g) in enumerate(zip(keep, got))]


def _reduce_finish(part, landed, c, tag):
    landed = [_fill_slot(q, lax.dynamic_slice_in_dim(p, _my_slot(), 1, axis=0)) for q, p in zip(landed, part)]
    halves = [_sum_slots(q, f"{tag}_chips_sum{j}") for j, q in enumerate(landed)]
    shared = _share_sibling(halves, f"{tag}_share")
    return [lax.dynamic_update_slice_in_dim(s, h, c * h.shape[0], axis=0) for s, h in zip(shared, halves)]


def _reduce_group(grads, c, wire, tag):
    part = _reduce_pair(grads, c, wire, tag)
    return _reduce_finish(part, _exchange_chips(part, f"{tag}_chips"), c, tag)


def _my_slot():
    return 2 * lax.axis_index("x") + lax.axis_index("y")


def _fill_slot(slots, mine):
    return lax.dynamic_update_slice_in_dim(slots, mine, _my_slot(), axis=0)


def _full_from_shards(g, axis):
    g = jnp.moveaxis(g, 0, axis)
    return g.reshape(g.shape[:axis] + (g.shape[axis] * g.shape[axis + 1],) + g.shape[axis + 2:])


def _shards_from_full(a, axis):
    a = a.reshape(a.shape[:axis] + (N_CHIPS, a.shape[axis] // N_CHIPS) + a.shape[axis + 1:])
    return jnp.moveaxis(a, axis, 0)


def _ffn_forward(x, xb, w_in, w_out, g, b, alpha, tag):
    gate, up, act = _ffn_in(xb, w_in, f"{tag}_in")
    out, outb, xhat, rstd = _mm_ln(act, w_out, x, g, b, alpha, 0.5, f"{tag}_out_ln")
    return (out, outb), dict(xb=xb, gate=gate, up=up, act=act, xhat=xhat, rstd=rstd)


EXCHANGE_HOSTS = ("dw_out", "dw_in", "dx")


def _exchange_host(name):
    if name in ("ffn1_w_in", "ffn2_w_in", "ffn1_w_out"):
        return "dw_in"
    return "dx" if name.endswith("_w_in") else "dw_out"


def _ffn_backward(dout, saved, w_in, w_out, g, tag, exchange=None):
    exchange = exchange or {}
    dr, dyb, dg, db = _ln_bwd(*dout, saved["xhat"], saved["rstd"], g, 0.5, f"{tag}_ln_bwd")
    dh = _ffn_dact(dyb, w_out, saved["gate"], saved["up"], f"{tag}_dact")
    out = {
        "dw_out": _mm(saved["act"], dyb, "tn", F32, f"{tag}_dw_out", exchange=exchange.get("dw_out") or None),
        "dw_in": _mm(saved["xb"], dh, "tn", F32, f"{tag}_dw_in", split_b=True, exchange=exchange.get("dw_in") or None),
        "dx": _mm(dh, w_in, "nt", F32, f"{tag}_dx", split_a=True, exchange=exchange.get("dx") or None),
    }
    landed = {h: out[h][1] for h in EXCHANGE_HOSTS if exchange.get(h)}
    res = {h: out[h][0] if exchange.get(h) else out[h] for h in EXCHANGE_HOSTS}
    return dr, res["dx"], res["dw_in"], res["dw_out"], dg, db, landed


def _sb_forward(xb, w_in, w_out, tag):
    nw = w_out.shape[0]
    h = _mm(xb, w_in, "nn", MXU_DTYPE, f"{tag}_qkv")
    q, k, v = (_to_heads(h[:, j * nw:(j + 1) * nw]) for j in range(3))
    o, ob = _sb_fwd(q, k, v, f"{tag}_att")
    ob = _from_heads(ob)
    return ob, dict(q=q, k=k, v=v, o=o, ob=ob)


def _sb_backward(dmix, saved, xb, w_in, w_out, tag):
    dw_out = _mm(saved["ob"], dmix, "tn", F32, f"{tag}_dw_out")
    do = _to_heads(_mm(dmix, w_out, "nt", MXU_DTYPE, f"{tag}_do"))
    dq, dk, dv = _sb_bwd(saved["q"], saved["k"], saved["v"], saved["o"], do, f"{tag}_att_bwd")
    dh = jnp.concatenate([_from_heads(t) for t in (dq, dk, dv)], axis=1).astype(MXU_DTYPE)
    dw_in = _mm(xb, dh, "tn", F32, f"{tag}_dw_in")
    return _mm(dh, w_in, "nt", F32, f"{tag}_dx"), dw_in, dw_out


def _swa_forward(xb, w_in, sinks, w_out, tables, tag):
    nq = w_out.shape[0]
    nkv = (w_in.shape[1] - nq) // 2
    h = _mm(xb, w_in, "nn", F32, f"{tag}_qkv")
    n_rot = (nq + nkv) // 128
    hb = _rope(h, *tables, lambda j: j // n_rot, MXU_DTYPE, f"{tag}_rope")
    q, k, v = _to_heads(hb[:, :nq]), _to_heads(hb[:, nq:nq + nkv]), _to_heads(hb[:, nq + nkv:])
    o, lse = _band_fwd(q, k, v, sinks, SWA_WINDOW - 1, f"{tag}_att")
    o, ob = _merge_fwd([o], [lse], f"{tag}_cast")
    ob = _from_heads(ob)
    return ob, dict(q=q, k=k, v=v, o=o, lse=lse, ob=ob, n_rot=n_rot)


def _swa_backward(dmix, saved, xb, w_in, sinks, w_out, tables, tag):
    dw_out = _mm(saved["ob"], dmix, "tn", F32, f"{tag}_dw_out")
    do = _to_heads(_mm(dmix, w_out, "nt", MXU_DTYPE, f"{tag}_do"))
    (dog,), (delta,) = _merge_bwd(do, saved["o"], [saved["lse"]], f"{tag}_delta")
    dq, dk, dv, dsink = _band_bwd(saved["q"], saved["k"], saved["v"], dog, saved["lse"], delta, sinks, SWA_WINDOW - 1, f"{tag}_att_bwd")
    dh = jnp.concatenate([_from_heads(t) for t in (dq, dk, dv)], axis=1)
    n_rot = saved["n_rot"]
    dhb = _rope(dh, tables[0], -tables[1], lambda j: j // n_rot, MXU_DTYPE, f"{tag}_rope_bwd")
    dw_in = _mm(xb, dhb, "tn", F32, f"{tag}_dw_in")
    return _mm(dhb, w_in, "nt", F32, f"{tag}_dx"), dw_in, dw_out, dsink[:, 0, 0]


def _dil_forward(xb, w_in, w_out, tables, tag):
    nw = w_out.shape[0]
    h = _mm(xb, w_in, "nn", F32, f"{tag}_qkv")
    hb = _rope(h, *tables, lambda j: (j % 3) // 2, MXU_DTYPE, f"{tag}_rope", width=nw)
    qkv, outs, lses = [], [], []
    for gi, (win, dil) in enumerate(DIL_GROUPS):
        base = gi * 3 * nw
        q, k, v = (_to_heads(hb[:, base + j * nw:base + (j + 1) * nw], dil) for j in range(3))
        o, lse = _band_fwd(q, k, v, None, win // dil, f"{tag}_att{gi}")
        qkv.append((q, k, v))
        outs.append(_unstride(o, dil))
        lses.append(_unstride(lse, dil))
    o, ob = _merge_fwd(outs, lses, f"{tag}_merge")
    ob = _from_heads(ob)
    return ob, dict(qkv=qkv, o=o, lses=lses, ob=ob)


def _dil_backward(dmix, saved, xb, w_in, w_out, tables, tag):
    dw_out = _mm(saved["ob"], dmix, "tn", F32, f"{tag}_dw_out")
    do = _to_heads(_mm(dmix, w_out, "nt", MXU_DTYPE, f"{tag}_do"))
    dogs, deltas = _merge_bwd(do, saved["o"], saved["lses"], f"{tag}_merge_bwd")
    parts = []
    for gi, (win, dil) in enumerate(DIL_GROUPS):
        q, k, v = saved["qkv"][gi]
        dq, dk, dv, _ = _band_bwd(q, k, v, _restride(dogs[gi], dil), _restride(saved["lses"][gi], dil), _restride(deltas[gi], dil), None, win // dil, f"{tag}_att{gi}_bwd")
        parts += [_from_heads(t, dil) for t in (dq, dk, dv)]
    dh = jnp.concatenate(parts, axis=1)
    dhb = _rope(dh, tables[0], -tables[1], lambda j: (j % 3) // 2, MXU_DTYPE, f"{tag}_rope_bwd", width=w_out.shape[0])
    dw_in = _mm(xb, dhb, "tn", F32, f"{tag}_dw_in")
    return _mm(dhb, w_in, "nt", F32, f"{tag}_dx"), dw_in, dw_out


def kernel(x, p, ffn1_w_in, ffn1_w_out, ffn2_w_in, ffn2_w_out, ln_g, ln_b, sb_w_in, sb_w_out, swa_w_in, swa_sinks, swa_w_out, dil_w_in, dil_w_out, ple_w_proj, ple_w_gate, loss_target, m_ffn1_w_in, m_ffn1_w_out, m_ffn2_w_in, m_ffn2_w_out, m_ln_g, m_ln_b, m_sb_w_in, m_sb_w_out, m_swa_w_in, m_swa_sinks, m_swa_w_out, m_dil_w_in, m_dil_w_out, m_ple_w_proj, m_ple_w_gate, v_ffn1_w_in, v_ffn1_w_out, v_ffn2_w_in, v_ffn2_w_out, v_ln_g, v_ln_b, v_sb_w_in, v_sb_w_out, v_swa_w_in, v_swa_sinks, v_swa_w_out, v_dil_w_in, v_dil_w_out, v_ple_w_proj, v_ple_w_gate):
    shard = dict(ffn1_w_in=ffn1_w_in, ffn1_w_out=ffn1_w_out, ffn2_w_in=ffn2_w_in, ffn2_w_out=ffn2_w_out, ln_g=ln_g, ln_b=ln_b, sb_w_in=sb_w_in, sb_w_out=sb_w_out, swa_w_in=swa_w_in, swa_sinks=swa_sinks, swa_w_out=swa_w_out, dil_w_in=dil_w_in, dil_w_out=dil_w_out, ple_w_proj=ple_w_proj, ple_w_gate=ple_w_gate)
    mom_m = dict(ffn1_w_in=m_ffn1_w_in, ffn1_w_out=m_ffn1_w_out, ffn2_w_in=m_ffn2_w_in, ffn2_w_out=m_ffn2_w_out, ln_g=m_ln_g, ln_b=m_ln_b, sb_w_in=m_sb_w_in, sb_w_out=m_sb_w_out, swa_w_in=m_swa_w_in, swa_sinks=m_swa_sinks, swa_w_out=m_swa_w_out, dil_w_in=m_dil_w_in, dil_w_out=m_dil_w_out, ple_w_proj=m_ple_w_proj, ple_w_gate=m_ple_w_gate)
    mom_v = dict(ffn1_w_in=v_ffn1_w_in, ffn1_w_out=v_ffn1_w_out, ffn2_w_in=v_ffn2_w_in, ffn2_w_out=v_ffn2_w_out, ln_g=v_ln_g, ln_b=v_ln_b, sb_w_in=v_sb_w_in, sb_w_out=v_sb_w_out, swa_w_in=v_swa_w_in, swa_sinks=v_swa_sinks, swa_w_out=v_swa_w_out, dil_w_in=v_dil_w_in, dil_w_out=v_dil_w_out, ple_w_proj=v_ple_w_proj, ple_w_gate=v_ple_w_gate)
    depth = ffn1_w_in.shape[0]
    alpha = (2 * depth) ** 0.25
    c = lax.axis_index("c")

    mats = [n for n in WEIGHTS if n not in SMALL]
    local = [shard[n].astype(MXU_DTYPE) for n in mats] + [_pack_small([ln_g, ln_b], ())[None]]
    got = [_fill_slot(g, a[None]) for g, a in zip(_gather_chips(local, "gather_weights"), local)]
    full = {"swa_sinks": swa_sinks}
    for n, g in zip(mats, got):
        full[n] = _full_from_shards(g, SHARD_AXIS[n])
    for n, g in zip(("ln_g", "ln_b"), _unpack_small(got[-1][:, 0], [ln_g.shape, ln_b.shape], (N_CHIPS,))):
        full[n] = _full_from_shards(g, SHARD_AXIS[n])

    seq = x.shape[1]
    tables = _rope_tables(seq)
    xf = x[0]
    xb = xf.astype(MXU_DTYPE)
    saved = []
    for i in range(depth):
        kind, j = i % 3, i // 3
        mixer = ("sb", "swa", "dil")[kind]
        sv = {}
        (x1, x1b), sv["ffn1"] = _ffn_forward(xf, xb, full["ffn1_w_in"][i], full["ffn1_w_out"][i], full["ln_g"][i, 0], full["ln_b"][i, 0], alpha, f"l{i}_ffn1")
        if kind == 0:
            mix, sv["mix"] = _sb_forward(x1b, full["sb_w_in"][j], full["sb_w_out"][j], f"l{i}_sb")
        elif kind == 1:
            mix, sv["mix"] = _swa_forward(x1b, full["swa_w_in"][j], swa_sinks[j], full["swa_w_out"][j], tables, f"l{i}_swa")
        else:
            mix, sv["mix"] = _dil_forward(x1b, full["dil_w_in"][j], full["dil_w_out"][j], tables, f"l{i}_dil")
        x2, x2b, sv["xhat2"], sv["rstd2"] = _mm_ln(mix, full[f"{mixer}_w_out"][j], x1, full["ln_g"][i, 1], full["ln_b"][i, 1], alpha, 1.0, f"l{i}_{mixer}_proj_ln")
        sv["x1b"] = x1b
        (x3, x3b), sv["ffn2"] = _ffn_forward(x2, x2b, full["ffn2_w_in"][i], full["ffn2_w_out"][i], full["ln_g"][i, 2], full["ln_b"][i, 2], alpha, f"l{i}_ffn2")
        xf, xb, sv["u"], sv["e"] = _ple_fwd(x3, x3b, p[i, 0], full["ple_w_gate"][i], full["ple_w_proj"][i], f"l{i}_ple")
        sv["x3b"] = x3b
        saved.append(sv)

    loss_part, dy = _loss(xf, loss_target[0], "loss")
    loss = lax.psum(loss_part[0, 0], ("x", "y", "c"))

    grads = {n: [None] * full[n].shape[0] for n in WEIGHTS if n not in ("ln_g", "ln_b")}
    gsum = {n: [None] * full[n].shape[0] for n in mats}
    dln_g = [[None] * 3 for _ in range(depth)]
    dln_b = [[None] * 3 for _ in range(depth)]
    dout = (dy, None, 1.0)
    pending = None

    def finish_reduce(layer, part, tag, landed):
        for (n, k), g in zip(layer, _reduce_finish(part, landed, c, tag)):
            gsum[n][k] = g

    for i in reversed(range(depth)):
        kind, j = i % 3, i // 3
        mixer = ("sb", "swa", "dil")[kind]
        sv = saved[i]
        dx4, dub, deb = _ple_bwd(*dout, sv["u"], sv["e"], f"l{i}_ple_bwd")
        grads["ple_w_gate"][i] = _mm(sv["x3b"], dub, "tn", F32, f"l{i}_ple_dw_gate")
        grads["ple_w_proj"][i] = _mm(p[i, 0], deb, "tn", F32, f"l{i}_ple_dw_proj")
        dxb = _mm(dub, full["ple_w_gate"][i], "nt", F32, f"l{i}_ple_dx")
        dr, dxb, grads["ffn2_w_in"][i], grads["ffn2_w_out"][i], dln_g[i][2], dln_b[i][2], landed = _ffn_backward((dx4, dxb, 1.0), sv["ffn2"], full["ffn2_w_in"][i], full["ffn2_w_out"][i], full["ln_g"][i, 2], f"l{i}_ffn2", exchange=pending and {h: [q for (n, _), q in zip(pending[0], pending[1]) if _exchange_host(n) == h] for h in EXCHANGE_HOSTS})
        if pending:
            arrived = {h: iter(v) for h, v in landed.items()}
            finish_reduce(*pending, [next(arrived[_exchange_host(n)]) for n, _ in pending[0]])
        dr, dmix, dln_g[i][1], dln_b[i][1] = _ln_bwd(dr, dxb, alpha, sv["xhat2"], sv["rstd2"], full["ln_g"][i, 1], 1.0, f"l{i}_mix_ln_bwd")
        if kind == 0:
            dxb, grads["sb_w_in"][j], grads["sb_w_out"][j] = _sb_backward(dmix, sv["mix"], sv["x1b"], full["sb_w_in"][j], full["sb_w_out"][j], f"l{i}_sb")
        elif kind == 1:
            dxb, grads["swa_w_in"][j], grads["swa_w_out"][j], grads["swa_sinks"][j] = _swa_backward(dmix, sv["mix"], sv["x1b"], full["swa_w_in"][j], swa_sinks[j], full["swa_w_out"][j], tables, f"l{i}_swa")
        else:
            dxb, grads["dil_w_in"][j], grads["dil_w_out"][j] = _dil_backward(dmix, sv["mix"], sv["x1b"], full["dil_w_in"][j], full["dil_w_out"][j], tables, f"l{i}_dil")
        dr, dxb, grads["ffn1_w_in"][i], grads["ffn1_w_out"][i], dln_g[i][0], dln_b[i][0], _ = _ffn_backward((dr, dxb, alpha), sv["ffn1"], full["ffn1_w_in"][i], full["ffn1_w_out"][i], full["ln_g"][i, 0], f"l{i}_ffn1")
        dout = (dr, dxb, alpha)
        layer = [("ffn1_w_in", i), ("ffn1_w_out", i), (f"{mixer}_w_in", j), (f"{mixer}_w_out", j), ("ffn2_w_in", i), ("ffn2_w_out", i), ("ple_w_proj", i), ("ple_w_gate", i)]
        pending = (layer, _reduce_pair([_shards_from_full(grads[n][k], SHARD_AXIS[n] - 1) for n, k in layer], c, MXU_DTYPE, f"l{i}_reduce"), f"l{i}_reduce")
    layer, part, tag = pending
    finish_reduce(layer, part, tag, _exchange_chips(part, f"{tag}_chips"))
    grad_x = _axpy(*dout, "grad_x")[None]

    gshard = {n: jnp.stack(g) for n, g in gsum.items()}
    small = [
        _shards_from_full(jnp.stack([jnp.concatenate(r, axis=0) for r in dln_g]), SHARD_AXIS["ln_g"]),
        _shards_from_full(jnp.stack([jnp.concatenate(r, axis=0) for r in dln_b]), SHARD_AXIS["ln_b"]),
        jnp.broadcast_to(jnp.stack(grads["swa_sinks"])[None], (N_CHIPS,) + swa_sinks.shape),
    ]
    (small_sum,) = _reduce_group([_pack_small(small, (N_CHIPS,))], c, F32, "small_reduce")
    for n, g in zip(SMALL, _unpack_small(small_sum, [shard[n].shape for n in SMALL], ())):
        gshard[n] = g

    delta, new_m, new_v = {}, {}, {}
    for n in WEIGHTS:
        shp = shard[n].shape
        two_d = (-1, shp[-1])
        d, nm, nv = _adamw(shard[n].reshape(two_d), gshard[n].reshape(two_d), mom_m[n].reshape(two_d), mom_v[n].reshape(two_d), f"adamw_{n}")
        delta[n], new_m[n], new_v[n] = d.reshape(shp), nm.reshape(shp), nv.reshape(shp)

    return (loss, grad_x, *[gshard[n] for n in WEIGHTS], *[delta[n] for n in WEIGHTS], *[new_m[n] for n in WEIGHTS], *[new_v[n] for n in WEIGHTS])
```
